```python
import math
import jax, jax.numpy as jnp
from jax import lax
import numpy as np

D_MODEL = 2048
BATCH = 1
SEQ = 16384
DEPTH = 1

HEAD_DIM = 128
N_DIFF_HEADS = D_MODEL // (4 * HEAD_DIM)
DIFF_V_DIM = 2 * HEAD_DIM
N_DIL_HEADS = D_MODEL // (2 * HEAD_DIM)
DIFF_QK_COLS = N_DIFF_HEADS * 2 * HEAD_DIM
DIFF_WIDTH = N_DIFF_HEADS * DIFF_V_DIM
DIL_WIDTH = N_DIL_HEADS * HEAD_DIM
MIX_WIDTH = DIFF_WIDTH + DIL_WIDTH
IN_PROJ_COLS = 2 * DIFF_QK_COLS + DIFF_WIDTH + 3 * DIL_WIDTH
DILATED_PATTERNS = ((128, 1), (512, 4), (2048, 16))
N_REL_BUCKETS = 32
REL_MAX_DISTANCE = 1024
D_FF = 5632
CONV_WIDTH = 3
Q_BLOCK = 128
NORM_EPS = 1e-6
SUBLN_EPS = 1e-5
NEG_INF = -1e30

kernel_name = "hybrid_diffattn_dilated_window_convffn_encoder"


def rmsnorm(x, gain, eps=NORM_EPS):
    xf = x.astype(jnp.float32)
    y = xf * lax.rsqrt(jnp.mean(xf * xf, axis=-1, keepdims=True) + eps)
    return (y * gain.astype(jnp.float32)).astype(x.dtype)


def rel_bucket(rel):
    nb = N_REL_BUCKETS // 2
    ret = jnp.where(rel > 0, nb, 0)
    n = jnp.abs(rel)
    max_exact = nb // 2
    nf = jnp.maximum(n, 1).astype(jnp.float32)
    large = max_exact + (jnp.log(nf / max_exact) / math.log(REL_MAX_DISTANCE / max_exact)
                         * (nb - max_exact)).astype(jnp.int32)
    large = jnp.minimum(large, nb - 1)
    return ret + jnp.where(n < max_exact, n, large)


def diff_attention(q1, q2, k1, k2, v, lam, table):
    B, H, S, D = q1.shape
    nblk = S // Q_BLOCK
    scale = 1.0 / math.sqrt(D)
    q1b = q1.reshape(B, H, nblk, Q_BLOCK, D).transpose(2, 0, 1, 3, 4)
    q2b = q2.reshape(B, H, nblk, Q_BLOCK, D).transpose(2, 0, 1, 3, 4)
    starts = jnp.arange(nblk, dtype=jnp.int32) * Q_BLOCK
    kpos = jnp.arange(S, dtype=jnp.int32)

    def block(args):
        q1_blk, q2_blk, start = args
        qpos = start + jnp.arange(Q_BLOCK, dtype=jnp.int32)
        bucket = rel_bucket(kpos[None, :] - qpos[:, None])
        bias = jnp.take(table, bucket, axis=0).transpose(2, 0, 1)[None]
        bias = bias.astype(jnp.float32)
        s1 = jnp.einsum('bhqd,bhkd->bhqk', q1_blk, k1).astype(jnp.float32) * scale + bias
        s2 = jnp.einsum('bhqd,bhkd->bhqk', q2_blk, k2).astype(jnp.float32) * scale + bias
        attn = jax.nn.softmax(s1, axis=-1) - lam * jax.nn.softmax(s2, axis=-1)
        return jnp.einsum('bhqk,bhkv->bhqv', attn.astype(v.dtype), v)

    out = lax.map(block, (q1b, q2b, starts))
    return out.transpose(1, 2, 0, 3, 4).reshape(B, H, S, v.shape[-1])


def dilated_window(q, k, v, window, dilation, table):
    B, S, H, D = q.shape
    half = window // (2 * dilation)
    L = S // dilation
    nb = -(-L // half)
    Lp = nb * half
    scale = 1.0 / math.sqrt(D)

    def to_sub(t):
        return t.reshape(B, L, dilation, H, D).transpose(0, 2, 3, 1, 4)

    def windows(t):
        tp = jnp.pad(to_sub(t), ((0, 0), (0, 0), (0, 0), (half, Lp - L + half), (0, 0)))
        tp = tp.reshape(B, dilation, H, nb + 2, half, D)
        return jnp.concatenate([tp[:, :, :, :-2], tp[:, :, :, 1:-1], tp[:, :, :, 2:]], axis=4)

    qs = jnp.pad(to_sub(q), ((0, 0), (0, 0), (0, 0), (0, Lp - L), (0, 0)))
    qs = qs.reshape(B, dilation, H, nb, half, D)
    kw = windows(k)
    vw = windows(v)

    a = jnp.arange(half, dtype=jnp.int32)[:, None]
    b = jnp.arange(3 * half, dtype=jnp.int32)[None, :]
    off = b - half - a
    band = jnp.abs(off) <= half
    key_idx = jnp.arange(nb, dtype=jnp.int32)[:, None] * half - half + jnp.arange(3 * half, dtype=jnp.int32)[None, :]
    inrange = (key_idx >= 0) & (key_idx < L)
    mask = band[None] & inrange[:, None, :]
    bias = jnp.take(table, rel_bucket(off * dilation), axis=0).transpose(2, 0, 1).astype(jnp.float32)

    s = jnp.einsum('bchnqd,bchnkd->bchnqk', qs, kw).astype(jnp.float32) * scale + bias[None, None, :, None]
    s = jnp.where(mask[None, None, None], s, NEG_INF)
    m = jnp.max(s, axis=-1, keepdims=True)
    e = jnp.exp(s - m)
    den = jnp.sum(e, axis=-1, keepdims=True)
    o = jnp.einsum('bchnqk,bchnkd->bchnqd', (e / den).astype(v.dtype), vw)
    lse = (m + jnp.log(den))[..., 0]

    o = o.reshape(B, dilation, H, Lp, D)[:, :, :, :L].transpose(0, 3, 1, 2, 4).reshape(B, S, H, D)
    lse = lse.reshape(B, dilation, H, Lp)[:, :, :, :L].transpose(0, 3, 1, 2).reshape(B, S, H)
    return o, lse


def depthwise_conv(h, w, bias):
    C = h.shape[-1]
    pad = CONV_WIDTH // 2
    y = lax.conv_general_dilated(h, w.astype(h.dtype)[:, None, :], window_strides=(1,),
                                 padding=((pad, pad),), dimension_numbers=('NWC', 'WIO', 'NWC'),
                                 feature_group_count=C)
    return y + bias.astype(h.dtype)


def setup_inputs(seed: int = 0) -> dict:
    key = jax.random.key(seed)
    ks = jax.random.split(key, 20)
    f32 = jnp.float32

    def nrm(k, shape, scale):
        return jax.random.normal(k, shape, f32) * scale

    return {
        "x": nrm(ks[0], (BATCH, SEQ, D_MODEL), 1.0),
        "norm1_gain": 1.0 + nrm(ks[1], (DEPTH, D_MODEL), 0.02),
        "w_in": nrm(ks[2], (DEPTH, D_MODEL, IN_PROJ_COLS), D_MODEL ** -0.5),
        "rel_bias_table": nrm(ks[3], (N_REL_BUCKETS, N_DIFF_HEADS + N_DIL_HEADS), 0.2),
        "lambda_q1": nrm(ks[4], (DEPTH, HEAD_DIM), 0.1),
        "lambda_k1": nrm(ks[5], (DEPTH, HEAD_DIM), 0.1),
        "lambda_q2": nrm(ks[6], (DEPTH, HEAD_DIM), 0.1),
        "lambda_k2": nrm(ks[7], (DEPTH, HEAD_DIM), 0.1),
        "diff_subln_gain": 1.0 + nrm(ks[8], (DEPTH, DIFF_V_DIM), 0.02),
        "dil_out_gain": 1.0 + nrm(ks[9], (DEPTH, DIL_WIDTH), 0.02),
        "w_out": nrm(ks[10], (DEPTH, MIX_WIDTH, D_MODEL), MIX_WIDTH ** -0.5),
        "norm2_gain": 1.0 + nrm(ks[11], (DEPTH, D_MODEL), 0.02),
        "w_gate_up": nrm(ks[12], (DEPTH, D_MODEL, 2 * D_FF), D_MODEL ** -0.5),
        "conv_w": nrm(ks[13], (DEPTH, CONV_WIDTH, D_FF), CONV_WIDTH ** -0.5),
        "conv_b": nrm(ks[14], (DEPTH, D_FF), 0.01),
        "w_down": nrm(ks[15], (DEPTH, D_FF, D_MODEL), D_FF ** -0.5),
        "final_gain": 1.0 + nrm(ks[16], (D_MODEL,), 0.02),
    }


def reference(x, norm1_gain, w_in, rel_bias_table, lambda_q1, lambda_k1, lambda_q2, lambda_k2,
              diff_subln_gain, dil_out_gain, w_out, norm2_gain, w_gate_up, conv_w, conv_b,
              w_down, final_gain):
    B, S, _ = x.shape
    table_diff = rel_bias_table[:, :N_DIFF_HEADS]
    table_dil = rel_bias_table[:, N_DIFF_HEADS:]
    split_at = [int(c) for c in np.cumsum([DIFF_QK_COLS, DIFF_QK_COLS, DIFF_WIDTH, DIL_WIDTH, DIL_WIDTH])]

    for l in range(DEPTH):
        h = rmsnorm(x, norm1_gain[l])
        proj = h @ w_in[l]
        dq, dk, dv, lq, lk, lv = jnp.split(proj, split_at, axis=-1)

        dq = dq.reshape(B, S, N_DIFF_HEADS, 2, HEAD_DIM)
        dk = dk.reshape(B, S, N_DIFF_HEADS, 2, HEAD_DIM)
        q1 = dq[:, :, :, 0].transpose(0, 2, 1, 3)
        q2 = dq[:, :, :, 1].transpose(0, 2, 1, 3)
        k1 = dk[:, :, :, 0].transpose(0, 2, 1, 3)
        k2 = dk[:, :, :, 1].transpose(0, 2, 1, 3)
        v_d = dv.reshape(B, S, N_DIFF_HEADS, DIFF_V_DIM).transpose(0, 2, 1, 3)
        lam_init = 0.8 - 0.6 * math.exp(-0.3 * l)
        lam = (jnp.exp(jnp.sum(lambda_q1[l].astype(jnp.float32) * lambda_k1[l].astype(jnp.float32)))
               - jnp.exp(jnp.sum(lambda_q2[l].astype(jnp.float32) * lambda_k2[l].astype(jnp.float32)))
               + lam_init)
        o_d = diff_attention(q1, q2, k1, k2, v_d, lam, table_diff)
        o_d = rmsnorm(o_d, diff_subln_gain[l], SUBLN_EPS) * (1.0 - lam_init)
        o_d = o_d.transpose(0, 2, 1, 3).reshape(B, S, DIFF_WIDTH)

        q = lq.reshape(B, S, N_DIL_HEADS, HEAD_DIM)
        k = lk.reshape(B, S, N_DIL_HEADS, HEAD_DIM)
        v = lv.reshape(B, S, N_DIL_HEADS, HEAD_DIM)
        outs, lses = [], []
        for window, dilation in DILATED_PATTERNS:
            o_p, lse_p = dilated_window(q, k, v, window, dilation, table_dil)
            outs.append(o_p)
            lses.append(lse_p)
        wts = jax.nn.softmax(jnp.stack(lses, axis=0), axis=0)
        o_l = jnp.sum(wts[..., None].astype(v.dtype) * jnp.stack(outs, axis=0), axis=0)
        o_l = rmsnorm(o_l, dil_out_gain[l].reshape(N_DIL_HEADS, HEAD_DIM))
        o_l = o_l.reshape(B, S, DIL_WIDTH)

        x = x + jnp.concatenate([o_d, o_l], axis=-1) @ w_out[l]

        h2 = rmsnorm(x, norm2_gain[l])
        g, u = jnp.split(h2 @ w_gate_up[l], 2, axis=-1)
        g = depthwise_conv(g, conv_w[l], conv_b[l])
        x = x + (jax.nn.silu(g) * u) @ w_down[l]

    return rmsnorm(x, final_gain)
```

```python
import functools
import math

import numpy as np
import jax
import jax.numpy as jnp
from jax import lax
from jax.experimental import pallas as pl
from jax.experimental.pallas import tpu as pltpu

F32 = jnp.float32
BF16 = jnp.bfloat16

HEAD_DIM = 128
N_DIFF_HEADS = 4
DIFF_V_DIM = 2 * HEAD_DIM
N_DIL_HEADS = 8
DIFF_QK_COLS = N_DIFF_HEADS * 2 * HEAD_DIM
DIFF_WIDTH = N_DIFF_HEADS * DIFF_V_DIM
DIL_WIDTH = N_DIL_HEADS * HEAD_DIM
DILATED_PATTERNS = ((128, 1), (512, 4), (2048, 16))
N_REL_BUCKETS = 32
REL_MAX_DISTANCE = 1024
NORM_EPS = 1e-6
SUBLN_EPS = 1e-5
NEG_INF = -1e30
LOG2E = math.log2(math.e)
LAM_INIT = 0.8 - 0.6 * math.exp(-0.3 * 0)

BF16_SUBLANES = 16
DIFF_TILE = 1024
DIL_CHUNK = 1024
VMEM_LIMIT = 56 * 1024 * 1024


def _bucket_breaks():
    nb = N_REL_BUCKETS // 2
    max_exact = nb // 2
    rel = np.arange(-2 * REL_MAX_DISTANCE, 2 * REL_MAX_DISTANCE + 1)
    n = np.abs(rel)
    pos = np.log(np.maximum(n, 1) / max_exact) / math.log(REL_MAX_DISTANCE / max_exact) * (nb - max_exact)
    large = np.minimum(max_exact + np.floor(pos).astype(np.int64), nb - 1)
    bucket = np.where(rel > 0, nb, 0) + np.where(n < max_exact, n, large)
    breaks = [(int(rel[i]), int(bucket[i])) for i in range(1, len(rel)) if bucket[i] != bucket[i - 1]]
    return int(bucket[0]), breaks


FIRST_BUCKET, BUCKET_BREAKS = _bucket_breaks()
LAST_BUCKET = BUCKET_BREAKS[-1][1]
FAR_DIST = max(-BUCKET_BREAKS[0][0] + 1, BUCKET_BREAKS[-1][0])


def _bias_from_rel(rel, tab_ref, col):
    val = jnp.full(rel.shape, tab_ref[FIRST_BUCKET, col], F32)
    for thr, b in BUCKET_BREAKS:
        val = jnp.where(rel >= thr, tab_ref[b, col], val)
    return val


def _inproj_kernel(x_ref, g_ref, w_ref, cs_ref, o_ref, kt_ref, h_ref, *, k_tile):
    j = pl.program_id(1)

    @pl.when(j == 0)
    def _():
        x = x_ref[...]
        ms = jnp.mean(x * x, axis=-1, keepdims=True)
        h_ref[...] = (x * lax.rsqrt(ms + NORM_EPS) * g_ref[...]).astype(BF16)

    acc = jnp.dot(h_ref[...], w_ref[...], preferred_element_type=F32) * cs_ref[...]
    o_ref[...] = acc.astype(o_ref.dtype)

    @pl.when(j == k_tile)
    def _():
        kt_ref[...] = acc.T.astype(kt_ref.dtype)


def _in_proj(x2, gain, w_bf, colscale, *, tm=1024, tn=1024):
    S, D = x2.shape
    N = w_bf.shape[1]
    assert DIFF_QK_COLS == tn, "K^T output assumes the diff-head key columns are exactly one column tile"
    kern = functools.partial(_inproj_kernel, k_tile=DIFF_QK_COLS // tn)
    return pl.pallas_call(
        kern,
        grid=(S // tm, N // tn),
        in_specs=[
            pl.BlockSpec((tm, D), lambda i, j: (i, 0)),
            pl.BlockSpec((1, D), lambda i, j: (0, 0)),
            pl.BlockSpec((D, tn), lambda i, j: (0, j)),
            pl.BlockSpec((1, tn), lambda i, j: (0, j)),
        ],
        out_specs=[
            pl.BlockSpec((tm, tn), lambda i, j: (i, j)),
            pl.BlockSpec((tn, tm), lambda i, j: (0, i)),
        ],
        out_shape=[
            jax.ShapeDtypeStruct((S, N), BF16),
            jax.ShapeDtypeStruct((DIFF_QK_COLS, S), BF16),
        ],
        scratch_shapes=[pltpu.VMEM((tm, D), BF16)],
        compiler_params=pltpu.CompilerParams(
            dimension_semantics=("arbitrary", "arbitrary"), vmem_limit_bytes=VMEM_LIMIT),
        name="in_proj",
    )(x2, gain, w_bf, colscale)


def _diff_attn_kernel(tab_ref, q1_ref, q2_ref, k1t_ref, k2t_ref, v_ref, lam_ref, gain_ref, o_ref,
                      bias_ref, m_ref, l_ref, acc_ref, *, T, n_near, n_tiles):
    h = pl.program_id(0)
    qi = pl.program_id(1)

    @pl.when(qi == 0)
    def _():
        rows = 8
        col = lax.broadcasted_iota(jnp.int32, (rows, T), 1)
        row = lax.broadcasted_iota(jnp.int32, (rows, T), 0)
        for di, d in enumerate(range(-n_near, n_near + 1)):
            def fill(r, carry, di=di, d=d):
                r0 = pl.multiple_of(r * rows, rows)
                rel = col - row + (d * T - r0)
                bias_ref[di, pl.ds(r0, rows), :] = _bias_from_rel(rel, tab_ref, h)
                return carry
            lax.fori_loop(0, T // rows, fill, 0)

    c_left = tab_ref[FIRST_BUCKET, h]
    c_right = tab_ref[LAST_BUCKET, h]

    m_ref[...] = jnp.full(m_ref.shape, -jnp.inf, F32)
    l_ref[...] = jnp.zeros(l_ref.shape, F32)
    acc_ref[...] = jnp.zeros(acc_ref.shape, F32)

    def update(mi, q_ref, kt_ref, kt, bias_tile, bias_const):
        k0 = pl.multiple_of(kt * T, T)
        m = m_ref[mi]
        s = jnp.dot(q_ref[...], kt_ref[:, pl.ds(k0, T)], preferred_element_type=F32)
        if bias_tile is not None:
            s = s + bias_tile
            m_new = jnp.maximum(m, jnp.max(s, axis=-1, keepdims=True))
            shift = m_new
        else:
            m_new = jnp.maximum(m, jnp.max(s, axis=-1, keepdims=True) + bias_const)
            shift = m_new - bias_const
        alpha = jnp.exp2(m - m_new)
        p = jnp.exp2(s - shift)
        m_ref[mi] = m_new
        l_ref[mi] = alpha * l_ref[mi] + jnp.sum(p, axis=-1, keepdims=True)
        pv = jnp.dot(p.astype(BF16), v_ref[pl.ds(k0, T), :], preferred_element_type=F32)
        acc_ref[mi] = alpha * acc_ref[mi] + pv

    def tile(kt, bias_tile, bias_const):
        update(0, q1_ref, k1t_ref, kt, bias_tile, bias_const)
        update(1, q2_ref, k2t_ref, kt, bias_tile, bias_const)

    lo = jnp.maximum(qi - n_near, 0)
    hi = jnp.minimum(qi + n_near + 1, n_tiles)

    @pl.loop(0, lo)
    def _(kt):
        tile(kt, None, c_left)

    for di, d in enumerate(range(-n_near, n_near + 1)):
        kt = qi + d

        @pl.when(jnp.logical_and(kt >= 0, kt < n_tiles))
        def _(di=di, kt=kt):
            tile(kt, bias_ref[di], None)

    @pl.loop(hi, n_tiles)
    def _(kt):
        tile(kt, None, c_right)

    lam = lam_ref[0, 0]
    o = acc_ref[0] / l_ref[0] - lam * (acc_ref[1] / l_ref[1])
    ms = jnp.mean(o * o, axis=-1, keepdims=True)
    o = o * lax.rsqrt(ms + SUBLN_EPS) * gain_ref[...]
    o_ref[...] = (o * (1.0 - LAM_INIT)).astype(o_ref.dtype)


def _lambda_kernel(q1_ref, k1_ref, q2_ref, k2_ref, o_ref):
    a = jnp.sum(q1_ref[...] * k1_ref[...], axis=-1, keepdims=True)
    b = jnp.sum(q2_ref[...] * k2_ref[...], axis=-1, keepdims=True)
    o_ref[...] = jnp.exp(a) - jnp.exp(b) + LAM_INIT


def _lambda(lq1, lk1, lq2, lk2):
    return pl.pallas_call(
        _lambda_kernel, out_shape=jax.ShapeDtypeStruct((1, 1), F32), name="diff_lambda",
    )(lq1, lk1, lq2, lk2)


def _diff_attention(tab_diff, proj, kt_all, lam, gain):
    S = proj.shape[0]
    T = DIFF_TILE
    n_tiles = S // T
    n_near = -(-(FAR_DIST - 1) // T)
    kern = functools.partial(_diff_attn_kernel, T=T, n_near=n_near, n_tiles=n_tiles)
    qk_blocks = DIFF_QK_COLS // HEAD_DIM
    v_block0 = 2 * DIFF_QK_COLS // DIFF_V_DIM
    smem = pl.BlockSpec(memory_space=pltpu.SMEM)
    return pl.pallas_call(
        kern,
        grid=(N_DIFF_HEADS, n_tiles),
        in_specs=[
            smem,
            pl.BlockSpec((T, HEAD_DIM), lambda h, i: (i, 2 * h)),
            pl.BlockSpec((T, HEAD_DIM), lambda h, i: (i, 2 * h + 1)),
            pl.BlockSpec((HEAD_DIM, S), lambda h, i: (2 * h, 0), pipeline_mode=pl.Buffered(1)),
            pl.BlockSpec((HEAD_DIM, S), lambda h, i: (2 * h + 1, 0), pipeline_mode=pl.Buffered(1)),
            pl.BlockSpec((S, DIFF_V_DIM), lambda h, i: (0, v_block0 + h), pipeline_mode=pl.Buffered(1)),
            smem,
            pl.BlockSpec((1, DIFF_V_DIM), lambda h, i: (0, 0)),
        ],
        out_specs=pl.BlockSpec((T, DIFF_V_DIM), lambda h, i: (i, h)),
        out_shape=jax.ShapeDtypeStruct((S, DIFF_WIDTH), BF16),
        scratch_shapes=[pltpu.VMEM((2 * n_near + 1, T, T), F32),
                        pltpu.VMEM((2, T, 1), F32), pltpu.VMEM((2, T, 1), F32),
                        pltpu.VMEM((2, T, DIFF_V_DIM), F32)],
        compiler_params=pltpu.CompilerParams(
            dimension_semantics=("arbitrary", "arbitrary"), vmem_limit_bytes=VMEM_LIMIT),
        name="diff_attn",
    )(tab_diff, proj, proj, kt_all, kt_all, proj, lam, gain)


def _dilated_kernel(tab_ref, q_ref, kp_ref, km_ref, kn_ref, vp_ref, vm_ref, vn_ref, o_ref, lse_ref,
                    bias_ref, kx_ref, vx_ref, *, R, B, half, dilation, n_chunks):
    c = pl.program_id(0)
    n = pl.program_id(1)
    W = B + 2 * half
    nblk = R // B

    @pl.when(jnp.logical_and(c == 0, n == 0))
    def _():
        rows = 8
        col = lax.broadcasted_iota(jnp.int32, (rows, W), 1)
        row = lax.broadcasted_iota(jnp.int32, (rows, W), 0)
        for hh in range(N_DIL_HEADS):
            def fill(r, carry, hh=hh):
                r0 = pl.multiple_of(r * rows, rows)
                off = col - half - (row + r0)
                bias = _bias_from_rel(off * dilation, tab_ref, N_DIFF_HEADS + hh)
                base = jnp.where(jnp.abs(off) <= half, bias, NEG_INF * LOG2E)
                bias_ref[hh, 1, pl.ds(r0, rows), :] = base
                bias_ref[hh, 0, pl.ds(r0, rows), :] = jnp.where(col >= half, base, NEG_INF * LOG2E)
                bias_ref[hh, 2, pl.ds(r0, rows), :] = jnp.where(col < B + half, base, NEG_INF * LOG2E)
                return carry
            lax.fori_loop(0, B // rows, fill, 0)

    kx_ref[0:half, :] = kp_ref[...]
    kx_ref[half:half + R, :] = km_ref[...]
    kx_ref[half + R:, :] = kn_ref[...]
    vx_ref[0:half, :] = vp_ref[...]
    vx_ref[half:half + R, :] = vm_ref[...]
    vx_ref[half + R:, :] = vn_ref[...]

    def head_body(hh, carry):
        c0 = pl.multiple_of(hh * HEAD_DIM, HEAD_DIM)

        def blk_body(b, carry2):
            r0 = pl.multiple_of(b * B, B)
            first = jnp.logical_and(n == 0, b == 0)
            last = jnp.logical_and(n == n_chunks - 1, b == nblk - 1)
            var = jnp.where(first, 0, jnp.where(last, 2, 1))
            q = q_ref[pl.ds(r0, B), pl.ds(c0, HEAD_DIM)]
            k = kx_ref[pl.ds(r0, W), pl.ds(c0, HEAD_DIM)]
            v = vx_ref[pl.ds(r0, W), pl.ds(c0, HEAD_DIM)]
            s = lax.dot_general(q, k, (((1,), (1,)), ((), ())), preferred_element_type=F32)
            s = s + bias_ref[hh, var]
            m = jnp.max(s, axis=-1, keepdims=True)
            e = jnp.exp2(s - m)
            den = jnp.sum(e, axis=-1, keepdims=True)
            o = jnp.dot(e.astype(BF16), v, preferred_element_type=F32) / den
            o_ref[pl.ds(r0, B), pl.ds(c0, HEAD_DIM)] = o.astype(o_ref.dtype)
            lse = m + jnp.log2(den)
            lse_ref[pl.ds(r0, B), pl.ds(c0, HEAD_DIM)] = jnp.broadcast_to(lse, (B, HEAD_DIM))
            return carry2

        return lax.fori_loop(0, nblk, blk_body, carry)

    lax.fori_loop(0, N_DIL_HEADS, head_body, 0)


def _dilated_pattern(tab, proj, window, dilation, *, B=256):
    S, N = proj.shape
    L = S // dilation
    R = min(DIL_CHUNK, L)
    half = window // (2 * dilation)
    assert L % R == 0 and R % B == 0 and half % BF16_SUBLANES == 0 and R % half == 0
    n_chunks = L // R
    view = proj.reshape(L, dilation * N)
    cb = N // DIL_WIDTH
    q_blk, k_blk, v_blk = cb - 3, cb - 2, cb - 1
    hb = R // half
    n_hblk = L // half

    def main(blk):
        return pl.BlockSpec((R, DIL_WIDTH), lambda c, n: (n, c * cb + blk))

    def prev(blk):
        return pl.BlockSpec((half, DIL_WIDTH), lambda c, n: (jnp.maximum(n * hb - 1, 0), c * cb + blk))

    def nxt(blk):
        return pl.BlockSpec((half, DIL_WIDTH), lambda c, n: (jnp.minimum((n + 1) * hb, n_hblk - 1), c * cb + blk))

    kern = functools.partial(_dilated_kernel, R=R, B=B, half=half, dilation=dilation, n_chunks=n_chunks)
    out_spec = pl.BlockSpec((R, DIL_WIDTH), lambda c, n: (n, c))
    o, lse = pl.pallas_call(
        kern,
        grid=(dilation, n_chunks),
        in_specs=[pl.BlockSpec(memory_space=pltpu.SMEM),
                  main(q_blk), prev(k_blk), main(k_blk), nxt(k_blk), prev(v_blk), main(v_blk), nxt(v_blk)],
        out_specs=[out_spec, out_spec],
        out_shape=[jax.ShapeDtypeStruct((L, dilation * DIL_WIDTH), BF16),
                   jax.ShapeDtypeStruct((L, dilation * DIL_WIDTH), F32)],
        scratch_shapes=[pltpu.VMEM((N_DIL_HEADS, 3, B, B + 2 * half), F32),
                        pltpu.VMEM((R + 2 * half, DIL_WIDTH), BF16),
                        pltpu.VMEM((R + 2 * half, DIL_WIDTH), BF16)],
        compiler_params=pltpu.CompilerParams(
            dimension_semantics=("arbitrary", "arbitrary"), vmem_limit_bytes=VMEM_LIMIT),
        name=f"dilated_d{dilation}",
    )(tab, view, view, view, view, view, view, view)
    return o.reshape(S, DIL_WIDTH), lse.reshape(S, DIL_WIDTH)


def _combine_kernel(o1_ref, o2_ref, o3_ref, l1_ref, l2_ref, l3_ref, g_ref, out_ref):
    l1, l2, l3 = l1_ref[...], l2_ref[...], l3_ref[...]
    m = jnp.maximum(jnp.maximum(l1, l2), l3)
    w1, w2, w3 = jnp.exp2(l1 - m), jnp.exp2(l2 - m), jnp.exp2(l3 - m)
    tot = w1 + w2 + w3
    o = (w1 / tot) * o1_ref[...].astype(F32) + (w2 / tot) * o2_ref[...].astype(F32) \
        + (w3 / tot) * o3_ref[...].astype(F32)
    for hh in range(N_DIL_HEADS):
        sl = slice(hh * HEAD_DIM, (hh + 1) * HEAD_DIM)
        oh = o[:, sl]
        ms = jnp.mean(oh * oh, axis=-1, keepdims=True)
        out_ref[:, sl] = (oh * lax.rsqrt(ms + NORM_EPS) * g_ref[:, sl]).astype(out_ref.dtype)


def _combine(outs, lses, gain, *, tm=1024):
    S = outs[0].shape[0]
    spec = pl.BlockSpec((tm, DIL_WIDTH), lambda i: (i, 0))
    return pl.pallas_call(
        _combine_kernel,
        grid=(S // tm,),
        in_specs=[spec] * 6 + [pl.BlockSpec((1, DIL_WIDTH), lambda i: (0, 0))],
        out_specs=spec,
        out_shape=jax.ShapeDtypeStruct((S, DIL_WIDTH), BF16),
        compiler_params=pltpu.CompilerParams(
            dimension_semantics=("arbitrary",), vmem_limit_bytes=VMEM_LIMIT),
        name="dilated_combine",
    )(*outs, *lses, gain)


def _outproj_kernel(od_ref, ol_ref, wd_ref, wl_ref, x_ref, g_ref, x1_ref, h2_ref):
    acc = jnp.dot(od_ref[...], wd_ref[...], preferred_element_type=F32)
    acc = acc + jnp.dot(ol_ref[...], wl_ref[...], preferred_element_type=F32)
    x1 = x_ref[...] + acc
    x1_ref[...] = x1
    ms = jnp.mean(x1 * x1, axis=-1, keepdims=True)
    h2_ref[...] = (x1 * lax.rsqrt(ms + NORM_EPS) * g_ref[...]).astype(h2_ref.dtype)


def _out_proj(o_d, o_l, w_bf, x2, gain, *, tm=512):
    S, D = x2.shape
    return pl.pallas_call(
        _outproj_kernel,
        grid=(S // tm,),
        in_specs=[
            pl.BlockSpec((tm, DIFF_WIDTH), lambda i: (i, 0)),
            pl.BlockSpec((tm, DIL_WIDTH), lambda i: (i, 0)),
            pl.BlockSpec((DIFF_WIDTH, D), lambda i: (0, 0)),
            pl.BlockSpec((DIL_WIDTH, D), lambda i: (1, 0)),
            pl.BlockSpec((tm, D), lambda i: (i, 0)),
            pl.BlockSpec((1, D), lambda i: (0, 0)),
        ],
        out_specs=[pl.BlockSpec((tm, D), lambda i: (i, 0)), pl.BlockSpec((tm, D), lambda i: (i, 0))],
        out_shape=[jax.ShapeDtypeStruct((S, D), F32), jax.ShapeDtypeStruct((S, D), BF16)],
        compiler_params=pltpu.CompilerParams(
            dimension_semantics=("arbitrary",), vmem_limit_bytes=VMEM_LIMIT),
        name="out_proj",
    )(o_d, o_l, w_bf, w_bf, x2, gain)


def _ffn_up_kernel(hm_ref, hp_ref, hn_ref, wg_ref, wu_ref, cw_ref, cb_ref, o_ref, lhs_ref, *, tm, n_row_tiles):
    i = pl.program_id(0)
    j = pl.program_id(1)
    halo = BF16_SUBLANES

    @pl.when(j == 0)
    def _():
        lhs_ref[0:halo, :] = jnp.where(i == 0, jnp.zeros_like(hp_ref[...]), hp_ref[...])
        lhs_ref[halo:halo + tm, :] = hm_ref[...]
        lhs_ref[halo + tm:, :] = jnp.where(i == n_row_tiles - 1, jnp.zeros_like(hn_ref[...]), hn_ref[...])

    g = jnp.dot(lhs_ref[...], wg_ref[...], preferred_element_type=F32)
    u = jnp.dot(lhs_ref[halo:halo + tm, :], wu_ref[...], preferred_element_type=F32)
    rows = tm + 2 * halo
    g_prev = pltpu.roll(g, 1, axis=0)
    g_next = pltpu.roll(g, rows - 1, axis=0)
    y = cw_ref[0:1, :] * g_prev + cw_ref[1:2, :] * g + cw_ref[2:3, :] * g_next + cb_ref[...]
    y = y[halo:halo + tm, :]
    act = y * (1.0 / (1.0 + jnp.exp(-y))) * u
    o_ref[...] = act.astype(o_ref.dtype)


def _ffn_up(h2, w_bf, conv_w, conv_b, *, tm=1024, tn=512):
    S, D = h2.shape
    d_ff = conv_w.shape[1]
    assert d_ff % tn == 0
    nj = d_ff // tn
    ni = S // tm
    hb = tm // BF16_SUBLANES
    n_hblk = S // BF16_SUBLANES
    kern = functools.partial(_ffn_up_kernel, tm=tm, n_row_tiles=ni)
    return pl.pallas_call(
        kern,
        grid=(ni, nj),
        in_specs=[
            pl.BlockSpec((tm, D), lambda i, j: (i, 0)),
            pl.BlockSpec((BF16_SUBLANES, D), lambda i, j: (jnp.maximum(i * hb - 1, 0), 0)),
            pl.BlockSpec((BF16_SUBLANES, D), lambda i, j: (jnp.minimum((i + 1) * hb, n_hblk - 1), 0)),
            pl.BlockSpec((D, tn), lambda i, j: (0, j)),
            pl.BlockSpec((D, tn), lambda i, j: (0, nj + j)),
            pl.BlockSpec((3, tn), lambda i, j: (0, j)),
            pl.BlockSpec((1, tn), lambda i, j: (0, j)),
        ],
        out_specs=pl.BlockSpec((tm, tn), lambda i, j: (i, j)),
        out_shape=jax.ShapeDtypeStruct((S, d_ff), BF16),
        scratch_shapes=[pltpu.VMEM((tm + 2 * BF16_SUBLANES, D), BF16)],
        compiler_params=pltpu.CompilerParams(
            dimension_semantics=("arbitrary", "arbitrary"), vmem_limit_bytes=VMEM_LIMIT),
        name="ffn_up",
    )(h2, h2, h2, w_bf, w_bf, conv_w, conv_b)


def _ffn_down_kernel(a_ref, w_ref, x1_ref, g_ref, o_ref, acc_ref, *, n_k):
    k = pl.program_id(1)

    @pl.when(k == 0)
    def _():
        acc_ref[...] = x1_ref[...]

    acc_ref[...] += jnp.dot(a_ref[...], w_ref[...], preferred_element_type=F32)

    @pl.when(k == n_k - 1)
    def _():
        y = acc_ref[...]
        ms = jnp.mean(y * y, axis=-1, keepdims=True)
        o_ref[...] = y * lax.rsqrt(ms + NORM_EPS) * g_ref[...]


def _ffn_down(act, w_bf, x1, gain, *, tm=512, tk=512):
    S, d_ff = act.shape
    D = x1.shape[1]
    n_k = d_ff // tk
    kern = functools.partial(_ffn_down_kernel, n_k=n_k)
    return pl.pallas_call(
        kern,
        grid=(S // tm, n_k),
        in_specs=[
            pl.BlockSpec((tm, tk), lambda i, k: (i, k)),
            pl.BlockSpec((tk, D), lambda i, k: (k, 0)),
            pl.BlockSpec((tm, D), lambda i, k: (i, 0)),
            pl.BlockSpec((1, D), lambda i, k: (0, 0)),
        ],
        out_specs=pl.BlockSpec((tm, D), lambda i, k: (i, 0)),
        out_shape=jax.ShapeDtypeStruct((S, D), F32),
        scratch_shapes=[pltpu.VMEM((tm, D), F32)],
        compiler_params=pltpu.CompilerParams(
            dimension_semantics=("arbitrary", "arbitrary"), vmem_limit_bytes=VMEM_LIMIT),
        name="ffn_down",
    )(act, w_bf, x1, gain)


def kernel(x, norm1_gain, w_in, rel_bias_table, lambda_q1, lambda_k1, lambda_q2, lambda_k2,
           diff_subln_gain, dil_out_gain, w_out, norm2_gain, w_gate_up, conv_w, conv_b, w_down, final_gain):
    B, S, D = x.shape
    assert B == 1 and w_in.shape[0] == 1
    x2 = x.reshape(S, D)
    n_cols = w_in.shape[2]

    qscale = LOG2E / math.sqrt(HEAD_DIM)
    col = np.arange(n_cols)
    dil_q0 = 2 * DIFF_QK_COLS + DIFF_WIDTH
    is_q = (col < DIFF_QK_COLS) | ((col >= dil_q0) & (col < dil_q0 + DIL_WIDTH))
    colscale = jnp.asarray(np.where(is_q, qscale, 1.0).astype(np.float32)).reshape(1, n_cols)
    tab = rel_bias_table.astype(F32) * LOG2E

    proj, kt_all = _in_proj(x2, norm1_gain.reshape(1, D), w_in[0].astype(BF16), colscale)

    lam = _lambda(lambda_q1.reshape(1, -1), lambda_k1.reshape(1, -1),
                  lambda_q2.reshape(1, -1), lambda_k2.reshape(1, -1))
    o_d = _diff_attention(tab, proj, kt_all, lam, diff_subln_gain.reshape(1, -1))

    outs, lses = [], []
    for window, dilation in DILATED_PATTERNS:
        o_p, lse_p = _dilated_pattern(tab, proj, window, dilation)
        outs.append(o_p)
        lses.append(lse_p)
    o_l = _combine(outs, lses, dil_out_gain.reshape(1, -1))

    x1, h2 = _out_proj(o_d, o_l, w_out[0].astype(BF16), x2, norm2_gain.reshape(1, D))
    act = _ffn_up(h2, w_gate_up[0].astype(BF16), conv_w[0], conv_b.reshape(1, -1))
    out = _ffn_down(act, w_down[0].astype(BF16), x1, final_gain.reshape(1, D))
    return out.reshape(B, S, D)
```

```python
import functools
import math

import numpy as np
import jax
import jax.numpy as jnp
from jax import lax
from jax.experimental import pallas as pl
from jax.experimental.pallas import tpu as pltpu

F32 = jnp.float32
BF16 = jnp.bfloat16

HEAD_DIM = 128
N_DIFF_HEADS = 4
DIFF_V_DIM = 2 * HEAD_DIM
N_DIL_HEADS = 8
DIFF_QK_COLS = N_DIFF_HEADS * 2 * HEAD_DIM
DIFF_WIDTH = N_DIFF_HEADS * DIFF_V_DIM
DIL_WIDTH = N_DIL_HEADS * HEAD_DIM
DILATED_PATTERNS = ((128, 1), (512, 4), (2048, 16))
N_REL_BUCKETS = 32
REL_MAX_DISTANCE = 1024
NORM_EPS = 1e-6
SUBLN_EPS = 1e-5
NEG_INF = -1e30
LOG2E = math.log2(math.e)
LAM_INIT = 0.8 - 0.6 * math.exp(-0.3 * 0)

BF16_SUBLANES = 16
DIFF_TILE = 1024
DIFF_QUERY_PANEL = 256
DIL_CHUNK = 1024
VMEM_LIMIT = 56 * 1024 * 1024


def _bucket_breaks():
    nb = N_REL_BUCKETS // 2
    max_exact = nb // 2
    rel = np.arange(-2 * REL_MAX_DISTANCE, 2 * REL_MAX_DISTANCE + 1)
    n = np.abs(rel)
    pos = np.log(np.maximum(n, 1) / max_exact) / math.log(REL_MAX_DISTANCE / max_exact) * (nb - max_exact)
    large = np.minimum(max_exact + np.floor(pos).astype(np.int64), nb - 1)
    bucket = np.where(rel > 0, nb, 0) + np.where(n < max_exact, n, large)
    breaks = [(int(rel[i]), int(bucket[i])) for i in range(1, len(rel)) if bucket[i] != bucket[i - 1]]
    return int(bucket[0]), breaks


FIRST_BUCKET, BUCKET_BREAKS = _bucket_breaks()
LAST_BUCKET = BUCKET_BREAKS[-1][1]
FAR_DIST = max(-BUCKET_BREAKS[0][0] + 1, BUCKET_BREAKS[-1][0])


def _bias_from_rel(rel, tab_ref, col):
    val = jnp.full(rel.shape, tab_ref[FIRST_BUCKET, col], F32)
    for thr, b in BUCKET_BREAKS:
        val = jnp.where(rel >= thr, tab_ref[b, col], val)
    return val


def _inproj_kernel(x_ref, g_ref, w_ref, cs_ref, o_ref, qt_ref, vt_ref, h_ref, *, q_tile, v_tile):
    j = pl.program_id(1)

    @pl.when(j == 0)
    def _():
        x = x_ref[...]
        ms = jnp.mean(x * x, axis=-1, keepdims=True)
        h_ref[...] = (x * lax.rsqrt(ms + NORM_EPS) * g_ref[...]).astype(BF16)

    acc = jnp.dot(h_ref[...], w_ref[...], preferred_element_type=F32) * cs_ref[...]
    o_ref[...] = acc.astype(o_ref.dtype)

    @pl.when(j == q_tile)
    def _():
        qt_ref[...] = acc.T.astype(qt_ref.dtype)

    @pl.when(j == v_tile)
    def _():
        vt_ref[...] = acc.T.astype(vt_ref.dtype)


def _in_proj(x2, gain, w_bf, colscale, *, tm=1024, tn=1024):
    S, D = x2.shape
    N = w_bf.shape[1]
    assert DIFF_QK_COLS == tn and DIFF_WIDTH == tn, "transposed outputs assume one column tile each"
    kern = functools.partial(_inproj_kernel, q_tile=0, v_tile=2 * DIFF_QK_COLS // tn)
    return pl.pallas_call(
        kern,
        grid=(S // tm, N // tn),
        in_specs=[
            pl.BlockSpec((tm, D), lambda i, j: (i, 0)),
            pl.BlockSpec((1, D), lambda i, j: (0, 0)),
            pl.BlockSpec((D, tn), lambda i, j: (0, j)),
            pl.BlockSpec((1, tn), lambda i, j: (0, j)),
        ],
        out_specs=[
            pl.BlockSpec((tm, tn), lambda i, j: (i, j)),
            pl.BlockSpec((tn, tm), lambda i, j: (0, i)),
            pl.BlockSpec((tn, tm), lambda i, j: (0, i)),
        ],
        out_shape=[
            jax.ShapeDtypeStruct((S, N), BF16),
            jax.ShapeDtypeStruct((DIFF_QK_COLS, S), BF16),
            jax.ShapeDtypeStruct((DIFF_WIDTH, S), BF16),
        ],
        scratch_shapes=[pltpu.VMEM((tm, D), BF16)],
        compiler_params=pltpu.CompilerParams(
            dimension_semantics=("arbitrary", "arbitrary"), vmem_limit_bytes=VMEM_LIMIT),
        name="in_proj",
    )(x2, gain, w_bf, colscale)


def _diff_attn_kernel(tab_ref, q1t_ref, q2t_ref, k1_ref, k2_ref, vt_ref, lam_ref, gain_ref, o_ref,
                      bias_ref, m_ref, l_ref, acc_ref, *, T, QP, n_near, n_tiles):
    h = pl.program_id(0)
    qi = pl.program_id(1)

    @pl.when(qi == 0)
    def _():
        rows = 8
        col = lax.broadcasted_iota(jnp.int32, (rows, T), 1)
        row = lax.broadcasted_iota(jnp.int32, (rows, T), 0)
        for di, d in enumerate(range(-n_near, n_near + 1)):
            def fill(r, carry, di=di, d=d):
                r0 = pl.multiple_of(r * rows, rows)
                rel = row - col + (d * T + r0)
                bias_ref[di, pl.ds(r0, rows), :] = _bias_from_rel(rel, tab_ref, h)
                return carry
            lax.fori_loop(0, T // rows, fill, 0)

    c_left = tab_ref[FIRST_BUCKET, h]
    c_right = tab_ref[LAST_BUCKET, h]

    m_ref[...] = jnp.full(m_ref.shape, -jnp.inf, F32)
    l_ref[...] = jnp.zeros(l_ref.shape, F32)
    acc_ref[...] = jnp.zeros(acc_ref.shape, F32)

    def update(mi, qt_ref, k_ref, kt, qp, bias_di, bias_const):
        k0 = pl.multiple_of(kt * T, T)
        qs = slice(qp * QP, (qp + 1) * QP)
        m = m_ref[mi, :, qs]
        s = jnp.dot(k_ref[pl.ds(k0, T), :], qt_ref[:, qs], preferred_element_type=F32)
        if bias_di is not None:
            s = s + bias_ref[bias_di, :, qs]
            m_new = jnp.maximum(m, jnp.max(s, axis=0, keepdims=True))
            shift = m_new
        else:
            m_new = jnp.maximum(m, jnp.max(s, axis=0, keepdims=True) + bias_const)
            shift = m_new - bias_const
        alpha = jnp.exp2(m - m_new)
        p = jnp.exp2(s - shift)
        m_ref[mi, :, qs] = m_new
        l_ref[mi, :, qs] = alpha * l_ref[mi, :, qs] + jnp.sum(p, axis=0, keepdims=True)
        pv = jnp.dot(vt_ref[:, pl.ds(k0, T)], p.astype(BF16), preferred_element_type=F32)
        acc_ref[mi, :, qs] = alpha * acc_ref[mi, :, qs] + pv

    def tile(kt, bias_di, bias_const):
        for qp in range(T // QP):
            update(0, q1t_ref, k1_ref, kt, qp, bias_di, bias_const)
            update(1, q2t_ref, k2_ref, kt, qp, bias_di, bias_const)

    lo = jnp.maximum(qi - n_near, 0)
    hi = jnp.minimum(qi + n_near + 1, n_tiles)

    @pl.loop(0, lo)
    def _(kt):
        tile(kt, None, c_left)

    for di, d in enumerate(range(-n_near, n_near + 1)):
        kt = qi + d

        @pl.when(jnp.logical_and(kt >= 0, kt < n_tiles))
        def _(di=di, kt=kt):
            tile(kt, di, None)

    @pl.loop(hi, n_tiles)
    def _(kt):
        tile(kt, None, c_right)

    lam = lam_ref[0, 0]
    o = acc_ref[0] / l_ref[0] - lam * (acc_ref[1] / l_ref[1])
    ms = jnp.mean(o * o, axis=0, keepdims=True)
    o = o * lax.rsqrt(ms + SUBLN_EPS) * (gain_ref[...] * (1.0 - LAM_INIT))
    o_ref[...] = o.T.astype(o_ref.dtype)


def _lambda_kernel(q1_ref, k1_ref, q2_ref, k2_ref, o_ref):
    a = jnp.sum(q1_ref[...] * k1_ref[...], axis=-1, keepdims=True)
    b = jnp.sum(q2_ref[...] * k2_ref[...], axis=-1, keepdims=True)
    o_ref[...] = jnp.exp(a) - jnp.exp(b) + LAM_INIT


def _lambda(lq1, lk1, lq2, lk2):
    return pl.pallas_call(
        _lambda_kernel, out_shape=jax.ShapeDtypeStruct((1, 1), F32), name="diff_lambda",
    )(lq1, lk1, lq2, lk2)


def _diff_attention(tab_diff, proj, qt_all, vt_all, lam, gain_col):
    S = proj.shape[0]
    T = DIFF_TILE
    n_tiles = S // T
    n_near = -(-(FAR_DIST - 1) // T)
    kern = functools.partial(_diff_attn_kernel, T=T, QP=DIFF_QUERY_PANEL, n_near=n_near, n_tiles=n_tiles)
    k_block0 = DIFF_QK_COLS // HEAD_DIM
    smem = pl.BlockSpec(memory_space=pltpu.SMEM)
    return pl.pallas_call(
        kern,
        grid=(N_DIFF_HEADS, n_tiles),
        in_specs=[
            smem,
            pl.BlockSpec((HEAD_DIM, T), lambda h, i: (2 * h, i)),
            pl.BlockSpec((HEAD_DIM, T), lambda h, i: (2 * h + 1, i)),
            pl.BlockSpec((S, HEAD_DIM), lambda h, i: (0, k_block0 + 2 * h), pipeline_mode=pl.Buffered(1)),
            pl.BlockSpec((S, HEAD_DIM), lambda h, i: (0, k_block0 + 2 * h + 1), pipeline_mode=pl.Buffered(1)),
            pl.BlockSpec((DIFF_V_DIM, S), lambda h, i: (h, 0), pipeline_mode=pl.Buffered(1)),
            smem,
            pl.BlockSpec((DIFF_V_DIM, 1), lambda h, i: (0, 0)),
        ],
        out_specs=pl.BlockSpec((T, DIFF_V_DIM), lambda h, i: (i, h)),
        out_shape=jax.ShapeDtypeStruct((S, DIFF_WIDTH), BF16),
        scratch_shapes=[pltpu.VMEM((2 * n_near + 1, T, T), F32),
                        pltpu.VMEM((2, 1, T), F32), pltpu.VMEM((2, 1, T), F32),
                        pltpu.VMEM((2, DIFF_V_DIM, T), F32)],
        compiler_params=pltpu.CompilerParams(
            dimension_semantics=("arbitrary", "arbitrary"), vmem_limit_bytes=VMEM_LIMIT),
        name="diff_attn",
    )(tab_diff, qt_all, qt_all, proj, proj, vt_all, lam, gain_col)


def _dilated_kernel(tab_ref, q_ref, kp_ref, km_ref, kn_ref, vp_ref, vm_ref, vn_ref, o_ref, lse_ref,
                    bias_ref, kx_ref, vx_ref, *, R, B, half, dilation, n_chunks):
    c = pl.program_id(0)
    n = pl.program_id(1)
    W = B + 2 * half
    nblk = R // B

    @pl.when(jnp.logical_and(c == 0, n == 0))
    def _():
        rows = 8
        col = lax.broadcasted_iota(jnp.int32, (rows, W), 1)
        row = lax.broadcasted_iota(jnp.int32, (rows, W), 0)
        for hh in range(N_DIL_HEADS):
            def fill(r, carry, hh=hh):
                r0 = pl.multiple_of(r * rows, rows)
                off = col - half - (row + r0)
                bias = _bias_from_rel(off * dilation, tab_ref, N_DIFF_HEADS + hh)
                base = jnp.where(jnp.abs(off) <= half, bias, NEG_INF * LOG2E)
                bias_ref[hh, 1, pl.ds(r0, rows), :] = base
                bias_ref[hh, 0, pl.ds(r0, rows), :] = jnp.where(col >= half, base, NEG_INF * LOG2E)
                bias_ref[hh, 2, pl.ds(r0, rows), :] = jnp.where(col < B + half, base, NEG_INF * LOG2E)
                return carry
            lax.fori_loop(0, B // rows, fill, 0)

    kx_ref[0:half, :] = kp_ref[...]
    kx_ref[half:half + R, :] = km_ref[...]
    kx_ref[half + R:, :] = kn_ref[...]
    vx_ref[0:half, :] = vp_ref[...]
    vx_ref[half:half + R, :] = vm_ref[...]
    vx_ref[half + R:, :] = vn_ref[...]

    def head_body(hh, carry):
        c0 = pl.multiple_of(hh * HEAD_DIM, HEAD_DIM)

        def blk_body(b, carry2):
            r0 = pl.multiple_of(b * B, B)
            first = jnp.logical_and(n == 0, b == 0)
            last = jnp.logical_and(n == n_chunks - 1, b == nblk - 1)
            var = jnp.where(first, 0, jnp.where(last, 2, 1))
            q = q_ref[pl.ds(r0, B), pl.ds(c0, HEAD_DIM)]
            k = kx_ref[pl.ds(r0, W), pl.ds(c0, HEAD_DIM)]
            v = vx_ref[pl.ds(r0, W), pl.ds(c0, HEAD_DIM)]
            s = lax.dot_general(q, k, (((1,), (1,)), ((), ())), preferred_element_type=F32)
            s = s + bias_ref[hh, var]
            m = jnp.max(s, axis=-1, keepdims=True)
            e = jnp.exp2(s - m)
            den = jnp.sum(e, axis=-1, keepdims=True)
            o = jnp.dot(e.astype(BF16), v, preferred_element_type=F32) / den
            o_ref[pl.ds(r0, B), pl.ds(c0, HEAD_DIM)] = o.astype(o_ref.dtype)
            lse = m + jnp.log2(den)
            lse_ref[pl.ds(r0, B), pl.ds(c0, HEAD_DIM)] = jnp.broadcast_to(lse, (B, HEAD_DIM))
            return carry2

        return lax.fori_loop(0, nblk, blk_body, carry)

    lax.fori_loop(0, N_DIL_HEADS, head_body, 0)


def _dilated_pattern(tab, proj, window, dilation, *, B=256):
    S, N = proj.shape
    L = S // dilation
    R = min(DIL_CHUNK, L)
    half = window // (2 * dilation)
    assert L % R == 0 and R % B == 0 and half % BF16_SUBLANES == 0 and R % half == 0
    n_chunks = L // R
    view = proj.reshape(L, dilation * N)
    cb = N // DIL_WIDTH
    q_blk, k_blk, v_blk = cb - 3, cb - 2, cb - 1
    hb = R // half
    n_hblk = L // half

    def main(blk):
        return pl.BlockSpec((R, DIL_WIDTH), lambda c, n: (n, c * cb + blk))

    def prev(blk):
        return pl.BlockSpec((half, DIL_WIDTH), lambda c, n: (jnp.maximum(n * hb - 1, 0), c * cb + blk))

    def nxt(blk):
        return pl.BlockSpec((half, DIL_WIDTH), lambda c, n: (jnp.minimum((n + 1) * hb, n_hblk - 1), c * cb + blk))

    kern = functools.partial(_dilated_kernel, R=R, B=B, half=half, dilation=dilation, n_chunks=n_chunks)
    out_spec = pl.BlockSpec((R, DIL_WIDTH), lambda c, n: (n, c))
    o, lse = pl.pallas_call(
        kern,
        grid=(dilation, n_chunks),
        in_specs=[pl.BlockSpec(memory_space=pltpu.SMEM),
                  main(q_blk), prev(k_blk), main(k_blk), nxt(k_blk), prev(v_blk), main(v_blk), nxt(v_blk)],
        out_specs=[out_spec, out_spec],
        out_shape=[jax.ShapeDtypeStruct((L, dilation * DIL_WIDTH), BF16),
                   jax.ShapeDtypeStruct((L, dilation * DIL_WIDTH), F32)],
        scratch_shapes=[pltpu.VMEM((N_DIL_HEADS, 3, B, B + 2 * half), F32),
                        pltpu.VMEM((R + 2 * half, DIL_WIDTH), BF16),
                        pltpu.VMEM((R + 2 * half, DIL_WIDTH), BF16)],
        compiler_params=pltpu.CompilerParams(
            dimension_semantics=("arbitrary", "arbitrary"), vmem_limit_bytes=VMEM_LIMIT),
        name=f"dilated_d{dilation}",
    )(tab, view, view, view, view, view, view, view)
    return o.reshape(S, DIL_WIDTH), lse.reshape(S, DIL_WIDTH)


def _combine_kernel(o1_ref, o2_ref, o3_ref, l1_ref, l2_ref, l3_ref, g_ref, out_ref):
    l1, l2, l3 = l1_ref[...], l2_ref[...], l3_ref[...]
    m = jnp.maximum(jnp.maximum(l1, l2), l3)
    w1, w2, w3 = jnp.exp2(l1 - m), jnp.exp2(l2 - m), jnp.exp2(l3 - m)
    tot = w1 + w2 + w3
    o = (w1 / tot) * o1_ref[...].astype(F32) + (w2 / tot) * o2_ref[...].astype(F32) \
        + (w3 / tot) * o3_ref[...].astype(F32)
    for hh in range(N_DIL_HEADS):
        sl = slice(hh * HEAD_DIM, (hh + 1) * HEAD_DIM)
        oh = o[:, sl]
        ms = jnp.mean(oh * oh, axis=-1, keepdims=True)
        out_ref[:, sl] = (oh * lax.rsqrt(ms + NORM_EPS) * g_ref[:, sl]).astype(out_ref.dtype)


def _combine(outs, lses, gain, *, tm=1024):
    S = outs[0].shape[0]
    spec = pl.BlockSpec((tm, DIL_WIDTH), lambda i: (i, 0))
    return pl.pallas_call(
        _combine_kernel,
        grid=(S // tm,),
        in_specs=[spec] * 6 + [pl.BlockSpec((1, DIL_WIDTH), lambda i: (0, 0))],
        out_specs=spec,
        out_shape=jax.ShapeDtypeStruct((S, DIL_WIDTH), BF16),
        compiler_params=pltpu.CompilerParams(
            dimension_semantics=("arbitrary",), vmem_limit_bytes=VMEM_LIMIT),
        name="dilated_combine",
    )(*outs, *lses, gain)


def _outproj_kernel(od_ref, ol_ref, wd_ref, wl_ref, x_ref, g_ref, x1_ref, h2_ref):
    acc = jnp.dot(od_ref[...], wd_ref[...], preferred_element_type=F32)
    acc = acc + jnp.dot(ol_ref[...], wl_ref[...], preferred_element_type=F32)
    x1 = x_ref[...] + acc
    x1_ref[...] = x1
    ms = jnp.mean(x1 * x1, axis=-1, keepdims=True)
    h2_ref[...] = (x1 * lax.rsqrt(ms + NORM_EPS) * g_ref[...]).astype(h2_ref.dtype)


def _out_proj(o_d, o_l, w_bf, x2, gain, *, tm=512):
    S, D = x2.shape
    return pl.pallas_call(
        _outproj_kernel,
        grid=(S // tm,),
        in_specs=[
            pl.BlockSpec((tm, DIFF_WIDTH), lambda i: (i, 0)),
            pl.BlockSpec((tm, DIL_WIDTH), lambda i: (i, 0)),
            pl.BlockSpec((DIFF_WIDTH, D), lambda i: (0, 0)),
            pl.BlockSpec((DIL_WIDTH, D), lambda i: (1, 0)),
            pl.BlockSpec((tm, D), lambda i: (i, 0)),
            pl.BlockSpec((1, D), lambda i: (0, 0)),
        ],
        out_specs=[pl.BlockSpec((tm, D), lambda i: (i, 0)), pl.BlockSpec((tm, D), lambda i: (i, 0))],
        out_shape=[jax.ShapeDtypeStruct((S, D), F32), jax.ShapeDtypeStruct((S, D), BF16)],
        compiler_params=pltpu.CompilerParams(
            dimension_semantics=("arbitrary",), vmem_limit_bytes=VMEM_LIMIT),
        name="out_proj",
    )(o_d, o_l, w_bf, w_bf, x2, gain)


def _ffn_up_kernel(hm_ref, hp_ref, hn_ref, wg_ref, wu_ref, cw_ref, cb_ref, o_ref, lhs_ref, *, tm, n_row_tiles):
    i = pl.program_id(0)
    j = pl.program_id(1)
    halo = BF16_SUBLANES

    @pl.when(j == 0)
    def _():
        lhs_ref[0:halo, :] = jnp.where(i == 0, jnp.zeros_like(hp_ref[...]), hp_ref[...])
        lhs_ref[halo:halo + tm, :] = hm_ref[...]
        lhs_ref[halo + tm:, :] = jnp.where(i == n_row_tiles - 1, jnp.zeros_like(hn_ref[...]), hn_ref[...])

    g = jnp.dot(lhs_ref[...], wg_ref[...], preferred_element_type=F32)
    u = jnp.dot(lhs_ref[halo:halo + tm, :], wu_ref[...], preferred_element_type=F32)
    rows = tm + 2 * halo
    g_prev = pltpu.roll(g, 1, axis=0)
    g_next = pltpu.roll(g, rows - 1, axis=0)
    y = cw_ref[0:1, :] * g_prev + cw_ref[1:2, :] * g + cw_ref[2:3, :] * g_next + cb_ref[...]
    y = y[halo:halo + tm, :]
    act = y * (1.0 / (1.0 + jnp.exp(-y))) * u
    o_ref[...] = act.astype(o_ref.dtype)


def _ffn_up(h2, w_bf, conv_w, conv_b, *, tm=1024, tn=512):
    S, D = h2.shape
    d_ff = conv_w.shape[1]
    assert d_ff % tn == 0
    nj = d_ff // tn
    ni = S // tm
    hb = tm // BF16_SUBLANES
    n_hblk = S // BF16_SUBLANES
    kern = functools.partial(_ffn_up_kernel, tm=tm, n_row_tiles=ni)
    return pl.pallas_call(
        kern,
        grid=(ni, nj),
        in_specs=[
            pl.BlockSpec((tm, D), lambda i, j: (i, 0)),
            pl.BlockSpec((BF16_SUBLANES, D), lambda i, j: (jnp.maximum(i * hb - 1, 0), 0)),
            pl.BlockSpec((BF16_SUBLANES, D), lambda i, j: (jnp.minimum((i + 1) * hb, n_hblk - 1), 0)),
            pl.BlockSpec((D, tn), lambda i, j: (0, j)),
            pl.BlockSpec((D, tn), lambda i, j: (0, nj + j)),
            pl.BlockSpec((3, tn), lambda i, j: (0, j)),
            pl.BlockSpec((1, tn), lambda i, j: (0, j)),
        ],
        out_specs=pl.BlockSpec((tm, tn), lambda i, j: (i, j)),
        out_shape=jax.ShapeDtypeStruct((S, d_ff), BF16),
        scratch_shapes=[pltpu.VMEM((tm + 2 * BF16_SUBLANES, D), BF16)],
        compiler_params=pltpu.CompilerParams(
            dimension_semantics=("arbitrary", "arbitrary"), vmem_limit_bytes=VMEM_LIMIT),
        name="ffn_up",
    )(h2, h2, h2, w_bf, w_bf, conv_w, conv_b)


def _ffn_down_kernel(a_ref, w_ref, x1_ref, g_ref, o_ref, acc_ref, *, n_k):
    k = pl.program_id(1)

    @pl.when(k == 0)
    def _():
        acc_ref[...] = x1_ref[...]

    acc_ref[...] += jnp.dot(a_ref[...], w_ref[...], preferred_element_type=F32)

    @pl.when(k == n_k - 1)
    def _():
        y = acc_ref[...]
        ms = jnp.mean(y * y, axis=-1, keepdims=True)
        o_ref[...] = y * lax.rsqrt(ms + NORM_EPS) * g_ref[...]


def _ffn_down(act, w_bf, x1, gain, *, tm=512, tk=512):
    S, d_ff = act.shape
    D = x1.shape[1]
    n_k = d_ff // tk
    kern = functools.partial(_ffn_down_kernel, n_k=n_k)
    return pl.pallas_call(
        kern,
        grid=(S // tm, n_k),
        in_specs=[
            pl.BlockSpec((tm, tk), lambda i, k: (i, k)),
            pl.BlockSpec((tk, D), lambda i, k: (k, 0)),
            pl.BlockSpec((tm, D), lambda i, k: (i, 0)),
            pl.BlockSpec((1, D), lambda i, k: (0, 0)),
        ],
        out_specs=pl.BlockSpec((tm, D), lambda i, k: (i, 0)),
        out_shape=jax.ShapeDtypeStruct((S, D), F32),
        scratch_shapes=[pltpu.VMEM((tm, D), F32)],
        compiler_params=pltpu.CompilerParams(
            dimension_semantics=("arbitrary", "arbitrary"), vmem_limit_bytes=VMEM_LIMIT),
        name="ffn_down",
    )(act, w_bf, x1, gain)


def kernel(x, norm1_gain, w_in, rel_bias_table, lambda_q1, lambda_k1, lambda_q2, lambda_k2,
           diff_subln_gain, dil_out_gain, w_out, norm2_gain, w_gate_up, conv_w, conv_b, w_down, final_gain):
    B, S, D = x.shape
    assert B == 1 and w_in.shape[0] == 1
    x2 = x.reshape(S, D)
    n_cols = w_in.shape[2]

    qscale = LOG2E / math.sqrt(HEAD_DIM)
    col = np.arange(n_cols)
    dil_q0 = 2 * DIFF_QK_COLS + DIFF_WIDTH
    is_q = (col < DIFF_QK_COLS) | ((col >= dil_q0) & (col < dil_q0 + DIL_WIDTH))
    colscale = jnp.asarray(np.where(is_q, qscale, 1.0).astype(np.float32)).reshape(1, n_cols)
    tab = rel_bias_table.astype(F32) * LOG2E

    proj, qt_all, vt_all = _in_proj(x2, norm1_gain.reshape(1, D), w_in[0].astype(BF16), colscale)

    lam = _lambda(lambda_q1.reshape(1, -1), lambda_k1.reshape(1, -1),
                  lambda_q2.reshape(1, -1), lambda_k2.reshape(1, -1))
    o_d = _diff_attention(tab, proj, qt_all, vt_all, lam, diff_subln_gain.reshape(-1, 1))

    outs, lses = [], []
    for window, dilation in DILATED_PATTERNS:
        o_p, lse_p = _dilated_pattern(tab, proj, window, dilation)
        outs.append(o_p)
        lses.append(lse_p)
    o_l = _combine(outs, lses, dil_out_gain.reshape(1, -1))

    x1, h2 = _out_proj(o_d, o_l, w_out[0].astype(BF16), x2, norm2_gain.reshape(1, D))
    act = _ffn_up(h2, w_gate_up[0].astype(BF16), conv_w[0], conv_b.reshape(1, -1))
    out = _ffn_down(act, w_down[0].astype(BF16), x1, final_gain.reshape(1, D))
    return out.reshape(B, S, D)
```

```python
import functools
import math

import numpy as np
import jax
import jax.numpy as jnp
from jax import lax
from jax.experimental import pallas as pl
from jax.experimental.pallas import tpu as pltpu

F32 = jnp.float32
BF16 = jnp.bfloat16

HEAD_DIM = 128
N_DIFF_HEADS = 4
DIFF_V_DIM = 2 * HEAD_DIM
N_DIL_HEADS = 8
DIFF_QK_COLS = N_DIFF_HEADS * 2 * HEAD_DIM
DIFF_WIDTH = N_DIFF_HEADS * DIFF_V_DIM
DIL_WIDTH = N_DIL_HEADS * HEAD_DIM
DILATED_PATTERNS = ((128, 1), (512, 4), (2048, 16))
N_REL_BUCKETS = 32
REL_MAX_DISTANCE = 1024
NORM_EPS = 1e-6
SUBLN_EPS = 1e-5
NEG_INF = -1e30
LOG2E = math.log2(math.e)
LAM_INIT = 0.8 - 0.6 * math.exp(-0.3 * 0)

LANES = 128
BF16_SUBLANES = 16
DIFF_TILE = 1024
DIFF_QUERY_PANEL = 256
DIL_CHUNK = 1024
VMEM_LIMIT = 56 * 1024 * 1024


def _bucket_breaks():
    nb = N_REL_BUCKETS // 2
    max_exact = nb // 2
    rel = np.arange(-2 * REL_MAX_DISTANCE, 2 * REL_MAX_DISTANCE + 1)
    n = np.abs(rel)
    pos = np.log(np.maximum(n, 1) / max_exact) / math.log(REL_MAX_DISTANCE / max_exact) * (nb - max_exact)
    large = np.minimum(max_exact + np.floor(pos).astype(np.int64), nb - 1)
    bucket = np.where(rel > 0, nb, 0) + np.where(n < max_exact, n, large)
    breaks = [(int(rel[i]), int(bucket[i])) for i in range(1, len(rel)) if bucket[i] != bucket[i - 1]]
    return int(bucket[0]), breaks


FIRST_BUCKET, BUCKET_BREAKS = _bucket_breaks()
LAST_BUCKET = BUCKET_BREAKS[-1][1]
FAR_DIST = max(-BUCKET_BREAKS[0][0] + 1, BUCKET_BREAKS[-1][0])


def _bias_from_rel(rel, tab_ref, col):
    val = jnp.full(rel.shape, tab_ref[FIRST_BUCKET, col], F32)
    for thr, b in BUCKET_BREAKS:
        val = jnp.where(rel >= thr, tab_ref[b, col], val)
    return val


def _inproj_kernel(x_ref, g_ref, w_ref, cs_ref, o_ref, qt_ref, vt_ref, *rest, q_tile, v_tile, dil_tile0,
                   dilations):
    cls_refs = rest[:len(dilations)]
    h_ref, acc_ref = rest[len(dilations):]
    j = pl.program_id(1)
    tm = x_ref.shape[0]

    @pl.when(j == 0)
    def _():
        x = x_ref[...]
        ms = jnp.mean(x * x, axis=-1, keepdims=True)
        h_ref[...] = (x * lax.rsqrt(ms + NORM_EPS) * g_ref[...]).astype(BF16)

    acc = jnp.dot(h_ref[...], w_ref[...], preferred_element_type=F32) * cs_ref[...]
    o_ref[...] = acc.astype(o_ref.dtype)

    @pl.when(j == q_tile)
    def _():
        qt_ref[...] = acc.T.astype(qt_ref.dtype)

    @pl.when(j == v_tile)
    def _():
        vt_ref[...] = acc.T.astype(vt_ref.dtype)

    @pl.when(j >= dil_tile0)
    def _():
        for cb in range(acc_ref.shape[0]):
            sl = slice(cb * LANES, (cb + 1) * LANES)
            acc_ref[cb] = acc[:, sl]
            for cls_ref, dil in zip(cls_refs, dilations):
                for c in range(dil):
                    cls_ref[c, :, sl] = acc_ref[cb, pl.ds(c, tm // dil, stride=dil), :].astype(cls_ref.dtype)


def _in_proj(x2, gain, w_bf, colscale, dilations, *, tm=512, tn=1024):
    S, D = x2.shape
    N = w_bf.shape[1]
    assert DIFF_QK_COLS == tn and DIFF_WIDTH == tn and DIL_WIDTH == tn, "outputs assume one column tile each"
    dil_tile0 = (2 * DIFF_QK_COLS + DIFF_WIDTH) // tn
    n_dil_tiles = N // tn - dil_tile0
    kern = functools.partial(_inproj_kernel, q_tile=0, v_tile=2 * DIFF_QK_COLS // tn, dil_tile0=dil_tile0,
                             dilations=dilations)
    cls_specs = [pl.BlockSpec((d, tm // d, tn), lambda i, j: (0, i, jnp.maximum(j - dil_tile0, 0)))
                 for d in dilations]
    cls_shapes = [jax.ShapeDtypeStruct((d, S // d, n_dil_tiles * tn), BF16) for d in dilations]
    return pl.pallas_call(
        kern,
        grid=(S // tm, N // tn),
        in_specs=[
            pl.BlockSpec((tm, D), lambda i, j: (i, 0)),
            pl.BlockSpec((1, D), lambda i, j: (0, 0)),
            pl.BlockSpec((D, tn), lambda i, j: (0, j)),
            pl.BlockSpec((1, tn), lambda i, j: (0, j)),
        ],
        out_specs=[
            pl.BlockSpec((tm, tn), lambda i, j: (i, j)),
            pl.BlockSpec((tn, tm), lambda i, j: (0, i)),
            pl.BlockSpec((tn, tm), lambda i, j: (0, i)),
        ] + cls_specs,
        out_shape=[
            jax.ShapeDtypeStruct((S, N), BF16),
            jax.ShapeDtypeStruct((DIFF_QK_COLS, S), BF16),
            jax.ShapeDtypeStruct((DIFF_WIDTH, S), BF16),
        ] + cls_shapes,
        scratch_shapes=[pltpu.VMEM((tm, D), BF16), pltpu.VMEM((tn // LANES, tm, LANES), F32)],
        compiler_params=pltpu.CompilerParams(
            dimension_semantics=("arbitrary", "arbitrary"), vmem_limit_bytes=VMEM_LIMIT),
        name="in_proj",
    )(x2, gain, w_bf, colscale)


def _diff_attn_kernel(tab_ref, q1t_ref, q2t_ref, k1_ref, k2_ref, vt_ref, lam_ref, gain_ref, o_ref,
                      bias_ref, m_ref, l_ref, acc_ref, *, T, QP, n_near, n_tiles):
    h = pl.program_id(0)
    qi = pl.program_id(1)

    @pl.when(qi == 0)
    def _():
        rows = 8
        col = lax.broadcasted_iota(jnp.int32, (rows, T), 1)
        row = lax.broadcasted_iota(jnp.int32, (rows, T), 0)
        for di, d in enumerate(range(-n_near, n_near + 1)):
            def fill(r, carry, di=di, d=d):
                r0 = pl.multiple_of(r * rows, rows)
                rel = row - col + (d * T + r0)
                bias_ref[di, pl.ds(r0, rows), :] = _bias_from_rel(rel, tab_ref, h)
                return carry
            lax.fori_loop(0, T // rows, fill, 0)

    c_left = tab_ref[FIRST_BUCKET, h]
    c_right = tab_ref[LAST_BUCKET, h]

    m_ref[...] = jnp.full(m_ref.shape, -jnp.inf, F32)
    l_ref[...] = jnp.zeros(l_ref.shape, F32)
    acc_ref[...] = jnp.zeros(acc_ref.shape, F32)

    def update(mi, qt_ref, k_ref, kt, qp, bias_di, bias_const):
        k0 = pl.multiple_of(kt * T, T)
        qs = slice(qp * QP, (qp + 1) * QP)
        m = m_ref[mi, :, qs]
        s = jnp.dot(k_ref[pl.ds(k0, T), :], qt_ref[:, qs], preferred_element_type=F32)
        if bias_di is not None:
            s = s + bias_ref[bias_di, :, qs]
            m_new = jnp.maximum(m, jnp.max(s, axis=0, keepdims=True))
            shift = m_new
        else:
            m_new = jnp.maximum(m, jnp.max(s, axis=0, keepdims=True) + bias_const)
            shift = m_new - bias_const
        alpha = jnp.exp2(m - m_new)
        p = jnp.exp2(s - shift)
        m_ref[mi, :, qs] = m_new
        l_ref[mi, :, qs] = alpha * l_ref[mi, :, qs] + jnp.sum(p, axis=0, keepdims=True)
        pv = jnp.dot(vt_ref[:, pl.ds(k0, T)], p.astype(BF16), preferred_element_type=F32)
        acc_ref[mi, :, qs] = alpha * acc_ref[mi, :, qs] + pv

    def tile(kt, bias_di, bias_const):
        for qp in range(T // QP):
            update(0, q1t_ref, k1_ref, kt, qp, bias_di, bias_const)
            update(1, q2t_ref, k2_ref, kt, qp, bias_di, bias_const)

    lo = jnp.maximum(qi - n_near, 0)
    hi = jnp.minimum(qi + n_near + 1, n_tiles)

    @pl.loop(0, lo)
    def _(kt):
        tile(kt, None, c_left)

    for di, d in enumerate(range(-n_near, n_near + 1)):
        kt = qi + d

        @pl.when(jnp.logical_and(kt >= 0, kt < n_tiles))
        def _(di=di, kt=kt):
            tile(kt, di, None)

    @pl.loop(hi, n_tiles)
    def _(kt):
        tile(kt, None, c_right)

    lam = lam_ref[0, 0]
    o = acc_ref[0] / l_ref[0] - lam * (acc_ref[1] / l_ref[1])
    ms = jnp.mean(o * o, axis=0, keepdims=True)
    o = o * lax.rsqrt(ms + SUBLN_EPS) * (gain_ref[...] * (1.0 - LAM_INIT))
    o_ref[...] = o.T.astype(o_ref.dtype)


def _lambda_kernel(q1_ref, k1_ref, q2_ref, k2_ref, o_ref):
    a = jnp.sum(q1_ref[...] * k1_ref[...], axis=-1, keepdims=True)
    b = jnp.sum(q2_ref[...] * k2_ref[...], axis=-1, keepdims=True)
    o_ref[...] = jnp.exp(a) - jnp.exp(b) + LAM_INIT


def _lambda(lq1, lk1, lq2, lk2):
    return pl.pallas_call(
        _lambda_kernel, out_shape=jax.ShapeDtypeStruct((1, 1), F32), name="diff_lambda",
    )(lq1, lk1, lq2, lk2)


def _diff_attention(tab_diff, proj, qt_all, vt_all, lam, gain_col):
    S = proj.shape[0]
    T = DIFF_TILE
    n_tiles = S // T
    n_near = -(-(FAR_DIST - 1) // T)
    kern = functools.partial(_diff_attn_kernel, T=T, QP=DIFF_QUERY_PANEL, n_near=n_near, n_tiles=n_tiles)
    k_block0 = DIFF_QK_COLS // HEAD_DIM
    smem = pl.BlockSpec(memory_space=pltpu.SMEM)
    return pl.pallas_call(
        kern,
        grid=(N_DIFF_HEADS, n_tiles),
        in_specs=[
            smem,
            pl.BlockSpec((HEAD_DIM, T), lambda h, i: (2 * h, i)),
            pl.BlockSpec((HEAD_DIM, T), lambda h, i: (2 * h + 1, i)),
            pl.BlockSpec((S, HEAD_DIM), lambda h, i: (0, k_block0 + 2 * h), pipeline_mode=pl.Buffered(1)),
            pl.BlockSpec((S, HEAD_DIM), lambda h, i: (0, k_block0 + 2 * h + 1), pipeline_mode=pl.Buffered(1)),
            pl.BlockSpec((DIFF_V_DIM, S), lambda h, i: (h, 0), pipeline_mode=pl.Buffered(1)),
            smem,
            pl.BlockSpec((DIFF_V_DIM, 1), lambda h, i: (0, 0)),
        ],
        out_specs=pl.BlockSpec((T, DIFF_V_DIM), lambda h, i: (i, h)),
        out_shape=jax.ShapeDtypeStruct((S, DIFF_WIDTH), BF16),
        scratch_shapes=[pltpu.VMEM((2 * n_near + 1, T, T), F32),
                        pltpu.VMEM((2, 1, T), F32), pltpu.VMEM((2, 1, T), F32),
                        pltpu.VMEM((2, DIFF_V_DIM, T), F32)],
        compiler_params=pltpu.CompilerParams(
            dimension_semantics=("arbitrary", "arbitrary"), vmem_limit_bytes=VMEM_LIMIT),
        name="diff_attn",
    )(tab_diff, qt_all, qt_all, proj, proj, vt_all, lam, gain_col)


def _dilated_kernel(tab_ref, q_ref, kp_ref, km_ref, kn_ref, vp_ref, vm_ref, vn_ref, o_ref, lse_ref,
                    bias_ref, kx_ref, vx_ref, *, R, B, half, dilation, n_chunks):
    c = pl.program_id(0)
    n = pl.program_id(1)
    W = B + 2 * half
    nblk = R // B

    @pl.when(jnp.logical_and(c == 0, n == 0))
    def _():
        rows = 8
        col = lax.broadcasted_iota(jnp.int32, (rows, W), 1)
        row = lax.broadcasted_iota(jnp.int32, (rows, W), 0)
        for hh in range(N_DIL_HEADS):
            def fill(r, carry, hh=hh):
                r0 = pl.multiple_of(r * rows, rows)
                off = col - half - (row + r0)
                bias = _bias_from_rel(off * dilation, tab_ref, N_DIFF_HEADS + hh)
                base = jnp.where(jnp.abs(off) <= half, bias, NEG_INF * LOG2E)
                bias_ref[hh, 1, pl.ds(r0, rows), :] = base
                bias_ref[hh, 0, pl.ds(r0, rows), :] = jnp.where(col >= half, base, NEG_INF * LOG2E)
                bias_ref[hh, 2, pl.ds(r0, rows), :] = jnp.where(col < B + half, base, NEG_INF * LOG2E)
                return carry
            lax.fori_loop(0, B // rows, fill, 0)

    kx_ref[0:half, :] = kp_ref[...]
    kx_ref[half:half + R, :] = km_ref[...]
    kx_ref[half + R:, :] = kn_ref[...]
    vx_ref[0:half, :] = vp_ref[...]
    vx_ref[half:half + R, :] = vm_ref[...]
    vx_ref[half + R:, :] = vn_ref[...]

    def head_body(hh, carry):
        c0 = pl.multiple_of(hh * HEAD_DIM, HEAD_DIM)

        for b in range(nblk):
            r0 = b * B
            var = 1
            if b == 0:
                var = jnp.where(n == 0, 0, var)
            if b == nblk - 1:
                var = jnp.where(n == n_chunks - 1, 2, var)
            q = q_ref[pl.ds(r0, B), pl.ds(c0, HEAD_DIM)]
            k = kx_ref[pl.ds(r0, W), pl.ds(c0, HEAD_DIM)]
            v = vx_ref[pl.ds(r0, W), pl.ds(c0, HEAD_DIM)]
            s = lax.dot_general(q, k, (((1,), (1,)), ((), ())), preferred_element_type=F32)
            s = s + bias_ref[hh, var]
            m = jnp.max(s, axis=-1, keepdims=True)
            e = jnp.exp2(s - m)
            den = jnp.sum(e, axis=-1, keepdims=True)
            o = jnp.dot(e.astype(BF16), v, preferred_element_type=F32) / den
            o_ref[pl.ds(r0, B), pl.ds(c0, HEAD_DIM)] = o.astype(o_ref.dtype)
            lse = m + jnp.log2(den)
            lse_ref[pl.ds(r0, B), pl.ds(c0, HEAD_DIM)] = jnp.broadcast_to(lse, (B, HEAD_DIM))
        return carry

    lax.fori_loop(0, N_DIL_HEADS, head_body, 0)


def _dilated_pattern(tab, qkv, col_blk0, window, dilation, *, B=256):
    _, L, _ = qkv.shape
    R = min(DIL_CHUNK, L)
    half = window // (2 * dilation)
    assert L % R == 0 and R % B == 0 and half % BF16_SUBLANES == 0 and R % half == 0
    n_chunks = L // R
    q_blk, k_blk, v_blk = col_blk0, col_blk0 + 1, col_blk0 + 2
    hb = R // half
    n_hblk = L // half

    def main(blk):
        return pl.BlockSpec((None, R, DIL_WIDTH), lambda c, n: (c, n, blk))

    def prev(blk):
        return pl.BlockSpec((None, half, DIL_WIDTH), lambda c, n: (c, jnp.maximum(n * hb - 1, 0), blk))

    def nxt(blk):
        return pl.BlockSpec((None, half, DIL_WIDTH), lambda c, n: (c, jnp.minimum((n + 1) * hb, n_hblk - 1), blk))

    kern = functools.partial(_dilated_kernel, R=R, B=B, half=half, dilation=dilation, n_chunks=n_chunks)
    out_spec = pl.BlockSpec((None, R, DIL_WIDTH), lambda c, n: (c, n, 0))
    return pl.pallas_call(
        kern,
        grid=(dilation, n_chunks),
        in_specs=[pl.BlockSpec(memory_space=pltpu.SMEM),
                  main(q_blk), prev(k_blk), main(k_blk), nxt(k_blk), prev(v_blk), main(v_blk), nxt(v_blk)],
        out_specs=[out_spec, out_spec],
        out_shape=[jax.ShapeDtypeStruct((dilation, L, DIL_WIDTH), BF16),
                   jax.ShapeDtypeStruct((dilation, L, DIL_WIDTH), F32)],
        scratch_shapes=[pltpu.VMEM((N_DIL_HEADS, 3, B, B + 2 * half), F32),
                        pltpu.VMEM((R + 2 * half, DIL_WIDTH), BF16),
                        pltpu.VMEM((R + 2 * half, DIL_WIDTH), BF16)],
        compiler_params=pltpu.CompilerParams(
            dimension_semantics=("arbitrary", "arbitrary"), vmem_limit_bytes=VMEM_LIMIT),
        name=f"dilated_d{dilation}",
    )(tab, qkv, qkv, qkv, qkv, qkv, qkv, qkv)


def _combine_kernel(*refs, dilations):
    n = len(dilations)
    o_refs, l_refs = refs[:n], refs[n:2 * n]
    g_ref, out_ref = refs[2 * n], refs[2 * n + 1]
    scratch = refs[2 * n + 2:]
    tm = out_ref.shape[0]

    for hh in range(N_DIL_HEADS):
        sl = slice(hh * HEAD_DIM, (hh + 1) * HEAD_DIM)
        outs, lses = [], []
        si = 0
        for o_ref, l_ref, dil in zip(o_refs, l_refs, dilations):
            if dil == 1:
                outs.append(o_ref[0, :, sl].astype(F32))
                lses.append(l_ref[0, :, sl])
                continue
            os_ref, ls_ref = scratch[si], scratch[si + 1]
            si += 2
            for c in range(dil):
                os_ref[hh, pl.ds(c, tm // dil, stride=dil), :] = o_ref[c, :, sl].astype(F32)
                ls_ref[hh, pl.ds(c, tm // dil, stride=dil), :] = l_ref[c, :, sl]
            outs.append(os_ref[hh])
            lses.append(ls_ref[hh])

        m = functools.reduce(jnp.maximum, lses)
        ws = [jnp.exp2(l - m) for l in lses]
        tot = functools.reduce(lambda a, b: a + b, ws)
        oh = functools.reduce(lambda a, b: a + b, [(w / tot) * op for w, op in zip(ws, outs)])
        ms = jnp.mean(oh * oh, axis=-1, keepdims=True)
        out_ref[:, sl] = (oh * lax.rsqrt(ms + NORM_EPS) * g_ref[:, sl]).astype(out_ref.dtype)


def _combine(outs, lses, gain, dilations, *, tm=512):
    S = outs[0].shape[0] * outs[0].shape[1]
    specs = [pl.BlockSpec((d, tm // d, DIL_WIDTH), lambda i: (0, i, 0)) for d in dilations]
    n_scr = sum(1 for d in dilations if d != 1)
    return pl.pallas_call(
        functools.partial(_combine_kernel, dilations=dilations),
        grid=(S // tm,),
        in_specs=specs + specs + [pl.BlockSpec((1, DIL_WIDTH), lambda i: (0, 0))],
        scratch_shapes=[pltpu.VMEM((N_DIL_HEADS, tm, HEAD_DIM), F32)] * (2 * n_scr),
        out_specs=pl.BlockSpec((tm, DIL_WIDTH), lambda i: (i, 0)),
        out_shape=jax.ShapeDtypeStruct((S, DIL_WIDTH), BF16),
        compiler_params=pltpu.CompilerParams(
            dimension_semantics=("arbitrary",), vmem_limit_bytes=VMEM_LIMIT),
        name="dilated_combine",
    )(*outs, *lses, gain)


def _outproj_kernel(od_ref, ol_ref, wd_ref, wl_ref, x_ref, g_ref, x1_ref, h2_ref):
    acc = jnp.dot(od_ref[...], wd_ref[...], preferred_element_type=F32)
    acc = acc + jnp.dot(ol_ref[...], wl_ref[...], preferred_element_type=F32)
    x1 = x_ref[...] + acc
    x1_ref[...] = x1
    ms = jnp.mean(x1 * x1, axis=-1, keepdims=True)
    h2_ref[...] = (x1 * lax.rsqrt(ms + NORM_EPS) * g_ref[...]).astype(h2_ref.dtype)


def _out_proj(o_d, o_l, w_bf, x2, gain, *, tm=512):
    S, D = x2.shape
    return pl.pallas_call(
        _outproj_kernel,
        grid=(S // tm,),
        in_specs=[
            pl.BlockSpec((tm, DIFF_WIDTH), lambda i: (i, 0)),
            pl.BlockSpec((tm, DIL_WIDTH), lambda i: (i, 0)),
            pl.BlockSpec((DIFF_WIDTH, D), lambda i: (0, 0)),
            pl.BlockSpec((DIL_WIDTH, D), lambda i: (1, 0)),
            pl.BlockSpec((tm, D), lambda i: (i, 0)),
            pl.BlockSpec((1, D), lambda i: (0, 0)),
        ],
        out_specs=[pl.BlockSpec((tm, D), lambda i: (i, 0)), pl.BlockSpec((tm, D), lambda i: (i, 0))],
        out_shape=[jax.ShapeDtypeStruct((S, D), F32), jax.ShapeDtypeStruct((S, D), BF16)],
        compiler_params=pltpu.CompilerParams(
            dimension_semantics=("arbitrary",), vmem_limit_bytes=VMEM_LIMIT),
        name="out_proj",
    )(o_d, o_l, w_bf, w_bf, x2, gain)


def _ffn_up_kernel(hm_ref, hp_ref, hn_ref, wg_ref, wu_ref, cw_ref, cb_ref, o_ref, lhs_ref, *, tm, n_row_tiles):
    i = pl.program_id(0)
    j = pl.program_id(1)
    halo = BF16_SUBLANES

    @pl.when(j == 0)
    def _():
        lhs_ref[0:halo, :] = jnp.where(i == 0, jnp.zeros_like(hp_ref[...]), hp_ref[...])
        lhs_ref[halo:halo + tm, :] = hm_ref[...]
        lhs_ref[halo + tm:, :] = jnp.where(i == n_row_tiles - 1, jnp.zeros_like(hn_ref[...]), hn_ref[...])

    g = jnp.dot(lhs_ref[...], wg_ref[...], preferred_element_type=F32)
    u = jnp.dot(lhs_ref[halo:halo + tm, :], wu_ref[...], preferred_element_type=F32)
    rows = tm + 2 * halo
    g_prev = pltpu.roll(g, 1, axis=0)
    g_next = pltpu.roll(g, rows - 1, axis=0)
    y = cw_ref[0:1, :] * g_prev + cw_ref[1:2, :] * g + cw_ref[2:3, :] * g_next + cb_ref[...]
    y = y[halo:halo + tm, :]
    act = y * (1.0 / (1.0 + jnp.exp(-y))) * u
    o_ref[...] = act.astype(o_ref.dtype)


def _ffn_up(h2, w_bf, conv_w, conv_b, *, tm=1024, tn=512):
    S, D = h2.shape
    d_ff = conv_w.shape[1]
    assert d_ff % tn == 0
    nj = d_ff // tn
    ni = S // tm
    hb = tm // BF16_SUBLANES
    n_hblk = S // BF16_SUBLANES
    kern = functools.partial(_ffn_up_kernel, tm=tm, n_row_tiles=ni)
    return pl.pallas_call(
        kern,
        grid=(ni, nj),
        in_specs=[
            pl.BlockSpec((tm, D), lambda i, j: (i, 0)),
            pl.BlockSpec((BF16_SUBLANES, D), lambda i, j: (jnp.maximum(i * hb - 1, 0), 0)),
            pl.BlockSpec((BF16_SUBLANES, D), lambda i, j: (jnp.minimum((i + 1) * hb, n_hblk - 1), 0)),
            pl.BlockSpec((D, tn), lambda i, j: (0, j)),
            pl.BlockSpec((D, tn), lambda i, j: (0, nj + j)),
            pl.BlockSpec((3, tn), lambda i, j: (0, j)),
            pl.BlockSpec((1, tn), lambda i, j: (0, j)),
        ],
        out_specs=pl.BlockSpec((tm, tn), lambda i, j: (i, j)),
        out_shape=jax.ShapeDtypeStruct((S, d_ff), BF16),
        scratch_shapes=[pltpu.VMEM((tm + 2 * BF16_SUBLANES, D), BF16)],
        compiler_params=pltpu.CompilerParams(
            dimension_semantics=("arbitrary", "arbitrary"), vmem_limit_bytes=VMEM_LIMIT),
        name="ffn_up",
    )(h2, h2, h2, w_bf, w_bf, conv_w, conv_b)


def _ffn_down_kernel(a_ref, w_ref, x1_ref, g_ref, o_ref, acc_ref, *, n_k):
    k = pl.program_id(1)

    @pl.when(k == 0)
    def _():
        acc_ref[...] = x1_ref[...]

    acc_ref[...] += jnp.dot(a_ref[...], w_ref[...], preferred_element_type=F32)

    @pl.when(k == n_k - 1)
    def _():
        y = acc_ref[...]
        ms = jnp.mean(y * y, axis=-1, keepdims=True)
        o_ref[...] = y * lax.rsqrt(ms + NORM_EPS) * g_ref[...]


def _ffn_down(act, w_bf, x1, gain, *, tm=512, tk=512):
    S, d_ff = act.shape
    D = x1.shape[1]
    n_k = d_ff // tk
    kern = functools.partial(_ffn_down_kernel, n_k=n_k)
    return pl.pallas_call(
        kern,
        grid=(S // tm, n_k),
        in_specs=[
            pl.BlockSpec((tm, tk), lambda i, k: (i, k)),
            pl.BlockSpec((tk, D), lambda i, k: (k, 0)),
            pl.BlockSpec((tm, D), lambda i, k: (i, 0)),
            pl.BlockSpec((1, D), lambda i, k: (0, 0)),
        ],
        out_specs=pl.BlockSpec((tm, D), lambda i, k: (i, 0)),
        out_shape=jax.ShapeDtypeStruct((S, D), F32),
        scratch_shapes=[pltpu.VMEM((tm, D), F32)],
        compiler_params=pltpu.CompilerParams(
            dimension_semantics=("arbitrary", "arbitrary"), vmem_limit_bytes=VMEM_LIMIT),
        name="ffn_down",
    )(act, w_bf, x1, gain)


def kernel(x, norm1_gain, w_in, rel_bias_table, lambda_q1, lambda_k1, lambda_q2, lambda_k2,
           diff_subln_gain, dil_out_gain, w_out, norm2_gain, w_gate_up, conv_w, conv_b, w_down, final_gain):
    B, S, D = x.shape
    assert B == 1 and w_in.shape[0] == 1
    x2 = x.reshape(S, D)
    n_cols = w_in.shape[2]

    qscale = LOG2E / math.sqrt(HEAD_DIM)
    col = np.arange(n_cols)
    dil_q0 = 2 * DIFF_QK_COLS + DIFF_WIDTH
    is_q = (col < DIFF_QK_COLS) | ((col >= dil_q0) & (col < dil_q0 + DIL_WIDTH))
    colscale = jnp.asarray(np.where(is_q, qscale, 1.0).astype(np.float32)).reshape(1, n_cols)
    tab = rel_bias_table.astype(F32) * LOG2E

    regroup = tuple(d for _, d in DILATED_PATTERNS if d != 1)
    proj, qt_all, vt_all, *cls = _in_proj(x2, norm1_gain.reshape(1, D), w_in[0].astype(BF16), colscale, regroup)
    cls_by_dil = dict(zip(regroup, cls))

    lam = _lambda(lambda_q1.reshape(1, -1), lambda_k1.reshape(1, -1),
                  lambda_q2.reshape(1, -1), lambda_k2.reshape(1, -1))
    o_d = _diff_attention(tab, proj, qt_all, vt_all, lam, diff_subln_gain.reshape(-1, 1))

    outs, lses = [], []
    dil_blk0 = (2 * DIFF_QK_COLS + DIFF_WIDTH) // DIL_WIDTH
    for window, dilation in DILATED_PATTERNS:
        if dilation == 1:
            o_p, lse_p = _dilated_pattern(tab, proj.reshape(1, S, n_cols), dil_blk0, window, dilation)
        else:
            o_p, lse_p = _dilated_pattern(tab, cls_by_dil[dilation], 0, window, dilation)
        outs.append(o_p)
        lses.append(lse_p)
    o_l = _combine(outs, lses, dil_out_gain.reshape(1, -1), tuple(d for _, d in DILATED_PATTERNS))

    x1, h2 = _out_proj(o_d, o_l, w_out[0].astype(BF16), x2, norm2_gain.reshape(1, D))
    act = _ffn_up(h2, w_gate_up[0].astype(BF16), conv_w[0], conv_b.reshape(1, -1))
    out = _ffn_down(act, w_down[0].astype(BF16), x1, final_gain.reshape(1, D))
    return out.reshape(B, S, D)
```

```python
import functools
import math

import numpy as np
import jax
import jax.numpy as jnp
from jax import lax
from jax.experimental import pallas as pl
from jax.experimental.pallas import tpu as pltpu

F32 = jnp.float32
BF16 = jnp.bfloat16

HEAD_DIM = 128
N_DIFF_HEADS = 4
DIFF_V_DIM = 2 * HEAD_DIM
N_DIL_HEADS = 8
DIFF_QK_COLS = N_DIFF_HEADS * 2 * HEAD_DIM
DIFF_WIDTH = N_DIFF_HEADS * DIFF_V_DIM
DIL_WIDTH = N_DIL_HEADS * HEAD_DIM
DILATED_PATTERNS = ((128, 1), (512, 4), (2048, 16))
N_REL_BUCKETS = 32
REL_MAX_DISTANCE = 1024
NORM_EPS = 1e-6
SUBLN_EPS = 1e-5
NEG_INF = -1e30
LOG2E = math.log2(math.e)
LAM_INIT = 0.8 - 0.6 * math.exp(-0.3 * 0)

LANES = 128
BF16_SUBLANES = 16
DIFF_TILE = 1024
DIFF_QUERY_PANEL = 256
DIL_CHUNK = 1024
VMEM_LIMIT = 56 * 1024 * 1024


def _bucket_breaks():
    nb = N_REL_BUCKETS // 2
    max_exact = nb // 2
    rel = np.arange(-2 * REL_MAX_DISTANCE, 2 * REL_MAX_DISTANCE + 1)
    n = np.abs(rel)
    pos = np.log(np.maximum(n, 1) / max_exact) / math.log(REL_MAX_DISTANCE / max_exact) * (nb - max_exact)
    large = np.minimum(max_exact + np.floor(pos).astype(np.int64), nb - 1)
    bucket = np.where(rel > 0, nb, 0) + np.where(n < max_exact, n, large)
    breaks = [(int(rel[i]), int(bucket[i])) for i in range(1, len(rel)) if bucket[i] != bucket[i - 1]]
    return int(bucket[0]), breaks


FIRST_BUCKET, BUCKET_BREAKS = _bucket_breaks()
LAST_BUCKET = BUCKET_BREAKS[-1][1]
FAR_DIST = max(-BUCKET_BREAKS[0][0] + 1, BUCKET_BREAKS[-1][0])


def _bias_from_rel(rel, tab_ref, col):
    val = jnp.full(rel.shape, tab_ref[FIRST_BUCKET, col], F32)
    for thr, b in BUCKET_BREAKS:
        val = jnp.where(rel >= thr, tab_ref[b, col], val)
    return val


def _inproj_kernel(x_ref, g_ref, w_ref, cs_ref, o_ref, qt_ref, vt_ref, *rest, q_tile, v_tile, dil_tile0,
                   dilations):
    cls_refs = rest[:len(dilations)]
    h_ref, acc_ref = rest[len(dilations):]
    j = pl.program_id(1)
    tm = x_ref.shape[0]

    @pl.when(j == 0)
    def _():
        x = x_ref[...]
        ms = jnp.mean(x * x, axis=-1, keepdims=True)
        h_ref[...] = (x * lax.rsqrt(ms + NORM_EPS) * g_ref[...]).astype(BF16)

    acc = jnp.dot(h_ref[...], w_ref[...], preferred_element_type=F32) * cs_ref[...]
    o_ref[...] = acc.astype(o_ref.dtype)

    @pl.when(j == q_tile)
    def _():
        qt_ref[...] = acc.T.astype(qt_ref.dtype)

    @pl.when(j == v_tile)
    def _():
        vt_ref[...] = acc.T.astype(vt_ref.dtype)

    @pl.when(j >= dil_tile0)
    def _():
        for cb in range(acc_ref.shape[0]):
            sl = slice(cb * LANES, (cb + 1) * LANES)
            acc_ref[cb] = acc[:, sl]
            for cls_ref, dil in zip(cls_refs, dilations):
                for c in range(dil):
                    cls_ref[c, :, sl] = acc_ref[cb, pl.ds(c, tm // dil, stride=dil), :].astype(cls_ref.dtype)


def _in_proj(x2, gain, w_bf, colscale, dilations, *, tm=512, tn=1024):
    S, D = x2.shape
    N = w_bf.shape[1]
    assert DIFF_QK_COLS == tn and DIFF_WIDTH == tn and DIL_WIDTH == tn, "outputs assume one column tile each"
    dil_tile0 = (2 * DIFF_QK_COLS + DIFF_WIDTH) // tn
    n_dil_tiles = N // tn - dil_tile0
    kern = functools.partial(_inproj_kernel, q_tile=0, v_tile=2 * DIFF_QK_COLS // tn, dil_tile0=dil_tile0,
                             dilations=dilations)
    cls_specs = [pl.BlockSpec((d, tm // d, tn), lambda i, j: (0, i, jnp.maximum(j - dil_tile0, 0)))
                 for d in dilations]
    cls_shapes = [jax.ShapeDtypeStruct((d, S // d, n_dil_tiles * tn), BF16) for d in dilations]
    return pl.pallas_call(
        kern,
        grid=(S // tm, N // tn),
        in_specs=[
            pl.BlockSpec((tm, D), lambda i, j: (i, 0)),
            pl.BlockSpec((1, D), lambda i, j: (0, 0)),
            pl.BlockSpec((D, tn), lambda i, j: (0, j)),
            pl.BlockSpec((1, tn), lambda i, j: (0, j)),
        ],
        out_specs=[
            pl.BlockSpec((tm, tn), lambda i, j: (i, j)),
            pl.BlockSpec((tn, tm), lambda i, j: (0, i)),
            pl.BlockSpec((tn, tm), lambda i, j: (0, i)),
        ] + cls_specs,
        out_shape=[
            jax.ShapeDtypeStruct((S, N), BF16),
            jax.ShapeDtypeStruct((DIFF_QK_COLS, S), BF16),
            jax.ShapeDtypeStruct((DIFF_WIDTH, S), BF16),
        ] + cls_shapes,
        scratch_shapes=[pltpu.VMEM((tm, D), BF16), pltpu.VMEM((tn // LANES, tm, LANES), F32)],
        compiler_params=pltpu.CompilerParams(
            dimension_semantics=("arbitrary", "arbitrary"), vmem_limit_bytes=VMEM_LIMIT),
        name="in_proj",
    )(x2, gain, w_bf, colscale)


def _diff_attn_kernel(tab_ref, q1t_ref, q2t_ref, k1_ref, k2_ref, vt_ref, lam_ref, gain_ref, o_ref,
                      bias_ref, m_ref, l_ref, acc_ref, p_ref, alpha_ref, s_ref, prev_ref, cprev_ref,
                      *, T, QP, n_near, n_tiles):
    h = pl.program_id(0)
    qi = pl.program_id(1)

    @pl.when(qi == 0)
    def _():
        rows = 8
        col = lax.broadcasted_iota(jnp.int32, (rows, T), 1)
        row = lax.broadcasted_iota(jnp.int32, (rows, T), 0)
        for di, d in enumerate(range(-n_near, n_near + 1)):
            def fill(r, carry, di=di, d=d):
                r0 = pl.multiple_of(r * rows, rows)
                rel = row - col + (d * T + r0)
                bias_ref[di, pl.ds(r0, rows), :] = _bias_from_rel(rel, tab_ref, h)
                return carry
            lax.fori_loop(0, T // rows, fill, 0)

    c_left = tab_ref[FIRST_BUCKET, h]
    c_right = tab_ref[LAST_BUCKET, h]

    m_ref[...] = jnp.full(m_ref.shape, -jnp.inf, F32)
    l_ref[...] = jnp.zeros(l_ref.shape, F32)
    acc_ref[...] = jnp.zeros(acc_ref.shape, F32)

    n_chains = 2 * (T // QP)
    qts = (q1t_ref, q2t_ref)
    ks = (k1_ref, k2_ref)

    def chain_of(i):
        qp = i // 2
        return i % 2, slice(qp * QP, (qp + 1) * QP)

    s_ref[(n_chains - 1) % 2] = jnp.full(s_ref.shape[1:], -jnp.inf, F32)
    for i in (n_chains - 2, n_chains - 1):
        mi, qs = chain_of(i)
        p_ref[i] = jnp.zeros((T, QP), BF16)
    alpha_ref[...] = jnp.ones(alpha_ref.shape, F32)
    prev_ref[0] = 0
    cprev_ref[0] = 0.0

    def score_stage(i, k0, bias_di):
        mi, qs = chain_of(i)
        s = jnp.dot(ks[mi][pl.ds(k0, T), :], qts[mi][:, qs], preferred_element_type=F32)
        if bias_di is not None:
            s = s + bias_ref[bias_di, :, qs]
        s_ref[i % 2] = s

    def softmax_stage(i, c):
        mi, qs = chain_of(i)
        s = s_ref[i % 2]
        m = m_ref[mi, :, qs]
        m_new = jnp.maximum(m, jnp.max(s, axis=0, keepdims=True) + c)
        empty = m_new == -jnp.inf
        alpha = jnp.where(empty, 1.0, jnp.exp2(m - m_new))
        shift = jnp.where(empty, 0.0, m_new - c)
        p = jnp.exp2(s - shift)
        m_ref[mi, :, qs] = m_new
        l_ref[mi, :, qs] = alpha * l_ref[mi, :, qs] + jnp.sum(p, axis=0, keepdims=True)
        alpha_ref[mi, :, qs] = alpha
        p_ref[i] = p.astype(BF16)

    def value_stage(i, k0):
        mi, qs = chain_of(i)
        pv = jnp.dot(vt_ref[:, pl.ds(k0, T)], p_ref[i], preferred_element_type=F32)
        acc_ref[mi, :, qs] = alpha_ref[mi, :, qs] * acc_ref[mi, :, qs] + pv

    def tile(kt, bias_di, bias_const):
        k0 = pl.multiple_of(kt * T, T)
        pk0 = pl.multiple_of(prev_ref[0] * T, T)
        c_tile = 0.0 if bias_const is None else bias_const
        for i in range(n_chains):
            value_stage((i - 2) % n_chains, pk0 if i < 2 else k0)
            softmax_stage((i - 1) % n_chains, cprev_ref[0] if i < 1 else c_tile)
            score_stage(i, k0, bias_di)
        prev_ref[0] = kt
        cprev_ref[0] = c_tile

    lo = jnp.maximum(qi - n_near, 0)
    hi = jnp.minimum(qi + n_near + 1, n_tiles)

    @pl.loop(0, lo)
    def _(kt):
        tile(kt, None, c_left)

    for di, d in enumerate(range(-n_near, n_near + 1)):
        kt = qi + d

        @pl.when(jnp.logical_and(kt >= 0, kt < n_tiles))
        def _(di=di, kt=kt):
            tile(kt, di, None)

    @pl.loop(hi, n_tiles)
    def _(kt):
        tile(kt, None, c_right)

    pk0 = pl.multiple_of(prev_ref[0] * T, T)
    value_stage(n_chains - 2, pk0)
    softmax_stage(n_chains - 1, cprev_ref[0])
    value_stage(n_chains - 1, pk0)

    lam = lam_ref[0, 0]
    o = acc_ref[0] / l_ref[0] - lam * (acc_ref[1] / l_ref[1])
    ms = jnp.mean(o * o, axis=0, keepdims=True)
    o = o * lax.rsqrt(ms + SUBLN_EPS) * (gain_ref[...] * (1.0 - LAM_INIT))
    o_ref[...] = o.T.astype(o_ref.dtype)


def _lambda_kernel(q1_ref, k1_ref, q2_ref, k2_ref, o_ref):
    a = jnp.sum(q1_ref[...] * k1_ref[...], axis=-1, keepdims=True)
    b = jnp.sum(q2_ref[...] * k2_ref[...], axis=-1, keepdims=True)
    o_ref[...] = jnp.exp(a) - jnp.exp(b) + LAM_INIT


def _lambda(lq1, lk1, lq2, lk2):
    return pl.pallas_call(
        _lambda_kernel, out_shape=jax.ShapeDtypeStruct((1, 1), F32), name="diff_lambda",
    )(lq1, lk1, lq2, lk2)


def _diff_attention(tab_diff, proj, qt_all, vt_all, lam, gain_col):
    S = proj.shape[0]
    T = DIFF_TILE
    n_tiles = S // T
    n_near = -(-(FAR_DIST - 1) // T)
    kern = functools.partial(_diff_attn_kernel, T=T, QP=DIFF_QUERY_PANEL, n_near=n_near, n_tiles=n_tiles)
    k_block0 = DIFF_QK_COLS // HEAD_DIM
    smem = pl.BlockSpec(memory_space=pltpu.SMEM)
    return pl.pallas_call(
        kern,
        grid=(N_DIFF_HEADS, n_tiles),
        in_specs=[
            smem,
            pl.BlockSpec((HEAD_DIM, T), lambda h, i: (2 * h, i)),
            pl.BlockSpec((HEAD_DIM, T), lambda h, i: (2 * h + 1, i)),
            pl.BlockSpec((S, HEAD_DIM), lambda h, i: (0, k_block0 + 2 * h), pipeline_mode=pl.Buffered(1)),
            pl.BlockSpec((S, HEAD_DIM), lambda h, i: (0, k_block0 + 2 * h + 1), pipeline_mode=pl.Buffered(1)),
            pl.BlockSpec((DIFF_V_DIM, S), lambda h, i: (h, 0), pipeline_mode=pl.Buffered(1)),
            smem,
            pl.BlockSpec((DIFF_V_DIM, 1), lambda h, i: (0, 0)),
        ],
        out_specs=pl.BlockSpec((T, DIFF_V_DIM), lambda h, i: (i, h)),
        out_shape=jax.ShapeDtypeStruct((S, DIFF_WIDTH), BF16),
        scratch_shapes=[pltpu.VMEM((2 * n_near + 1, T, T), F32),
                        pltpu.VMEM((2, 1, T), F32), pltpu.VMEM((2, 1, T), F32),
                        pltpu.VMEM((2, DIFF_V_DIM, T), F32),
                        pltpu.VMEM((2 * (T // DIFF_QUERY_PANEL), T, DIFF_QUERY_PANEL), BF16),
                        pltpu.VMEM((2, 1, T), F32),
                        pltpu.VMEM((2, T, DIFF_QUERY_PANEL), F32),
                        pltpu.SMEM((1,), jnp.int32), pltpu.SMEM((1,), F32)],
        compiler_params=pltpu.CompilerParams(
            dimension_semantics=("arbitrary", "arbitrary"), vmem_limit_bytes=VMEM_LIMIT),
        name="diff_attn",
    )(tab_diff, qt_all, qt_all, proj, proj, vt_all, lam, gain_col)


def _dilated_kernel(tab_ref, q_ref, kp_ref, km_ref, kn_ref, vp_ref, vm_ref, vn_ref, o_ref, lse_ref,
                    bias_ref, kx_ref, vx_ref, *, R, B, half, dilation, n_chunks):
    c = pl.program_id(0)
    n = pl.program_id(1)
    W = B + 2 * half
    nblk = R // B

    @pl.when(jnp.logical_and(c == 0, n == 0))
    def _():
        rows = 8
        col = lax.broadcasted_iota(jnp.int32, (rows, W), 1)
        row = lax.broadcasted_iota(jnp.int32, (rows, W), 0)
        for hh in range(N_DIL_HEADS):
            def fill(r, carry, hh=hh):
                r0 = pl.multiple_of(r * rows, rows)
                off = col - half - (row + r0)
                bias = _bias_from_rel(off * dilation, tab_ref, N_DIFF_HEADS + hh)
                base = jnp.where(jnp.abs(off) <= half, bias, NEG_INF * LOG2E)
                bias_ref[hh, 1, pl.ds(r0, rows), :] = base
                bias_ref[hh, 0, pl.ds(r0, rows), :] = jnp.where(col >= half, base, NEG_INF * LOG2E)
                bias_ref[hh, 2, pl.ds(r0, rows), :] = jnp.where(col < B + half, base, NEG_INF * LOG2E)
                return carry
            lax.fori_loop(0, B // rows, fill, 0)

    kx_ref[0:half, :] = kp_ref[...]
    kx_ref[half:half + R, :] = km_ref[...]
    kx_ref[half + R:, :] = kn_ref[...]
    vx_ref[0:half, :] = vp_ref[...]
    vx_ref[half:half + R, :] = vm_ref[...]
    vx_ref[half + R:, :] = vn_ref[...]

    def head_body(hh, carry):
        c0 = pl.multiple_of(hh * HEAD_DIM, HEAD_DIM)

        for b in range(nblk):
            r0 = b * B
            var = 1
            if b == 0:
                var = jnp.where(n == 0, 0, var)
            if b == nblk - 1:
                var = jnp.where(n == n_chunks - 1, 2, var)
            q = q_ref[pl.ds(r0, B), pl.ds(c0, HEAD_DIM)]
            k = kx_ref[pl.ds(r0, W), pl.ds(c0, HEAD_DIM)]
            v = vx_ref[pl.ds(r0, W), pl.ds(c0, HEAD_DIM)]
            s = lax.dot_general(q, k, (((1,), (1,)), ((), ())), preferred_element_type=F32)
            s = s + bias_ref[hh, var]
            m = jnp.max(s, axis=-1, keepdims=True)
            e = jnp.exp2(s - m)
            den = jnp.sum(e, axis=-1, keepdims=True)
            o = jnp.dot(e.astype(BF16), v, preferred_element_type=F32) / den
            o_ref[pl.ds(r0, B), pl.ds(c0, HEAD_DIM)] = o.astype(o_ref.dtype)
            lse = m + jnp.log2(den)
            lse_ref[pl.ds(r0, B), pl.ds(c0, HEAD_DIM)] = jnp.broadcast_to(lse, (B, HEAD_DIM))
        return carry

    lax.fori_loop(0, N_DIL_HEADS, head_body, 0)


def _dilated_pattern(tab, qkv, col_blk0, window, dilation, *, B=256):
    _, L, _ = qkv.shape
    R = min(DIL_CHUNK, L)
    half = window // (2 * dilation)
    assert L % R == 0 and R % B == 0 and half % BF16_SUBLANES == 0 and R % half == 0
    n_chunks = L // R
    q_blk, k_blk, v_blk = col_blk0, col_blk0 + 1, col_blk0 + 2
    hb = R // half
    n_hblk = L // half

    def main(blk):
        return pl.BlockSpec((None, R, DIL_WIDTH), lambda c, n: (c, n, blk))

    def prev(blk):
        return pl.BlockSpec((None, half, DIL_WIDTH), lambda c, n: (c, jnp.maximum(n * hb - 1, 0), blk))

    def nxt(blk):
        return pl.BlockSpec((None, half, DIL_WIDTH), lambda c, n: (c, jnp.minimum((n + 1) * hb, n_hblk - 1), blk))

    kern = functools.partial(_dilated_kernel, R=R, B=B, half=half, dilation=dilation, n_chunks=n_chunks)
    out_spec = pl.BlockSpec((None, R, DIL_WIDTH), lambda c, n: (c, n, 0))
    return pl.pallas_call(
        kern,
        grid=(dilation, n_chunks),
        in_specs=[pl.BlockSpec(memory_space=pltpu.SMEM),
                  main(q_blk), prev(k_blk), main(k_blk), nxt(k_blk), prev(v_blk), main(v_blk), nxt(v_blk)],
        out_specs=[out_spec, out_spec],
        out_shape=[jax.ShapeDtypeStruct((dilation, L, DIL_WIDTH), BF16),
                   jax.ShapeDtypeStruct((dilation, L, DIL_WIDTH), F32)],
        scratch_shapes=[pltpu.VMEM((N_DIL_HEADS, 3, B, B + 2 * half), F32),
                        pltpu.VMEM((R + 2 * half, DIL_WIDTH), BF16),
                        pltpu.VMEM((R + 2 * half, DIL_WIDTH), BF16)],
        compiler_params=pltpu.CompilerParams(
            dimension_semantics=("arbitrary", "arbitrary"), vmem_limit_bytes=VMEM_LIMIT),
        name=f"dilated_d{dilation}",
    )(tab, qkv, qkv, qkv, qkv, qkv, qkv, qkv)


def _combine_kernel(*refs, dilations):
    n = len(dilations)
    o_refs, l_refs = refs[:n], refs[n:2 * n]
    g_ref, out_ref = refs[2 * n], refs[2 * n + 1]
    scratch = refs[2 * n + 2:]
    tm = out_ref.shape[0]

    for hh in range(N_DIL_HEADS):
        sl = slice(hh * HEAD_DIM, (hh + 1) * HEAD_DIM)
        outs, lses = [], []
        si = 0
        for o_ref, l_ref, dil in zip(o_refs, l_refs, dilations):
            if dil == 1:
                outs.append(o_ref[0, :, sl].astype(F32))
                lses.append(l_ref[0, :, sl])
                continue
            os_ref, ls_ref = scratch[si], scratch[si + 1]
            si += 2
            for c in range(dil):
                os_ref[hh, pl.ds(c, tm // dil, stride=dil), :] = o_ref[c, :, sl].astype(F32)
                ls_ref[hh, pl.ds(c, tm // dil, stride=dil), :] = l_ref[c, :, sl]
            outs.append(os_ref[hh])
            lses.append(ls_ref[hh])

        m = functools.reduce(jnp.maximum, lses)
        ws = [jnp.exp2(l - m) for l in lses]
        tot = functools.reduce(lambda a, b: a + b, ws)
        oh = functools.reduce(lambda a, b: a + b, [(w / tot) * op for w, op in zip(ws, outs)])
        ms = jnp.mean(oh * oh, axis=-1, keepdims=True)
        out_ref[:, sl] = (oh * lax.rsqrt(ms + NORM_EPS) * g_ref[:, sl]).astype(out_ref.dtype)


def _combine(outs, lses, gain, dilations, *, tm=512):
    S = outs[0].shape[0] * outs[0].shape[1]
    specs = [pl.BlockSpec((d, tm // d, DIL_WIDTH), lambda i: (0, i, 0)) for d in dilations]
    n_scr = sum(1 for d in dilations if d != 1)
    return pl.pallas_call(
        functools.partial(_combine_kernel, dilations=dilations),
        grid=(S // tm,),
        in_specs=specs + specs + [pl.BlockSpec((1, DIL_WIDTH), lambda i: (0, 0))],
        scratch_shapes=[pltpu.VMEM((N_DIL_HEADS, tm, HEAD_DIM), F32)] * (2 * n_scr),
        out_specs=pl.BlockSpec((tm, DIL_WIDTH), lambda i: (i, 0)),
        out_shape=jax.ShapeDtypeStruct((S, DIL_WIDTH), BF16),
        compiler_params=pltpu.CompilerParams(
            dimension_semantics=("arbitrary",), vmem_limit_bytes=VMEM_LIMIT),
        name="dilated_combine",
    )(*outs, *lses, gain)


def _outproj_kernel(od_ref, ol_ref, wd_ref, wl_ref, x_ref, g_ref, x1_ref, h2_ref):
    acc = jnp.dot(od_ref[...], wd_ref[...], preferred_element_type=F32)
    acc = acc + jnp.dot(ol_ref[...], wl_ref[...], preferred_element_type=F32)
    x1 = x_ref[...] + acc
    x1_ref[...] = x1
    ms = jnp.mean(x1 * x1, axis=-1, keepdims=True)
    h2_ref[...] = (x1 * lax.rsqrt(ms + NORM_EPS) * g_ref[...]).astype(h2_ref.dtype)


def _out_proj(o_d, o_l, w_bf, x2, gain, *, tm=512):
    S, D = x2.shape
    return pl.pallas_call(
        _outproj_kernel,
        grid=(S // tm,),
        in_specs=[
            pl.BlockSpec((tm, DIFF_WIDTH), lambda i: (i, 0)),
            pl.BlockSpec((tm, DIL_WIDTH), lambda i: (i, 0)),
            pl.BlockSpec((DIFF_WIDTH, D), lambda i: (0, 0)),
            pl.BlockSpec((DIL_WIDTH, D), lambda i: (1, 0)),
            pl.BlockSpec((tm, D), lambda i: (i, 0)),
            pl.BlockSpec((1, D), lambda i: (0, 0)),
        ],
        out_specs=[pl.BlockSpec((tm, D), lambda i: (i, 0)), pl.BlockSpec((tm, D), lambda i: (i, 0))],
        out_shape=[jax.ShapeDtypeStruct((S, D), F32), jax.ShapeDtypeStruct((S, D), BF16)],
        compiler_params=pltpu.CompilerParams(
            dimension_semantics=("arbitrary",), vmem_limit_bytes=VMEM_LIMIT),
        name="out_proj",
    )(o_d, o_l, w_bf, w_bf, x2, gain)


def _ffn_up_kernel(hm_ref, hp_ref, hn_ref, wg_ref, wu_ref, cw_ref, cb_ref, o_ref, lhs_ref, *, tm, n_row_tiles):
    i = pl.program_id(0)
    j = pl.program_id(1)
    halo = BF16_SUBLANES

    @pl.when(j == 0)
    def _():
        lhs_ref[0:halo, :] = jnp.where(i == 0, jnp.zeros_like(hp_ref[...]), hp_ref[...])
        lhs_ref[halo:halo + tm, :] = hm_ref[...]
        lhs_ref[halo + tm:, :] = jnp.where(i == n_row_tiles - 1, jnp.zeros_like(hn_ref[...]), hn_ref[...])

    g = jnp.dot(lhs_ref[...], wg_ref[...], preferred_element_type=F32)
    u = jnp.dot(lhs_ref[halo:halo + tm, :], wu_ref[...], preferred_element_type=F32)
    rows = tm + 2 * halo
    g_prev = pltpu.roll(g, 1, axis=0)
    g_next = pltpu.roll(g, rows - 1, axis=0)
    y = cw_ref[0:1, :] * g_prev + cw_ref[1:2, :] * g + cw_ref[2:3, :] * g_next + cb_ref[...]
    y = y[halo:halo + tm, :]
    act = y * (1.0 / (1.0 + jnp.exp(-y))) * u
    o_ref[...] = act.astype(o_ref.dtype)


def _ffn_up(h2, w_bf, conv_w, conv_b, *, tm=1024, tn=512):
    S, D = h2.shape
    d_ff = conv_w.shape[1]
    assert d_ff % tn == 0
    nj = d_ff // tn
    ni = S // tm
    hb = tm // BF16_SUBLANES
    n_hblk = S // BF16_SUBLANES
    kern = functools.partial(_ffn_up_kernel, tm=tm, n_row_tiles=ni)
    return pl.pallas_call(
        kern,
        grid=(ni, nj),
        in_specs=[
            pl.BlockSpec((tm, D), lambda i, j: (i, 0)),
            pl.BlockSpec((BF16_SUBLANES, D), lambda i, j: (jnp.maximum(i * hb - 1, 0), 0)),
            pl.BlockSpec((BF16_SUBLANES, D), lambda i, j: (jnp.minimum((i + 1) * hb, n_hblk - 1), 0)),
            pl.BlockSpec((D, tn), lambda i, j: (0, j)),
            pl.BlockSpec((D, tn), lambda i, j: (0, nj + j)),
            pl.BlockSpec((3, tn), lambda i, j: (0, j)),
            pl.BlockSpec((1, tn), lambda i, j: (0, j)),
        ],
        out_specs=pl.BlockSpec((tm, tn), lambda i, j: (i, j)),
        out_shape=jax.ShapeDtypeStruct((S, d_ff), BF16),
        scratch_shapes=[pltpu.VMEM((tm + 2 * BF16_SUBLANES, D), BF16)],
        compiler_params=pltpu.CompilerParams(
            dimension_semantics=("arbitrary", "arbitrary"), vmem_limit_bytes=VMEM_LIMIT),
        name="ffn_up",
    )(h2, h2, h2, w_bf, w_bf, conv_w, conv_b)


def _ffn_down_kernel(a_ref, w_ref, x1_ref, g_ref, o_ref, acc_ref, *, n_k):
    k = pl.program_id(1)

    @pl.when(k == 0)
    def _():
        acc_ref[...] = x1_ref[...]

    acc_ref[...] += jnp.dot(a_ref[...], w_ref[...], preferred_element_type=F32)

    @pl.when(k == n_k - 1)
    def _():
        y = acc_ref[...]
        ms = jnp.mean(y * y, axis=-1, keepdims=True)
        o_ref[...] = y * lax.rsqrt(ms + NORM_EPS) * g_ref[...]


def _ffn_down(act, w_bf, x1, gain, *, tm=512, tk=512):
    S, d_ff = act.shape
    D = x1.shape[1]
    n_k = d_ff // tk
    kern = functools.partial(_ffn_down_kernel, n_k=n_k)
    return pl.pallas_call(
        kern,
        grid=(S // tm, n_k),
        in_specs=[
            pl.BlockSpec((tm, tk), lambda i, k: (i, k)),
            pl.BlockSpec((tk, D), lambda i, k: (k, 0)),
            pl.BlockSpec((tm, D), lambda i, k: (i, 0)),
            pl.BlockSpec((1, D), lambda i, k: (0, 0)),
        ],
        out_specs=pl.BlockSpec((tm, D), lambda i, k: (i, 0)),
        out_shape=jax.ShapeDtypeStruct((S, D), F32),
        scratch_shapes=[pltpu.VMEM((tm, D), F32)],
        compiler_params=pltpu.CompilerParams(
            dimension_semantics=("arbitrary", "arbitrary"), vmem_limit_bytes=VMEM_LIMIT),
        name="ffn_down",
    )(act, w_bf, x1, gain)


def kernel(x, norm1_gain, w_in, rel_bias_table, lambda_q1, lambda_k1, lambda_q2, lambda_k2,
           diff_subln_gain, dil_out_gain, w_out, norm2_gain, w_gate_up, conv_w, conv_b, w_down, final_gain):
    B, S, D = x.shape
    assert B == 1 and w_in.shape[0] == 1
    x2 = x.reshape(S, D)
    n_cols = w_in.shape[2]

    qscale = LOG2E / math.sqrt(HEAD_DIM)
    col = np.arange(n_cols)
    dil_q0 = 2 * DIFF_QK_COLS + DIFF_WIDTH
    is_q = (col < DIFF_QK_COLS) | ((col >= dil_q0) & (col < dil_q0 + DIL_WIDTH))
    colscale = jnp.asarray(np.where(is_q, qscale, 1.0).astype(np.float32)).reshape(1, n_cols)
    tab = rel_bias_table.astype(F32) * LOG2E

    regroup = tuple(d for _, d in DILATED_PATTERNS if d != 1)
    proj, qt_all, vt_all, *cls = _in_proj(x2, norm1_gain.reshape(1, D), w_in[0].astype(BF16), colscale, regroup)
    cls_by_dil = dict(zip(regroup, cls))

    lam = _lambda(lambda_q1.reshape(1, -1), lambda_k1.reshape(1, -1),
                  lambda_q2.reshape(1, -1), lambda_k2.reshape(1, -1))
    o_d = _diff_attention(tab, proj, qt_all, vt_all, lam, diff_subln_gain.reshape(-1, 1))

    outs, lses = [], []
    dil_blk0 = (2 * DIFF_QK_COLS + DIFF_WIDTH) // DIL_WIDTH
    for window, dilation in DILATED_PATTERNS:
        if dilation == 1:
            o_p, lse_p = _dilated_pattern(tab, proj.reshape(1, S, n_cols), dil_blk0, window, dilation)
        else:
            o_p, lse_p = _dilated_pattern(tab, cls_by_dil[dilation], 0, window, dilation)
        outs.append(o_p)
        lses.append(lse_p)
    o_l = _combine(outs, lses, dil_out_gain.reshape(1, -1), tuple(d for _, d in DILATED_PATTERNS))

    x1, h2 = _out_proj(o_d, o_l, w_out[0].astype(BF16), x2, norm2_gain.reshape(1, D))
    act = _ffn_up(h2, w_gate_up[0].astype(BF16), conv_w[0], conv_b.reshape(1, -1))
    out = _ffn_down(act, w_down[0].astype(BF16), x1, final_gain.reshape(1, D))
    return out.reshape(B, S, D)
```

```python
import functools
import math

import numpy as np
import jax
import jax.numpy as jnp
from jax import lax
from jax.experimental import pallas as pl
from jax.experimental.pallas import tpu as pltpu

F32 = jnp.float32
BF16 = jnp.bfloat16

HEAD_DIM = 128
N_DIFF_HEADS = 4
DIFF_V_DIM = 2 * HEAD_DIM
N_DIL_HEADS = 8
DIFF_QK_COLS = N_DIFF_HEADS * 2 * HEAD_DIM
DIFF_WIDTH = N_DIFF_HEADS * DIFF_V_DIM
DIL_WIDTH = N_DIL_HEADS * HEAD_DIM
DILATED_PATTERNS = ((128, 1), (512, 4), (2048, 16))
N_REL_BUCKETS = 32
REL_MAX_DISTANCE = 1024
NORM_EPS = 1e-6
SUBLN_EPS = 1e-5
NEG_INF = -1e30
LOG2E = math.log2(math.e)
LAM_INIT = 0.8 - 0.6 * math.exp(-0.3 * 0)

LANES = 128
BF16_SUBLANES = 16
DIFF_TILE = 1024
DIFF_QUERY_PANEL = 256
DIL_CHUNK = 1024
DIL_HEAD_UNROLL = 4
VMEM_LIMIT = 56 * 1024 * 1024


def _bucket_breaks():
    nb = N_REL_BUCKETS // 2
    max_exact = nb // 2
    rel = np.arange(-2 * REL_MAX_DISTANCE, 2 * REL_MAX_DISTANCE + 1)
    n = np.abs(rel)
    pos = np.log(np.maximum(n, 1) / max_exact) / math.log(REL_MAX_DISTANCE / max_exact) * (nb - max_exact)
    large = np.minimum(max_exact + np.floor(pos).astype(np.int64), nb - 1)
    bucket = np.where(rel > 0, nb, 0) + np.where(n < max_exact, n, large)
    breaks = [(int(rel[i]), int(bucket[i])) for i in range(1, len(rel)) if bucket[i] != bucket[i - 1]]
    return int(bucket[0]), breaks


FIRST_BUCKET, BUCKET_BREAKS = _bucket_breaks()
LAST_BUCKET = BUCKET_BREAKS[-1][1]
FAR_DIST = max(-BUCKET_BREAKS[0][0] + 1, BUCKET_BREAKS[-1][0])


def _bias_from_rel(rel, tab_ref, col):
    val = jnp.full(rel.shape, tab_ref[FIRST_BUCKET, col], F32)
    for thr, b in BUCKET_BREAKS:
        val = jnp.where(rel >= thr, tab_ref[b, col], val)
    return val


def _inproj_kernel(x_ref, g_ref, w_ref, cs_ref, o_ref, qt_ref, vt_ref, *rest, q_tile, v_tile, dil_tile0,
                   dilations):
    cls_refs = rest[:len(dilations)]
    h_ref, acc_ref = rest[len(dilations):]
    j = pl.program_id(1)
    tm = x_ref.shape[0]

    @pl.when(j == 0)
    def _():
        x = x_ref[...]
        ms = jnp.mean(x * x, axis=-1, keepdims=True)
        h_ref[...] = (x * lax.rsqrt(ms + NORM_EPS) * g_ref[...]).astype(BF16)

    acc = jnp.dot(h_ref[...], w_ref[...], preferred_element_type=F32) * cs_ref[...]
    o_ref[...] = acc.astype(o_ref.dtype)

    @pl.when(j == q_tile)
    def _():
        qt_ref[...] = acc.T.astype(qt_ref.dtype)

    @pl.when(j == v_tile)
    def _():
        vt_ref[...] = acc.T.astype(vt_ref.dtype)

    @pl.when(j >= dil_tile0)
    def _():
        for cb in range(acc_ref.shape[0]):
            sl = slice(cb * LANES, (cb + 1) * LANES)
            acc_ref[cb] = acc[:, sl]
            for cls_ref, dil in zip(cls_refs, dilations):
                for c in range(dil):
                    cls_ref[c, :, sl] = acc_ref[cb, pl.ds(c, tm // dil, stride=dil), :].astype(cls_ref.dtype)


def _in_proj(x2, gain, w_bf, colscale, dilations, *, tm=512, tn=1024):
    S, D = x2.shape
    N = w_bf.shape[1]
    assert DIFF_QK_COLS == tn and DIFF_WIDTH == tn and DIL_WIDTH == tn, "outputs assume one column tile each"
    dil_tile0 = (2 * DIFF_QK_COLS + DIFF_WIDTH) // tn
    n_dil_tiles = N // tn - dil_tile0
    kern = functools.partial(_inproj_kernel, q_tile=0, v_tile=2 * DIFF_QK_COLS // tn, dil_tile0=dil_tile0,
                             dilations=dilations)
    cls_specs = [pl.BlockSpec((d, tm // d, tn), lambda i, j: (0, i, jnp.maximum(j - dil_tile0, 0)))
                 for d in dilations]
    cls_shapes = [jax.ShapeDtypeStruct((d, S // d, n_dil_tiles * tn), BF16) for d in dilations]
    return pl.pallas_call(
        kern,
        grid=(S // tm, N // tn),
        in_specs=[
            pl.BlockSpec((tm, D), lambda i, j: (i, 0)),
            pl.BlockSpec((1, D), lambda i, j: (0, 0)),
            pl.BlockSpec((D, tn), lambda i, j: (0, j)),
            pl.BlockSpec((1, tn), lambda i, j: (0, j)),
        ],
        out_specs=[
            pl.BlockSpec((tm, tn), lambda i, j: (i, j)),
            pl.BlockSpec((tn, tm), lambda i, j: (0, i)),
            pl.BlockSpec((tn, tm), lambda i, j: (0, i)),
        ] + cls_specs,
        out_shape=[
            jax.ShapeDtypeStruct((S, N), BF16),
            jax.ShapeDtypeStruct((DIFF_QK_COLS, S), BF16),
            jax.ShapeDtypeStruct((DIFF_WIDTH, S), BF16),
        ] + cls_shapes,
        scratch_shapes=[pltpu.VMEM((tm, D), BF16), pltpu.VMEM((tn // LANES, tm, LANES), F32)],
        compiler_params=pltpu.CompilerParams(
            dimension_semantics=("arbitrary", "arbitrary"), vmem_limit_bytes=VMEM_LIMIT),
        name="in_proj",
    )(x2, gain, w_bf, colscale)


def _diff_attn_kernel(tab_ref, q1t_ref, q2t_ref, k1_ref, k2_ref, vt_ref, lam_ref, gain_ref, o_ref,
                      bias_ref, m_ref, l_ref, acc_ref, p_ref, alpha_ref, s_ref, smax_ref, prev_ref, cprev_ref,
                      *, T, QP, n_near, n_tiles):
    h = pl.program_id(0)
    qi = pl.program_id(1)

    @pl.when(qi == 0)
    def _():
        rows = 8
        col = lax.broadcasted_iota(jnp.int32, (rows, T), 1)
        row = lax.broadcasted_iota(jnp.int32, (rows, T), 0)
        for di, d in enumerate(range(-n_near, n_near + 1)):
            def fill(r, carry, di=di, d=d):
                r0 = pl.multiple_of(r * rows, rows)
                rel = row - col + (d * T + r0)
                bias_ref[di, pl.ds(r0, rows), :] = _bias_from_rel(rel, tab_ref, h)
                return carry
            lax.fori_loop(0, T // rows, fill, 0)

    c_left = tab_ref[FIRST_BUCKET, h]
    c_right = tab_ref[LAST_BUCKET, h]

    m_ref[...] = jnp.full(m_ref.shape, -jnp.inf, F32)
    l_ref[...] = jnp.zeros(l_ref.shape, F32)
    acc_ref[...] = jnp.zeros(acc_ref.shape, F32)

    n_chains = 2 * (T // QP)
    qts = (q1t_ref, q2t_ref)
    ks = (k1_ref, k2_ref)

    def chain_of(i):
        qp = i // 2
        return i % 2, slice(qp * QP, (qp + 1) * QP)

    s_ref[(n_chains - 1) % 2] = jnp.full(s_ref.shape[1:], -jnp.inf, F32)
    smax_ref[(n_chains - 1) % 2] = jnp.full(smax_ref.shape[1:], -jnp.inf, F32)
    for i in (n_chains - 2, n_chains - 1):
        mi, qs = chain_of(i)
        p_ref[i] = jnp.zeros((T, QP), BF16)
    alpha_ref[...] = jnp.ones(alpha_ref.shape, F32)
    prev_ref[0] = 0
    cprev_ref[0] = 0.0

    def score_stage(i, k0, bias_di):
        mi, qs = chain_of(i)
        s = jnp.dot(ks[mi][pl.ds(k0, T), :], qts[mi][:, qs], preferred_element_type=F32)
        if bias_di is not None:
            s = s + bias_ref[bias_di, :, qs]
        s_ref[i % 2] = s
        smax_ref[i % 2] = jnp.max(s, axis=0, keepdims=True)

    def softmax_stage(i, c):
        mi, qs = chain_of(i)
        s = s_ref[i % 2]
        m = m_ref[mi, :, qs]
        m_new = jnp.maximum(m, smax_ref[i % 2] + c)
        empty = m_new == -jnp.inf
        alpha = jnp.where(empty, 1.0, jnp.exp2(m - m_new))
        shift = jnp.where(empty, 0.0, m_new - c)
        p = jnp.exp2(s - shift)
        m_ref[mi, :, qs] = m_new
        l_ref[mi, :, qs] = alpha * l_ref[mi, :, qs] + jnp.sum(p, axis=0, keepdims=True)
        alpha_ref[mi, :, qs] = alpha
        p_ref[i] = p.astype(BF16)

    def value_stage(i, k0):
        mi, qs = chain_of(i)
        pv = jnp.dot(vt_ref[:, pl.ds(k0, T)], p_ref[i], preferred_element_type=F32)
        acc_ref[mi, :, qs] = alpha_ref[mi, :, qs] * acc_ref[mi, :, qs] + pv

    def tile(kt, bias_di, bias_const):
        k0 = pl.multiple_of(kt * T, T)
        pk0 = pl.multiple_of(prev_ref[0] * T, T)
        c_tile = 0.0 if bias_const is None else bias_const
        for i in range(n_chains):
            value_stage((i - 2) % n_chains, pk0 if i < 2 else k0)
            softmax_stage((i - 1) % n_chains, cprev_ref[0] if i < 1 else c_tile)
            score_stage(i, k0, bias_di)
        prev_ref[0] = kt
        cprev_ref[0] = c_tile

    lo = jnp.maximum(qi - n_near, 0)
    hi = jnp.minimum(qi + n_near + 1, n_tiles)

    @pl.loop(0, lo)
    def _(kt):
        tile(kt, None, c_left)

    for di, d in enumerate(range(-n_near, n_near + 1)):
        kt = qi + d

        @pl.when(jnp.logical_and(kt >= 0, kt < n_tiles))
        def _(di=di, kt=kt):
            tile(kt, di, None)

    @pl.loop(hi, n_tiles)
    def _(kt):
        tile(kt, None, c_right)

    pk0 = pl.multiple_of(prev_ref[0] * T, T)
    value_stage(n_chains - 2, pk0)
    softmax_stage(n_chains - 1, cprev_ref[0])
    value_stage(n_chains - 1, pk0)

    lam = lam_ref[0, 0]
    o = acc_ref[0] / l_ref[0] - lam * (acc_ref[1] / l_ref[1])
    ms = jnp.mean(o * o, axis=0, keepdims=True)
    o = o * lax.rsqrt(ms + SUBLN_EPS) * (gain_ref[...] * (1.0 - LAM_INIT))
    o_ref[...] = o.T.astype(o_ref.dtype)


def _lambda_kernel(q1_ref, k1_ref, q2_ref, k2_ref, o_ref):
    a = jnp.sum(q1_ref[...] * k1_ref[...], axis=-1, keepdims=True)
    b = jnp.sum(q2_ref[...] * k2_ref[...], axis=-1, keepdims=True)
    o_ref[...] = jnp.exp(a) - jnp.exp(b) + LAM_INIT


def _lambda(lq1, lk1, lq2, lk2):
    return pl.pallas_call(
        _lambda_kernel, out_shape=jax.ShapeDtypeStruct((1, 1), F32), name="diff_lambda",
    )(lq1, lk1, lq2, lk2)


def _diff_attention(tab_diff, proj, qt_all, vt_all, lam, gain_col):
    S = proj.shape[0]
    T = DIFF_TILE
    n_tiles = S // T
    n_near = -(-(FAR_DIST - 1) // T)
    kern = functools.partial(_diff_attn_kernel, T=T, QP=DIFF_QUERY_PANEL, n_near=n_near, n_tiles=n_tiles)
    k_block0 = DIFF_QK_COLS // HEAD_DIM
    smem = pl.BlockSpec(memory_space=pltpu.SMEM)
    return pl.pallas_call(
        kern,
        grid=(N_DIFF_HEADS, n_tiles),
        in_specs=[
            smem,
            pl.BlockSpec((HEAD_DIM, T), lambda h, i: (2 * h, i)),
            pl.BlockSpec((HEAD_DIM, T), lambda h, i: (2 * h + 1, i)),
            pl.BlockSpec((S, HEAD_DIM), lambda h, i: (0, k_block0 + 2 * h), pipeline_mode=pl.Buffered(1)),
            pl.BlockSpec((S, HEAD_DIM), lambda h, i: (0, k_block0 + 2 * h + 1), pipeline_mode=pl.Buffered(1)),
            pl.BlockSpec((DIFF_V_DIM, S), lambda h, i: (h, 0), pipeline_mode=pl.Buffered(1)),
            smem,
            pl.BlockSpec((DIFF_V_DIM, 1), lambda h, i: (0, 0)),
        ],
        out_specs=pl.BlockSpec((T, DIFF_V_DIM), lambda h, i: (i, h)),
        out_shape=jax.ShapeDtypeStruct((S, DIFF_WIDTH), BF16),
        scratch_shapes=[pltpu.VMEM((2 * n_near + 1, T, T), F32),
                        pltpu.VMEM((2, 1, T), F32), pltpu.VMEM((2, 1, T), F32),
                        pltpu.VMEM((2, DIFF_V_DIM, T), F32),
                        pltpu.VMEM((2 * (T // DIFF_QUERY_PANEL), T, DIFF_QUERY_PANEL), BF16),
                        pltpu.VMEM((2, 1, T), F32),
                        pltpu.VMEM((2, T, DIFF_QUERY_PANEL), F32), pltpu.VMEM((2, 1, DIFF_QUERY_PANEL), F32),
                        pltpu.SMEM((1,), jnp.int32), pltpu.SMEM((1,), F32)],
        compiler_params=pltpu.CompilerParams(
            dimension_semantics=("arbitrary", "arbitrary"), vmem_limit_bytes=VMEM_LIMIT),
        name="diff_attn",
    )(tab_diff, qt_all, qt_all, proj, proj, vt_all, lam, gain_col)


def _dilated_kernel(tab_ref, q_ref, kp_ref, km_ref, kn_ref, vp_ref, vm_ref, vn_ref, o_ref, lse_ref,
                    bias_ref, kx_ref, vx_ref, *, R, B, half, dilation, n_chunks):
    c = pl.program_id(0)
    n = pl.program_id(1)
    W = B + 2 * half
    nblk = R // B

    @pl.when(jnp.logical_and(c == 0, n == 0))
    def _():
        rows = 8
        col = lax.broadcasted_iota(jnp.int32, (rows, W), 1)
        row = lax.broadcasted_iota(jnp.int32, (rows, W), 0)
        for hh in range(N_DIL_HEADS):
            def fill(r, carry, hh=hh):
                r0 = pl.multiple_of(r * rows, rows)
                off = col - half - (row + r0)
                bias = _bias_from_rel(off * dilation, tab_ref, N_DIFF_HEADS + hh)
                base = jnp.where(jnp.abs(off) <= half, bias, NEG_INF * LOG2E)
                bias_ref[hh, 1, pl.ds(r0, rows), :] = base
                bias_ref[hh, 0, pl.ds(r0, rows), :] = jnp.where(col >= half, base, NEG_INF * LOG2E)
                bias_ref[hh, 2, pl.ds(r0, rows), :] = jnp.where(col < B + half, base, NEG_INF * LOG2E)
                return carry
            lax.fori_loop(0, B // rows, fill, 0)

    kx_ref[0:half, :] = kp_ref[...]
    kx_ref[half:half + R, :] = km_ref[...]
    kx_ref[half + R:, :] = kn_ref[...]
    vx_ref[0:half, :] = vp_ref[...]
    vx_ref[half:half + R, :] = vm_ref[...]
    vx_ref[half + R:, :] = vn_ref[...]

    def chain(hh, b):
        c0 = pl.multiple_of(hh * HEAD_DIM, HEAD_DIM)
        r0 = b * B
        var = 1
        if b == 0:
            var = jnp.where(n == 0, 0, var)
        if b == nblk - 1:
            var = jnp.where(n == n_chunks - 1, 2, var)
        q = q_ref[pl.ds(r0, B), pl.ds(c0, HEAD_DIM)]
        k = kx_ref[pl.ds(r0, W), pl.ds(c0, HEAD_DIM)]
        v = vx_ref[pl.ds(r0, W), pl.ds(c0, HEAD_DIM)]
        s = lax.dot_general(q, k, (((1,), (1,)), ((), ())), preferred_element_type=F32)
        s = s + bias_ref[hh, var]
        m = jnp.max(s, axis=-1, keepdims=True)
        e = jnp.exp2(s - m)
        den = jnp.sum(e, axis=-1, keepdims=True)
        o = jnp.dot(e.astype(BF16), v, preferred_element_type=F32) / den
        o_ref[pl.ds(r0, B), pl.ds(c0, HEAD_DIM)] = o.astype(o_ref.dtype)
        lse = m + jnp.log2(den)
        lse_ref[pl.ds(r0, B), pl.ds(c0, HEAD_DIM)] = jnp.broadcast_to(lse, (B, HEAD_DIM))

    @pl.loop(0, N_DIL_HEADS // DIL_HEAD_UNROLL)
    def _(hg):
        for u in range(DIL_HEAD_UNROLL):
            for b in range(nblk):
                chain(hg * DIL_HEAD_UNROLL + u, b)


def _dilated_pattern(tab, qkv, col_blk0, window, dilation, *, B=256):
    _, L, _ = qkv.shape
    R = min(DIL_CHUNK, L)
    half = window // (2 * dilation)
    assert L % R == 0 and R % B == 0 and half % BF16_SUBLANES == 0 and R % half == 0
    n_chunks = L // R
    q_blk, k_blk, v_blk = col_blk0, col_blk0 + 1, col_blk0 + 2
    hb = R // half
    n_hblk = L // half

    def main(blk):
        return pl.BlockSpec((None, R, DIL_WIDTH), lambda c, n: (c, n, blk))

    def prev(blk):
        return pl.BlockSpec((None, half, DIL_WIDTH), lambda c, n: (c, jnp.maximum(n * hb - 1, 0), blk))

    def nxt(blk):
        return pl.BlockSpec((None, half, DIL_WIDTH), lambda c, n: (c, jnp.minimum((n + 1) * hb, n_hblk - 1), blk))

    kern = functools.partial(_dilated_kernel, R=R, B=B, half=half, dilation=dilation, n_chunks=n_chunks)
    out_spec = pl.BlockSpec((None, R, DIL_WIDTH), lambda c, n: (c, n, 0))
    return pl.pallas_call(
        kern,
        grid=(dilation, n_chunks),
        in_specs=[pl.BlockSpec(memory_space=pltpu.SMEM),
                  main(q_blk), prev(k_blk), main(k_blk), nxt(k_blk), prev(v_blk), main(v_blk), nxt(v_blk)],
        out_specs=[out_spec, out_spec],
        out_shape=[jax.ShapeDtypeStruct((dilation, L, DIL_WIDTH), BF16),
                   jax.ShapeDtypeStruct((dilation, L, DIL_WIDTH), F32)],
        scratch_shapes=[pltpu.VMEM((N_DIL_HEADS, 3, B, B + 2 * half), F32),
                        pltpu.VMEM((R + 2 * half, DIL_WIDTH), BF16),
                        pltpu.VMEM((R + 2 * half, DIL_WIDTH), BF16)],
        compiler_params=pltpu.CompilerParams(
            dimension_semantics=("arbitrary", "arbitrary"), vmem_limit_bytes=VMEM_LIMIT),
        name=f"dilated_d{dilation}",
    )(tab, qkv, qkv, qkv, qkv, qkv, qkv, qkv)


def _combine_kernel(*refs, dilations):
    n = len(dilations)
    o_refs, l_refs = refs[:n], refs[n:2 * n]
    g_ref, out_ref = refs[2 * n], refs[2 * n + 1]
    scratch = refs[2 * n + 2:]
    tm = out_ref.shape[0]

    for hh in range(N_DIL_HEADS):
        sl = slice(hh * HEAD_DIM, (hh + 1) * HEAD_DIM)
        outs, lses = [], []
        si = 0
        for o_ref, l_ref, dil in zip(o_refs, l_refs, dilations):
            if dil == 1:
                outs.append(o_ref[0, :, sl].astype(F32))
                lses.append(l_ref[0, :, sl])
                continue
            os_ref, ls_ref = scratch[si], scratch[si + 1]
            si += 2
            for c in range(dil):
                os_ref[hh, pl.ds(c, tm // dil, stride=dil), :] = o_ref[c, :, sl].astype(F32)
                ls_ref[hh, pl.ds(c, tm // dil, stride=dil), :] = l_ref[c, :, sl]
            outs.append(os_ref[hh])
            lses.append(ls_ref[hh])

        m = functools.reduce(jnp.maximum, lses)
        ws = [jnp.exp2(l - m) for l in lses]
        tot = functools.reduce(lambda a, b: a + b, ws)
        oh = functools.reduce(lambda a, b: a + b, [(w / tot) * op for w, op in zip(ws, outs)])
        ms = jnp.mean(oh * oh, axis=-1, keepdims=True)
        out_ref[:, sl] = (oh * lax.rsqrt(ms + NORM_EPS) * g_ref[:, sl]).astype(out_ref.dtype)


def _combine(outs, lses, gain, dilations, *, tm=512):
    S = outs[0].shape[0] * outs[0].shape[1]
    specs = [pl.BlockSpec((d, tm // d, DIL_WIDTH), lambda i: (0, i, 0)) for d in dilations]
    n_scr = sum(1 for d in dilations if d != 1)
    return pl.pallas_call(
        functools.partial(_combine_kernel, dilations=dilations),
        grid=(S // tm,),
        in_specs=specs + specs + [pl.BlockSpec((1, DIL_WIDTH), lambda i: (0, 0))],
        scratch_shapes=[pltpu.VMEM((N_DIL_HEADS, tm, HEAD_DIM), F32)] * (2 * n_scr),
        out_specs=pl.BlockSpec((tm, DIL_WIDTH), lambda i: (i, 0)),
        out_shape=jax.ShapeDtypeStruct((S, DIL_WIDTH), BF16),
        compiler_params=pltpu.CompilerParams(
            dimension_semantics=("arbitrary",), vmem_limit_bytes=VMEM_LIMIT),
        name="dilated_combine",
    )(*outs, *lses, gain)


def _outproj_kernel(od_ref, ol_ref, wd_ref, wl_ref, x_ref, g_ref, x1_ref, h2_ref):
    acc = jnp.dot(od_ref[...], wd_ref[...], preferred_element_type=F32)
    acc = acc + jnp.dot(ol_ref[...], wl_ref[...], preferred_element_type=F32)
    x1 = x_ref[...] + acc
    x1_ref[...] = x1
    ms = jnp.mean(x1 * x1, axis=-1, keepdims=True)
    h2_ref[...] = (x1 * lax.rsqrt(ms + NORM_EPS) * g_ref[...]).astype(h2_ref.dtype)


def _out_proj(o_d, o_l, w_bf, x2, gain, *, tm=512):
    S, D = x2.shape
    return pl.pallas_call(
        _outproj_kernel,
        grid=(S // tm,),
        in_specs=[
            pl.BlockSpec((tm, DIFF_WIDTH), lambda i: (i, 0)),
            pl.BlockSpec((tm, DIL_WIDTH), lambda i: (i, 0)),
            pl.BlockSpec((DIFF_WIDTH, D), lambda i: (0, 0)),
            pl.BlockSpec((DIL_WIDTH, D), lambda i: (1, 0)),
            pl.BlockSpec((tm, D), lambda i: (i, 0)),
            pl.BlockSpec((1, D), lambda i: (0, 0)),
        ],
        out_specs=[pl.BlockSpec((tm, D), lambda i: (i, 0)), pl.BlockSpec((tm, D), lambda i: (i, 0))],
        out_shape=[jax.ShapeDtypeStruct((S, D), F32), jax.ShapeDtypeStruct((S, D), BF16)],
        compiler_params=pltpu.CompilerParams(
            dimension_semantics=("arbitrary",), vmem_limit_bytes=VMEM_LIMIT),
        name="out_proj",
    )(o_d, o_l, w_bf, w_bf, x2, gain)


def _ffn_up_kernel(hm_ref, hp_ref, hn_ref, wg_ref, wu_ref, cw_ref, cb_ref, o_ref, lhs_ref, *, tm, n_row_tiles):
    i = pl.program_id(0)
    j = pl.program_id(1)
    halo = BF16_SUBLANES

    @pl.when(j == 0)
    def _():
        lhs_ref[0:halo, :] = jnp.where(i == 0, jnp.zeros_like(hp_ref[...]), hp_ref[...])
        lhs_ref[halo:halo + tm, :] = hm_ref[...]
        lhs_ref[halo + tm:, :] = jnp.where(i == n_row_tiles - 1, jnp.zeros_like(hn_ref[...]), hn_ref[...])

    g = jnp.dot(lhs_ref[...], wg_ref[...], preferred_element_type=F32)
    u = jnp.dot(lhs_ref[halo:halo + tm, :], wu_ref[...], preferred_element_type=F32)
    rows = tm + 2 * halo
    g_prev = pltpu.roll(g, 1, axis=0)
    g_next = pltpu.roll(g, rows - 1, axis=0)
    y = cw_ref[0:1, :] * g_prev + cw_ref[1:2, :] * g + cw_ref[2:3, :] * g_next + cb_ref[...]
    y = y[halo:halo + tm, :]
    act = y * (1.0 / (1.0 + jnp.exp(-y))) * u
    o_ref[...] = act.astype(o_ref.dtype)


def _ffn_up(h2, w_bf, conv_w, conv_b, *, tm=1024, tn=512):
    S, D = h2.shape
    d_ff = conv_w.shape[1]
    assert d_ff % tn == 0
    nj = d_ff // tn
    ni = S // tm
    hb = tm // BF16_SUBLANES
    n_hblk = S // BF16_SUBLANES
    kern = functools.partial(_ffn_up_kernel, tm=tm, n_row_tiles=ni)
    return pl.pallas_call(
        kern,
        grid=(ni, nj),
        in_specs=[
            pl.BlockSpec((tm, D), lambda i, j: (i, 0)),
            pl.BlockSpec((BF16_SUBLANES, D), lambda i, j: (jnp.maximum(i * hb - 1, 0), 0)),
            pl.BlockSpec((BF16_SUBLANES, D), lambda i, j: (jnp.minimum((i + 1) * hb, n_hblk - 1), 0)),
            pl.BlockSpec((D, tn), lambda i, j: (0, j)),
            pl.BlockSpec((D, tn), lambda i, j: (0, nj + j)),
            pl.BlockSpec((3, tn), lambda i, j: (0, j)),
            pl.BlockSpec((1, tn), lambda i, j: (0, j)),
        ],
        out_specs=pl.BlockSpec((tm, tn), lambda i, j: (i, j)),
        out_shape=jax.ShapeDtypeStruct((S, d_ff), BF16),
        scratch_shapes=[pltpu.VMEM((tm + 2 * BF16_SUBLANES, D), BF16)],
        compiler_params=pltpu.CompilerParams(
            dimension_semantics=("arbitrary", "arbitrary"), vmem_limit_bytes=VMEM_LIMIT),
        name="ffn_up",
    )(h2, h2, h2, w_bf, w_bf, conv_w, conv_b)


def _ffn_down_kernel(a_ref, w_ref, x1_ref, g_ref, o_ref, acc_ref, *, n_k):
    k = pl.program_id(1)

    @pl.when(k == 0)
    def _():
        acc_ref[...] = x1_ref[...]

    acc_ref[...] += jnp.dot(a_ref[...], w_ref[...], preferred_element_type=F32)

    @pl.when(k == n_k - 1)
    def _():
        y = acc_ref[...]
        ms = jnp.mean(y * y, axis=-1, keepdims=True)
        o_ref[...] = y * lax.rsqrt(ms + NORM_EPS) * g_ref[...]


def _ffn_down(act, w_bf, x1, gain, *, tm=512, tk=1408):
    S, d_ff = act.shape
    D = x1.shape[1]
    n_k = d_ff // tk
    kern = functools.partial(_ffn_down_kernel, n_k=n_k)
    return pl.pallas_call(
        kern,
        grid=(S // tm, n_k),
        in_specs=[
            pl.BlockSpec((tm, tk), lambda i, k: (i, k)),
            pl.BlockSpec((tk, D), lambda i, k: (k, 0)),
            pl.BlockSpec((tm, D), lambda i, k: (i, 0)),
            pl.BlockSpec((1, D), lambda i, k: (0, 0)),
        ],
        out_specs=pl.BlockSpec((tm, D), lambda i, k: (i, 0)),
        out_shape=jax.ShapeDtypeStruct((S, D), F32),
        scratch_shapes=[pltpu.VMEM((tm, D), F32)],
        compiler_params=pltpu.CompilerParams(
            dimension_semantics=("arbitrary", "arbitrary"), vmem_limit_bytes=VMEM_LIMIT),
        name="ffn_down",
    )(act, w_bf, x1, gain)


def kernel(x, norm1_gain, w_in, rel_bias_table, lambda_q1, lambda_k1, lambda_q2, lambda_k2,
           diff_subln_gain, dil_out_gain, w_out, norm2_gain, w_gate_up, conv_w, conv_b, w_down, final_gain):
    B, S, D = x.shape
    assert B == 1 and w_in.shape[0] == 1
    x2 = x.reshape(S, D)
    n_cols = w_in.shape[2]

    qscale = LOG2E / math.sqrt(HEAD_DIM)
    col = np.arange(n_cols)
    dil_q0 = 2 * DIFF_QK_COLS + DIFF_WIDTH
    is_q = (col < DIFF_QK_COLS) | ((col >= dil_q0) & (col < dil_q0 + DIL_WIDTH))
    colscale = jnp.asarray(np.where(is_q, qscale, 1.0).astype(np.float32)).reshape(1, n_cols)
    tab = rel_bias_table.astype(F32) * LOG2E

    regroup = tuple(d for _, d in DILATED_PATTERNS if d != 1)
    proj, qt_all, vt_all, *cls = _in_proj(x2, norm1_gain.reshape(1, D), w_in[0].astype(BF16), colscale, regroup)
    cls_by_dil = dict(zip(regroup, cls))

    lam = _lambda(lambda_q1.reshape(1, -1), lambda_k1.reshape(1, -1),
                  lambda_q2.reshape(1, -1), lambda_k2.reshape(1, -1))
    o_d = _diff_attention(tab, proj, qt_all, vt_all, lam, diff_subln_gain.reshape(-1, 1))

    outs, lses = [], []
    dil_blk0 = (2 * DIFF_QK_COLS + DIFF_WIDTH) // DIL_WIDTH
    for window, dilation in DILATED_PATTERNS:
        if dilation == 1:
            o_p, lse_p = _dilated_pattern(tab, proj.reshape(1, S, n_cols), dil_blk0, window, dilation)
        else:
            o_p, lse_p = _dilated_pattern(tab, cls_by_dil[dilation], 0, window, dilation)
        outs.append(o_p)
        lses.append(lse_p)
    o_l = _combine(outs, lses, dil_out_gain.reshape(1, -1), tuple(d for _, d in DILATED_PATTERNS))

    x1, h2 = _out_proj(o_d, o_l, w_out[0].astype(BF16), x2, norm2_gain.reshape(1, D))
    act = _ffn_up(h2, w_gate_up[0].astype(BF16), conv_w[0], conv_b.reshape(1, -1))
    out = _ffn_down(act, w_down[0].astype(BF16), x1, final_gain.reshape(1, D))
    return out.reshape(B, S, D)
```

```python
import functools
import math

import numpy as np
import jax
import jax.numpy as jnp
from jax import lax
from jax.experimental import pallas as pl
from jax.experimental.pallas import tpu as pltpu

F32 = jnp.float32
BF16 = jnp.bfloat16

HEAD_DIM = 128
N_DIFF_HEADS = 4
DIFF_V_DIM = 2 * HEAD_DIM
N_DIL_HEADS = 8
DIFF_QK_COLS = N_DIFF_HEADS * 2 * HEAD_DIM
DIFF_WIDTH = N_DIFF_HEADS * DIFF_V_DIM
DIL_WIDTH = N_DIL_HEADS * HEAD_DIM
DILATED_PATTERNS = ((128, 1), (512, 4), (2048, 16))
N_REL_BUCKETS = 32
REL_MAX_DISTANCE = 1024
NORM_EPS = 1e-6
SUBLN_EPS = 1e-5
NEG_INF = -1e30
LOG2E = math.log2(math.e)
LAM_INIT = 0.8 - 0.6 * math.exp(-0.3 * 0)

LANES = 128
BF16_SUBLANES = 16
DIFF_TILE = 1024
DIFF_QUERY_PANEL = 256
SHIFT_MARGIN = 1.0 + 2.0 ** -8
MIN_DENOMINATOR = 2.0 ** -60
DIL_CHUNK = 1024
DIL_HEAD_UNROLL = 4
VMEM_LIMIT = 56 * 1024 * 1024


def _bucket_breaks():
    nb = N_REL_BUCKETS // 2
    max_exact = nb // 2
    rel = np.arange(-2 * REL_MAX_DISTANCE, 2 * REL_MAX_DISTANCE + 1)
    n = np.abs(rel)
    pos = np.log(np.maximum(n, 1) / max_exact) / math.log(REL_MAX_DISTANCE / max_exact) * (nb - max_exact)
    large = np.minimum(max_exact + np.floor(pos).astype(np.int64), nb - 1)
    bucket = np.where(rel > 0, nb, 0) + np.where(n < max_exact, n, large)
    breaks = [(int(rel[i]), int(bucket[i])) for i in range(1, len(rel)) if bucket[i] != bucket[i - 1]]
    return int(bucket[0]), breaks


FIRST_BUCKET, BUCKET_BREAKS = _bucket_breaks()
LAST_BUCKET = BUCKET_BREAKS[-1][1]
FAR_DIST = max(-BUCKET_BREAKS[0][0] + 1, BUCKET_BREAKS[-1][0])


def _bias_from_rel(rel, tab_ref, col):
    val = jnp.full(rel.shape, tab_ref[FIRST_BUCKET, col], F32)
    for thr, b in BUCKET_BREAKS:
        val = jnp.where(rel >= thr, tab_ref[b, col], val)
    return val


def _inproj_kernel(x_ref, g_ref, w_ref, cs_ref, o_ref, qt_ref, vt_ref, *rest, q_tile, v_tile, dil_tile0,
                   dilations):
    cls_refs = rest[:len(dilations)]
    h_ref, acc_ref = rest[len(dilations):]
    j = pl.program_id(1)
    tm = x_ref.shape[0]

    @pl.when(j == 0)
    def _():
        x = x_ref[...]
        ms = jnp.mean(x * x, axis=-1, keepdims=True)
        h_ref[...] = (x * lax.rsqrt(ms + NORM_EPS) * g_ref[...]).astype(BF16)

    acc = jnp.dot(h_ref[...], w_ref[...], preferred_element_type=F32) * cs_ref[...]
    o_ref[...] = acc.astype(o_ref.dtype)

    @pl.when(j == q_tile)
    def _():
        qt_ref[...] = acc.T.astype(qt_ref.dtype)

    @pl.when(j == v_tile)
    def _():
        vt_ref[...] = acc.T.astype(vt_ref.dtype)

    @pl.when(j >= dil_tile0)
    def _():
        for cb in range(acc_ref.shape[0]):
            sl = slice(cb * LANES, (cb + 1) * LANES)
            acc_ref[cb] = acc[:, sl]
            for cls_ref, dil in zip(cls_refs, dilations):
                for c in range(dil):
                    cls_ref[c, :, sl] = acc_ref[cb, pl.ds(c, tm // dil, stride=dil), :].astype(cls_ref.dtype)


def _in_proj(x2, gain, w_bf, colscale, dilations, *, tm=512, tn=1024):
    S, D = x2.shape
    N = w_bf.shape[1]
    assert DIFF_QK_COLS == tn and DIFF_WIDTH == tn and DIL_WIDTH == tn, "outputs assume one column tile each"
    dil_tile0 = (2 * DIFF_QK_COLS + DIFF_WIDTH) // tn
    n_dil_tiles = N // tn - dil_tile0
    kern = functools.partial(_inproj_kernel, q_tile=0, v_tile=2 * DIFF_QK_COLS // tn, dil_tile0=dil_tile0,
                             dilations=dilations)
    cls_specs = [pl.BlockSpec((d, tm // d, tn), lambda i, j: (0, i, jnp.maximum(j - dil_tile0, 0)))
                 for d in dilations]
    cls_shapes = [jax.ShapeDtypeStruct((d, S // d, n_dil_tiles * tn), BF16) for d in dilations]
    return pl.pallas_call(
        kern,
        grid=(S // tm, N // tn),
        in_specs=[
            pl.BlockSpec((tm, D), lambda i, j: (i, 0)),
            pl.BlockSpec((1, D), lambda i, j: (0, 0)),
            pl.BlockSpec((D, tn), lambda i, j: (0, j)),
            pl.BlockSpec((1, tn), lambda i, j: (0, j)),
        ],
        out_specs=[
            pl.BlockSpec((tm, tn), lambda i, j: (i, j)),
            pl.BlockSpec((tn, tm), lambda i, j: (0, i)),
            pl.BlockSpec((tn, tm), lambda i, j: (0, i)),
        ] + cls_specs,
        out_shape=[
            jax.ShapeDtypeStruct((S, N), BF16),
            jax.ShapeDtypeStruct((DIFF_QK_COLS, S), BF16),
            jax.ShapeDtypeStruct((DIFF_WIDTH, S), BF16),
        ] + cls_shapes,
        scratch_shapes=[pltpu.VMEM((tm, D), BF16), pltpu.VMEM((tn // LANES, tm, LANES), F32)],
        compiler_params=pltpu.CompilerParams(
            dimension_semantics=("arbitrary", "arbitrary"), vmem_limit_bytes=VMEM_LIMIT),
        name="in_proj",
    )(x2, gain, w_bf, colscale)


def _diff_attn_kernel(tab_ref, q1t_ref, q2t_ref, k1_ref, k2_ref, vt_ref, lam_ref, gain_ref, o_ref,
                      bias_ref, m_ref, l_ref, acc_ref, p_ref, shift_ref, knorm_ref, prev_ref,
                      *, T, QP, n_near, n_tiles):
    h = pl.program_id(0)
    qi = pl.program_id(1)
    n_chains = 2 * (T // QP)
    qts = (q1t_ref, q2t_ref)
    ks = (k1_ref, k2_ref)

    def chain_of(i):
        qp = i // 2
        return i % 2, slice(qp * QP, (qp + 1) * QP)

    @pl.when(qi == 0)
    def _():
        for mi in range(2):
            def knorm(t, best, mi=mi):
                k = ks[mi][pl.ds(pl.multiple_of(t * T, T), T), :].astype(F32)
                return jnp.maximum(best, jnp.max(jnp.sum(k * k, axis=1, keepdims=True), axis=0, keepdims=True))
            knorm_ref[mi] = lax.fori_loop(0, n_tiles, knorm, jnp.zeros((1, 1), F32))

        rows = 8
        col = lax.broadcasted_iota(jnp.int32, (rows, T), 1)
        row = lax.broadcasted_iota(jnp.int32, (rows, T), 0)
        for di, d in enumerate(range(-n_near, n_near + 1)):
            def fill(r, carry, di=di, d=d):
                r0 = pl.multiple_of(r * rows, rows)
                rel = row - col + (d * T + r0)
                bias_ref[di, pl.ds(r0, rows), :] = _bias_from_rel(rel, tab_ref, h)
                return carry
            lax.fori_loop(0, T // rows, fill, 0)

    c_left = tab_ref[FIRST_BUCKET, h]
    c_right = tab_ref[LAST_BUCKET, h]

    bias_max = tab_ref[0, h]
    for b in range(1, N_REL_BUCKETS):
        bias_max = jnp.maximum(bias_max, tab_ref[b, h])

    for mi in range(2):
        q = qts[mi][...].astype(F32)
        qnorm2 = jnp.sum(q * q, axis=0, keepdims=True)
        shift_ref[mi] = jnp.sqrt(qnorm2 * knorm_ref[mi]) * SHIFT_MARGIN + bias_max
    l_ref[...] = jnp.zeros(l_ref.shape, F32)
    acc_ref[...] = jnp.zeros(acc_ref.shape, F32)
    p_ref[n_chains - 1] = jnp.zeros((T, QP), BF16)
    prev_ref[0] = 0

    def score_stage(i, k0, bias_di, bias_const):
        mi, qs = chain_of(i)
        s = jnp.dot(ks[mi][pl.ds(k0, T), :], qts[mi][:, qs], preferred_element_type=F32)
        if bias_di is not None:
            p = jnp.exp2(s + bias_ref[bias_di, :, qs] - shift_ref[mi, :, qs])
        else:
            p = jnp.exp2(s - (shift_ref[mi, :, qs] - bias_const))
        l_ref[mi, :, qs] += jnp.sum(p, axis=0, keepdims=True)
        p_ref[i] = p.astype(BF16)

    def value_stage(i, k0):
        mi, qs = chain_of(i)
        acc_ref[mi, :, qs] += jnp.dot(vt_ref[:, pl.ds(k0, T)], p_ref[i], preferred_element_type=F32)

    def tile(kt, bias_di, bias_const):
        k0 = pl.multiple_of(kt * T, T)
        pk0 = pl.multiple_of(prev_ref[0] * T, T)
        for i in range(n_chains):
            value_stage((i - 1) % n_chains, pk0 if i < 1 else k0)
            score_stage(i, k0, bias_di, bias_const)
        prev_ref[0] = kt

    lo = jnp.maximum(qi - n_near, 0)
    hi = jnp.minimum(qi + n_near + 1, n_tiles)

    @pl.loop(0, lo)
    def _(kt):
        tile(kt, None, c_left)

    for di, d in enumerate(range(-n_near, n_near + 1)):
        kt = qi + d

        @pl.when(jnp.logical_and(kt >= 0, kt < n_tiles))
        def _(di=di, kt=kt):
            tile(kt, di, None)

    @pl.loop(hi, n_tiles)
    def _(kt):
        tile(kt, None, c_right)

    value_stage(n_chains - 1, pl.multiple_of(prev_ref[0] * T, T))

    @pl.when(jnp.min(l_ref[...]) < MIN_DENOMINATOR)
    def _():
        m_ref[...] = jnp.full(m_ref.shape, -jnp.inf, F32)
        l_ref[...] = jnp.zeros(l_ref.shape, F32)
        acc_ref[...] = jnp.zeros(acc_ref.shape, F32)

        @pl.loop(0, n_tiles)
        def _(kt):
            k0 = pl.multiple_of(kt * T, T)
            d = kt - qi
            near = jnp.abs(d) <= n_near
            di = jnp.clip(d + n_near, 0, 2 * n_near)
            c_far = jnp.where(d < 0, c_left, c_right)
            for i in range(n_chains):
                mi, qs = chain_of(i)
                s = jnp.dot(ks[mi][pl.ds(k0, T), :], qts[mi][:, qs], preferred_element_type=F32)
                s = s + jnp.where(near, bias_ref[di, :, qs], c_far)
                m = m_ref[mi, :, qs]
                m_new = jnp.maximum(m, jnp.max(s, axis=0, keepdims=True))
                alpha = jnp.exp2(m - m_new)
                p = jnp.exp2(s - m_new)
                m_ref[mi, :, qs] = m_new
                l_ref[mi, :, qs] = alpha * l_ref[mi, :, qs] + jnp.sum(p, axis=0, keepdims=True)
                pv = jnp.dot(vt_ref[:, pl.ds(k0, T)], p.astype(BF16), preferred_element_type=F32)
                acc_ref[mi, :, qs] = alpha * acc_ref[mi, :, qs] + pv

    lam = lam_ref[0, 0]
    o = acc_ref[0] / l_ref[0] - lam * (acc_ref[1] / l_ref[1])
    ms = jnp.mean(o * o, axis=0, keepdims=True)
    o = o * lax.rsqrt(ms + SUBLN_EPS) * (gain_ref[...] * (1.0 - LAM_INIT))
    o_ref[...] = o.T.astype(o_ref.dtype)


def _lambda_kernel(q1_ref, k1_ref, q2_ref, k2_ref, o_ref):
    a = jnp.sum(q1_ref[...] * k1_ref[...], axis=-1, keepdims=True)
    b = jnp.sum(q2_ref[...] * k2_ref[...], axis=-1, keepdims=True)
    o_ref[...] = jnp.exp(a) - jnp.exp(b) + LAM_INIT


def _lambda(lq1, lk1, lq2, lk2):
    return pl.pallas_call(
        _lambda_kernel, out_shape=jax.ShapeDtypeStruct((1, 1), F32), name="diff_lambda",
    )(lq1, lk1, lq2, lk2)


def _diff_attention(tab_diff, proj, qt_all, vt_all, lam, gain_col):
    S = proj.shape[0]
    T = DIFF_TILE
    n_tiles = S // T
    n_near = -(-(FAR_DIST - 1) // T)
    kern = functools.partial(_diff_attn_kernel, T=T, QP=DIFF_QUERY_PANEL, n_near=n_near, n_tiles=n_tiles)
    k_block0 = DIFF_QK_COLS // HEAD_DIM
    smem = pl.BlockSpec(memory_space=pltpu.SMEM)
    return pl.pallas_call(
        kern,
        grid=(N_DIFF_HEADS, n_tiles),
        in_specs=[
            smem,
            pl.BlockSpec((HEAD_DIM, T), lambda h, i: (2 * h, i)),
            pl.BlockSpec((HEAD_DIM, T), lambda h, i: (2 * h + 1, i)),
            pl.BlockSpec((S, HEAD_DIM), lambda h, i: (0, k_block0 + 2 * h), pipeline_mode=pl.Buffered(1)),
            pl.BlockSpec((S, HEAD_DIM), lambda h, i: (0, k_block0 + 2 * h + 1), pipeline_mode=pl.Buffered(1)),
            pl.BlockSpec((DIFF_V_DIM, S), lambda h, i: (h, 0), pipeline_mode=pl.Buffered(1)),
            smem,
            pl.BlockSpec((DIFF_V_DIM, 1), lambda h, i: (0, 0)),
        ],
        out_specs=pl.BlockSpec((T, DIFF_V_DIM), lambda h, i: (i, h)),
        out_shape=jax.ShapeDtypeStruct((S, DIFF_WIDTH), BF16),
        scratch_shapes=[pltpu.VMEM((2 * n_near + 1, T, T), F32),
                        pltpu.VMEM((2, 1, T), F32), pltpu.VMEM((2, 1, T), F32),
                        pltpu.VMEM((2, DIFF_V_DIM, T), F32),
                        pltpu.VMEM((2 * (T // DIFF_QUERY_PANEL), T, DIFF_QUERY_PANEL), BF16),
                        pltpu.VMEM((2, 1, T), F32), pltpu.VMEM((2, 1, 1), F32),
                        pltpu.SMEM((1,), jnp.int32)],
        compiler_params=pltpu.CompilerParams(
            dimension_semantics=("arbitrary", "arbitrary"), vmem_limit_bytes=VMEM_LIMIT),
        name="diff_attn",
    )(tab_diff, qt_all, qt_all, proj, proj, vt_all, lam, gain_col)


def _dilated_kernel(tab_ref, q_ref, kp_ref, km_ref, kn_ref, vp_ref, vm_ref, vn_ref, o_ref, lse_ref,
                    bias_ref, kx_ref, vx_ref, *, R, B, half, dilation, n_chunks):
    c = pl.program_id(0)
    n = pl.program_id(1)
    W = B + 2 * half
    nblk = R // B

    @pl.when(jnp.logical_and(c == 0, n == 0))
    def _():
        rows = 8
        col = lax.broadcasted_iota(jnp.int32, (rows, W), 1)
        row = lax.broadcasted_iota(jnp.int32, (rows, W), 0)
        for hh in range(N_DIL_HEADS):
            def fill(r, carry, hh=hh):
                r0 = pl.multiple_of(r * rows, rows)
                off = col - half - (row + r0)
                bias = _bias_from_rel(off * dilation, tab_ref, N_DIFF_HEADS + hh)
                base = jnp.where(jnp.abs(off) <= half, bias, NEG_INF * LOG2E)
                bias_ref[hh, 1, pl.ds(r0, rows), :] = base
                bias_ref[hh, 0, pl.ds(r0, rows), :] = jnp.where(col >= half, base, NEG_INF * LOG2E)
                bias_ref[hh, 2, pl.ds(r0, rows), :] = jnp.where(col < B + half, base, NEG_INF * LOG2E)
                return carry
            lax.fori_loop(0, B // rows, fill, 0)

    kx_ref[0:half, :] = kp_ref[...]
    kx_ref[half:half + R, :] = km_ref[...]
    kx_ref[half + R:, :] = kn_ref[...]
    vx_ref[0:half, :] = vp_ref[...]
    vx_ref[half:half + R, :] = vm_ref[...]
    vx_ref[half + R:, :] = vn_ref[...]

    def chain(hh, b):
        c0 = pl.multiple_of(hh * HEAD_DIM, HEAD_DIM)
        r0 = b * B
        var = 1
        if b == 0:
            var = jnp.where(n == 0, 0, var)
        if b == nblk - 1:
            var = jnp.where(n == n_chunks - 1, 2, var)
        q = q_ref[pl.ds(r0, B), pl.ds(c0, HEAD_DIM)]
        k = kx_ref[pl.ds(r0, W), pl.ds(c0, HEAD_DIM)]
        v = vx_ref[pl.ds(r0, W), pl.ds(c0, HEAD_DIM)]
        s = lax.dot_general(q, k, (((1,), (1,)), ((), ())), preferred_element_type=F32)
        s = s + bias_ref[hh, var]
        m = jnp.max(s, axis=-1, keepdims=True)
        e = jnp.exp2(s - m)
        den = jnp.sum(e, axis=-1, keepdims=True)
        o = jnp.dot(e.astype(BF16), v, preferred_element_type=F32) / den
        o_ref[pl.ds(r0, B), pl.ds(c0, HEAD_DIM)] = o.astype(o_ref.dtype)
        lse = m + jnp.log2(den)
        lse_ref[pl.ds(r0, B), pl.ds(c0, HEAD_DIM)] = jnp.broadcast_to(lse, (B, HEAD_DIM))

    @pl.loop(0, N_DIL_HEADS // DIL_HEAD_UNROLL)
    def _(hg):
        for u in range(DIL_HEAD_UNROLL):
            for b in range(nblk):
                chain(hg * DIL_HEAD_UNROLL + u, b)


def _dilated_pattern(tab, qkv, col_blk0, window, dilation, *, B=256):
    _, L, _ = qkv.shape
    R = min(DIL_CHUNK, L)
    half = window // (2 * dilation)
    assert L % R == 0 and R % B == 0 and half % BF16_SUBLANES == 0 and R % half == 0
    n_chunks = L // R
    q_blk, k_blk, v_blk = col_blk0, col_blk0 + 1, col_blk0 + 2
    hb = R // half
    n_hblk = L // half

    def main(blk):
        return pl.BlockSpec((None, R, DIL_WIDTH), lambda c, n: (c, n, blk))

    def prev(blk):
        return pl.BlockSpec((None, half, DIL_WIDTH), lambda c, n: (c, jnp.maximum(n * hb - 1, 0), blk))

    def nxt(blk):
        return pl.BlockSpec((None, half, DIL_WIDTH), lambda c, n: (c, jnp.minimum((n + 1) * hb, n_hblk - 1), blk))

    kern = functools.partial(_dilated_kernel, R=R, B=B, half=half, dilation=dilation, n_chunks=n_chunks)
    out_spec = pl.BlockSpec((None, R, DIL_WIDTH), lambda c, n: (c, n, 0))
    return pl.pallas_call(
        kern,
        grid=(dilation, n_chunks),
        in_specs=[pl.BlockSpec(memory_space=pltpu.SMEM),
                  main(q_blk), prev(k_blk), main(k_blk), nxt(k_blk), prev(v_blk), main(v_blk), nxt(v_blk)],
        out_specs=[out_spec, out_spec],
        out_shape=[jax.ShapeDtypeStruct((dilation, L, DIL_WIDTH), BF16),
                   jax.ShapeDtypeStruct((dilation, L, DIL_WIDTH), F32)],
        scratch_shapes=[pltpu.VMEM((N_DIL_HEADS, 3, B, B + 2 * half), F32),
                        pltpu.VMEM((R + 2 * half, DIL_WIDTH), BF16),
                        pltpu.VMEM((R + 2 * half, DIL_WIDTH), BF16)],
        compiler_params=pltpu.CompilerParams(
            dimension_semantics=("arbitrary", "arbitrary"), vmem_limit_bytes=VMEM_LIMIT),
        name=f"dilated_d{dilation}",
    )(tab, qkv, qkv, qkv, qkv, qkv, qkv, qkv)


def _combine_kernel(*refs, dilations):
    n = len(dilations)
    o_refs, l_refs = refs[:n], refs[n:2 * n]
    g_ref, out_ref = refs[2 * n], refs[2 * n + 1]
    scratch = refs[2 * n + 2:]
    tm = out_ref.shape[0]

    for hh in range(N_DIL_HEADS):
        sl = slice(hh * HEAD_DIM, (hh + 1) * HEAD_DIM)
        outs, lses = [], []
        si = 0
        for o_ref, l_ref, dil in zip(o_refs, l_refs, dilations):
            if dil == 1:
                outs.append(o_ref[0, :, sl].astype(F32))
                lses.append(l_ref[0, :, sl])
                continue
            os_ref, ls_ref = scratch[si], scratch[si + 1]
            si += 2
            for c in range(dil):
                os_ref[hh, pl.ds(c, tm // dil, stride=dil), :] = o_ref[c, :, sl].astype(F32)
                ls_ref[hh, pl.ds(c, tm // dil, stride=dil), :] = l_ref[c, :, sl]
            outs.append(os_ref[hh])
            lses.append(ls_ref[hh])

        m = functools.reduce(jnp.maximum, lses)
        ws = [jnp.exp2(l - m) for l in lses]
        tot = functools.reduce(lambda a, b: a + b, ws)
        oh = functools.reduce(lambda a, b: a + b, [(w / tot) * op for w, op in zip(ws, outs)])
        ms = jnp.mean(oh * oh, axis=-1, keepdims=True)
        out_ref[:, sl] = (oh * lax.rsqrt(ms + NORM_EPS) * g_ref[:, sl]).astype(out_ref.dtype)


def _combine(outs, lses, gain, dilations, *, tm=512):
    S = outs[0].shape[0] * outs[0].shape[1]
    specs = [pl.BlockSpec((d, tm // d, DIL_WIDTH), lambda i: (0, i, 0)) for d in dilations]
    n_scr = sum(1 for d in dilations if d != 1)
    return pl.pallas_call(
        functools.partial(_combine_kernel, dilations=dilations),
        grid=(S // tm,),
        in_specs=specs + specs + [pl.BlockSpec((1, DIL_WIDTH), lambda i: (0, 0))],
        scratch_shapes=[pltpu.VMEM((N_DIL_HEADS, tm, HEAD_DIM), F32)] * (2 * n_scr),
        out_specs=pl.BlockSpec((tm, DIL_WIDTH), lambda i: (i, 0)),
        out_shape=jax.ShapeDtypeStruct((S, DIL_WIDTH), BF16),
        compiler_params=pltpu.CompilerParams(
            dimension_semantics=("arbitrary",), vmem_limit_bytes=VMEM_LIMIT),
        name="dilated_combine",
    )(*outs, *lses, gain)


def _outproj_kernel(od_ref, ol_ref, wd_ref, wl_ref, x_ref, g_ref, x1_ref, h2_ref):
    acc = jnp.dot(od_ref[...], wd_ref[...], preferred_element_type=F32)
    acc = acc + jnp.dot(ol_ref[...], wl_ref[...], preferred_element_type=F32)
    x1 = x_ref[...] + acc
    x1_ref[...] = x1
    ms = jnp.mean(x1 * x1, axis=-1, keepdims=True)
    h2_ref[...] = (x1 * lax.rsqrt(ms + NORM_EPS) * g_ref[...]).astype(h2_ref.dtype)


def _out_proj(o_d, o_l, w_bf, x2, gain, *, tm=512):
    S, D = x2.shape
    return pl.pallas_call(
        _outproj_kernel,
        grid=(S // tm,),
        in_specs=[
            pl.BlockSpec((tm, DIFF_WIDTH), lambda i: (i, 0)),
            pl.BlockSpec((tm, DIL_WIDTH), lambda i: (i, 0)),
            pl.BlockSpec((DIFF_WIDTH, D), lambda i: (0, 0)),
            pl.BlockSpec((DIL_WIDTH, D), lambda i: (1, 0)),
            pl.BlockSpec((tm, D), lambda i: (i, 0)),
            pl.BlockSpec((1, D), lambda i: (0, 0)),
        ],
        out_specs=[pl.BlockSpec((tm, D), lambda i: (i, 0)), pl.BlockSpec((tm, D), lambda i: (i, 0))],
        out_shape=[jax.ShapeDtypeStruct((S, D), F32), jax.ShapeDtypeStruct((S, D), BF16)],
        compiler_params=pltpu.CompilerParams(
            dimension_semantics=("arbitrary",), vmem_limit_bytes=VMEM_LIMIT),
        name="out_proj",
    )(o_d, o_l, w_bf, w_bf, x2, gain)


def _ffn_up_kernel(hm_ref, hp_ref, hn_ref, wg_ref, wu_ref, cw_ref, cb_ref, o_ref, lhs_ref, *, tm, n_row_tiles):
    i = pl.program_id(0)
    j = pl.program_id(1)
    halo = BF16_SUBLANES

    @pl.when(j == 0)
    def _():
        lhs_ref[0:halo, :] = jnp.where(i == 0, jnp.zeros_like(hp_ref[...]), hp_ref[...])
        lhs_ref[halo:halo + tm, :] = hm_ref[...]
        lhs_ref[halo + tm:, :] = jnp.where(i == n_row_tiles - 1, jnp.zeros_like(hn_ref[...]), hn_ref[...])

    g = jnp.dot(lhs_ref[...], wg_ref[...], preferred_element_type=F32)
    u = jnp.dot(lhs_ref[halo:halo + tm, :], wu_ref[...], preferred_element_type=F32)
    rows = tm + 2 * halo
    g_prev = pltpu.roll(g, 1, axis=0)
    g_next = pltpu.roll(g, rows - 1, axis=0)
    y = cw_ref[0:1, :] * g_prev + cw_ref[1:2, :] * g + cw_ref[2:3, :] * g_next + cb_ref[...]
    y = y[halo:halo + tm, :]
    act = y * (1.0 / (1.0 + jnp.exp(-y))) * u
    o_ref[...] = act.astype(o_ref.dtype)


def _ffn_up(h2, w_bf, conv_w, conv_b, *, tm=1024, tn=512):
    S, D = h2.shape
    d_ff = conv_w.shape[1]
    assert d_ff % tn == 0
    nj = d_ff // tn
    ni = S // tm
    hb = tm // BF16_SUBLANES
    n_hblk = S // BF16_SUBLANES
    kern = functools.partial(_ffn_up_kernel, tm=tm, n_row_tiles=ni)
    return pl.pallas_call(
        kern,
        grid=(ni, nj),
        in_specs=[
            pl.BlockSpec((tm, D), lambda i, j: (i, 0)),
            pl.BlockSpec((BF16_SUBLANES, D), lambda i, j: (jnp.maximum(i * hb - 1, 0), 0)),
            pl.BlockSpec((BF16_SUBLANES, D), lambda i, j: (jnp.minimum((i + 1) * hb, n_hblk - 1), 0)),
            pl.BlockSpec((D, tn), lambda i, j: (0, j)),
            pl.BlockSpec((D, tn), lambda i, j: (0, nj + j)),
            pl.BlockSpec((3, tn), lambda i, j: (0, j)),
            pl.BlockSpec((1, tn), lambda i, j: (0, j)),
        ],
        out_specs=pl.BlockSpec((tm, tn), lambda i, j: (i, j)),
        out_shape=jax.ShapeDtypeStruct((S, d_ff), BF16),
        scratch_shapes=[pltpu.VMEM((tm + 2 * BF16_SUBLANES, D), BF16)],
        compiler_params=pltpu.CompilerParams(
            dimension_semantics=("arbitrary", "arbitrary"), vmem_limit_bytes=VMEM_LIMIT),
        name="ffn_up",
    )(h2, h2, h2, w_bf, w_bf, conv_w, conv_b)


def _ffn_down_kernel(a_ref, w_ref, x1_ref, g_ref, o_ref, acc_ref, *, n_k):
    k = pl.program_id(1)

    @pl.when(k == 0)
    def _():
        acc_ref[...] = x1_ref[...]

    acc_ref[...] += jnp.dot(a_ref[...], w_ref[...], preferred_element_type=F32)

    @pl.when(k == n_k - 1)
    def _():
        y = acc_ref[...]
        ms = jnp.mean(y * y, axis=-1, keepdims=True)
        o_ref[...] = y * lax.rsqrt(ms + NORM_EPS) * g_ref[...]


def _ffn_down(act, w_bf, x1, gain, *, tm=512, tk=1408):
    S, d_ff = act.shape
    D = x1.shape[1]
    n_k = d_ff // tk
    kern = functools.partial(_ffn_down_kernel, n_k=n_k)
    return pl.pallas_call(
        kern,
        grid=(S // tm, n_k),
        in_specs=[
            pl.BlockSpec((tm, tk), lambda i, k: (i, k)),
            pl.BlockSpec((tk, D), lambda i, k: (k, 0)),
            pl.BlockSpec((tm, D), lambda i, k: (i, 0)),
            pl.BlockSpec((1, D), lambda i, k: (0, 0)),
        ],
        out_specs=pl.BlockSpec((tm, D), lambda i, k: (i, 0)),
        out_shape=jax.ShapeDtypeStruct((S, D), F32),
        scratch_shapes=[pltpu.VMEM((tm, D), F32)],
        compiler_params=pltpu.CompilerParams(
            dimension_semantics=("arbitrary", "arbitrary"), vmem_limit_bytes=VMEM_LIMIT),
        name="ffn_down",
    )(act, w_bf, x1, gain)


def kernel(x, norm1_gain, w_in, rel_bias_table, lambda_q1, lambda_k1, lambda_q2, lambda_k2,
           diff_subln_gain, dil_out_gain, w_out, norm2_gain, w_gate_up, conv_w, conv_b, w_down, final_gain):
    B, S, D = x.shape
    assert B == 1 and w_in.shape[0] == 1
    x2 = x.reshape(S, D)
    n_cols = w_in.shape[2]

    qscale = LOG2E / math.sqrt(HEAD_DIM)
    col = np.arange(n_cols)
    dil_q0 = 2 * DIFF_QK_COLS + DIFF_WIDTH
    is_q = (col < DIFF_QK_COLS) | ((col >= dil_q0) & (col < dil_q0 + DIL_WIDTH))
    colscale = jnp.asarray(np.where(is_q, qscale, 1.0).astype(np.float32)).reshape(1, n_cols)
    tab = rel_bias_table.astype(F32) * LOG2E

    regroup = tuple(d for _, d in DILATED_PATTERNS if d != 1)
    proj, qt_all, vt_all, *cls = _in_proj(x2, norm1_gain.reshape(1, D), w_in[0].astype(BF16), colscale, regroup)
    cls_by_dil = dict(zip(regroup, cls))

    lam = _lambda(lambda_q1.reshape(1, -1), lambda_k1.reshape(1, -1),
                  lambda_q2.reshape(1, -1), lambda_k2.reshape(1, -1))
    o_d = _diff_attention(tab, proj, qt_all, vt_all, lam, diff_subln_gain.reshape(-1, 1))

    outs, lses = [], []
    dil_blk0 = (2 * DIFF_QK_COLS + DIFF_WIDTH) // DIL_WIDTH
    for window, dilation in DILATED_PATTERNS:
        if dilation == 1:
            o_p, lse_p = _dilated_pattern(tab, proj.reshape(1, S, n_cols), dil_blk0, window, dilation)
        else:
            o_p, lse_p = _dilated_pattern(tab, cls_by_dil[dilation], 0, window, dilation)
        outs.append(o_p)
        lses.append(lse_p)
    o_l = _combine(outs, lses, dil_out_gain.reshape(1, -1), tuple(d for _, d in DILATED_PATTERNS))

    x1, h2 = _out_proj(o_d, o_l, w_out[0].astype(BF16), x2, norm2_gain.reshape(1, D))
    act = _ffn_up(h2, w_gate_up[0].astype(BF16), conv_w[0], conv_b.reshape(1, -1))
    out = _ffn_down(act, w_down[0].astype(BF16), x1, final_gain.reshape(1, D))
    return out.reshape(B, S, D)
```

```python
import functools
import math

import numpy as np
import jax
import jax.numpy as jnp
from jax import lax
from jax.experimental import pallas as pl
from jax.experimental.pallas import tpu as pltpu

F32 = jnp.float32
BF16 = jnp.bfloat16

HEAD_DIM = 128
N_DIFF_HEADS = 4
DIFF_V_DIM = 2 * HEAD_DIM
N_DIL_HEADS = 8
DIFF_QK_COLS = N_DIFF_HEADS * 2 * HEAD_DIM
DIFF_WIDTH = N_DIFF_HEADS * DIFF_V_DIM
DIL_WIDTH = N_DIL_HEADS * HEAD_DIM
DILATED_PATTERNS = ((128, 1), (512, 4), (2048, 16))
N_REL_BUCKETS = 32
REL_MAX_DISTANCE = 1024
NORM_EPS = 1e-6
SUBLN_EPS = 1e-5
NEG_INF = -1e30
LOG2E = math.log2(math.e)
LAM_INIT = 0.8 - 0.6 * math.exp(-0.3 * 0)

LANES = 128
BF16_SUBLANES = 16
DIFF_TILE = 1024
DIFF_QUERY_PANEL = 256
SHIFT_MARGIN = 1.0 + 2.0 ** -8
MIN_DENOMINATOR = 2.0 ** -60
DIL_CHUNK = 1024
DIL_HEAD_UNROLL = 4
VMEM_LIMIT = 56 * 1024 * 1024


def _bucket_breaks():
    nb = N_REL_BUCKETS // 2
    max_exact = nb // 2
    rel = np.arange(-2 * REL_MAX_DISTANCE, 2 * REL_MAX_DISTANCE + 1)
    n = np.abs(rel)
    pos = np.log(np.maximum(n, 1) / max_exact) / math.log(REL_MAX_DISTANCE / max_exact) * (nb - max_exact)
    large = np.minimum(max_exact + np.floor(pos).astype(np.int64), nb - 1)
    bucket = np.where(rel > 0, nb, 0) + np.where(n < max_exact, n, large)
    breaks = [(int(rel[i]), int(bucket[i])) for i in range(1, len(rel)) if bucket[i] != bucket[i - 1]]
    return int(bucket[0]), breaks


FIRST_BUCKET, BUCKET_BREAKS = _bucket_breaks()
LAST_BUCKET = BUCKET_BREAKS[-1][1]
FAR_DIST = max(-BUCKET_BREAKS[0][0] + 1, BUCKET_BREAKS[-1][0])


def _bias_from_rel(rel, tab_ref, col):
    val = jnp.full(rel.shape, tab_ref[FIRST_BUCKET, col], F32)
    for thr, b in BUCKET_BREAKS:
        val = jnp.where(rel >= thr, tab_ref[b, col], val)
    return val


def _inproj_kernel(x_ref, g_ref, w_ref, cs_ref, o_ref, qt_ref, vt_ref, *rest, q_tile, v_tile, dil_tile0,
                   dilations):
    cls_refs = rest[:len(dilations)]
    h_ref, acc_ref = rest[len(dilations):]
    j = pl.program_id(1)
    tm = x_ref.shape[0]

    @pl.when(j == 0)
    def _():
        x = x_ref[...]
        ms = jnp.mean(x * x, axis=-1, keepdims=True)
        h_ref[...] = (x * lax.rsqrt(ms + NORM_EPS) * g_ref[...]).astype(BF16)

    acc = jnp.dot(h_ref[...], w_ref[...], preferred_element_type=F32) * cs_ref[...]
    o_ref[...] = acc.astype(o_ref.dtype)

    @pl.when(j == q_tile)
    def _():
        qt_ref[...] = acc.T.astype(qt_ref.dtype)

    @pl.when(j == v_tile)
    def _():
        vt_ref[...] = acc.T.astype(vt_ref.dtype)

    @pl.when(j >= dil_tile0)
    def _():
        for cb in range(acc_ref.shape[0]):
            sl = slice(cb * LANES, (cb + 1) * LANES)
            acc_ref[cb] = acc[:, sl]
            for cls_ref, dil in zip(cls_refs, dilations):
                for c in range(dil):
                    cls_ref[c, :, sl] = acc_ref[cb, pl.ds(c, tm // dil, stride=dil), :].astype(cls_ref.dtype)


def _in_proj(x2, gain, w_bf, colscale, dilations, *, tm=512, tn=1024):
    S, D = x2.shape
    N = w_bf.shape[1]
    assert DIFF_QK_COLS == tn and DIFF_WIDTH == tn and DIL_WIDTH == tn, "outputs assume one column tile each"
    dil_tile0 = (2 * DIFF_QK_COLS + DIFF_WIDTH) // tn
    n_dil_tiles = N // tn - dil_tile0
    kern = functools.partial(_inproj_kernel, q_tile=0, v_tile=2 * DIFF_QK_COLS // tn, dil_tile0=dil_tile0,
                             dilations=dilations)
    cls_specs = [pl.BlockSpec((d, tm // d, tn), lambda i, j: (0, i, jnp.maximum(j - dil_tile0, 0)))
                 for d in dilations]
    cls_shapes = [jax.ShapeDtypeStruct((d, S // d, n_dil_tiles * tn), BF16) for d in dilations]
    return pl.pallas_call(
        kern,
        grid=(S // tm, N // tn),
        in_specs=[
            pl.BlockSpec((tm, D), lambda i, j: (i, 0)),
            pl.BlockSpec((1, D), lambda i, j: (0, 0)),
            pl.BlockSpec((D, tn), lambda i, j: (0, j)),
            pl.BlockSpec((1, tn), lambda i, j: (0, j)),
        ],
        out_specs=[
            pl.BlockSpec((tm, tn), lambda i, j: (i, j)),
            pl.BlockSpec((tn, tm), lambda i, j: (0, i)),
            pl.BlockSpec((tn, tm), lambda i, j: (0, i)),
        ] + cls_specs,
        out_shape=[
            jax.ShapeDtypeStruct((S, N), BF16),
            jax.ShapeDtypeStruct((DIFF_QK_COLS, S), BF16),
            jax.ShapeDtypeStruct((DIFF_WIDTH, S), BF16),
        ] + cls_shapes,
        scratch_shapes=[pltpu.VMEM((tm, D), BF16), pltpu.VMEM((tn // LANES, tm, LANES), F32)],
        compiler_params=pltpu.CompilerParams(
            dimension_semantics=("arbitrary", "arbitrary"), vmem_limit_bytes=VMEM_LIMIT),
        name="in_proj",
    )(x2, gain, w_bf, colscale)


def _diff_attn_kernel(tab_ref, q1t_ref, q2t_ref, k1_ref, k2_ref, vt_ref, lam_ref, gain_ref, o_ref,
                      bias_ref, m_ref, l_ref, acc_ref, p_ref, shift_ref, knorm_ref, prev_ref,
                      *, T, QP, n_near, n_tiles):
    h = pl.program_id(0)
    qi = pl.program_id(1)
    n_chains = 2 * (T // QP)
    qts = (q1t_ref, q2t_ref)
    ks = (k1_ref, k2_ref)

    def chain_of(i):
        qp = i // 2
        return i % 2, slice(qp * QP, (qp + 1) * QP)

    @pl.when(qi == 0)
    def _():
        for mi in range(2):
            def knorm(t, best, mi=mi):
                k = ks[mi][pl.ds(pl.multiple_of(t * T, T), T), :].astype(F32)
                return jnp.maximum(best, jnp.max(jnp.sum(k * k, axis=1, keepdims=True), axis=0, keepdims=True))
            knorm_ref[mi] = lax.fori_loop(0, n_tiles, knorm, jnp.zeros((1, 1), F32))

        x = lax.broadcasted_iota(jnp.int32, (8, 2 * T), 1)
        x = jnp.where(x < T, x, x - 2 * T)
        for di, d in enumerate(range(-n_near, n_near + 1)):
            g = _bias_from_rel(d * T - x, tab_ref, h)
            base = jnp.broadcast_to(g[0:1, :], (LANES, 2 * T))
            for rb in range(T // LANES):
                blk = pltpu.roll(base, rb * LANES, 1, stride=1, stride_axis=0)
                bias_ref[di, rb * LANES:(rb + 1) * LANES, :] = blk[:, :T]

    c_left = tab_ref[FIRST_BUCKET, h]
    c_right = tab_ref[LAST_BUCKET, h]

    bias_max = tab_ref[0, h]
    for b in range(1, N_REL_BUCKETS):
        bias_max = jnp.maximum(bias_max, tab_ref[b, h])

    for mi in range(2):
        q = qts[mi][...].astype(F32)
        qnorm2 = jnp.sum(q * q, axis=0, keepdims=True)
        shift_ref[mi] = jnp.sqrt(qnorm2 * knorm_ref[mi]) * SHIFT_MARGIN + bias_max
    l_ref[...] = jnp.zeros(l_ref.shape, F32)
    acc_ref[...] = jnp.zeros(acc_ref.shape, F32)
    p_ref[n_chains - 1] = jnp.zeros((T, QP), BF16)
    prev_ref[0] = 0

    def score_stage(i, k0, bias_di, bias_const):
        mi, qs = chain_of(i)
        s = jnp.dot(ks[mi][pl.ds(k0, T), :], qts[mi][:, qs], preferred_element_type=F32)
        if bias_di is not None:
            p = jnp.exp2(s + bias_ref[bias_di, :, qs] - shift_ref[mi, :, qs])
        else:
            p = jnp.exp2(s - (shift_ref[mi, :, qs] - bias_const))
        l_ref[mi, :, qs] += jnp.sum(p, axis=0, keepdims=True)
        p_ref[i] = p.astype(BF16)

    def value_stage(i, k0):
        mi, qs = chain_of(i)
        acc_ref[mi, :, qs] += jnp.dot(vt_ref[:, pl.ds(k0, T)], p_ref[i], preferred_element_type=F32)

    def tile(kt, bias_di, bias_const):
        k0 = pl.multiple_of(kt * T, T)
        pk0 = pl.multiple_of(prev_ref[0] * T, T)
        for i in range(n_chains):
            value_stage((i - 1) % n_chains, pk0 if i < 1 else k0)
            score_stage(i, k0, bias_di, bias_const)
        prev_ref[0] = kt

    lo = jnp.maximum(qi - n_near, 0)
    hi = jnp.minimum(qi + n_near + 1, n_tiles)

    @pl.loop(0, lo)
    def _(kt):
        tile(kt, None, c_left)

    for di, d in enumerate(range(-n_near, n_near + 1)):
        kt = qi + d

        @pl.when(jnp.logical_and(kt >= 0, kt < n_tiles))
        def _(di=di, kt=kt):
            tile(kt, di, None)

    @pl.loop(hi, n_tiles)
    def _(kt):
        tile(kt, None, c_right)

    value_stage(n_chains - 1, pl.multiple_of(prev_ref[0] * T, T))

    @pl.when(jnp.min(l_ref[...]) < MIN_DENOMINATOR)
    def _():
        m_ref[...] = jnp.full(m_ref.shape, -jnp.inf, F32)
        l_ref[...] = jnp.zeros(l_ref.shape, F32)
        acc_ref[...] = jnp.zeros(acc_ref.shape, F32)

        @pl.loop(0, n_tiles)
        def _(kt):
            k0 = pl.multiple_of(kt * T, T)
            d = kt - qi
            near = jnp.abs(d) <= n_near
            di = jnp.clip(d + n_near, 0, 2 * n_near)
            c_far = jnp.where(d < 0, c_left, c_right)
            for i in range(n_chains):
                mi, qs = chain_of(i)
                s = jnp.dot(ks[mi][pl.ds(k0, T), :], qts[mi][:, qs], preferred_element_type=F32)
                s = s + jnp.where(near, bias_ref[di, :, qs], c_far)
                m = m_ref[mi, :, qs]
                m_new = jnp.maximum(m, jnp.max(s, axis=0, keepdims=True))
                alpha = jnp.exp2(m - m_new)
                p = jnp.exp2(s - m_new)
                m_ref[mi, :, qs] = m_new
                l_ref[mi, :, qs] = alpha * l_ref[mi, :, qs] + jnp.sum(p, axis=0, keepdims=True)
                pv = jnp.dot(vt_ref[:, pl.ds(k0, T)], p.astype(BF16), preferred_element_type=F32)
                acc_ref[mi, :, qs] = alpha * acc_ref[mi, :, qs] + pv

    lam = lam_ref[0, 0]
    o = acc_ref[0] / l_ref[0] - lam * (acc_ref[1] / l_ref[1])
    ms = jnp.mean(o * o, axis=0, keepdims=True)
    o = o * lax.rsqrt(ms + SUBLN_EPS) * (gain_ref[...] * (1.0 - LAM_INIT))
    o_ref[...] = o.T.astype(o_ref.dtype)


def _lambda_kernel(q1_ref, k1_ref, q2_ref, k2_ref, o_ref):
    a = jnp.sum(q1_ref[...] * k1_ref[...], axis=-1, keepdims=True)
    b = jnp.sum(q2_ref[...] * k2_ref[...], axis=-1, keepdims=True)
    o_ref[...] = jnp.exp(a) - jnp.exp(b) + LAM_INIT


def _lambda(lq1, lk1, lq2, lk2):
    return pl.pallas_call(
        _lambda_kernel, out_shape=jax.ShapeDtypeStruct((1, 1), F32), name="diff_lambda",
    )(lq1, lk1, lq2, lk2)


def _diff_attention(tab_diff, proj, qt_all, vt_all, lam, gain_col):
    S = proj.shape[0]
    T = DIFF_TILE
    n_tiles = S // T
    n_near = -(-(FAR_DIST - 1) // T)
    kern = functools.partial(_diff_attn_kernel, T=T, QP=DIFF_QUERY_PANEL, n_near=n_near, n_tiles=n_tiles)
    k_block0 = DIFF_QK_COLS // HEAD_DIM
    smem = pl.BlockSpec(memory_space=pltpu.SMEM)
    return pl.pallas_call(
        kern,
        grid=(N_DIFF_HEADS, n_tiles),
        in_specs=[
            smem,
            pl.BlockSpec((HEAD_DIM, T), lambda h, i: (2 * h, i)),
            pl.BlockSpec((HEAD_DIM, T), lambda h, i: (2 * h + 1, i)),
            pl.BlockSpec((S, HEAD_DIM), lambda h, i: (0, k_block0 + 2 * h), pipeline_mode=pl.Buffered(1)),
            pl.BlockSpec((S, HEAD_DIM), lambda h, i: (0, k_block0 + 2 * h + 1), pipeline_mode=pl.Buffered(1)),
            pl.BlockSpec((DIFF_V_DIM, S), lambda h, i: (h, 0), pipeline_mode=pl.Buffered(1)),
            smem,
            pl.BlockSpec((DIFF_V_DIM, 1), lambda h, i: (0, 0)),
        ],
        out_specs=pl.BlockSpec((T, DIFF_V_DIM), lambda h, i: (i, h)),
        out_shape=jax.ShapeDtypeStruct((S, DIFF_WIDTH), BF16),
        scratch_shapes=[pltpu.VMEM((2 * n_near + 1, T, T), F32),
                        pltpu.VMEM((2, 1, T), F32), pltpu.VMEM((2, 1, T), F32),
                        pltpu.VMEM((2, DIFF_V_DIM, T), F32),
                        pltpu.VMEM((2 * (T // DIFF_QUERY_PANEL), T, DIFF_QUERY_PANEL), BF16),
                        pltpu.VMEM((2, 1, T), F32), pltpu.VMEM((2, 1, 1), F32),
                        pltpu.SMEM((1,), jnp.int32)],
        compiler_params=pltpu.CompilerParams(
            dimension_semantics=("arbitrary", "arbitrary"), vmem_limit_bytes=VMEM_LIMIT),
        name="diff_attn",
    )(tab_diff, qt_all, qt_all, proj, proj, vt_all, lam, gain_col)


def _dilated_kernel(tab_ref, q_ref, kp_ref, km_ref, kn_ref, vp_ref, vm_ref, vn_ref, o_ref, lse_ref,
                    bias_ref, kx_ref, vx_ref, *, R, B, half, dilation, n_chunks):
    c = pl.program_id(0)
    n = pl.program_id(1)
    W = B + 2 * half
    nblk = R // B

    @pl.when(jnp.logical_and(c == 0, n == 0))
    def _():
        rows = 8
        col = lax.broadcasted_iota(jnp.int32, (rows, W), 1)
        row = lax.broadcasted_iota(jnp.int32, (rows, W), 0)
        for hh in range(N_DIL_HEADS):
            def fill(r, carry, hh=hh):
                r0 = pl.multiple_of(r * rows, rows)
                off = col - half - (row + r0)
                bias = _bias_from_rel(off * dilation, tab_ref, N_DIFF_HEADS + hh)
                base = jnp.where(jnp.abs(off) <= half, bias, NEG_INF * LOG2E)
                bias_ref[hh, 1, pl.ds(r0, rows), :] = base
                bias_ref[hh, 0, pl.ds(r0, rows), :] = jnp.where(col >= half, base, NEG_INF * LOG2E)
                bias_ref[hh, 2, pl.ds(r0, rows), :] = jnp.where(col < B + half, base, NEG_INF * LOG2E)
                return carry
            lax.fori_loop(0, B // rows, fill, 0)

    kx_ref[0:half, :] = kp_ref[...]
    kx_ref[half:half + R, :] = km_ref[...]
    kx_ref[half + R:, :] = kn_ref[...]
    vx_ref[0:half, :] = vp_ref[...]
    vx_ref[half:half + R, :] = vm_ref[...]
    vx_ref[half + R:, :] = vn_ref[...]

    def chain(hh, b):
        c0 = pl.multiple_of(hh * HEAD_DIM, HEAD_DIM)
        r0 = b * B
        var = 1
        if b == 0:
            var = jnp.where(n == 0, 0, var)
        if b == nblk - 1:
            var = jnp.where(n == n_chunks - 1, 2, var)
        q = q_ref[pl.ds(r0, B), pl.ds(c0, HEAD_DIM)]
        k = kx_ref[pl.ds(r0, W), pl.ds(c0, HEAD_DIM)]
        v = vx_ref[pl.ds(r0, W), pl.ds(c0, HEAD_DIM)]
        s = lax.dot_general(q, k, (((1,), (1,)), ((), ())), preferred_element_type=F32)
        s = s + bias_ref[hh, var]
        m = jnp.max(s, axis=-1, keepdims=True)
        e = jnp.exp2(s - m)
        den = jnp.sum(e, axis=-1, keepdims=True)
        o = jnp.dot(e.astype(BF16), v, preferred_element_type=F32) / den
        o_ref[pl.ds(r0, B), pl.ds(c0, HEAD_DIM)] = o.astype(o_ref.dtype)
        lse = m + jnp.log2(den)
        lse_ref[pl.ds(r0, B), pl.ds(c0, HEAD_DIM)] = jnp.broadcast_to(lse, (B, HEAD_DIM))

    @pl.loop(0, N_DIL_HEADS // DIL_HEAD_UNROLL)
    def _(hg):
        for u in range(DIL_HEAD_UNROLL):
            for b in range(nblk):
                chain(hg * DIL_HEAD_UNROLL + u, b)


def _dilated_pattern(tab, qkv, col_blk0, window, dilation, *, B=256):
    _, L, _ = qkv.shape
    R = min(DIL_CHUNK, L)
    half = window // (2 * dilation)
    assert L % R == 0 and R % B == 0 and half % BF16_SUBLANES == 0 and R % half == 0
    n_chunks = L // R
    q_blk, k_blk, v_blk = col_blk0, col_blk0 + 1, col_blk0 + 2
    hb = R // half
    n_hblk = L // half

    def main(blk):
        return pl.BlockSpec((None, R, DIL_WIDTH), lambda c, n: (c, n, blk))

    def prev(blk):
        return pl.BlockSpec((None, half, DIL_WIDTH), lambda c, n: (c, jnp.maximum(n * hb - 1, 0), blk))

    def nxt(blk):
        return pl.BlockSpec((None, half, DIL_WIDTH), lambda c, n: (c, jnp.minimum((n + 1) * hb, n_hblk - 1), blk))

    kern = functools.partial(_dilated_kernel, R=R, B=B, half=half, dilation=dilation, n_chunks=n_chunks)
    out_spec = pl.BlockSpec((None, R, DIL_WIDTH), lambda c, n: (c, n, 0))
    return pl.pallas_call(
        kern,
        grid=(dilation, n_chunks),
        in_specs=[pl.BlockSpec(memory_space=pltpu.SMEM),
                  main(q_blk), prev(k_blk), main(k_blk), nxt(k_blk), prev(v_blk), main(v_blk), nxt(v_blk)],
        out_specs=[out_spec, out_spec],
        out_shape=[jax.ShapeDtypeStruct((dilation, L, DIL_WIDTH), BF16),
                   jax.ShapeDtypeStruct((dilation, L, DIL_WIDTH), F32)],
        scratch_shapes=[pltpu.VMEM((N_DIL_HEADS, 3, B, B + 2 * half), F32),
                        pltpu.VMEM((R + 2 * half, DIL_WIDTH), BF16),
                        pltpu.VMEM((R + 2 * half, DIL_WIDTH), BF16)],
        compiler_params=pltpu.CompilerParams(
            dimension_semantics=("arbitrary", "arbitrary"), vmem_limit_bytes=VMEM_LIMIT),
        name=f"dilated_d{dilation}",
    )(tab, qkv, qkv, qkv, qkv, qkv, qkv, qkv)


def _combine_kernel(*refs, dilations):
    n = len(dilations)
    o_refs, l_refs = refs[:n], refs[n:2 * n]
    g_ref, out_ref = refs[2 * n], refs[2 * n + 1]
    scratch = refs[2 * n + 2:]
    tm = out_ref.shape[0]

    for hh in range(N_DIL_HEADS):
        sl = slice(hh * HEAD_DIM, (hh + 1) * HEAD_DIM)
        outs, lses = [], []
        si = 0
        for o_ref, l_ref, dil in zip(o_refs, l_refs, dilations):
            if dil == 1:
                outs.append(o_ref[0, :, sl].astype(F32))
                lses.append(l_ref[0, :, sl])
                continue
            os_ref, ls_ref = scratch[si], scratch[si + 1]
            si += 2
            for c in range(dil):
                os_ref[hh, pl.ds(c, tm // dil, stride=dil), :] = o_ref[c, :, sl].astype(F32)
                ls_ref[hh, pl.ds(c, tm // dil, stride=dil), :] = l_ref[c, :, sl]
            outs.append(os_ref[hh])
            lses.append(ls_ref[hh])

        m = functools.reduce(jnp.maximum, lses)
        ws = [jnp.exp2(l - m) for l in lses]
        tot = functools.reduce(lambda a, b: a + b, ws)
        oh = functools.reduce(lambda a, b: a + b, [(w / tot) * op for w, op in zip(ws, outs)])
        ms = jnp.mean(oh * oh, axis=-1, keepdims=True)
        out_ref[:, sl] = (oh * lax.rsqrt(ms + NORM_EPS) * g_ref[:, sl]).astype(out_ref.dtype)


def _combine(outs, lses, gain, dilations, *, tm=512):
    S = outs[0].shape[0] * outs[0].shape[1]
    specs = [pl.BlockSpec((d, tm // d, DIL_WIDTH), lambda i: (0, i, 0)) for d in dilations]
    n_scr = sum(1 for d in dilations if d != 1)
    return pl.pallas_call(
        functools.partial(_combine_kernel, dilations=dilations),
        grid=(S // tm,),
        in_specs=specs + specs + [pl.BlockSpec((1, DIL_WIDTH), lambda i: (0, 0))],
        scratch_shapes=[pltpu.VMEM((N_DIL_HEADS, tm, HEAD_DIM), F32)] * (2 * n_scr),
        out_specs=pl.BlockSpec((tm, DIL_WIDTH), lambda i: (i, 0)),
        out_shape=jax.ShapeDtypeStruct((S, DIL_WIDTH), BF16),
        compiler_params=pltpu.CompilerParams(
            dimension_semantics=("arbitrary",), vmem_limit_bytes=VMEM_LIMIT),
        name="dilated_combine",
    )(*outs, *lses, gain)


def _outproj_kernel(od_ref, ol_ref, wd_ref, wl_ref, x_ref, g_ref, x1_ref, h2_ref):
    acc = jnp.dot(od_ref[...], wd_ref[...], preferred_element_type=F32)
    acc = acc + jnp.dot(ol_ref[...], wl_ref[...], preferred_element_type=F32)
    x1 = x_ref[...] + acc
    x1_ref[...] = x1
    ms = jnp.mean(x1 * x1, axis=-1, keepdims=True)
    h2_ref[...] = (x1 * lax.rsqrt(ms + NORM_EPS) * g_ref[...]).astype(h2_ref.dtype)


def _out_proj(o_d, o_l, w_bf, x2, gain, *, tm=512):
    S, D = x2.shape
    return pl.pallas_call(
        _outproj_kernel,
        grid=(S // tm,),
        in_specs=[
            pl.BlockSpec((tm, DIFF_WIDTH), lambda i: (i, 0)),
            pl.BlockSpec((tm, DIL_WIDTH), lambda i: (i, 0)),
            pl.BlockSpec((DIFF_WIDTH, D), lambda i: (0, 0)),
            pl.BlockSpec((DIL_WIDTH, D), lambda i: (1, 0)),
            pl.BlockSpec((tm, D), lambda i: (i, 0)),
            pl.BlockSpec((1, D), lambda i: (0, 0)),
        ],
        out_specs=[pl.BlockSpec((tm, D), lambda i: (i, 0)), pl.BlockSpec((tm, D), lambda i: (i, 0))],
        out_shape=[jax.ShapeDtypeStruct((S, D), F32), jax.ShapeDtypeStruct((S, D), BF16)],
        compiler_params=pltpu.CompilerParams(
            dimension_semantics=("arbitrary",), vmem_limit_bytes=VMEM_LIMIT),
        name="out_proj",
    )(o_d, o_l, w_bf, w_bf, x2, gain)


def _ffn_up_kernel(hm_ref, hp_ref, hn_ref, wg_ref, wu_ref, cw_ref, cb_ref, o_ref, lhs_ref, *, tm, n_row_tiles):
    i = pl.program_id(0)
    j = pl.program_id(1)
    halo = BF16_SUBLANES

    @pl.when(j == 0)
    def _():
        lhs_ref[0:halo, :] = jnp.where(i == 0, jnp.zeros_like(hp_ref[...]), hp_ref[...])
        lhs_ref[halo:halo + tm, :] = hm_ref[...]
        lhs_ref[halo + tm:, :] = jnp.where(i == n_row_tiles - 1, jnp.zeros_like(hn_ref[...]), hn_ref[...])

    g = jnp.dot(lhs_ref[...], wg_ref[...], preferred_element_type=F32)
    u = jnp.dot(lhs_ref[halo:halo + tm, :], wu_ref[...], preferred_element_type=F32)
    rows = tm + 2 * halo
    g_prev = pltpu.roll(g, 1, axis=0)
    g_next = pltpu.roll(g, rows - 1, axis=0)
    y = cw_ref[0:1, :] * g_prev + cw_ref[1:2, :] * g + cw_ref[2:3, :] * g_next + cb_ref[...]
    y = y[halo:halo + tm, :]
    act = y * (1.0 / (1.0 + jnp.exp(-y))) * u
    o_ref[...] = act.astype(o_ref.dtype)


def _ffn_up(h2, w_bf, conv_w, conv_b, *, tm=1024, tn=512):
    S, D = h2.shape
    d_ff = conv_w.shape[1]
    assert d_ff % tn == 0
    nj = d_ff // tn
    ni = S // tm
    hb = tm // BF16_SUBLANES
    n_hblk = S // BF16_SUBLANES
    kern = functools.partial(_ffn_up_kernel, tm=tm, n_row_tiles=ni)
    return pl.pallas_call(
        kern,
        grid=(ni, nj),
        in_specs=[
            pl.BlockSpec((tm, D), lambda i, j: (i, 0)),
            pl.BlockSpec((BF16_SUBLANES, D), lambda i, j: (jnp.maximum(i * hb - 1, 0), 0)),
            pl.BlockSpec((BF16_SUBLANES, D), lambda i, j: (jnp.minimum((i + 1) * hb, n_hblk - 1), 0)),
            pl.BlockSpec((D, tn), lambda i, j: (0, j)),
            pl.BlockSpec((D, tn), lambda i, j: (0, nj + j)),
            pl.BlockSpec((3, tn), lambda i, j: (0, j)),
            pl.BlockSpec((1, tn), lambda i, j: (0, j)),
        ],
        out_specs=pl.BlockSpec((tm, tn), lambda i, j: (i, j)),
        out_shape=jax.ShapeDtypeStruct((S, d_ff), BF16),
        scratch_shapes=[pltpu.VMEM((tm + 2 * BF16_SUBLANES, D), BF16)],
        compiler_params=pltpu.CompilerParams(
            dimension_semantics=("arbitrary", "arbitrary"), vmem_limit_bytes=VMEM_LIMIT),
        name="ffn_up",
    )(h2, h2, h2, w_bf, w_bf, conv_w, conv_b)


def _ffn_down_kernel(a_ref, w_ref, x1_ref, g_ref, o_ref, acc_ref, *, n_k):
    k = pl.program_id(1)

    @pl.when(k == 0)
    def _():
        acc_ref[...] = x1_ref[...]

    acc_ref[...] += jnp.dot(a_ref[...], w_ref[...], preferred_element_type=F32)

    @pl.when(k == n_k - 1)
    def _():
        y = acc_ref[...]
        ms = jnp.mean(y * y, axis=-1, keepdims=True)
        o_ref[...] = y * lax.rsqrt(ms + NORM_EPS) * g_ref[...]


def _ffn_down(act, w_bf, x1, gain, *, tm=512, tk=1408):
    S, d_ff = act.shape
    D = x1.shape[1]
    n_k = d_ff // tk
    kern = functools.partial(_ffn_down_kernel, n_k=n_k)
    return pl.pallas_call(
        kern,
        grid=(S // tm, n_k),
        in_specs=[
            pl.BlockSpec((tm, tk), lambda i, k: (i, k)),
            pl.BlockSpec((tk, D), lambda i, k: (k, 0)),
            pl.BlockSpec((tm, D), lambda i, k: (i, 0)),
            pl.BlockSpec((1, D), lambda i, k: (0, 0)),
        ],
        out_specs=pl.BlockSpec((tm, D), lambda i, k: (i, 0)),
        out_shape=jax.ShapeDtypeStruct((S, D), F32),
        scratch_shapes=[pltpu.VMEM((tm, D), F32)],
        compiler_params=pltpu.CompilerParams(
            dimension_semantics=("arbitrary", "arbitrary"), vmem_limit_bytes=VMEM_LIMIT),
        name="ffn_down",
    )(act, w_bf, x1, gain)


def kernel(x, norm1_gain, w_in, rel_bias_table, lambda_q1, lambda_k1, lambda_q2, lambda_k2,
           diff_subln_gain, dil_out_gain, w_out, norm2_gain, w_gate_up, conv_w, conv_b, w_down, final_gain):
    B, S, D = x.shape
    assert B == 1 and w_in.shape[0] == 1
    x2 = x.reshape(S, D)
    n_cols = w_in.shape[2]

    qscale = LOG2E / math.sqrt(HEAD_DIM)
    col = np.arange(n_cols)
    dil_q0 = 2 * DIFF_QK_COLS + DIFF_WIDTH
    is_q = (col < DIFF_QK_COLS) | ((col >= dil_q0) & (col < dil_q0 + DIL_WIDTH))
    colscale = jnp.asarray(np.where(is_q, qscale, 1.0).astype(np.float32)).reshape(1, n_cols)
    tab = rel_bias_table.astype(F32) * LOG2E

    regroup = tuple(d for _, d in DILATED_PATTERNS if d != 1)
    proj, qt_all, vt_all, *cls = _in_proj(x2, norm1_gain.reshape(1, D), w_in[0].astype(BF16), colscale, regroup)
    cls_by_dil = dict(zip(regroup, cls))

    lam = _lambda(lambda_q1.reshape(1, -1), lambda_k1.reshape(1, -1),
                  lambda_q2.reshape(1, -1), lambda_k2.reshape(1, -1))
    o_d = _diff_attention(tab, proj, qt_all, vt_all, lam, diff_subln_gain.reshape(-1, 1))

    outs, lses = [], []
    dil_blk0 = (2 * DIFF_QK_COLS + DIFF_WIDTH) // DIL_WIDTH
    for window, dilation in DILATED_PATTERNS:
        if dilation == 1:
            o_p, lse_p = _dilated_pattern(tab, proj.reshape(1, S, n_cols), dil_blk0, window, dilation)
        else:
            o_p, lse_p = _dilated_pattern(tab, cls_by_dil[dilation], 0, window, dilation)
        outs.append(o_p)
        lses.append(lse_p)
    o_l = _combine(outs, lses, dil_out_gain.reshape(1, -1), tuple(d for _, d in DILATED_PATTERNS))

    x1, h2 = _out_proj(o_d, o_l, w_out[0].astype(BF16), x2, norm2_gain.reshape(1, D))
    act = _ffn_up(h2, w_gate_up[0].astype(BF16), conv_w[0], conv_b.reshape(1, -1))
    out = _ffn_down(act, w_down[0].astype(BF16), x1, final_gain.reshape(1, D))
    return out.reshape(B, S, D)
```

```python
import functools
import math

import numpy as np
import jax
import jax.numpy as jnp
from jax import lax
from jax.experimental import pallas as pl
from jax.experimental.pallas import tpu as pltpu

F32 = jnp.float32
BF16 = jnp.bfloat16

HEAD_DIM = 128
N_DIFF_HEADS = 4
DIFF_V_DIM = 2 * HEAD_DIM
N_DIL_HEADS = 8
DIFF_QK_COLS = N_DIFF_HEADS * 2 * HEAD_DIM
DIFF_WIDTH = N_DIFF_HEADS * DIFF_V_DIM
DIL_WIDTH = N_DIL_HEADS * HEAD_DIM
DILATED_PATTERNS = ((128, 1), (512, 4), (2048, 16))
N_REL_BUCKETS = 32
REL_MAX_DISTANCE = 1024
NORM_EPS = 1e-6
SUBLN_EPS = 1e-5
NEG_INF = -1e30
LOG2E = math.log2(math.e)
LAM_INIT = 0.8 - 0.6 * math.exp(-0.3 * 0)

LANES = 128
BF16_SUBLANES = 16
DIFF_TILE = 1024
DIFF_QUERY_PANEL = 256
SHIFT_MARGIN = 1.0 + 2.0 ** -8
MIN_DENOMINATOR = 2.0 ** -60
DIL_CHUNK = 1024
DIL_HEAD_UNROLL = 4
VMEM_LIMIT = 56 * 1024 * 1024


def _bucket_breaks():
    nb = N_REL_BUCKETS // 2
    max_exact = nb // 2
    rel = np.arange(-2 * REL_MAX_DISTANCE, 2 * REL_MAX_DISTANCE + 1)
    n = np.abs(rel)
    pos = np.log(np.maximum(n, 1) / max_exact) / math.log(REL_MAX_DISTANCE / max_exact) * (nb - max_exact)
    large = np.minimum(max_exact + np.floor(pos).astype(np.int64), nb - 1)
    bucket = np.where(rel > 0, nb, 0) + np.where(n < max_exact, n, large)
    breaks = [(int(rel[i]), int(bucket[i])) for i in range(1, len(rel)) if bucket[i] != bucket[i - 1]]
    return int(bucket[0]), breaks


FIRST_BUCKET, BUCKET_BREAKS = _bucket_breaks()
LAST_BUCKET = BUCKET_BREAKS[-1][1]
FAR_DIST = max(-BUCKET_BREAKS[0][0] + 1, BUCKET_BREAKS[-1][0])


def _bias_from_rel(rel, tab_ref, col):
    val = jnp.full(rel.shape, tab_ref[FIRST_BUCKET, col], F32)
    for thr, b in BUCKET_BREAKS:
        val = jnp.where(rel >= thr, tab_ref[b, col], val)
    return val


def _inproj_kernel(x_ref, g_ref, w_ref, cs_ref, o_ref, qt_ref, vt_ref, *rest, q_tile, v_tile, dil_tile0,
                   dilations):
    cls_refs = rest[:len(dilations)]
    h_ref, acc_ref = rest[len(dilations):]
    j = pl.program_id(1)
    tm = x_ref.shape[0]

    @pl.when(j == 0)
    def _():
        x = x_ref[...]
        ms = jnp.mean(x * x, axis=-1, keepdims=True)
        h_ref[...] = (x * lax.rsqrt(ms + NORM_EPS) * g_ref[...]).astype(BF16)

    acc = jnp.dot(h_ref[...], w_ref[...], preferred_element_type=F32) * cs_ref[...]
    o_ref[...] = acc.astype(o_ref.dtype)

    @pl.when(j == q_tile)
    def _():
        qt_ref[...] = acc.T.astype(qt_ref.dtype)

    @pl.when(j == v_tile)
    def _():
        vt_ref[...] = acc.T.astype(vt_ref.dtype)

    @pl.when(j >= dil_tile0)
    def _():
        for cb in range(acc_ref.shape[0]):
            sl = slice(cb * LANES, (cb + 1) * LANES)
            acc_ref[cb] = acc[:, sl]
            for cls_ref, dil in zip(cls_refs, dilations):
                for c in range(dil):
                    cls_ref[c, :, sl] = acc_ref[cb, pl.ds(c, tm // dil, stride=dil), :].astype(cls_ref.dtype)


def _in_proj(x2, gain, w_bf, colscale, dilations, *, tm=512, tn=1024):
    S, D = x2.shape
    N = w_bf.shape[1]
    assert DIFF_QK_COLS == tn and DIFF_WIDTH == tn and DIL_WIDTH == tn, "outputs assume one column tile each"
    dil_tile0 = (2 * DIFF_QK_COLS + DIFF_WIDTH) // tn
    n_dil_tiles = N // tn - dil_tile0
    kern = functools.partial(_inproj_kernel, q_tile=0, v_tile=2 * DIFF_QK_COLS // tn, dil_tile0=dil_tile0,
                             dilations=dilations)
    cls_specs = [pl.BlockSpec((d, tm // d, tn), lambda i, j: (0, i, jnp.maximum(j - dil_tile0, 0)))
                 for d in dilations]
    cls_shapes = [jax.ShapeDtypeStruct((d, S // d, n_dil_tiles * tn), BF16) for d in dilations]
    return pl.pallas_call(
        kern,
        grid=(S // tm, N // tn),
        in_specs=[
            pl.BlockSpec((tm, D), lambda i, j: (i, 0)),
            pl.BlockSpec((1, D), lambda i, j: (0, 0)),
            pl.BlockSpec((D, tn), lambda i, j: (0, j)),
            pl.BlockSpec((1, tn), lambda i, j: (0, j)),
        ],
        out_specs=[
            pl.BlockSpec((tm, tn), lambda i, j: (i, j)),
            pl.BlockSpec((tn, tm), lambda i, j: (0, i)),
            pl.BlockSpec((tn, tm), lambda i, j: (0, i)),
        ] + cls_specs,
        out_shape=[
            jax.ShapeDtypeStruct((S, N), BF16),
            jax.ShapeDtypeStruct((DIFF_QK_COLS, S), BF16),
            jax.ShapeDtypeStruct((DIFF_WIDTH, S), BF16),
        ] + cls_shapes,
        scratch_shapes=[pltpu.VMEM((tm, D), BF16), pltpu.VMEM((tn // LANES, tm, LANES), F32)],
        compiler_params=pltpu.CompilerParams(
            dimension_semantics=("arbitrary", "arbitrary"), vmem_limit_bytes=VMEM_LIMIT),
        name="in_proj",
    )(x2, gain, w_bf, colscale)


def _diff_attn_kernel(tab_ref, q1t_ref, q2t_ref, k1_ref, k2_ref, vt_ref, lam_ref, gain_ref, o_ref,
                      bias_ref, m_ref, l_ref, acc_ref, p_ref, shift_ref, knorm_ref, prev_ref,
                      *, T, QP, n_near, n_tiles):
    h = pl.program_id(0)
    qi = pl.program_id(1)
    n_chains = 2 * (T // QP)
    qts = (q1t_ref, q2t_ref)
    ks = (k1_ref, k2_ref)

    def chain_of(i):
        qp = i // 2
        return i % 2, slice(qp * QP, (qp + 1) * QP)

    @pl.when(qi == 0)
    def _():
        for mi in range(2):
            def knorm(t, best, mi=mi):
                k = ks[mi][pl.ds(pl.multiple_of(t * T, T), T), :].astype(F32)
                return jnp.maximum(best, jnp.max(jnp.sum(k * k, axis=1, keepdims=True), axis=0, keepdims=True))
            knorm_ref[mi] = lax.fori_loop(0, n_tiles, knorm, jnp.zeros((1, 1), F32))

        x = lax.broadcasted_iota(jnp.int32, (8, 2 * T), 1)
        x = jnp.where(x < T, x, x - 2 * T)
        for di, d in enumerate(range(-n_near, n_near + 1)):
            g = _bias_from_rel(d * T - x, tab_ref, h)
            base = jnp.broadcast_to(g[0:1, :], (LANES, 2 * T))
            for rb in range(T // LANES):
                blk = pltpu.roll(base, rb * LANES, 1, stride=1, stride_axis=0)
                bias_ref[di, rb * LANES:(rb + 1) * LANES, :] = blk[:, :T]

    c_left = tab_ref[FIRST_BUCKET, h]
    c_right = tab_ref[LAST_BUCKET, h]

    bias_max = tab_ref[0, h]
    for b in range(1, N_REL_BUCKETS):
        bias_max = jnp.maximum(bias_max, tab_ref[b, h])

    for mi in range(2):
        q = qts[mi][...].astype(F32)
        qnorm2 = jnp.sum(q * q, axis=0, keepdims=True)
        shift_ref[mi] = jnp.sqrt(qnorm2 * knorm_ref[mi]) * SHIFT_MARGIN + bias_max
    l_ref[...] = jnp.zeros(l_ref.shape, F32)
    acc_ref[...] = jnp.zeros(acc_ref.shape, F32)
    p_ref[n_chains - 1] = jnp.zeros((T, QP), BF16)
    prev_ref[0] = 0

    def score_stage(i, k0, bias_di, bias_const):
        mi, qs = chain_of(i)
        s = jnp.dot(ks[mi][pl.ds(k0, T), :], qts[mi][:, qs], preferred_element_type=F32)
        if bias_di is not None:
            p = jnp.exp2(s + bias_ref[bias_di, :, qs] - shift_ref[mi, :, qs])
        else:
            p = jnp.exp2(s - (shift_ref[mi, :, qs] - bias_const))
        l_ref[mi, :, qs] += jnp.sum(p, axis=0, keepdims=True)
        p_ref[i] = p.astype(BF16)

    def value_stage(i, k0):
        mi, qs = chain_of(i)
        acc_ref[mi, :, qs] += jnp.dot(vt_ref[:, pl.ds(k0, T)], p_ref[i], preferred_element_type=F32)

    def tile(kt, bias_di, bias_const):
        k0 = pl.multiple_of(kt * T, T)
        pk0 = pl.multiple_of(prev_ref[0] * T, T)
        for i in range(n_chains):
            value_stage((i - 1) % n_chains, pk0 if i < 1 else k0)
            score_stage(i, k0, bias_di, bias_const)
        prev_ref[0] = kt

    lo = jnp.maximum(qi - n_near, 0)
    hi = jnp.minimum(qi + n_near + 1, n_tiles)

    @pl.loop(0, lo)
    def _(kt):
        tile(kt, None, c_left)

    for di, d in enumerate(range(-n_near, n_near + 1)):
        kt = qi + d

        @pl.when(jnp.logical_and(kt >= 0, kt < n_tiles))
        def _(di=di, kt=kt):
            tile(kt, di, None)

    @pl.loop(hi, n_tiles)
    def _(kt):
        tile(kt, None, c_right)

    value_stage(n_chains - 1, pl.multiple_of(prev_ref[0] * T, T))

    @pl.when(jnp.min(l_ref[...]) < MIN_DENOMINATOR)
    def _():
        m_ref[...] = jnp.full(m_ref.shape, -jnp.inf, F32)
        l_ref[...] = jnp.zeros(l_ref.shape, F32)
        acc_ref[...] = jnp.zeros(acc_ref.shape, F32)

        @pl.loop(0, n_tiles)
        def _(kt):
            k0 = pl.multiple_of(kt * T, T)
            d = kt - qi
            near = jnp.abs(d) <= n_near
            di = jnp.clip(d + n_near, 0, 2 * n_near)
            c_far = jnp.where(d < 0, c_left, c_right)
            for i in range(n_chains):
                mi, qs = chain_of(i)
                s = jnp.dot(ks[mi][pl.ds(k0, T), :], qts[mi][:, qs], preferred_element_type=F32)
                s = s + jnp.where(near, bias_ref[di, :, qs], c_far)
                m = m_ref[mi, :, qs]
                m_new = jnp.maximum(m, jnp.max(s, axis=0, keepdims=True))
                alpha = jnp.exp2(m - m_new)
                p = jnp.exp2(s - m_new)
                m_ref[mi, :, qs] = m_new
                l_ref[mi, :, qs] = alpha * l_ref[mi, :, qs] + jnp.sum(p, axis=0, keepdims=True)
                pv = jnp.dot(vt_ref[:, pl.ds(k0, T)], p.astype(BF16), preferred_element_type=F32)
                acc_ref[mi, :, qs] = alpha * acc_ref[mi, :, qs] + pv

    lam = lam_ref[0, 0]
    o = acc_ref[0] / l_ref[0] - lam * (acc_ref[1] / l_ref[1])
    ms = jnp.mean(o * o, axis=0, keepdims=True)
    o = o * lax.rsqrt(ms + SUBLN_EPS) * (gain_ref[...] * (1.0 - LAM_INIT))
    o_ref[...] = o.T.astype(o_ref.dtype)


def _lambda_kernel(q1_ref, k1_ref, q2_ref, k2_ref, o_ref):
    a = jnp.sum(q1_ref[...] * k1_ref[...], axis=-1, keepdims=True)
    b = jnp.sum(q2_ref[...] * k2_ref[...], axis=-1, keepdims=True)
    o_ref[...] = jnp.exp(a) - jnp.exp(b) + LAM_INIT


def _lambda(lq1, lk1, lq2, lk2):
    return pl.pallas_call(
        _lambda_kernel, out_shape=jax.ShapeDtypeStruct((1, 1), F32), name="diff_lambda",
    )(lq1, lk1, lq2, lk2)


def _diff_attention(tab_diff, proj, qt_all, vt_all, lam, gain_col):
    S = proj.shape[0]
    T = DIFF_TILE
    n_tiles = S // T
    n_near = -(-(FAR_DIST - 1) // T)
    kern = functools.partial(_diff_attn_kernel, T=T, QP=DIFF_QUERY_PANEL, n_near=n_near, n_tiles=n_tiles)
    k_block0 = DIFF_QK_COLS // HEAD_DIM
    smem = pl.BlockSpec(memory_space=pltpu.SMEM)
    return pl.pallas_call(
        kern,
        grid=(N_DIFF_HEADS, n_tiles),
        in_specs=[
            smem,
            pl.BlockSpec((HEAD_DIM, T), lambda h, i: (2 * h, i)),
            pl.BlockSpec((HEAD_DIM, T), lambda h, i: (2 * h + 1, i)),
            pl.BlockSpec((S, HEAD_DIM), lambda h, i: (0, k_block0 + 2 * h), pipeline_mode=pl.Buffered(1)),
            pl.BlockSpec((S, HEAD_DIM), lambda h, i: (0, k_block0 + 2 * h + 1), pipeline_mode=pl.Buffered(1)),
            pl.BlockSpec((DIFF_V_DIM, S), lambda h, i: (h, 0), pipeline_mode=pl.Buffered(1)),
            smem,
            pl.BlockSpec((DIFF_V_DIM, 1), lambda h, i: (0, 0)),
        ],
        out_specs=pl.BlockSpec((T, DIFF_V_DIM), lambda h, i: (i, h)),
        out_shape=jax.ShapeDtypeStruct((S, DIFF_WIDTH), BF16),
        scratch_shapes=[pltpu.VMEM((2 * n_near + 1, T, T), F32),
                        pltpu.VMEM((2, 1, T), F32), pltpu.VMEM((2, 1, T), F32),
                        pltpu.VMEM((2, DIFF_V_DIM, T), F32),
                        pltpu.VMEM((2 * (T // DIFF_QUERY_PANEL), T, DIFF_QUERY_PANEL), BF16),
                        pltpu.VMEM((2, 1, T), F32), pltpu.VMEM((2, 1, 1), F32),
                        pltpu.SMEM((1,), jnp.int32)],
        compiler_params=pltpu.CompilerParams(
            dimension_semantics=("arbitrary", "arbitrary"), vmem_limit_bytes=VMEM_LIMIT),
        name="diff_attn",
    )(tab_diff, qt_all, qt_all, proj, proj, vt_all, lam, gain_col)


def _dilated_kernel(tab_ref, q_ref, kp_ref, km_ref, kn_ref, vp_ref, vm_ref, vn_ref, o_ref, lse_ref,
                    bias_ref, kx_ref, vx_ref, *, R, B, half, dilation, n_chunks):
    c = pl.program_id(0)
    n = pl.program_id(1)
    W = B + 2 * half
    nblk = R // B

    @pl.when(jnp.logical_and(c == 0, n == 0))
    def _():
        rows = 8
        col = lax.broadcasted_iota(jnp.int32, (rows, W), 1)
        row = lax.broadcasted_iota(jnp.int32, (rows, W), 0)
        for hh in range(N_DIL_HEADS):
            def fill(r, carry, hh=hh):
                r0 = pl.multiple_of(r * rows, rows)
                off = col - half - (row + r0)
                bias = _bias_from_rel(off * dilation, tab_ref, N_DIFF_HEADS + hh)
                base = jnp.where(jnp.abs(off) <= half, bias, NEG_INF * LOG2E)
                bias_ref[hh, 1, pl.ds(r0, rows), :] = base
                bias_ref[hh, 0, pl.ds(r0, rows), :] = jnp.where(col >= half, base, NEG_INF * LOG2E)
                bias_ref[hh, 2, pl.ds(r0, rows), :] = jnp.where(col < B + half, base, NEG_INF * LOG2E)
                return carry
            lax.fori_loop(0, B // rows, fill, 0)

    kx_ref[0:half, :] = kp_ref[...]
    kx_ref[half:half + R, :] = km_ref[...]
    kx_ref[half + R:, :] = kn_ref[...]
    vx_ref[0:half, :] = vp_ref[...]
    vx_ref[half:half + R, :] = vm_ref[...]
    vx_ref[half + R:, :] = vn_ref[...]

    def chain(hh, b):
        c0 = pl.multiple_of(hh * HEAD_DIM, HEAD_DIM)
        r0 = b * B
        var = 1
        if b == 0:
            var = jnp.where(n == 0, 0, var)
        if b == nblk - 1:
            var = jnp.where(n == n_chunks - 1, 2, var)
        q = q_ref[pl.ds(r0, B), pl.ds(c0, HEAD_DIM)]
        k = kx_ref[pl.ds(r0, W), pl.ds(c0, HEAD_DIM)]
        v = vx_ref[pl.ds(r0, W), pl.ds(c0, HEAD_DIM)]
        s = lax.dot_general(q, k, (((1,), (1,)), ((), ())), preferred_element_type=F32)
        s = s + bias_ref[hh, var]
        m = jnp.max(s, axis=-1, keepdims=True)
        e = jnp.exp2(s - m)
        den = jnp.sum(e, axis=-1, keepdims=True)
        o = jnp.dot(e.astype(BF16), v, preferred_element_type=F32) / den
        o_ref[pl.ds(r0, B), pl.ds(c0, HEAD_DIM)] = o.astype(o_ref.dtype)
        lse = m + jnp.log2(den)
        lse_ref[pl.ds(r0, B), pl.ds(c0, HEAD_DIM)] = jnp.broadcast_to(lse, (B, HEAD_DIM))

    @pl.loop(0, N_DIL_HEADS // DIL_HEAD_UNROLL)
    def _(hg):
        for u in range(DIL_HEAD_UNROLL):
            for b in range(nblk):
                chain(hg * DIL_HEAD_UNROLL + u, b)


def _dilated_pattern(tab, qkv, col_blk0, window, dilation, *, B=256):
    _, L, _ = qkv.shape
    R = min(DIL_CHUNK, L)
    half = window // (2 * dilation)
    assert L % R == 0 and R % B == 0 and half % BF16_SUBLANES == 0 and R % half == 0
    n_chunks = L // R
    q_blk, k_blk, v_blk = col_blk0, col_blk0 + 1, col_blk0 + 2
    hb = R // half
    n_hblk = L // half

    def main(blk):
        return pl.BlockSpec((None, R, DIL_WIDTH), lambda c, n: (c, n, blk))

    def prev(blk):
        return pl.BlockSpec((None, half, DIL_WIDTH), lambda c, n: (c, jnp.maximum(n * hb - 1, 0), blk))

    def nxt(blk):
        return pl.BlockSpec((None, half, DIL_WIDTH), lambda c, n: (c, jnp.minimum((n + 1) * hb, n_hblk - 1), blk))

    kern = functools.partial(_dilated_kernel, R=R, B=B, half=half, dilation=dilation, n_chunks=n_chunks)
    out_spec = pl.BlockSpec((None, R, DIL_WIDTH), lambda c, n: (c, n, 0))
    return pl.pallas_call(
        kern,
        grid=(dilation, n_chunks),
        in_specs=[pl.BlockSpec(memory_space=pltpu.SMEM),
                  main(q_blk), prev(k_blk), main(k_blk), nxt(k_blk), prev(v_blk), main(v_blk), nxt(v_blk)],
        out_specs=[out_spec, out_spec],
        out_shape=[jax.ShapeDtypeStruct((dilation, L, DIL_WIDTH), BF16),
                   jax.ShapeDtypeStruct((dilation, L, DIL_WIDTH), F32)],
        scratch_shapes=[pltpu.VMEM((N_DIL_HEADS, 3, B, B + 2 * half), F32),
                        pltpu.VMEM((R + 2 * half, DIL_WIDTH), BF16),
                        pltpu.VMEM((R + 2 * half, DIL_WIDTH), BF16)],
        compiler_params=pltpu.CompilerParams(
            dimension_semantics=("arbitrary", "arbitrary"), vmem_limit_bytes=VMEM_LIMIT),
        name=f"dilated_d{dilation}",
    )(tab, qkv, qkv, qkv, qkv, qkv, qkv, qkv)


def _combine_kernel(*refs, dilations):
    n = len(dilations)
    o_refs, l_refs = refs[:n], refs[n:2 * n]
    g_ref, out_ref = refs[2 * n], refs[2 * n + 1]
    scratch = refs[2 * n + 2:]
    tm = out_ref.shape[0]

    for hh in range(N_DIL_HEADS):
        sl = slice(hh * HEAD_DIM, (hh + 1) * HEAD_DIM)
        outs, lses = [], []
        si = 0
        for o_ref, l_ref, dil in zip(o_refs, l_refs, dilations):
            if dil == 1:
                outs.append(o_ref[0, :, sl].astype(F32))
                lses.append(l_ref[0, :, sl])
                continue
            os_ref, ls_ref = scratch[si], scratch[si + 1]
            si += 2
            for c in range(dil):
                os_ref[hh, pl.ds(c, tm // dil, stride=dil), :] = o_ref[c, :, sl].astype(F32)
                ls_ref[hh, pl.ds(c, tm // dil, stride=dil), :] = l_ref[c, :, sl]
            outs.append(os_ref[hh])
            lses.append(ls_ref[hh])

        m = functools.reduce(jnp.maximum, lses)
        ws = [jnp.exp2(l - m) for l in lses]
        tot = functools.reduce(lambda a, b: a + b, ws)
        oh = functools.reduce(lambda a, b: a + b, [(w / tot) * op for w, op in zip(ws, outs)])
        ms = jnp.mean(oh * oh, axis=-1, keepdims=True)
        out_ref[:, sl] = (oh * lax.rsqrt(ms + NORM_EPS) * g_ref[:, sl]).astype(out_ref.dtype)


def _combine(outs, lses, gain, dilations, *, tm=512):
    S = outs[0].shape[0] * outs[0].shape[1]
    specs = [pl.BlockSpec((d, tm // d, DIL_WIDTH), lambda i: (0, i, 0)) for d in dilations]
    n_scr = sum(1 for d in dilations if d != 1)
    return pl.pallas_call(
        functools.partial(_combine_kernel, dilations=dilations),
        grid=(S // tm,),
        in_specs=specs + specs + [pl.BlockSpec((1, DIL_WIDTH), lambda i: (0, 0))],
        scratch_shapes=[pltpu.VMEM((N_DIL_HEADS, tm, HEAD_DIM), F32)] * (2 * n_scr),
        out_specs=pl.BlockSpec((tm, DIL_WIDTH), lambda i: (i, 0)),
        out_shape=jax.ShapeDtypeStruct((S, DIL_WIDTH), BF16),
        compiler_params=pltpu.CompilerParams(
            dimension_semantics=("arbitrary",), vmem_limit_bytes=VMEM_LIMIT),
        name="dilated_combine",
    )(*outs, *lses, gain)


def _key_norm2_kernel(k_ref, o_ref):
    @pl.when(pl.program_id(0) == 0)
    def _():
        o_ref[...] = jnp.zeros(o_ref.shape, F32)

    k = k_ref[...].astype(F32)
    for hh in range(N_DIL_HEADS):
        kh = k[:, hh * HEAD_DIM:(hh + 1) * HEAD_DIM]
        n2 = jnp.max(jnp.sum(kh * kh, axis=1, keepdims=True), axis=0, keepdims=True)
        o_ref[hh:hh + 1, :] = jnp.maximum(o_ref[hh:hh + 1, :], jnp.broadcast_to(n2, (1, LANES)))


def _key_norm2(proj, k_blk, *, tm=1024):
    S = proj.shape[0]
    return pl.pallas_call(
        _key_norm2_kernel,
        grid=(S // tm,),
        in_specs=[pl.BlockSpec((tm, DIL_WIDTH), lambda i: (i, k_blk))],
        out_specs=pl.BlockSpec((N_DIL_HEADS, LANES), lambda i: (0, 0)),
        out_shape=jax.ShapeDtypeStruct((N_DIL_HEADS, LANES), F32),
        compiler_params=pltpu.CompilerParams(dimension_semantics=("arbitrary",), vmem_limit_bytes=VMEM_LIMIT),
        name="dilated_key_norm",
    )(proj)


def _dilated_fast_kernel(tab_ref, kn2_ref, q_ref, kp_ref, km_ref, kn_ref, vp_ref, vm_ref, vn_ref, num_ref, den_ref,
                         bias_ref, kx_ref, vx_ref, p_ref, *, R, B, half, dilation, n_chunks):
    c = pl.program_id(0)
    n = pl.program_id(1)
    W = B + 2 * half
    nblk = R // B
    VW = 2 * HEAD_DIM

    @pl.when(jnp.logical_and(c == 0, n == 0))
    def _():
        rows = 8
        col = lax.broadcasted_iota(jnp.int32, (rows, W), 1)
        row = lax.broadcasted_iota(jnp.int32, (rows, W), 0)
        for hh in range(N_DIL_HEADS):
            def fill(r, carry, hh=hh):
                r0 = pl.multiple_of(r * rows, rows)
                off = col - half - (row + r0)
                bias = _bias_from_rel(off * dilation, tab_ref, N_DIFF_HEADS + hh)
                base = jnp.where(jnp.abs(off) <= half, bias, NEG_INF * LOG2E)
                bias_ref[hh, 1, pl.ds(r0, rows), :] = base
                bias_ref[hh, 0, pl.ds(r0, rows), :] = jnp.where(col >= half, base, NEG_INF * LOG2E)
                bias_ref[hh, 2, pl.ds(r0, rows), :] = jnp.where(col < B + half, base, NEG_INF * LOG2E)
                return carry
            lax.fori_loop(0, B // rows, fill, 0)
        vx_ref[...] = jnp.ones(vx_ref.shape, BF16)

    kx_ref[0:half, :] = kp_ref[...]
    kx_ref[half:half + R, :] = km_ref[...]
    kx_ref[half + R:, :] = kn_ref[...]
    for hh in range(N_DIL_HEADS):
        src = slice(hh * HEAD_DIM, (hh + 1) * HEAD_DIM)
        dst = slice(hh * VW, hh * VW + HEAD_DIM)
        vx_ref[0:half, dst] = vp_ref[:, src]
        vx_ref[half:half + R, dst] = vm_ref[:, src]
        vx_ref[half + R:, dst] = vn_ref[:, src]

    n_slots = p_ref.shape[0]
    bias_max = []
    for hh in range(N_DIL_HEADS):
        bm = tab_ref[0, N_DIFF_HEADS + hh]
        for bk in range(1, N_REL_BUCKETS):
            bm = jnp.maximum(bm, tab_ref[bk, N_DIFF_HEADS + hh])
        bias_max.append(bm)

    def score_stage(hh, b, slot):
        cs = slice(hh * HEAD_DIM, (hh + 1) * HEAD_DIM)
        r0 = b * B
        var = 1
        if b == 0:
            var = jnp.where(n == 0, 0, var)
        if b == nblk - 1:
            var = jnp.where(n == n_chunks - 1, 2, var)
        q = q_ref[r0:r0 + B, cs]
        qf = q.astype(F32)
        qn2 = jnp.sum(qf * qf, axis=1, keepdims=True)
        shift = jnp.sqrt(qn2 * kn2_ref[hh:hh + 1, 0:1]) * SHIFT_MARGIN + bias_max[hh]
        s = lax.dot_general(q, kx_ref[r0:r0 + W, cs], (((1,), (1,)), ((), ())), preferred_element_type=F32)
        p_ref[slot] = jnp.exp2(s + bias_ref[hh, var] - shift).astype(BF16)

    def value_stage(hh, b, slot):
        cs = slice(hh * HEAD_DIM, (hh + 1) * HEAD_DIM)
        r0 = b * B
        nd = jnp.dot(p_ref[slot], vx_ref[r0:r0 + W, hh * VW:(hh + 1) * VW], preferred_element_type=F32)
        num_ref[r0:r0 + B, cs] = nd[:, :HEAD_DIM].astype(num_ref.dtype)
        den_ref[r0:r0 + B, cs] = nd[:, HEAD_DIM:].astype(den_ref.dtype)

    chains = [(hh, b) for hh in range(N_DIL_HEADS) for b in range(nblk)]
    for i, (hh, b) in enumerate(chains):
        if i > 0:
            value_stage(*chains[i - 1], (i - 1) % n_slots)
        score_stage(hh, b, i % n_slots)
    value_stage(*chains[-1], (len(chains) - 1) % n_slots)


def _dilated_fast(tab, kn2, qkv, col_blk0, window, dilation, *, B=128):
    _, L, _ = qkv.shape
    R = min(DIL_CHUNK, L)
    half = window // (2 * dilation)
    assert L % R == 0 and R % B == 0 and half % BF16_SUBLANES == 0 and R % half == 0
    n_chunks = L // R
    q_blk, k_blk, v_blk = col_blk0, col_blk0 + 1, col_blk0 + 2
    hb = R // half
    n_hblk = L // half

    def main(blk):
        return pl.BlockSpec((None, R, DIL_WIDTH), lambda c, n: (c, n, blk))

    def prev(blk):
        return pl.BlockSpec((None, half, DIL_WIDTH), lambda c, n: (c, jnp.maximum(n * hb - 1, 0), blk))

    def nxt(blk):
        return pl.BlockSpec((None, half, DIL_WIDTH), lambda c, n: (c, jnp.minimum((n + 1) * hb, n_hblk - 1), blk))

    kern = functools.partial(_dilated_fast_kernel, R=R, B=B, half=half, dilation=dilation, n_chunks=n_chunks)
    out_spec = pl.BlockSpec((None, R, DIL_WIDTH), lambda c, n: (c, n, 0))
    return pl.pallas_call(
        kern,
        grid=(dilation, n_chunks),
        in_specs=[pl.BlockSpec(memory_space=pltpu.SMEM),
                  pl.BlockSpec((N_DIL_HEADS, LANES), lambda c, n: (0, 0)),
                  main(q_blk), prev(k_blk), main(k_blk), nxt(k_blk), prev(v_blk), main(v_blk), nxt(v_blk)],
        out_specs=[out_spec, out_spec],
        out_shape=[jax.ShapeDtypeStruct((dilation, L, DIL_WIDTH), BF16),
                   jax.ShapeDtypeStruct((dilation, L, DIL_WIDTH), BF16)],
        scratch_shapes=[pltpu.VMEM((N_DIL_HEADS, 3, B, B + 2 * half), F32),
                        pltpu.VMEM((R + 2 * half, DIL_WIDTH), BF16),
                        pltpu.VMEM((R + 2 * half, 2 * DIL_WIDTH), BF16),
                        pltpu.VMEM((4, B, B + 2 * half), BF16)],
        compiler_params=pltpu.CompilerParams(
            dimension_semantics=("arbitrary", "arbitrary"), vmem_limit_bytes=VMEM_LIMIT),
        name=f"dilated_fast_d{dilation}",
    )(tab, kn2, qkv, qkv, qkv, qkv, qkv, qkv, qkv)


def _combine_fast_kernel(*refs, dilations):
    n = len(dilations)
    n_refs, d_refs = refs[:n], refs[n:2 * n]
    g_ref, out_ref, dmin_ref = refs[2 * n], refs[2 * n + 1], refs[2 * n + 2]
    scratch = refs[2 * n + 3:]
    tm = out_ref.shape[0]

    for hh in range(N_DIL_HEADS):
        sl = slice(hh * HEAD_DIM, (hh + 1) * HEAD_DIM)
        nums, dens = [], []
        si = 0
        for n_ref, d_ref, dil in zip(n_refs, d_refs, dilations):
            if dil == 1:
                nums.append(n_ref[0, :, sl].astype(F32))
                dens.append(d_ref[0, :, sl].astype(F32))
                continue
            ns_ref, ds_ref = scratch[si], scratch[si + 1]
            si += 2
            for c in range(dil):
                ns_ref[hh, pl.ds(c, tm // dil, stride=dil), :] = n_ref[c, :, sl].astype(F32)
                ds_ref[hh, pl.ds(c, tm // dil, stride=dil), :] = d_ref[c, :, sl].astype(F32)
            nums.append(ns_ref[hh])
            dens.append(ds_ref[hh])

        den = functools.reduce(lambda a, b: a + b, dens)
        oh = functools.reduce(lambda a, b: a + b, nums) / den
        ms = jnp.mean(oh * oh, axis=-1, keepdims=True)
        out_ref[:, sl] = (oh * lax.rsqrt(ms + NORM_EPS) * g_ref[:, sl]).astype(out_ref.dtype)
        dmin_ref[0, hh:hh + 1, :] = jnp.min(den, axis=0, keepdims=True)


def _combine_fast(nums, dens, gain, dilations, *, tm=512):
    S = nums[0].shape[0] * nums[0].shape[1]
    specs = [pl.BlockSpec((d, tm // d, DIL_WIDTH), lambda i: (0, i, 0)) for d in dilations]
    n_scr = sum(1 for d in dilations if d != 1)
    return pl.pallas_call(
        functools.partial(_combine_fast_kernel, dilations=dilations),
        grid=(S // tm,),
        in_specs=specs + specs + [pl.BlockSpec((1, DIL_WIDTH), lambda i: (0, 0))],
        scratch_shapes=[pltpu.VMEM((N_DIL_HEADS, tm, HEAD_DIM), F32)] * (2 * n_scr),
        out_specs=[pl.BlockSpec((tm, DIL_WIDTH), lambda i: (i, 0)),
                   pl.BlockSpec((1, N_DIL_HEADS, LANES), lambda i: (i, 0, 0))],
        out_shape=[jax.ShapeDtypeStruct((S, DIL_WIDTH), BF16),
                   jax.ShapeDtypeStruct((S // tm, N_DIL_HEADS, LANES), F32)],
        compiler_params=pltpu.CompilerParams(
            dimension_semantics=("arbitrary",), vmem_limit_bytes=VMEM_LIMIT),
        name="dilated_combine_fast",
    )(*nums, *dens, gain)


def _outproj_kernel(od_ref, ol_ref, wd_ref, wl_ref, x_ref, g_ref, x1_ref, h2_ref):
    acc = jnp.dot(od_ref[...], wd_ref[...], preferred_element_type=F32)
    acc = acc + jnp.dot(ol_ref[...], wl_ref[...], preferred_element_type=F32)
    x1 = x_ref[...] + acc
    x1_ref[...] = x1
    ms = jnp.mean(x1 * x1, axis=-1, keepdims=True)
    h2_ref[...] = (x1 * lax.rsqrt(ms + NORM_EPS) * g_ref[...]).astype(h2_ref.dtype)


def _out_proj(o_d, o_l, w_bf, x2, gain, *, tm=512):
    S, D = x2.shape
    return pl.pallas_call(
        _outproj_kernel,
        grid=(S // tm,),
        in_specs=[
            pl.BlockSpec((tm, DIFF_WIDTH), lambda i: (i, 0)),
            pl.BlockSpec((tm, DIL_WIDTH), lambda i: (i, 0)),
            pl.BlockSpec((DIFF_WIDTH, D), lambda i: (0, 0)),
            pl.BlockSpec((DIL_WIDTH, D), lambda i: (1, 0)),
            pl.BlockSpec((tm, D), lambda i: (i, 0)),
            pl.BlockSpec((1, D), lambda i: (0, 0)),
        ],
        out_specs=[pl.BlockSpec((tm, D), lambda i: (i, 0)), pl.BlockSpec((tm, D), lambda i: (i, 0))],
        out_shape=[jax.ShapeDtypeStruct((S, D), F32), jax.ShapeDtypeStruct((S, D), BF16)],
        compiler_params=pltpu.CompilerParams(
            dimension_semantics=("arbitrary",), vmem_limit_bytes=VMEM_LIMIT),
        name="out_proj",
    )(o_d, o_l, w_bf, w_bf, x2, gain)


def _ffn_up_kernel(hm_ref, hp_ref, hn_ref, wg_ref, wu_ref, cw_ref, cb_ref, o_ref, lhs_ref, *, tm, n_row_tiles):
    i = pl.program_id(0)
    j = pl.program_id(1)
    halo = BF16_SUBLANES

    @pl.when(j == 0)
    def _():
        lhs_ref[0:halo, :] = jnp.where(i == 0, jnp.zeros_like(hp_ref[...]), hp_ref[...])
        lhs_ref[halo:halo + tm, :] = hm_ref[...]
        lhs_ref[halo + tm:, :] = jnp.where(i == n_row_tiles - 1, jnp.zeros_like(hn_ref[...]), hn_ref[...])

    g = jnp.dot(lhs_ref[...], wg_ref[...], preferred_element_type=F32)
    u = jnp.dot(lhs_ref[halo:halo + tm, :], wu_ref[...], preferred_element_type=F32)
    rows = tm + 2 * halo
    g_prev = pltpu.roll(g, 1, axis=0)
    g_next = pltpu.roll(g, rows - 1, axis=0)
    y = cw_ref[0:1, :] * g_prev + cw_ref[1:2, :] * g + cw_ref[2:3, :] * g_next + cb_ref[...]
    y = y[halo:halo + tm, :]
    act = y * (1.0 / (1.0 + jnp.exp(-y))) * u
    o_ref[...] = act.astype(o_ref.dtype)


def _ffn_up(h2, w_bf, conv_w, conv_b, *, tm=1024, tn=512):
    S, D = h2.shape
    d_ff = conv_w.shape[1]
    assert d_ff % tn == 0
    nj = d_ff // tn
    ni = S // tm
    hb = tm // BF16_SUBLANES
    n_hblk = S // BF16_SUBLANES
    kern = functools.partial(_ffn_up_kernel, tm=tm, n_row_tiles=ni)
    return pl.pallas_call(
        kern,
        grid=(ni, nj),
        in_specs=[
            pl.BlockSpec((tm, D), lambda i, j: (i, 0)),
            pl.BlockSpec((BF16_SUBLANES, D), lambda i, j: (jnp.maximum(i * hb - 1, 0), 0)),
            pl.BlockSpec((BF16_SUBLANES, D), lambda i, j: (jnp.minimum((i + 1) * hb, n_hblk - 1), 0)),
            pl.BlockSpec((D, tn), lambda i, j: (0, j)),
            pl.BlockSpec((D, tn), lambda i, j: (0, nj + j)),
            pl.BlockSpec((3, tn), lambda i, j: (0, j)),
            pl.BlockSpec((1, tn), lambda i, j: (0, j)),
        ],
        out_specs=pl.BlockSpec((tm, tn), lambda i, j: (i, j)),
        out_shape=jax.ShapeDtypeStruct((S, d_ff), BF16),
        scratch_shapes=[pltpu.VMEM((tm + 2 * BF16_SUBLANES, D), BF16)],
        compiler_params=pltpu.CompilerParams(
            dimension_semantics=("arbitrary", "arbitrary"), vmem_limit_bytes=VMEM_LIMIT),
        name="ffn_up",
    )(h2, h2, h2, w_bf, w_bf, conv_w, conv_b)


def _ffn_down_kernel(a_ref, w_ref, x1_ref, g_ref, o_ref, acc_ref, *, n_k):
    k = pl.program_id(1)

    @pl.when(k == 0)
    def _():
        acc_ref[...] = x1_ref[...]

    acc_ref[...] += jnp.dot(a_ref[...], w_ref[...], preferred_element_type=F32)

    @pl.when(k == n_k - 1)
    def _():
        y = acc_ref[...]
        ms = jnp.mean(y * y, axis=-1, keepdims=True)
        o_ref[...] = y * lax.rsqrt(ms + NORM_EPS) * g_ref[...]


def _ffn_down(act, w_bf, x1, gain, *, tm=512, tk=1408):
    S, d_ff = act.shape
    D = x1.shape[1]
    n_k = d_ff // tk
    kern = functools.partial(_ffn_down_kernel, n_k=n_k)
    return pl.pallas_call(
        kern,
        grid=(S // tm, n_k),
        in_specs=[
            pl.BlockSpec((tm, tk), lambda i, k: (i, k)),
            pl.BlockSpec((tk, D), lambda i, k: (k, 0)),
            pl.BlockSpec((tm, D), lambda i, k: (i, 0)),
            pl.BlockSpec((1, D), lambda i, k: (0, 0)),
        ],
        out_specs=pl.BlockSpec((tm, D), lambda i, k: (i, 0)),
        out_shape=jax.ShapeDtypeStruct((S, D), F32),
        scratch_shapes=[pltpu.VMEM((tm, D), F32)],
        compiler_params=pltpu.CompilerParams(
            dimension_semantics=("arbitrary", "arbitrary"), vmem_limit_bytes=VMEM_LIMIT),
        name="ffn_down",
    )(act, w_bf, x1, gain)


def kernel(x, norm1_gain, w_in, rel_bias_table, lambda_q1, lambda_k1, lambda_q2, lambda_k2,
           diff_subln_gain, dil_out_gain, w_out, norm2_gain, w_gate_up, conv_w, conv_b, w_down, final_gain):
    B, S, D = x.shape
    assert B == 1 and w_in.shape[0] == 1
    x2 = x.reshape(S, D)
    n_cols = w_in.shape[2]

    qscale = LOG2E / math.sqrt(HEAD_DIM)
    col = np.arange(n_cols)
    dil_q0 = 2 * DIFF_QK_COLS + DIFF_WIDTH
    is_q = (col < DIFF_QK_COLS) | ((col >= dil_q0) & (col < dil_q0 + DIL_WIDTH))
    colscale = jnp.asarray(np.where(is_q, qscale, 1.0).astype(np.float32)).reshape(1, n_cols)
    tab = rel_bias_table.astype(F32) * LOG2E

    regroup = tuple(d for _, d in DILATED_PATTERNS if d != 1)
    proj, qt_all, vt_all, *cls = _in_proj(x2, norm1_gain.reshape(1, D), w_in[0].astype(BF16), colscale, regroup)
    cls_by_dil = dict(zip(regroup, cls))

    lam = _lambda(lambda_q1.reshape(1, -1), lambda_k1.reshape(1, -1),
                  lambda_q2.reshape(1, -1), lambda_k2.reshape(1, -1))
    o_d = _diff_attention(tab, proj, qt_all, vt_all, lam, diff_subln_gain.reshape(-1, 1))

    dil_blk0 = (2 * DIFF_QK_COLS + DIFF_WIDTH) // DIL_WIDTH
    dilations = tuple(d for _, d in DILATED_PATTERNS)
    dil_gain = dil_out_gain.reshape(1, -1)

    def pattern_inputs(dilation):
        if dilation == 1:
            return proj.reshape(1, S, n_cols), dil_blk0
        return cls_by_dil[dilation], 0

    kn2 = _key_norm2(proj, dil_blk0 + 1)
    nums, dens = [], []
    for window, dilation in DILATED_PATTERNS:
        n_p, d_p = _dilated_fast(tab, kn2, *pattern_inputs(dilation), window, dilation)
        nums.append(n_p)
        dens.append(d_p)
    o_l_fast, den_min = _combine_fast(nums, dens, dil_gain, dilations)

    def exact_dilated():
        outs, lses = [], []
        for window, dilation in DILATED_PATTERNS:
            o_p, lse_p = _dilated_pattern(tab, *pattern_inputs(dilation), window, dilation)
            outs.append(o_p)
            lses.append(lse_p)
        return _combine(outs, lses, dil_gain, dilations)

    o_l = lax.cond(jnp.min(den_min) < MIN_DENOMINATOR, exact_dilated, lambda: o_l_fast)

    x1, h2 = _out_proj(o_d, o_l, w_out[0].astype(BF16), x2, norm2_gain.reshape(1, D))
    act = _ffn_up(h2, w_gate_up[0].astype(BF16), conv_w[0], conv_b.reshape(1, -1))
    out = _ffn_down(act, w_down[0].astype(BF16), x1, final_gain.reshape(1, D))
    return out.reshape(B, S, D)
```

```python
import functools
import math

import numpy as np
import jax
import jax.numpy as jnp
from jax import lax
from jax.experimental import pallas as pl
from jax.experimental.pallas import tpu as pltpu

F32 = jnp.float32
BF16 = jnp.bfloat16

HEAD_DIM = 128
N_DIFF_HEADS = 4
DIFF_V_DIM = 2 * HEAD_DIM
N_DIL_HEADS = 8
DIFF_QK_COLS = N_DIFF_HEADS * 2 * HEAD_DIM
DIFF_WIDTH = N_DIFF_HEADS * DIFF_V_DIM
DIL_WIDTH = N_DIL_HEADS * HEAD_DIM
DILATED_PATTERNS = ((128, 1), (512, 4), (2048, 16))
N_REL_BUCKETS = 32
REL_MAX_DISTANCE = 1024
NORM_EPS = 1e-6
SUBLN_EPS = 1e-5
NEG_INF = -1e30
LOG2E = math.log2(math.e)
LAM_INIT = 0.8 - 0.6 * math.exp(-0.3 * 0)

LANES = 128
BF16_SUBLANES = 16
DIFF_TILE = 1024
DIFF_QUERY_PANEL = 256
SHIFT_MARGIN = 1.0 + 2.0 ** -8
MIN_DENOMINATOR = 2.0 ** -60
DIL_CHUNK = 1024
DIL_HEAD_UNROLL = 4
VMEM_LIMIT = 56 * 1024 * 1024


def _bucket_breaks():
    nb = N_REL_BUCKETS // 2
    max_exact = nb // 2
    rel = np.arange(-2 * REL_MAX_DISTANCE, 2 * REL_MAX_DISTANCE + 1)
    n = np.abs(rel)
    pos = np.log(np.maximum(n, 1) / max_exact) / math.log(REL_MAX_DISTANCE / max_exact) * (nb - max_exact)
    large = np.minimum(max_exact + np.floor(pos).astype(np.int64), nb - 1)
    bucket = np.where(rel > 0, nb, 0) + np.where(n < max_exact, n, large)
    breaks = [(int(rel[i]), int(bucket[i])) for i in range(1, len(rel)) if bucket[i] != bucket[i - 1]]
    return int(bucket[0]), breaks


FIRST_BUCKET, BUCKET_BREAKS = _bucket_breaks()
LAST_BUCKET = BUCKET_BREAKS[-1][1]
FAR_DIST = max(-BUCKET_BREAKS[0][0] + 1, BUCKET_BREAKS[-1][0])


def _bias_from_rel(rel, tab_ref, col):
    val = jnp.full(rel.shape, tab_ref[FIRST_BUCKET, col], F32)
    for thr, b in BUCKET_BREAKS:
        val = jnp.where(rel >= thr, tab_ref[b, col], val)
    return val


def _inproj_kernel(x_ref, g_ref, w_ref, cs_ref, o_ref, qt_ref, vt_ref, *rest, q_tile, v_tile, dil_tile0,
                   dilations):
    cls_refs = rest[:len(dilations)]
    h_ref, acc_ref = rest[len(dilations):]
    j = pl.program_id(1)
    tm = x_ref.shape[0]

    @pl.when(j == 0)
    def _():
        x = x_ref[...]
        ms = jnp.mean(x * x, axis=-1, keepdims=True)
        h_ref[...] = (x * lax.rsqrt(ms + NORM_EPS) * g_ref[...]).astype(BF16)

    acc = jnp.dot(h_ref[...], w_ref[...], preferred_element_type=F32) * cs_ref[...]
    o_ref[...] = acc.astype(o_ref.dtype)

    @pl.when(j == q_tile)
    def _():
        qt_ref[...] = acc.T.astype(qt_ref.dtype)

    @pl.when(j == v_tile)
    def _():
        vt_ref[...] = acc.T.astype(vt_ref.dtype)

    @pl.when(j >= dil_tile0)
    def _():
        for cb in range(acc_ref.shape[0]):
            sl = slice(cb * LANES, (cb + 1) * LANES)
            acc_ref[cb] = acc[:, sl]
            for cls_ref, dil in zip(cls_refs, dilations):
                for c in range(dil):
                    cls_ref[c, :, sl] = acc_ref[cb, pl.ds(c, tm // dil, stride=dil), :].astype(cls_ref.dtype)


def _in_proj(x2, gain, w_bf, colscale, dilations, *, tm=512, tn=1024):
    S, D = x2.shape
    N = w_bf.shape[1]
    assert DIFF_QK_COLS == tn and DIFF_WIDTH == tn and DIL_WIDTH == tn, "outputs assume one column tile each"
    dil_tile0 = (2 * DIFF_QK_COLS + DIFF_WIDTH) // tn
    n_dil_tiles = N // tn - dil_tile0
    kern = functools.partial(_inproj_kernel, q_tile=0, v_tile=2 * DIFF_QK_COLS // tn, dil_tile0=dil_tile0,
                             dilations=dilations)
    cls_specs = [pl.BlockSpec((d, tm // d, tn), lambda i, j: (0, i, jnp.maximum(j - dil_tile0, 0)))
                 for d in dilations]
    cls_shapes = [jax.ShapeDtypeStruct((d, S // d, n_dil_tiles * tn), BF16) for d in dilations]
    return pl.pallas_call(
        kern,
        grid=(S // tm, N // tn),
        in_specs=[
            pl.BlockSpec((tm, D), lambda i, j: (i, 0)),
            pl.BlockSpec((1, D), lambda i, j: (0, 0)),
            pl.BlockSpec((D, tn), lambda i, j: (0, j)),
            pl.BlockSpec((1, tn), lambda i, j: (0, j)),
        ],
        out_specs=[
            pl.BlockSpec((tm, tn), lambda i, j: (i, j)),
            pl.BlockSpec((tn, tm), lambda i, j: (0, i)),
            pl.BlockSpec((tn, tm), lambda i, j: (0, i)),
        ] + cls_specs,
        out_shape=[
            jax.ShapeDtypeStruct((S, N), BF16),
            jax.ShapeDtypeStruct((DIFF_QK_COLS, S), BF16),
            jax.ShapeDtypeStruct((DIFF_WIDTH, S), BF16),
        ] + cls_shapes,
        scratch_shapes=[pltpu.VMEM((tm, D), BF16), pltpu.VMEM((tn // LANES, tm, LANES), F32)],
        compiler_params=pltpu.CompilerParams(
            dimension_semantics=("arbitrary", "arbitrary"), vmem_limit_bytes=VMEM_LIMIT),
        name="in_proj",
    )(x2, gain, w_bf, colscale)


def _diff_attn_kernel(tab_ref, q1t_ref, q2t_ref, k1_ref, k2_ref, vt_ref, lam_ref, gain_ref, o_ref,
                      bias_ref, m_ref, l_ref, acc_ref, p_ref, shift_ref, knorm_ref, prev_ref,
                      *, T, QP, n_near, n_tiles):
    h = pl.program_id(0)
    qi = pl.program_id(1)
    n_chains = 2 * (T // QP)
    qts = (q1t_ref, q2t_ref)
    ks = (k1_ref, k2_ref)

    def chain_of(i):
        qp = i // 2
        return i % 2, slice(qp * QP, (qp + 1) * QP)

    @pl.when(qi == 0)
    def _():
        for mi in range(2):
            def knorm(t, best, mi=mi):
                k = ks[mi][pl.ds(pl.multiple_of(t * T, T), T), :].astype(F32)
                return jnp.maximum(best, jnp.max(jnp.sum(k * k, axis=1, keepdims=True), axis=0, keepdims=True))
            knorm_ref[mi] = lax.fori_loop(0, n_tiles, knorm, jnp.zeros((1, 1), F32))

        x = lax.broadcasted_iota(jnp.int32, (8, 2 * T), 1)
        x = jnp.where(x < T, x, x - 2 * T)
        for di, d in enumerate(range(-n_near, n_near + 1)):
            g = _bias_from_rel(d * T - x, tab_ref, h)
            base = jnp.broadcast_to(g[0:1, :], (LANES, 2 * T))
            for rb in range(T // LANES):
                blk = pltpu.roll(base, rb * LANES, 1, stride=1, stride_axis=0)
                bias_ref[di, rb * LANES:(rb + 1) * LANES, :] = blk[:, :T]

    c_left = tab_ref[FIRST_BUCKET, h]
    c_right = tab_ref[LAST_BUCKET, h]

    bias_max = tab_ref[0, h]
    for b in range(1, N_REL_BUCKETS):
        bias_max = jnp.maximum(bias_max, tab_ref[b, h])

    for mi in range(2):
        q = qts[mi][...].astype(F32)
        qnorm2 = jnp.sum(q * q, axis=0, keepdims=True)
        shift_ref[mi] = jnp.sqrt(qnorm2 * knorm_ref[mi]) * SHIFT_MARGIN + bias_max
    l_ref[...] = jnp.zeros(l_ref.shape, F32)
    acc_ref[...] = jnp.zeros(acc_ref.shape, F32)
    p_ref[n_chains - 1] = jnp.zeros((T, QP), BF16)
    prev_ref[0] = 0

    def score_stage(i, k0, bias_di, bias_const):
        mi, qs = chain_of(i)
        s = jnp.dot(ks[mi][pl.ds(k0, T), :], qts[mi][:, qs], preferred_element_type=F32)
        if bias_di is not None:
            p = jnp.exp2(s + bias_ref[bias_di, :, qs] - shift_ref[mi, :, qs])
        else:
            p = jnp.exp2(s - (shift_ref[mi, :, qs] - bias_const))
        l_ref[mi, :, qs] += jnp.sum(p, axis=0, keepdims=True)
        p_ref[i] = p.astype(BF16)

    def value_stage(i, k0):
        mi, qs = chain_of(i)
        acc_ref[mi, :, qs] += jnp.dot(vt_ref[:, pl.ds(k0, T)], p_ref[i], preferred_element_type=F32)

    def tile(kt, bias_di, bias_const):
        k0 = pl.multiple_of(kt * T, T)
        pk0 = pl.multiple_of(prev_ref[0] * T, T)
        for i in range(n_chains):
            value_stage((i - 1) % n_chains, pk0 if i < 1 else k0)
            score_stage(i, k0, bias_di, bias_const)
        prev_ref[0] = kt

    lo = jnp.maximum(qi - n_near, 0)
    hi = jnp.minimum(qi + n_near + 1, n_tiles)

    @pl.loop(0, lo)
    def _(kt):
        tile(kt, None, c_left)

    for di, d in enumerate(range(-n_near, n_near + 1)):
        kt = qi + d

        @pl.when(jnp.logical_and(kt >= 0, kt < n_tiles))
        def _(di=di, kt=kt):
            tile(kt, di, None)

    @pl.loop(hi, n_tiles)
    def _(kt):
        tile(kt, None, c_right)

    value_stage(n_chains - 1, pl.multiple_of(prev_ref[0] * T, T))

    @pl.when(jnp.min(l_ref[...]) < MIN_DENOMINATOR)
    def _():
        m_ref[...] = jnp.full(m_ref.shape, -jnp.inf, F32)
        l_ref[...] = jnp.zeros(l_ref.shape, F32)
        acc_ref[...] = jnp.zeros(acc_ref.shape, F32)

        @pl.loop(0, n_tiles)
        def _(kt):
            k0 = pl.multiple_of(kt * T, T)
            d = kt - qi
            near = jnp.abs(d) <= n_near
            di = jnp.clip(d + n_near, 0, 2 * n_near)
            c_far = jnp.where(d < 0, c_left, c_right)
            for i in range(n_chains):
                mi, qs = chain_of(i)
                s = jnp.dot(ks[mi][pl.ds(k0, T), :], qts[mi][:, qs], preferred_element_type=F32)
                s = s + jnp.where(near, bias_ref[di, :, qs], c_far)
                m = m_ref[mi, :, qs]
                m_new = jnp.maximum(m, jnp.max(s, axis=0, keepdims=True))
                alpha = jnp.exp2(m - m_new)
                p = jnp.exp2(s - m_new)
                m_ref[mi, :, qs] = m_new
                l_ref[mi, :, qs] = alpha * l_ref[mi, :, qs] + jnp.sum(p, axis=0, keepdims=True)
                pv = jnp.dot(vt_ref[:, pl.ds(k0, T)], p.astype(BF16), preferred_element_type=F32)
                acc_ref[mi, :, qs] = alpha * acc_ref[mi, :, qs] + pv

    lam = lam_ref[0, 0]
    o = acc_ref[0] / l_ref[0] - lam * (acc_ref[1] / l_ref[1])
    ms = jnp.mean(o * o, axis=0, keepdims=True)
    o = o * lax.rsqrt(ms + SUBLN_EPS) * (gain_ref[...] * (1.0 - LAM_INIT))
    o_ref[...] = o.T.astype(o_ref.dtype)


def _lambda_kernel(q1_ref, k1_ref, q2_ref, k2_ref, o_ref):
    a = jnp.sum(q1_ref[...] * k1_ref[...], axis=-1, keepdims=True)
    b = jnp.sum(q2_ref[...] * k2_ref[...], axis=-1, keepdims=True)
    o_ref[...] = jnp.exp(a) - jnp.exp(b) + LAM_INIT


def _lambda(lq1, lk1, lq2, lk2):
    return pl.pallas_call(
        _lambda_kernel, out_shape=jax.ShapeDtypeStruct((1, 1), F32), name="diff_lambda",
    )(lq1, lk1, lq2, lk2)


def _diff_attention(tab_diff, proj, qt_all, vt_all, lam, gain_col):
    S = proj.shape[0]
    T = DIFF_TILE
    n_tiles = S // T
    n_near = -(-(FAR_DIST - 1) // T)
    kern = functools.partial(_diff_attn_kernel, T=T, QP=DIFF_QUERY_PANEL, n_near=n_near, n_tiles=n_tiles)
    k_block0 = DIFF_QK_COLS // HEAD_DIM
    smem = pl.BlockSpec(memory_space=pltpu.SMEM)
    return pl.pallas_call(
        kern,
        grid=(N_DIFF_HEADS, n_tiles),
        in_specs=[
            smem,
            pl.BlockSpec((HEAD_DIM, T), lambda h, i: (2 * h, i)),
            pl.BlockSpec((HEAD_DIM, T), lambda h, i: (2 * h + 1, i)),
            pl.BlockSpec((S, HEAD_DIM), lambda h, i: (0, k_block0 + 2 * h), pipeline_mode=pl.Buffered(1)),
            pl.BlockSpec((S, HEAD_DIM), lambda h, i: (0, k_block0 + 2 * h + 1), pipeline_mode=pl.Buffered(1)),
            pl.BlockSpec((DIFF_V_DIM, S), lambda h, i: (h, 0), pipeline_mode=pl.Buffered(1)),
            smem,
            pl.BlockSpec((DIFF_V_DIM, 1), lambda h, i: (0, 0)),
        ],
        out_specs=pl.BlockSpec((T, DIFF_V_DIM), lambda h, i: (i, h)),
        out_shape=jax.ShapeDtypeStruct((S, DIFF_WIDTH), BF16),
        scratch_shapes=[pltpu.VMEM((2 * n_near + 1, T, T), F32),
                        pltpu.VMEM((2, 1, T), F32), pltpu.VMEM((2, 1, T), F32),
                        pltpu.VMEM((2, DIFF_V_DIM, T), F32),
                        pltpu.VMEM((2 * (T // DIFF_QUERY_PANEL), T, DIFF_QUERY_PANEL), BF16),
                        pltpu.VMEM((2, 1, T), F32), pltpu.VMEM((2, 1, 1), F32),
                        pltpu.SMEM((1,), jnp.int32)],
        compiler_params=pltpu.CompilerParams(
            dimension_semantics=("arbitrary", "arbitrary"), vmem_limit_bytes=VMEM_LIMIT),
        name="diff_attn",
    )(tab_diff, qt_all, qt_all, proj, proj, vt_all, lam, gain_col)


def _dilated_kernel(tab_ref, q_ref, kp_ref, km_ref, kn_ref, vp_ref, vm_ref, vn_ref, o_ref, lse_ref,
                    bias_ref, kx_ref, vx_ref, *, R, B, half, dilation, n_chunks):
    c = pl.program_id(0)
    n = pl.program_id(1)
    W = B + 2 * half
    nblk = R // B

    @pl.when(jnp.logical_and(c == 0, n == 0))
    def _():
        rows = 8
        col = lax.broadcasted_iota(jnp.int32, (rows, W), 1)
        row = lax.broadcasted_iota(jnp.int32, (rows, W), 0)
        for hh in range(N_DIL_HEADS):
            def fill(r, carry, hh=hh):
                r0 = pl.multiple_of(r * rows, rows)
                off = col - half - (row + r0)
                bias = _bias_from_rel(off * dilation, tab_ref, N_DIFF_HEADS + hh)
                base = jnp.where(jnp.abs(off) <= half, bias, NEG_INF * LOG2E)
                bias_ref[hh, 1, pl.ds(r0, rows), :] = base
                bias_ref[hh, 0, pl.ds(r0, rows), :] = jnp.where(col >= half, base, NEG_INF * LOG2E)
                bias_ref[hh, 2, pl.ds(r0, rows), :] = jnp.where(col < B + half, base, NEG_INF * LOG2E)
                return carry
            lax.fori_loop(0, B // rows, fill, 0)

    kx_ref[0:half, :] = kp_ref[...]
    kx_ref[half:half + R, :] = km_ref[...]
    kx_ref[half + R:, :] = kn_ref[...]
    vx_ref[0:half, :] = vp_ref[...]
    vx_ref[half:half + R, :] = vm_ref[...]
    vx_ref[half + R:, :] = vn_ref[...]

    def chain(hh, b):
        c0 = pl.multiple_of(hh * HEAD_DIM, HEAD_DIM)
        r0 = b * B
        var = 1
        if b == 0:
            var = jnp.where(n == 0, 0, var)
        if b == nblk - 1:
            var = jnp.where(n == n_chunks - 1, 2, var)
        q = q_ref[pl.ds(r0, B), pl.ds(c0, HEAD_DIM)]
        k = kx_ref[pl.ds(r0, W), pl.ds(c0, HEAD_DIM)]
        v = vx_ref[pl.ds(r0, W), pl.ds(c0, HEAD_DIM)]
        s = lax.dot_general(q, k, (((1,), (1,)), ((), ())), preferred_element_type=F32)
        s = s + bias_ref[hh, var]
        m = jnp.max(s, axis=-1, keepdims=True)
        e = jnp.exp2(s - m)
        den = jnp.sum(e, axis=-1, keepdims=True)
        o = jnp.dot(e.astype(BF16), v, preferred_element_type=F32) / den
        o_ref[pl.ds(r0, B), pl.ds(c0, HEAD_DIM)] = o.astype(o_ref.dtype)
        lse = m + jnp.log2(den)
        lse_ref[pl.ds(r0, B), pl.ds(c0, HEAD_DIM)] = jnp.broadcast_to(lse, (B, HEAD_DIM))

    @pl.loop(0, N_DIL_HEADS // DIL_HEAD_UNROLL)
    def _(hg):
        for u in range(DIL_HEAD_UNROLL):
            for b in range(nblk):
                chain(hg * DIL_HEAD_UNROLL + u, b)


def _dilated_pattern(tab, qkv, col_blk0, window, dilation, *, B=256):
    _, L, _ = qkv.shape
    R = min(DIL_CHUNK, L)
    half = window // (2 * dilation)
    assert L % R == 0 and R % B == 0 and half % BF16_SUBLANES == 0 and R % half == 0
    n_chunks = L // R
    q_blk, k_blk, v_blk = col_blk0, col_blk0 + 1, col_blk0 + 2
    hb = R // half
    n_hblk = L // half

    def main(blk):
        return pl.BlockSpec((None, R, DIL_WIDTH), lambda c, n: (c, n, blk))

    def prev(blk):
        return pl.BlockSpec((None, half, DIL_WIDTH), lambda c, n: (c, jnp.maximum(n * hb - 1, 0), blk))

    def nxt(blk):
        return pl.BlockSpec((None, half, DIL_WIDTH), lambda c, n: (c, jnp.minimum((n + 1) * hb, n_hblk - 1), blk))

    kern = functools.partial(_dilated_kernel, R=R, B=B, half=half, dilation=dilation, n_chunks=n_chunks)
    out_spec = pl.BlockSpec((None, R, DIL_WIDTH), lambda c, n: (c, n, 0))
    return pl.pallas_call(
        kern,
        grid=(dilation, n_chunks),
        in_specs=[pl.BlockSpec(memory_space=pltpu.SMEM),
                  main(q_blk), prev(k_blk), main(k_blk), nxt(k_blk), prev(v_blk), main(v_blk), nxt(v_blk)],
        out_specs=[out_spec, out_spec],
        out_shape=[jax.ShapeDtypeStruct((dilation, L, DIL_WIDTH), BF16),
                   jax.ShapeDtypeStruct((dilation, L, DIL_WIDTH), F32)],
        scratch_shapes=[pltpu.VMEM((N_DIL_HEADS, 3, B, B + 2 * half), F32),
                        pltpu.VMEM((R + 2 * half, DIL_WIDTH), BF16),
                        pltpu.VMEM((R + 2 * half, DIL_WIDTH), BF16)],
        compiler_params=pltpu.CompilerParams(
            dimension_semantics=("arbitrary", "arbitrary"), vmem_limit_bytes=VMEM_LIMIT),
        name=f"dilated_d{dilation}",
    )(tab, qkv, qkv, qkv, qkv, qkv, qkv, qkv)


def _combine_kernel(*refs, dilations):
    n = len(dilations)
    o_refs, l_refs = refs[:n], refs[n:2 * n]
    g_ref, out_ref = refs[2 * n], refs[2 * n + 1]
    scratch = refs[2 * n + 2:]
    tm = out_ref.shape[0]

    for hh in range(N_DIL_HEADS):
        sl = slice(hh * HEAD_DIM, (hh + 1) * HEAD_DIM)
        outs, lses = [], []
        si = 0
        for o_ref, l_ref, dil in zip(o_refs, l_refs, dilations):
            if dil == 1:
                outs.append(o_ref[0, :, sl].astype(F32))
                lses.append(l_ref[0, :, sl])
                continue
            os_ref, ls_ref = scratch[si], scratch[si + 1]
            si += 2
            for c in range(dil):
                os_ref[hh, pl.ds(c, tm // dil, stride=dil), :] = o_ref[c, :, sl].astype(F32)
                ls_ref[hh, pl.ds(c, tm // dil, stride=dil), :] = l_ref[c, :, sl]
            outs.append(os_ref[hh])
            lses.append(ls_ref[hh])

        m = functools.reduce(jnp.maximum, lses)
        ws = [jnp.exp2(l - m) for l in lses]
        tot = functools.reduce(lambda a, b: a + b, ws)
        oh = functools.reduce(lambda a, b: a + b, [(w / tot) * op for w, op in zip(ws, outs)])
        ms = jnp.mean(oh * oh, axis=-1, keepdims=True)
        out_ref[:, sl] = (oh * lax.rsqrt(ms + NORM_EPS) * g_ref[:, sl]).astype(out_ref.dtype)


def _combine(outs, lses, gain, dilations, *, tm=512):
    S = outs[0].shape[0] * outs[0].shape[1]
    specs = [pl.BlockSpec((d, tm // d, DIL_WIDTH), lambda i: (0, i, 0)) for d in dilations]
    n_scr = sum(1 for d in dilations if d != 1)
    return pl.pallas_call(
        functools.partial(_combine_kernel, dilations=dilations),
        grid=(S // tm,),
        in_specs=specs + specs + [pl.BlockSpec((1, DIL_WIDTH), lambda i: (0, 0))],
        scratch_shapes=[pltpu.VMEM((N_DIL_HEADS, tm, HEAD_DIM), F32)] * (2 * n_scr),
        out_specs=pl.BlockSpec((tm, DIL_WIDTH), lambda i: (i, 0)),
        out_shape=jax.ShapeDtypeStruct((S, DIL_WIDTH), BF16),
        compiler_params=pltpu.CompilerParams(
            dimension_semantics=("arbitrary",), vmem_limit_bytes=VMEM_LIMIT),
        name="dilated_combine",
    )(*outs, *lses, gain)


def _key_norm2_kernel(k_ref, o_ref):
    @pl.when(pl.program_id(0) == 0)
    def _():
        o_ref[...] = jnp.zeros(o_ref.shape, F32)

    k = k_ref[...].astype(F32)
    for hh in range(N_DIL_HEADS):
        kh = k[:, hh * HEAD_DIM:(hh + 1) * HEAD_DIM]
        n2 = jnp.max(jnp.sum(kh * kh, axis=1, keepdims=True), axis=0, keepdims=True)
        o_ref[hh:hh + 1, :] = jnp.maximum(o_ref[hh:hh + 1, :], jnp.broadcast_to(n2, (1, LANES)))


def _key_norm2(proj, k_blk, *, tm=1024):
    S = proj.shape[0]
    return pl.pallas_call(
        _key_norm2_kernel,
        grid=(S // tm,),
        in_specs=[pl.BlockSpec((tm, DIL_WIDTH), lambda i: (i, k_blk))],
        out_specs=pl.BlockSpec((N_DIL_HEADS, LANES), lambda i: (0, 0)),
        out_shape=jax.ShapeDtypeStruct((N_DIL_HEADS, LANES), F32),
        compiler_params=pltpu.CompilerParams(dimension_semantics=("arbitrary",), vmem_limit_bytes=VMEM_LIMIT),
        name="dilated_key_norm",
    )(proj)


def _dilated_fast_kernel(tab_ref, kn2_ref, q_ref, kp_ref, km_ref, kn_ref, vp_ref, vm_ref, vn_ref, num_ref, den_ref,
                         bias_ref, kx_ref, vx_ref, p_ref, *, R, B, half, dilation, n_chunks):
    c = pl.program_id(0)
    n = pl.program_id(1)
    W = B + 2 * half
    nblk = R // B
    VW = 2 * HEAD_DIM

    @pl.when(jnp.logical_and(c == 0, n == 0))
    def _():
        rows = 8
        col = lax.broadcasted_iota(jnp.int32, (rows, W), 1)
        row = lax.broadcasted_iota(jnp.int32, (rows, W), 0)
        for hh in range(N_DIL_HEADS):
            def fill(r, carry, hh=hh):
                r0 = pl.multiple_of(r * rows, rows)
                off = col - half - (row + r0)
                bias = _bias_from_rel(off * dilation, tab_ref, N_DIFF_HEADS + hh)
                base = jnp.where(jnp.abs(off) <= half, bias, NEG_INF * LOG2E)
                bias_ref[hh, 1, pl.ds(r0, rows), :] = base
                bias_ref[hh, 0, pl.ds(r0, rows), :] = jnp.where(col >= half, base, NEG_INF * LOG2E)
                bias_ref[hh, 2, pl.ds(r0, rows), :] = jnp.where(col < B + half, base, NEG_INF * LOG2E)
                return carry
            lax.fori_loop(0, B // rows, fill, 0)
        vx_ref[...] = jnp.ones(vx_ref.shape, BF16)

    kx_ref[0:half, :] = kp_ref[...]
    kx_ref[half:half + R, :] = km_ref[...]
    kx_ref[half + R:, :] = kn_ref[...]
    for hh in range(N_DIL_HEADS):
        src = slice(hh * HEAD_DIM, (hh + 1) * HEAD_DIM)
        dst = slice(hh * VW, hh * VW + HEAD_DIM)
        vx_ref[0:half, dst] = vp_ref[:, src]
        vx_ref[half:half + R, dst] = vm_ref[:, src]
        vx_ref[half + R:, dst] = vn_ref[:, src]

    n_slots = p_ref.shape[0]
    bias_max = []
    for hh in range(N_DIL_HEADS):
        bm = tab_ref[0, N_DIFF_HEADS + hh]
        for bk in range(1, N_REL_BUCKETS):
            bm = jnp.maximum(bm, tab_ref[bk, N_DIFF_HEADS + hh])
        bias_max.append(bm)

    def score_stage(hh, b, slot):
        cs = slice(hh * HEAD_DIM, (hh + 1) * HEAD_DIM)
        r0 = b * B
        var = 1
        if b == 0:
            var = jnp.where(n == 0, 0, var)
        if b == nblk - 1:
            var = jnp.where(n == n_chunks - 1, 2, var)
        q = q_ref[r0:r0 + B, cs]
        qf = q.astype(F32)
        qn2 = jnp.sum(qf * qf, axis=1, keepdims=True)
        shift = jnp.sqrt(qn2 * kn2_ref[hh:hh + 1, 0:1]) * SHIFT_MARGIN + bias_max[hh]
        s = lax.dot_general(q, kx_ref[r0:r0 + W, cs], (((1,), (1,)), ((), ())), preferred_element_type=F32)
        p_ref[slot] = jnp.exp2(s + bias_ref[hh, var] - shift).astype(BF16)

    def value_stage(hh, b, slot):
        cs = slice(hh * HEAD_DIM, (hh + 1) * HEAD_DIM)
        r0 = b * B
        nd = jnp.dot(p_ref[slot], vx_ref[r0:r0 + W, hh * VW:(hh + 1) * VW], preferred_element_type=F32)
        num_ref[r0:r0 + B, cs] = nd[:, :HEAD_DIM].astype(num_ref.dtype)
        den_ref[r0:r0 + B, cs] = nd[:, HEAD_DIM:].astype(den_ref.dtype)

    chains = [(hh, b) for hh in range(N_DIL_HEADS) for b in range(nblk)]
    for i, (hh, b) in enumerate(chains):
        if i > 0:
            value_stage(*chains[i - 1], (i - 1) % n_slots)
        score_stage(hh, b, i % n_slots)
    value_stage(*chains[-1], (len(chains) - 1) % n_slots)


def _dilated_fast(tab, kn2, qkv, col_blk0, window, dilation, *, B=128):
    _, L, _ = qkv.shape
    R = min(DIL_CHUNK, L)
    half = window // (2 * dilation)
    assert L % R == 0 and R % B == 0 and half % BF16_SUBLANES == 0 and R % half == 0
    n_chunks = L // R
    q_blk, k_blk, v_blk = col_blk0, col_blk0 + 1, col_blk0 + 2
    hb = R // half
    n_hblk = L // half

    def main(blk):
        return pl.BlockSpec((None, R, DIL_WIDTH), lambda c, n: (c, n, blk))

    def prev(blk):
        return pl.BlockSpec((None, half, DIL_WIDTH), lambda c, n: (c, jnp.maximum(n * hb - 1, 0), blk))

    def nxt(blk):
        return pl.BlockSpec((None, half, DIL_WIDTH), lambda c, n: (c, jnp.minimum((n + 1) * hb, n_hblk - 1), blk))

    kern = functools.partial(_dilated_fast_kernel, R=R, B=B, half=half, dilation=dilation, n_chunks=n_chunks)
    out_spec = pl.BlockSpec((None, R, DIL_WIDTH), lambda c, n: (c, n, 0))
    return pl.pallas_call(
        kern,
        grid=(dilation, n_chunks),
        in_specs=[pl.BlockSpec(memory_space=pltpu.SMEM),
                  pl.BlockSpec((N_DIL_HEADS, LANES), lambda c, n: (0, 0)),
                  main(q_blk), prev(k_blk), main(k_blk), nxt(k_blk), prev(v_blk), main(v_blk), nxt(v_blk)],
        out_specs=[out_spec, out_spec],
        out_shape=[jax.ShapeDtypeStruct((dilation, L, DIL_WIDTH), BF16),
                   jax.ShapeDtypeStruct((dilation, L, DIL_WIDTH), BF16)],
        scratch_shapes=[pltpu.VMEM((N_DIL_HEADS, 3, B, B + 2 * half), F32),
                        pltpu.VMEM((R + 2 * half, DIL_WIDTH), BF16),
                        pltpu.VMEM((R + 2 * half, 2 * DIL_WIDTH), BF16),
                        pltpu.VMEM((4, B, B + 2 * half), BF16)],
        compiler_params=pltpu.CompilerParams(
            dimension_semantics=("arbitrary", "arbitrary"), vmem_limit_bytes=VMEM_LIMIT),
        name=f"dilated_fast_d{dilation}",
    )(tab, kn2, qkv, qkv, qkv, qkv, qkv, qkv, qkv)


def _combine_fast_kernel(*refs, dilations):
    n = len(dilations)
    n_refs, d_refs = refs[:n], refs[n:2 * n]
    g_ref, out_ref, dmin_ref = refs[2 * n], refs[2 * n + 1], refs[2 * n + 2]
    scratch = refs[2 * n + 3:]
    tm = out_ref.shape[0]

    for hh in range(N_DIL_HEADS):
        sl = slice(hh * HEAD_DIM, (hh + 1) * HEAD_DIM)
        nums, dens = [], []
        si = 0
        for n_ref, d_ref, dil in zip(n_refs, d_refs, dilations):
            if dil == 1:
                nums.append(n_ref[0, :, sl].astype(F32))
                dens.append(d_ref[0, :, sl].astype(F32))
                continue
            ns_ref, ds_ref = scratch[si], scratch[si + 1]
            si += 2
            for c in range(dil):
                ns_ref[hh, pl.ds(c, tm // dil, stride=dil), :] = n_ref[c, :, sl].astype(F32)
                ds_ref[hh, pl.ds(c, tm // dil, stride=dil), :] = d_ref[c, :, sl].astype(F32)
            nums.append(ns_ref[hh])
            dens.append(ds_ref[hh])

        den = functools.reduce(lambda a, b: a + b, dens)
        oh = functools.reduce(lambda a, b: a + b, nums) / den
        ms = jnp.mean(oh * oh, axis=-1, keepdims=True)
        out_ref[:, sl] = (oh * lax.rsqrt(ms + NORM_EPS) * g_ref[:, sl]).astype(out_ref.dtype)
        dmin_ref[0, hh:hh + 1, :] = jnp.min(den, axis=0, keepdims=True)


def _combine_fast(nums, dens, gain, dilations, *, tm=512):
    S = nums[0].shape[0] * nums[0].shape[1]
    specs = [pl.BlockSpec((d, tm // d, DIL_WIDTH), lambda i: (0, i, 0)) for d in dilations]
    n_scr = sum(1 for d in dilations if d != 1)
    return pl.pallas_call(
        functools.partial(_combine_fast_kernel, dilations=dilations),
        grid=(S // tm,),
        in_specs=specs + specs + [pl.BlockSpec((1, DIL_WIDTH), lambda i: (0, 0))],
        scratch_shapes=[pltpu.VMEM((N_DIL_HEADS, tm, HEAD_DIM), F32)] * (2 * n_scr),
        out_specs=[pl.BlockSpec((tm, DIL_WIDTH), lambda i: (i, 0)),
                   pl.BlockSpec((1, N_DIL_HEADS, LANES), lambda i: (i, 0, 0))],
        out_shape=[jax.ShapeDtypeStruct((S, DIL_WIDTH), BF16),
                   jax.ShapeDtypeStruct((S // tm, N_DIL_HEADS, LANES), F32)],
        compiler_params=pltpu.CompilerParams(
            dimension_semantics=("arbitrary",), vmem_limit_bytes=VMEM_LIMIT),
        name="dilated_combine_fast",
    )(*nums, *dens, gain)


def _outproj_kernel(od_ref, ol_ref, wd_ref, wl_ref, x_ref, g_ref, x1_ref, h2_ref):
    acc = jnp.dot(od_ref[...], wd_ref[...], preferred_element_type=F32)
    acc = acc + jnp.dot(ol_ref[...], wl_ref[...], preferred_element_type=F32)
    x1 = x_ref[...] + acc
    x1_ref[...] = x1
    ms = jnp.mean(x1 * x1, axis=-1, keepdims=True)
    h2_ref[...] = (x1 * lax.rsqrt(ms + NORM_EPS) * g_ref[...]).astype(h2_ref.dtype)


def _out_proj(o_d, o_l, w_bf, x2, gain, *, tm=512):
    S, D = x2.shape
    return pl.pallas_call(
        _outproj_kernel,
        grid=(S // tm,),
        in_specs=[
            pl.BlockSpec((tm, DIFF_WIDTH), lambda i: (i, 0)),
            pl.BlockSpec((tm, DIL_WIDTH), lambda i: (i, 0)),
            pl.BlockSpec((DIFF_WIDTH, D), lambda i: (0, 0)),
            pl.BlockSpec((DIL_WIDTH, D), lambda i: (1, 0)),
            pl.BlockSpec((tm, D), lambda i: (i, 0)),
            pl.BlockSpec((1, D), lambda i: (0, 0)),
        ],
        out_specs=[pl.BlockSpec((tm, D), lambda i: (i, 0)), pl.BlockSpec((tm, D), lambda i: (i, 0))],
        out_shape=[jax.ShapeDtypeStruct((S, D), F32), jax.ShapeDtypeStruct((S, D), BF16)],
        compiler_params=pltpu.CompilerParams(
            dimension_semantics=("arbitrary",), vmem_limit_bytes=VMEM_LIMIT),
        name="out_proj",
    )(o_d, o_l, w_bf, w_bf, x2, gain)


def _ffn_up_kernel(hm_ref, hp_ref, hn_ref, wg_ref, wu_ref, cw_ref, cb_ref, o_ref, lhs_ref, *, tm, n_row_tiles):
    i = pl.program_id(0)
    j = pl.program_id(1)
    halo = BF16_SUBLANES

    @pl.when(j == 0)
    def _():
        lhs_ref[0:halo, :] = jnp.where(i == 0, jnp.zeros_like(hp_ref[...]), hp_ref[...])
        lhs_ref[halo:halo + tm, :] = hm_ref[...]
        lhs_ref[halo + tm:, :] = jnp.where(i == n_row_tiles - 1, jnp.zeros_like(hn_ref[...]), hn_ref[...])

    g = jnp.dot(lhs_ref[...], wg_ref[...], preferred_element_type=F32)
    u = jnp.dot(lhs_ref[halo:halo + tm, :], wu_ref[...], preferred_element_type=F32)
    rows = tm + 2 * halo
    g_prev = pltpu.roll(g, 1, axis=0)
    g_next = pltpu.roll(g, rows - 1, axis=0)
    y = cw_ref[0:1, :] * g_prev + cw_ref[1:2, :] * g + cw_ref[2:3, :] * g_next + cb_ref[...]
    y = y[halo:halo + tm, :]
    act = y * (1.0 / (1.0 + jnp.exp(-y))) * u
    o_ref[...] = act.astype(o_ref.dtype)


def _ffn_up(h2, w_bf, conv_w, conv_b, *, tm=1024, tn=512):
    S, D = h2.shape
    d_ff = conv_w.shape[1]
    assert d_ff % tn == 0
    nj = d_ff // tn
    ni = S // tm
    hb = tm // BF16_SUBLANES
    n_hblk = S // BF16_SUBLANES
    kern = functools.partial(_ffn_up_kernel, tm=tm, n_row_tiles=ni)
    return pl.pallas_call(
        kern,
        grid=(ni, nj),
        in_specs=[
            pl.BlockSpec((tm, D), lambda i, j: (i, 0)),
            pl.BlockSpec((BF16_SUBLANES, D), lambda i, j: (jnp.maximum(i * hb - 1, 0), 0)),
            pl.BlockSpec((BF16_SUBLANES, D), lambda i, j: (jnp.minimum((i + 1) * hb, n_hblk - 1), 0)),
            pl.BlockSpec((D, tn), lambda i, j: (0, j)),
            pl.BlockSpec((D, tn), lambda i, j: (0, nj + j)),
            pl.BlockSpec((3, tn), lambda i, j: (0, j)),
            pl.BlockSpec((1, tn), lambda i, j: (0, j)),
        ],
        out_specs=pl.BlockSpec((tm, tn), lambda i, j: (i, j)),
        out_shape=jax.ShapeDtypeStruct((S, d_ff), BF16),
        scratch_shapes=[pltpu.VMEM((tm + 2 * BF16_SUBLANES, D), BF16)],
        compiler_params=pltpu.CompilerParams(
            dimension_semantics=("arbitrary", "arbitrary"), vmem_limit_bytes=VMEM_LIMIT),
        name="ffn_up",
    )(h2, h2, h2, w_bf, w_bf, conv_w, conv_b)


def _ffn_down_kernel(a_ref, w_ref, x1_ref, g_ref, o_ref, *, n_k):
    k = pl.program_id(1)

    @pl.when(k == 0)
    def _():
        o_ref[...] = x1_ref[...]

    o_ref[...] += jnp.dot(a_ref[...], w_ref[...], preferred_element_type=F32)

    @pl.when(k == n_k - 1)
    def _():
        y = o_ref[...]
        ms = jnp.mean(y * y, axis=-1, keepdims=True)
        o_ref[...] = y * lax.rsqrt(ms + NORM_EPS) * g_ref[...]


def _ffn_down(act, w_bf, x1, gain, *, tm=1024, tk=1408):
    S, d_ff = act.shape
    D = x1.shape[1]
    n_k = d_ff // tk
    kern = functools.partial(_ffn_down_kernel, n_k=n_k)
    return pl.pallas_call(
        kern,
        grid=(S // tm, n_k),
        in_specs=[
            pl.BlockSpec((tm, tk), lambda i, k: (i, k)),
            pl.BlockSpec((tk, D), lambda i, k: (k, 0)),
            pl.BlockSpec((tm, D), lambda i, k: (i, 0)),
            pl.BlockSpec((1, D), lambda i, k: (0, 0)),
        ],
        out_specs=pl.BlockSpec((tm, D), lambda i, k: (i, 0)),
        out_shape=jax.ShapeDtypeStruct((S, D), F32),
        compiler_params=pltpu.CompilerParams(
            dimension_semantics=("arbitrary", "arbitrary"), vmem_limit_bytes=VMEM_LIMIT),
        name="ffn_down",
    )(act, w_bf, x1, gain)


def kernel(x, norm1_gain, w_in, rel_bias_table, lambda_q1, lambda_k1, lambda_q2, lambda_k2,
           diff_subln_gain, dil_out_gain, w_out, norm2_gain, w_gate_up, conv_w, conv_b, w_down, final_gain):
    B, S, D = x.shape
    assert B == 1 and w_in.shape[0] == 1
    x2 = x.reshape(S, D)
    n_cols = w_in.shape[2]

    qscale = LOG2E / math.sqrt(HEAD_DIM)
    col = np.arange(n_cols)
    dil_q0 = 2 * DIFF_QK_COLS + DIFF_WIDTH
    is_q = (col < DIFF_QK_COLS) | ((col >= dil_q0) & (col < dil_q0 + DIL_WIDTH))
    colscale = jnp.asarray(np.where(is_q, qscale, 1.0).astype(np.float32)).reshape(1, n_cols)
    tab = rel_bias_table.astype(F32) * LOG2E

    regroup = tuple(d for _, d in DILATED_PATTERNS if d != 1)
    proj, qt_all, vt_all, *cls = _in_proj(x2, norm1_gain.reshape(1, D), w_in[0].astype(BF16), colscale, regroup)
    cls_by_dil = dict(zip(regroup, cls))

    lam = _lambda(lambda_q1.reshape(1, -1), lambda_k1.reshape(1, -1),
                  lambda_q2.reshape(1, -1), lambda_k2.reshape(1, -1))
    o_d = _diff_attention(tab, proj, qt_all, vt_all, lam, diff_subln_gain.reshape(-1, 1))

    dil_blk0 = (2 * DIFF_QK_COLS + DIFF_WIDTH) // DIL_WIDTH
    dilations = tuple(d for _, d in DILATED_PATTERNS)
    dil_gain = dil_out_gain.reshape(1, -1)

    def pattern_inputs(dilation):
        if dilation == 1:
            return proj.reshape(1, S, n_cols), dil_blk0
        return cls_by_dil[dilation], 0

    kn2 = _key_norm2(proj, dil_blk0 + 1)
    nums, dens = [], []
    for window, dilation in DILATED_PATTERNS:
        n_p, d_p = _dilated_fast(tab, kn2, *pattern_inputs(dilation), window, dilation)
        nums.append(n_p)
        dens.append(d_p)
    o_l_fast, den_min = _combine_fast(nums, dens, dil_gain, dilations)

    def exact_dilated():
        outs, lses = [], []
        for window, dilation in DILATED_PATTERNS:
            o_p, lse_p = _dilated_pattern(tab, *pattern_inputs(dilation), window, dilation)
            outs.append(o_p)
            lses.append(lse_p)
        return _combine(outs, lses, dil_gain, dilations)

    o_l = lax.cond(jnp.min(den_min) < MIN_DENOMINATOR, exact_dilated, lambda: o_l_fast)

    x1, h2 = _out_proj(o_d, o_l, w_out[0].astype(BF16), x2, norm2_gain.reshape(1, D))
    act = _ffn_up(h2, w_gate_up[0].astype(BF16), conv_w[0], conv_b.reshape(1, -1))
    out = _ffn_down(act, w_down[0].astype(BF16), x1, final_gain.reshape(1, D))
    return out.reshape(B, S, D)
```

```python
import functools
import math

import numpy as np
import jax
import jax.numpy as jnp
from jax import lax
from jax.experimental import pallas as pl
from jax.experimental.pallas import tpu as pltpu

F32 = jnp.float32
BF16 = jnp.bfloat16

HEAD_DIM = 128
N_DIFF_HEADS = 4
DIFF_V_DIM = 2 * HEAD_DIM
N_DIL_HEADS = 8
DIFF_QK_COLS = N_DIFF_HEADS * 2 * HEAD_DIM
DIFF_WIDTH = N_DIFF_HEADS * DIFF_V_DIM
DIL_WIDTH = N_DIL_HEADS * HEAD_DIM
DILATED_PATTERNS = ((128, 1), (512, 4), (2048, 16))
N_REL_BUCKETS = 32
REL_MAX_DISTANCE = 1024
NORM_EPS = 1e-6
SUBLN_EPS = 1e-5
NEG_INF = -1e30
LOG2E = math.log2(math.e)
LAM_INIT = 0.8 - 0.6 * math.exp(-0.3 * 0)

LANES = 128
BF16_SUBLANES = 16
DIFF_TILE = 1024
DIFF_QUERY_PANEL = 256
SHIFT_MARGIN = 1.0 + 2.0 ** -8
MIN_DENOMINATOR = 2.0 ** -60
DIL_CHUNK = 1024
DIL_HEAD_UNROLL = 4
VMEM_LIMIT = 56 * 1024 * 1024


def _bucket_breaks():
    nb = N_REL_BUCKETS // 2
    max_exact = nb // 2
    rel = np.arange(-2 * REL_MAX_DISTANCE, 2 * REL_MAX_DISTANCE + 1)
    n = np.abs(rel)
    pos = np.log(np.maximum(n, 1) / max_exact) / math.log(REL_MAX_DISTANCE / max_exact) * (nb - max_exact)
    large = np.minimum(max_exact + np.floor(pos).astype(np.int64), nb - 1)
    bucket = np.where(rel > 0, nb, 0) + np.where(n < max_exact, n, large)
    breaks = [(int(rel[i]), int(bucket[i])) for i in range(1, len(rel)) if bucket[i] != bucket[i - 1]]
    return int(bucket[0]), breaks


FIRST_BUCKET, BUCKET_BREAKS = _bucket_breaks()
LAST_BUCKET = BUCKET_BREAKS[-1][1]
FAR_DIST = max(-BUCKET_BREAKS[0][0] + 1, BUCKET_BREAKS[-1][0])


def _bias_from_rel(rel, tab_ref, col):
    val = jnp.full(rel.shape, tab_ref[FIRST_BUCKET, col], F32)
    for thr, b in BUCKET_BREAKS:
        val = jnp.where(rel >= thr, tab_ref[b, col], val)
    return val


def _inproj_kernel(x_ref, g_ref, w_ref, cs_ref, o_ref, qt_ref, vt_ref, *rest, tiles_per_group, dilations):
    cls_refs = rest[:len(dilations)]
    h_ref, acc_ref = rest[len(dilations):]
    j = pl.program_id(1)
    tm = x_ref.shape[0]
    tpg = tiles_per_group

    @pl.when(j == 0)
    def _():
        x = x_ref[...]
        ms = jnp.mean(x * x, axis=-1, keepdims=True)
        h_ref[...] = (x * lax.rsqrt(ms + NORM_EPS) * g_ref[...]).astype(BF16)

    acc = jnp.dot(h_ref[...], w_ref[...], preferred_element_type=F32) * cs_ref[...]

    @pl.when(j < tpg)
    def _():
        qt_ref[...] = acc.T.astype(qt_ref.dtype)

    @pl.when(jnp.logical_and(j >= 2 * tpg, j < 3 * tpg))
    def _():
        vt_ref[...] = acc.T.astype(vt_ref.dtype)

    @pl.when(jnp.logical_or(jnp.logical_and(j >= tpg, j < 2 * tpg), j >= 3 * tpg))
    def _():
        o_ref[...] = acc.astype(o_ref.dtype)

    @pl.when(j >= 3 * tpg)
    def _():
        for cb in range(acc_ref.shape[0]):
            sl = slice(cb * LANES, (cb + 1) * LANES)
            acc_ref[cb] = acc[:, sl]
            for cls_ref, dil in zip(cls_refs, dilations):
                for c in range(dil):
                    cls_ref[c, :, sl] = acc_ref[cb, pl.ds(c, tm // dil, stride=dil), :].astype(cls_ref.dtype)


def _in_proj(x2, gain, w_bf, colscale, dilations, *, tm=1024, tn=512):
    S, D = x2.shape
    N = w_bf.shape[1]
    group = DIFF_QK_COLS
    assert DIFF_WIDTH == group and DIL_WIDTH == group and N == 6 * group and group % tn == 0
    tpg = group // tn
    kern = functools.partial(_inproj_kernel, tiles_per_group=tpg, dilations=dilations)

    def kept_block(j):
        return jnp.where(j < 2 * tpg, jnp.maximum(j - tpg, 0), jnp.maximum(j - 2 * tpg, tpg))

    cls_specs = [pl.BlockSpec((d, tm // d, tn), lambda i, j: (0, i, jnp.maximum(j - 3 * tpg, 0)))
                 for d in dilations]
    cls_shapes = [jax.ShapeDtypeStruct((d, S // d, 3 * group), BF16) for d in dilations]
    return pl.pallas_call(
        kern,
        grid=(S // tm, N // tn),
        in_specs=[
            pl.BlockSpec((tm, D), lambda i, j: (i, 0)),
            pl.BlockSpec((1, D), lambda i, j: (0, 0)),
            pl.BlockSpec((D, tn), lambda i, j: (0, j)),
            pl.BlockSpec((1, tn), lambda i, j: (0, j)),
        ],
        out_specs=[
            pl.BlockSpec((tm, tn), lambda i, j: (i, kept_block(j))),
            pl.BlockSpec((tn, tm), lambda i, j: (jnp.minimum(j, tpg - 1), i)),
            pl.BlockSpec((tn, tm), lambda i, j: (jnp.clip(j - 2 * tpg, 0, tpg - 1), i)),
        ] + cls_specs,
        out_shape=[
            jax.ShapeDtypeStruct((S, 4 * group), BF16),
            jax.ShapeDtypeStruct((group, S), BF16),
            jax.ShapeDtypeStruct((group, S), BF16),
        ] + cls_shapes,
        scratch_shapes=[pltpu.VMEM((tm, D), BF16), pltpu.VMEM((tn // LANES, tm, LANES), F32)],
        compiler_params=pltpu.CompilerParams(
            dimension_semantics=("arbitrary", "arbitrary"), vmem_limit_bytes=VMEM_LIMIT),
        name="in_proj",
    )(x2, gain, w_bf, colscale)


def _diff_attn_kernel(tab_ref, q1t_ref, q2t_ref, k1_ref, k2_ref, vt_ref, lam_ref, gain_ref, o_ref,
                      bias_ref, m_ref, l_ref, acc_ref, p_ref, shift_ref, knorm_ref, prev_ref,
                      *, T, QP, n_near, n_tiles):
    h = pl.program_id(0)
    qi = pl.program_id(1)
    n_chains = 2 * (T // QP)
    qts = (q1t_ref, q2t_ref)
    ks = (k1_ref, k2_ref)

    def chain_of(i):
        qp = i // 2
        return i % 2, slice(qp * QP, (qp + 1) * QP)

    @pl.when(qi == 0)
    def _():
        for mi in range(2):
            def knorm(t, best, mi=mi):
                k = ks[mi][pl.ds(pl.multiple_of(t * T, T), T), :].astype(F32)
                return jnp.maximum(best, jnp.max(jnp.sum(k * k, axis=1, keepdims=True), axis=0, keepdims=True))
            knorm_ref[mi] = lax.fori_loop(0, n_tiles, knorm, jnp.zeros((1, 1), F32))

        x = lax.broadcasted_iota(jnp.int32, (8, 2 * T), 1)
        x = jnp.where(x < T, x, x - 2 * T)
        for di, d in enumerate(range(-n_near, n_near + 1)):
            g = _bias_from_rel(d * T - x, tab_ref, h)
            base = jnp.broadcast_to(g[0:1, :], (LANES, 2 * T))
            for rb in range(T // LANES):
                blk = pltpu.roll(base, rb * LANES, 1, stride=1, stride_axis=0)
                bias_ref[di, rb * LANES:(rb + 1) * LANES, :] = blk[:, :T]

    c_left = tab_ref[FIRST_BUCKET, h]
    c_right = tab_ref[LAST_BUCKET, h]

    bias_max = tab_ref[0, h]
    for b in range(1, N_REL_BUCKETS):
        bias_max = jnp.maximum(bias_max, tab_ref[b, h])

    for mi in range(2):
        q = qts[mi][...].astype(F32)
        qnorm2 = jnp.sum(q * q, axis=0, keepdims=True)
        shift_ref[mi] = jnp.sqrt(qnorm2 * knorm_ref[mi]) * SHIFT_MARGIN + bias_max
    l_ref[...] = jnp.zeros(l_ref.shape, F32)
    acc_ref[...] = jnp.zeros(acc_ref.shape, F32)
    p_ref[n_chains - 1] = jnp.zeros((T, QP), BF16)
    prev_ref[0] = 0

    def score_stage(i, k0, bias_di, bias_const):
        mi, qs = chain_of(i)
        s = jnp.dot(ks[mi][pl.ds(k0, T), :], qts[mi][:, qs], preferred_element_type=F32)
        if bias_di is not None:
            p = jnp.exp2(s + bias_ref[bias_di, :, qs] - shift_ref[mi, :, qs])
        else:
            p = jnp.exp2(s - (shift_ref[mi, :, qs] - bias_const))
        l_ref[mi, :, qs] += jnp.sum(p, axis=0, keepdims=True)
        p_ref[i] = p.astype(BF16)

    def value_stage(i, k0):
        mi, qs = chain_of(i)
        acc_ref[mi, :, qs] += jnp.dot(vt_ref[:, pl.ds(k0, T)], p_ref[i], preferred_element_type=F32)

    def tile(kt, bias_di, bias_const):
        k0 = pl.multiple_of(kt * T, T)
        pk0 = pl.multiple_of(prev_ref[0] * T, T)
        for i in range(n_chains):
            value_stage((i - 1) % n_chains, pk0 if i < 1 else k0)
            score_stage(i, k0, bias_di, bias_const)
        prev_ref[0] = kt

    lo = jnp.maximum(qi - n_near, 0)
    hi = jnp.minimum(qi + n_near + 1, n_tiles)

    @pl.loop(0, lo)
    def _(kt):
        tile(kt, None, c_left)

    for di, d in enumerate(range(-n_near, n_near + 1)):
        kt = qi + d

        @pl.when(jnp.logical_and(kt >= 0, kt < n_tiles))
        def _(di=di, kt=kt):
            tile(kt, di, None)

    @pl.loop(hi, n_tiles)
    def _(kt):
        tile(kt, None, c_right)

    value_stage(n_chains - 1, pl.multiple_of(prev_ref[0] * T, T))

    @pl.when(jnp.min(l_ref[...]) < MIN_DENOMINATOR)
    def _():
        m_ref[...] = jnp.full(m_ref.shape, -jnp.inf, F32)
        l_ref[...] = jnp.zeros(l_ref.shape, F32)
        acc_ref[...] = jnp.zeros(acc_ref.shape, F32)

        @pl.loop(0, n_tiles)
        def _(kt):
            k0 = pl.multiple_of(kt * T, T)
            d = kt - qi
            near = jnp.abs(d) <= n_near
            di = jnp.clip(d + n_near, 0, 2 * n_near)
            c_far = jnp.where(d < 0, c_left, c_right)
            for i in range(n_chains):
                mi, qs = chain_of(i)
                s = jnp.dot(ks[mi][pl.ds(k0, T), :], qts[mi][:, qs], preferred_element_type=F32)
                s = s + jnp.where(near, bias_ref[di, :, qs], c_far)
                m = m_ref[mi, :, qs]
                m_new = jnp.maximum(m, jnp.max(s, axis=0, keepdims=True))
                alpha = jnp.exp2(m - m_new)
                p = jnp.exp2(s - m_new)
                m_ref[mi, :, qs] = m_new
                l_ref[mi, :, qs] = alpha * l_ref[mi, :, qs] + jnp.sum(p, axis=0, keepdims=True)
                pv = jnp.dot(vt_ref[:, pl.ds(k0, T)], p.astype(BF16), preferred_element_type=F32)
                acc_ref[mi, :, qs] = alpha * acc_ref[mi, :, qs] + pv

    lam = lam_ref[0, 0]
    o = acc_ref[0] / l_ref[0] - lam * (acc_ref[1] / l_ref[1])
    ms = jnp.mean(o * o, axis=0, keepdims=True)
    o = o * lax.rsqrt(ms + SUBLN_EPS) * (gain_ref[...] * (1.0 - LAM_INIT))
    o_ref[...] = o.T.astype(o_ref.dtype)


def _lambda_kernel(q1_ref, k1_ref, q2_ref, k2_ref, o_ref):
    a = jnp.sum(q1_ref[...] * k1_ref[...], axis=-1, keepdims=True)
    b = jnp.sum(q2_ref[...] * k2_ref[...], axis=-1, keepdims=True)
    o_ref[...] = jnp.exp(a) - jnp.exp(b) + LAM_INIT


def _lambda(lq1, lk1, lq2, lk2):
    return pl.pallas_call(
        _lambda_kernel, out_shape=jax.ShapeDtypeStruct((1, 1), F32), name="diff_lambda",
    )(lq1, lk1, lq2, lk2)


def _diff_attention(tab_diff, proj, qt_all, vt_all, lam, gain_col):
    S = proj.shape[0]
    T = DIFF_TILE
    n_tiles = S // T
    n_near = -(-(FAR_DIST - 1) // T)
    kern = functools.partial(_diff_attn_kernel, T=T, QP=DIFF_QUERY_PANEL, n_near=n_near, n_tiles=n_tiles)
    k_block0 = 0
    smem = pl.BlockSpec(memory_space=pltpu.SMEM)
    return pl.pallas_call(
        kern,
        grid=(N_DIFF_HEADS, n_tiles),
        in_specs=[
            smem,
            pl.BlockSpec((HEAD_DIM, T), lambda h, i: (2 * h, i)),
            pl.BlockSpec((HEAD_DIM, T), lambda h, i: (2 * h + 1, i)),
            pl.BlockSpec((S, HEAD_DIM), lambda h, i: (0, k_block0 + 2 * h), pipeline_mode=pl.Buffered(1)),
            pl.BlockSpec((S, HEAD_DIM), lambda h, i: (0, k_block0 + 2 * h + 1), pipeline_mode=pl.Buffered(1)),
            pl.BlockSpec((DIFF_V_DIM, S), lambda h, i: (h, 0), pipeline_mode=pl.Buffered(1)),
            smem,
            pl.BlockSpec((DIFF_V_DIM, 1), lambda h, i: (0, 0)),
        ],
        out_specs=pl.BlockSpec((T, DIFF_V_DIM), lambda h, i: (i, h)),
        out_shape=jax.ShapeDtypeStruct((S, DIFF_WIDTH), BF16),
        scratch_shapes=[pltpu.VMEM((2 * n_near + 1, T, T), F32),
                        pltpu.VMEM((2, 1, T), F32), pltpu.VMEM((2, 1, T), F32),
                        pltpu.VMEM((2, DIFF_V_DIM, T), F32),
                        pltpu.VMEM((2 * (T // DIFF_QUERY_PANEL), T, DIFF_QUERY_PANEL), BF16),
                        pltpu.VMEM((2, 1, T), F32), pltpu.VMEM((2, 1, 1), F32),
                        pltpu.SMEM((1,), jnp.int32)],
        compiler_params=pltpu.CompilerParams(
            dimension_semantics=("arbitrary", "arbitrary"), vmem_limit_bytes=VMEM_LIMIT),
        name="diff_attn",
    )(tab_diff, qt_all, qt_all, proj, proj, vt_all, lam, gain_col)


def _dilated_kernel(tab_ref, q_ref, kp_ref, km_ref, kn_ref, vp_ref, vm_ref, vn_ref, o_ref, lse_ref,
                    bias_ref, kx_ref, vx_ref, *, R, B, half, dilation, n_chunks):
    c = pl.program_id(0)
    n = pl.program_id(1)
    W = B + 2 * half
    nblk = R // B

    @pl.when(jnp.logical_and(c == 0, n == 0))
    def _():
        rows = 8
        col = lax.broadcasted_iota(jnp.int32, (rows, W), 1)
        row = lax.broadcasted_iota(jnp.int32, (rows, W), 0)
        for hh in range(N_DIL_HEADS):
            def fill(r, carry, hh=hh):
                r0 = pl.multiple_of(r * rows, rows)
                off = col - half - (row + r0)
                bias = _bias_from_rel(off * dilation, tab_ref, N_DIFF_HEADS + hh)
                base = jnp.where(jnp.abs(off) <= half, bias, NEG_INF * LOG2E)
                bias_ref[hh, 1, pl.ds(r0, rows), :] = base
                bias_ref[hh, 0, pl.ds(r0, rows), :] = jnp.where(col >= half, base, NEG_INF * LOG2E)
                bias_ref[hh, 2, pl.ds(r0, rows), :] = jnp.where(col < B + half, base, NEG_INF * LOG2E)
                return carry
            lax.fori_loop(0, B // rows, fill, 0)

    kx_ref[0:half, :] = kp_ref[...]
    kx_ref[half:half + R, :] = km_ref[...]
    kx_ref[half + R:, :] = kn_ref[...]
    vx_ref[0:half, :] = vp_ref[...]
    vx_ref[half:half + R, :] = vm_ref[...]
    vx_ref[half + R:, :] = vn_ref[...]

    def chain(hh, b):
        c0 = pl.multiple_of(hh * HEAD_DIM, HEAD_DIM)
        r0 = b * B
        var = 1
        if b == 0:
            var = jnp.where(n == 0, 0, var)
        if b == nblk - 1:
            var = jnp.where(n == n_chunks - 1, 2, var)
        q = q_ref[pl.ds(r0, B), pl.ds(c0, HEAD_DIM)]
        k = kx_ref[pl.ds(r0, W), pl.ds(c0, HEAD_DIM)]
        v = vx_ref[pl.ds(r0, W), pl.ds(c0, HEAD_DIM)]
        s = lax.dot_general(q, k, (((1,), (1,)), ((), ())), preferred_element_type=F32)
        s = s + bias_ref[hh, var]
        m = jnp.max(s, axis=-1, keepdims=True)
        e = jnp.exp2(s - m)
        den = jnp.sum(e, axis=-1, keepdims=True)
        o = jnp.dot(e.astype(BF16), v, preferred_element_type=F32) / den
        o_ref[pl.ds(r0, B), pl.ds(c0, HEAD_DIM)] = o.astype(o_ref.dtype)
        lse = m + jnp.log2(den)
        lse_ref[pl.ds(r0, B), pl.ds(c0, HEAD_DIM)] = jnp.broadcast_to(lse, (B, HEAD_DIM))

    @pl.loop(0, N_DIL_HEADS // DIL_HEAD_UNROLL)
    def _(hg):
        for u in range(DIL_HEAD_UNROLL):
            for b in range(nblk):
                chain(hg * DIL_HEAD_UNROLL + u, b)


def _dilated_pattern(tab, qkv, col_blk0, window, dilation, *, B=256):
    _, L, _ = qkv.shape
    R = min(DIL_CHUNK, L)
    half = window // (2 * dilation)
    assert L % R == 0 and R % B == 0 and half % BF16_SUBLANES == 0 and R % half == 0
    n_chunks = L // R
    q_blk, k_blk, v_blk = col_blk0, col_blk0 + 1, col_blk0 + 2
    hb = R // half
    n_hblk = L // half

    def main(blk):
        return pl.BlockSpec((None, R, DIL_WIDTH), lambda c, n: (c, n, blk))

    def prev(blk):
        return pl.BlockSpec((None, half, DIL_WIDTH), lambda c, n: (c, jnp.maximum(n * hb - 1, 0), blk))

    def nxt(blk):
        return pl.BlockSpec((None, half, DIL_WIDTH), lambda c, n: (c, jnp.minimum((n + 1) * hb, n_hblk - 1), blk))

    kern = functools.partial(_dilated_kernel, R=R, B=B, half=half, dilation=dilation, n_chunks=n_chunks)
    out_spec = pl.BlockSpec((None, R, DIL_WIDTH), lambda c, n: (c, n, 0))
    return pl.pallas_call(
        kern,
        grid=(dilation, n_chunks),
        in_specs=[pl.BlockSpec(memory_space=pltpu.SMEM),
                  main(q_blk), prev(k_blk), main(k_blk), nxt(k_blk), prev(v_blk), main(v_blk), nxt(v_blk)],
        out_specs=[out_spec, out_spec],
        out_shape=[jax.ShapeDtypeStruct((dilation, L, DIL_WIDTH), BF16),
                   jax.ShapeDtypeStruct((dilation, L, DIL_WIDTH), F32)],
        scratch_shapes=[pltpu.VMEM((N_DIL_HEADS, 3, B, B + 2 * half), F32),
                        pltpu.VMEM((R + 2 * half, DIL_WIDTH), BF16),
                        pltpu.VMEM((R + 2 * half, DIL_WIDTH), BF16)],
        compiler_params=pltpu.CompilerParams(
            dimension_semantics=("arbitrary", "arbitrary"), vmem_limit_bytes=VMEM_LIMIT),
        name=f"dilated_d{dilation}",
    )(tab, qkv, qkv, qkv, qkv, qkv, qkv, qkv)


def _combine_kernel(*refs, dilations):
    n = len(dilations)
    o_refs, l_refs = refs[:n], refs[n:2 * n]
    g_ref, out_ref = refs[2 * n], refs[2 * n + 1]
    scratch = refs[2 * n + 2:]
    tm = out_ref.shape[0]

    for hh in range(N_DIL_HEADS):
        sl = slice(hh * HEAD_DIM, (hh + 1) * HEAD_DIM)
        outs, lses = [], []
        si = 0
        for o_ref, l_ref, dil in zip(o_refs, l_refs, dilations):
            if dil == 1:
                outs.append(o_ref[0, :, sl].astype(F32))
                lses.append(l_ref[0, :, sl])
                continue
            os_ref, ls_ref = scratch[si], scratch[si + 1]
            si += 2
            for c in range(dil):
                os_ref[hh, pl.ds(c, tm // dil, stride=dil), :] = o_ref[c, :, sl].astype(F32)
                ls_ref[hh, pl.ds(c, tm // dil, stride=dil), :] = l_ref[c, :, sl]
            outs.append(os_ref[hh])
            lses.append(ls_ref[hh])

        m = functools.reduce(jnp.maximum, lses)
        ws = [jnp.exp2(l - m) for l in lses]
        tot = functools.reduce(lambda a, b: a + b, ws)
        oh = functools.reduce(lambda a, b: a + b, [(w / tot) * op for w, op in zip(ws, outs)])
        ms = jnp.mean(oh * oh, axis=-1, keepdims=True)
        out_ref[:, sl] = (oh * lax.rsqrt(ms + NORM_EPS) * g_ref[:, sl]).astype(out_ref.dtype)


def _combine(outs, lses, gain, dilations, *, tm=512):
    S = outs[0].shape[0] * outs[0].shape[1]
    specs = [pl.BlockSpec((d, tm // d, DIL_WIDTH), lambda i: (0, i, 0)) for d in dilations]
    n_scr = sum(1 for d in dilations if d != 1)
    return pl.pallas_call(
        functools.partial(_combine_kernel, dilations=dilations),
        grid=(S // tm,),
        in_specs=specs + specs + [pl.BlockSpec((1, DIL_WIDTH), lambda i: (0, 0))],
        scratch_shapes=[pltpu.VMEM((N_DIL_HEADS, tm, HEAD_DIM), F32)] * (2 * n_scr),
        out_specs=pl.BlockSpec((tm, DIL_WIDTH), lambda i: (i, 0)),
        out_shape=jax.ShapeDtypeStruct((S, DIL_WIDTH), BF16),
        compiler_params=pltpu.CompilerParams(
            dimension_semantics=("arbitrary",), vmem_limit_bytes=VMEM_LIMIT),
        name="dilated_combine",
    )(*outs, *lses, gain)


def _key_norm2_kernel(k_ref, o_ref):
    @pl.when(pl.program_id(0) == 0)
    def _():
        o_ref[...] = jnp.zeros(o_ref.shape, F32)

    k = k_ref[...].astype(F32)
    for hh in range(N_DIL_HEADS):
        kh = k[:, hh * HEAD_DIM:(hh + 1) * HEAD_DIM]
        n2 = jnp.max(jnp.sum(kh * kh, axis=1, keepdims=True), axis=0, keepdims=True)
        o_ref[hh:hh + 1, :] = jnp.maximum(o_ref[hh:hh + 1, :], jnp.broadcast_to(n2, (1, LANES)))


def _key_norm2(proj, k_blk, *, tm=1024):
    S = proj.shape[0]
    return pl.pallas_call(
        _key_norm2_kernel,
        grid=(S // tm,),
        in_specs=[pl.BlockSpec((tm, DIL_WIDTH), lambda i: (i, k_blk))],
        out_specs=pl.BlockSpec((N_DIL_HEADS, LANES), lambda i: (0, 0)),
        out_shape=jax.ShapeDtypeStruct((N_DIL_HEADS, LANES), F32),
        compiler_params=pltpu.CompilerParams(dimension_semantics=("arbitrary",), vmem_limit_bytes=VMEM_LIMIT),
        name="dilated_key_norm",
    )(proj)


def _dilated_fast_kernel(tab_ref, kn2_ref, q_ref, kp_ref, km_ref, kn_ref, vp_ref, vm_ref, vn_ref, num_ref, den_ref,
                         bias_ref, kx_ref, vx_ref, p_ref, *, R, B, half, dilation, n_chunks):
    c = pl.program_id(0)
    n = pl.program_id(1)
    W = B + 2 * half
    nblk = R // B
    VW = 2 * HEAD_DIM

    @pl.when(jnp.logical_and(c == 0, n == 0))
    def _():
        rows = 8
        col = lax.broadcasted_iota(jnp.int32, (rows, W), 1)
        row = lax.broadcasted_iota(jnp.int32, (rows, W), 0)
        for hh in range(N_DIL_HEADS):
            def fill(r, carry, hh=hh):
                r0 = pl.multiple_of(r * rows, rows)
                off = col - half - (row + r0)
                bias = _bias_from_rel(off * dilation, tab_ref, N_DIFF_HEADS + hh)
                base = jnp.where(jnp.abs(off) <= half, bias, NEG_INF * LOG2E)
                bias_ref[hh, 1, pl.ds(r0, rows), :] = base
                bias_ref[hh, 0, pl.ds(r0, rows), :] = jnp.where(col >= half, base, NEG_INF * LOG2E)
                bias_ref[hh, 2, pl.ds(r0, rows), :] = jnp.where(col < B + half, base, NEG_INF * LOG2E)
                return carry
            lax.fori_loop(0, B // rows, fill, 0)
        vx_ref[...] = jnp.ones(vx_ref.shape, BF16)

    kx_ref[0:half, :] = kp_ref[...]
    kx_ref[half:half + R, :] = km_ref[...]
    kx_ref[half + R:, :] = kn_ref[...]
    for hh in range(N_DIL_HEADS):
        src = slice(hh * HEAD_DIM, (hh + 1) * HEAD_DIM)
        dst = slice(hh * VW, hh * VW + HEAD_DIM)
        vx_ref[0:half, dst] = vp_ref[:, src]
        vx_ref[half:half + R, dst] = vm_ref[:, src]
        vx_ref[half + R:, dst] = vn_ref[:, src]

    n_slots = p_ref.shape[0]
    bias_max = []
    for hh in range(N_DIL_HEADS):
        bm = tab_ref[0, N_DIFF_HEADS + hh]
        for bk in range(1, N_REL_BUCKETS):
            bm = jnp.maximum(bm, tab_ref[bk, N_DIFF_HEADS + hh])
        bias_max.append(bm)

    def score_stage(hh, b, slot):
        cs = slice(hh * HEAD_DIM, (hh + 1) * HEAD_DIM)
        r0 = b * B
        var = 1
        if b == 0:
            var = jnp.where(n == 0, 0, var)
        if b == nblk - 1:
            var = jnp.where(n == n_chunks - 1, 2, var)
        q = q_ref[r0:r0 + B, cs]
        qf = q.astype(F32)
        qn2 = jnp.sum(qf * qf, axis=1, keepdims=True)
        shift = jnp.sqrt(qn2 * kn2_ref[hh:hh + 1, 0:1]) * SHIFT_MARGIN + bias_max[hh]
        s = lax.dot_general(q, kx_ref[r0:r0 + W, cs], (((1,), (1,)), ((), ())), preferred_element_type=F32)
        p_ref[slot] = jnp.exp2(s + bias_ref[hh, var] - shift).astype(BF16)

    def value_stage(hh, b, slot):
        cs = slice(hh * HEAD_DIM, (hh + 1) * HEAD_DIM)
        r0 = b * B
        nd = jnp.dot(p_ref[slot], vx_ref[r0:r0 + W, hh * VW:(hh + 1) * VW], preferred_element_type=F32)
        num_ref[r0:r0 + B, cs] = nd[:, :HEAD_DIM].astype(num_ref.dtype)
        den_ref[r0:r0 + B, cs] = nd[:, HEAD_DIM:].astype(den_ref.dtype)

    chains = [(hh, b) for hh in range(N_DIL_HEADS) for b in range(nblk)]
    for i, (hh, b) in enumerate(chains):
        if i > 0:
            value_stage(*chains[i - 1], (i - 1) % n_slots)
        score_stage(hh, b, i % n_slots)
    value_stage(*chains[-1], (len(chains) - 1) % n_slots)


def _dilated_fast(tab, kn2, qkv, col_blk0, window, dilation, *, B=128):
    _, L, _ = qkv.shape
    R = min(DIL_CHUNK, L)
    half = window // (2 * dilation)
    assert L % R == 0 and R % B == 0 and half % BF16_SUBLANES == 0 and R % half == 0
    n_chunks = L // R
    q_blk, k_blk, v_blk = col_blk0, col_blk0 + 1, col_blk0 + 2
    hb = R // half
    n_hblk = L // half

    def main(blk):
        return pl.BlockSpec((None, R, DIL_WIDTH), lambda c, n: (c, n, blk))

    def prev(blk):
        return pl.BlockSpec((None, half, DIL_WIDTH), lambda c, n: (c, jnp.maximum(n * hb - 1, 0), blk))

    def nxt(blk):
        return pl.BlockSpec((None, half, DIL_WIDTH), lambda c, n: (c, jnp.minimum((n + 1) * hb, n_hblk - 1), blk))

    kern = functools.partial(_dilated_fast_kernel, R=R, B=B, half=half, dilation=dilation, n_chunks=n_chunks)
    out_spec = pl.BlockSpec((None, R, DIL_WIDTH), lambda c, n: (c, n, 0))
    return pl.pallas_call(
        kern,
        grid=(dilation, n_chunks),
        in_specs=[pl.BlockSpec(memory_space=pltpu.SMEM),
                  pl.BlockSpec((N_DIL_HEADS, LANES), lambda c, n: (0, 0)),
                  main(q_blk), prev(k_blk), main(k_blk), nxt(k_blk), prev(v_blk), main(v_blk), nxt(v_blk)],
        out_specs=[out_spec, out_spec],
        out_shape=[jax.ShapeDtypeStruct((dilation, L, DIL_WIDTH), BF16),
                   jax.ShapeDtypeStruct((dilation, L, DIL_WIDTH), BF16)],
        scratch_shapes=[pltpu.VMEM((N_DIL_HEADS, 3, B, B + 2 * half), F32),
                        pltpu.VMEM((R + 2 * half, DIL_WIDTH), BF16),
                        pltpu.VMEM((R + 2 * half, 2 * DIL_WIDTH), BF16),
                        pltpu.VMEM((4, B, B + 2 * half), BF16)],
        compiler_params=pltpu.CompilerParams(
            dimension_semantics=("arbitrary", "arbitrary"), vmem_limit_bytes=VMEM_LIMIT),
        name=f"dilated_fast_d{dilation}",
    )(tab, kn2, qkv, qkv, qkv, qkv, qkv, qkv, qkv)


def _combine_fast_kernel(*refs, dilations):
    n = len(dilations)
    n_refs, d_refs = refs[:n], refs[n:2 * n]
    g_ref, out_ref, dmin_ref = refs[2 * n], refs[2 * n + 1], refs[2 * n + 2]
    scratch = refs[2 * n + 3:]
    tm = out_ref.shape[0]

    for hh in range(N_DIL_HEADS):
        sl = slice(hh * HEAD_DIM, (hh + 1) * HEAD_DIM)
        nums, dens = [], []
        si = 0
        for n_ref, d_ref, dil in zip(n_refs, d_refs, dilations):
            if dil == 1:
                nums.append(n_ref[0, :, sl].astype(F32))
                dens.append(d_ref[0, :, sl].astype(F32))
                continue
            ns_ref, ds_ref = scratch[si], scratch[si + 1]
            si += 2
            for c in range(dil):
                ns_ref[hh, pl.ds(c, tm // dil, stride=dil), :] = n_ref[c, :, sl].astype(F32)
                ds_ref[hh, pl.ds(c, tm // dil, stride=dil), :] = d_ref[c, :, sl].astype(F32)
            nums.append(ns_ref[hh])
            dens.append(ds_ref[hh])

        den = functools.reduce(lambda a, b: a + b, dens)
        oh = functools.reduce(lambda a, b: a + b, nums) / den
        ms = jnp.mean(oh * oh, axis=-1, keepdims=True)
        out_ref[:, sl] = (oh * lax.rsqrt(ms + NORM_EPS) * g_ref[:, sl]).astype(out_ref.dtype)
        dmin_ref[0, hh:hh + 1, :] = jnp.min(den, axis=0, keepdims=True)


def _combine_fast(nums, dens, gain, dilations, *, tm=512):
    S = nums[0].shape[0] * nums[0].shape[1]
    specs = [pl.BlockSpec((d, tm // d, DIL_WIDTH), lambda i: (0, i, 0)) for d in dilations]
    n_scr = sum(1 for d in dilations if d != 1)
    return pl.pallas_call(
        functools.partial(_combine_fast_kernel, dilations=dilations),
        grid=(S // tm,),
        in_specs=specs + specs + [pl.BlockSpec((1, DIL_WIDTH), lambda i: (0, 0))],
        scratch_shapes=[pltpu.VMEM((N_DIL_HEADS, tm, HEAD_DIM), F32)] * (2 * n_scr),
        out_specs=[pl.BlockSpec((tm, DIL_WIDTH), lambda i: (i, 0)),
                   pl.BlockSpec((1, N_DIL_HEADS, LANES), lambda i: (i, 0, 0))],
        out_shape=[jax.ShapeDtypeStruct((S, DIL_WIDTH), BF16),
                   jax.ShapeDtypeStruct((S // tm, N_DIL_HEADS, LANES), F32)],
        compiler_params=pltpu.CompilerParams(
            dimension_semantics=("arbitrary",), vmem_limit_bytes=VMEM_LIMIT),
        name="dilated_combine_fast",
    )(*nums, *dens, gain)


def _outproj_kernel(od_ref, ol_ref, wd_ref, wl_ref, x_ref, g_ref, x1_ref, h2_ref):
    acc = jnp.dot(od_ref[...], wd_ref[...], preferred_element_type=F32)
    acc = acc + jnp.dot(ol_ref[...], wl_ref[...], preferred_element_type=F32)
    x1 = x_ref[...] + acc
    x1_ref[...] = x1
    ms = jnp.mean(x1 * x1, axis=-1, keepdims=True)
    h2_ref[...] = (x1 * lax.rsqrt(ms + NORM_EPS) * g_ref[...]).astype(h2_ref.dtype)


def _out_proj(o_d, o_l, w_bf, x2, gain, *, tm=512):
    S, D = x2.shape
    return pl.pallas_call(
        _outproj_kernel,
        grid=(S // tm,),
        in_specs=[
            pl.BlockSpec((tm, DIFF_WIDTH), lambda i: (i, 0)),
            pl.BlockSpec((tm, DIL_WIDTH), lambda i: (i, 0)),
            pl.BlockSpec((DIFF_WIDTH, D), lambda i: (0, 0)),
            pl.BlockSpec((DIL_WIDTH, D), lambda i: (1, 0)),
            pl.BlockSpec((tm, D), lambda i: (i, 0)),
            pl.BlockSpec((1, D), lambda i: (0, 0)),
        ],
        out_specs=[pl.BlockSpec((tm, D), lambda i: (i, 0)), pl.BlockSpec((tm, D), lambda i: (i, 0))],
        out_shape=[jax.ShapeDtypeStruct((S, D), F32), jax.ShapeDtypeStruct((S, D), BF16)],
        compiler_params=pltpu.CompilerParams(
            dimension_semantics=("arbitrary",), vmem_limit_bytes=VMEM_LIMIT),
        name="out_proj",
    )(o_d, o_l, w_bf, w_bf, x2, gain)


def _ffn_up_kernel(hm_ref, hp_ref, hn_ref, wg_ref, wu_ref, cw_ref, cb_ref, o_ref, lhs_ref, *, tm, n_row_tiles):
    i = pl.program_id(0)
    j = pl.program_id(1)
    halo = BF16_SUBLANES

    @pl.when(j == 0)
    def _():
        lhs_ref[0:halo, :] = jnp.where(i == 0, jnp.zeros_like(hp_ref[...]), hp_ref[...])
        lhs_ref[halo:halo + tm, :] = hm_ref[...]
        lhs_ref[halo + tm:, :] = jnp.where(i == n_row_tiles - 1, jnp.zeros_like(hn_ref[...]), hn_ref[...])

    g = jnp.dot(lhs_ref[...], wg_ref[...], preferred_element_type=F32)
    u = jnp.dot(lhs_ref[halo:halo + tm, :], wu_ref[...], preferred_element_type=F32)
    rows = tm + 2 * halo
    g_prev = pltpu.roll(g, 1, axis=0)
    g_next = pltpu.roll(g, rows - 1, axis=0)
    y = cw_ref[0:1, :] * g_prev + cw_ref[1:2, :] * g + cw_ref[2:3, :] * g_next + cb_ref[...]
    y = y[halo:halo + tm, :]
    act = y * (1.0 / (1.0 + jnp.exp(-y))) * u
    o_ref[...] = act.astype(o_ref.dtype)


def _ffn_up(h2, w_bf, conv_w, conv_b, *, tm=1024, tn=512):
    S, D = h2.shape
    d_ff = conv_w.shape[1]
    assert d_ff % tn == 0
    nj = d_ff // tn
    ni = S // tm
    hb = tm // BF16_SUBLANES
    n_hblk = S // BF16_SUBLANES
    kern = functools.partial(_ffn_up_kernel, tm=tm, n_row_tiles=ni)
    return pl.pallas_call(
        kern,
        grid=(ni, nj),
        in_specs=[
            pl.BlockSpec((tm, D), lambda i, j: (i, 0)),
            pl.BlockSpec((BF16_SUBLANES, D), lambda i, j: (jnp.maximum(i * hb - 1, 0), 0)),
            pl.BlockSpec((BF16_SUBLANES, D), lambda i, j: (jnp.minimum((i + 1) * hb, n_hblk - 1), 0)),
            pl.BlockSpec((D, tn), lambda i, j: (0, j)),
            pl.BlockSpec((D, tn), lambda i, j: (0, nj + j)),
            pl.BlockSpec((3, tn), lambda i, j: (0, j)),
            pl.BlockSpec((1, tn), lambda i, j: (0, j)),
        ],
        out_specs=pl.BlockSpec((tm, tn), lambda i, j: (i, j)),
        out_shape=jax.ShapeDtypeStruct((S, d_ff), BF16),
        scratch_shapes=[pltpu.VMEM((tm + 2 * BF16_SUBLANES, D), BF16)],
        compiler_params=pltpu.CompilerParams(
            dimension_semantics=("arbitrary", "arbitrary"), vmem_limit_bytes=VMEM_LIMIT),
        name="ffn_up",
    )(h2, h2, h2, w_bf, w_bf, conv_w, conv_b)


def _ffn_down_kernel(a_ref, w_ref, x1_ref, g_ref, o_ref, *, n_k):
    k = pl.program_id(1)

    @pl.when(k == 0)
    def _():
        o_ref[...] = x1_ref[...]

    o_ref[...] += jnp.dot(a_ref[...], w_ref[...], preferred_element_type=F32)

    @pl.when(k == n_k - 1)
    def _():
        y = o_ref[...]
        ms = jnp.mean(y * y, axis=-1, keepdims=True)
        o_ref[...] = y * lax.rsqrt(ms + NORM_EPS) * g_ref[...]


def _ffn_down(act, w_bf, x1, gain, *, tm=1024, tk=1408):
    S, d_ff = act.shape
    D = x1.shape[1]
    n_k = d_ff // tk
    kern = functools.partial(_ffn_down_kernel, n_k=n_k)
    return pl.pallas_call(
        kern,
        grid=(S // tm, n_k),
        in_specs=[
            pl.BlockSpec((tm, tk), lambda i, k: (i, k)),
            pl.BlockSpec((tk, D), lambda i, k: (k, 0)),
            pl.BlockSpec((tm, D), lambda i, k: (i, 0)),
            pl.BlockSpec((1, D), lambda i, k: (0, 0)),
        ],
        out_specs=pl.BlockSpec((tm, D), lambda i, k: (i, 0)),
        out_shape=jax.ShapeDtypeStruct((S, D), F32),
        compiler_params=pltpu.CompilerParams(
            dimension_semantics=("arbitrary", "arbitrary"), vmem_limit_bytes=VMEM_LIMIT),
        name="ffn_down",
    )(act, w_bf, x1, gain)


def kernel(x, norm1_gain, w_in, rel_bias_table, lambda_q1, lambda_k1, lambda_q2, lambda_k2,
           diff_subln_gain, dil_out_gain, w_out, norm2_gain, w_gate_up, conv_w, conv_b, w_down, final_gain):
    B, S, D = x.shape
    assert B == 1 and w_in.shape[0] == 1
    x2 = x.reshape(S, D)
    n_cols = w_in.shape[2]

    qscale = LOG2E / math.sqrt(HEAD_DIM)
    col = np.arange(n_cols)
    dil_q0 = 2 * DIFF_QK_COLS + DIFF_WIDTH
    is_q = (col < DIFF_QK_COLS) | ((col >= dil_q0) & (col < dil_q0 + DIL_WIDTH))
    colscale = jnp.asarray(np.where(is_q, qscale, 1.0).astype(np.float32)).reshape(1, n_cols)
    tab = rel_bias_table.astype(F32) * LOG2E

    regroup = tuple(d for _, d in DILATED_PATTERNS if d != 1)
    proj, qt_all, vt_all, *cls = _in_proj(x2, norm1_gain.reshape(1, D), w_in[0].astype(BF16), colscale, regroup)
    cls_by_dil = dict(zip(regroup, cls))

    lam = _lambda(lambda_q1.reshape(1, -1), lambda_k1.reshape(1, -1),
                  lambda_q2.reshape(1, -1), lambda_k2.reshape(1, -1))
    o_d = _diff_attention(tab, proj, qt_all, vt_all, lam, diff_subln_gain.reshape(-1, 1))

    dil_blk0 = 1
    dilations = tuple(d for _, d in DILATED_PATTERNS)
    dil_gain = dil_out_gain.reshape(1, -1)

    def pattern_inputs(dilation):
        if dilation == 1:
            return proj.reshape(1, S, proj.shape[1]), dil_blk0
        return cls_by_dil[dilation], 0

    kn2 = _key_norm2(proj, dil_blk0 + 1)
    nums, dens = [], []
    for window, dilation in DILATED_PATTERNS:
        n_p, d_p = _dilated_fast(tab, kn2, *pattern_inputs(dilation), window, dilation)
        nums.append(n_p)
        dens.append(d_p)
    o_l_fast, den_min = _combine_fast(nums, dens, dil_gain, dilations)

    def exact_dilated():
        outs, lses = [], []
        for window, dilation in DILATED_PATTERNS:
            o_p, lse_p = _dilated_pattern(tab, *pattern_inputs(dilation), window, dilation)
            outs.append(o_p)
            lses.append(lse_p)
        return _combine(outs, lses, dil_gain, dilations)

    o_l = lax.cond(jnp.min(den_min) < MIN_DENOMINATOR, exact_dilated, lambda: o_l_fast)

    x1, h2 = _out_proj(o_d, o_l, w_out[0].astype(BF16), x2, norm2_gain.reshape(1, D))
    act = _ffn_up(h2, w_gate_up[0].astype(BF16), conv_w[0], conv_b.reshape(1, -1))
    out = _ffn_down(act, w_down[0].astype(BF16), x1, final_gain.reshape(1, D))
    return out.reshape(B, S, D)
```

```python
import functools
import math

import numpy as np
import jax
import jax.numpy as jnp
from jax import lax
from jax.experimental import pallas as pl
from jax.experimental.pallas import tpu as pltpu

F32 = jnp.float32
BF16 = jnp.bfloat16

HEAD_DIM = 128
N_DIFF_HEADS = 4
DIFF_V_DIM = 2 * HEAD_DIM
N_DIL_HEADS = 8
DIFF_QK_COLS = N_DIFF_HEADS * 2 * HEAD_DIM
DIFF_WIDTH = N_DIFF_HEADS * DIFF_V_DIM
DIL_WIDTH = N_DIL_HEADS * HEAD_DIM
DILATED_PATTERNS = ((128, 1), (512, 4), (2048, 16))
N_REL_BUCKETS = 32
REL_MAX_DISTANCE = 1024
NORM_EPS = 1e-6
SUBLN_EPS = 1e-5
NEG_INF = -1e30
LOG2E = math.log2(math.e)
LAM_INIT = 0.8 - 0.6 * math.exp(-0.3 * 0)

LANES = 128
BF16_SUBLANES = 16
DIFF_TILE = 1024
DIFF_QUERY_PANEL = 256
SHIFT_MARGIN = 1.0 + 2.0 ** -8
MIN_DENOMINATOR = 2.0 ** -60
DIL_CHUNK = 1024
DIL_HEAD_UNROLL = 4
VMEM_LIMIT = 56 * 1024 * 1024


def _bucket_breaks():
    nb = N_REL_BUCKETS // 2
    max_exact = nb // 2
    rel = np.arange(-2 * REL_MAX_DISTANCE, 2 * REL_MAX_DISTANCE + 1)
    n = np.abs(rel)
    pos = np.log(np.maximum(n, 1) / max_exact) / math.log(REL_MAX_DISTANCE / max_exact) * (nb - max_exact)
    large = np.minimum(max_exact + np.floor(pos).astype(np.int64), nb - 1)
    bucket = np.where(rel > 0, nb, 0) + np.where(n < max_exact, n, large)
    breaks = [(int(rel[i]), int(bucket[i])) for i in range(1, len(rel)) if bucket[i] != bucket[i - 1]]
    return int(bucket[0]), breaks


FIRST_BUCKET, BUCKET_BREAKS = _bucket_breaks()
LAST_BUCKET = BUCKET_BREAKS[-1][1]
FAR_DIST = max(-BUCKET_BREAKS[0][0] + 1, BUCKET_BREAKS[-1][0])


def _bias_from_rel(rel, tab_ref, col):
    val = jnp.full(rel.shape, tab_ref[FIRST_BUCKET, col], F32)
    for thr, b in BUCKET_BREAKS:
        val = jnp.where(rel >= thr, tab_ref[b, col], val)
    return val


def _inproj_kernel(x_ref, g_ref, w_ref, cs_ref, o_ref, qt_ref, vt_ref, *rest, tiles_per_group, dilations):
    cls_refs = rest[:len(dilations)]
    h_ref, acc_ref = rest[len(dilations):]
    j = pl.program_id(1)
    tm = x_ref.shape[0]
    tpg = tiles_per_group

    @pl.when(j == 0)
    def _():
        x = x_ref[...]
        ms = jnp.mean(x * x, axis=-1, keepdims=True)
        h_ref[...] = (x * lax.rsqrt(ms + NORM_EPS) * g_ref[...]).astype(BF16)

    acc = jnp.dot(h_ref[...], w_ref[...], preferred_element_type=F32) * cs_ref[...]

    @pl.when(j < tpg)
    def _():
        qt_ref[...] = acc.T.astype(qt_ref.dtype)

    @pl.when(jnp.logical_and(j >= 2 * tpg, j < 3 * tpg))
    def _():
        vt_ref[...] = acc.T.astype(vt_ref.dtype)

    @pl.when(jnp.logical_or(jnp.logical_and(j >= tpg, j < 2 * tpg), j >= 3 * tpg))
    def _():
        o_ref[...] = acc.astype(o_ref.dtype)

    @pl.when(j >= 3 * tpg)
    def _():
        for cb in range(acc_ref.shape[0]):
            sl = slice(cb * LANES, (cb + 1) * LANES)
            acc_ref[cb] = acc[:, sl]
            for cls_ref, dil in zip(cls_refs, dilations):
                for c in range(dil):
                    cls_ref[c, :, sl] = acc_ref[cb, pl.ds(c, tm // dil, stride=dil), :].astype(cls_ref.dtype)


def _in_proj(x2, gain, w_bf, colscale, dilations, *, tm=512, tn=1024):
    S, D = x2.shape
    N = w_bf.shape[1]
    group = DIFF_QK_COLS
    assert DIFF_WIDTH == group and DIL_WIDTH == group and N == 6 * group and group % tn == 0
    tpg = group // tn
    kern = functools.partial(_inproj_kernel, tiles_per_group=tpg, dilations=dilations)

    def kept_block(j):
        return jnp.where(j < 2 * tpg, jnp.maximum(j - tpg, 0), jnp.maximum(j - 2 * tpg, tpg))

    cls_specs = [pl.BlockSpec((d, tm // d, tn), lambda i, j: (0, i, jnp.maximum(j - 3 * tpg, 0)))
                 for d in dilations]
    cls_shapes = [jax.ShapeDtypeStruct((d, S // d, 3 * group), BF16) for d in dilations]
    return pl.pallas_call(
        kern,
        grid=(S // tm, N // tn),
        in_specs=[
            pl.BlockSpec((tm, D), lambda i, j: (i, 0)),
            pl.BlockSpec((1, D), lambda i, j: (0, 0)),
            pl.BlockSpec((D, tn), lambda i, j: (0, j)),
            pl.BlockSpec((1, tn), lambda i, j: (0, j)),
        ],
        out_specs=[
            pl.BlockSpec((tm, tn), lambda i, j: (i, kept_block(j))),
            pl.BlockSpec((tn, tm), lambda i, j: (jnp.minimum(j, tpg - 1), i)),
            pl.BlockSpec((tn, tm), lambda i, j: (jnp.clip(j - 2 * tpg, 0, tpg - 1), i)),
        ] + cls_specs,
        out_shape=[
            jax.ShapeDtypeStruct((S, 4 * group), BF16),
            jax.ShapeDtypeStruct((group, S), BF16),
            jax.ShapeDtypeStruct((group, S), BF16),
        ] + cls_shapes,
        scratch_shapes=[pltpu.VMEM((tm, D), BF16), pltpu.VMEM((tn // LANES, tm, LANES), F32)],
        compiler_params=pltpu.CompilerParams(
            dimension_semantics=("arbitrary", "arbitrary"), vmem_limit_bytes=VMEM_LIMIT),
        name="in_proj",
    )(x2, gain, w_bf, colscale)


def _diff_attn_kernel(tab_ref, q1t_ref, q2t_ref, k1_ref, k2_ref, vt_ref, lam_ref, gain_ref, o_ref,
                      bias_ref, m_ref, l_ref, acc_ref, p_ref, shift_ref, knorm_ref, prev_ref,
                      *, T, QP, n_near, n_tiles):
    h = pl.program_id(0)
    qi = pl.program_id(1)
    n_chains = 2 * (T // QP)
    qts = (q1t_ref, q2t_ref)
    ks = (k1_ref, k2_ref)

    def chain_of(i):
        qp = i // 2
        return i % 2, slice(qp * QP, (qp + 1) * QP)

    @pl.when(qi == 0)
    def _():
        for mi in range(2):
            def knorm(t, best, mi=mi):
                k = ks[mi][pl.ds(pl.multiple_of(t * T, T), T), :].astype(F32)
                return jnp.maximum(best, jnp.max(jnp.sum(k * k, axis=1, keepdims=True), axis=0, keepdims=True))
            knorm_ref[mi] = lax.fori_loop(0, n_tiles, knorm, jnp.zeros((1, 1), F32))

        x = lax.broadcasted_iota(jnp.int32, (8, 2 * T), 1)
        x = jnp.where(x < T, x, x - 2 * T)
        for di, d in enumerate(range(-n_near, n_near + 1)):
            g = _bias_from_rel(d * T - x, tab_ref, h)
            base = jnp.broadcast_to(g[0:1, :], (LANES, 2 * T))
            for rb in range(T // LANES):
                blk = pltpu.roll(base, rb * LANES, 1, stride=1, stride_axis=0)
                bias_ref[di, rb * LANES:(rb + 1) * LANES, :] = blk[:, :T]

    c_left = tab_ref[FIRST_BUCKET, h]
    c_right = tab_ref[LAST_BUCKET, h]

    bias_max = tab_ref[0, h]
    for b in range(1, N_REL_BUCKETS):
        bias_max = jnp.maximum(bias_max, tab_ref[b, h])

    for mi in range(2):
        q = qts[mi][...].astype(F32)
        qnorm2 = jnp.sum(q * q, axis=0, keepdims=True)
        shift_ref[mi] = jnp.sqrt(qnorm2 * knorm_ref[mi]) * SHIFT_MARGIN + bias_max
    l_ref[...] = jnp.zeros(l_ref.shape, F32)
    acc_ref[...] = jnp.zeros(acc_ref.shape, F32)
    p_ref[n_chains - 1] = jnp.zeros((T, QP), BF16)
    prev_ref[0] = 0

    def score_stage(i, k0, bias_di, bias_const):
        mi, qs = chain_of(i)
        s = jnp.dot(ks[mi][pl.ds(k0, T), :], qts[mi][:, qs], preferred_element_type=F32)
        if bias_di is not None:
            p = jnp.exp2(s + bias_ref[bias_di, :, qs] - shift_ref[mi, :, qs])
        else:
            p = jnp.exp2(s - (shift_ref[mi, :, qs] - bias_const))
        l_ref[mi, :, qs] += jnp.sum(p, axis=0, keepdims=True)
        p_ref[i] = p.astype(BF16)

    def value_stage(i, k0):
        mi, qs = chain_of(i)
        acc_ref[mi, :, qs] += jnp.dot(vt_ref[:, pl.ds(k0, T)], p_ref[i], preferred_element_type=F32)

    def tile(kt, bias_di, bias_const):
        k0 = pl.multiple_of(kt * T, T)
        pk0 = pl.multiple_of(prev_ref[0] * T, T)
        for i in range(n_chains):
            value_stage((i - 1) % n_chains, pk0 if i < 1 else k0)
            score_stage(i, k0, bias_di, bias_const)
        prev_ref[0] = kt

    lo = jnp.maximum(qi - n_near, 0)
    hi = jnp.minimum(qi + n_near + 1, n_tiles)

    @pl.loop(0, lo)
    def _(kt):
        tile(kt, None, c_left)

    for di, d in enumerate(range(-n_near, n_near + 1)):
        kt = qi + d

        @pl.when(jnp.logical_and(kt >= 0, kt < n_tiles))
        def _(di=di, kt=kt):
            tile(kt, di, None)

    @pl.loop(hi, n_tiles)
    def _(kt):
        tile(kt, None, c_right)

    value_stage(n_chains - 1, pl.multiple_of(prev_ref[0] * T, T))

    @pl.when(jnp.min(l_ref[...]) < MIN_DENOMINATOR)
    def _():
        m_ref[...] = jnp.full(m_ref.shape, -jnp.inf, F32)
        l_ref[...] = jnp.zeros(l_ref.shape, F32)
        acc_ref[...] = jnp.zeros(acc_ref.shape, F32)

        @pl.loop(0, n_tiles)
        def _(kt):
            k0 = pl.multiple_of(kt * T, T)
            d = kt - qi
            near = jnp.abs(d) <= n_near
            di = jnp.clip(d + n_near, 0, 2 * n_near)
            c_far = jnp.where(d < 0, c_left, c_right)
            for i in range(n_chains):
                mi, qs = chain_of(i)
                s = jnp.dot(ks[mi][pl.ds(k0, T), :], qts[mi][:, qs], preferred_element_type=F32)
                s = s + jnp.where(near, bias_ref[di, :, qs], c_far)
                m = m_ref[mi, :, qs]
                m_new = jnp.maximum(m, jnp.max(s, axis=0, keepdims=True))
                alpha = jnp.exp2(m - m_new)
                p = jnp.exp2(s - m_new)
                m_ref[mi, :, qs] = m_new
                l_ref[mi, :, qs] = alpha * l_ref[mi, :, qs] + jnp.sum(p, axis=0, keepdims=True)
                pv = jnp.dot(vt_ref[:, pl.ds(k0, T)], p.astype(BF16), preferred_element_type=F32)
                acc_ref[mi, :, qs] = alpha * acc_ref[mi, :, qs] + pv

    lam = lam_ref[0, 0]
    o = acc_ref[0] / l_ref[0] - lam * (acc_ref[1] / l_ref[1])
    ms = jnp.mean(o * o, axis=0, keepdims=True)
    o = o * lax.rsqrt(ms + SUBLN_EPS) * (gain_ref[...] * (1.0 - LAM_INIT))
    o_ref[...] = o.T.astype(o_ref.dtype)


def _lambda_kernel(q1_ref, k1_ref, q2_ref, k2_ref, o_ref):
    a = jnp.sum(q1_ref[...] * k1_ref[...], axis=-1, keepdims=True)
    b = jnp.sum(q2_ref[...] * k2_ref[...], axis=-1, keepdims=True)
    o_ref[...] = jnp.exp(a) - jnp.exp(b) + LAM_INIT


def _lambda(lq1, lk1, lq2, lk2):
    return pl.pallas_call(
        _lambda_kernel, out_shape=jax.ShapeDtypeStruct((1, 1), F32), name="diff_lambda",
    )(lq1, lk1, lq2, lk2)


def _diff_attention(tab_diff, proj, qt_all, vt_all, lam, gain_col):
    S = proj.shape[0]
    T = DIFF_TILE
    n_tiles = S // T
    n_near = -(-(FAR_DIST - 1) // T)
    kern = functools.partial(_diff_attn_kernel, T=T, QP=DIFF_QUERY_PANEL, n_near=n_near, n_tiles=n_tiles)
    k_block0 = 0
    smem = pl.BlockSpec(memory_space=pltpu.SMEM)
    return pl.pallas_call(
        kern,
        grid=(N_DIFF_HEADS, n_tiles),
        in_specs=[
            smem,
            pl.BlockSpec((HEAD_DIM, T), lambda h, i: (2 * h, i)),
            pl.BlockSpec((HEAD_DIM, T), lambda h, i: (2 * h + 1, i)),
            pl.BlockSpec((S, HEAD_DIM), lambda h, i: (0, k_block0 + 2 * h), pipeline_mode=pl.Buffered(1)),
            pl.BlockSpec((S, HEAD_DIM), lambda h, i: (0, k_block0 + 2 * h + 1), pipeline_mode=pl.Buffered(1)),
            pl.BlockSpec((DIFF_V_DIM, S), lambda h, i: (h, 0), pipeline_mode=pl.Buffered(1)),
            smem,
            pl.BlockSpec((DIFF_V_DIM, 1), lambda h, i: (0, 0)),
        ],
        out_specs=pl.BlockSpec((T, DIFF_V_DIM), lambda h, i: (i, h)),
        out_shape=jax.ShapeDtypeStruct((S, DIFF_WIDTH), BF16),
        scratch_shapes=[pltpu.VMEM((2 * n_near + 1, T, T), F32),
                        pltpu.VMEM((2, 1, T), F32), pltpu.VMEM((2, 1, T), F32),
                        pltpu.VMEM((2, DIFF_V_DIM, T), F32),
                        pltpu.VMEM((2 * (T // DIFF_QUERY_PANEL), T, DIFF_QUERY_PANEL), BF16),
                        pltpu.VMEM((2, 1, T), F32), pltpu.VMEM((2, 1, 1), F32),
                        pltpu.SMEM((1,), jnp.int32)],
        compiler_params=pltpu.CompilerParams(
            dimension_semantics=("arbitrary", "arbitrary"), vmem_limit_bytes=VMEM_LIMIT),
        name="diff_attn",
    )(tab_diff, qt_all, qt_all, proj, proj, vt_all, lam, gain_col)


def _dilated_kernel(tab_ref, q_ref, kp_ref, km_ref, kn_ref, vp_ref, vm_ref, vn_ref, o_ref, lse_ref,
                    bias_ref, kx_ref, vx_ref, *, R, B, half, dilation, n_chunks):
    c = pl.program_id(0)
    n = pl.program_id(1)
    W = B + 2 * half
    nblk = R // B

    @pl.when(jnp.logical_and(c == 0, n == 0))
    def _():
        rows = 8
        col = lax.broadcasted_iota(jnp.int32, (rows, W), 1)
        row = lax.broadcasted_iota(jnp.int32, (rows, W), 0)
        for hh in range(N_DIL_HEADS):
            def fill(r, carry, hh=hh):
                r0 = pl.multiple_of(r * rows, rows)
                off = col - half - (row + r0)
                bias = _bias_from_rel(off * dilation, tab_ref, N_DIFF_HEADS + hh)
                base = jnp.where(jnp.abs(off) <= half, bias, NEG_INF * LOG2E)
                bias_ref[hh, 1, pl.ds(r0, rows), :] = base
                bias_ref[hh, 0, pl.ds(r0, rows), :] = jnp.where(col >= half, base, NEG_INF * LOG2E)
                bias_ref[hh, 2, pl.ds(r0, rows), :] = jnp.where(col < B + half, base, NEG_INF * LOG2E)
                return carry
            lax.fori_loop(0, B // rows, fill, 0)

    kx_ref[0:half, :] = kp_ref[...]
    kx_ref[half:half + R, :] = km_ref[...]
    kx_ref[half + R:, :] = kn_ref[...]
    vx_ref[0:half, :] = vp_ref[...]
    vx_ref[half:half + R, :] = vm_ref[...]
    vx_ref[half + R:, :] = vn_ref[...]

    def chain(hh, b):
        c0 = pl.multiple_of(hh * HEAD_DIM, HEAD_DIM)
        r0 = b * B
        var = 1
        if b == 0:
            var = jnp.where(n == 0, 0, var)
        if b == nblk - 1:
            var = jnp.where(n == n_chunks - 1, 2, var)
        q = q_ref[pl.ds(r0, B), pl.ds(c0, HEAD_DIM)]
        k = kx_ref[pl.ds(r0, W), pl.ds(c0, HEAD_DIM)]
        v = vx_ref[pl.ds(r0, W), pl.ds(c0, HEAD_DIM)]
        s = lax.dot_general(q, k, (((1,), (1,)), ((), ())), preferred_element_type=F32)
        s = s + bias_ref[hh, var]
        m = jnp.max(s, axis=-1, keepdims=True)
        e = jnp.exp2(s - m)
        den = jnp.sum(e, axis=-1, keepdims=True)
        o = jnp.dot(e.astype(BF16), v, preferred_element_type=F32) / den
        o_ref[pl.ds(r0, B), pl.ds(c0, HEAD_DIM)] = o.astype(o_ref.dtype)
        lse = m + jnp.log2(den)
        lse_ref[pl.ds(r0, B), pl.ds(c0, HEAD_DIM)] = jnp.broadcast_to(lse, (B, HEAD_DIM))

    @pl.loop(0, N_DIL_HEADS // DIL_HEAD_UNROLL)
    def _(hg):
        for u in range(DIL_HEAD_UNROLL):
            for b in range(nblk):
                chain(hg * DIL_HEAD_UNROLL + u, b)


def _dilated_pattern(tab, qkv, col_blk0, window, dilation, *, B=256):
    _, L, _ = qkv.shape
    R = min(DIL_CHUNK, L)
    half = window // (2 * dilation)
    assert L % R == 0 and R % B == 0 and half % BF16_SUBLANES == 0 and R % half == 0
    n_chunks = L // R
    q_blk, k_blk, v_blk = col_blk0, col_blk0 + 1, col_blk0 + 2
    hb = R // half
    n_hblk = L // half

    def main(blk):
        return pl.BlockSpec((None, R, DIL_WIDTH), lambda c, n: (c, n, blk))

    def prev(blk):
        return pl.BlockSpec((None, half, DIL_WIDTH), lambda c, n: (c, jnp.maximum(n * hb - 1, 0), blk))

    def nxt(blk):
        return pl.BlockSpec((None, half, DIL_WIDTH), lambda c, n: (c, jnp.minimum((n + 1) * hb, n_hblk - 1), blk))

    kern = functools.partial(_dilated_kernel, R=R, B=B, half=half, dilation=dilation, n_chunks=n_chunks)
    out_spec = pl.BlockSpec((None, R, DIL_WIDTH), lambda c, n: (c, n, 0))
    return pl.pallas_call(
        kern,
        grid=(dilation, n_chunks),
        in_specs=[pl.BlockSpec(memory_space=pltpu.SMEM),
                  main(q_blk), prev(k_blk), main(k_blk), nxt(k_blk), prev(v_blk), main(v_blk), nxt(v_blk)],
        out_specs=[out_spec, out_spec],
        out_shape=[jax.ShapeDtypeStruct((dilation, L, DIL_WIDTH), BF16),
                   jax.ShapeDtypeStruct((dilation, L, DIL_WIDTH), F32)],
        scratch_shapes=[pltpu.VMEM((N_DIL_HEADS, 3, B, B + 2 * half), F32),
                        pltpu.VMEM((R + 2 * half, DIL_WIDTH), BF16),
                        pltpu.VMEM((R + 2 * half, DIL_WIDTH), BF16)],
        compiler_params=pltpu.CompilerParams(
            dimension_semantics=("arbitrary", "arbitrary"), vmem_limit_bytes=VMEM_LIMIT),
        name=f"dilated_d{dilation}",
    )(tab, qkv, qkv, qkv, qkv, qkv, qkv, qkv)


def _combine_kernel(*refs, dilations):
    n = len(dilations)
    o_refs, l_refs = refs[:n], refs[n:2 * n]
    g_ref, out_ref = refs[2 * n], refs[2 * n + 1]
    scratch = refs[2 * n + 2:]
    tm = out_ref.shape[0]

    for hh in range(N_DIL_HEADS):
        sl = slice(hh * HEAD_DIM, (hh + 1) * HEAD_DIM)
        outs, lses = [], []
        si = 0
        for o_ref, l_ref, dil in zip(o_refs, l_refs, dilations):
            if dil == 1:
                outs.append(o_ref[0, :, sl].astype(F32))
                lses.append(l_ref[0, :, sl])
                continue
            os_ref, ls_ref = scratch[si], scratch[si + 1]
            si += 2
            for c in range(dil):
                os_ref[hh, pl.ds(c, tm // dil, stride=dil), :] = o_ref[c, :, sl].astype(F32)
                ls_ref[hh, pl.ds(c, tm // dil, stride=dil), :] = l_ref[c, :, sl]
            outs.append(os_ref[hh])
            lses.append(ls_ref[hh])

        m = functools.reduce(jnp.maximum, lses)
        ws = [jnp.exp2(l - m) for l in lses]
        tot = functools.reduce(lambda a, b: a + b, ws)
        oh = functools.reduce(lambda a, b: a + b, [(w / tot) * op for w, op in zip(ws, outs)])
        ms = jnp.mean(oh * oh, axis=-1, keepdims=True)
        out_ref[:, sl] = (oh * lax.rsqrt(ms + NORM_EPS) * g_ref[:, sl]).astype(out_ref.dtype)


def _combine(outs, lses, gain, dilations, *, tm=512):
    S = outs[0].shape[0] * outs[0].shape[1]
    specs = [pl.BlockSpec((d, tm // d, DIL_WIDTH), lambda i: (0, i, 0)) for d in dilations]
    n_scr = sum(1 for d in dilations if d != 1)
    return pl.pallas_call(
        functools.partial(_combine_kernel, dilations=dilations),
        grid=(S // tm,),
        in_specs=specs + specs + [pl.BlockSpec((1, DIL_WIDTH), lambda i: (0, 0))],
        scratch_shapes=[pltpu.VMEM((N_DIL_HEADS, tm, HEAD_DIM), F32)] * (2 * n_scr),
        out_specs=pl.BlockSpec((tm, DIL_WIDTH), lambda i: (i, 0)),
        out_shape=jax.ShapeDtypeStruct((S, DIL_WIDTH), BF16),
        compiler_params=pltpu.CompilerParams(
            dimension_semantics=("arbitrary",), vmem_limit_bytes=VMEM_LIMIT),
        name="dilated_combine",
    )(*outs, *lses, gain)


def _key_norm2_kernel(k_ref, o_ref):
    @pl.when(pl.program_id(0) == 0)
    def _():
        o_ref[...] = jnp.zeros(o_ref.shape, F32)

    k = k_ref[...].astype(F32)
    for hh in range(N_DIL_HEADS):
        kh = k[:, hh * HEAD_DIM:(hh + 1) * HEAD_DIM]
        n2 = jnp.max(jnp.sum(kh * kh, axis=1, keepdims=True), axis=0, keepdims=True)
        o_ref[hh:hh + 1, :] = jnp.maximum(o_ref[hh:hh + 1, :], jnp.broadcast_to(n2, (1, LANES)))


def _key_norm2(proj, k_blk, *, tm=1024):
    S = proj.shape[0]
    return pl.pallas_call(
        _key_norm2_kernel,
        grid=(S // tm,),
        in_specs=[pl.BlockSpec((tm, DIL_WIDTH), lambda i: (i, k_blk))],
        out_specs=pl.BlockSpec((N_DIL_HEADS, LANES), lambda i: (0, 0)),
        out_shape=jax.ShapeDtypeStruct((N_DIL_HEADS, LANES), F32),
        compiler_params=pltpu.CompilerParams(dimension_semantics=("arbitrary",), vmem_limit_bytes=VMEM_LIMIT),
        name="dilated_key_norm",
    )(proj)


def _dilated_fast_kernel(tab_ref, kn2_ref, q_ref, kp_ref, km_ref, kn_ref, vp_ref, vm_ref, vn_ref, num_ref, den_ref,
                         bias_ref, kx_ref, vx_ref, p_ref, *, R, B, half, dilation, n_chunks):
    c = pl.program_id(0)
    n = pl.program_id(1)
    W = B + 2 * half
    nblk = R // B
    VW = 2 * HEAD_DIM

    @pl.when(jnp.logical_and(c == 0, n == 0))
    def _():
        rows = 8
        col = lax.broadcasted_iota(jnp.int32, (rows, W), 1)
        row = lax.broadcasted_iota(jnp.int32, (rows, W), 0)
        for hh in range(N_DIL_HEADS):
            def fill(r, carry, hh=hh):
                r0 = pl.multiple_of(r * rows, rows)
                off = col - half - (row + r0)
                bias = _bias_from_rel(off * dilation, tab_ref, N_DIFF_HEADS + hh)
                base = jnp.where(jnp.abs(off) <= half, bias, NEG_INF * LOG2E)
                bias_ref[hh, 1, pl.ds(r0, rows), :] = base
                bias_ref[hh, 0, pl.ds(r0, rows), :] = jnp.where(col >= half, base, NEG_INF * LOG2E)
                bias_ref[hh, 2, pl.ds(r0, rows), :] = jnp.where(col < B + half, base, NEG_INF * LOG2E)
                return carry
            lax.fori_loop(0, B // rows, fill, 0)
        vx_ref[...] = jnp.ones(vx_ref.shape, BF16)

    kx_ref[0:half, :] = kp_ref[...]
    kx_ref[half:half + R, :] = km_ref[...]
    kx_ref[half + R:, :] = kn_ref[...]
    for hh in range(N_DIL_HEADS):
        src = slice(hh * HEAD_DIM, (hh + 1) * HEAD_DIM)
        dst = slice(hh * VW, hh * VW + HEAD_DIM)
        vx_ref[0:half, dst] = vp_ref[:, src]
        vx_ref[half:half + R, dst] = vm_ref[:, src]
        vx_ref[half + R:, dst] = vn_ref[:, src]

    n_slots = p_ref.shape[0]
    bias_max = []
    for hh in range(N_DIL_HEADS):
        bm = tab_ref[0, N_DIFF_HEADS + hh]
        for bk in range(1, N_REL_BUCKETS):
            bm = jnp.maximum(bm, tab_ref[bk, N_DIFF_HEADS + hh])
        bias_max.append(bm)

    def score_stage(hh, b, slot):
        cs = slice(hh * HEAD_DIM, (hh + 1) * HEAD_DIM)
        r0 = b * B
        var = 1
        if b == 0:
            var = jnp.where(n == 0, 0, var)
        if b == nblk - 1:
            var = jnp.where(n == n_chunks - 1, 2, var)
        q = q_ref[r0:r0 + B, cs]
        qf = q.astype(F32)
        qn2 = jnp.sum(qf * qf, axis=1, keepdims=True)
        shift = jnp.sqrt(qn2 * kn2_ref[hh:hh + 1, 0:1]) * SHIFT_MARGIN + bias_max[hh]
        s = lax.dot_general(q, kx_ref[r0:r0 + W, cs], (((1,), (1,)), ((), ())), preferred_element_type=F32)
        p_ref[slot] = jnp.exp2(s + bias_ref[hh, var] - shift).astype(BF16)

    def value_stage(hh, b, slot):
        cs = slice(hh * HEAD_DIM, (hh + 1) * HEAD_DIM)
        r0 = b * B
        nd = jnp.dot(p_ref[slot], vx_ref[r0:r0 + W, hh * VW:(hh + 1) * VW], preferred_element_type=F32)
        num_ref[r0:r0 + B, cs] = nd[:, :HEAD_DIM].astype(num_ref.dtype)
        den_ref[r0:r0 + B, cs] = nd[:, HEAD_DIM:].astype(den_ref.dtype)

    chains = [(hh, b) for hh in range(N_DIL_HEADS) for b in range(nblk)]
    for i, (hh, b) in enumerate(chains):
        if i > 0:
            value_stage(*chains[i - 1], (i - 1) % n_slots)
        score_stage(hh, b, i % n_slots)
    value_stage(*chains[-1], (len(chains) - 1) % n_slots)


def _dilated_fast(tab, kn2, qkv, col_blk0, window, dilation, *, B=128):
    _, L, _ = qkv.shape
    R = min(DIL_CHUNK, L)
    half = window // (2 * dilation)
    assert L % R == 0 and R % B == 0 and half % BF16_SUBLANES == 0 and R % half == 0
    n_chunks = L // R
    q_blk, k_blk, v_blk = col_blk0, col_blk0 + 1, col_blk0 + 2
    hb = R // half
    n_hblk = L // half

    def main(blk):
        return pl.BlockSpec((None, R, DIL_WIDTH), lambda c, n: (c, n, blk))

    def prev(blk):
        return pl.BlockSpec((None, half, DIL_WIDTH), lambda c, n: (c, jnp.maximum(n * hb - 1, 0), blk))

    def nxt(blk):
        return pl.BlockSpec((None, half, DIL_WIDTH), lambda c, n: (c, jnp.minimum((n + 1) * hb, n_hblk - 1), blk))

    kern = functools.partial(_dilated_fast_kernel, R=R, B=B, half=half, dilation=dilation, n_chunks=n_chunks)
    out_spec = pl.BlockSpec((None, R, DIL_WIDTH), lambda c, n: (c, n, 0))
    return pl.pallas_call(
        kern,
        grid=(dilation, n_chunks),
        in_specs=[pl.BlockSpec(memory_space=pltpu.SMEM),
                  pl.BlockSpec((N_DIL_HEADS, LANES), lambda c, n: (0, 0)),
                  main(q_blk), prev(k_blk), main(k_blk), nxt(k_blk), prev(v_blk), main(v_blk), nxt(v_blk)],
        out_specs=[out_spec, out_spec],
        out_shape=[jax.ShapeDtypeStruct((dilation, L, DIL_WIDTH), BF16),
                   jax.ShapeDtypeStruct((dilation, L, DIL_WIDTH), BF16)],
        scratch_shapes=[pltpu.VMEM((N_DIL_HEADS, 3, B, B + 2 * half), F32),
                        pltpu.VMEM((R + 2 * half, DIL_WIDTH), BF16),
                        pltpu.VMEM((R + 2 * half, 2 * DIL_WIDTH), BF16),
                        pltpu.VMEM((4, B, B + 2 * half), BF16)],
        compiler_params=pltpu.CompilerParams(
            dimension_semantics=("arbitrary", "arbitrary"), vmem_limit_bytes=VMEM_LIMIT),
        name=f"dilated_fast_d{dilation}",
    )(tab, kn2, qkv, qkv, qkv, qkv, qkv, qkv, qkv)


def _combine_fast_kernel(*refs, dilations):
    n = len(dilations)
    n_refs, d_refs = refs[:n], refs[n:2 * n]
    g_ref, out_ref, dmin_ref = refs[2 * n], refs[2 * n + 1], refs[2 * n + 2]
    scratch = refs[2 * n + 3:]
    tm = out_ref.shape[0]

    for hh in range(N_DIL_HEADS):
        sl = slice(hh * HEAD_DIM, (hh + 1) * HEAD_DIM)
        nums, dens = [], []
        si = 0
        for n_ref, d_ref, dil in zip(n_refs, d_refs, dilations):
            if dil == 1:
                nums.append(n_ref[0, :, sl].astype(F32))
                dens.append(d_ref[0, :, sl].astype(F32))
                continue
            ns_ref, ds_ref = scratch[si], scratch[si + 1]
            si += 2
            for c in range(dil):
                ns_ref[hh, pl.ds(c, tm // dil, stride=dil), :] = n_ref[c, :, sl].astype(F32)
                ds_ref[hh, pl.ds(c, tm // dil, stride=dil), :] = d_ref[c, :, sl].astype(F32)
            nums.append(ns_ref[hh])
            dens.append(ds_ref[hh])

        den = functools.reduce(lambda a, b: a + b, dens)
        oh = functools.reduce(lambda a, b: a + b, nums) / den
        ms = jnp.mean(oh * oh, axis=-1, keepdims=True)
        out_ref[:, sl] = (oh * lax.rsqrt(ms + NORM_EPS) * g_ref[:, sl]).astype(out_ref.dtype)
        dmin_ref[0, hh:hh + 1, :] = jnp.min(den, axis=0, keepdims=True)


def _combine_fast(nums, dens, gain, dilations, *, tm=512):
    S = nums[0].shape[0] * nums[0].shape[1]
    specs = [pl.BlockSpec((d, tm // d, DIL_WIDTH), lambda i: (0, i, 0)) for d in dilations]
    n_scr = sum(1 for d in dilations if d != 1)
    return pl.pallas_call(
        functools.partial(_combine_fast_kernel, dilations=dilations),
        grid=(S // tm,),
        in_specs=specs + specs + [pl.BlockSpec((1, DIL_WIDTH), lambda i: (0, 0))],
        scratch_shapes=[pltpu.VMEM((N_DIL_HEADS, tm, HEAD_DIM), F32)] * (2 * n_scr),
        out_specs=[pl.BlockSpec((tm, DIL_WIDTH), lambda i: (i, 0)),
                   pl.BlockSpec((1, N_DIL_HEADS, LANES), lambda i: (i, 0, 0))],
        out_shape=[jax.ShapeDtypeStruct((S, DIL_WIDTH), BF16),
                   jax.ShapeDtypeStruct((S // tm, N_DIL_HEADS, LANES), F32)],
        compiler_params=pltpu.CompilerParams(
            dimension_semantics=("arbitrary",), vmem_limit_bytes=VMEM_LIMIT),
        name="dilated_combine_fast",
    )(*nums, *dens, gain)


def _outproj_kernel(od_ref, ol_ref, wd_ref, wl_ref, x_ref, g_ref, x1_ref, h2_ref):
    acc = jnp.dot(od_ref[...], wd_ref[...], preferred_element_type=F32)
    acc = acc + jnp.dot(ol_ref[...], wl_ref[...], preferred_element_type=F32)
    x1 = x_ref[...] + acc
    x1_ref[...] = x1
    ms = jnp.mean(x1 * x1, axis=-1, keepdims=True)
    h2_ref[...] = (x1 * lax.rsqrt(ms + NORM_EPS) * g_ref[...]).astype(h2_ref.dtype)


def _out_proj(o_d, o_l, w_bf, x2, gain, *, tm=512):
    S, D = x2.shape
    return pl.pallas_call(
        _outproj_kernel,
        grid=(S // tm,),
        in_specs=[
            pl.BlockSpec((tm, DIFF_WIDTH), lambda i: (i, 0)),
            pl.BlockSpec((tm, DIL_WIDTH), lambda i: (i, 0)),
            pl.BlockSpec((DIFF_WIDTH, D), lambda i: (0, 0)),
            pl.BlockSpec((DIL_WIDTH, D), lambda i: (1, 0)),
            pl.BlockSpec((tm, D), lambda i: (i, 0)),
            pl.BlockSpec((1, D), lambda i: (0, 0)),
        ],
        out_specs=[pl.BlockSpec((tm, D), lambda i: (i, 0)), pl.BlockSpec((tm, D), lambda i: (i, 0))],
        out_shape=[jax.ShapeDtypeStruct((S, D), F32), jax.ShapeDtypeStruct((S, D), BF16)],
        compiler_params=pltpu.CompilerParams(
            dimension_semantics=("arbitrary",), vmem_limit_bytes=VMEM_LIMIT),
        name="out_proj",
    )(o_d, o_l, w_bf, w_bf, x2, gain)


def _ffn_up_kernel(hm_ref, hp_ref, hn_ref, wg_ref, wu_ref, cw_ref, cb_ref, o_ref, lhs_ref, *, tm, n_row_tiles):
    i = pl.program_id(0)
    j = pl.program_id(1)
    halo = BF16_SUBLANES

    @pl.when(j == 0)
    def _():
        lhs_ref[0:halo, :] = jnp.where(i == 0, jnp.zeros_like(hp_ref[...]), hp_ref[...])
        lhs_ref[halo:halo + tm, :] = hm_ref[...]
        lhs_ref[halo + tm:, :] = jnp.where(i == n_row_tiles - 1, jnp.zeros_like(hn_ref[...]), hn_ref[...])

    g = jnp.dot(lhs_ref[...], wg_ref[...], preferred_element_type=F32)
    u = jnp.dot(lhs_ref[halo:halo + tm, :], wu_ref[...], preferred_element_type=F32)
    rows = tm + 2 * halo
    g_prev = pltpu.roll(g, 1, axis=0)
    g_next = pltpu.roll(g, rows - 1, axis=0)
    y = cw_ref[0:1, :] * g_prev + cw_ref[1:2, :] * g + cw_ref[2:3, :] * g_next + cb_ref[...]
    y = y[halo:halo + tm, :]
    act = y * (1.0 / (1.0 + jnp.exp(-y))) * u
    o_ref[...] = act.astype(o_ref.dtype)


def _ffn_up(h2, w_bf, conv_w, conv_b, *, tm=1024, tn=512):
    S, D = h2.shape
    d_ff = conv_w.shape[1]
    assert d_ff % tn == 0
    nj = d_ff // tn
    ni = S // tm
    hb = tm // BF16_SUBLANES
    n_hblk = S // BF16_SUBLANES
    kern = functools.partial(_ffn_up_kernel, tm=tm, n_row_tiles=ni)
    return pl.pallas_call(
        kern,
        grid=(ni, nj),
        in_specs=[
            pl.BlockSpec((tm, D), lambda i, j: (i, 0)),
            pl.BlockSpec((BF16_SUBLANES, D), lambda i, j: (jnp.maximum(i * hb - 1, 0), 0)),
            pl.BlockSpec((BF16_SUBLANES, D), lambda i, j: (jnp.minimum((i + 1) * hb, n_hblk - 1), 0)),
            pl.BlockSpec((D, tn), lambda i, j: (0, j)),
            pl.BlockSpec((D, tn), lambda i, j: (0, nj + j)),
            pl.BlockSpec((3, tn), lambda i, j: (0, j)),
            pl.BlockSpec((1, tn), lambda i, j: (0, j)),
        ],
        out_specs=pl.BlockSpec((tm, tn), lambda i, j: (i, j)),
        out_shape=jax.ShapeDtypeStruct((S, d_ff), BF16),
        scratch_shapes=[pltpu.VMEM((tm + 2 * BF16_SUBLANES, D), BF16)],
        compiler_params=pltpu.CompilerParams(
            dimension_semantics=("arbitrary", "arbitrary"), vmem_limit_bytes=VMEM_LIMIT),
        name="ffn_up",
    )(h2, h2, h2, w_bf, w_bf, conv_w, conv_b)


def _ffn_down_kernel(a_ref, w_ref, x1_ref, g_ref, o_ref, *, n_k):
    k = pl.program_id(1)

    @pl.when(k == 0)
    def _():
        o_ref[...] = x1_ref[...]

    o_ref[...] += jnp.dot(a_ref[...], w_ref[...], preferred_element_type=F32)

    @pl.when(k == n_k - 1)
    def _():
        y = o_ref[...]
        ms = jnp.mean(y * y, axis=-1, keepdims=True)
        o_ref[...] = y * lax.rsqrt(ms + NORM_EPS) * g_ref[...]


def _ffn_down(act, w_bf, x1, gain, *, tm=1024, tk=1408):
    S, d_ff = act.shape
    D = x1.shape[1]
    n_k = d_ff // tk
    kern = functools.partial(_ffn_down_kernel, n_k=n_k)
    return pl.pallas_call(
        kern,
        grid=(S // tm, n_k),
        in_specs=[
            pl.BlockSpec((tm, tk), lambda i, k: (i, k)),
            pl.BlockSpec((tk, D), lambda i, k: (k, 0)),
            pl.BlockSpec((tm, D), lambda i, k: (i, 0)),
            pl.BlockSpec((1, D), lambda i, k: (0, 0)),
        ],
        out_specs=pl.BlockSpec((tm, D), lambda i, k: (i, 0)),
        out_shape=jax.ShapeDtypeStruct((S, D), F32),
        compiler_params=pltpu.CompilerParams(
            dimension_semantics=("arbitrary", "arbitrary"), vmem_limit_bytes=VMEM_LIMIT),
        name="ffn_down",
    )(act, w_bf, x1, gain)


def kernel(x, norm1_gain, w_in, rel_bias_table, lambda_q1, lambda_k1, lambda_q2, lambda_k2,
           diff_subln_gain, dil_out_gain, w_out, norm2_gain, w_gate_up, conv_w, conv_b, w_down, final_gain):
    B, S, D = x.shape
    assert B == 1 and w_in.shape[0] == 1
    x2 = x.reshape(S, D)
    n_cols = w_in.shape[2]

    qscale = LOG2E / math.sqrt(HEAD_DIM)
    col = np.arange(n_cols)
    dil_q0 = 2 * DIFF_QK_COLS + DIFF_WIDTH
    is_q = (col < DIFF_QK_COLS) | ((col >= dil_q0) & (col < dil_q0 + DIL_WIDTH))
    colscale = jnp.asarray(np.where(is_q, qscale, 1.0).astype(np.float32)).reshape(1, n_cols)
    tab = rel_bias_table.astype(F32) * LOG2E

    regroup = tuple(d for _, d in DILATED_PATTERNS if d != 1)
    proj, qt_all, vt_all, *cls = _in_proj(x2, norm1_gain.reshape(1, D), w_in[0].astype(BF16), colscale, regroup)
    cls_by_dil = dict(zip(regroup, cls))

    lam = _lambda(lambda_q1.reshape(1, -1), lambda_k1.reshape(1, -1),
                  lambda_q2.reshape(1, -1), lambda_k2.reshape(1, -1))
    o_d = _diff_attention(tab, proj, qt_all, vt_all, lam, diff_subln_gain.reshape(-1, 1))

    dil_blk0 = 1
    dilations = tuple(d for _, d in DILATED_PATTERNS)
    dil_gain = dil_out_gain.reshape(1, -1)

    def pattern_inputs(dilation):
        if dilation == 1:
            return proj.reshape(1, S, proj.shape[1]), dil_blk0
        return cls_by_dil[dilation], 0

    kn2 = _key_norm2(proj, dil_blk0 + 1)
    nums, dens = [], []
    for window, dilation in DILATED_PATTERNS:
        n_p, d_p = _dilated_fast(tab, kn2, *pattern_inputs(dilation), window, dilation)
        nums.append(n_p)
        dens.append(d_p)
    o_l_fast, den_min = _combine_fast(nums, dens, dil_gain, dilations)

    def exact_dilated():
        outs, lses = [], []
        for window, dilation in DILATED_PATTERNS:
            o_p, lse_p = _dilated_pattern(tab, *pattern_inputs(dilation), window, dilation)
            outs.append(o_p)
            lses.append(lse_p)
        return _combine(outs, lses, dil_gain, dilations)

    o_l = lax.cond(jnp.min(den_min) < MIN_DENOMINATOR, exact_dilated, lambda: o_l_fast)

    x1, h2 = _out_proj(o_d, o_l, w_out[0].astype(BF16), x2, norm2_gain.reshape(1, D))
    act = _ffn_up(h2, w_gate_up[0].astype(BF16), conv_w[0], conv_b.reshape(1, -1))
    out = _ffn_down(act, w_down[0].astype(BF16), x1, final_gain.reshape(1, D))
    return out.reshape(B, S, D)
```

```python
import functools
import math

import numpy as np
import jax
import jax.numpy as jnp
from jax import lax
from jax.experimental import pallas as pl
from jax.experimental.pallas import tpu as pltpu

F32 = jnp.float32
BF16 = jnp.bfloat16

HEAD_DIM = 128
N_DIFF_HEADS = 4
DIFF_V_DIM = 2 * HEAD_DIM
N_DIL_HEADS = 8
DIFF_QK_COLS = N_DIFF_HEADS * 2 * HEAD_DIM
DIFF_WIDTH = N_DIFF_HEADS * DIFF_V_DIM
DIL_WIDTH = N_DIL_HEADS * HEAD_DIM
DILATED_PATTERNS = ((128, 1), (512, 4), (2048, 16))
N_REL_BUCKETS = 32
REL_MAX_DISTANCE = 1024
NORM_EPS = 1e-6
SUBLN_EPS = 1e-5
NEG_INF = -1e30
LOG2E = math.log2(math.e)
LAM_INIT = 0.8 - 0.6 * math.exp(-0.3 * 0)

LANES = 128
BF16_SUBLANES = 16
DIFF_TILE = 1024
DIFF_QUERY_PANEL = 256
SHIFT_MARGIN = 1.0 + 2.0 ** -8
MIN_DENOMINATOR = 2.0 ** -60
DIL_CHUNK = 1024
DIL_HEAD_UNROLL = 4
VMEM_LIMIT = 56 * 1024 * 1024


def _bucket_breaks():
    nb = N_REL_BUCKETS // 2
    max_exact = nb // 2
    rel = np.arange(-2 * REL_MAX_DISTANCE, 2 * REL_MAX_DISTANCE + 1)
    n = np.abs(rel)
    pos = np.log(np.maximum(n, 1) / max_exact) / math.log(REL_MAX_DISTANCE / max_exact) * (nb - max_exact)
    large = np.minimum(max_exact + np.floor(pos).astype(np.int64), nb - 1)
    bucket = np.where(rel > 0, nb, 0) + np.where(n < max_exact, n, large)
    breaks = [(int(rel[i]), int(bucket[i])) for i in range(1, len(rel)) if bucket[i] != bucket[i - 1]]
    return int(bucket[0]), breaks


FIRST_BUCKET, BUCKET_BREAKS = _bucket_breaks()
LAST_BUCKET = BUCKET_BREAKS[-1][1]
FAR_DIST = max(-BUCKET_BREAKS[0][0] + 1, BUCKET_BREAKS[-1][0])


def _bias_from_rel(rel, tab_ref, col):
    val = jnp.full(rel.shape, tab_ref[FIRST_BUCKET, col], F32)
    for thr, b in BUCKET_BREAKS:
        val = jnp.where(rel >= thr, tab_ref[b, col], val)
    return val


N_COL_GROUPS = 6
FIRST_DIL_GROUP = 3


def _inproj_kernel(x_ref, g_ref, w_ref, cs_ref, o_ref, qt_ref, vt_ref, *rest, dilations):
    cls_refs = rest[:len(dilations)]
    h_ref, stage_ref = rest[len(dilations):]
    t = pl.program_id(1)
    tm = x_ref.shape[0]
    n_dil = N_COL_GROUPS - FIRST_DIL_GROUP

    @pl.when(t == 0)
    def _():
        x = x_ref[...]
        ms = jnp.mean(x * x, axis=-1, keepdims=True)
        h_ref[...] = (x * lax.rsqrt(ms + NORM_EPS) * g_ref[...]).astype(BF16)

    def matmul():
        acc = jnp.dot(h_ref[...], w_ref[...], preferred_element_type=F32) * cs_ref[...]
        o_ref[...] = acc.astype(o_ref.dtype)
        return acc

    def stage(acc, slot):
        for cb in range(stage_ref.shape[1]):
            stage_ref[slot, cb] = acc[:, cb * LANES:(cb + 1) * LANES]

    def regroup(slot):
        for cb in range(stage_ref.shape[1]):
            sl = slice(cb * LANES, (cb + 1) * LANES)
            for cls_ref, dil in zip(cls_refs, dilations):
                for c in range(dil):
                    cls_ref[c, :, sl] = stage_ref[slot, cb, pl.ds(c, tm // dil, stride=dil), :].astype(cls_ref.dtype)

    for step in range(N_COL_GROUPS):
        @pl.when(t == step)
        def _(step=step):
            acc = matmul()
            if 1 <= step <= n_dil:
                regroup((step - 1) % 2)
            if step < n_dil:
                stage(acc, step % 2)
            if step == n_dil:
                qt_ref[...] = acc.T.astype(qt_ref.dtype)
            if step == n_dil + 2:
                vt_ref[...] = acc.T.astype(vt_ref.dtype)


def _in_proj(x2, gain, w_bf, colscale, dilations, *, tm=512):
    S, D = x2.shape
    N = w_bf.shape[1]
    tn = DIFF_QK_COLS
    assert DIFF_WIDTH == tn and DIL_WIDTH == tn and N == N_COL_GROUPS * tn
    n_dil = N_COL_GROUPS - FIRST_DIL_GROUP
    kern = functools.partial(_inproj_kernel, dilations=dilations)

    def group(t):
        return (t + FIRST_DIL_GROUP) % N_COL_GROUPS

    cls_specs = [pl.BlockSpec((d, tm // d, tn), lambda i, t: (0, i, jnp.clip(t - 1, 0, n_dil - 1)))
                 for d in dilations]
    cls_shapes = [jax.ShapeDtypeStruct((d, S // d, n_dil * tn), BF16) for d in dilations]
    return pl.pallas_call(
        kern,
        grid=(S // tm, N_COL_GROUPS),
        in_specs=[
            pl.BlockSpec((tm, D), lambda i, t: (i, 0)),
            pl.BlockSpec((1, D), lambda i, t: (0, 0)),
            pl.BlockSpec((D, tn), lambda i, t: (0, group(t))),
            pl.BlockSpec((1, tn), lambda i, t: (0, group(t))),
        ],
        out_specs=[
            pl.BlockSpec((tm, tn), lambda i, t: (i, group(t))),
            pl.BlockSpec((tn, tm), lambda i, t: (0, i)),
            pl.BlockSpec((tn, tm), lambda i, t: (0, i)),
        ] + cls_specs,
        out_shape=[
            jax.ShapeDtypeStruct((S, N), BF16),
            jax.ShapeDtypeStruct((tn, S), BF16),
            jax.ShapeDtypeStruct((tn, S), BF16),
        ] + cls_shapes,
        scratch_shapes=[pltpu.VMEM((tm, D), BF16), pltpu.VMEM((2, tn // LANES, tm, LANES), F32)],
        compiler_params=pltpu.CompilerParams(
            dimension_semantics=("arbitrary", "arbitrary"), vmem_limit_bytes=VMEM_LIMIT),
        name="in_proj",
    )(x2, gain, w_bf, colscale)


def _diff_attn_kernel(tab_ref, q1t_ref, q2t_ref, k1_ref, k2_ref, vt_ref, lam_ref, gain_ref, o_ref,
                      bias_ref, m_ref, l_ref, acc_ref, p_ref, shift_ref, knorm_ref, prev_ref,
                      *, T, QP, n_near, n_tiles):
    h = pl.program_id(0)
    qi = pl.program_id(1)
    n_chains = 2 * (T // QP)
    qts = (q1t_ref, q2t_ref)
    ks = (k1_ref, k2_ref)

    def chain_of(i):
        qp = i // 2
        return i % 2, slice(qp * QP, (qp + 1) * QP)

    @pl.when(qi == 0)
    def _():
        for mi in range(2):
            def knorm(t, best, mi=mi):
                k = ks[mi][pl.ds(pl.multiple_of(t * T, T), T), :].astype(F32)
                return jnp.maximum(best, jnp.max(jnp.sum(k * k, axis=1, keepdims=True), axis=0, keepdims=True))
            knorm_ref[mi] = lax.fori_loop(0, n_tiles, knorm, jnp.zeros((1, 1), F32))

        x = lax.broadcasted_iota(jnp.int32, (8, 2 * T), 1)
        x = jnp.where(x < T, x, x - 2 * T)
        for di, d in enumerate(range(-n_near, n_near + 1)):
            g = _bias_from_rel(d * T - x, tab_ref, h)
            base = jnp.broadcast_to(g[0:1, :], (LANES, 2 * T))
            for rb in range(T // LANES):
                blk = pltpu.roll(base, rb * LANES, 1, stride=1, stride_axis=0)
                bias_ref[di, rb * LANES:(rb + 1) * LANES, :] = blk[:, :T]

    c_left = tab_ref[FIRST_BUCKET, h]
    c_right = tab_ref[LAST_BUCKET, h]

    bias_max = tab_ref[0, h]
    for b in range(1, N_REL_BUCKETS):
        bias_max = jnp.maximum(bias_max, tab_ref[b, h])

    for mi in range(2):
        q = qts[mi][...].astype(F32)
        qnorm2 = jnp.sum(q * q, axis=0, keepdims=True)
        shift_ref[mi] = jnp.sqrt(qnorm2 * knorm_ref[mi]) * SHIFT_MARGIN + bias_max
    l_ref[...] = jnp.zeros(l_ref.shape, F32)
    acc_ref[...] = jnp.zeros(acc_ref.shape, F32)
    p_ref[n_chains - 1] = jnp.zeros((T, QP), BF16)
    prev_ref[0] = 0

    def score_stage(i, k0, bias_di, bias_const):
        mi, qs = chain_of(i)
        s = jnp.dot(ks[mi][pl.ds(k0, T), :], qts[mi][:, qs], preferred_element_type=F32)
        if bias_di is not None:
            p = jnp.exp2(s + bias_ref[bias_di, :, qs] - shift_ref[mi, :, qs])
        else:
            p = jnp.exp2(s - (shift_ref[mi, :, qs] - bias_const))
        l_ref[mi, :, qs] += jnp.sum(p, axis=0, keepdims=True)
        p_ref[i] = p.astype(BF16)

    def value_stage(i, k0):
        mi, qs = chain_of(i)
        acc_ref[mi, :, qs] += jnp.dot(vt_ref[:, pl.ds(k0, T)], p_ref[i], preferred_element_type=F32)

    def tile(kt, bias_di, bias_const):
        k0 = pl.multiple_of(kt * T, T)
        pk0 = pl.multiple_of(prev_ref[0] * T, T)
        for i in range(n_chains):
            value_stage((i - 1) % n_chains, pk0 if i < 1 else k0)
            score_stage(i, k0, bias_di, bias_const)
        prev_ref[0] = kt

    lo = jnp.maximum(qi - n_near, 0)
    hi = jnp.minimum(qi + n_near + 1, n_tiles)

    @pl.loop(0, lo)
    def _(kt):
        tile(kt, None, c_left)

    for di, d in enumerate(range(-n_near, n_near + 1)):
        kt = qi + d

        @pl.when(jnp.logical_and(kt >= 0, kt < n_tiles))
        def _(di=di, kt=kt):
            tile(kt, di, None)

    @pl.loop(hi, n_tiles)
    def _(kt):
        tile(kt, None, c_right)

    value_stage(n_chains - 1, pl.multiple_of(prev_ref[0] * T, T))

    @pl.when(jnp.min(l_ref[...]) < MIN_DENOMINATOR)
    def _():
        m_ref[...] = jnp.full(m_ref.shape, -jnp.inf, F32)
        l_ref[...] = jnp.zeros(l_ref.shape, F32)
        acc_ref[...] = jnp.zeros(acc_ref.shape, F32)

        @pl.loop(0, n_tiles)
        def _(kt):
            k0 = pl.multiple_of(kt * T, T)
            d = kt - qi
            near = jnp.abs(d) <= n_near
            di = jnp.clip(d + n_near, 0, 2 * n_near)
            c_far = jnp.where(d < 0, c_left, c_right)
            for i in range(n_chains):
                mi, qs = chain_of(i)
                s = jnp.dot(ks[mi][pl.ds(k0, T), :], qts[mi][:, qs], preferred_element_type=F32)
                s = s + jnp.where(near, bias_ref[di, :, qs], c_far)
                m = m_ref[mi, :, qs]
                m_new = jnp.maximum(m, jnp.max(s, axis=0, keepdims=True))
                alpha = jnp.exp2(m - m_new)
                p = jnp.exp2(s - m_new)
                m_ref[mi, :, qs] = m_new
                l_ref[mi, :, qs] = alpha * l_ref[mi, :, qs] + jnp.sum(p, axis=0, keepdims=True)
                pv = jnp.dot(vt_ref[:, pl.ds(k0, T)], p.astype(BF16), preferred_element_type=F32)
                acc_ref[mi, :, qs] = alpha * acc_ref[mi, :, qs] + pv

    lam = lam_ref[0, 0]
    o = acc_ref[0] / l_ref[0] - lam * (acc_ref[1] / l_ref[1])
    ms = jnp.mean(o * o, axis=0, keepdims=True)
    o = o * lax.rsqrt(ms + SUBLN_EPS) * (gain_ref[...] * (1.0 - LAM_INIT))
    o_ref[...] = o.T.astype(o_ref.dtype)


def _lambda_kernel(q1_ref, k1_ref, q2_ref, k2_ref, o_ref):
    a = jnp.sum(q1_ref[...] * k1_ref[...], axis=-1, keepdims=True)
    b = jnp.sum(q2_ref[...] * k2_ref[...], axis=-1, keepdims=True)
    o_ref[...] = jnp.exp(a) - jnp.exp(b) + LAM_INIT


def _lambda(lq1, lk1, lq2, lk2):
    return pl.pallas_call(
        _lambda_kernel, out_shape=jax.ShapeDtypeStruct((1, 1), F32), name="diff_lambda",
    )(lq1, lk1, lq2, lk2)


def _diff_attention(tab_diff, proj, qt_all, vt_all, lam, gain_col):
    S = proj.shape[0]
    T = DIFF_TILE
    n_tiles = S // T
    n_near = -(-(FAR_DIST - 1) // T)
    kern = functools.partial(_diff_attn_kernel, T=T, QP=DIFF_QUERY_PANEL, n_near=n_near, n_tiles=n_tiles)
    k_block0 = DIFF_QK_COLS // HEAD_DIM
    smem = pl.BlockSpec(memory_space=pltpu.SMEM)
    return pl.pallas_call(
        kern,
        grid=(N_DIFF_HEADS, n_tiles),
        in_specs=[
            smem,
            pl.BlockSpec((HEAD_DIM, T), lambda h, i: (2 * h, i)),
            pl.BlockSpec((HEAD_DIM, T), lambda h, i: (2 * h + 1, i)),
            pl.BlockSpec((S, HEAD_DIM), lambda h, i: (0, k_block0 + 2 * h), pipeline_mode=pl.Buffered(1)),
            pl.BlockSpec((S, HEAD_DIM), lambda h, i: (0, k_block0 + 2 * h + 1), pipeline_mode=pl.Buffered(1)),
            pl.BlockSpec((DIFF_V_DIM, S), lambda h, i: (h, 0), pipeline_mode=pl.Buffered(1)),
            smem,
            pl.BlockSpec((DIFF_V_DIM, 1), lambda h, i: (0, 0)),
        ],
        out_specs=pl.BlockSpec((T, DIFF_V_DIM), lambda h, i: (i, h)),
        out_shape=jax.ShapeDtypeStruct((S, DIFF_WIDTH), BF16),
        scratch_shapes=[pltpu.VMEM((2 * n_near + 1, T, T), F32),
                        pltpu.VMEM((2, 1, T), F32), pltpu.VMEM((2, 1, T), F32),
                        pltpu.VMEM((2, DIFF_V_DIM, T), F32),
                        pltpu.VMEM((2 * (T // DIFF_QUERY_PANEL), T, DIFF_QUERY_PANEL), BF16),
                        pltpu.VMEM((2, 1, T), F32), pltpu.VMEM((2, 1, 1), F32),
                        pltpu.SMEM((1,), jnp.int32)],
        compiler_params=pltpu.CompilerParams(
            dimension_semantics=("arbitrary", "arbitrary"), vmem_limit_bytes=VMEM_LIMIT),
        name="diff_attn",
    )(tab_diff, qt_all, qt_all, proj, proj, vt_all, lam, gain_col)


def _dilated_kernel(tab_ref, q_ref, kp_ref, km_ref, kn_ref, vp_ref, vm_ref, vn_ref, o_ref, lse_ref,
                    bias_ref, kx_ref, vx_ref, *, R, B, half, dilation, n_chunks):
    c = pl.program_id(0)
    n = pl.program_id(1)
    W = B + 2 * half
    nblk = R // B

    @pl.when(jnp.logical_and(c == 0, n == 0))
    def _():
        rows = 8
        col = lax.broadcasted_iota(jnp.int32, (rows, W), 1)
        row = lax.broadcasted_iota(jnp.int32, (rows, W), 0)
        for hh in range(N_DIL_HEADS):
            def fill(r, carry, hh=hh):
                r0 = pl.multiple_of(r * rows, rows)
                off = col - half - (row + r0)
                bias = _bias_from_rel(off * dilation, tab_ref, N_DIFF_HEADS + hh)
                base = jnp.where(jnp.abs(off) <= half, bias, NEG_INF * LOG2E)
                bias_ref[hh, 1, pl.ds(r0, rows), :] = base
                bias_ref[hh, 0, pl.ds(r0, rows), :] = jnp.where(col >= half, base, NEG_INF * LOG2E)
                bias_ref[hh, 2, pl.ds(r0, rows), :] = jnp.where(col < B + half, base, NEG_INF * LOG2E)
                return carry
            lax.fori_loop(0, B // rows, fill, 0)

    kx_ref[0:half, :] = kp_ref[...]
    kx_ref[half:half + R, :] = km_ref[...]
    kx_ref[half + R:, :] = kn_ref[...]
    vx_ref[0:half, :] = vp_ref[...]
    vx_ref[half:half + R, :] = vm_ref[...]
    vx_ref[half + R:, :] = vn_ref[...]

    def chain(hh, b):
        c0 = pl.multiple_of(hh * HEAD_DIM, HEAD_DIM)
        r0 = b * B
        var = 1
        if b == 0:
            var = jnp.where(n == 0, 0, var)
        if b == nblk - 1:
            var = jnp.where(n == n_chunks - 1, 2, var)
        q = q_ref[pl.ds(r0, B), pl.ds(c0, HEAD_DIM)]
        k = kx_ref[pl.ds(r0, W), pl.ds(c0, HEAD_DIM)]
        v = vx_ref[pl.ds(r0, W), pl.ds(c0, HEAD_DIM)]
        s = lax.dot_general(q, k, (((1,), (1,)), ((), ())), preferred_element_type=F32)
        s = s + bias_ref[hh, var]
        m = jnp.max(s, axis=-1, keepdims=True)
        e = jnp.exp2(s - m)
        den = jnp.sum(e, axis=-1, keepdims=True)
        o = jnp.dot(e.astype(BF16), v, preferred_element_type=F32) / den
        o_ref[pl.ds(r0, B), pl.ds(c0, HEAD_DIM)] = o.astype(o_ref.dtype)
        lse = m + jnp.log2(den)
        lse_ref[pl.ds(r0, B), pl.ds(c0, HEAD_DIM)] = jnp.broadcast_to(lse, (B, HEAD_DIM))

    @pl.loop(0, N_DIL_HEADS // DIL_HEAD_UNROLL)
    def _(hg):
        for u in range(DIL_HEAD_UNROLL):
            for b in range(nblk):
                chain(hg * DIL_HEAD_UNROLL + u, b)


def _dilated_pattern(tab, qkv, col_blk0, window, dilation, *, B=256):
    _, L, _ = qkv.shape
    R = min(DIL_CHUNK, L)
    half = window // (2 * dilation)
    assert L % R == 0 and R % B == 0 and half % BF16_SUBLANES == 0 and R % half == 0
    n_chunks = L // R
    q_blk, k_blk, v_blk = col_blk0, col_blk0 + 1, col_blk0 + 2
    hb = R // half
    n_hblk = L // half

    def main(blk):
        return pl.BlockSpec((None, R, DIL_WIDTH), lambda c, n: (c, n, blk))

    def prev(blk):
        return pl.BlockSpec((None, half, DIL_WIDTH), lambda c, n: (c, jnp.maximum(n * hb - 1, 0), blk))

    def nxt(blk):
        return pl.BlockSpec((None, half, DIL_WIDTH), lambda c, n: (c, jnp.minimum((n + 1) * hb, n_hblk - 1), blk))

    kern = functools.partial(_dilated_kernel, R=R, B=B, half=half, dilation=dilation, n_chunks=n_chunks)
    out_spec = pl.BlockSpec((None, R, DIL_WIDTH), lambda c, n: (c, n, 0))
    return pl.pallas_call(
        kern,
        grid=(dilation, n_chunks),
        in_specs=[pl.BlockSpec(memory_space=pltpu.SMEM),
                  main(q_blk), prev(k_blk), main(k_blk), nxt(k_blk), prev(v_blk), main(v_blk), nxt(v_blk)],
        out_specs=[out_spec, out_spec],
        out_shape=[jax.ShapeDtypeStruct((dilation, L, DIL_WIDTH), BF16),
                   jax.ShapeDtypeStruct((dilation, L, DIL_WIDTH), F32)],
        scratch_shapes=[pltpu.VMEM((N_DIL_HEADS, 3, B, B + 2 * half), F32),
                        pltpu.VMEM((R + 2 * half, DIL_WIDTH), BF16),
                        pltpu.VMEM((R + 2 * half, DIL_WIDTH), BF16)],
        compiler_params=pltpu.CompilerParams(
            dimension_semantics=("arbitrary", "arbitrary"), vmem_limit_bytes=VMEM_LIMIT),
        name=f"dilated_d{dilation}",
    )(tab, qkv, qkv, qkv, qkv, qkv, qkv, qkv)


def _combine_kernel(*refs, dilations):
    n = len(dilations)
    o_refs, l_refs = refs[:n], refs[n:2 * n]
    g_ref, out_ref = refs[2 * n], refs[2 * n + 1]
    scratch = refs[2 * n + 2:]
    tm = out_ref.shape[0]

    for hh in range(N_DIL_HEADS):
        sl = slice(hh * HEAD_DIM, (hh + 1) * HEAD_DIM)
        outs, lses = [], []
        si = 0
        for o_ref, l_ref, dil in zip(o_refs, l_refs, dilations):
            if dil == 1:
                outs.append(o_ref[0, :, sl].astype(F32))
                lses.append(l_ref[0, :, sl])
                continue
            os_ref, ls_ref = scratch[si], scratch[si + 1]
            si += 2
            for c in range(dil):
                os_ref[hh, pl.ds(c, tm // dil, stride=dil), :] = o_ref[c, :, sl].astype(F32)
                ls_ref[hh, pl.ds(c, tm // dil, stride=dil), :] = l_ref[c, :, sl]
            outs.append(os_ref[hh])
            lses.append(ls_ref[hh])

        m = functools.reduce(jnp.maximum, lses)
        ws = [jnp.exp2(l - m) for l in lses]
        tot = functools.reduce(lambda a, b: a + b, ws)
        oh = functools.reduce(lambda a, b: a + b, [(w / tot) * op for w, op in zip(ws, outs)])
        ms = jnp.mean(oh * oh, axis=-1, keepdims=True)
        out_ref[:, sl] = (oh * lax.rsqrt(ms + NORM_EPS) * g_ref[:, sl]).astype(out_ref.dtype)


def _combine(outs, lses, gain, dilations, *, tm=512):
    S = outs[0].shape[0] * outs[0].shape[1]
    specs = [pl.BlockSpec((d, tm // d, DIL_WIDTH), lambda i: (0, i, 0)) for d in dilations]
    n_scr = sum(1 for d in dilations if d != 1)
    return pl.pallas_call(
        functools.partial(_combine_kernel, dilations=dilations),
        grid=(S // tm,),
        in_specs=specs + specs + [pl.BlockSpec((1, DIL_WIDTH), lambda i: (0, 0))],
        scratch_shapes=[pltpu.VMEM((N_DIL_HEADS, tm, HEAD_DIM), F32)] * (2 * n_scr),
        out_specs=pl.BlockSpec((tm, DIL_WIDTH), lambda i: (i, 0)),
        out_shape=jax.ShapeDtypeStruct((S, DIL_WIDTH), BF16),
        compiler_params=pltpu.CompilerParams(
            dimension_semantics=("arbitrary",), vmem_limit_bytes=VMEM_LIMIT),
        name="dilated_combine",
    )(*outs, *lses, gain)


def _key_norm2_kernel(k_ref, o_ref):
    @pl.when(pl.program_id(0) == 0)
    def _():
        o_ref[...] = jnp.zeros(o_ref.shape, F32)

    k = k_ref[...].astype(F32)
    for hh in range(N_DIL_HEADS):
        kh = k[:, hh * HEAD_DIM:(hh + 1) * HEAD_DIM]
        n2 = jnp.max(jnp.sum(kh * kh, axis=1, keepdims=True), axis=0, keepdims=True)
        o_ref[hh:hh + 1, :] = jnp.maximum(o_ref[hh:hh + 1, :], jnp.broadcast_to(n2, (1, LANES)))


def _key_norm2(proj, k_blk, *, tm=1024):
    S = proj.shape[0]
    return pl.pallas_call(
        _key_norm2_kernel,
        grid=(S // tm,),
        in_specs=[pl.BlockSpec((tm, DIL_WIDTH), lambda i: (i, k_blk))],
        out_specs=pl.BlockSpec((N_DIL_HEADS, LANES), lambda i: (0, 0)),
        out_shape=jax.ShapeDtypeStruct((N_DIL_HEADS, LANES), F32),
        compiler_params=pltpu.CompilerParams(dimension_semantics=("arbitrary",), vmem_limit_bytes=VMEM_LIMIT),
        name="dilated_key_norm",
    )(proj)


def _dilated_fast_kernel(tab_ref, kn2_ref, q_ref, kp_ref, km_ref, kn_ref, vp_ref, vm_ref, vn_ref, num_ref, den_ref,
                         bias_ref, kx_ref, vx_ref, p_ref, *, R, B, half, dilation, n_chunks):
    c = pl.program_id(0)
    n = pl.program_id(1)
    W = B + 2 * half
    nblk = R // B
    VW = 2 * HEAD_DIM

    @pl.when(jnp.logical_and(c == 0, n == 0))
    def _():
        rows = 8
        col = lax.broadcasted_iota(jnp.int32, (rows, W), 1)
        row = lax.broadcasted_iota(jnp.int32, (rows, W), 0)
        for hh in range(N_DIL_HEADS):
            def fill(r, carry, hh=hh):
                r0 = pl.multiple_of(r * rows, rows)
                off = col - half - (row + r0)
                bias = _bias_from_rel(off * dilation, tab_ref, N_DIFF_HEADS + hh)
                base = jnp.where(jnp.abs(off) <= half, bias, NEG_INF * LOG2E)
                bias_ref[hh, 1, pl.ds(r0, rows), :] = base
                bias_ref[hh, 0, pl.ds(r0, rows), :] = jnp.where(col >= half, base, NEG_INF * LOG2E)
                bias_ref[hh, 2, pl.ds(r0, rows), :] = jnp.where(col < B + half, base, NEG_INF * LOG2E)
                return carry
            lax.fori_loop(0, B // rows, fill, 0)
        vx_ref[...] = jnp.ones(vx_ref.shape, BF16)

    kx_ref[0:half, :] = kp_ref[...]
    kx_ref[half:half + R, :] = km_ref[...]
    kx_ref[half + R:, :] = kn_ref[...]
    for hh in range(N_DIL_HEADS):
        src = slice(hh * HEAD_DIM, (hh + 1) * HEAD_DIM)
        dst = slice(hh * VW, hh * VW + HEAD_DIM)
        vx_ref[0:half, dst] = vp_ref[:, src]
        vx_ref[half:half + R, dst] = vm_ref[:, src]
        vx_ref[half + R:, dst] = vn_ref[:, src]

    n_slots = p_ref.shape[0]
    bias_max = []
    for hh in range(N_DIL_HEADS):
        bm = tab_ref[0, N_DIFF_HEADS + hh]
        for bk in range(1, N_REL_BUCKETS):
            bm = jnp.maximum(bm, tab_ref[bk, N_DIFF_HEADS + hh])
        bias_max.append(bm)

    def score_stage(hh, b, slot):
        cs = slice(hh * HEAD_DIM, (hh + 1) * HEAD_DIM)
        r0 = b * B
        var = 1
        if b == 0:
            var = jnp.where(n == 0, 0, var)
        if b == nblk - 1:
            var = jnp.where(n == n_chunks - 1, 2, var)
        q = q_ref[r0:r0 + B, cs]
        qf = q.astype(F32)
        qn2 = jnp.sum(qf * qf, axis=1, keepdims=True)
        shift = jnp.sqrt(qn2 * kn2_ref[hh:hh + 1, 0:1]) * SHIFT_MARGIN + bias_max[hh]
        s = lax.dot_general(q, kx_ref[r0:r0 + W, cs], (((1,), (1,)), ((), ())), preferred_element_type=F32)
        p_ref[slot] = jnp.exp2(s + bias_ref[hh, var] - shift).astype(BF16)

    def value_stage(hh, b, slot):
        cs = slice(hh * HEAD_DIM, (hh + 1) * HEAD_DIM)
        r0 = b * B
        nd = jnp.dot(p_ref[slot], vx_ref[r0:r0 + W, hh * VW:(hh + 1) * VW], preferred_element_type=F32)
        num_ref[r0:r0 + B, cs] = nd[:, :HEAD_DIM].astype(num_ref.dtype)
        den_ref[r0:r0 + B, cs] = nd[:, HEAD_DIM:].astype(den_ref.dtype)

    chains = [(hh, b) for hh in range(N_DIL_HEADS) for b in range(nblk)]
    for i, (hh, b) in enumerate(chains):
        if i > 0:
            value_stage(*chains[i - 1], (i - 1) % n_slots)
        score_stage(hh, b, i % n_slots)
    value_stage(*chains[-1], (len(chains) - 1) % n_slots)


def _dilated_fast(tab, kn2, qkv, col_blk0, window, dilation, *, B=128):
    _, L, _ = qkv.shape
    R = min(DIL_CHUNK, L)
    half = window // (2 * dilation)
    assert L % R == 0 and R % B == 0 and half % BF16_SUBLANES == 0 and R % half == 0
    n_chunks = L // R
    q_blk, k_blk, v_blk = col_blk0, col_blk0 + 1, col_blk0 + 2
    hb = R // half
    n_hblk = L // half

    def main(blk):
        return pl.BlockSpec((None, R, DIL_WIDTH), lambda c, n: (c, n, blk))

    def prev(blk):
        return pl.BlockSpec((None, half, DIL_WIDTH), lambda c, n: (c, jnp.maximum(n * hb - 1, 0), blk))

    def nxt(blk):
        return pl.BlockSpec((None, half, DIL_WIDTH), lambda c, n: (c, jnp.minimum((n + 1) * hb, n_hblk - 1), blk))

    kern = functools.partial(_dilated_fast_kernel, R=R, B=B, half=half, dilation=dilation, n_chunks=n_chunks)
    out_spec = pl.BlockSpec((None, R, DIL_WIDTH), lambda c, n: (c, n, 0))
    return pl.pallas_call(
        kern,
        grid=(dilation, n_chunks),
        in_specs=[pl.BlockSpec(memory_space=pltpu.SMEM),
                  pl.BlockSpec((N_DIL_HEADS, LANES), lambda c, n: (0, 0)),
                  main(q_blk), prev(k_blk), main(k_blk), nxt(k_blk), prev(v_blk), main(v_blk), nxt(v_blk)],
        out_specs=[out_spec, out_spec],
        out_shape=[jax.ShapeDtypeStruct((dilation, L, DIL_WIDTH), BF16),
                   jax.ShapeDtypeStruct((dilation, L, DIL_WIDTH), BF16)],
        scratch_shapes=[pltpu.VMEM((N_DIL_HEADS, 3, B, B + 2 * half), F32),
                        pltpu.VMEM((R + 2 * half, DIL_WIDTH), BF16),
                        pltpu.VMEM((R + 2 * half, 2 * DIL_WIDTH), BF16),
                        pltpu.VMEM((4, B, B + 2 * half), BF16)],
        compiler_params=pltpu.CompilerParams(
            dimension_semantics=("arbitrary", "arbitrary"), vmem_limit_bytes=VMEM_LIMIT),
        name=f"dilated_fast_d{dilation}",
    )(tab, kn2, qkv, qkv, qkv, qkv, qkv, qkv, qkv)


def _combine_fast_kernel(*refs, dilations):
    n = len(dilations)
    n_refs, d_refs = refs[:n], refs[n:2 * n]
    g_ref, out_ref, dmin_ref = refs[2 * n], refs[2 * n + 1], refs[2 * n + 2]
    scratch = refs[2 * n + 3:]
    tm = out_ref.shape[0]

    for hh in range(N_DIL_HEADS):
        sl = slice(hh * HEAD_DIM, (hh + 1) * HEAD_DIM)
        nums, dens = [], []
        si = 0
        for n_ref, d_ref, dil in zip(n_refs, d_refs, dilations):
            if dil == 1:
                nums.append(n_ref[0, :, sl].astype(F32))
                dens.append(d_ref[0, :, sl].astype(F32))
                continue
            ns_ref, ds_ref = scratch[si], scratch[si + 1]
            si += 2
            for c in range(dil):
                ns_ref[hh, pl.ds(c, tm // dil, stride=dil), :] = n_ref[c, :, sl].astype(F32)
                ds_ref[hh, pl.ds(c, tm // dil, stride=dil), :] = d_ref[c, :, sl].astype(F32)
            nums.append(ns_ref[hh])
            dens.append(ds_ref[hh])

        den = functools.reduce(lambda a, b: a + b, dens)
        oh = functools.reduce(lambda a, b: a + b, nums) / den
        ms = jnp.mean(oh * oh, axis=-1, keepdims=True)
        out_ref[:, sl] = (oh * lax.rsqrt(ms + NORM_EPS) * g_ref[:, sl]).astype(out_ref.dtype)
        dmin_ref[0, hh:hh + 1, :] = jnp.min(den, axis=0, keepdims=True)


def _combine_fast(nums, dens, gain, dilations, *, tm=512):
    S = nums[0].shape[0] * nums[0].shape[1]
    specs = [pl.BlockSpec((d, tm // d, DIL_WIDTH), lambda i: (0, i, 0)) for d in dilations]
    n_scr = sum(1 for d in dilations if d != 1)
    return pl.pallas_call(
        functools.partial(_combine_fast_kernel, dilations=dilations),
        grid=(S // tm,),
        in_specs=specs + specs + [pl.BlockSpec((1, DIL_WIDTH), lambda i: (0, 0))],
        scratch_shapes=[pltpu.VMEM((N_DIL_HEADS, tm, HEAD_DIM), F32)] * (2 * n_scr),
        out_specs=[pl.BlockSpec((tm, DIL_WIDTH), lambda i: (i, 0)),
                   pl.BlockSpec((1, N_DIL_HEADS, LANES), lambda i: (i, 0, 0))],
        out_shape=[jax.ShapeDtypeStruct((S, DIL_WIDTH), BF16),
                   jax.ShapeDtypeStruct((S // tm, N_DIL_HEADS, LANES), F32)],
        compiler_params=pltpu.CompilerParams(
            dimension_semantics=("arbitrary",), vmem_limit_bytes=VMEM_LIMIT),
        name="dilated_combine_fast",
    )(*nums, *dens, gain)


def _outproj_kernel(od_ref, ol_ref, wd_ref, wl_ref, x_ref, g_ref, x1_ref, h2_ref):
    acc = jnp.dot(od_ref[...], wd_ref[...], preferred_element_type=F32)
    acc = acc + jnp.dot(ol_ref[...], wl_ref[...], preferred_element_type=F32)
    x1 = x_ref[...] + acc
    x1_ref[...] = x1
    ms = jnp.mean(x1 * x1, axis=-1, keepdims=True)
    h2_ref[...] = (x1 * lax.rsqrt(ms + NORM_EPS) * g_ref[...]).astype(h2_ref.dtype)


def _out_proj(o_d, o_l, w_bf, x2, gain, *, tm=512):
    S, D = x2.shape
    return pl.pallas_call(
        _outproj_kernel,
        grid=(S // tm,),
        in_specs=[
            pl.BlockSpec((tm, DIFF_WIDTH), lambda i: (i, 0)),
            pl.BlockSpec((tm, DIL_WIDTH), lambda i: (i, 0)),
            pl.BlockSpec((DIFF_WIDTH, D), lambda i: (0, 0)),
            pl.BlockSpec((DIL_WIDTH, D), lambda i: (1, 0)),
            pl.BlockSpec((tm, D), lambda i: (i, 0)),
            pl.BlockSpec((1, D), lambda i: (0, 0)),
        ],
        out_specs=[pl.BlockSpec((tm, D), lambda i: (i, 0)), pl.BlockSpec((tm, D), lambda i: (i, 0))],
        out_shape=[jax.ShapeDtypeStruct((S, D), F32), jax.ShapeDtypeStruct((S, D), BF16)],
        compiler_params=pltpu.CompilerParams(
            dimension_semantics=("arbitrary",), vmem_limit_bytes=VMEM_LIMIT),
        name="out_proj",
    )(o_d, o_l, w_bf, w_bf, x2, gain)


def _ffn_up_kernel(hm_ref, hp_ref, hn_ref, wg_ref, wu_ref, cw_ref, cb_ref, o_ref, lhs_ref, *, tm, n_row_tiles):
    i = pl.program_id(0)
    j = pl.program_id(1)
    halo = BF16_SUBLANES

    @pl.when(j == 0)
    def _():
        lhs_ref[0:halo, :] = jnp.where(i == 0, jnp.zeros_like(hp_ref[...]), hp_ref[...])
        lhs_ref[halo:halo + tm, :] = hm_ref[...]
        lhs_ref[halo + tm:, :] = jnp.where(i == n_row_tiles - 1, jnp.zeros_like(hn_ref[...]), hn_ref[...])

    g = jnp.dot(lhs_ref[...], wg_ref[...], preferred_element_type=F32)
    u = jnp.dot(lhs_ref[halo:halo + tm, :], wu_ref[...], preferred_element_type=F32)
    rows = tm + 2 * halo
    g_prev = pltpu.roll(g, 1, axis=0)
    g_next = pltpu.roll(g, rows - 1, axis=0)
    y = cw_ref[0:1, :] * g_prev + cw_ref[1:2, :] * g + cw_ref[2:3, :] * g_next + cb_ref[...]
    y = y[halo:halo + tm, :]
    act = y * (1.0 / (1.0 + jnp.exp(-y))) * u
    o_ref[...] = act.astype(o_ref.dtype)


def _ffn_up(h2, w_bf, conv_w, conv_b, *, tm=1024, tn=512):
    S, D = h2.shape
    d_ff = conv_w.shape[1]
    assert d_ff % tn == 0
    nj = d_ff // tn
    ni = S // tm
    hb = tm // BF16_SUBLANES
    n_hblk = S // BF16_SUBLANES
    kern = functools.partial(_ffn_up_kernel, tm=tm, n_row_tiles=ni)
    return pl.pallas_call(
        kern,
        grid=(ni, nj),
        in_specs=[
            pl.BlockSpec((tm, D), lambda i, j: (i, 0)),
            pl.BlockSpec((BF16_SUBLANES, D), lambda i, j: (jnp.maximum(i * hb - 1, 0), 0)),
            pl.BlockSpec((BF16_SUBLANES, D), lambda i, j: (jnp.minimum((i + 1) * hb, n_hblk - 1), 0)),
            pl.BlockSpec((D, tn), lambda i, j: (0, j)),
            pl.BlockSpec((D, tn), lambda i, j: (0, nj + j)),
            pl.BlockSpec((3, tn), lambda i, j: (0, j)),
            pl.BlockSpec((1, tn), lambda i, j: (0, j)),
        ],
        out_specs=pl.BlockSpec((tm, tn), lambda i, j: (i, j)),
        out_shape=jax.ShapeDtypeStruct((S, d_ff), BF16),
        scratch_shapes=[pltpu.VMEM((tm + 2 * BF16_SUBLANES, D), BF16)],
        compiler_params=pltpu.CompilerParams(
            dimension_semantics=("arbitrary", "arbitrary"), vmem_limit_bytes=VMEM_LIMIT),
        name="ffn_up",
    )(h2, h2, h2, w_bf, w_bf, conv_w, conv_b)


def _ffn_down_kernel(a_ref, w_ref, x1_ref, g_ref, o_ref, *, n_k):
    k = pl.program_id(1)

    @pl.when(k == 0)
    def _():
        o_ref[...] = x1_ref[...]

    o_ref[...] += jnp.dot(a_ref[...], w_ref[...], preferred_element_type=F32)

    @pl.when(k == n_k - 1)
    def _():
        y = o_ref[...]
        ms = jnp.mean(y * y, axis=-1, keepdims=True)
        o_ref[...] = y * lax.rsqrt(ms + NORM_EPS) * g_ref[...]


def _ffn_down(act, w_bf, x1, gain, *, tm=1024, tk=1408):
    S, d_ff = act.shape
    D = x1.shape[1]
    n_k = d_ff // tk
    kern = functools.partial(_ffn_down_kernel, n_k=n_k)
    return pl.pallas_call(
        kern,
        grid=(S // tm, n_k),
        in_specs=[
            pl.BlockSpec((tm, tk), lambda i, k: (i, k)),
            pl.BlockSpec((tk, D), lambda i, k: (k, 0)),
            pl.BlockSpec((tm, D), lambda i, k: (i, 0)),
            pl.BlockSpec((1, D), lambda i, k: (0, 0)),
        ],
        out_specs=pl.BlockSpec((tm, D), lambda i, k: (i, 0)),
        out_shape=jax.ShapeDtypeStruct((S, D), F32),
        compiler_params=pltpu.CompilerParams(
            dimension_semantics=("arbitrary", "arbitrary"), vmem_limit_bytes=VMEM_LIMIT),
        name="ffn_down",
    )(act, w_bf, x1, gain)


def kernel(x, norm1_gain, w_in, rel_bias_table, lambda_q1, lambda_k1, lambda_q2, lambda_k2,
           diff_subln_gain, dil_out_gain, w_out, norm2_gain, w_gate_up, conv_w, conv_b, w_down, final_gain):
    B, S, D = x.shape
    assert B == 1 and w_in.shape[0] == 1
    x2 = x.reshape(S, D)
    n_cols = w_in.shape[2]

    qscale = LOG2E / math.sqrt(HEAD_DIM)
    col = np.arange(n_cols)
    dil_q0 = 2 * DIFF_QK_COLS + DIFF_WIDTH
    is_q = (col < DIFF_QK_COLS) | ((col >= dil_q0) & (col < dil_q0 + DIL_WIDTH))
    colscale = jnp.asarray(np.where(is_q, qscale, 1.0).astype(np.float32)).reshape(1, n_cols)
    tab = rel_bias_table.astype(F32) * LOG2E

    regroup = tuple(d for _, d in DILATED_PATTERNS if d != 1)
    proj, qt_all, vt_all, *cls = _in_proj(x2, norm1_gain.reshape(1, D), w_in[0].astype(BF16), colscale, regroup)
    cls_by_dil = dict(zip(regroup, cls))

    lam = _lambda(lambda_q1.reshape(1, -1), lambda_k1.reshape(1, -1),
                  lambda_q2.reshape(1, -1), lambda_k2.reshape(1, -1))
    o_d = _diff_attention(tab, proj, qt_all, vt_all, lam, diff_subln_gain.reshape(-1, 1))

    dil_blk0 = (2 * DIFF_QK_COLS + DIFF_WIDTH) // DIL_WIDTH
    dilations = tuple(d for _, d in DILATED_PATTERNS)
    dil_gain = dil_out_gain.reshape(1, -1)

    def pattern_inputs(dilation):
        if dilation == 1:
            return proj.reshape(1, S, proj.shape[1]), dil_blk0
        return cls_by_dil[dilation], 0

    kn2 = _key_norm2(proj, dil_blk0 + 1)
    nums, dens = [], []
    for window, dilation in DILATED_PATTERNS:
        n_p, d_p = _dilated_fast(tab, kn2, *pattern_inputs(dilation), window, dilation)
        nums.append(n_p)
        dens.append(d_p)
    o_l_fast, den_min = _combine_fast(nums, dens, dil_gain, dilations)

    def exact_dilated():
        outs, lses = [], []
        for window, dilation in DILATED_PATTERNS:
            o_p, lse_p = _dilated_pattern(tab, *pattern_inputs(dilation), window, dilation)
            outs.append(o_p)
            lses.append(lse_p)
        return _combine(outs, lses, dil_gain, dilations)

    o_l = lax.cond(jnp.min(den_min) < MIN_DENOMINATOR, exact_dilated, lambda: o_l_fast)

    x1, h2 = _out_proj(o_d, o_l, w_out[0].astype(BF16), x2, norm2_gain.reshape(1, D))
    act = _ffn_up(h2, w_gate_up[0].astype(BF16), conv_w[0], conv_b.reshape(1, -1))
    out = _ffn_down(act, w_down[0].astype(BF16), x1, final_gain.reshape(1, D))
    return out.reshape(B, S, D)
```

```python
import functools
import math

import numpy as np
import jax
import jax.numpy as jnp
from jax import lax
from jax.experimental import pallas as pl
from jax.experimental.pallas import tpu as pltpu

F32 = jnp.float32
BF16 = jnp.bfloat16

HEAD_DIM = 128
N_DIFF_HEADS = 4
DIFF_V_DIM = 2 * HEAD_DIM
N_DIL_HEADS = 8
DIFF_QK_COLS = N_DIFF_HEADS * 2 * HEAD_DIM
DIFF_WIDTH = N_DIFF_HEADS * DIFF_V_DIM
DIL_WIDTH = N_DIL_HEADS * HEAD_DIM
DILATED_PATTERNS = ((128, 1), (512, 4), (2048, 16))
N_REL_BUCKETS = 32
REL_MAX_DISTANCE = 1024
NORM_EPS = 1e-6
SUBLN_EPS = 1e-5
NEG_INF = -1e30
LOG2E = math.log2(math.e)
LAM_INIT = 0.8 - 0.6 * math.exp(-0.3 * 0)

LANES = 128
BF16_SUBLANES = 16
DIFF_TILE = 1024
DIFF_QUERY_PANEL = 256
SHIFT_MARGIN = 1.0 + 2.0 ** -8
MIN_DENOMINATOR = 2.0 ** -60
DIL_CHUNK = 1024
DIL_HEAD_UNROLL = 4
VMEM_LIMIT = 56 * 1024 * 1024


def _bucket_breaks():
    nb = N_REL_BUCKETS // 2
    max_exact = nb // 2
    rel = np.arange(-2 * REL_MAX_DISTANCE, 2 * REL_MAX_DISTANCE + 1)
    n = np.abs(rel)
    pos = np.log(np.maximum(n, 1) / max_exact) / math.log(REL_MAX_DISTANCE / max_exact) * (nb - max_exact)
    large = np.minimum(max_exact + np.floor(pos).astype(np.int64), nb - 1)
    bucket = np.where(rel > 0, nb, 0) + np.where(n < max_exact, n, large)
    breaks = [(int(rel[i]), int(bucket[i])) for i in range(1, len(rel)) if bucket[i] != bucket[i - 1]]
    return int(bucket[0]), breaks


FIRST_BUCKET, BUCKET_BREAKS = _bucket_breaks()
LAST_BUCKET = BUCKET_BREAKS[-1][1]
FAR_DIST = max(-BUCKET_BREAKS[0][0] + 1, BUCKET_BREAKS[-1][0])


def _bias_from_rel(rel, tab_ref, col):
    val = jnp.full(rel.shape, tab_ref[FIRST_BUCKET, col], F32)
    for thr, b in BUCKET_BREAKS:
        val = jnp.where(rel >= thr, tab_ref[b, col], val)
    return val


N_COL_GROUPS = 6
FIRST_DIL_GROUP = 3


def _inproj_kernel(x_ref, g_ref, w_ref, cs_ref, o_ref, qt_ref, vt_ref, *rest, dilations):
    cls_refs = rest[:len(dilations)]
    h_ref, stage_ref = rest[len(dilations):]
    t = pl.program_id(1)
    tm = x_ref.shape[0]
    n_dil = N_COL_GROUPS - FIRST_DIL_GROUP

    @pl.when(t == 0)
    def _():
        x = x_ref[...]
        ms = jnp.mean(x * x, axis=-1, keepdims=True)
        h_ref[...] = (x * lax.rsqrt(ms + NORM_EPS) * g_ref[...]).astype(BF16)

    def matmul():
        acc = jnp.dot(h_ref[...], w_ref[...], preferred_element_type=F32) * cs_ref[...]
        o_ref[...] = acc.astype(o_ref.dtype)
        return acc

    def stage(acc, slot):
        for cb in range(stage_ref.shape[1]):
            stage_ref[slot, cb] = acc[:, cb * LANES:(cb + 1) * LANES]

    def regroup(slot):
        for cb in range(stage_ref.shape[1]):
            sl = slice(cb * LANES, (cb + 1) * LANES)
            for cls_ref, dil in zip(cls_refs, dilations):
                for c in range(dil):
                    cls_ref[c, :, sl] = stage_ref[slot, cb, pl.ds(c, tm // dil, stride=dil), :].astype(cls_ref.dtype)

    def transpose(slot, out_ref):
        for cb in range(stage_ref.shape[1]):
            out_ref[cb * LANES:(cb + 1) * LANES, :] = stage_ref[slot, cb].T.astype(out_ref.dtype)

    for step in range(N_COL_GROUPS):
        @pl.when(t == step)
        def _(step=step):
            acc = matmul()
            if 1 <= step <= n_dil:
                regroup((step - 1) % 2)
            if step == n_dil + 1:
                transpose((step - 1) % 2, qt_ref)
            if step == n_dil + 2:
                transpose((step - 1) % 2, vt_ref)
            if step < N_COL_GROUPS - 1:
                stage(acc, step % 2)


def _in_proj(x2, gain, w_bf, colscale, dilations, *, tm=512):
    S, D = x2.shape
    N = w_bf.shape[1]
    tn = DIFF_QK_COLS
    assert DIFF_WIDTH == tn and DIL_WIDTH == tn and N == N_COL_GROUPS * tn
    n_dil = N_COL_GROUPS - FIRST_DIL_GROUP
    kern = functools.partial(_inproj_kernel, dilations=dilations)

    def group(t):
        return jnp.where(t < n_dil, t + FIRST_DIL_GROUP, jnp.where(t == n_dil, 0, N_COL_GROUPS - t))

    cls_specs = [pl.BlockSpec((d, tm // d, tn), lambda i, t: (0, i, jnp.clip(t - 1, 0, n_dil - 1)))
                 for d in dilations]
    cls_shapes = [jax.ShapeDtypeStruct((d, S // d, n_dil * tn), BF16) for d in dilations]
    return pl.pallas_call(
        kern,
        grid=(S // tm, N_COL_GROUPS),
        in_specs=[
            pl.BlockSpec((tm, D), lambda i, t: (i, 0)),
            pl.BlockSpec((1, D), lambda i, t: (0, 0)),
            pl.BlockSpec((D, tn), lambda i, t: (0, group(t))),
            pl.BlockSpec((1, tn), lambda i, t: (0, group(t))),
        ],
        out_specs=[
            pl.BlockSpec((tm, tn), lambda i, t: (i, group(t))),
            pl.BlockSpec((tn, tm), lambda i, t: (0, i)),
            pl.BlockSpec((tn, tm), lambda i, t: (0, i)),
        ] + cls_specs,
        out_shape=[
            jax.ShapeDtypeStruct((S, N), BF16),
            jax.ShapeDtypeStruct((tn, S), BF16),
            jax.ShapeDtypeStruct((tn, S), BF16),
        ] + cls_shapes,
        scratch_shapes=[pltpu.VMEM((tm, D), BF16), pltpu.VMEM((2, tn // LANES, tm, LANES), F32)],
        compiler_params=pltpu.CompilerParams(
            dimension_semantics=("arbitrary", "arbitrary"), vmem_limit_bytes=VMEM_LIMIT),
        name="in_proj",
    )(x2, gain, w_bf, colscale)


def _diff_attn_kernel(tab_ref, q1t_ref, q2t_ref, k1_ref, k2_ref, vt_ref, lam_ref, gain_ref, o_ref,
                      bias_ref, m_ref, l_ref, acc_ref, p_ref, shift_ref, knorm_ref, prev_ref,
                      *, T, QP, n_near, n_tiles):
    h = pl.program_id(0)
    qi = pl.program_id(1)
    n_chains = 2 * (T // QP)
    qts = (q1t_ref, q2t_ref)
    ks = (k1_ref, k2_ref)

    def chain_of(i):
        qp = i // 2
        return i % 2, slice(qp * QP, (qp + 1) * QP)

    @pl.when(qi == 0)
    def _():
        for mi in range(2):
            def knorm(t, best, mi=mi):
                k = ks[mi][pl.ds(pl.multiple_of(t * T, T), T), :].astype(F32)
                return jnp.maximum(best, jnp.max(jnp.sum(k * k, axis=1, keepdims=True), axis=0, keepdims=True))
            knorm_ref[mi] = lax.fori_loop(0, n_tiles, knorm, jnp.zeros((1, 1), F32))

        x = lax.broadcasted_iota(jnp.int32, (8, 2 * T), 1)
        x = jnp.where(x < T, x, x - 2 * T)
        for di, d in enumerate(range(-n_near, n_near + 1)):
            g = _bias_from_rel(d * T - x, tab_ref, h)
            base = jnp.broadcast_to(g[0:1, :], (LANES, 2 * T))
            for rb in range(T // LANES):
                blk = pltpu.roll(base, rb * LANES, 1, stride=1, stride_axis=0)
                bias_ref[di, rb * LANES:(rb + 1) * LANES, :] = blk[:, :T]

    c_left = tab_ref[FIRST_BUCKET, h]
    c_right = tab_ref[LAST_BUCKET, h]

    bias_max = tab_ref[0, h]
    for b in range(1, N_REL_BUCKETS):
        bias_max = jnp.maximum(bias_max, tab_ref[b, h])

    for mi in range(2):
        q = qts[mi][...].astype(F32)
        qnorm2 = jnp.sum(q * q, axis=0, keepdims=True)
        shift_ref[mi] = jnp.sqrt(qnorm2 * knorm_ref[mi]) * SHIFT_MARGIN + bias_max
    l_ref[...] = jnp.zeros(l_ref.shape, F32)
    acc_ref[...] = jnp.zeros(acc_ref.shape, F32)
    p_ref[n_chains - 1] = jnp.zeros((T, QP), BF16)
    prev_ref[0] = 0

    def score_stage(i, k0, bias_di, bias_const):
        mi, qs = chain_of(i)
        s = jnp.dot(ks[mi][pl.ds(k0, T), :], qts[mi][:, qs], preferred_element_type=F32)
        if bias_di is not None:
            p = jnp.exp2(s + bias_ref[bias_di, :, qs] - shift_ref[mi, :, qs])
        else:
            p = jnp.exp2(s - (shift_ref[mi, :, qs] - bias_const))
        l_ref[mi, :, qs] += jnp.sum(p, axis=0, keepdims=True)
        p_ref[i] = p.astype(BF16)

    def value_stage(i, k0):
        mi, qs = chain_of(i)
        acc_ref[mi, :, qs] += jnp.dot(vt_ref[:, pl.ds(k0, T)], p_ref[i], preferred_element_type=F32)

    def tile(kt, bias_di, bias_const):
        k0 = pl.multiple_of(kt * T, T)
        pk0 = pl.multiple_of(prev_ref[0] * T, T)
        for i in range(n_chains):
            value_stage((i - 1) % n_chains, pk0 if i < 1 else k0)
            score_stage(i, k0, bias_di, bias_const)
        prev_ref[0] = kt

    lo = jnp.maximum(qi - n_near, 0)
    hi = jnp.minimum(qi + n_near + 1, n_tiles)

    @pl.loop(0, lo)
    def _(kt):
        tile(kt, None, c_left)

    for di, d in enumerate(range(-n_near, n_near + 1)):
        kt = qi + d

        @pl.when(jnp.logical_and(kt >= 0, kt < n_tiles))
        def _(di=di, kt=kt):
            tile(kt, di, None)

    @pl.loop(hi, n_tiles)
    def _(kt):
        tile(kt, None, c_right)

    value_stage(n_chains - 1, pl.multiple_of(prev_ref[0] * T, T))

    @pl.when(jnp.min(l_ref[...]) < MIN_DENOMINATOR)
    def _():
        m_ref[...] = jnp.full(m_ref.shape, -jnp.inf, F32)
        l_ref[...] = jnp.zeros(l_ref.shape, F32)
        acc_ref[...] = jnp.zeros(acc_ref.shape, F32)

        @pl.loop(0, n_tiles)
        def _(kt):
            k0 = pl.multiple_of(kt * T, T)
            d = kt - qi
            near = jnp.abs(d) <= n_near
            di = jnp.clip(d + n_near, 0, 2 * n_near)
            c_far = jnp.where(d < 0, c_left, c_right)
            for i in range(n_chains):
                mi, qs = chain_of(i)
                s = jnp.dot(ks[mi][pl.ds(k0, T), :], qts[mi][:, qs], preferred_element_type=F32)
                s = s + jnp.where(near, bias_ref[di, :, qs], c_far)
                m = m_ref[mi, :, qs]
                m_new = jnp.maximum(m, jnp.max(s, axis=0, keepdims=True))
                alpha = jnp.exp2(m - m_new)
                p = jnp.exp2(s - m_new)
                m_ref[mi, :, qs] = m_new
                l_ref[mi, :, qs] = alpha * l_ref[mi, :, qs] + jnp.sum(p, axis=0, keepdims=True)
                pv = jnp.dot(vt_ref[:, pl.ds(k0, T)], p.astype(BF16), preferred_element_type=F32)
                acc_ref[mi, :, qs] = alpha * acc_ref[mi, :, qs] + pv

    lam = lam_ref[0, 0]
    o = acc_ref[0] / l_ref[0] - lam * (acc_ref[1] / l_ref[1])
    ms = jnp.mean(o * o, axis=0, keepdims=True)
    o = o * lax.rsqrt(ms + SUBLN_EPS) * (gain_ref[...] * (1.0 - LAM_INIT))
    o_ref[...] = o.T.astype(o_ref.dtype)


def _lambda_kernel(q1_ref, k1_ref, q2_ref, k2_ref, o_ref):
    a = jnp.sum(q1_ref[...] * k1_ref[...], axis=-1, keepdims=True)
    b = jnp.sum(q2_ref[...] * k2_ref[...], axis=-1, keepdims=True)
    o_ref[...] = jnp.exp(a) - jnp.exp(b) + LAM_INIT


def _lambda(lq1, lk1, lq2, lk2):
    return pl.pallas_call(
        _lambda_kernel, out_shape=jax.ShapeDtypeStruct((1, 1), F32), name="diff_lambda",
    )(lq1, lk1, lq2, lk2)


def _diff_attention(tab_diff, proj, qt_all, vt_all, lam, gain_col):
    S = proj.shape[0]
    T = DIFF_TILE
    n_tiles = S // T
    n_near = -(-(FAR_DIST - 1) // T)
    kern = functools.partial(_diff_attn_kernel, T=T, QP=DIFF_QUERY_PANEL, n_near=n_near, n_tiles=n_tiles)
    k_block0 = DIFF_QK_COLS // HEAD_DIM
    smem = pl.BlockSpec(memory_space=pltpu.SMEM)
    return pl.pallas_call(
        kern,
        grid=(N_DIFF_HEADS, n_tiles),
        in_specs=[
            smem,
            pl.BlockSpec((HEAD_DIM, T), lambda h, i: (2 * h, i)),
            pl.BlockSpec((HEAD_DIM, T), lambda h, i: (2 * h + 1, i)),
            pl.BlockSpec((S, HEAD_DIM), lambda h, i: (0, k_block0 + 2 * h), pipeline_mode=pl.Buffered(1)),
            pl.BlockSpec((S, HEAD_DIM), lambda h, i: (0, k_block0 + 2 * h + 1), pipeline_mode=pl.Buffered(1)),
            pl.BlockSpec((DIFF_V_DIM, S), lambda h, i: (h, 0), pipeline_mode=pl.Buffered(1)),
            smem,
            pl.BlockSpec((DIFF_V_DIM, 1), lambda h, i: (0, 0)),
        ],
        out_specs=pl.BlockSpec((T, DIFF_V_DIM), lambda h, i: (i, h)),
        out_shape=jax.ShapeDtypeStruct((S, DIFF_WIDTH), BF16),
        scratch_shapes=[pltpu.VMEM((2 * n_near + 1, T, T), F32),
                        pltpu.VMEM((2, 1, T), F32), pltpu.VMEM((2, 1, T), F32),
                        pltpu.VMEM((2, DIFF_V_DIM, T), F32),
                        pltpu.VMEM((2 * (T // DIFF_QUERY_PANEL), T, DIFF_QUERY_PANEL), BF16),
                        pltpu.VMEM((2, 1, T), F32), pltpu.VMEM((2, 1, 1), F32),
                        pltpu.SMEM((1,), jnp.int32)],
        compiler_params=pltpu.CompilerParams(
            dimension_semantics=("arbitrary", "arbitrary"), vmem_limit_bytes=VMEM_LIMIT),
        name="diff_attn",
    )(tab_diff, qt_all, qt_all, proj, proj, vt_all, lam, gain_col)


def _dilated_kernel(tab_ref, q_ref, kp_ref, km_ref, kn_ref, vp_ref, vm_ref, vn_ref, o_ref, lse_ref,
                    bias_ref, kx_ref, vx_ref, *, R, B, half, dilation, n_chunks):
    c = pl.program_id(0)
    n = pl.program_id(1)
    W = B + 2 * half
    nblk = R // B

    @pl.when(jnp.logical_and(c == 0, n == 0))
    def _():
        rows = 8
        col = lax.broadcasted_iota(jnp.int32, (rows, W), 1)
        row = lax.broadcasted_iota(jnp.int32, (rows, W), 0)
        for hh in range(N_DIL_HEADS):
            def fill(r, carry, hh=hh):
                r0 = pl.multiple_of(r * rows, rows)
                off = col - half - (row + r0)
                bias = _bias_from_rel(off * dilation, tab_ref, N_DIFF_HEADS + hh)
                base = jnp.where(jnp.abs(off) <= half, bias, NEG_INF * LOG2E)
                bias_ref[hh, 1, pl.ds(r0, rows), :] = base
                bias_ref[hh, 0, pl.ds(r0, rows), :] = jnp.where(col >= half, base, NEG_INF * LOG2E)
                bias_ref[hh, 2, pl.ds(r0, rows), :] = jnp.where(col < B + half, base, NEG_INF * LOG2E)
                return carry
            lax.fori_loop(0, B // rows, fill, 0)

    kx_ref[0:half, :] = kp_ref[...]
    kx_ref[half:half + R, :] = km_ref[...]
    kx_ref[half + R:, :] = kn_ref[...]
    vx_ref[0:half, :] = vp_ref[...]
    vx_ref[half:half + R, :] = vm_ref[...]
    vx_ref[half + R:, :] = vn_ref[...]

    def chain(hh, b):
        c0 = pl.multiple_of(hh * HEAD_DIM, HEAD_DIM)
        r0 = b * B
        var = 1
        if b == 0:
            var = jnp.where(n == 0, 0, var)
        if b == nblk - 1:
            var = jnp.where(n == n_chunks - 1, 2, var)
        q = q_ref[pl.ds(r0, B), pl.ds(c0, HEAD_DIM)]
        k = kx_ref[pl.ds(r0, W), pl.ds(c0, HEAD_DIM)]
        v = vx_ref[pl.ds(r0, W), pl.ds(c0, HEAD_DIM)]
        s = lax.dot_general(q, k, (((1,), (1,)), ((), ())), preferred_element_type=F32)
        s = s + bias_ref[hh, var]
        m = jnp.max(s, axis=-1, keepdims=True)
        e = jnp.exp2(s - m)
        den = jnp.sum(e, axis=-1, keepdims=True)
        o = jnp.dot(e.astype(BF16), v, preferred_element_type=F32) / den
        o_ref[pl.ds(r0, B), pl.ds(c0, HEAD_DIM)] = o.astype(o_ref.dtype)
        lse = m + jnp.log2(den)
        lse_ref[pl.ds(r0, B), pl.ds(c0, HEAD_DIM)] = jnp.broadcast_to(lse, (B, HEAD_DIM))

    @pl.loop(0, N_DIL_HEADS // DIL_HEAD_UNROLL)
    def _(hg):
        for u in range(DIL_HEAD_UNROLL):
            for b in range(nblk):
                chain(hg * DIL_HEAD_UNROLL + u, b)


def _dilated_pattern(tab, qkv, col_blk0, window, dilation, *, B=256):
    _, L, _ = qkv.shape
    R = min(DIL_CHUNK, L)
    half = window // (2 * dilation)
    assert L % R == 0 and R % B == 0 and half % BF16_SUBLANES == 0 and R % half == 0
    n_chunks = L // R
    q_blk, k_blk, v_blk = col_blk0, col_blk0 + 1, col_blk0 + 2
    hb = R // half
    n_hblk = L // half

    def main(blk):
        return pl.BlockSpec((None, R, DIL_WIDTH), lambda c, n: (c, n, blk))

    def prev(blk):
        return pl.BlockSpec((None, half, DIL_WIDTH), lambda c, n: (c, jnp.maximum(n * hb - 1, 0), blk))

    def nxt(blk):
        return pl.BlockSpec((None, half, DIL_WIDTH), lambda c, n: (c, jnp.minimum((n + 1) * hb, n_hblk - 1), blk))

    kern = functools.partial(_dilated_kernel, R=R, B=B, half=half, dilation=dilation, n_chunks=n_chunks)
    out_spec = pl.BlockSpec((None, R, DIL_WIDTH), lambda c, n: (c, n, 0))
    return pl.pallas_call(
        kern,
        grid=(dilation, n_chunks),
        in_specs=[pl.BlockSpec(memory_space=pltpu.SMEM),
                  main(q_blk), prev(k_blk), main(k_blk), nxt(k_blk), prev(v_blk), main(v_blk), nxt(v_blk)],
        out_specs=[out_spec, out_spec],
        out_shape=[jax.ShapeDtypeStruct((dilation, L, DIL_WIDTH), BF16),
                   jax.ShapeDtypeStruct((dilation, L, DIL_WIDTH), F32)],
        scratch_shapes=[pltpu.VMEM((N_DIL_HEADS, 3, B, B + 2 * half), F32),
                        pltpu.VMEM((R + 2 * half, DIL_WIDTH), BF16),
                        pltpu.VMEM((R + 2 * half, DIL_WIDTH), BF16)],
        compiler_params=pltpu.CompilerParams(
            dimension_semantics=("arbitrary", "arbitrary"), vmem_limit_bytes=VMEM_LIMIT),
        name=f"dilated_d{dilation}",
    )(tab, qkv, qkv, qkv, qkv, qkv, qkv, qkv)


def _combine_kernel(*refs, dilations):
    n = len(dilations)
    o_refs, l_refs = refs[:n], refs[n:2 * n]
    g_ref, out_ref = refs[2 * n], refs[2 * n + 1]
    scratch = refs[2 * n + 2:]
    tm = out_ref.shape[0]

    for hh in range(N_DIL_HEADS):
        sl = slice(hh * HEAD_DIM, (hh + 1) * HEAD_DIM)
        outs, lses = [], []
        si = 0
        for o_ref, l_ref, dil in zip(o_refs, l_refs, dilations):
            if dil == 1:
                outs.append(o_ref[0, :, sl].astype(F32))
                lses.append(l_ref[0, :, sl])
                continue
            os_ref, ls_ref = scratch[si], scratch[si + 1]
            si += 2
            for c in range(dil):
                os_ref[hh, pl.ds(c, tm // dil, stride=dil), :] = o_ref[c, :, sl].astype(F32)
                ls_ref[hh, pl.ds(c, tm // dil, stride=dil), :] = l_ref[c, :, sl]
            outs.append(os_ref[hh])
            lses.append(ls_ref[hh])

        m = functools.reduce(jnp.maximum, lses)
        ws = [jnp.exp2(l - m) for l in lses]
        tot = functools.reduce(lambda a, b: a + b, ws)
        oh = functools.reduce(lambda a, b: a + b, [(w / tot) * op for w, op in zip(ws, outs)])
        ms = jnp.mean(oh * oh, axis=-1, keepdims=True)
        out_ref[:, sl] = (oh * lax.rsqrt(ms + NORM_EPS) * g_ref[:, sl]).astype(out_ref.dtype)


def _combine(outs, lses, gain, dilations, *, tm=512):
    S = outs[0].shape[0] * outs[0].shape[1]
    specs = [pl.BlockSpec((d, tm // d, DIL_WIDTH), lambda i: (0, i, 0)) for d in dilations]
    n_scr = sum(1 for d in dilations if d != 1)
    return pl.pallas_call(
        functools.partial(_combine_kernel, dilations=dilations),
        grid=(S // tm,),
        in_specs=specs + specs + [pl.BlockSpec((1, DIL_WIDTH), lambda i: (0, 0))],
        scratch_shapes=[pltpu.VMEM((N_DIL_HEADS, tm, HEAD_DIM), F32)] * (2 * n_scr),
        out_specs=pl.BlockSpec((tm, DIL_WIDTH), lambda i: (i, 0)),
        out_shape=jax.ShapeDtypeStruct((S, DIL_WIDTH), BF16),
        compiler_params=pltpu.CompilerParams(
            dimension_semantics=("arbitrary",), vmem_limit_bytes=VMEM_LIMIT),
        name="dilated_combine",
    )(*outs, *lses, gain)


def _key_norm2_kernel(k_ref, o_ref):
    @pl.when(pl.program_id(0) == 0)
    def _():
        o_ref[...] = jnp.zeros(o_ref.shape, F32)

    k = k_ref[...].astype(F32)
    for hh in range(N_DIL_HEADS):
        kh = k[:, hh * HEAD_DIM:(hh + 1) * HEAD_DIM]
        n2 = jnp.max(jnp.sum(kh * kh, axis=1, keepdims=True), axis=0, keepdims=True)
        o_ref[hh:hh + 1, :] = jnp.maximum(o_ref[hh:hh + 1, :], jnp.broadcast_to(n2, (1, LANES)))


def _key_norm2(proj, k_blk, *, tm=1024):
    S = proj.shape[0]
    return pl.pallas_call(
        _key_norm2_kernel,
        grid=(S // tm,),
        in_specs=[pl.BlockSpec((tm, DIL_WIDTH), lambda i: (i, k_blk))],
        out_specs=pl.BlockSpec((N_DIL_HEADS, LANES), lambda i: (0, 0)),
        out_shape=jax.ShapeDtypeStruct((N_DIL_HEADS, LANES), F32),
        compiler_params=pltpu.CompilerParams(dimension_semantics=("arbitrary",), vmem_limit_bytes=VMEM_LIMIT),
        name="dilated_key_norm",
    )(proj)


def _dilated_fast_kernel(tab_ref, kn2_ref, q_ref, kp_ref, km_ref, kn_ref, vp_ref, vm_ref, vn_ref, num_ref, den_ref,
                         bias_ref, kx_ref, vx_ref, p_ref, *, R, B, half, dilation, n_chunks):
    c = pl.program_id(0)
    n = pl.program_id(1)
    W = B + 2 * half
    nblk = R // B
    VW = 2 * HEAD_DIM

    @pl.when(jnp.logical_and(c == 0, n == 0))
    def _():
        rows = 8
        col = lax.broadcasted_iota(jnp.int32, (rows, W), 1)
        row = lax.broadcasted_iota(jnp.int32, (rows, W), 0)
        for hh in range(N_DIL_HEADS):
            def fill(r, carry, hh=hh):
                r0 = pl.multiple_of(r * rows, rows)
                off = col - half - (row + r0)
                bias = _bias_from_rel(off * dilation, tab_ref, N_DIFF_HEADS + hh)
                base = jnp.where(jnp.abs(off) <= half, bias, NEG_INF * LOG2E)
                bias_ref[hh, 1, pl.ds(r0, rows), :] = base
                bias_ref[hh, 0, pl.ds(r0, rows), :] = jnp.where(col >= half, base, NEG_INF * LOG2E)
                bias_ref[hh, 2, pl.ds(r0, rows), :] = jnp.where(col < B + half, base, NEG_INF * LOG2E)
                return carry
            lax.fori_loop(0, B // rows, fill, 0)
        vx_ref[...] = jnp.ones(vx_ref.shape, BF16)

    kx_ref[0:half, :] = kp_ref[...]
    kx_ref[half:half + R, :] = km_ref[...]
    kx_ref[half + R:, :] = kn_ref[...]
    for hh in range(N_DIL_HEADS):
        src = slice(hh * HEAD_DIM, (hh + 1) * HEAD_DIM)
        dst = slice(hh * VW, hh * VW + HEAD_DIM)
        vx_ref[0:half, dst] = vp_ref[:, src]
        vx_ref[half:half + R, dst] = vm_ref[:, src]
        vx_ref[half + R:, dst] = vn_ref[:, src]

    n_slots = p_ref.shape[0]
    bias_max = []
    for hh in range(N_DIL_HEADS):
        bm = tab_ref[0, N_DIFF_HEADS + hh]
        for bk in range(1, N_REL_BUCKETS):
            bm = jnp.maximum(bm, tab_ref[bk, N_DIFF_HEADS + hh])
        bias_max.append(bm)

    def score_stage(hh, b, slot):
        cs = slice(hh * HEAD_DIM, (hh + 1) * HEAD_DIM)
        r0 = b * B
        var = 1
        if b == 0:
            var = jnp.where(n == 0, 0, var)
        if b == nblk - 1:
            var = jnp.where(n == n_chunks - 1, 2, var)
        q = q_ref[r0:r0 + B, cs]
        qf = q.astype(F32)
        qn2 = jnp.sum(qf * qf, axis=1, keepdims=True)
        shift = jnp.sqrt(qn2 * kn2_ref[hh:hh + 1, 0:1]) * SHIFT_MARGIN + bias_max[hh]
        s = lax.dot_general(q, kx_ref[r0:r0 + W, cs], (((1,), (1,)), ((), ())), preferred_element_type=F32)
        p_ref[slot] = jnp.exp2(s + bias_ref[hh, var] - shift).astype(BF16)

    def value_stage(hh, b, slot):
        cs = slice(hh * HEAD_DIM, (hh + 1) * HEAD_DIM)
        r0 = b * B
        nd = jnp.dot(p_ref[slot], vx_ref[r0:r0 + W, hh * VW:(hh + 1) * VW], preferred_element_type=F32)
        num_ref[r0:r0 + B, cs] = nd[:, :HEAD_DIM].astype(num_ref.dtype)
        den_ref[r0:r0 + B, cs] = nd[:, HEAD_DIM:].astype(den_ref.dtype)

    chains = [(hh, b) for hh in range(N_DIL_HEADS) for b in range(nblk)]
    for i, (hh, b) in enumerate(chains):
        if i > 0:
            value_stage(*chains[i - 1], (i - 1) % n_slots)
        score_stage(hh, b, i % n_slots)
    value_stage(*chains[-1], (len(chains) - 1) % n_slots)


def _dilated_fast(tab, kn2, qkv, col_blk0, window, dilation, *, B=128):
    _, L, _ = qkv.shape
    R = min(DIL_CHUNK, L)
    half = window // (2 * dilation)
    assert L % R == 0 and R % B == 0 and half % BF16_SUBLANES == 0 and R % half == 0
    n_chunks = L // R
    q_blk, k_blk, v_blk = col_blk0, col_blk0 + 1, col_blk0 + 2
    hb = R // half
    n_hblk = L // half

    def main(blk):
        return pl.BlockSpec((None, R, DIL_WIDTH), lambda c, n: (c, n, blk))

    def prev(blk):
        return pl.BlockSpec((None, half, DIL_WIDTH), lambda c, n: (c, jnp.maximum(n * hb - 1, 0), blk))

    def nxt(blk):
        return pl.BlockSpec((None, half, DIL_WIDTH), lambda c, n: (c, jnp.minimum((n + 1) * hb, n_hblk - 1), blk))

    kern = functools.partial(_dilated_fast_kernel, R=R, B=B, half=half, dilation=dilation, n_chunks=n_chunks)
    out_spec = pl.BlockSpec((None, R, DIL_WIDTH), lambda c, n: (c, n, 0))
    return pl.pallas_call(
        kern,
        grid=(dilation, n_chunks),
        in_specs=[pl.BlockSpec(memory_space=pltpu.SMEM),
                  pl.BlockSpec((N_DIL_HEADS, LANES), lambda c, n: (0, 0)),
                  main(q_blk), prev(k_blk), main(k_blk), nxt(k_blk), prev(v_blk), main(v_blk), nxt(v_blk)],
        out_specs=[out_spec, out_spec],
        out_shape=[jax.ShapeDtypeStruct((dilation, L, DIL_WIDTH), BF16),
                   jax.ShapeDtypeStruct((dilation, L, DIL_WIDTH), BF16)],
        scratch_shapes=[pltpu.VMEM((N_DIL_HEADS, 3, B, B + 2 * half), F32),
                        pltpu.VMEM((R + 2 * half, DIL_WIDTH), BF16),
                        pltpu.VMEM((R + 2 * half, 2 * DIL_WIDTH), BF16),
                        pltpu.VMEM((4, B, B + 2 * half), BF16)],
        compiler_params=pltpu.CompilerParams(
            dimension_semantics=("arbitrary", "arbitrary"), vmem_limit_bytes=VMEM_LIMIT),
        name=f"dilated_fast_d{dilation}",
    )(tab, kn2, qkv, qkv, qkv, qkv, qkv, qkv, qkv)


def _combine_fast_kernel(*refs, dilations):
    n = len(dilations)
    n_refs, d_refs = refs[:n], refs[n:2 * n]
    g_ref, out_ref, dmin_ref = refs[2 * n], refs[2 * n + 1], refs[2 * n + 2]
    scratch = refs[2 * n + 3:]
    tm = out_ref.shape[0]

    for hh in range(N_DIL_HEADS):
        sl = slice(hh * HEAD_DIM, (hh + 1) * HEAD_DIM)
        nums, dens = [], []
        si = 0
        for n_ref, d_ref, dil in zip(n_refs, d_refs, dilations):
            if dil == 1:
                nums.append(n_ref[0, :, sl].astype(F32))
                dens.append(d_ref[0, :, sl].astype(F32))
                continue
            ns_ref, ds_ref = scratch[si], scratch[si + 1]
            si += 2
            for c in range(dil):
                ns_ref[hh, pl.ds(c, tm // dil, stride=dil), :] = n_ref[c, :, sl].astype(F32)
                ds_ref[hh, pl.ds(c, tm // dil, stride=dil), :] = d_ref[c, :, sl].astype(F32)
            nums.append(ns_ref[hh])
            dens.append(ds_ref[hh])

        den = functools.reduce(lambda a, b: a + b, dens)
        oh = functools.reduce(lambda a, b: a + b, nums) / den
        ms = jnp.mean(oh * oh, axis=-1, keepdims=True)
        out_ref[:, sl] = (oh * lax.rsqrt(ms + NORM_EPS) * g_ref[:, sl]).astype(out_ref.dtype)
        dmin_ref[0, hh:hh + 1, :] = jnp.min(den, axis=0, keepdims=True)


def _combine_fast(nums, dens, gain, dilations, *, tm=512):
    S = nums[0].shape[0] * nums[0].shape[1]
    specs = [pl.BlockSpec((d, tm // d, DIL_WIDTH), lambda i: (0, i, 0)) for d in dilations]
    n_scr = sum(1 for d in dilations if d != 1)
    return pl.pallas_call(
        functools.partial(_combine_fast_kernel, dilations=dilations),
        grid=(S // tm,),
        in_specs=specs + specs + [pl.BlockSpec((1, DIL_WIDTH), lambda i: (0, 0))],
        scratch_shapes=[pltpu.VMEM((N_DIL_HEADS, tm, HEAD_DIM), F32)] * (2 * n_scr),
        out_specs=[pl.BlockSpec((tm, DIL_WIDTH), lambda i: (i, 0)),
                   pl.BlockSpec((1, N_DIL_HEADS, LANES), lambda i: (i, 0, 0))],
        out_shape=[jax.ShapeDtypeStruct((S, DIL_WIDTH), BF16),
                   jax.ShapeDtypeStruct((S // tm, N_DIL_HEADS, LANES), F32)],
        compiler_params=pltpu.CompilerParams(
            dimension_semantics=("arbitrary",), vmem_limit_bytes=VMEM_LIMIT),
        name="dilated_combine_fast",
    )(*nums, *dens, gain)


def _outproj_kernel(od_ref, ol_ref, wd_ref, wl_ref, x_ref, g_ref, x1_ref, h2_ref):
    acc = jnp.dot(od_ref[...], wd_ref[...], preferred_element_type=F32)
    acc = acc + jnp.dot(ol_ref[...], wl_ref[...], preferred_element_type=F32)
    x1 = x_ref[...] + acc
    x1_ref[...] = x1
    ms = jnp.mean(x1 * x1, axis=-1, keepdims=True)
    h2_ref[...] = (x1 * lax.rsqrt(ms + NORM_EPS) * g_ref[...]).astype(h2_ref.dtype)


def _out_proj(o_d, o_l, w_bf, x2, gain, *, tm=512):
    S, D = x2.shape
    return pl.pallas_call(
        _outproj_kernel,
        grid=(S // tm,),
        in_specs=[
            pl.BlockSpec((tm, DIFF_WIDTH), lambda i: (i, 0)),
            pl.BlockSpec((tm, DIL_WIDTH), lambda i: (i, 0)),
            pl.BlockSpec((DIFF_WIDTH, D), lambda i: (0, 0)),
            pl.BlockSpec((DIL_WIDTH, D), lambda i: (1, 0)),
            pl.BlockSpec((tm, D), lambda i: (i, 0)),
            pl.BlockSpec((1, D), lambda i: (0, 0)),
        ],
        out_specs=[pl.BlockSpec((tm, D), lambda i: (i, 0)), pl.BlockSpec((tm, D), lambda i: (i, 0))],
        out_shape=[jax.ShapeDtypeStruct((S, D), F32), jax.ShapeDtypeStruct((S, D), BF16)],
        compiler_params=pltpu.CompilerParams(
            dimension_semantics=("arbitrary",), vmem_limit_bytes=VMEM_LIMIT),
        name="out_proj",
    )(o_d, o_l, w_bf, w_bf, x2, gain)


def _ffn_up_kernel(hm_ref, hp_ref, hn_ref, wg_ref, wu_ref, cw_ref, cb_ref, o_ref, lhs_ref, *, tm, n_row_tiles):
    i = pl.program_id(0)
    j = pl.program_id(1)
    halo = BF16_SUBLANES

    @pl.when(j == 0)
    def _():
        lhs_ref[0:halo, :] = jnp.where(i == 0, jnp.zeros_like(hp_ref[...]), hp_ref[...])
        lhs_ref[halo:halo + tm, :] = hm_ref[...]
        lhs_ref[halo + tm:, :] = jnp.where(i == n_row_tiles - 1, jnp.zeros_like(hn_ref[...]), hn_ref[...])

    g = jnp.dot(lhs_ref[...], wg_ref[...], preferred_element_type=F32)
    u = jnp.dot(lhs_ref[halo:halo + tm, :], wu_ref[...], preferred_element_type=F32)
    rows = tm + 2 * halo
    g_prev = pltpu.roll(g, 1, axis=0)
    g_next = pltpu.roll(g, rows - 1, axis=0)
    y = cw_ref[0:1, :] * g_prev + cw_ref[1:2, :] * g + cw_ref[2:3, :] * g_next + cb_ref[...]
    y = y[halo:halo + tm, :]
    act = y * (1.0 / (1.0 + jnp.exp(-y))) * u
    o_ref[...] = act.astype(o_ref.dtype)


def _ffn_up(h2, w_bf, conv_w, conv_b, *, tm=1024, tn=512):
    S, D = h2.shape
    d_ff = conv_w.shape[1]
    assert d_ff % tn == 0
    nj = d_ff // tn
    ni = S // tm
    hb = tm // BF16_SUBLANES
    n_hblk = S // BF16_SUBLANES
    kern = functools.partial(_ffn_up_kernel, tm=tm, n_row_tiles=ni)
    return pl.pallas_call(
        kern,
        grid=(ni, nj),
        in_specs=[
            pl.BlockSpec((tm, D), lambda i, j: (i, 0)),
            pl.BlockSpec((BF16_SUBLANES, D), lambda i, j: (jnp.maximum(i * hb - 1, 0), 0)),
            pl.BlockSpec((BF16_SUBLANES, D), lambda i, j: (jnp.minimum((i + 1) * hb, n_hblk - 1), 0)),
            pl.BlockSpec((D, tn), lambda i, j: (0, j)),
            pl.BlockSpec((D, tn), lambda i, j: (0, nj + j)),
            pl.BlockSpec((3, tn), lambda i, j: (0, j)),
            pl.BlockSpec((1, tn), lambda i, j: (0, j)),
        ],
        out_specs=pl.BlockSpec((tm, tn), lambda i, j: (i, j)),
        out_shape=jax.ShapeDtypeStruct((S, d_ff), BF16),
        scratch_shapes=[pltpu.VMEM((tm + 2 * BF16_SUBLANES, D), BF16)],
        compiler_params=pltpu.CompilerParams(
            dimension_semantics=("arbitrary", "arbitrary"), vmem_limit_bytes=VMEM_LIMIT),
        name="ffn_up",
    )(h2, h2, h2, w_bf, w_bf, conv_w, conv_b)


def _ffn_down_kernel(a_ref, w_ref, x1_ref, g_ref, o_ref, *, n_k):
    k = pl.program_id(1)

    @pl.when(k == 0)
    def _():
        o_ref[...] = x1_ref[...]

    o_ref[...] += jnp.dot(a_ref[...], w_ref[...], preferred_element_type=F32)

    @pl.when(k == n_k - 1)
    def _():
        y = o_ref[...]
        ms = jnp.mean(y * y, axis=-1, keepdims=True)
        o_ref[...] = y * lax.rsqrt(ms + NORM_EPS) * g_ref[...]


def _ffn_down(act, w_bf, x1, gain, *, tm=1024, tk=1408):
    S, d_ff = act.shape
    D = x1.shape[1]
    n_k = d_ff // tk
    kern = functools.partial(_ffn_down_kernel, n_k=n_k)
    return pl.pallas_call(
        kern,
        grid=(S // tm, n_k),
        in_specs=[
            pl.BlockSpec((tm, tk), lambda i, k: (i, k)),
            pl.BlockSpec((tk, D), lambda i, k: (k, 0)),
            pl.BlockSpec((tm, D), lambda i, k: (i, 0)),
            pl.BlockSpec((1, D), lambda i, k: (0, 0)),
        ],
        out_specs=pl.BlockSpec((tm, D), lambda i, k: (i, 0)),
        out_shape=jax.ShapeDtypeStruct((S, D), F32),
        compiler_params=pltpu.CompilerParams(
            dimension_semantics=("arbitrary", "arbitrary"), vmem_limit_bytes=VMEM_LIMIT),
        name="ffn_down",
    )(act, w_bf, x1, gain)


def kernel(x, norm1_gain, w_in, rel_bias_table, lambda_q1, lambda_k1, lambda_q2, lambda_k2,
           diff_subln_gain, dil_out_gain, w_out, norm2_gain, w_gate_up, conv_w, conv_b, w_down, final_gain):
    B, S, D = x.shape
    assert B == 1 and w_in.shape[0] == 1
    x2 = x.reshape(S, D)
    n_cols = w_in.shape[2]

    qscale = LOG2E / math.sqrt(HEAD_DIM)
    col = np.arange(n_cols)
    dil_q0 = 2 * DIFF_QK_COLS + DIFF_WIDTH
    is_q = (col < DIFF_QK_COLS) | ((col >= dil_q0) & (col < dil_q0 + DIL_WIDTH))
    colscale = jnp.asarray(np.where(is_q, qscale, 1.0).astype(np.float32)).reshape(1, n_cols)
    tab = rel_bias_table.astype(F32) * LOG2E

    regroup = tuple(d for _, d in DILATED_PATTERNS if d != 1)
    proj, qt_all, vt_all, *cls = _in_proj(x2, norm1_gain.reshape(1, D), w_in[0].astype(BF16), colscale, regroup)
    cls_by_dil = dict(zip(regroup, cls))

    lam = _lambda(lambda_q1.reshape(1, -1), lambda_k1.reshape(1, -1),
                  lambda_q2.reshape(1, -1), lambda_k2.reshape(1, -1))
    o_d = _diff_attention(tab, proj, qt_all, vt_all, lam, diff_subln_gain.reshape(-1, 1))

    dil_blk0 = (2 * DIFF_QK_COLS + DIFF_WIDTH) // DIL_WIDTH
    dilations = tuple(d for _, d in DILATED_PATTERNS)
    dil_gain = dil_out_gain.reshape(1, -1)

    def pattern_inputs(dilation):
        if dilation == 1:
            return proj.reshape(1, S, proj.shape[1]), dil_blk0
        return cls_by_dil[dilation], 0

    kn2 = _key_norm2(proj, dil_blk0 + 1)
    nums, dens = [], []
    for window, dilation in DILATED_PATTERNS:
        n_p, d_p = _dilated_fast(tab, kn2, *pattern_inputs(dilation), window, dilation)
        nums.append(n_p)
        dens.append(d_p)
    o_l_fast, den_min = _combine_fast(nums, dens, dil_gain, dilations)

    def exact_dilated():
        outs, lses = [], []
        for window, dilation in DILATED_PATTERNS:
            o_p, lse_p = _dilated_pattern(tab, *pattern_inputs(dilation), window, dilation)
            outs.append(o_p)
            lses.append(lse_p)
        return _combine(outs, lses, dil_gain, dilations)

    o_l = lax.cond(jnp.min(den_min) < MIN_DENOMINATOR, exact_dilated, lambda: o_l_fast)

    x1, h2 = _out_proj(o_d, o_l, w_out[0].astype(BF16), x2, norm2_gain.reshape(1, D))
    act = _ffn_up(h2, w_gate_up[0].astype(BF16), conv_w[0], conv_b.reshape(1, -1))
    out = _ffn_down(act, w_down[0].astype(BF16), x1, final_gain.reshape(1, D))
    return out.reshape(B, S, D)
```

```python
import functools
import math

import numpy as np
import jax
import jax.numpy as jnp
from jax import lax
from jax.experimental import pallas as pl
from jax.experimental.pallas import tpu as pltpu

F32 = jnp.float32
BF16 = jnp.bfloat16

HEAD_DIM = 128
N_DIFF_HEADS = 4
DIFF_V_DIM = 2 * HEAD_DIM
N_DIL_HEADS = 8
DIFF_QK_COLS = N_DIFF_HEADS * 2 * HEAD_DIM
DIFF_WIDTH = N_DIFF_HEADS * DIFF_V_DIM
DIL_WIDTH = N_DIL_HEADS * HEAD_DIM
DILATED_PATTERNS = ((128, 1), (512, 4), (2048, 16))
N_REL_BUCKETS = 32
REL_MAX_DISTANCE = 1024
NORM_EPS = 1e-6
SUBLN_EPS = 1e-5
NEG_INF = -1e30
LOG2E = math.log2(math.e)
LAM_INIT = 0.8 - 0.6 * math.exp(-0.3 * 0)

LANES = 128
BF16_SUBLANES = 16
DIFF_TILE = 1024
DIFF_QUERY_PANEL = 256
SHIFT_MARGIN = 1.0 + 2.0 ** -8
MIN_DENOMINATOR = 2.0 ** -60
DIL_CHUNK = 1024
DIL_HEAD_UNROLL = 4
VMEM_LIMIT = 56 * 1024 * 1024


def _bucket_breaks():
    nb = N_REL_BUCKETS // 2
    max_exact = nb // 2
    rel = np.arange(-2 * REL_MAX_DISTANCE, 2 * REL_MAX_DISTANCE + 1)
    n = np.abs(rel)
    pos = np.log(np.maximum(n, 1) / max_exact) / math.log(REL_MAX_DISTANCE / max_exact) * (nb - max_exact)
    large = np.minimum(max_exact + np.floor(pos).astype(np.int64), nb - 1)
    bucket = np.where(rel > 0, nb, 0) + np.where(n < max_exact, n, large)
    breaks = [(int(rel[i]), int(bucket[i])) for i in range(1, len(rel)) if bucket[i] != bucket[i - 1]]
    return int(bucket[0]), breaks


FIRST_BUCKET, BUCKET_BREAKS = _bucket_breaks()
LAST_BUCKET = BUCKET_BREAKS[-1][1]
FAR_DIST = max(-BUCKET_BREAKS[0][0] + 1, BUCKET_BREAKS[-1][0])


def _bias_from_rel(rel, tab_ref, col):
    val = jnp.full(rel.shape, tab_ref[FIRST_BUCKET, col], F32)
    for thr, b in BUCKET_BREAKS:
        val = jnp.where(rel >= thr, tab_ref[b, col], val)
    return val


N_COL_GROUPS = 6
FIRST_DIL_GROUP = 3


def _inproj_kernel(x_ref, g_ref, w_ref, cs_ref, o_ref, qt_ref, vt_ref, *rest, dilations):
    cls_refs = rest[:len(dilations)]
    h_ref, stage_ref = rest[len(dilations):]
    t = pl.program_id(1)
    tm = x_ref.shape[0]
    n_dil = N_COL_GROUPS - FIRST_DIL_GROUP

    @pl.when(t == 0)
    def _():
        x = x_ref[...]
        ms = jnp.mean(x * x, axis=-1, keepdims=True)
        h_ref[...] = (x * lax.rsqrt(ms + NORM_EPS) * g_ref[...]).astype(BF16)

    def matmul():
        acc = jnp.dot(h_ref[...], w_ref[...], preferred_element_type=F32) * cs_ref[...]
        o_ref[...] = acc.astype(o_ref.dtype)
        return acc

    def stage(acc, slot):
        for cb in range(stage_ref.shape[1]):
            stage_ref[slot, cb] = acc[:, cb * LANES:(cb + 1) * LANES]

    def regroup(slot):
        for cb in range(stage_ref.shape[1]):
            sl = slice(cb * LANES, (cb + 1) * LANES)
            for cls_ref, dil in zip(cls_refs, dilations):
                for c in range(dil):
                    cls_ref[c, :, sl] = stage_ref[slot, cb, pl.ds(c, tm // dil, stride=dil), :].astype(cls_ref.dtype)

    for step in range(N_COL_GROUPS):
        @pl.when(t == step)
        def _(step=step):
            acc = matmul()
            if 1 <= step <= n_dil:
                regroup((step - 1) % 2)
            if step < n_dil:
                stage(acc, step % 2)
            if step == n_dil:
                qt_ref[...] = acc.T.astype(qt_ref.dtype)
            if step == n_dil + 2:
                vt_ref[...] = acc.T.astype(vt_ref.dtype)


def _in_proj(x2, gain, w_bf, colscale, dilations, *, tm=512):
    S, D = x2.shape
    N = w_bf.shape[1]
    tn = DIFF_QK_COLS
    assert DIFF_WIDTH == tn and DIL_WIDTH == tn and N == N_COL_GROUPS * tn
    n_dil = N_COL_GROUPS - FIRST_DIL_GROUP
    kern = functools.partial(_inproj_kernel, dilations=dilations)

    def group(t):
        return (t + FIRST_DIL_GROUP) % N_COL_GROUPS

    cls_specs = [pl.BlockSpec((d, tm // d, tn), lambda i, t: (0, i, jnp.clip(t - 1, 0, n_dil - 1)))
                 for d in dilations]
    cls_shapes = [jax.ShapeDtypeStruct((d, S // d, n_dil * tn), BF16) for d in dilations]
    return pl.pallas_call(
        kern,
        grid=(S // tm, N_COL_GROUPS),
        in_specs=[
            pl.BlockSpec((tm, D), lambda i, t: (i, 0)),
            pl.BlockSpec((1, D), lambda i, t: (0, 0)),
            pl.BlockSpec((D, tn), lambda i, t: (0, group(t))),
            pl.BlockSpec((1, tn), lambda i, t: (0, group(t))),
        ],
        out_specs=[
            pl.BlockSpec((tm, tn), lambda i, t: (i, group(t))),
            pl.BlockSpec((tn, tm), lambda i, t: (0, i)),
            pl.BlockSpec((tn, tm), lambda i, t: (0, i)),
        ] + cls_specs,
        out_shape=[
            jax.ShapeDtypeStruct((S, N), BF16),
            jax.ShapeDtypeStruct((tn, S), BF16),
            jax.ShapeDtypeStruct((tn, S), BF16),
        ] + cls_shapes,
        scratch_shapes=[pltpu.VMEM((tm, D), BF16), pltpu.VMEM((2, tn // LANES, tm, LANES), F32)],
        compiler_params=pltpu.CompilerParams(
            dimension_semantics=("arbitrary", "arbitrary"), vmem_limit_bytes=VMEM_LIMIT),
        name="in_proj",
    )(x2, gain, w_bf, colscale)


def _diff_attn_kernel(*refs, T, QP, n_near, n_tiles, n_cast):
    tab_ref, q1t_ref, q2t_ref, k1_ref, k2_ref, vt_ref, lam_ref, gain_ref = refs[:8]
    cast_in, o_ref, cast_out = refs[8:8 + n_cast], refs[8 + n_cast], refs[9 + n_cast:9 + 2 * n_cast]
    bias_ref, m_ref, l_ref, acc_ref, p_ref, shift_ref, knorm_ref, prev_ref = refs[9 + 2 * n_cast:]
    _diff_attn_body(tab_ref, q1t_ref, q2t_ref, k1_ref, k2_ref, vt_ref, lam_ref, gain_ref, o_ref,
                    bias_ref, m_ref, l_ref, acc_ref, p_ref, shift_ref, knorm_ref, prev_ref,
                    T=T, QP=QP, n_near=n_near, n_tiles=n_tiles)
    for src, dst in zip(cast_in, cast_out):
        dst[...] = src[...].astype(dst.dtype)


def _diff_attn_body(tab_ref, q1t_ref, q2t_ref, k1_ref, k2_ref, vt_ref, lam_ref, gain_ref, o_ref,
                    bias_ref, m_ref, l_ref, acc_ref, p_ref, shift_ref, knorm_ref, prev_ref,
                    *, T, QP, n_near, n_tiles):
    h = pl.program_id(0)
    qi = pl.program_id(1)
    n_chains = 2 * (T // QP)
    qts = (q1t_ref, q2t_ref)
    ks = (k1_ref, k2_ref)

    def chain_of(i):
        qp = i // 2
        return i % 2, slice(qp * QP, (qp + 1) * QP)

    @pl.when(qi == 0)
    def _():
        for mi in range(2):
            def knorm(t, best, mi=mi):
                k = ks[mi][pl.ds(pl.multiple_of(t * T, T), T), :].astype(F32)
                return jnp.maximum(best, jnp.max(jnp.sum(k * k, axis=1, keepdims=True), axis=0, keepdims=True))
            knorm_ref[mi] = lax.fori_loop(0, n_tiles, knorm, jnp.zeros((1, 1), F32))

        x = lax.broadcasted_iota(jnp.int32, (8, 2 * T), 1)
        x = jnp.where(x < T, x, x - 2 * T)
        for di, d in enumerate(range(-n_near, n_near + 1)):
            g = _bias_from_rel(d * T - x, tab_ref, h)
            base = jnp.broadcast_to(g[0:1, :], (LANES, 2 * T))
            for rb in range(T // LANES):
                blk = pltpu.roll(base, rb * LANES, 1, stride=1, stride_axis=0)
                bias_ref[di, rb * LANES:(rb + 1) * LANES, :] = blk[:, :T]

    c_left = tab_ref[FIRST_BUCKET, h]
    c_right = tab_ref[LAST_BUCKET, h]

    bias_max = tab_ref[0, h]
    for b in range(1, N_REL_BUCKETS):
        bias_max = jnp.maximum(bias_max, tab_ref[b, h])

    for mi in range(2):
        q = qts[mi][...].astype(F32)
        qnorm2 = jnp.sum(q * q, axis=0, keepdims=True)
        shift_ref[mi] = jnp.sqrt(qnorm2 * knorm_ref[mi]) * SHIFT_MARGIN + bias_max
    l_ref[...] = jnp.zeros(l_ref.shape, F32)
    acc_ref[...] = jnp.zeros(acc_ref.shape, F32)
    p_ref[n_chains - 1] = jnp.zeros((T, QP), BF16)
    prev_ref[0] = 0

    def score_stage(i, k0, bias_di, bias_const):
        mi, qs = chain_of(i)
        s = jnp.dot(ks[mi][pl.ds(k0, T), :], qts[mi][:, qs], preferred_element_type=F32)
        if bias_di is not None:
            p = jnp.exp2(s + bias_ref[bias_di, :, qs] - shift_ref[mi, :, qs])
        else:
            p = jnp.exp2(s - (shift_ref[mi, :, qs] - bias_const))
        l_ref[mi, :, qs] += jnp.sum(p, axis=0, keepdims=True)
        p_ref[i] = p.astype(BF16)

    def value_stage(i, k0):
        mi, qs = chain_of(i)
        acc_ref[mi, :, qs] += jnp.dot(vt_ref[:, pl.ds(k0, T)], p_ref[i], preferred_element_type=F32)

    def tile(kt, bias_di, bias_const):
        k0 = pl.multiple_of(kt * T, T)
        pk0 = pl.multiple_of(prev_ref[0] * T, T)
        for i in range(n_chains):
            value_stage((i - 1) % n_chains, pk0 if i < 1 else k0)
            score_stage(i, k0, bias_di, bias_const)
        prev_ref[0] = kt

    lo = jnp.maximum(qi - n_near, 0)
    hi = jnp.minimum(qi + n_near + 1, n_tiles)

    @pl.loop(0, lo)
    def _(kt):
        tile(kt, None, c_left)

    for di, d in enumerate(range(-n_near, n_near + 1)):
        kt = qi + d

        @pl.when(jnp.logical_and(kt >= 0, kt < n_tiles))
        def _(di=di, kt=kt):
            tile(kt, di, None)

    @pl.loop(hi, n_tiles)
    def _(kt):
        tile(kt, None, c_right)

    value_stage(n_chains - 1, pl.multiple_of(prev_ref[0] * T, T))

    @pl.when(jnp.min(l_ref[...]) < MIN_DENOMINATOR)
    def _():
        m_ref[...] = jnp.full(m_ref.shape, -jnp.inf, F32)
        l_ref[...] = jnp.zeros(l_ref.shape, F32)
        acc_ref[...] = jnp.zeros(acc_ref.shape, F32)

        @pl.loop(0, n_tiles)
        def _(kt):
            k0 = pl.multiple_of(kt * T, T)
            d = kt - qi
            near = jnp.abs(d) <= n_near
            di = jnp.clip(d + n_near, 0, 2 * n_near)
            c_far = jnp.where(d < 0, c_left, c_right)
            for i in range(n_chains):
                mi, qs = chain_of(i)
                s = jnp.dot(ks[mi][pl.ds(k0, T), :], qts[mi][:, qs], preferred_element_type=F32)
                s = s + jnp.where(near, bias_ref[di, :, qs], c_far)
                m = m_ref[mi, :, qs]
                m_new = jnp.maximum(m, jnp.max(s, axis=0, keepdims=True))
                alpha = jnp.exp2(m - m_new)
                p = jnp.exp2(s - m_new)
                m_ref[mi, :, qs] = m_new
                l_ref[mi, :, qs] = alpha * l_ref[mi, :, qs] + jnp.sum(p, axis=0, keepdims=True)
                pv = jnp.dot(vt_ref[:, pl.ds(k0, T)], p.astype(BF16), preferred_element_type=F32)
                acc_ref[mi, :, qs] = alpha * acc_ref[mi, :, qs] + pv

    lam = lam_ref[0, 0]
    o = acc_ref[0] / l_ref[0] - lam * (acc_ref[1] / l_ref[1])
    ms = jnp.mean(o * o, axis=0, keepdims=True)
    o = o * lax.rsqrt(ms + SUBLN_EPS) * (gain_ref[...] * (1.0 - LAM_INIT))
    o_ref[...] = o.T.astype(o_ref.dtype)


def _lambda_kernel(q1_ref, k1_ref, q2_ref, k2_ref, o_ref):
    a = jnp.sum(q1_ref[...] * k1_ref[...], axis=-1, keepdims=True)
    b = jnp.sum(q2_ref[...] * k2_ref[...], axis=-1, keepdims=True)
    o_ref[...] = jnp.exp(a) - jnp.exp(b) + LAM_INIT


def _lambda(lq1, lk1, lq2, lk2):
    return pl.pallas_call(
        _lambda_kernel, out_shape=jax.ShapeDtypeStruct((1, 1), F32), name="diff_lambda",
    )(lq1, lk1, lq2, lk2)


def _diff_attention(tab_diff, proj, qt_all, vt_all, lam, gain_col, f32_weights):
    S = proj.shape[0]
    T = DIFF_TILE
    n_tiles = S // T
    n_near = -(-(FAR_DIST - 1) // T)
    kern = functools.partial(_diff_attn_kernel, T=T, QP=DIFF_QUERY_PANEL, n_near=n_near, n_tiles=n_tiles,
                             n_cast=len(f32_weights))

    n_steps = N_DIFF_HEADS * n_tiles
    cast_specs = []
    for w in f32_weights:
        rows, cols = w.shape
        every = next(e for e in (1, 2, 4, 8) if rows % (n_steps // e) == 0
                     and (rows // (n_steps // e)) % BF16_SUBLANES == 0)
        cast_specs.append(pl.BlockSpec((rows // (n_steps // every), cols),
                                       lambda h, i, every=every: ((h * n_tiles + i) // every, 0)))
    k_block0 = DIFF_QK_COLS // HEAD_DIM
    smem = pl.BlockSpec(memory_space=pltpu.SMEM)
    return pl.pallas_call(
        kern,
        grid=(N_DIFF_HEADS, n_tiles),
        in_specs=[
            smem,
            pl.BlockSpec((HEAD_DIM, T), lambda h, i: (2 * h, i)),
            pl.BlockSpec((HEAD_DIM, T), lambda h, i: (2 * h + 1, i)),
            pl.BlockSpec((S, HEAD_DIM), lambda h, i: (0, k_block0 + 2 * h), pipeline_mode=pl.Buffered(1)),
            pl.BlockSpec((S, HEAD_DIM), lambda h, i: (0, k_block0 + 2 * h + 1), pipeline_mode=pl.Buffered(1)),
            pl.BlockSpec((DIFF_V_DIM, S), lambda h, i: (h, 0), pipeline_mode=pl.Buffered(1)),
            smem,
            pl.BlockSpec((DIFF_V_DIM, 1), lambda h, i: (0, 0)),
        ] + cast_specs,
        out_specs=[pl.BlockSpec((T, DIFF_V_DIM), lambda h, i: (i, h))] + cast_specs,
        out_shape=[jax.ShapeDtypeStruct((S, DIFF_WIDTH), BF16)]
        + [jax.ShapeDtypeStruct(w.shape, BF16) for w in f32_weights],
        scratch_shapes=[pltpu.VMEM((2 * n_near + 1, T, T), F32),
                        pltpu.VMEM((2, 1, T), F32), pltpu.VMEM((2, 1, T), F32),
                        pltpu.VMEM((2, DIFF_V_DIM, T), F32),
                        pltpu.VMEM((2 * (T // DIFF_QUERY_PANEL), T, DIFF_QUERY_PANEL), BF16),
                        pltpu.VMEM((2, 1, T), F32), pltpu.VMEM((2, 1, 1), F32),
                        pltpu.SMEM((1,), jnp.int32)],
        compiler_params=pltpu.CompilerParams(
            dimension_semantics=("arbitrary", "arbitrary"), vmem_limit_bytes=VMEM_LIMIT),
        name="diff_attn",
    )(tab_diff, qt_all, qt_all, proj, proj, vt_all, lam, gain_col, *f32_weights)


def _dilated_kernel(tab_ref, q_ref, kp_ref, km_ref, kn_ref, vp_ref, vm_ref, vn_ref, o_ref, lse_ref,
                    bias_ref, kx_ref, vx_ref, *, R, B, half, dilation, n_chunks):
    c = pl.program_id(0)
    n = pl.program_id(1)
    W = B + 2 * half
    nblk = R // B

    @pl.when(jnp.logical_and(c == 0, n == 0))
    def _():
        rows = 8
        col = lax.broadcasted_iota(jnp.int32, (rows, W), 1)
        row = lax.broadcasted_iota(jnp.int32, (rows, W), 0)
        for hh in range(N_DIL_HEADS):
            def fill(r, carry, hh=hh):
                r0 = pl.multiple_of(r * rows, rows)
                off = col - half - (row + r0)
                bias = _bias_from_rel(off * dilation, tab_ref, N_DIFF_HEADS + hh)
                base = jnp.where(jnp.abs(off) <= half, bias, NEG_INF * LOG2E)
                bias_ref[hh, 1, pl.ds(r0, rows), :] = base
                bias_ref[hh, 0, pl.ds(r0, rows), :] = jnp.where(col >= half, base, NEG_INF * LOG2E)
                bias_ref[hh, 2, pl.ds(r0, rows), :] = jnp.where(col < B + half, base, NEG_INF * LOG2E)
                return carry
            lax.fori_loop(0, B // rows, fill, 0)

    kx_ref[0:half, :] = kp_ref[...]
    kx_ref[half:half + R, :] = km_ref[...]
    kx_ref[half + R:, :] = kn_ref[...]
    vx_ref[0:half, :] = vp_ref[...]
    vx_ref[half:half + R, :] = vm_ref[...]
    vx_ref[half + R:, :] = vn_ref[...]

    def chain(hh, b):
        c0 = pl.multiple_of(hh * HEAD_DIM, HEAD_DIM)
        r0 = b * B
        var = 1
        if b == 0:
            var = jnp.where(n == 0, 0, var)
        if b == nblk - 1:
            var = jnp.where(n == n_chunks - 1, 2, var)
        q = q_ref[pl.ds(r0, B), pl.ds(c0, HEAD_DIM)]
        k = kx_ref[pl.ds(r0, W), pl.ds(c0, HEAD_DIM)]
        v = vx_ref[pl.ds(r0, W), pl.ds(c0, HEAD_DIM)]
        s = lax.dot_general(q, k, (((1,), (1,)), ((), ())), preferred_element_type=F32)
        s = s + bias_ref[hh, var]
        m = jnp.max(s, axis=-1, keepdims=True)
        e = jnp.exp2(s - m)
        den = jnp.sum(e, axis=-1, keepdims=True)
        o = jnp.dot(e.astype(BF16), v, preferred_element_type=F32) / den
        o_ref[pl.ds(r0, B), pl.ds(c0, HEAD_DIM)] = o.astype(o_ref.dtype)
        lse = m + jnp.log2(den)
        lse_ref[pl.ds(r0, B), pl.ds(c0, HEAD_DIM)] = jnp.broadcast_to(lse, (B, HEAD_DIM))

    @pl.loop(0, N_DIL_HEADS // DIL_HEAD_UNROLL)
    def _(hg):
        for u in range(DIL_HEAD_UNROLL):
            for b in range(nblk):
                chain(hg * DIL_HEAD_UNROLL + u, b)


def _dilated_pattern(tab, qkv, col_blk0, window, dilation, *, B=256):
    _, L, _ = qkv.shape
    R = min(DIL_CHUNK, L)
    half = window // (2 * dilation)
    assert L % R == 0 and R % B == 0 and half % BF16_SUBLANES == 0 and R % half == 0
    n_chunks = L // R
    q_blk, k_blk, v_blk = col_blk0, col_blk0 + 1, col_blk0 + 2
    hb = R // half
    n_hblk = L // half

    def main(blk):
        return pl.BlockSpec((None, R, DIL_WIDTH), lambda c, n: (c, n, blk))

    def prev(blk):
        return pl.BlockSpec((None, half, DIL_WIDTH), lambda c, n: (c, jnp.maximum(n * hb - 1, 0), blk))

    def nxt(blk):
        return pl.BlockSpec((None, half, DIL_WIDTH), lambda c, n: (c, jnp.minimum((n + 1) * hb, n_hblk - 1), blk))

    kern = functools.partial(_dilated_kernel, R=R, B=B, half=half, dilation=dilation, n_chunks=n_chunks)
    out_spec = pl.BlockSpec((None, R, DIL_WIDTH), lambda c, n: (c, n, 0))
    return pl.pallas_call(
        kern,
        grid=(dilation, n_chunks),
        in_specs=[pl.BlockSpec(memory_space=pltpu.SMEM),
                  main(q_blk), prev(k_blk), main(k_blk), nxt(k_blk), prev(v_blk), main(v_blk), nxt(v_blk)],
        out_specs=[out_spec, out_spec],
        out_shape=[jax.ShapeDtypeStruct((dilation, L, DIL_WIDTH), BF16),
                   jax.ShapeDtypeStruct((dilation, L, DIL_WIDTH), F32)],
        scratch_shapes=[pltpu.VMEM((N_DIL_HEADS, 3, B, B + 2 * half), F32),
                        pltpu.VMEM((R + 2 * half, DIL_WIDTH), BF16),
                        pltpu.VMEM((R + 2 * half, DIL_WIDTH), BF16)],
        compiler_params=pltpu.CompilerParams(
            dimension_semantics=("arbitrary", "arbitrary"), vmem_limit_bytes=VMEM_LIMIT),
        name=f"dilated_d{dilation}",
    )(tab, qkv, qkv, qkv, qkv, qkv, qkv, qkv)


def _combine_kernel(*refs, dilations):
    n = len(dilations)
    o_refs, l_refs = refs[:n], refs[n:2 * n]
    g_ref, out_ref = refs[2 * n], refs[2 * n + 1]
    scratch = refs[2 * n + 2:]
    tm = out_ref.shape[0]

    for hh in range(N_DIL_HEADS):
        sl = slice(hh * HEAD_DIM, (hh + 1) * HEAD_DIM)
        outs, lses = [], []
        si = 0
        for o_ref, l_ref, dil in zip(o_refs, l_refs, dilations):
            if dil == 1:
                outs.append(o_ref[0, :, sl].astype(F32))
                lses.append(l_ref[0, :, sl])
                continue
            os_ref, ls_ref = scratch[si], scratch[si + 1]
            si += 2
            for c in range(dil):
                os_ref[hh, pl.ds(c, tm // dil, stride=dil), :] = o_ref[c, :, sl].astype(F32)
                ls_ref[hh, pl.ds(c, tm // dil, stride=dil), :] = l_ref[c, :, sl]
            outs.append(os_ref[hh])
            lses.append(ls_ref[hh])

        m = functools.reduce(jnp.maximum, lses)
        ws = [jnp.exp2(l - m) for l in lses]
        tot = functools.reduce(lambda a, b: a + b, ws)
        oh = functools.reduce(lambda a, b: a + b, [(w / tot) * op for w, op in zip(ws, outs)])
        ms = jnp.mean(oh * oh, axis=-1, keepdims=True)
        out_ref[:, sl] = (oh * lax.rsqrt(ms + NORM_EPS) * g_ref[:, sl]).astype(out_ref.dtype)


def _combine(outs, lses, gain, dilations, *, tm=512):
    S = outs[0].shape[0] * outs[0].shape[1]
    specs = [pl.BlockSpec((d, tm // d, DIL_WIDTH), lambda i: (0, i, 0)) for d in dilations]
    n_scr = sum(1 for d in dilations if d != 1)
    return pl.pallas_call(
        functools.partial(_combine_kernel, dilations=dilations),
        grid=(S // tm,),
        in_specs=specs + specs + [pl.BlockSpec((1, DIL_WIDTH), lambda i: (0, 0))],
        scratch_shapes=[pltpu.VMEM((N_DIL_HEADS, tm, HEAD_DIM), F32)] * (2 * n_scr),
        out_specs=pl.BlockSpec((tm, DIL_WIDTH), lambda i: (i, 0)),
        out_shape=jax.ShapeDtypeStruct((S, DIL_WIDTH), BF16),
        compiler_params=pltpu.CompilerParams(
            dimension_semantics=("arbitrary",), vmem_limit_bytes=VMEM_LIMIT),
        name="dilated_combine",
    )(*outs, *lses, gain)


def _key_norm2_kernel(k_ref, o_ref):
    @pl.when(pl.program_id(0) == 0)
    def _():
        o_ref[...] = jnp.zeros(o_ref.shape, F32)

    k = k_ref[...].astype(F32)
    for hh in range(N_DIL_HEADS):
        kh = k[:, hh * HEAD_DIM:(hh + 1) * HEAD_DIM]
        n2 = jnp.max(jnp.sum(kh * kh, axis=1, keepdims=True), axis=0, keepdims=True)
        o_ref[hh:hh + 1, :] = jnp.maximum(o_ref[hh:hh + 1, :], jnp.broadcast_to(n2, (1, LANES)))


def _key_norm2(proj, k_blk, *, tm=1024):
    S = proj.shape[0]
    return pl.pallas_call(
        _key_norm2_kernel,
        grid=(S // tm,),
        in_specs=[pl.BlockSpec((tm, DIL_WIDTH), lambda i: (i, k_blk))],
        out_specs=pl.BlockSpec((N_DIL_HEADS, LANES), lambda i: (0, 0)),
        out_shape=jax.ShapeDtypeStruct((N_DIL_HEADS, LANES), F32),
        compiler_params=pltpu.CompilerParams(dimension_semantics=("arbitrary",), vmem_limit_bytes=VMEM_LIMIT),
        name="dilated_key_norm",
    )(proj)


def _dilated_fast_kernel(tab_ref, kn2_ref, q_ref, kp_ref, km_ref, kn_ref, vp_ref, vm_ref, vn_ref, num_ref, den_ref,
                         bias_ref, kx_ref, vx_ref, p_ref, *, R, B, half, dilation, n_chunks):
    c = pl.program_id(0)
    n = pl.program_id(1)
    W = B + 2 * half
    nblk = R // B
    VW = 2 * HEAD_DIM

    @pl.when(jnp.logical_and(c == 0, n == 0))
    def _():
        rows = 8
        col = lax.broadcasted_iota(jnp.int32, (rows, W), 1)
        row = lax.broadcasted_iota(jnp.int32, (rows, W), 0)
        for hh in range(N_DIL_HEADS):
            def fill(r, carry, hh=hh):
                r0 = pl.multiple_of(r * rows, rows)
                off = col - half - (row + r0)
                bias = _bias_from_rel(off * dilation, tab_ref, N_DIFF_HEADS + hh)
                base = jnp.where(jnp.abs(off) <= half, bias, NEG_INF * LOG2E)
                bias_ref[hh, 1, pl.ds(r0, rows), :] = base
                bias_ref[hh, 0, pl.ds(r0, rows), :] = jnp.where(col >= half, base, NEG_INF * LOG2E)
                bias_ref[hh, 2, pl.ds(r0, rows), :] = jnp.where(col < B + half, base, NEG_INF * LOG2E)
                return carry
            lax.fori_loop(0, B // rows, fill, 0)
        vx_ref[...] = jnp.ones(vx_ref.shape, BF16)

    kx_ref[0:half, :] = kp_ref[...]
    kx_ref[half:half + R, :] = km_ref[...]
    kx_ref[half + R:, :] = kn_ref[...]
    for hh in range(N_DIL_HEADS):
        src = slice(hh * HEAD_DIM, (hh + 1) * HEAD_DIM)
        dst = slice(hh * VW, hh * VW + HEAD_DIM)
        vx_ref[0:half, dst] = vp_ref[:, src]
        vx_ref[half:half + R, dst] = vm_ref[:, src]
        vx_ref[half + R:, dst] = vn_ref[:, src]

    n_slots = p_ref.shape[0]
    bias_max = []
    for hh in range(N_DIL_HEADS):
        bm = tab_ref[0, N_DIFF_HEADS + hh]
        for bk in range(1, N_REL_BUCKETS):
            bm = jnp.maximum(bm, tab_ref[bk, N_DIFF_HEADS + hh])
        bias_max.append(bm)

    def score_stage(hh, b, slot):
        cs = slice(hh * HEAD_DIM, (hh + 1) * HEAD_DIM)
        r0 = b * B
        var = 1
        if b == 0:
            var = jnp.where(n == 0, 0, var)
        if b == nblk - 1:
            var = jnp.where(n == n_chunks - 1, 2, var)
        q = q_ref[r0:r0 + B, cs]
        qf = q.astype(F32)
        qn2 = jnp.sum(qf * qf, axis=1, keepdims=True)
        shift = jnp.sqrt(qn2 * kn2_ref[hh:hh + 1, 0:1]) * SHIFT_MARGIN + bias_max[hh]
        s = lax.dot_general(q, kx_ref[r0:r0 + W, cs], (((1,), (1,)), ((), ())), preferred_element_type=F32)
        p_ref[slot] = jnp.exp2(s + bias_ref[hh, var] - shift).astype(BF16)

    def value_stage(hh, b, slot):
        cs = slice(hh * HEAD_DIM, (hh + 1) * HEAD_DIM)
        r0 = b * B
        nd = jnp.dot(p_ref[slot], vx_ref[r0:r0 + W, hh * VW:(hh + 1) * VW], preferred_element_type=F32)
        num_ref[r0:r0 + B, cs] = nd[:, :HEAD_DIM].astype(num_ref.dtype)
        den_ref[r0:r0 + B, cs] = nd[:, HEAD_DIM:].astype(den_ref.dtype)

    chains = [(hh, b) for hh in range(N_DIL_HEADS) for b in range(nblk)]
    for i, (hh, b) in enumerate(chains):
        if i > 0:
            value_stage(*chains[i - 1], (i - 1) % n_slots)
        score_stage(hh, b, i % n_slots)
    value_stage(*chains[-1], (len(chains) - 1) % n_slots)


def _dilated_fast(tab, kn2, qkv, col_blk0, window, dilation, *, B=128):
    _, L, _ = qkv.shape
    R = min(DIL_CHUNK, L)
    half = window // (2 * dilation)
    assert L % R == 0 and R % B == 0 and half % BF16_SUBLANES == 0 and R % half == 0
    n_chunks = L // R
    q_blk, k_blk, v_blk = col_blk0, col_blk0 + 1, col_blk0 + 2
    hb = R // half
    n_hblk = L // half

    def main(blk):
        return pl.BlockSpec((None, R, DIL_WIDTH), lambda c, n: (c, n, blk))

    def prev(blk):
        return pl.BlockSpec((None, half, DIL_WIDTH), lambda c, n: (c, jnp.maximum(n * hb - 1, 0), blk))

    def nxt(blk):
        return pl.BlockSpec((None, half, DIL_WIDTH), lambda c, n: (c, jnp.minimum((n + 1) * hb, n_hblk - 1), blk))

    kern = functools.partial(_dilated_fast_kernel, R=R, B=B, half=half, dilation=dilation, n_chunks=n_chunks)
    out_spec = pl.BlockSpec((None, R, DIL_WIDTH), lambda c, n: (c, n, 0))
    return pl.pallas_call(
        kern,
        grid=(dilation, n_chunks),
        in_specs=[pl.BlockSpec(memory_space=pltpu.SMEM),
                  pl.BlockSpec((N_DIL_HEADS, LANES), lambda c, n: (0, 0)),
                  main(q_blk), prev(k_blk), main(k_blk), nxt(k_blk), prev(v_blk), main(v_blk), nxt(v_blk)],
        out_specs=[out_spec, out_spec],
        out_shape=[jax.ShapeDtypeStruct((dilation, L, DIL_WIDTH), BF16),
                   jax.ShapeDtypeStruct((dilation, L, DIL_WIDTH), BF16)],
        scratch_shapes=[pltpu.VMEM((N_DIL_HEADS, 3, B, B + 2 * half), F32),
                        pltpu.VMEM((R + 2 * half, DIL_WIDTH), BF16),
                        pltpu.VMEM((R + 2 * half, 2 * DIL_WIDTH), BF16),
                        pltpu.VMEM((4, B, B + 2 * half), BF16)],
        compiler_params=pltpu.CompilerParams(
            dimension_semantics=("arbitrary", "arbitrary"), vmem_limit_bytes=VMEM_LIMIT),
        name=f"dilated_fast_d{dilation}",
    )(tab, kn2, qkv, qkv, qkv, qkv, qkv, qkv, qkv)


def _combine_fast_kernel(*refs, dilations):
    n = len(dilations)
    n_refs, d_refs = refs[:n], refs[n:2 * n]
    g_ref, out_ref, dmin_ref = refs[2 * n], refs[2 * n + 1], refs[2 * n + 2]
    scratch = refs[2 * n + 3:]
    tm = out_ref.shape[0]

    for hh in range(N_DIL_HEADS):
        sl = slice(hh * HEAD_DIM, (hh + 1) * HEAD_DIM)
        nums, dens = [], []
        si = 0
        for n_ref, d_ref, dil in zip(n_refs, d_refs, dilations):
            if dil == 1:
                nums.append(n_ref[0, :, sl].astype(F32))
                dens.append(d_ref[0, :, sl].astype(F32))
                continue
            ns_ref, ds_ref = scratch[si], scratch[si + 1]
            si += 2
            for c in range(dil):
                ns_ref[hh, pl.ds(c, tm // dil, stride=dil), :] = n_ref[c, :, sl].astype(F32)
                ds_ref[hh, pl.ds(c, tm // dil, stride=dil), :] = d_ref[c, :, sl].astype(F32)
            nums.append(ns_ref[hh])
            dens.append(ds_ref[hh])

        den = functools.reduce(lambda a, b: a + b, dens)
        oh = functools.reduce(lambda a, b: a + b, nums) / den
        ms = jnp.mean(oh * oh, axis=-1, keepdims=True)
        out_ref[:, sl] = (oh * lax.rsqrt(ms + NORM_EPS) * g_ref[:, sl]).astype(out_ref.dtype)
        dmin_ref[0, hh:hh + 1, :] = jnp.min(den, axis=0, keepdims=True)


def _combine_fast(nums, dens, gain, dilations, *, tm=512):
    S = nums[0].shape[0] * nums[0].shape[1]
    specs = [pl.BlockSpec((d, tm // d, DIL_WIDTH), lambda i: (0, i, 0)) for d in dilations]
    n_scr = sum(1 for d in dilations if d != 1)
    return pl.pallas_call(
        functools.partial(_combine_fast_kernel, dilations=dilations),
        grid=(S // tm,),
        in_specs=specs + specs + [pl.BlockSpec((1, DIL_WIDTH), lambda i: (0, 0))],
        scratch_shapes=[pltpu.VMEM((N_DIL_HEADS, tm, HEAD_DIM), F32)] * (2 * n_scr),
        out_specs=[pl.BlockSpec((tm, DIL_WIDTH), lambda i: (i, 0)),
                   pl.BlockSpec((1, N_DIL_HEADS, LANES), lambda i: (i, 0, 0))],
        out_shape=[jax.ShapeDtypeStruct((S, DIL_WIDTH), BF16),
                   jax.ShapeDtypeStruct((S // tm, N_DIL_HEADS, LANES), F32)],
        compiler_params=pltpu.CompilerParams(
            dimension_semantics=("arbitrary",), vmem_limit_bytes=VMEM_LIMIT),
        name="dilated_combine_fast",
    )(*nums, *dens, gain)


def _outproj_kernel(od_ref, ol_ref, wd_ref, wl_ref, x_ref, g_ref, x1_ref, h2_ref):
    acc = jnp.dot(od_ref[...], wd_ref[...], preferred_element_type=F32)
    acc = acc + jnp.dot(ol_ref[...], wl_ref[...], preferred_element_type=F32)
    x1 = x_ref[...] + acc
    x1_ref[...] = x1
    ms = jnp.mean(x1 * x1, axis=-1, keepdims=True)
    h2_ref[...] = (x1 * lax.rsqrt(ms + NORM_EPS) * g_ref[...]).astype(h2_ref.dtype)


def _out_proj(o_d, o_l, w_bf, x2, gain, *, tm=512):
    S, D = x2.shape
    return pl.pallas_call(
        _outproj_kernel,
        grid=(S // tm,),
        in_specs=[
            pl.BlockSpec((tm, DIFF_WIDTH), lambda i: (i, 0)),
            pl.BlockSpec((tm, DIL_WIDTH), lambda i: (i, 0)),
            pl.BlockSpec((DIFF_WIDTH, D), lambda i: (0, 0)),
            pl.BlockSpec((DIL_WIDTH, D), lambda i: (1, 0)),
            pl.BlockSpec((tm, D), lambda i: (i, 0)),
            pl.BlockSpec((1, D), lambda i: (0, 0)),
        ],
        out_specs=[pl.BlockSpec((tm, D), lambda i: (i, 0)), pl.BlockSpec((tm, D), lambda i: (i, 0))],
        out_shape=[jax.ShapeDtypeStruct((S, D), F32), jax.ShapeDtypeStruct((S, D), BF16)],
        compiler_params=pltpu.CompilerParams(
            dimension_semantics=("arbitrary",), vmem_limit_bytes=VMEM_LIMIT),
        name="out_proj",
    )(o_d, o_l, w_bf, w_bf, x2, gain)


def _ffn_up_kernel(hm_ref, hp_ref, hn_ref, wg_ref, wu_ref, cw_ref, cb_ref, o_ref, lhs_ref, *, tm, n_row_tiles):
    i = pl.program_id(0)
    j = pl.program_id(1)
    halo = BF16_SUBLANES

    @pl.when(j == 0)
    def _():
        lhs_ref[0:halo, :] = jnp.where(i == 0, jnp.zeros_like(hp_ref[...]), hp_ref[...])
        lhs_ref[halo:halo + tm, :] = hm_ref[...]
        lhs_ref[halo + tm:, :] = jnp.where(i == n_row_tiles - 1, jnp.zeros_like(hn_ref[...]), hn_ref[...])

    g = jnp.dot(lhs_ref[...], wg_ref[...], preferred_element_type=F32)
    u = jnp.dot(lhs_ref[halo:halo + tm, :], wu_ref[...], preferred_element_type=F32)
    rows = tm + 2 * halo
    g_prev = pltpu.roll(g, 1, axis=0)
    g_next = pltpu.roll(g, rows - 1, axis=0)
    y = cw_ref[0:1, :] * g_prev + cw_ref[1:2, :] * g + cw_ref[2:3, :] * g_next + cb_ref[...]
    y = y[halo:halo + tm, :]
    act = y * (1.0 / (1.0 + jnp.exp(-y))) * u
    o_ref[...] = act.astype(o_ref.dtype)


def _ffn_up(h2, w_bf, conv_w, conv_b, *, tm=1024, tn=512):
    S, D = h2.shape
    d_ff = conv_w.shape[1]
    assert d_ff % tn == 0
    nj = d_ff // tn
    ni = S // tm
    hb = tm // BF16_SUBLANES
    n_hblk = S // BF16_SUBLANES
    kern = functools.partial(_ffn_up_kernel, tm=tm, n_row_tiles=ni)
    return pl.pallas_call(
        kern,
        grid=(ni, nj),
        in_specs=[
            pl.BlockSpec((tm, D), lambda i, j: (i, 0)),
            pl.BlockSpec((BF16_SUBLANES, D), lambda i, j: (jnp.maximum(i * hb - 1, 0), 0)),
            pl.BlockSpec((BF16_SUBLANES, D), lambda i, j: (jnp.minimum((i + 1) * hb, n_hblk - 1), 0)),
            pl.BlockSpec((D, tn), lambda i, j: (0, j)),
            pl.BlockSpec((D, tn), lambda i, j: (0, nj + j)),
            pl.BlockSpec((3, tn), lambda i, j: (0, j)),
            pl.BlockSpec((1, tn), lambda i, j: (0, j)),
        ],
        out_specs=pl.BlockSpec((tm, tn), lambda i, j: (i, j)),
        out_shape=jax.ShapeDtypeStruct((S, d_ff), BF16),
        scratch_shapes=[pltpu.VMEM((tm + 2 * BF16_SUBLANES, D), BF16)],
        compiler_params=pltpu.CompilerParams(
            dimension_semantics=("arbitrary", "arbitrary"), vmem_limit_bytes=VMEM_LIMIT),
        name="ffn_up",
    )(h2, h2, h2, w_bf, w_bf, conv_w, conv_b)


def _ffn_down_kernel(a_ref, w_ref, x1_ref, g_ref, o_ref, *, n_k):
    k = pl.program_id(1)

    @pl.when(k == 0)
    def _():
        o_ref[...] = x1_ref[...]

    o_ref[...] += jnp.dot(a_ref[...], w_ref[...], preferred_element_type=F32)

    @pl.when(k == n_k - 1)
    def _():
        y = o_ref[...]
        ms = jnp.mean(y * y, axis=-1, keepdims=True)
        o_ref[...] = y * lax.rsqrt(ms + NORM_EPS) * g_ref[...]


def _ffn_down(act, w_bf, x1, gain, *, tm=1024, tk=1408):
    S, d_ff = act.shape
    D = x1.shape[1]
    n_k = d_ff // tk
    kern = functools.partial(_ffn_down_kernel, n_k=n_k)
    return pl.pallas_call(
        kern,
        grid=(S // tm, n_k),
        in_specs=[
            pl.BlockSpec((tm, tk), lambda i, k: (i, k)),
            pl.BlockSpec((tk, D), lambda i, k: (k, 0)),
            pl.BlockSpec((tm, D), lambda i, k: (i, 0)),
            pl.BlockSpec((1, D), lambda i, k: (0, 0)),
        ],
        out_specs=pl.BlockSpec((tm, D), lambda i, k: (i, 0)),
        out_shape=jax.ShapeDtypeStruct((S, D), F32),
        compiler_params=pltpu.CompilerParams(
            dimension_semantics=("arbitrary", "arbitrary"), vmem_limit_bytes=VMEM_LIMIT),
        name="ffn_down",
    )(act, w_bf, x1, gain)


def kernel(x, norm1_gain, w_in, rel_bias_table, lambda_q1, lambda_k1, lambda_q2, lambda_k2,
           diff_subln_gain, dil_out_gain, w_out, norm2_gain, w_gate_up, conv_w, conv_b, w_down, final_gain):
    B, S, D = x.shape
    assert B == 1 and w_in.shape[0] == 1
    x2 = x.reshape(S, D)
    n_cols = w_in.shape[2]

    qscale = LOG2E / math.sqrt(HEAD_DIM)
    col = np.arange(n_cols)
    dil_q0 = 2 * DIFF_QK_COLS + DIFF_WIDTH
    is_q = (col < DIFF_QK_COLS) | ((col >= dil_q0) & (col < dil_q0 + DIL_WIDTH))
    colscale = jnp.asarray(np.where(is_q, qscale, 1.0).astype(np.float32)).reshape(1, n_cols)
    tab = rel_bias_table.astype(F32) * LOG2E

    regroup = tuple(d for _, d in DILATED_PATTERNS if d != 1)
    proj, qt_all, vt_all, *cls = _in_proj(x2, norm1_gain.reshape(1, D), w_in[0].astype(BF16), colscale, regroup)
    cls_by_dil = dict(zip(regroup, cls))

    lam = _lambda(lambda_q1.reshape(1, -1), lambda_k1.reshape(1, -1),
                  lambda_q2.reshape(1, -1), lambda_k2.reshape(1, -1))
    o_d, w_out_bf, w_gate_up_bf, w_down_bf = _diff_attention(
        tab, proj, qt_all, vt_all, lam, diff_subln_gain.reshape(-1, 1), (w_out[0], w_gate_up[0], w_down[0]))

    dil_blk0 = (2 * DIFF_QK_COLS + DIFF_WIDTH) // DIL_WIDTH
    dilations = tuple(d for _, d in DILATED_PATTERNS)
    dil_gain = dil_out_gain.reshape(1, -1)

    def pattern_inputs(dilation):
        if dilation == 1:
            return proj.reshape(1, S, proj.shape[1]), dil_blk0
        return cls_by_dil[dilation], 0

    kn2 = _key_norm2(proj, dil_blk0 + 1)
    nums, dens = [], []
    for window, dilation in DILATED_PATTERNS:
        n_p, d_p = _dilated_fast(tab, kn2, *pattern_inputs(dilation), window, dilation)
        nums.append(n_p)
        dens.append(d_p)
    o_l_fast, den_min = _combine_fast(nums, dens, dil_gain, dilations)

    def exact_dilated():
        outs, lses = [], []
        for window, dilation in DILATED_PATTERNS:
            o_p, lse_p = _dilated_pattern(tab, *pattern_inputs(dilation), window, dilation)
            outs.append(o_p)
            lses.append(lse_p)
        return _combine(outs, lses, dil_gain, dilations)

    o_l = lax.cond(jnp.min(den_min) < MIN_DENOMINATOR, exact_dilated, lambda: o_l_fast)

    x1, h2 = _out_proj(o_d, o_l, w_out_bf, x2, norm2_gain.reshape(1, D))
    act = _ffn_up(h2, w_gate_up_bf, conv_w[0], conv_b.reshape(1, -1))
    out = _ffn_down(act, w_down_bf, x1, final_gain.reshape(1, D))
    return out.reshape(B, S, D)
```

```python
import functools
import math

import numpy as np
import jax
import jax.numpy as jnp
from jax import lax
from jax.experimental import pallas as pl
from jax.experimental.pallas import tpu as pltpu

F32 = jnp.float32
BF16 = jnp.bfloat16

HEAD_DIM = 128
N_DIFF_HEADS = 4
DIFF_V_DIM = 2 * HEAD_DIM
N_DIL_HEADS = 8
DIFF_QK_COLS = N_DIFF_HEADS * 2 * HEAD_DIM
DIFF_WIDTH = N_DIFF_HEADS * DIFF_V_DIM
DIL_WIDTH = N_DIL_HEADS * HEAD_DIM
DILATED_PATTERNS = ((128, 1), (512, 4), (2048, 16))
N_REL_BUCKETS = 32
REL_MAX_DISTANCE = 1024
NORM_EPS = 1e-6
SUBLN_EPS = 1e-5
NEG_INF = -1e30
LOG2E = math.log2(math.e)
LAM_INIT = 0.8 - 0.6 * math.exp(-0.3 * 0)

LANES = 128
BF16_SUBLANES = 16
DIFF_TILE = 1024
DIFF_QUERY_PANEL = 256
SHIFT_MARGIN = 1.0 + 2.0 ** -8
MIN_DENOMINATOR = 2.0 ** -60
DIL_CHUNK = 1024
DIL_HEAD_UNROLL = 4
VMEM_LIMIT = 56 * 1024 * 1024


def _bucket_breaks():
    nb = N_REL_BUCKETS // 2
    max_exact = nb // 2
    rel = np.arange(-2 * REL_MAX_DISTANCE, 2 * REL_MAX_DISTANCE + 1)
    n = np.abs(rel)
    pos = np.log(np.maximum(n, 1) / max_exact) / math.log(REL_MAX_DISTANCE / max_exact) * (nb - max_exact)
    large = np.minimum(max_exact + np.floor(pos).astype(np.int64), nb - 1)
    bucket = np.where(rel > 0, nb, 0) + np.where(n < max_exact, n, large)
    breaks = [(int(rel[i]), int(bucket[i])) for i in range(1, len(rel)) if bucket[i] != bucket[i - 1]]
    return int(bucket[0]), breaks


FIRST_BUCKET, BUCKET_BREAKS = _bucket_breaks()
LAST_BUCKET = BUCKET_BREAKS[-1][1]
FAR_DIST = max(-BUCKET_BREAKS[0][0] + 1, BUCKET_BREAKS[-1][0])


def _bias_from_rel(rel, tab_ref, col):
    val = jnp.full(rel.shape, tab_ref[FIRST_BUCKET, col], F32)
    for thr, b in BUCKET_BREAKS:
        val = jnp.where(rel >= thr, tab_ref[b, col], val)
    return val


N_COL_GROUPS = 6
FIRST_DIL_GROUP = 3


def _inproj_kernel(x_ref, g_ref, w_ref, cs_ref, o_ref, qt_ref, vt_ref, *rest, dilations):
    cls_refs = rest[:len(dilations)]
    h_ref, stage_ref = rest[len(dilations):]
    t = pl.program_id(1)
    tm = x_ref.shape[0]
    n_dil = N_COL_GROUPS - FIRST_DIL_GROUP

    @pl.when(t == 0)
    def _():
        x = x_ref[...]
        ms = jnp.mean(x * x, axis=-1, keepdims=True)
        h_ref[...] = (x * lax.rsqrt(ms + NORM_EPS) * g_ref[...]).astype(BF16)

    def matmul():
        acc = jnp.dot(h_ref[...], w_ref[...], preferred_element_type=F32) * cs_ref[...]
        o_ref[...] = acc.astype(o_ref.dtype)
        return acc

    def stage(acc, slot):
        for cb in range(stage_ref.shape[1]):
            stage_ref[slot, cb] = acc[:, cb * LANES:(cb + 1) * LANES]

    def regroup(slot):
        for cb in range(stage_ref.shape[1]):
            sl = slice(cb * LANES, (cb + 1) * LANES)
            for cls_ref, dil in zip(cls_refs, dilations):
                for c in range(dil):
                    cls_ref[c, :, sl] = stage_ref[slot, cb, pl.ds(c, tm // dil, stride=dil), :].astype(cls_ref.dtype)

    for step in range(N_COL_GROUPS):
        @pl.when(t == step)
        def _(step=step):
            acc = matmul()
            if 1 <= step <= n_dil:
                regroup((step - 1) % 2)
            if step < n_dil:
                stage(acc, step % 2)
            if step == n_dil:
                qt_ref[...] = acc.T.astype(qt_ref.dtype)
            if step == n_dil + 2:
                vt_ref[...] = acc.T.astype(vt_ref.dtype)


def _in_proj(x2, gain, w_bf, colscale, dilations, *, tm=512):
    S, D = x2.shape
    N = w_bf.shape[1]
    tn = DIFF_QK_COLS
    assert DIFF_WIDTH == tn and DIL_WIDTH == tn and N == N_COL_GROUPS * tn
    n_dil = N_COL_GROUPS - FIRST_DIL_GROUP
    kern = functools.partial(_inproj_kernel, dilations=dilations)

    def group(t):
        return (t + FIRST_DIL_GROUP) % N_COL_GROUPS

    cls_specs = [pl.BlockSpec((d, tm // d, tn), lambda i, t: (0, i, jnp.clip(t - 1, 0, n_dil - 1)))
                 for d in dilations]
    cls_shapes = [jax.ShapeDtypeStruct((d, S // d, n_dil * tn), BF16) for d in dilations]
    return pl.pallas_call(
        kern,
        grid=(S // tm, N_COL_GROUPS),
        in_specs=[
            pl.BlockSpec((tm, D), lambda i, t: (i, 0)),
            pl.BlockSpec((1, D), lambda i, t: (0, 0)),
            pl.BlockSpec((D, tn), lambda i, t: (0, group(t))),
            pl.BlockSpec((1, tn), lambda i, t: (0, group(t))),
        ],
        out_specs=[
            pl.BlockSpec((tm, tn), lambda i, t: (i, group(t))),
            pl.BlockSpec((tn, tm), lambda i, t: (0, i)),
            pl.BlockSpec((tn, tm), lambda i, t: (0, i)),
        ] + cls_specs,
        out_shape=[
            jax.ShapeDtypeStruct((S, N), BF16),
            jax.ShapeDtypeStruct((tn, S), BF16),
            jax.ShapeDtypeStruct((tn, S), BF16),
        ] + cls_shapes,
        scratch_shapes=[pltpu.VMEM((tm, D), BF16), pltpu.VMEM((2, tn // LANES, tm, LANES), F32)],
        compiler_params=pltpu.CompilerParams(
            dimension_semantics=("arbitrary", "arbitrary"), vmem_limit_bytes=VMEM_LIMIT),
        name="in_proj",
    )(x2, gain, w_bf, colscale)


def _diff_attn_kernel(*refs, T, QP, n_near, n_tiles, n_cast):
    tab_ref, q1t_ref, q2t_ref, k1_ref, k2_ref, vt_ref, lam_ref, gain_ref = refs[:8]
    cast_in, o_ref, cast_out = refs[8:8 + n_cast], refs[8 + n_cast], refs[9 + n_cast:9 + 2 * n_cast]
    bias_ref, m_ref, l_ref, acc_ref, p_ref, shift_ref, knorm_ref, prev_ref = refs[9 + 2 * n_cast:]
    _diff_attn_body(tab_ref, q1t_ref, q2t_ref, k1_ref, k2_ref, vt_ref, lam_ref, gain_ref, o_ref,
                    bias_ref, m_ref, l_ref, acc_ref, p_ref, shift_ref, knorm_ref, prev_ref,
                    T=T, QP=QP, n_near=n_near, n_tiles=n_tiles)
    for src, dst in zip(cast_in, cast_out):
        dst[...] = src[...].astype(dst.dtype)


def _diff_attn_body(tab_ref, q1t_ref, q2t_ref, k1_ref, k2_ref, vt_ref, lam_ref, gain_ref, o_ref,
                    bias_ref, m_ref, l_ref, acc_ref, p_ref, shift_ref, knorm_ref, prev_ref,
                    *, T, QP, n_near, n_tiles):
    h = pl.program_id(0)
    qi = pl.program_id(1)
    n_chains = 2 * (T // QP)
    qts = (q1t_ref, q2t_ref)
    ks = (k1_ref, k2_ref)

    def chain_of(i):
        qp = i // 2
        return i % 2, slice(qp * QP, (qp + 1) * QP)

    @pl.when(qi == 0)
    def _():
        for mi in range(2):
            def knorm(t, best, mi=mi):
                k = ks[mi][pl.ds(pl.multiple_of(t * T, T), T), :].astype(F32)
                return jnp.maximum(best, jnp.max(jnp.sum(k * k, axis=1, keepdims=True), axis=0, keepdims=True))
            knorm_ref[mi] = lax.fori_loop(0, n_tiles, knorm, jnp.zeros((1, 1), F32))

        x = lax.broadcasted_iota(jnp.int32, (8, 2 * T), 1)
        x = jnp.where(x < T, x, x - 2 * T)
        for di, d in enumerate(range(-n_near, n_near + 1)):
            g = _bias_from_rel(d * T - x, tab_ref, h)
            base = jnp.broadcast_to(g[0:1, :], (LANES, 2 * T))
            for rb in range(T // LANES):
                blk = pltpu.roll(base, rb * LANES, 1, stride=1, stride_axis=0)
                bias_ref[di, rb * LANES:(rb + 1) * LANES, :] = blk[:, :T]

    c_left = tab_ref[FIRST_BUCKET, h]
    c_right = tab_ref[LAST_BUCKET, h]

    bias_max = tab_ref[0, h]
    for b in range(1, N_REL_BUCKETS):
        bias_max = jnp.maximum(bias_max, tab_ref[b, h])

    for mi in range(2):
        q = qts[mi][...].astype(F32)
        qnorm2 = jnp.sum(q * q, axis=0, keepdims=True)
        shift_ref[mi] = jnp.sqrt(qnorm2 * knorm_ref[mi]) * SHIFT_MARGIN + bias_max
    l_ref[...] = jnp.zeros(l_ref.shape, F32)
    acc_ref[...] = jnp.zeros(acc_ref.shape, F32)
    p_ref[n_chains - 1] = jnp.zeros((T, QP), BF16)
    prev_ref[0] = 0

    def score_stage(i, k0, bias_di, bias_const):
        mi, qs = chain_of(i)
        s = jnp.dot(ks[mi][pl.ds(k0, T), :], qts[mi][:, qs], preferred_element_type=F32)
        if bias_di is not None:
            p = jnp.exp2(s + bias_ref[bias_di, :, qs] - shift_ref[mi, :, qs])
        else:
            p = jnp.exp2(s - (shift_ref[mi, :, qs] - bias_const))
        l_ref[mi, :, qs] += jnp.sum(p, axis=0, keepdims=True)
        p_ref[i] = p.astype(BF16)

    def value_stage(i, k0):
        mi, qs = chain_of(i)
        acc_ref[mi, :, qs] += jnp.dot(vt_ref[:, pl.ds(k0, T)], p_ref[i], preferred_element_type=F32)

    def tile(kt, bias_di, bias_const):
        k0 = pl.multiple_of(kt * T, T)
        pk0 = pl.multiple_of(prev_ref[0] * T, T)
        for i in range(n_chains):
            value_stage((i - 1) % n_chains, pk0 if i < 1 else k0)
            score_stage(i, k0, bias_di, bias_const)
        prev_ref[0] = kt

    lo = jnp.maximum(qi - n_near, 0)
    hi = jnp.minimum(qi + n_near + 1, n_tiles)

    @pl.loop(0, lo)
    def _(kt):
        tile(kt, None, c_left)

    for di, d in enumerate(range(-n_near, n_near + 1)):
        kt = qi + d

        @pl.when(jnp.logical_and(kt >= 0, kt < n_tiles))
        def _(di=di, kt=kt):
            tile(kt, di, None)

    @pl.loop(hi, n_tiles)
    def _(kt):
        tile(kt, None, c_right)

    value_stage(n_chains - 1, pl.multiple_of(prev_ref[0] * T, T))

    @pl.when(jnp.min(l_ref[...]) < MIN_DENOMINATOR)
    def _():
        m_ref[...] = jnp.full(m_ref.shape, -jnp.inf, F32)
        l_ref[...] = jnp.zeros(l_ref.shape, F32)
        acc_ref[...] = jnp.zeros(acc_ref.shape, F32)

        @pl.loop(0, n_tiles)
        def _(kt):
            k0 = pl.multiple_of(kt * T, T)
            d = kt - qi
            near = jnp.abs(d) <= n_near
            di = jnp.clip(d + n_near, 0, 2 * n_near)
            c_far = jnp.where(d < 0, c_left, c_right)
            for i in range(n_chains):
                mi, qs = chain_of(i)
                s = jnp.dot(ks[mi][pl.ds(k0, T), :], qts[mi][:, qs], preferred_element_type=F32)
                s = s + jnp.where(near, bias_ref[di, :, qs], c_far)
                m = m_ref[mi, :, qs]
                m_new = jnp.maximum(m, jnp.max(s, axis=0, keepdims=True))
                alpha = jnp.exp2(m - m_new)
                p = jnp.exp2(s - m_new)
                m_ref[mi, :, qs] = m_new
                l_ref[mi, :, qs] = alpha * l_ref[mi, :, qs] + jnp.sum(p, axis=0, keepdims=True)
                pv = jnp.dot(vt_ref[:, pl.ds(k0, T)], p.astype(BF16), preferred_element_type=F32)
                acc_ref[mi, :, qs] = alpha * acc_ref[mi, :, qs] + pv

    lam = lam_ref[0, 0]
    o = acc_ref[0] / l_ref[0] - lam * (acc_ref[1] / l_ref[1])
    ms = jnp.mean(o * o, axis=0, keepdims=True)
    o = o * lax.rsqrt(ms + SUBLN_EPS) * (gain_ref[...] * (1.0 - LAM_INIT))
    o_ref[...] = o.T.astype(o_ref.dtype)


def _lambda_kernel(q1_ref, k1_ref, q2_ref, k2_ref, o_ref):
    a = jnp.sum(q1_ref[...] * k1_ref[...], axis=-1, keepdims=True)
    b = jnp.sum(q2_ref[...] * k2_ref[...], axis=-1, keepdims=True)
    o_ref[...] = jnp.exp(a) - jnp.exp(b) + LAM_INIT


def _lambda(lq1, lk1, lq2, lk2):
    return pl.pallas_call(
        _lambda_kernel, out_shape=jax.ShapeDtypeStruct((1, 1), F32), name="diff_lambda",
    )(lq1, lk1, lq2, lk2)


def _diff_attention(tab_diff, proj, qt_all, vt_all, lam, gain_col, f32_weights):
    S = proj.shape[0]
    T = DIFF_TILE
    n_tiles = S // T
    n_near = -(-(FAR_DIST - 1) // T)
    kern = functools.partial(_diff_attn_kernel, T=T, QP=DIFF_QUERY_PANEL, n_near=n_near, n_tiles=n_tiles,
                             n_cast=len(f32_weights))

    n_steps = N_DIFF_HEADS * n_tiles
    cast_specs = []
    for w in f32_weights:
        rows, cols = w.shape
        every = next(e for e in (1, 2, 4, 8) if rows % (n_steps // e) == 0
                     and (rows // (n_steps // e)) % BF16_SUBLANES == 0)
        cast_specs.append(pl.BlockSpec((rows // (n_steps // every), cols),
                                       lambda h, i, every=every: ((h * n_tiles + i) // every, 0)))
    k_block0 = DIFF_QK_COLS // HEAD_DIM
    smem = pl.BlockSpec(memory_space=pltpu.SMEM)
    return pl.pallas_call(
        kern,
        grid=(N_DIFF_HEADS, n_tiles),
        in_specs=[
            smem,
            pl.BlockSpec((HEAD_DIM, T), lambda h, i: (2 * h, i)),
            pl.BlockSpec((HEAD_DIM, T), lambda h, i: (2 * h + 1, i)),
            pl.BlockSpec((S, HEAD_DIM), lambda h, i: (0, k_block0 + 2 * h), pipeline_mode=pl.Buffered(1)),
            pl.BlockSpec((S, HEAD_DIM), lambda h, i: (0, k_block0 + 2 * h + 1), pipeline_mode=pl.Buffered(1)),
            pl.BlockSpec((DIFF_V_DIM, S), lambda h, i: (h, 0), pipeline_mode=pl.Buffered(1)),
            smem,
            pl.BlockSpec((DIFF_V_DIM, 1), lambda h, i: (0, 0)),
        ] + cast_specs,
        out_specs=[pl.BlockSpec((T, DIFF_V_DIM), lambda h, i: (i, h))] + cast_specs,
        out_shape=[jax.ShapeDtypeStruct((S, DIFF_WIDTH), BF16)]
        + [jax.ShapeDtypeStruct(w.shape, BF16) for w in f32_weights],
        scratch_shapes=[pltpu.VMEM((2 * n_near + 1, T, T), F32),
                        pltpu.VMEM((2, 1, T), F32), pltpu.VMEM((2, 1, T), F32),
                        pltpu.VMEM((2, DIFF_V_DIM, T), F32),
                        pltpu.VMEM((2 * (T // DIFF_QUERY_PANEL), T, DIFF_QUERY_PANEL), BF16),
                        pltpu.VMEM((2, 1, T), F32), pltpu.VMEM((2, 1, 1), F32),
                        pltpu.SMEM((1,), jnp.int32)],
        compiler_params=pltpu.CompilerParams(
            dimension_semantics=("arbitrary", "arbitrary"), vmem_limit_bytes=VMEM_LIMIT),
        name="diff_attn",
    )(tab_diff, qt_all, qt_all, proj, proj, vt_all, lam, gain_col, *f32_weights)


def _dilated_kernel(tab_ref, q_ref, kp_ref, km_ref, kn_ref, vp_ref, vm_ref, vn_ref, o_ref, lse_ref,
                    bias_ref, kx_ref, vx_ref, *, R, B, half, dilation, n_chunks):
    c = pl.program_id(0)
    n = pl.program_id(1)
    W = B + 2 * half
    nblk = R // B

    @pl.when(jnp.logical_and(c == 0, n == 0))
    def _():
        rows = 8
        col = lax.broadcasted_iota(jnp.int32, (rows, W), 1)
        row = lax.broadcasted_iota(jnp.int32, (rows, W), 0)
        for hh in range(N_DIL_HEADS):
            def fill(r, carry, hh=hh):
                r0 = pl.multiple_of(r * rows, rows)
                off = col - half - (row + r0)
                bias = _bias_from_rel(off * dilation, tab_ref, N_DIFF_HEADS + hh)
                base = jnp.where(jnp.abs(off) <= half, bias, NEG_INF * LOG2E)
                bias_ref[hh, 1, pl.ds(r0, rows), :] = base
                bias_ref[hh, 0, pl.ds(r0, rows), :] = jnp.where(col >= half, base, NEG_INF * LOG2E)
                bias_ref[hh, 2, pl.ds(r0, rows), :] = jnp.where(col < B + half, base, NEG_INF * LOG2E)
                return carry
            lax.fori_loop(0, B // rows, fill, 0)

    kx_ref[0:half, :] = kp_ref[...]
    kx_ref[half:half + R, :] = km_ref[...]
    kx_ref[half + R:, :] = kn_ref[...]
    vx_ref[0:half, :] = vp_ref[...]
    vx_ref[half:half + R, :] = vm_ref[...]
    vx_ref[half + R:, :] = vn_ref[...]

    def chain(hh, b):
        c0 = pl.multiple_of(hh * HEAD_DIM, HEAD_DIM)
        r0 = b * B
        var = 1
        if b == 0:
            var = jnp.where(n == 0, 0, var)
        if b == nblk - 1:
            var = jnp.where(n == n_chunks - 1, 2, var)
        q = q_ref[pl.ds(r0, B), pl.ds(c0, HEAD_DIM)]
        k = kx_ref[pl.ds(r0, W), pl.ds(c0, HEAD_DIM)]
        v = vx_ref[pl.ds(r0, W), pl.ds(c0, HEAD_DIM)]
        s = lax.dot_general(q, k, (((1,), (1,)), ((), ())), preferred_element_type=F32)
        s = s + bias_ref[hh, var]
        m = jnp.max(s, axis=-1, keepdims=True)
        e = jnp.exp2(s - m)
        den = jnp.sum(e, axis=-1, keepdims=True)
        o = jnp.dot(e.astype(BF16), v, preferred_element_type=F32) / den
        o_ref[pl.ds(r0, B), pl.ds(c0, HEAD_DIM)] = o.astype(o_ref.dtype)
        lse = m + jnp.log2(den)
        lse_ref[pl.ds(r0, B), pl.ds(c0, HEAD_DIM)] = jnp.broadcast_to(lse, (B, HEAD_DIM))

    @pl.loop(0, N_DIL_HEADS // DIL_HEAD_UNROLL)
    def _(hg):
        for u in range(DIL_HEAD_UNROLL):
            for b in range(nblk):
                chain(hg * DIL_HEAD_UNROLL + u, b)


def _dilated_pattern(tab, qkv, col_blk0, window, dilation, *, B=256):
    _, L, _ = qkv.shape
    R = min(DIL_CHUNK, L)
    half = window // (2 * dilation)
    assert L % R == 0 and R % B == 0 and half % BF16_SUBLANES == 0 and R % half == 0
    n_chunks = L // R
    q_blk, k_blk, v_blk = col_blk0, col_blk0 + 1, col_blk0 + 2
    hb = R // half
    n_hblk = L // half

    def main(blk):
        return pl.BlockSpec((None, R, DIL_WIDTH), lambda c, n: (c, n, blk))

    def prev(blk):
        return pl.BlockSpec((None, half, DIL_WIDTH), lambda c, n: (c, jnp.maximum(n * hb - 1, 0), blk))

    def nxt(blk):
        return pl.BlockSpec((None, half, DIL_WIDTH), lambda c, n: (c, jnp.minimum((n + 1) * hb, n_hblk - 1), blk))

    kern = functools.partial(_dilated_kernel, R=R, B=B, half=half, dilation=dilation, n_chunks=n_chunks)
    out_spec = pl.BlockSpec((None, R, DIL_WIDTH), lambda c, n: (c, n, 0))
    return pl.pallas_call(
        kern,
        grid=(dilation, n_chunks),
        in_specs=[pl.BlockSpec(memory_space=pltpu.SMEM),
                  main(q_blk), prev(k_blk), main(k_blk), nxt(k_blk), prev(v_blk), main(v_blk), nxt(v_blk)],
        out_specs=[out_spec, out_spec],
        out_shape=[jax.ShapeDtypeStruct((dilation, L, DIL_WIDTH), BF16),
                   jax.ShapeDtypeStruct((dilation, L, DIL_WIDTH), F32)],
        scratch_shapes=[pltpu.VMEM((N_DIL_HEADS, 3, B, B + 2 * half), F32),
                        pltpu.VMEM((R + 2 * half, DIL_WIDTH), BF16),
                        pltpu.VMEM((R + 2 * half, DIL_WIDTH), BF16)],
        compiler_params=pltpu.CompilerParams(
            dimension_semantics=("arbitrary", "arbitrary"), vmem_limit_bytes=VMEM_LIMIT),
        name=f"dilated_d{dilation}",
    )(tab, qkv, qkv, qkv, qkv, qkv, qkv, qkv)


def _combine_kernel(*refs, dilations):
    n = len(dilations)
    o_refs, l_refs = refs[:n], refs[n:2 * n]
    g_ref, out_ref = refs[2 * n], refs[2 * n + 1]
    scratch = refs[2 * n + 2:]
    tm = out_ref.shape[0]

    for hh in range(N_DIL_HEADS):
        sl = slice(hh * HEAD_DIM, (hh + 1) * HEAD_DIM)
        outs, lses = [], []
        si = 0
        for o_ref, l_ref, dil in zip(o_refs, l_refs, dilations):
            if dil == 1:
                outs.append(o_ref[0, :, sl].astype(F32))
                lses.append(l_ref[0, :, sl])
                continue
            os_ref, ls_ref = scratch[si], scratch[si + 1]
            si += 2
            for c in range(dil):
                os_ref[hh, pl.ds(c, tm // dil, stride=dil), :] = o_ref[c, :, sl].astype(F32)
                ls_ref[hh, pl.ds(c, tm // dil, stride=dil), :] = l_ref[c, :, sl]
            outs.append(os_ref[hh])
            lses.append(ls_ref[hh])

        m = functools.reduce(jnp.maximum, lses)
        ws = [jnp.exp2(l - m) for l in lses]
        tot = functools.reduce(lambda a, b: a + b, ws)
        oh = functools.reduce(lambda a, b: a + b, [(w / tot) * op for w, op in zip(ws, outs)])
        ms = jnp.mean(oh * oh, axis=-1, keepdims=True)
        out_ref[:, sl] = (oh * lax.rsqrt(ms + NORM_EPS) * g_ref[:, sl]).astype(out_ref.dtype)


def _combine(outs, lses, gain, dilations, *, tm=512):
    S = outs[0].shape[0] * outs[0].shape[1]
    specs = [pl.BlockSpec((d, tm // d, DIL_WIDTH), lambda i: (0, i, 0)) for d in dilations]
    n_scr = sum(1 for d in dilations if d != 1)
    return pl.pallas_call(
        functools.partial(_combine_kernel, dilations=dilations),
        grid=(S // tm,),
        in_specs=specs + specs + [pl.BlockSpec((1, DIL_WIDTH), lambda i: (0, 0))],
        scratch_shapes=[pltpu.VMEM((N_DIL_HEADS, tm, HEAD_DIM), F32)] * (2 * n_scr),
        out_specs=pl.BlockSpec((tm, DIL_WIDTH), lambda i: (i, 0)),
        out_shape=jax.ShapeDtypeStruct((S, DIL_WIDTH), BF16),
        compiler_params=pltpu.CompilerParams(
            dimension_semantics=("arbitrary",), vmem_limit_bytes=VMEM_LIMIT),
        name="dilated_combine",
    )(*outs, *lses, gain)


def _key_norm2_kernel(k_ref, o_ref):
    @pl.when(pl.program_id(0) == 0)
    def _():
        o_ref[...] = jnp.zeros(o_ref.shape, F32)

    k = k_ref[...].astype(F32)
    for hh in range(N_DIL_HEADS):
        kh = k[:, hh * HEAD_DIM:(hh + 1) * HEAD_DIM]
        n2 = jnp.max(jnp.sum(kh * kh, axis=1, keepdims=True), axis=0, keepdims=True)
        o_ref[hh:hh + 1, :] = jnp.maximum(o_ref[hh:hh + 1, :], jnp.broadcast_to(n2, (1, LANES)))


def _key_norm2(proj, k_blk, *, tm=1024):
    S = proj.shape[0]
    return pl.pallas_call(
        _key_norm2_kernel,
        grid=(S // tm,),
        in_specs=[pl.BlockSpec((tm, DIL_WIDTH), lambda i: (i, k_blk))],
        out_specs=pl.BlockSpec((N_DIL_HEADS, LANES), lambda i: (0, 0)),
        out_shape=jax.ShapeDtypeStruct((N_DIL_HEADS, LANES), F32),
        compiler_params=pltpu.CompilerParams(dimension_semantics=("arbitrary",), vmem_limit_bytes=VMEM_LIMIT),
        name="dilated_key_norm",
    )(proj)


def _dilated_fast_kernel(tab_ref, kn2_ref, q_ref, kp_ref, km_ref, kn_ref, vp_ref, vm_ref, vn_ref, num_ref, den_ref,
                         bias_ref, kx_ref, vx_ref, p_ref, *, R, B, half, dilation, n_chunks):
    c = pl.program_id(0)
    n = pl.program_id(1)
    W = B + 2 * half
    nblk = R // B
    VW = 2 * HEAD_DIM

    @pl.when(jnp.logical_and(c == 0, n == 0))
    def _():
        rows = 8
        col = lax.broadcasted_iota(jnp.int32, (rows, W), 1)
        row = lax.broadcasted_iota(jnp.int32, (rows, W), 0)
        for hh in range(N_DIL_HEADS):
            def fill(r, carry, hh=hh):
                r0 = pl.multiple_of(r * rows, rows)
                off = col - half - (row + r0)
                bias = _bias_from_rel(off * dilation, tab_ref, N_DIFF_HEADS + hh)
                base = jnp.where(jnp.abs(off) <= half, bias, NEG_INF * LOG2E)
                bias_ref[hh, 1, pl.ds(r0, rows), :] = base
                bias_ref[hh, 0, pl.ds(r0, rows), :] = jnp.where(col >= half, base, NEG_INF * LOG2E)
                bias_ref[hh, 2, pl.ds(r0, rows), :] = jnp.where(col < B + half, base, NEG_INF * LOG2E)
                return carry
            lax.fori_loop(0, B // rows, fill, 0)
        vx_ref[...] = jnp.ones(vx_ref.shape, BF16)

    kx_ref[0:half, :] = kp_ref[...]
    kx_ref[half:half + R, :] = km_ref[...]
    kx_ref[half + R:, :] = kn_ref[...]
    for hh in range(N_DIL_HEADS):
        src = slice(hh * HEAD_DIM, (hh + 1) * HEAD_DIM)
        dst = slice(hh * VW, hh * VW + HEAD_DIM)
        vx_ref[0:half, dst] = vp_ref[:, src]
        vx_ref[half:half + R, dst] = vm_ref[:, src]
        vx_ref[half + R:, dst] = vn_ref[:, src]

    den_ref[...] = jnp.zeros(den_ref.shape, F32)
    n_slots = p_ref.shape[0]
    bias_max = []
    for hh in range(N_DIL_HEADS):
        bm = tab_ref[0, N_DIFF_HEADS + hh]
        for bk in range(1, N_REL_BUCKETS):
            bm = jnp.maximum(bm, tab_ref[bk, N_DIFF_HEADS + hh])
        bias_max.append(bm)

    def score_stage(hh, b, slot):
        cs = slice(hh * HEAD_DIM, (hh + 1) * HEAD_DIM)
        r0 = b * B
        var = 1
        if b == 0:
            var = jnp.where(n == 0, 0, var)
        if b == nblk - 1:
            var = jnp.where(n == n_chunks - 1, 2, var)
        q = q_ref[r0:r0 + B, cs]
        qf = q.astype(F32)
        qn2 = jnp.sum(qf * qf, axis=1, keepdims=True)
        shift = jnp.sqrt(qn2 * kn2_ref[hh:hh + 1, 0:1]) * SHIFT_MARGIN + bias_max[hh]
        s = lax.dot_general(q, kx_ref[r0:r0 + W, cs], (((1,), (1,)), ((), ())), preferred_element_type=F32)
        p_ref[slot] = jnp.exp2(s + bias_ref[hh, var] - shift).astype(BF16)

    def value_stage(hh, b, slot):
        cs = slice(hh * HEAD_DIM, (hh + 1) * HEAD_DIM)
        r0 = b * B
        nd = jnp.dot(p_ref[slot], vx_ref[r0:r0 + W, hh * VW:(hh + 1) * VW], preferred_element_type=F32)
        num_ref[r0:r0 + B, cs] = nd[:, :HEAD_DIM].astype(num_ref.dtype)
        den_ref[r0:r0 + B, hh:hh + 1] = nd[:, HEAD_DIM + hh:HEAD_DIM + hh + 1]

    chains = [(hh, b) for hh in range(N_DIL_HEADS) for b in range(nblk)]
    for i, (hh, b) in enumerate(chains):
        if i > 0:
            value_stage(*chains[i - 1], (i - 1) % n_slots)
        score_stage(hh, b, i % n_slots)
    value_stage(*chains[-1], (len(chains) - 1) % n_slots)


def _dilated_fast(tab, kn2, qkv, col_blk0, window, dilation, *, B=128):
    _, L, _ = qkv.shape
    R = min(DIL_CHUNK, L)
    half = window // (2 * dilation)
    assert L % R == 0 and R % B == 0 and half % BF16_SUBLANES == 0 and R % half == 0
    n_chunks = L // R
    q_blk, k_blk, v_blk = col_blk0, col_blk0 + 1, col_blk0 + 2
    hb = R // half
    n_hblk = L // half

    def main(blk):
        return pl.BlockSpec((None, R, DIL_WIDTH), lambda c, n: (c, n, blk))

    def prev(blk):
        return pl.BlockSpec((None, half, DIL_WIDTH), lambda c, n: (c, jnp.maximum(n * hb - 1, 0), blk))

    def nxt(blk):
        return pl.BlockSpec((None, half, DIL_WIDTH), lambda c, n: (c, jnp.minimum((n + 1) * hb, n_hblk - 1), blk))

    kern = functools.partial(_dilated_fast_kernel, R=R, B=B, half=half, dilation=dilation, n_chunks=n_chunks)
    out_spec = pl.BlockSpec((None, R, DIL_WIDTH), lambda c, n: (c, n, 0))
    return pl.pallas_call(
        kern,
        grid=(dilation, n_chunks),
        in_specs=[pl.BlockSpec(memory_space=pltpu.SMEM),
                  pl.BlockSpec((N_DIL_HEADS, LANES), lambda c, n: (0, 0)),
                  main(q_blk), prev(k_blk), main(k_blk), nxt(k_blk), prev(v_blk), main(v_blk), nxt(v_blk)],
        out_specs=[out_spec, pl.BlockSpec((None, R, LANES), lambda c, n: (c, n, 0))],
        out_shape=[jax.ShapeDtypeStruct((dilation, L, DIL_WIDTH), BF16),
                   jax.ShapeDtypeStruct((dilation, L, LANES), F32)],
        scratch_shapes=[pltpu.VMEM((N_DIL_HEADS, 3, B, B + 2 * half), F32),
                        pltpu.VMEM((R + 2 * half, DIL_WIDTH), BF16),
                        pltpu.VMEM((R + 2 * half, 2 * DIL_WIDTH), BF16),
                        pltpu.VMEM((4, B, B + 2 * half), BF16)],
        compiler_params=pltpu.CompilerParams(
            dimension_semantics=("arbitrary", "arbitrary"), vmem_limit_bytes=VMEM_LIMIT),
        name=f"dilated_fast_d{dilation}",
    )(tab, kn2, qkv, qkv, qkv, qkv, qkv, qkv, qkv)


def _combine_fast_kernel(*refs, dilations):
    n = len(dilations)
    n_refs, d_refs = refs[:n], refs[n:2 * n]
    g_ref, out_ref, dmin_ref = refs[2 * n], refs[2 * n + 1], refs[2 * n + 2]
    scratch = refs[2 * n + 3:]
    tm = out_ref.shape[0]

    den_all = None
    si = 0
    for d_ref, dil in zip(d_refs, dilations):
        if dil == 1:
            den_p = d_ref[0]
        else:
            ds_ref = scratch[si + 1]
            for c in range(dil):
                ds_ref[pl.ds(c, tm // dil, stride=dil), :] = d_ref[c]
            den_p = ds_ref[...]
            si += 2
        den_all = den_p if den_all is None else den_all + den_p

    for hh in range(N_DIL_HEADS):
        sl = slice(hh * HEAD_DIM, (hh + 1) * HEAD_DIM)
        nums = []
        si = 0
        for n_ref, dil in zip(n_refs, dilations):
            if dil == 1:
                nums.append(n_ref[0, :, sl].astype(F32))
                continue
            ns_ref = scratch[si]
            si += 2
            for c in range(dil):
                ns_ref[hh, pl.ds(c, tm // dil, stride=dil), :] = n_ref[c, :, sl].astype(F32)
            nums.append(ns_ref[hh])

        den = den_all[:, hh:hh + 1]
        oh = functools.reduce(lambda a, b: a + b, nums) / den
        ms = jnp.mean(oh * oh, axis=-1, keepdims=True)
        out_ref[:, sl] = (oh * lax.rsqrt(ms + NORM_EPS) * g_ref[:, sl]).astype(out_ref.dtype)
        dmin_ref[0, hh:hh + 1, :] = jnp.broadcast_to(jnp.min(den, axis=0, keepdims=True), (1, LANES))


def _combine_fast(nums, dens, gain, dilations, *, tm=512):
    S = nums[0].shape[0] * nums[0].shape[1]
    num_specs = [pl.BlockSpec((d, tm // d, DIL_WIDTH), lambda i: (0, i, 0)) for d in dilations]
    den_specs = [pl.BlockSpec((d, tm // d, LANES), lambda i: (0, i, 0)) for d in dilations]
    n_scr = sum(1 for d in dilations if d != 1)
    return pl.pallas_call(
        functools.partial(_combine_fast_kernel, dilations=dilations),
        grid=(S // tm,),
        in_specs=num_specs + den_specs + [pl.BlockSpec((1, DIL_WIDTH), lambda i: (0, 0))],
        scratch_shapes=[pltpu.VMEM((N_DIL_HEADS, tm, HEAD_DIM), F32), pltpu.VMEM((tm, LANES), F32)] * n_scr,
        out_specs=[pl.BlockSpec((tm, DIL_WIDTH), lambda i: (i, 0)),
                   pl.BlockSpec((1, N_DIL_HEADS, LANES), lambda i: (i, 0, 0))],
        out_shape=[jax.ShapeDtypeStruct((S, DIL_WIDTH), BF16),
                   jax.ShapeDtypeStruct((S // tm, N_DIL_HEADS, LANES), F32)],
        compiler_params=pltpu.CompilerParams(
            dimension_semantics=("arbitrary",), vmem_limit_bytes=VMEM_LIMIT),
        name="dilated_combine_fast",
    )(*nums, *dens, gain)


def _outproj_kernel(od_ref, ol_ref, wd_ref, wl_ref, x_ref, g_ref, x1_ref, h2_ref):
    acc = jnp.dot(od_ref[...], wd_ref[...], preferred_element_type=F32)
    acc = acc + jnp.dot(ol_ref[...], wl_ref[...], preferred_element_type=F32)
    x1 = x_ref[...] + acc
    x1_ref[...] = x1
    ms = jnp.mean(x1 * x1, axis=-1, keepdims=True)
    h2_ref[...] = (x1 * lax.rsqrt(ms + NORM_EPS) * g_ref[...]).astype(h2_ref.dtype)


def _out_proj(o_d, o_l, w_bf, x2, gain, *, tm=512):
    S, D = x2.shape
    return pl.pallas_call(
        _outproj_kernel,
        grid=(S // tm,),
        in_specs=[
            pl.BlockSpec((tm, DIFF_WIDTH), lambda i: (i, 0)),
            pl.BlockSpec((tm, DIL_WIDTH), lambda i: (i, 0)),
            pl.BlockSpec((DIFF_WIDTH, D), lambda i: (0, 0)),
            pl.BlockSpec((DIL_WIDTH, D), lambda i: (1, 0)),
            pl.BlockSpec((tm, D), lambda i: (i, 0)),
            pl.BlockSpec((1, D), lambda i: (0, 0)),
        ],
        out_specs=[pl.BlockSpec((tm, D), lambda i: (i, 0)), pl.BlockSpec((tm, D), lambda i: (i, 0))],
        out_shape=[jax.ShapeDtypeStruct((S, D), F32), jax.ShapeDtypeStruct((S, D), BF16)],
        compiler_params=pltpu.CompilerParams(
            dimension_semantics=("arbitrary",), vmem_limit_bytes=VMEM_LIMIT),
        name="out_proj",
    )(o_d, o_l, w_bf, w_bf, x2, gain)


def _ffn_up_kernel(hm_ref, hp_ref, hn_ref, wg_ref, wu_ref, cw_ref, cb_ref, o_ref, lhs_ref, *, tm, n_row_tiles):
    i = pl.program_id(0)
    j = pl.program_id(1)
    halo = BF16_SUBLANES

    @pl.when(j == 0)
    def _():
        lhs_ref[0:halo, :] = jnp.where(i == 0, jnp.zeros_like(hp_ref[...]), hp_ref[...])
        lhs_ref[halo:halo + tm, :] = hm_ref[...]
        lhs_ref[halo + tm:, :] = jnp.where(i == n_row_tiles - 1, jnp.zeros_like(hn_ref[...]), hn_ref[...])

    g = jnp.dot(lhs_ref[...], wg_ref[...], preferred_element_type=F32)
    u = jnp.dot(lhs_ref[halo:halo + tm, :], wu_ref[...], preferred_element_type=F32)
    rows = tm + 2 * halo
    g_prev = pltpu.roll(g, 1, axis=0)
    g_next = pltpu.roll(g, rows - 1, axis=0)
    y = cw_ref[0:1, :] * g_prev + cw_ref[1:2, :] * g + cw_ref[2:3, :] * g_next + cb_ref[...]
    y = y[halo:halo + tm, :]
    act = y * (1.0 / (1.0 + jnp.exp(-y))) * u
    o_ref[...] = act.astype(o_ref.dtype)


def _ffn_up(h2, w_bf, conv_w, conv_b, *, tm=1024, tn=512):
    S, D = h2.shape
    d_ff = conv_w.shape[1]
    assert d_ff % tn == 0
    nj = d_ff // tn
    ni = S // tm
    hb = tm // BF16_SUBLANES
    n_hblk = S // BF16_SUBLANES
    kern = functools.partial(_ffn_up_kernel, tm=tm, n_row_tiles=ni)
    return pl.pallas_call(
        kern,
        grid=(ni, nj),
        in_specs=[
            pl.BlockSpec((tm, D), lambda i, j: (i, 0)),
            pl.BlockSpec((BF16_SUBLANES, D), lambda i, j: (jnp.maximum(i * hb - 1, 0), 0)),
            pl.BlockSpec((BF16_SUBLANES, D), lambda i, j: (jnp.minimum((i + 1) * hb, n_hblk - 1), 0)),
            pl.BlockSpec((D, tn), lambda i, j: (0, j)),
            pl.BlockSpec((D, tn), lambda i, j: (0, nj + j)),
            pl.BlockSpec((3, tn), lambda i, j: (0, j)),
            pl.BlockSpec((1, tn), lambda i, j: (0, j)),
        ],
        out_specs=pl.BlockSpec((tm, tn), lambda i, j: (i, j)),
        out_shape=jax.ShapeDtypeStruct((S, d_ff), BF16),
        scratch_shapes=[pltpu.VMEM((tm + 2 * BF16_SUBLANES, D), BF16)],
        compiler_params=pltpu.CompilerParams(
            dimension_semantics=("arbitrary", "arbitrary"), vmem_limit_bytes=VMEM_LIMIT),
        name="ffn_up",
    )(h2, h2, h2, w_bf, w_bf, conv_w, conv_b)


def _ffn_down_kernel(a_ref, w_ref, x1_ref, g_ref, o_ref, *, n_k):
    k = pl.program_id(1)

    @pl.when(k == 0)
    def _():
        o_ref[...] = x1_ref[...]

    o_ref[...] += jnp.dot(a_ref[...], w_ref[...], preferred_element_type=F32)

    @pl.when(k == n_k - 1)
    def _():
        y = o_ref[...]
        ms = jnp.mean(y * y, axis=-1, keepdims=True)
        o_ref[...] = y * lax.rsqrt(ms + NORM_EPS) * g_ref[...]


def _ffn_down(act, w_bf, x1, gain, *, tm=1024, tk=1408):
    S, d_ff = act.shape
    D = x1.shape[1]
    n_k = d_ff // tk
    kern = functools.partial(_ffn_down_kernel, n_k=n_k)
    return pl.pallas_call(
        kern,
        grid=(S // tm, n_k),
        in_specs=[
            pl.BlockSpec((tm, tk), lambda i, k: (i, k)),
            pl.BlockSpec((tk, D), lambda i, k: (k, 0)),
            pl.BlockSpec((tm, D), lambda i, k: (i, 0)),
            pl.BlockSpec((1, D), lambda i, k: (0, 0)),
        ],
        out_specs=pl.BlockSpec((tm, D), lambda i, k: (i, 0)),
        out_shape=jax.ShapeDtypeStruct((S, D), F32),
        compiler_params=pltpu.CompilerParams(
            dimension_semantics=("arbitrary", "arbitrary"), vmem_limit_bytes=VMEM_LIMIT),
        name="ffn_down",
    )(act, w_bf, x1, gain)


def kernel(x, norm1_gain, w_in, rel_bias_table, lambda_q1, lambda_k1, lambda_q2, lambda_k2,
           diff_subln_gain, dil_out_gain, w_out, norm2_gain, w_gate_up, conv_w, conv_b, w_down, final_gain):
    B, S, D = x.shape
    assert B == 1 and w_in.shape[0] == 1
    x2 = x.reshape(S, D)
    n_cols = w_in.shape[2]

    qscale = LOG2E / math.sqrt(HEAD_DIM)
    col = np.arange(n_cols)
    dil_q0 = 2 * DIFF_QK_COLS + DIFF_WIDTH
    is_q = (col < DIFF_QK_COLS) | ((col >= dil_q0) & (col < dil_q0 + DIL_WIDTH))
    colscale = jnp.asarray(np.where(is_q, qscale, 1.0).astype(np.float32)).reshape(1, n_cols)
    tab = rel_bias_table.astype(F32) * LOG2E

    regroup = tuple(d for _, d in DILATED_PATTERNS if d != 1)
    proj, qt_all, vt_all, *cls = _in_proj(x2, norm1_gain.reshape(1, D), w_in[0].astype(BF16), colscale, regroup)
    cls_by_dil = dict(zip(regroup, cls))

    lam = _lambda(lambda_q1.reshape(1, -1), lambda_k1.reshape(1, -1),
                  lambda_q2.reshape(1, -1), lambda_k2.reshape(1, -1))
    o_d, w_out_bf, w_gate_up_bf, w_down_bf = _diff_attention(
        tab, proj, qt_all, vt_all, lam, diff_subln_gain.reshape(-1, 1), (w_out[0], w_gate_up[0], w_down[0]))

    dil_blk0 = (2 * DIFF_QK_COLS + DIFF_WIDTH) // DIL_WIDTH
    dilations = tuple(d for _, d in DILATED_PATTERNS)
    dil_gain = dil_out_gain.reshape(1, -1)

    def pattern_inputs(dilation):
        if dilation == 1:
            return proj.reshape(1, S, proj.shape[1]), dil_blk0
        return cls_by_dil[dilation], 0

    kn2 = _key_norm2(proj, dil_blk0 + 1)
    nums, dens = [], []
    for window, dilation in DILATED_PATTERNS:
        n_p, d_p = _dilated_fast(tab, kn2, *pattern_inputs(dilation), window, dilation)
        nums.append(n_p)
        dens.append(d_p)
    o_l_fast, den_min = _combine_fast(nums, dens, dil_gain, dilations)

    def exact_dilated():
        outs, lses = [], []
        for window, dilation in DILATED_PATTERNS:
            o_p, lse_p = _dilated_pattern(tab, *pattern_inputs(dilation), window, dilation)
            outs.append(o_p)
            lses.append(lse_p)
        return _combine(outs, lses, dil_gain, dilations)

    o_l = lax.cond(jnp.min(den_min) < MIN_DENOMINATOR, lambda fast: exact_dilated(), lambda fast: fast, o_l_fast)

    x1, h2 = _out_proj(o_d, o_l, w_out_bf, x2, norm2_gain.reshape(1, D))
    act = _ffn_up(h2, w_gate_up_bf, conv_w[0], conv_b.reshape(1, -1))
    out = _ffn_down(act, w_down_bf, x1, final_gain.reshape(1, D))
    return out.reshape(B, S, D)
```

```python
import functools
import math

import numpy as np
import jax
import jax.numpy as jnp
from jax import lax
from jax.experimental import pallas as pl
from jax.experimental.pallas import tpu as pltpu

F32 = jnp.float32
BF16 = jnp.bfloat16

HEAD_DIM = 128
N_DIFF_HEADS = 4
DIFF_V_DIM = 2 * HEAD_DIM
N_DIL_HEADS = 8
DIFF_QK_COLS = N_DIFF_HEADS * 2 * HEAD_DIM
DIFF_WIDTH = N_DIFF_HEADS * DIFF_V_DIM
DIL_WIDTH = N_DIL_HEADS * HEAD_DIM
DILATED_PATTERNS = ((128, 1), (512, 4), (2048, 16))
N_REL_BUCKETS = 32
REL_MAX_DISTANCE = 1024
NORM_EPS = 1e-6
SUBLN_EPS = 1e-5
NEG_INF = -1e30
LOG2E = math.log2(math.e)
LAM_INIT = 0.8 - 0.6 * math.exp(-0.3 * 0)

LANES = 128
BF16_SUBLANES = 16
DIFF_TILE = 1024
DIFF_QUERY_PANEL = 256
SHIFT_MARGIN = 1.0 + 2.0 ** -8
MIN_DENOMINATOR = 2.0 ** -60
DIL_CHUNK = 1024
DIL_HEAD_UNROLL = 4
VMEM_LIMIT = 56 * 1024 * 1024


def _bucket_breaks():
    nb = N_REL_BUCKETS // 2
    max_exact = nb // 2
    rel = np.arange(-2 * REL_MAX_DISTANCE, 2 * REL_MAX_DISTANCE + 1)
    n = np.abs(rel)
    pos = np.log(np.maximum(n, 1) / max_exact) / math.log(REL_MAX_DISTANCE / max_exact) * (nb - max_exact)
    large = np.minimum(max_exact + np.floor(pos).astype(np.int64), nb - 1)
    bucket = np.where(rel > 0, nb, 0) + np.where(n < max_exact, n, large)
    breaks = [(int(rel[i]), int(bucket[i])) for i in range(1, len(rel)) if bucket[i] != bucket[i - 1]]
    return int(bucket[0]), breaks


FIRST_BUCKET, BUCKET_BREAKS = _bucket_breaks()
LAST_BUCKET = BUCKET_BREAKS[-1][1]
FAR_DIST = max(-BUCKET_BREAKS[0][0] + 1, BUCKET_BREAKS[-1][0])


def _bias_from_rel(rel, tab_ref, col):
    val = jnp.full(rel.shape, tab_ref[FIRST_BUCKET, col], F32)
    for thr, b in BUCKET_BREAKS:
        val = jnp.where(rel >= thr, tab_ref[b, col], val)
    return val


N_COL_GROUPS = 6
FIRST_DIL_GROUP = 3


def _inproj_kernel(x_ref, g_ref, w_ref, cs_ref, o_ref, qt_ref, vt_ref, *rest, dilations):
    cls_refs = rest[:len(dilations)]
    h_ref, stage_ref = rest[len(dilations):]
    t = pl.program_id(1)
    tm = x_ref.shape[0]
    n_dil = N_COL_GROUPS - FIRST_DIL_GROUP

    @pl.when(t == 0)
    def _():
        x = x_ref[...]
        ms = jnp.mean(x * x, axis=-1, keepdims=True)
        h_ref[...] = (x * lax.rsqrt(ms + NORM_EPS) * g_ref[...]).astype(BF16)

    def matmul():
        acc = jnp.dot(h_ref[...], w_ref[...], preferred_element_type=F32) * cs_ref[...]
        o_ref[...] = acc.astype(o_ref.dtype)
        return acc

    def stage(acc, slot):
        for cb in range(stage_ref.shape[1]):
            stage_ref[slot, cb] = acc[:, cb * LANES:(cb + 1) * LANES]

    def regroup(slot):
        for cb in range(stage_ref.shape[1]):
            sl = slice(cb * LANES, (cb + 1) * LANES)
            for cls_ref, dil in zip(cls_refs, dilations):
                for c in range(dil):
                    cls_ref[c, :, sl] = stage_ref[slot, cb, pl.ds(c, tm // dil, stride=dil), :].astype(cls_ref.dtype)

    for step in range(N_COL_GROUPS):
        @pl.when(t == step)
        def _(step=step):
            acc = matmul()
            if 1 <= step <= n_dil:
                regroup((step - 1) % 2)
            if step < n_dil:
                stage(acc, step % 2)
            if step == n_dil:
                qt_ref[...] = acc.T.astype(qt_ref.dtype)
            if step == n_dil + 2:
                vt_ref[...] = acc.T.astype(vt_ref.dtype)


def _in_proj(x2, gain, w_bf, colscale, dilations, *, tm=512):
    S, D = x2.shape
    N = w_bf.shape[1]
    tn = DIFF_QK_COLS
    assert DIFF_WIDTH == tn and DIL_WIDTH == tn and N == N_COL_GROUPS * tn
    n_dil = N_COL_GROUPS - FIRST_DIL_GROUP
    kern = functools.partial(_inproj_kernel, dilations=dilations)

    def group(t):
        return (t + FIRST_DIL_GROUP) % N_COL_GROUPS

    cls_specs = [pl.BlockSpec((d, tm // d, tn), lambda i, t: (0, i, jnp.clip(t - 1, 0, n_dil - 1)))
                 for d in dilations]
    cls_shapes = [jax.ShapeDtypeStruct((d, S // d, n_dil * tn), BF16) for d in dilations]
    return pl.pallas_call(
        kern,
        grid=(S // tm, N_COL_GROUPS),
        in_specs=[
            pl.BlockSpec((tm, D), lambda i, t: (i, 0)),
            pl.BlockSpec((1, D), lambda i, t: (0, 0)),
            pl.BlockSpec((D, tn), lambda i, t: (0, group(t))),
            pl.BlockSpec((1, tn), lambda i, t: (0, group(t))),
        ],
        out_specs=[
            pl.BlockSpec((tm, tn), lambda i, t: (i, group(t))),
            pl.BlockSpec((tn, tm), lambda i, t: (0, i)),
            pl.BlockSpec((tn, tm), lambda i, t: (0, i)),
        ] + cls_specs,
        out_shape=[
            jax.ShapeDtypeStruct((S, N), BF16),
            jax.ShapeDtypeStruct((tn, S), BF16),
            jax.ShapeDtypeStruct((tn, S), BF16),
        ] + cls_shapes,
        scratch_shapes=[pltpu.VMEM((tm, D), BF16), pltpu.VMEM((2, tn // LANES, tm, LANES), F32)],
        compiler_params=pltpu.CompilerParams(
            dimension_semantics=("arbitrary", "arbitrary"), vmem_limit_bytes=VMEM_LIMIT),
        name="in_proj",
    )(x2, gain, w_bf, colscale)


def _diff_attn_kernel(*refs, T, QP, n_near, n_tiles, n_cast):
    tab_ref, q1t_ref, q2t_ref, k1_ref, k2_ref, vt_ref, lam_ref, gain_ref = refs[:8]
    cast_in, o_ref, cast_out = refs[8:8 + n_cast], refs[8 + n_cast], refs[9 + n_cast:9 + 2 * n_cast]
    bias_ref, m_ref, l_ref, acc_ref, p_ref, shift_ref, knorm_ref, prev_ref = refs[9 + 2 * n_cast:]
    _diff_attn_body(tab_ref, q1t_ref, q2t_ref, k1_ref, k2_ref, vt_ref, lam_ref, gain_ref, o_ref,
                    bias_ref, m_ref, l_ref, acc_ref, p_ref, shift_ref, knorm_ref, prev_ref,
                    T=T, QP=QP, n_near=n_near, n_tiles=n_tiles)
    for src, dst in zip(cast_in, cast_out):
        dst[...] = src[...].astype(dst.dtype)


def _diff_attn_body(tab_ref, q1t_ref, q2t_ref, k1_ref, k2_ref, vt_ref, lam_ref, gain_ref, o_ref,
                    bias_ref, m_ref, l_ref, acc_ref, p_ref, shift_ref, knorm_ref, prev_ref,
                    *, T, QP, n_near, n_tiles):
    h = pl.program_id(0)
    qi = pl.program_id(1)
    n_chains = 2 * (T // QP)
    qts = (q1t_ref, q2t_ref)
    ks = (k1_ref, k2_ref)

    def chain_of(i):
        qp = i // 2
        return i % 2, slice(qp * QP, (qp + 1) * QP)

    @pl.when(qi == 0)
    def _():
        for mi in range(2):
            def knorm(t, best, mi=mi):
                k = ks[mi][pl.ds(pl.multiple_of(t * T, T), T), :].astype(F32)
                return jnp.maximum(best, jnp.max(jnp.sum(k * k, axis=1, keepdims=True), axis=0, keepdims=True))
            knorm_ref[mi] = lax.fori_loop(0, n_tiles, knorm, jnp.zeros((1, 1), F32))

        x = lax.broadcasted_iota(jnp.int32, (8, 2 * T), 1)
        x = jnp.where(x < T, x, x - 2 * T)
        for di, d in enumerate(range(-n_near, n_near + 1)):
            g = _bias_from_rel(d * T - x, tab_ref, h)
            base = jnp.broadcast_to(g[0:1, :], (LANES, 2 * T))
            for rb in range(T // LANES):
                blk = pltpu.roll(base, rb * LANES, 1, stride=1, stride_axis=0)
                bias_ref[di, rb * LANES:(rb + 1) * LANES, :] = blk[:, :T]

    c_left = tab_ref[FIRST_BUCKET, h]
    c_right = tab_ref[LAST_BUCKET, h]

    bias_max = tab_ref[0, h]
    for b in range(1, N_REL_BUCKETS):
        bias_max = jnp.maximum(bias_max, tab_ref[b, h])

    for mi in range(2):
        q = qts[mi][...].astype(F32)
        qnorm2 = jnp.sum(q * q, axis=0, keepdims=True)
        shift_ref[mi] = jnp.sqrt(qnorm2 * knorm_ref[mi]) * SHIFT_MARGIN + bias_max
    l_ref[...] = jnp.zeros(l_ref.shape, F32)
    acc_ref[...] = jnp.zeros(acc_ref.shape, F32)
    p_ref[n_chains - 1] = jnp.zeros((T, QP), BF16)
    prev_ref[0] = 0

    def score_stage(i, k0, bias_di, bias_const):
        mi, qs = chain_of(i)
        s = jnp.dot(ks[mi][pl.ds(k0, T), :], qts[mi][:, qs], preferred_element_type=F32)
        if bias_di is not None:
            p = jnp.exp2(s + bias_ref[bias_di, :, qs] - shift_ref[mi, :, qs])
        else:
            p = jnp.exp2(s - (shift_ref[mi, :, qs] - bias_const))
        l_ref[mi, :, qs] += jnp.sum(p, axis=0, keepdims=True)
        p_ref[i] = p.astype(BF16)

    def value_stage(i, k0):
        mi, qs = chain_of(i)
        acc_ref[mi, :, qs] += jnp.dot(vt_ref[:, pl.ds(k0, T)], p_ref[i], preferred_element_type=F32)

    def tile(kt, bias_di, bias_const, prev_kt=None):
        k0 = pl.multiple_of(kt * T, T)
        pk0 = pl.multiple_of((prev_ref[0] if prev_kt is None else prev_kt) * T, T)
        for i in range(n_chains):
            value_stage((i - 1) % n_chains, pk0 if i < 1 else k0)
            score_stage(i, k0, bias_di, bias_const)
        prev_ref[0] = kt

    def far_tiles(start, stop, bias_const):
        count = stop - start
        odd = jnp.bitwise_and(count, 1)

        @pl.when(odd == 1)
        def _():
            tile(start, None, bias_const)

        @pl.loop(0, lax.shift_right_logical(count, 1))
        def _(pair):
            kt = start + odd + 2 * pair
            tile(kt, None, bias_const)
            tile(kt + 1, None, bias_const, prev_kt=kt)

    lo = jnp.maximum(qi - n_near, 0)
    hi = jnp.minimum(qi + n_near + 1, n_tiles)

    far_tiles(0, lo, c_left)

    for di, d in enumerate(range(-n_near, n_near + 1)):
        kt = qi + d

        @pl.when(jnp.logical_and(kt >= 0, kt < n_tiles))
        def _(di=di, kt=kt):
            tile(kt, di, None)

    far_tiles(hi, n_tiles, c_right)

    value_stage(n_chains - 1, pl.multiple_of(prev_ref[0] * T, T))

    @pl.when(jnp.min(l_ref[...]) < MIN_DENOMINATOR)
    def _():
        m_ref[...] = jnp.full(m_ref.shape, -jnp.inf, F32)
        l_ref[...] = jnp.zeros(l_ref.shape, F32)
        acc_ref[...] = jnp.zeros(acc_ref.shape, F32)

        @pl.loop(0, n_tiles)
        def _(kt):
            k0 = pl.multiple_of(kt * T, T)
            d = kt - qi
            near = jnp.abs(d) <= n_near
            di = jnp.clip(d + n_near, 0, 2 * n_near)
            c_far = jnp.where(d < 0, c_left, c_right)
            for i in range(n_chains):
                mi, qs = chain_of(i)
                s = jnp.dot(ks[mi][pl.ds(k0, T), :], qts[mi][:, qs], preferred_element_type=F32)
                s = s + jnp.where(near, bias_ref[di, :, qs], c_far)
                m = m_ref[mi, :, qs]
                m_new = jnp.maximum(m, jnp.max(s, axis=0, keepdims=True))
                alpha = jnp.exp2(m - m_new)
                p = jnp.exp2(s - m_new)
                m_ref[mi, :, qs] = m_new
                l_ref[mi, :, qs] = alpha * l_ref[mi, :, qs] + jnp.sum(p, axis=0, keepdims=True)
                pv = jnp.dot(vt_ref[:, pl.ds(k0, T)], p.astype(BF16), preferred_element_type=F32)
                acc_ref[mi, :, qs] = alpha * acc_ref[mi, :, qs] + pv

    lam = lam_ref[0, 0]
    o = acc_ref[0] / l_ref[0] - lam * (acc_ref[1] / l_ref[1])
    ms = jnp.mean(o * o, axis=0, keepdims=True)
    o = o * lax.rsqrt(ms + SUBLN_EPS) * (gain_ref[...] * (1.0 - LAM_INIT))
    o_ref[...] = o.T.astype(o_ref.dtype)


def _lambda_kernel(q1_ref, k1_ref, q2_ref, k2_ref, o_ref):
    a = jnp.sum(q1_ref[...] * k1_ref[...], axis=-1, keepdims=True)
    b = jnp.sum(q2_ref[...] * k2_ref[...], axis=-1, keepdims=True)
    o_ref[...] = jnp.exp(a) - jnp.exp(b) + LAM_INIT


def _lambda(lq1, lk1, lq2, lk2):
    return pl.pallas_call(
        _lambda_kernel, out_shape=jax.ShapeDtypeStruct((1, 1), F32), name="diff_lambda",
    )(lq1, lk1, lq2, lk2)


def _diff_attention(tab_diff, proj, qt_all, vt_all, lam, gain_col, f32_weights):
    S = proj.shape[0]
    T = DIFF_TILE
    n_tiles = S // T
    n_near = -(-(FAR_DIST - 1) // T)
    kern = functools.partial(_diff_attn_kernel, T=T, QP=DIFF_QUERY_PANEL, n_near=n_near, n_tiles=n_tiles,
                             n_cast=len(f32_weights))

    n_steps = N_DIFF_HEADS * n_tiles
    cast_specs = []
    for w in f32_weights:
        rows, cols = w.shape
        every = next(e for e in (1, 2, 4, 8) if rows % (n_steps // e) == 0
                     and (rows // (n_steps // e)) % BF16_SUBLANES == 0)
        cast_specs.append(pl.BlockSpec((rows // (n_steps // every), cols),
                                       lambda h, i, every=every: ((h * n_tiles + i) // every, 0)))
    k_block0 = DIFF_QK_COLS // HEAD_DIM
    smem = pl.BlockSpec(memory_space=pltpu.SMEM)
    return pl.pallas_call(
        kern,
        grid=(N_DIFF_HEADS, n_tiles),
        in_specs=[
            smem,
            pl.BlockSpec((HEAD_DIM, T), lambda h, i: (2 * h, i)),
            pl.BlockSpec((HEAD_DIM, T), lambda h, i: (2 * h + 1, i)),
            pl.BlockSpec((S, HEAD_DIM), lambda h, i: (0, k_block0 + 2 * h), pipeline_mode=pl.Buffered(1)),
            pl.BlockSpec((S, HEAD_DIM), lambda h, i: (0, k_block0 + 2 * h + 1), pipeline_mode=pl.Buffered(1)),
            pl.BlockSpec((DIFF_V_DIM, S), lambda h, i: (h, 0), pipeline_mode=pl.Buffered(1)),
            smem,
            pl.BlockSpec((DIFF_V_DIM, 1), lambda h, i: (0, 0)),
        ] + cast_specs,
        out_specs=[pl.BlockSpec((T, DIFF_V_DIM), lambda h, i: (i, h))] + cast_specs,
        out_shape=[jax.ShapeDtypeStruct((S, DIFF_WIDTH), BF16)]
        + [jax.ShapeDtypeStruct(w.shape, BF16) for w in f32_weights],
        scratch_shapes=[pltpu.VMEM((2 * n_near + 1, T, T), F32),
                        pltpu.VMEM((2, 1, T), F32), pltpu.VMEM((2, 1, T), F32),
                        pltpu.VMEM((2, DIFF_V_DIM, T), F32),
                        pltpu.VMEM((2 * (T // DIFF_QUERY_PANEL), T, DIFF_QUERY_PANEL), BF16),
                        pltpu.VMEM((2, 1, T), F32), pltpu.VMEM((2, 1, 1), F32),
                        pltpu.SMEM((1,), jnp.int32)],
        compiler_params=pltpu.CompilerParams(
            dimension_semantics=("arbitrary", "arbitrary"), vmem_limit_bytes=VMEM_LIMIT),
        name="diff_attn",
    )(tab_diff, qt_all, qt_all, proj, proj, vt_all, lam, gain_col, *f32_weights)


def _dilated_kernel(tab_ref, q_ref, kp_ref, km_ref, kn_ref, vp_ref, vm_ref, vn_ref, o_ref, lse_ref,
                    bias_ref, kx_ref, vx_ref, *, R, B, half, dilation, n_chunks):
    c = pl.program_id(0)
    n = pl.program_id(1)
    W = B + 2 * half
    nblk = R // B

    @pl.when(jnp.logical_and(c == 0, n == 0))
    def _():
        rows = 8
        col = lax.broadcasted_iota(jnp.int32, (rows, W), 1)
        row = lax.broadcasted_iota(jnp.int32, (rows, W), 0)
        for hh in range(N_DIL_HEADS):
            def fill(r, carry, hh=hh):
                r0 = pl.multiple_of(r * rows, rows)
                off = col - half - (row + r0)
                bias = _bias_from_rel(off * dilation, tab_ref, N_DIFF_HEADS + hh)
                base = jnp.where(jnp.abs(off) <= half, bias, NEG_INF * LOG2E)
                bias_ref[hh, 1, pl.ds(r0, rows), :] = base
                bias_ref[hh, 0, pl.ds(r0, rows), :] = jnp.where(col >= half, base, NEG_INF * LOG2E)
                bias_ref[hh, 2, pl.ds(r0, rows), :] = jnp.where(col < B + half, base, NEG_INF * LOG2E)
                return carry
            lax.fori_loop(0, B // rows, fill, 0)

    kx_ref[0:half, :] = kp_ref[...]
    kx_ref[half:half + R, :] = km_ref[...]
    kx_ref[half + R:, :] = kn_ref[...]
    vx_ref[0:half, :] = vp_ref[...]
    vx_ref[half:half + R, :] = vm_ref[...]
    vx_ref[half + R:, :] = vn_ref[...]

    def chain(hh, b):
        c0 = pl.multiple_of(hh * HEAD_DIM, HEAD_DIM)
        r0 = b * B
        var = 1
        if b == 0:
            var = jnp.where(n == 0, 0, var)
        if b == nblk - 1:
            var = jnp.where(n == n_chunks - 1, 2, var)
        q = q_ref[pl.ds(r0, B), pl.ds(c0, HEAD_DIM)]
        k = kx_ref[pl.ds(r0, W), pl.ds(c0, HEAD_DIM)]
        v = vx_ref[pl.ds(r0, W), pl.ds(c0, HEAD_DIM)]
        s = lax.dot_general(q, k, (((1,), (1,)), ((), ())), preferred_element_type=F32)
        s = s + bias_ref[hh, var]
        m = jnp.max(s, axis=-1, keepdims=True)
        e = jnp.exp2(s - m)
        den = jnp.sum(e, axis=-1, keepdims=True)
        o = jnp.dot(e.astype(BF16), v, preferred_element_type=F32) / den
        o_ref[pl.ds(r0, B), pl.ds(c0, HEAD_DIM)] = o.astype(o_ref.dtype)
        lse = m + jnp.log2(den)
        lse_ref[pl.ds(r0, B), pl.ds(c0, HEAD_DIM)] = jnp.broadcast_to(lse, (B, HEAD_DIM))

    @pl.loop(0, N_DIL_HEADS // DIL_HEAD_UNROLL)
    def _(hg):
        for u in range(DIL_HEAD_UNROLL):
            for b in range(nblk):
                chain(hg * DIL_HEAD_UNROLL + u, b)


def _dilated_pattern(tab, qkv, col_blk0, window, dilation, *, B=256):
    _, L, _ = qkv.shape
    R = min(DIL_CHUNK, L)
    half = window // (2 * dilation)
    assert L % R == 0 and R % B == 0 and half % BF16_SUBLANES == 0 and R % half == 0
    n_chunks = L // R
    q_blk, k_blk, v_blk = col_blk0, col_blk0 + 1, col_blk0 + 2
    hb = R // half
    n_hblk = L // half

    def main(blk):
        return pl.BlockSpec((None, R, DIL_WIDTH), lambda c, n: (c, n, blk))

    def prev(blk):
        return pl.BlockSpec((None, half, DIL_WIDTH), lambda c, n: (c, jnp.maximum(n * hb - 1, 0), blk))

    def nxt(blk):
        return pl.BlockSpec((None, half, DIL_WIDTH), lambda c, n: (c, jnp.minimum((n + 1) * hb, n_hblk - 1), blk))

    kern = functools.partial(_dilated_kernel, R=R, B=B, half=half, dilation=dilation, n_chunks=n_chunks)
    out_spec = pl.BlockSpec((None, R, DIL_WIDTH), lambda c, n: (c, n, 0))
    return pl.pallas_call(
        kern,
        grid=(dilation, n_chunks),
        in_specs=[pl.BlockSpec(memory_space=pltpu.SMEM),
                  main(q_blk), prev(k_blk), main(k_blk), nxt(k_blk), prev(v_blk), main(v_blk), nxt(v_blk)],
        out_specs=[out_spec, out_spec],
        out_shape=[jax.ShapeDtypeStruct((dilation, L, DIL_WIDTH), BF16),
                   jax.ShapeDtypeStruct((dilation, L, DIL_WIDTH), F32)],
        scratch_shapes=[pltpu.VMEM((N_DIL_HEADS, 3, B, B + 2 * half), F32),
                        pltpu.VMEM((R + 2 * half, DIL_WIDTH), BF16),
                        pltpu.VMEM((R + 2 * half, DIL_WIDTH), BF16)],
        compiler_params=pltpu.CompilerParams(
            dimension_semantics=("arbitrary", "arbitrary"), vmem_limit_bytes=VMEM_LIMIT),
        name=f"dilated_d{dilation}",
    )(tab, qkv, qkv, qkv, qkv, qkv, qkv, qkv)


def _combine_kernel(*refs, dilations):
    n = len(dilations)
    o_refs, l_refs = refs[:n], refs[n:2 * n]
    g_ref, out_ref = refs[2 * n], refs[2 * n + 1]
    scratch = refs[2 * n + 2:]
    tm = out_ref.shape[0]

    for hh in range(N_DIL_HEADS):
        sl = slice(hh * HEAD_DIM, (hh + 1) * HEAD_DIM)
        outs, lses = [], []
        si = 0
        for o_ref, l_ref, dil in zip(o_refs, l_refs, dilations):
            if dil == 1:
                outs.append(o_ref[0, :, sl].astype(F32))
                lses.append(l_ref[0, :, sl])
                continue
            os_ref, ls_ref = scratch[si], scratch[si + 1]
            si += 2
            for c in range(dil):
                os_ref[hh, pl.ds(c, tm // dil, stride=dil), :] = o_ref[c, :, sl].astype(F32)
                ls_ref[hh, pl.ds(c, tm // dil, stride=dil), :] = l_ref[c, :, sl]
            outs.append(os_ref[hh])
            lses.append(ls_ref[hh])

        m = functools.reduce(jnp.maximum, lses)
        ws = [jnp.exp2(l - m) for l in lses]
        tot = functools.reduce(lambda a, b: a + b, ws)
        oh = functools.reduce(lambda a, b: a + b, [(w / tot) * op for w, op in zip(ws, outs)])
        ms = jnp.mean(oh * oh, axis=-1, keepdims=True)
        out_ref[:, sl] = (oh * lax.rsqrt(ms + NORM_EPS) * g_ref[:, sl]).astype(out_ref.dtype)


def _combine(outs, lses, gain, dilations, *, tm=512):
    S = outs[0].shape[0] * outs[0].shape[1]
    specs = [pl.BlockSpec((d, tm // d, DIL_WIDTH), lambda i: (0, i, 0)) for d in dilations]
    n_scr = sum(1 for d in dilations if d != 1)
    return pl.pallas_call(
        functools.partial(_combine_kernel, dilations=dilations),
        grid=(S // tm,),
        in_specs=specs + specs + [pl.BlockSpec((1, DIL_WIDTH), lambda i: (0, 0))],
        scratch_shapes=[pltpu.VMEM((N_DIL_HEADS, tm, HEAD_DIM), F32)] * (2 * n_scr),
        out_specs=pl.BlockSpec((tm, DIL_WIDTH), lambda i: (i, 0)),
        out_shape=jax.ShapeDtypeStruct((S, DIL_WIDTH), BF16),
        compiler_params=pltpu.CompilerParams(
            dimension_semantics=("arbitrary",), vmem_limit_bytes=VMEM_LIMIT),
        name="dilated_combine",
    )(*outs, *lses, gain)


def _key_norm2_kernel(k_ref, o_ref):
    @pl.when(pl.program_id(0) == 0)
    def _():
        o_ref[...] = jnp.zeros(o_ref.shape, F32)

    k = k_ref[...].astype(F32)
    for hh in range(N_DIL_HEADS):
        kh = k[:, hh * HEAD_DIM:(hh + 1) * HEAD_DIM]
        n2 = jnp.max(jnp.sum(kh * kh, axis=1, keepdims=True), axis=0, keepdims=True)
        o_ref[hh:hh + 1, :] = jnp.maximum(o_ref[hh:hh + 1, :], jnp.broadcast_to(n2, (1, LANES)))


def _key_norm2(proj, k_blk, *, tm=1024):
    S = proj.shape[0]
    return pl.pallas_call(
        _key_norm2_kernel,
        grid=(S // tm,),
        in_specs=[pl.BlockSpec((tm, DIL_WIDTH), lambda i: (i, k_blk))],
        out_specs=pl.BlockSpec((N_DIL_HEADS, LANES), lambda i: (0, 0)),
        out_shape=jax.ShapeDtypeStruct((N_DIL_HEADS, LANES), F32),
        compiler_params=pltpu.CompilerParams(dimension_semantics=("arbitrary",), vmem_limit_bytes=VMEM_LIMIT),
        name="dilated_key_norm",
    )(proj)


def _dilated_fast_kernel(tab_ref, kn2_ref, q_ref, kp_ref, km_ref, kn_ref, vp_ref, vm_ref, vn_ref, num_ref, den_ref,
                         bias_ref, kx_ref, vx_ref, p_ref, *, R, B, half, dilation, n_chunks):
    c = pl.program_id(0)
    n = pl.program_id(1)
    W = B + 2 * half
    nblk = R // B
    VW = 2 * HEAD_DIM

    @pl.when(jnp.logical_and(c == 0, n == 0))
    def _():
        rows = 8
        col = lax.broadcasted_iota(jnp.int32, (rows, W), 1)
        row = lax.broadcasted_iota(jnp.int32, (rows, W), 0)
        for hh in range(N_DIL_HEADS):
            def fill(r, carry, hh=hh):
                r0 = pl.multiple_of(r * rows, rows)
                off = col - half - (row + r0)
                bias = _bias_from_rel(off * dilation, tab_ref, N_DIFF_HEADS + hh)
                base = jnp.where(jnp.abs(off) <= half, bias, NEG_INF * LOG2E)
                bias_ref[hh, 1, pl.ds(r0, rows), :] = base
                bias_ref[hh, 0, pl.ds(r0, rows), :] = jnp.where(col >= half, base, NEG_INF * LOG2E)
                bias_ref[hh, 2, pl.ds(r0, rows), :] = jnp.where(col < B + half, base, NEG_INF * LOG2E)
                return carry
            lax.fori_loop(0, B // rows, fill, 0)
        vx_ref[...] = jnp.ones(vx_ref.shape, BF16)

    kx_ref[0:half, :] = kp_ref[...]
    kx_ref[half:half + R, :] = km_ref[...]
    kx_ref[half + R:, :] = kn_ref[...]
    for hh in range(N_DIL_HEADS):
        src = slice(hh * HEAD_DIM, (hh + 1) * HEAD_DIM)
        dst = slice(hh * VW, hh * VW + HEAD_DIM)
        vx_ref[0:half, dst] = vp_ref[:, src]
        vx_ref[half:half + R, dst] = vm_ref[:, src]
        vx_ref[half + R:, dst] = vn_ref[:, src]

    den_ref[...] = jnp.zeros(den_ref.shape, F32)
    n_slots = p_ref.shape[0]
    bias_max = []
    for hh in range(N_DIL_HEADS):
        bm = tab_ref[0, N_DIFF_HEADS + hh]
        for bk in range(1, N_REL_BUCKETS):
            bm = jnp.maximum(bm, tab_ref[bk, N_DIFF_HEADS + hh])
        bias_max.append(bm)

    def score_stage(hh, b, slot):
        cs = slice(hh * HEAD_DIM, (hh + 1) * HEAD_DIM)
        r0 = b * B
        var = 1
        if b == 0:
            var = jnp.where(n == 0, 0, var)
        if b == nblk - 1:
            var = jnp.where(n == n_chunks - 1, 2, var)
        q = q_ref[r0:r0 + B, cs]
        qf = q.astype(F32)
        qn2 = jnp.sum(qf * qf, axis=1, keepdims=True)
        shift = jnp.sqrt(qn2 * kn2_ref[hh:hh + 1, 0:1]) * SHIFT_MARGIN + bias_max[hh]
        s = lax.dot_general(q, kx_ref[r0:r0 + W, cs], (((1,), (1,)), ((), ())), preferred_element_type=F32)
        p_ref[slot] = jnp.exp2(s + bias_ref[hh, var] - shift).astype(BF16)

    def value_stage(hh, b, slot):
        cs = slice(hh * HEAD_DIM, (hh + 1) * HEAD_DIM)
        r0 = b * B
        nd = jnp.dot(p_ref[slot], vx_ref[r0:r0 + W, hh * VW:(hh + 1) * VW], preferred_element_type=F32)
        num_ref[r0:r0 + B, cs] = nd[:, :HEAD_DIM].astype(num_ref.dtype)
        den_ref[r0:r0 + B, hh:hh + 1] = nd[:, HEAD_DIM + hh:HEAD_DIM + hh + 1]

    chains = [(hh, b) for hh in range(N_DIL_HEADS) for b in range(nblk)]
    for i, (hh, b) in enumerate(chains):
        if i > 0:
            value_stage(*chains[i - 1], (i - 1) % n_slots)
        score_stage(hh, b, i % n_slots)
    value_stage(*chains[-1], (len(chains) - 1) % n_slots)


def _dilated_fast(tab, kn2, qkv, col_blk0, window, dilation, *, B=128):
    _, L, _ = qkv.shape
    R = min(DIL_CHUNK, L)
    half = window // (2 * dilation)
    assert L % R == 0 and R % B == 0 and half % BF16_SUBLANES == 0 and R % half == 0
    n_chunks = L // R
    q_blk, k_blk, v_blk = col_blk0, col_blk0 + 1, col_blk0 + 2
    hb = R // half
    n_hblk = L // half

    def main(blk):
        return pl.BlockSpec((None, R, DIL_WIDTH), lambda c, n: (c, n, blk))

    def prev(blk):
        return pl.BlockSpec((None, half, DIL_WIDTH), lambda c, n: (c, jnp.maximum(n * hb - 1, 0), blk))

    def nxt(blk):
        return pl.BlockSpec((None, half, DIL_WIDTH), lambda c, n: (c, jnp.minimum((n + 1) * hb, n_hblk - 1), blk))

    kern = functools.partial(_dilated_fast_kernel, R=R, B=B, half=half, dilation=dilation, n_chunks=n_chunks)
    out_spec = pl.BlockSpec((None, R, DIL_WIDTH), lambda c, n: (c, n, 0))
    return pl.pallas_call(
        kern,
        grid=(dilation, n_chunks),
        in_specs=[pl.BlockSpec(memory_space=pltpu.SMEM),
                  pl.BlockSpec((N_DIL_HEADS, LANES), lambda c, n: (0, 0)),
                  main(q_blk), prev(k_blk), main(k_blk), nxt(k_blk), prev(v_blk), main(v_blk), nxt(v_blk)],
        out_specs=[out_spec, pl.BlockSpec((None, R, LANES), lambda c, n: (c, n, 0))],
        out_shape=[jax.ShapeDtypeStruct((dilation, L, DIL_WIDTH), BF16),
                   jax.ShapeDtypeStruct((dilation, L, LANES), F32)],
        scratch_shapes=[pltpu.VMEM((N_DIL_HEADS, 3, B, B + 2 * half), F32),
                        pltpu.VMEM((R + 2 * half, DIL_WIDTH), BF16),
                        pltpu.VMEM((R + 2 * half, 2 * DIL_WIDTH), BF16),
                        pltpu.VMEM((4, B, B + 2 * half), BF16)],
        compiler_params=pltpu.CompilerParams(
            dimension_semantics=("arbitrary", "arbitrary"), vmem_limit_bytes=VMEM_LIMIT),
        name=f"dilated_fast_d{dilation}",
    )(tab, kn2, qkv, qkv, qkv, qkv, qkv, qkv, qkv)


def _combine_fast_kernel(*refs, dilations):
    n = len(dilations)
    n_refs, d_refs = refs[:n], refs[n:2 * n]
    g_ref, out_ref, dmin_ref = refs[2 * n], refs[2 * n + 1], refs[2 * n + 2]
    scratch = refs[2 * n + 3:]
    tm = out_ref.shape[0]

    den_all = None
    si = 0
    for d_ref, dil in zip(d_refs, dilations):
        if dil == 1:
            den_p = d_ref[0]
        else:
            ds_ref = scratch[si + 1]
            for c in range(dil):
                ds_ref[pl.ds(c, tm // dil, stride=dil), :] = d_ref[c]
            den_p = ds_ref[...]
            si += 2
        den_all = den_p if den_all is None else den_all + den_p

    for hh in range(N_DIL_HEADS):
        sl = slice(hh * HEAD_DIM, (hh + 1) * HEAD_DIM)
        nums = []
        si = 0
        for n_ref, dil in zip(n_refs, dilations):
            if dil == 1:
                nums.append(n_ref[0, :, sl].astype(F32))
                continue
            ns_ref = scratch[si]
            si += 2
            for c in range(dil):
                ns_ref[hh, pl.ds(c, tm // dil, stride=dil), :] = n_ref[c, :, sl].astype(F32)
            nums.append(ns_ref[hh])

        den = den_all[:, hh:hh + 1]
        oh = functools.reduce(lambda a, b: a + b, nums) / den
        ms = jnp.mean(oh * oh, axis=-1, keepdims=True)
        out_ref[:, sl] = (oh * lax.rsqrt(ms + NORM_EPS) * g_ref[:, sl]).astype(out_ref.dtype)
        dmin_ref[0, hh:hh + 1, :] = jnp.broadcast_to(jnp.min(den, axis=0, keepdims=True), (1, LANES))


def _combine_fast(nums, dens, gain, dilations, *, tm=512):
    S = nums[0].shape[0] * nums[0].shape[1]
    num_specs = [pl.BlockSpec((d, tm // d, DIL_WIDTH), lambda i: (0, i, 0)) for d in dilations]
    den_specs = [pl.BlockSpec((d, tm // d, LANES), lambda i: (0, i, 0)) for d in dilations]
    n_scr = sum(1 for d in dilations if d != 1)
    return pl.pallas_call(
        functools.partial(_combine_fast_kernel, dilations=dilations),
        grid=(S // tm,),
        in_specs=num_specs + den_specs + [pl.BlockSpec((1, DIL_WIDTH), lambda i: (0, 0))],
        scratch_shapes=[pltpu.VMEM((N_DIL_HEADS, tm, HEAD_DIM), F32), pltpu.VMEM((tm, LANES), F32)] * n_scr,
        out_specs=[pl.BlockSpec((tm, DIL_WIDTH), lambda i: (i, 0)),
                   pl.BlockSpec((1, N_DIL_HEADS, LANES), lambda i: (i, 0, 0))],
        out_shape=[jax.ShapeDtypeStruct((S, DIL_WIDTH), BF16),
                   jax.ShapeDtypeStruct((S // tm, N_DIL_HEADS, LANES), F32)],
        compiler_params=pltpu.CompilerParams(
            dimension_semantics=("arbitrary",), vmem_limit_bytes=VMEM_LIMIT),
        name="dilated_combine_fast",
    )(*nums, *dens, gain)


def _outproj_kernel(od_ref, ol_ref, wd_ref, wl_ref, x_ref, g_ref, x1_ref, h2_ref):
    acc = jnp.dot(od_ref[...], wd_ref[...], preferred_element_type=F32)
    acc = acc + jnp.dot(ol_ref[...], wl_ref[...], preferred_element_type=F32)
    x1 = x_ref[...] + acc
    x1_ref[...] = x1
    ms = jnp.mean(x1 * x1, axis=-1, keepdims=True)
    h2_ref[...] = (x1 * lax.rsqrt(ms + NORM_EPS) * g_ref[...]).astype(h2_ref.dtype)


def _out_proj(o_d, o_l, w_bf, x2, gain, *, tm=512):
    S, D = x2.shape
    return pl.pallas_call(
        _outproj_kernel,
        grid=(S // tm,),
        in_specs=[
            pl.BlockSpec((tm, DIFF_WIDTH), lambda i: (i, 0)),
            pl.BlockSpec((tm, DIL_WIDTH), lambda i: (i, 0)),
            pl.BlockSpec((DIFF_WIDTH, D), lambda i: (0, 0)),
            pl.BlockSpec((DIL_WIDTH, D), lambda i: (1, 0)),
            pl.BlockSpec((tm, D), lambda i: (i, 0)),
            pl.BlockSpec((1, D), lambda i: (0, 0)),
        ],
        out_specs=[pl.BlockSpec((tm, D), lambda i: (i, 0)), pl.BlockSpec((tm, D), lambda i: (i, 0))],
        out_shape=[jax.ShapeDtypeStruct((S, D), F32), jax.ShapeDtypeStruct((S, D), BF16)],
        compiler_params=pltpu.CompilerParams(
            dimension_semantics=("arbitrary",), vmem_limit_bytes=VMEM_LIMIT),
        name="out_proj",
    )(o_d, o_l, w_bf, w_bf, x2, gain)


def _ffn_up_kernel(hm_ref, hp_ref, hn_ref, wg_ref, wu_ref, cw_ref, cb_ref, o_ref, lhs_ref, *, tm, n_row_tiles):
    i = pl.program_id(0)
    j = pl.program_id(1)
    halo = BF16_SUBLANES

    @pl.when(j == 0)
    def _():
        lhs_ref[0:halo, :] = jnp.where(i == 0, jnp.zeros_like(hp_ref[...]), hp_ref[...])
        lhs_ref[halo:halo + tm, :] = hm_ref[...]
        lhs_ref[halo + tm:, :] = jnp.where(i == n_row_tiles - 1, jnp.zeros_like(hn_ref[...]), hn_ref[...])

    g = jnp.dot(lhs_ref[...], wg_ref[...], preferred_element_type=F32)
    u = jnp.dot(lhs_ref[halo:halo + tm, :], wu_ref[...], preferred_element_type=F32)
    rows = tm + 2 * halo
    g_prev = pltpu.roll(g, 1, axis=0)
    g_next = pltpu.roll(g, rows - 1, axis=0)
    y = cw_ref[0:1, :] * g_prev + cw_ref[1:2, :] * g + cw_ref[2:3, :] * g_next + cb_ref[...]
    y = y[halo:halo + tm, :]
    act = y * (1.0 / (1.0 + jnp.exp(-y))) * u
    o_ref[...] = act.astype(o_ref.dtype)


def _ffn_up(h2, w_bf, conv_w, conv_b, *, tm=1024, tn=512):
    S, D = h2.shape
    d_ff = conv_w.shape[1]
    assert d_ff % tn == 0
    nj = d_ff // tn
    ni = S // tm
    hb = tm // BF16_SUBLANES
    n_hblk = S // BF16_SUBLANES
    kern = functools.partial(_ffn_up_kernel, tm=tm, n_row_tiles=ni)
    return pl.pallas_call(
        kern,
        grid=(ni, nj),
        in_specs=[
            pl.BlockSpec((tm, D), lambda i, j: (i, 0)),
            pl.BlockSpec((BF16_SUBLANES, D), lambda i, j: (jnp.maximum(i * hb - 1, 0), 0)),
            pl.BlockSpec((BF16_SUBLANES, D), lambda i, j: (jnp.minimum((i + 1) * hb, n_hblk - 1), 0)),
            pl.BlockSpec((D, tn), lambda i, j: (0, j)),
            pl.BlockSpec((D, tn), lambda i, j: (0, nj + j)),
            pl.BlockSpec((3, tn), lambda i, j: (0, j)),
            pl.BlockSpec((1, tn), lambda i, j: (0, j)),
        ],
        out_specs=pl.BlockSpec((tm, tn), lambda i, j: (i, j)),
        out_shape=jax.ShapeDtypeStruct((S, d_ff), BF16),
        scratch_shapes=[pltpu.VMEM((tm + 2 * BF16_SUBLANES, D), BF16)],
        compiler_params=pltpu.CompilerParams(
            dimension_semantics=("arbitrary", "arbitrary"), vmem_limit_bytes=VMEM_LIMIT),
        name="ffn_up",
    )(h2, h2, h2, w_bf, w_bf, conv_w, conv_b)


def _ffn_down_kernel(a_ref, w_ref, x1_ref, g_ref, o_ref, *, n_k):
    k = pl.program_id(1)

    @pl.when(k == 0)
    def _():
        o_ref[...] = x1_ref[...]

    o_ref[...] += jnp.dot(a_ref[...], w_ref[...], preferred_element_type=F32)

    @pl.when(k == n_k - 1)
    def _():
        y = o_ref[...]
        ms = jnp.mean(y * y, axis=-1, keepdims=True)
        o_ref[...] = y * lax.rsqrt(ms + NORM_EPS) * g_ref[...]


def _ffn_down(act, w_bf, x1, gain, *, tm=1024, tk=1408):
    S, d_ff = act.shape
    D = x1.shape[1]
    n_k = d_ff // tk
    kern = functools.partial(_ffn_down_kernel, n_k=n_k)
    return pl.pallas_call(
        kern,
        grid=(S // tm, n_k),
        in_specs=[
            pl.BlockSpec((tm, tk), lambda i, k: (i, k)),
            pl.BlockSpec((tk, D), lambda i, k: (k, 0)),
            pl.BlockSpec((tm, D), lambda i, k: (i, 0)),
            pl.BlockSpec((1, D), lambda i, k: (0, 0)),
        ],
        out_specs=pl.BlockSpec((tm, D), lambda i, k: (i, 0)),
        out_shape=jax.ShapeDtypeStruct((S, D), F32),
        compiler_params=pltpu.CompilerParams(
            dimension_semantics=("arbitrary", "arbitrary"), vmem_limit_bytes=VMEM_LIMIT),
        name="ffn_down",
    )(act, w_bf, x1, gain)


def kernel(x, norm1_gain, w_in, rel_bias_table, lambda_q1, lambda_k1, lambda_q2, lambda_k2,
           diff_subln_gain, dil_out_gain, w_out, norm2_gain, w_gate_up, conv_w, conv_b, w_down, final_gain):
    B, S, D = x.shape
    assert B == 1 and w_in.shape[0] == 1
    x2 = x.reshape(S, D)
    n_cols = w_in.shape[2]

    qscale = LOG2E / math.sqrt(HEAD_DIM)
    col = np.arange(n_cols)
    dil_q0 = 2 * DIFF_QK_COLS + DIFF_WIDTH
    is_q = (col < DIFF_QK_COLS) | ((col >= dil_q0) & (col < dil_q0 + DIL_WIDTH))
    colscale = jnp.asarray(np.where(is_q, qscale, 1.0).astype(np.float32)).reshape(1, n_cols)
    tab = rel_bias_table.astype(F32) * LOG2E

    regroup = tuple(d for _, d in DILATED_PATTERNS if d != 1)
    proj, qt_all, vt_all, *cls = _in_proj(x2, norm1_gain.reshape(1, D), w_in[0].astype(BF16), colscale, regroup)
    cls_by_dil = dict(zip(regroup, cls))

    lam = _lambda(lambda_q1.reshape(1, -1), lambda_k1.reshape(1, -1),
                  lambda_q2.reshape(1, -1), lambda_k2.reshape(1, -1))
    o_d, w_out_bf, w_gate_up_bf, w_down_bf = _diff_attention(
        tab, proj, qt_all, vt_all, lam, diff_subln_gain.reshape(-1, 1), (w_out[0], w_gate_up[0], w_down[0]))

    dil_blk0 = (2 * DIFF_QK_COLS + DIFF_WIDTH) // DIL_WIDTH
    dilations = tuple(d for _, d in DILATED_PATTERNS)
    dil_gain = dil_out_gain.reshape(1, -1)

    def pattern_inputs(dilation):
        if dilation == 1:
            return proj.reshape(1, S, proj.shape[1]), dil_blk0
        return cls_by_dil[dilation], 0

    kn2 = _key_norm2(proj, dil_blk0 + 1)
    nums, dens = [], []
    for window, dilation in DILATED_PATTERNS:
        n_p, d_p = _dilated_fast(tab, kn2, *pattern_inputs(dilation), window, dilation)
        nums.append(n_p)
        dens.append(d_p)
    o_l_fast, den_min = _combine_fast(nums, dens, dil_gain, dilations)

    def exact_dilated():
        outs, lses = [], []
        for window, dilation in DILATED_PATTERNS:
            o_p, lse_p = _dilated_pattern(tab, *pattern_inputs(dilation), window, dilation)
            outs.append(o_p)
            lses.append(lse_p)
        return _combine(outs, lses, dil_gain, dilations)

    o_l = lax.cond(jnp.min(den_min) < MIN_DENOMINATOR, lambda fast: exact_dilated(), lambda fast: fast, o_l_fast)

    x1, h2 = _out_proj(o_d, o_l, w_out_bf, x2, norm2_gain.reshape(1, D))
    act = _ffn_up(h2, w_gate_up_bf, conv_w[0], conv_b.reshape(1, -1))
    out = _ffn_down(act, w_down_bf, x1, final_gain.reshape(1, D))
    return out.reshape(B, S, D)
```

```python
import functools
import math

import numpy as np
import jax
import jax.numpy as jnp
from jax import lax
from jax.experimental import pallas as pl
from jax.experimental.pallas import tpu as pltpu

F32 = jnp.float32
BF16 = jnp.bfloat16

HEAD_DIM = 128
N_DIFF_HEADS = 4
DIFF_V_DIM = 2 * HEAD_DIM
N_DIL_HEADS = 8
DIFF_QK_COLS = N_DIFF_HEADS * 2 * HEAD_DIM
DIFF_WIDTH = N_DIFF_HEADS * DIFF_V_DIM
DIL_WIDTH = N_DIL_HEADS * HEAD_DIM
DILATED_PATTERNS = ((128, 1), (512, 4), (2048, 16))
N_REL_BUCKETS = 32
REL_MAX_DISTANCE = 1024
NORM_EPS = 1e-6
SUBLN_EPS = 1e-5
NEG_INF = -1e30
LOG2E = math.log2(math.e)
LAM_INIT = 0.8 - 0.6 * math.exp(-0.3 * 0)

LANES = 128
BF16_SUBLANES = 16
DIFF_TILE = 1024
DIFF_QUERY_PANEL = 256
FAR_UNROLL = 4
SHIFT_MARGIN = 1.0 + 2.0 ** -8
MIN_DENOMINATOR = 2.0 ** -60
DIL_CHUNK = 1024
DIL_HEAD_UNROLL = 4
VMEM_LIMIT = 56 * 1024 * 1024


def _bucket_breaks():
    nb = N_REL_BUCKETS // 2
    max_exact = nb // 2
    rel = np.arange(-2 * REL_MAX_DISTANCE, 2 * REL_MAX_DISTANCE + 1)
    n = np.abs(rel)
    pos = np.log(np.maximum(n, 1) / max_exact) / math.log(REL_MAX_DISTANCE / max_exact) * (nb - max_exact)
    large = np.minimum(max_exact + np.floor(pos).astype(np.int64), nb - 1)
    bucket = np.where(rel > 0, nb, 0) + np.where(n < max_exact, n, large)
    breaks = [(int(rel[i]), int(bucket[i])) for i in range(1, len(rel)) if bucket[i] != bucket[i - 1]]
    return int(bucket[0]), breaks


FIRST_BUCKET, BUCKET_BREAKS = _bucket_breaks()
LAST_BUCKET = BUCKET_BREAKS[-1][1]
FAR_DIST = max(-BUCKET_BREAKS[0][0] + 1, BUCKET_BREAKS[-1][0])


def _bias_from_rel(rel, tab_ref, col):
    val = jnp.full(rel.shape, tab_ref[FIRST_BUCKET, col], F32)
    for thr, b in BUCKET_BREAKS:
        val = jnp.where(rel >= thr, tab_ref[b, col], val)
    return val


N_COL_GROUPS = 6
FIRST_DIL_GROUP = 3


def _inproj_kernel(x_ref, g_ref, w_ref, cs_ref, o_ref, qt_ref, vt_ref, *rest, dilations):
    cls_refs = rest[:len(dilations)]
    h_ref, stage_ref = rest[len(dilations):]
    t = pl.program_id(1)
    tm = x_ref.shape[0]
    n_dil = N_COL_GROUPS - FIRST_DIL_GROUP

    @pl.when(t == 0)
    def _():
        x = x_ref[...]
        ms = jnp.mean(x * x, axis=-1, keepdims=True)
        h_ref[...] = (x * lax.rsqrt(ms + NORM_EPS) * g_ref[...]).astype(BF16)

    def matmul():
        acc = jnp.dot(h_ref[...], w_ref[...], preferred_element_type=F32) * cs_ref[...]
        o_ref[...] = acc.astype(o_ref.dtype)
        return acc

    def stage(acc, slot):
        for cb in range(stage_ref.shape[1]):
            stage_ref[slot, cb] = acc[:, cb * LANES:(cb + 1) * LANES]

    def regroup(slot):
        for cb in range(stage_ref.shape[1]):
            sl = slice(cb * LANES, (cb + 1) * LANES)
            for cls_ref, dil in zip(cls_refs, dilations):
                for c in range(dil):
                    cls_ref[c, :, sl] = stage_ref[slot, cb, pl.ds(c, tm // dil, stride=dil), :].astype(cls_ref.dtype)

    for step in range(N_COL_GROUPS):
        @pl.when(t == step)
        def _(step=step):
            acc = matmul()
            if 1 <= step <= n_dil:
                regroup((step - 1) % 2)
            if step < n_dil:
                stage(acc, step % 2)
            if step == n_dil:
                qt_ref[...] = acc.T.astype(qt_ref.dtype)
            if step == n_dil + 2:
                vt_ref[...] = acc.T.astype(vt_ref.dtype)


def _in_proj(x2, gain, w_bf, colscale, dilations, *, tm=512):
    S, D = x2.shape
    N = w_bf.shape[1]
    tn = DIFF_QK_COLS
    assert DIFF_WIDTH == tn and DIL_WIDTH == tn and N == N_COL_GROUPS * tn
    n_dil = N_COL_GROUPS - FIRST_DIL_GROUP
    kern = functools.partial(_inproj_kernel, dilations=dilations)

    def group(t):
        return (t + FIRST_DIL_GROUP) % N_COL_GROUPS

    cls_specs = [pl.BlockSpec((d, tm // d, tn), lambda i, t: (0, i, jnp.clip(t - 1, 0, n_dil - 1)))
                 for d in dilations]
    cls_shapes = [jax.ShapeDtypeStruct((d, S // d, n_dil * tn), BF16) for d in dilations]
    return pl.pallas_call(
        kern,
        grid=(S // tm, N_COL_GROUPS),
        in_specs=[
            pl.BlockSpec((tm, D), lambda i, t: (i, 0)),
            pl.BlockSpec((1, D), lambda i, t: (0, 0)),
            pl.BlockSpec((D, tn), lambda i, t: (0, group(t))),
            pl.BlockSpec((1, tn), lambda i, t: (0, group(t))),
        ],
        out_specs=[
            pl.BlockSpec((tm, tn), lambda i, t: (i, group(t))),
            pl.BlockSpec((tn, tm), lambda i, t: (0, i)),
            pl.BlockSpec((tn, tm), lambda i, t: (0, i)),
        ] + cls_specs,
        out_shape=[
            jax.ShapeDtypeStruct((S, N), BF16),
            jax.ShapeDtypeStruct((tn, S), BF16),
            jax.ShapeDtypeStruct((tn, S), BF16),
        ] + cls_shapes,
        scratch_shapes=[pltpu.VMEM((tm, D), BF16), pltpu.VMEM((2, tn // LANES, tm, LANES), F32)],
        compiler_params=pltpu.CompilerParams(
            dimension_semantics=("arbitrary", "arbitrary"), vmem_limit_bytes=VMEM_LIMIT),
        name="in_proj",
    )(x2, gain, w_bf, colscale)


def _diff_attn_kernel(*refs, T, QP, n_near, n_tiles, n_cast):
    tab_ref, q1t_ref, q2t_ref, k1_ref, k2_ref, vt_ref, lam_ref, gain_ref = refs[:8]
    cast_in, o_ref, cast_out = refs[8:8 + n_cast], refs[8 + n_cast], refs[9 + n_cast:9 + 2 * n_cast]
    bias_ref, m_ref, l_ref, acc_ref, p_ref, shift_ref, knorm_ref, prev_ref = refs[9 + 2 * n_cast:]
    _diff_attn_body(tab_ref, q1t_ref, q2t_ref, k1_ref, k2_ref, vt_ref, lam_ref, gain_ref, o_ref,
                    bias_ref, m_ref, l_ref, acc_ref, p_ref, shift_ref, knorm_ref, prev_ref,
                    T=T, QP=QP, n_near=n_near, n_tiles=n_tiles)
    for src, dst in zip(cast_in, cast_out):
        dst[...] = src[...].astype(dst.dtype)


def _diff_attn_body(tab_ref, q1t_ref, q2t_ref, k1_ref, k2_ref, vt_ref, lam_ref, gain_ref, o_ref,
                    bias_ref, m_ref, l_ref, acc_ref, p_ref, shift_ref, knorm_ref, prev_ref,
                    *, T, QP, n_near, n_tiles):
    h = pl.program_id(0)
    qi = pl.program_id(1)
    n_chains = 2 * (T // QP)
    qts = (q1t_ref, q2t_ref)
    ks = (k1_ref, k2_ref)

    def chain_of(i):
        qp = i // 2
        return i % 2, slice(qp * QP, (qp + 1) * QP)

    @pl.when(qi == 0)
    def _():
        for mi in range(2):
            def knorm(t, best, mi=mi):
                k = ks[mi][pl.ds(pl.multiple_of(t * T, T), T), :].astype(F32)
                return jnp.maximum(best, jnp.max(jnp.sum(k * k, axis=1, keepdims=True), axis=0, keepdims=True))
            knorm_ref[mi] = lax.fori_loop(0, n_tiles, knorm, jnp.zeros((1, 1), F32))

        x = lax.broadcasted_iota(jnp.int32, (8, 2 * T), 1)
        x = jnp.where(x < T, x, x - 2 * T)
        for di, d in enumerate(range(-n_near, n_near + 1)):
            g = _bias_from_rel(d * T - x, tab_ref, h)
            base = jnp.broadcast_to(g[0:1, :], (LANES, 2 * T))
            for rb in range(T // LANES):
                blk = pltpu.roll(base, rb * LANES, 1, stride=1, stride_axis=0)
                bias_ref[di, rb * LANES:(rb + 1) * LANES, :] = blk[:, :T]

    c_left = tab_ref[FIRST_BUCKET, h]
    c_right = tab_ref[LAST_BUCKET, h]

    bias_max = tab_ref[0, h]
    for b in range(1, N_REL_BUCKETS):
        bias_max = jnp.maximum(bias_max, tab_ref[b, h])

    for mi in range(2):
        q = qts[mi][...].astype(F32)
        qnorm2 = jnp.sum(q * q, axis=0, keepdims=True)
        shift_ref[mi] = jnp.sqrt(qnorm2 * knorm_ref[mi]) * SHIFT_MARGIN + bias_max
    l_ref[...] = jnp.zeros(l_ref.shape, F32)
    acc_ref[...] = jnp.zeros(acc_ref.shape, F32)
    p_ref[n_chains - 1] = jnp.zeros((T, QP), BF16)
    prev_ref[0] = 0

    def score_stage(i, k0, bias_di, bias_const):
        mi, qs = chain_of(i)
        s = jnp.dot(ks[mi][pl.ds(k0, T), :], qts[mi][:, qs], preferred_element_type=F32)
        if bias_di is not None:
            p = jnp.exp2(s + bias_ref[bias_di, :, qs] - shift_ref[mi, :, qs])
        else:
            p = jnp.exp2(s - (shift_ref[mi, :, qs] - bias_const))
        l_ref[mi, :, qs] += jnp.sum(p, axis=0, keepdims=True)
        p_ref[i] = p.astype(BF16)

    def value_stage(i, k0):
        mi, qs = chain_of(i)
        acc_ref[mi, :, qs] += jnp.dot(vt_ref[:, pl.ds(k0, T)], p_ref[i], preferred_element_type=F32)

    def tile(kt, bias_di, bias_const, prev_kt=None):
        k0 = pl.multiple_of(kt * T, T)
        pk0 = pl.multiple_of((prev_ref[0] if prev_kt is None else prev_kt) * T, T)
        for i in range(n_chains):
            value_stage((i - 1) % n_chains, pk0 if i < 1 else k0)
            score_stage(i, k0, bias_di, bias_const)
        prev_ref[0] = kt

    lo = jnp.maximum(qi - n_near, 0)
    hi = jnp.minimum(qi + n_near + 1, n_tiles)

    for di, d in enumerate(range(-n_near, n_near + 1)):
        kt = qi + d

        @pl.when(jnp.logical_and(kt >= 0, kt < n_tiles))
        def _(di=di, kt=kt):
            tile(kt, di, None)

    n_far = n_tiles - (hi - lo)

    def far_run(first, length):
        prev_kt = None
        for j in range(length):
            before = first + j < lo
            kt = jnp.where(before, first + j, first + j + (hi - lo))
            tile(kt, None, jnp.where(before, c_left, c_right), prev_kt=prev_kt)
            prev_kt = kt

    done = 0
    run = 1
    while run < FAR_UNROLL:
        has_run = jnp.bitwise_and(n_far, run)

        @pl.when(has_run != 0)
        def _(done=done, run=run):
            far_run(done, run)

        done = done + has_run
        run *= 2

    @pl.loop(0, n_far // FAR_UNROLL)
    def _(trip, done=done):
        far_run(done + trip * FAR_UNROLL, FAR_UNROLL)

    value_stage(n_chains - 1, pl.multiple_of(prev_ref[0] * T, T))

    @pl.when(jnp.min(l_ref[...]) < MIN_DENOMINATOR)
    def _():
        m_ref[...] = jnp.full(m_ref.shape, -jnp.inf, F32)
        l_ref[...] = jnp.zeros(l_ref.shape, F32)
        acc_ref[...] = jnp.zeros(acc_ref.shape, F32)

        @pl.loop(0, n_tiles)
        def _(kt):
            k0 = pl.multiple_of(kt * T, T)
            d = kt - qi
            near = jnp.abs(d) <= n_near
            di = jnp.clip(d + n_near, 0, 2 * n_near)
            c_far = jnp.where(d < 0, c_left, c_right)
            for i in range(n_chains):
                mi, qs = chain_of(i)
                s = jnp.dot(ks[mi][pl.ds(k0, T), :], qts[mi][:, qs], preferred_element_type=F32)
                s = s + jnp.where(near, bias_ref[di, :, qs], c_far)
                m = m_ref[mi, :, qs]
                m_new = jnp.maximum(m, jnp.max(s, axis=0, keepdims=True))
                alpha = jnp.exp2(m - m_new)
                p = jnp.exp2(s - m_new)
                m_ref[mi, :, qs] = m_new
                l_ref[mi, :, qs] = alpha * l_ref[mi, :, qs] + jnp.sum(p, axis=0, keepdims=True)
                pv = jnp.dot(vt_ref[:, pl.ds(k0, T)], p.astype(BF16), preferred_element_type=F32)
                acc_ref[mi, :, qs] = alpha * acc_ref[mi, :, qs] + pv

    lam = lam_ref[0, 0]
    o = acc_ref[0] / l_ref[0] - lam * (acc_ref[1] / l_ref[1])
    ms = jnp.mean(o * o, axis=0, keepdims=True)
    o = o * lax.rsqrt(ms + SUBLN_EPS) * (gain_ref[...] * (1.0 - LAM_INIT))
    o_ref[...] = o.T.astype(o_ref.dtype)


def _lambda_kernel(q1_ref, k1_ref, q2_ref, k2_ref, o_ref):
    a = jnp.sum(q1_ref[...] * k1_ref[...], axis=-1, keepdims=True)
    b = jnp.sum(q2_ref[...] * k2_ref[...], axis=-1, keepdims=True)
    o_ref[...] = jnp.exp(a) - jnp.exp(b) + LAM_INIT


def _lambda(lq1, lk1, lq2, lk2):
    return pl.pallas_call(
        _lambda_kernel, out_shape=jax.ShapeDtypeStruct((1, 1), F32), name="diff_lambda",
    )(lq1, lk1, lq2, lk2)


def _diff_attention(tab_diff, proj, qt_all, vt_all, lam, gain_col, f32_weights):
    S = proj.shape[0]
    T = DIFF_TILE
    n_tiles = S // T
    n_near = -(-(FAR_DIST - 1) // T)
    kern = functools.partial(_diff_attn_kernel, T=T, QP=DIFF_QUERY_PANEL, n_near=n_near, n_tiles=n_tiles,
                             n_cast=len(f32_weights))

    n_steps = N_DIFF_HEADS * n_tiles
    cast_specs = []
    for w in f32_weights:
        rows, cols = w.shape
        every = next(e for e in (1, 2, 4, 8) if rows % (n_steps // e) == 0
                     and (rows // (n_steps // e)) % BF16_SUBLANES == 0)
        cast_specs.append(pl.BlockSpec((rows // (n_steps // every), cols),
                                       lambda h, i, every=every: ((h * n_tiles + i) // every, 0)))
    k_block0 = DIFF_QK_COLS // HEAD_DIM
    smem = pl.BlockSpec(memory_space=pltpu.SMEM)
    return pl.pallas_call(
        kern,
        grid=(N_DIFF_HEADS, n_tiles),
        in_specs=[
            smem,
            pl.BlockSpec((HEAD_DIM, T), lambda h, i: (2 * h, i)),
            pl.BlockSpec((HEAD_DIM, T), lambda h, i: (2 * h + 1, i)),
            pl.BlockSpec((S, HEAD_DIM), lambda h, i: (0, k_block0 + 2 * h), pipeline_mode=pl.Buffered(1)),
            pl.BlockSpec((S, HEAD_DIM), lambda h, i: (0, k_block0 + 2 * h + 1), pipeline_mode=pl.Buffered(1)),
            pl.BlockSpec((DIFF_V_DIM, S), lambda h, i: (h, 0), pipeline_mode=pl.Buffered(1)),
            smem,
            pl.BlockSpec((DIFF_V_DIM, 1), lambda h, i: (0, 0)),
        ] + cast_specs,
        out_specs=[pl.BlockSpec((T, DIFF_V_DIM), lambda h, i: (i, h))] + cast_specs,
        out_shape=[jax.ShapeDtypeStruct((S, DIFF_WIDTH), BF16)]
        + [jax.ShapeDtypeStruct(w.shape, BF16) for w in f32_weights],
        scratch_shapes=[pltpu.VMEM((2 * n_near + 1, T, T), F32),
                        pltpu.VMEM((2, 1, T), F32), pltpu.VMEM((2, 1, T), F32),
                        pltpu.VMEM((2, DIFF_V_DIM, T), F32),
                        pltpu.VMEM((2 * (T // DIFF_QUERY_PANEL), T, DIFF_QUERY_PANEL), BF16),
                        pltpu.VMEM((2, 1, T), F32), pltpu.VMEM((2, 1, 1), F32),
                        pltpu.SMEM((1,), jnp.int32)],
        compiler_params=pltpu.CompilerParams(
            dimension_semantics=("arbitrary", "arbitrary"), vmem_limit_bytes=VMEM_LIMIT),
        name="diff_attn",
    )(tab_diff, qt_all, qt_all, proj, proj, vt_all, lam, gain_col, *f32_weights)


def _dilated_kernel(tab_ref, q_ref, kp_ref, km_ref, kn_ref, vp_ref, vm_ref, vn_ref, o_ref, lse_ref,
                    bias_ref, kx_ref, vx_ref, *, R, B, half, dilation, n_chunks):
    c = pl.program_id(0)
    n = pl.program_id(1)
    W = B + 2 * half
    nblk = R // B

    @pl.when(jnp.logical_and(c == 0, n == 0))
    def _():
        rows = 8
        col = lax.broadcasted_iota(jnp.int32, (rows, W), 1)
        row = lax.broadcasted_iota(jnp.int32, (rows, W), 0)
        for hh in range(N_DIL_HEADS):
            def fill(r, carry, hh=hh):
                r0 = pl.multiple_of(r * rows, rows)
                off = col - half - (row + r0)
                bias = _bias_from_rel(off * dilation, tab_ref, N_DIFF_HEADS + hh)
                base = jnp.where(jnp.abs(off) <= half, bias, NEG_INF * LOG2E)
                bias_ref[hh, 1, pl.ds(r0, rows), :] = base
                bias_ref[hh, 0, pl.ds(r0, rows), :] = jnp.where(col >= half, base, NEG_INF * LOG2E)
                bias_ref[hh, 2, pl.ds(r0, rows), :] = jnp.where(col < B + half, base, NEG_INF * LOG2E)
                return carry
            lax.fori_loop(0, B // rows, fill, 0)

    kx_ref[0:half, :] = kp_ref[...]
    kx_ref[half:half + R, :] = km_ref[...]
    kx_ref[half + R:, :] = kn_ref[...]
    vx_ref[0:half, :] = vp_ref[...]
    vx_ref[half:half + R, :] = vm_ref[...]
    vx_ref[half + R:, :] = vn_ref[...]

    def chain(hh, b):
        c0 = pl.multiple_of(hh * HEAD_DIM, HEAD_DIM)
        r0 = b * B
        var = 1
        if b == 0:
            var = jnp.where(n == 0, 0, var)
        if b == nblk - 1:
            var = jnp.where(n == n_chunks - 1, 2, var)
        q = q_ref[pl.ds(r0, B), pl.ds(c0, HEAD_DIM)]
        k = kx_ref[pl.ds(r0, W), pl.ds(c0, HEAD_DIM)]
        v = vx_ref[pl.ds(r0, W), pl.ds(c0, HEAD_DIM)]
        s = lax.dot_general(q, k, (((1,), (1,)), ((), ())), preferred_element_type=F32)
        s = s + bias_ref[hh, var]
        m = jnp.max(s, axis=-1, keepdims=True)
        e = jnp.exp2(s - m)
        den = jnp.sum(e, axis=-1, keepdims=True)
        o = jnp.dot(e.astype(BF16), v, preferred_element_type=F32) / den
        o_ref[pl.ds(r0, B), pl.ds(c0, HEAD_DIM)] = o.astype(o_ref.dtype)
        lse = m + jnp.log2(den)
        lse_ref[pl.ds(r0, B), pl.ds(c0, HEAD_DIM)] = jnp.broadcast_to(lse, (B, HEAD_DIM))

    @pl.loop(0, N_DIL_HEADS // DIL_HEAD_UNROLL)
    def _(hg):
        for u in range(DIL_HEAD_UNROLL):
            for b in range(nblk):
                chain(hg * DIL_HEAD_UNROLL + u, b)


def _dilated_pattern(tab, qkv, col_blk0, window, dilation, *, B=256):
    _, L, _ = qkv.shape
    R = min(DIL_CHUNK, L)
    half = window // (2 * dilation)
    assert L % R == 0 and R % B == 0 and half % BF16_SUBLANES == 0 and R % half == 0
    n_chunks = L // R
    q_blk, k_blk, v_blk = col_blk0, col_blk0 + 1, col_blk0 + 2
    hb = R // half
    n_hblk = L // half

    def main(blk):
        return pl.BlockSpec((None, R, DIL_WIDTH), lambda c, n: (c, n, blk))

    def prev(blk):
        return pl.BlockSpec((None, half, DIL_WIDTH), lambda c, n: (c, jnp.maximum(n * hb - 1, 0), blk))

    def nxt(blk):
        return pl.BlockSpec((None, half, DIL_WIDTH), lambda c, n: (c, jnp.minimum((n + 1) * hb, n_hblk - 1), blk))

    kern = functools.partial(_dilated_kernel, R=R, B=B, half=half, dilation=dilation, n_chunks=n_chunks)
    out_spec = pl.BlockSpec((None, R, DIL_WIDTH), lambda c, n: (c, n, 0))
    return pl.pallas_call(
        kern,
        grid=(dilation, n_chunks),
        in_specs=[pl.BlockSpec(memory_space=pltpu.SMEM),
                  main(q_blk), prev(k_blk), main(k_blk), nxt(k_blk), prev(v_blk), main(v_blk), nxt(v_blk)],
        out_specs=[out_spec, out_spec],
        out_shape=[jax.ShapeDtypeStruct((dilation, L, DIL_WIDTH), BF16),
                   jax.ShapeDtypeStruct((dilation, L, DIL_WIDTH), F32)],
        scratch_shapes=[pltpu.VMEM((N_DIL_HEADS, 3, B, B + 2 * half), F32),
                        pltpu.VMEM((R + 2 * half, DIL_WIDTH), BF16),
                        pltpu.VMEM((R + 2 * half, DIL_WIDTH), BF16)],
        compiler_params=pltpu.CompilerParams(
            dimension_semantics=("arbitrary", "arbitrary"), vmem_limit_bytes=VMEM_LIMIT),
        name=f"dilated_d{dilation}",
    )(tab, qkv, qkv, qkv, qkv, qkv, qkv, qkv)


def _combine_kernel(*refs, dilations):
    n = len(dilations)
    o_refs, l_refs = refs[:n], refs[n:2 * n]
    g_ref, out_ref = refs[2 * n], refs[2 * n + 1]
    scratch = refs[2 * n + 2:]
    tm = out_ref.shape[0]

    for hh in range(N_DIL_HEADS):
        sl = slice(hh * HEAD_DIM, (hh + 1) * HEAD_DIM)
        outs, lses = [], []
        si = 0
        for o_ref, l_ref, dil in zip(o_refs, l_refs, dilations):
            if dil == 1:
                outs.append(o_ref[0, :, sl].astype(F32))
                lses.append(l_ref[0, :, sl])
                continue
            os_ref, ls_ref = scratch[si], scratch[si + 1]
            si += 2
            for c in range(dil):
                os_ref[hh, pl.ds(c, tm // dil, stride=dil), :] = o_ref[c, :, sl].astype(F32)
                ls_ref[hh, pl.ds(c, tm // dil, stride=dil), :] = l_ref[c, :, sl]
            outs.append(os_ref[hh])
            lses.append(ls_ref[hh])

        m = functools.reduce(jnp.maximum, lses)
        ws = [jnp.exp2(l - m) for l in lses]
        tot = functools.reduce(lambda a, b: a + b, ws)
        oh = functools.reduce(lambda a, b: a + b, [(w / tot) * op for w, op in zip(ws, outs)])
        ms = jnp.mean(oh * oh, axis=-1, keepdims=True)
        out_ref[:, sl] = (oh * lax.rsqrt(ms + NORM_EPS) * g_ref[:, sl]).astype(out_ref.dtype)


def _combine(outs, lses, gain, dilations, *, tm=512):
    S = outs[0].shape[0] * outs[0].shape[1]
    specs = [pl.BlockSpec((d, tm // d, DIL_WIDTH), lambda i: (0, i, 0)) for d in dilations]
    n_scr = sum(1 for d in dilations if d != 1)
    return pl.pallas_call(
        functools.partial(_combine_kernel, dilations=dilations),
        grid=(S // tm,),
        in_specs=specs + specs + [pl.BlockSpec((1, DIL_WIDTH), lambda i: (0, 0))],
        scratch_shapes=[pltpu.VMEM((N_DIL_HEADS, tm, HEAD_DIM), F32)] * (2 * n_scr),
        out_specs=pl.BlockSpec((tm, DIL_WIDTH), lambda i: (i, 0)),
        out_shape=jax.ShapeDtypeStruct((S, DIL_WIDTH), BF16),
        compiler_params=pltpu.CompilerParams(
            dimension_semantics=("arbitrary",), vmem_limit_bytes=VMEM_LIMIT),
        name="dilated_combine",
    )(*outs, *lses, gain)


def _key_norm2_kernel(k_ref, o_ref):
    @pl.when(pl.program_id(0) == 0)
    def _():
        o_ref[...] = jnp.zeros(o_ref.shape, F32)

    k = k_ref[...].astype(F32)
    for hh in range(N_DIL_HEADS):
        kh = k[:, hh * HEAD_DIM:(hh + 1) * HEAD_DIM]
        n2 = jnp.max(jnp.sum(kh * kh, axis=1, keepdims=True), axis=0, keepdims=True)
        o_ref[hh:hh + 1, :] = jnp.maximum(o_ref[hh:hh + 1, :], jnp.broadcast_to(n2, (1, LANES)))


def _key_norm2(proj, k_blk, *, tm=1024):
    S = proj.shape[0]
    return pl.pallas_call(
        _key_norm2_kernel,
        grid=(S // tm,),
        in_specs=[pl.BlockSpec((tm, DIL_WIDTH), lambda i: (i, k_blk))],
        out_specs=pl.BlockSpec((N_DIL_HEADS, LANES), lambda i: (0, 0)),
        out_shape=jax.ShapeDtypeStruct((N_DIL_HEADS, LANES), F32),
        compiler_params=pltpu.CompilerParams(dimension_semantics=("arbitrary",), vmem_limit_bytes=VMEM_LIMIT),
        name="dilated_key_norm",
    )(proj)


def _dilated_fast_kernel(tab_ref, kn2_ref, q_ref, kp_ref, km_ref, kn_ref, vp_ref, vm_ref, vn_ref, num_ref, den_ref,
                         bias_ref, kx_ref, vx_ref, p_ref, *, R, B, half, dilation, n_chunks):
    c = pl.program_id(0)
    n = pl.program_id(1)
    W = B + 2 * half
    nblk = R // B
    VW = 2 * HEAD_DIM

    @pl.when(jnp.logical_and(c == 0, n == 0))
    def _():
        rows = 8
        col = lax.broadcasted_iota(jnp.int32, (rows, W), 1)
        row = lax.broadcasted_iota(jnp.int32, (rows, W), 0)
        for hh in range(N_DIL_HEADS):
            def fill(r, carry, hh=hh):
                r0 = pl.multiple_of(r * rows, rows)
                off = col - half - (row + r0)
                bias = _bias_from_rel(off * dilation, tab_ref, N_DIFF_HEADS + hh)
                base = jnp.where(jnp.abs(off) <= half, bias, NEG_INF * LOG2E)
                bias_ref[hh, 1, pl.ds(r0, rows), :] = base
                bias_ref[hh, 0, pl.ds(r0, rows), :] = jnp.where(col >= half, base, NEG_INF * LOG2E)
                bias_ref[hh, 2, pl.ds(r0, rows), :] = jnp.where(col < B + half, base, NEG_INF * LOG2E)
                return carry
            lax.fori_loop(0, B // rows, fill, 0)
        vx_ref[...] = jnp.ones(vx_ref.shape, BF16)

    kx_ref[0:half, :] = kp_ref[...]
    kx_ref[half:half + R, :] = km_ref[...]
    kx_ref[half + R:, :] = kn_ref[...]
    for hh in range(N_DIL_HEADS):
        src = slice(hh * HEAD_DIM, (hh + 1) * HEAD_DIM)
        dst = slice(hh * VW, hh * VW + HEAD_DIM)
        vx_ref[0:half, dst] = vp_ref[:, src]
        vx_ref[half:half + R, dst] = vm_ref[:, src]
        vx_ref[half + R:, dst] = vn_ref[:, src]

    den_ref[...] = jnp.zeros(den_ref.shape, F32)
    n_slots = p_ref.shape[0]
    bias_max = []
    for hh in range(N_DIL_HEADS):
        bm = tab_ref[0, N_DIFF_HEADS + hh]
        for bk in range(1, N_REL_BUCKETS):
            bm = jnp.maximum(bm, tab_ref[bk, N_DIFF_HEADS + hh])
        bias_max.append(bm)

    def score_stage(hh, b, slot):
        cs = slice(hh * HEAD_DIM, (hh + 1) * HEAD_DIM)
        r0 = b * B
        var = 1
        if b == 0:
            var = jnp.where(n == 0, 0, var)
        if b == nblk - 1:
            var = jnp.where(n == n_chunks - 1, 2, var)
        q = q_ref[r0:r0 + B, cs]
        qf = q.astype(F32)
        qn2 = jnp.sum(qf * qf, axis=1, keepdims=True)
        shift = jnp.sqrt(qn2 * kn2_ref[hh:hh + 1, 0:1]) * SHIFT_MARGIN + bias_max[hh]
        s = lax.dot_general(q, kx_ref[r0:r0 + W, cs], (((1,), (1,)), ((), ())), preferred_element_type=F32)
        p_ref[slot] = jnp.exp2(s + bias_ref[hh, var] - shift).astype(BF16)

    def value_stage(hh, b, slot):
        cs = slice(hh * HEAD_DIM, (hh + 1) * HEAD_DIM)
        r0 = b * B
        nd = jnp.dot(p_ref[slot], vx_ref[r0:r0 + W, hh * VW:(hh + 1) * VW], preferred_element_type=F32)
        num_ref[r0:r0 + B, cs] = nd[:, :HEAD_DIM].astype(num_ref.dtype)
        den_ref[r0:r0 + B, hh:hh + 1] = nd[:, HEAD_DIM + hh:HEAD_DIM + hh + 1]

    chains = [(hh, b) for hh in range(N_DIL_HEADS) for b in range(nblk)]
    for i, (hh, b) in enumerate(chains):
        if i > 0:
            value_stage(*chains[i - 1], (i - 1) % n_slots)
        score_stage(hh, b, i % n_slots)
    value_stage(*chains[-1], (len(chains) - 1) % n_slots)


def _dilated_fast(tab, kn2, qkv, col_blk0, window, dilation, *, B=128):
    _, L, _ = qkv.shape
    R = min(DIL_CHUNK, L)
    half = window // (2 * dilation)
    assert L % R == 0 and R % B == 0 and half % BF16_SUBLANES == 0 and R % half == 0
    n_chunks = L // R
    q_blk, k_blk, v_blk = col_blk0, col_blk0 + 1, col_blk0 + 2
    hb = R // half
    n_hblk = L // half

    def main(blk):
        return pl.BlockSpec((None, R, DIL_WIDTH), lambda c, n: (c, n, blk))

    def prev(blk):
        return pl.BlockSpec((None, half, DIL_WIDTH), lambda c, n: (c, jnp.maximum(n * hb - 1, 0), blk))

    def nxt(blk):
        return pl.BlockSpec((None, half, DIL_WIDTH), lambda c, n: (c, jnp.minimum((n + 1) * hb, n_hblk - 1), blk))

    kern = functools.partial(_dilated_fast_kernel, R=R, B=B, half=half, dilation=dilation, n_chunks=n_chunks)
    out_spec = pl.BlockSpec((None, R, DIL_WIDTH), lambda c, n: (c, n, 0))
    return pl.pallas_call(
        kern,
        grid=(dilation, n_chunks),
        in_specs=[pl.BlockSpec(memory_space=pltpu.SMEM),
                  pl.BlockSpec((N_DIL_HEADS, LANES), lambda c, n: (0, 0)),
                  main(q_blk), prev(k_blk), main(k_blk), nxt(k_blk), prev(v_blk), main(v_blk), nxt(v_blk)],
        out_specs=[out_spec, pl.BlockSpec((None, R, LANES), lambda c, n: (c, n, 0))],
        out_shape=[jax.ShapeDtypeStruct((dilation, L, DIL_WIDTH), BF16),
                   jax.ShapeDtypeStruct((dilation, L, LANES), F32)],
        scratch_shapes=[pltpu.VMEM((N_DIL_HEADS, 3, B, B + 2 * half), F32),
                        pltpu.VMEM((R + 2 * half, DIL_WIDTH), BF16),
                        pltpu.VMEM((R + 2 * half, 2 * DIL_WIDTH), BF16),
                        pltpu.VMEM((4, B, B + 2 * half), BF16)],
        compiler_params=pltpu.CompilerParams(
            dimension_semantics=("arbitrary", "arbitrary"), vmem_limit_bytes=VMEM_LIMIT),
        name=f"dilated_fast_d{dilation}",
    )(tab, kn2, qkv, qkv, qkv, qkv, qkv, qkv, qkv)


def _combine_fast_kernel(*refs, dilations):
    n = len(dilations)
    n_refs, d_refs = refs[:n], refs[n:2 * n]
    g_ref, out_ref, dmin_ref = refs[2 * n], refs[2 * n + 1], refs[2 * n + 2]
    scratch = refs[2 * n + 3:]
    tm = out_ref.shape[0]

    den_all = None
    si = 0
    for d_ref, dil in zip(d_refs, dilations):
        if dil == 1:
            den_p = d_ref[0]
        else:
            ds_ref = scratch[si + 1]
            for c in range(dil):
                ds_ref[pl.ds(c, tm // dil, stride=dil), :] = d_ref[c]
            den_p = ds_ref[...]
            si += 2
        den_all = den_p if den_all is None else den_all + den_p

    for hh in range(N_DIL_HEADS):
        sl = slice(hh * HEAD_DIM, (hh + 1) * HEAD_DIM)
        nums = []
        si = 0
        for n_ref, dil in zip(n_refs, dilations):
            if dil == 1:
                nums.append(n_ref[0, :, sl].astype(F32))
                continue
            ns_ref = scratch[si]
            si += 2
            for c in range(dil):
                ns_ref[hh, pl.ds(c, tm // dil, stride=dil), :] = n_ref[c, :, sl].astype(F32)
            nums.append(ns_ref[hh])

        den = den_all[:, hh:hh + 1]
        oh = functools.reduce(lambda a, b: a + b, nums) / den
        ms = jnp.mean(oh * oh, axis=-1, keepdims=True)
        out_ref[:, sl] = (oh * lax.rsqrt(ms + NORM_EPS) * g_ref[:, sl]).astype(out_ref.dtype)
        dmin_ref[0, hh:hh + 1, :] = jnp.broadcast_to(jnp.min(den, axis=0, keepdims=True), (1, LANES))


def _combine_fast(nums, dens, gain, dilations, *, tm=512):
    S = nums[0].shape[0] * nums[0].shape[1]
    num_specs = [pl.BlockSpec((d, tm // d, DIL_WIDTH), lambda i: (0, i, 0)) for d in dilations]
    den_specs = [pl.BlockSpec((d, tm // d, LANES), lambda i: (0, i, 0)) for d in dilations]
    n_scr = sum(1 for d in dilations if d != 1)
    return pl.pallas_call(
        functools.partial(_combine_fast_kernel, dilations=dilations),
        grid=(S // tm,),
        in_specs=num_specs + den_specs + [pl.BlockSpec((1, DIL_WIDTH), lambda i: (0, 0))],
        scratch_shapes=[pltpu.VMEM((N_DIL_HEADS, tm, HEAD_DIM), F32), pltpu.VMEM((tm, LANES), F32)] * n_scr,
        out_specs=[pl.BlockSpec((tm, DIL_WIDTH), lambda i: (i, 0)),
                   pl.BlockSpec((1, N_DIL_HEADS, LANES), lambda i: (i, 0, 0))],
        out_shape=[jax.ShapeDtypeStruct((S, DIL_WIDTH), BF16),
                   jax.ShapeDtypeStruct((S // tm, N_DIL_HEADS, LANES), F32)],
        compiler_params=pltpu.CompilerParams(
            dimension_semantics=("arbitrary",), vmem_limit_bytes=VMEM_LIMIT),
        name="dilated_combine_fast",
    )(*nums, *dens, gain)


def _outproj_kernel(od_ref, ol_ref, wd_ref, wl_ref, x_ref, g_ref, x1_ref, h2_ref):
    acc = jnp.dot(od_ref[...], wd_ref[...], preferred_element_type=F32)
    acc = acc + jnp.dot(ol_ref[...], wl_ref[...], preferred_element_type=F32)
    x1 = x_ref[...] + acc
    x1_ref[...] = x1
    ms = jnp.mean(x1 * x1, axis=-1, keepdims=True)
    h2_ref[...] = (x1 * lax.rsqrt(ms + NORM_EPS) * g_ref[...]).astype(h2_ref.dtype)


def _out_proj(o_d, o_l, w_bf, x2, gain, *, tm=512):
    S, D = x2.shape
    return pl.pallas_call(
        _outproj_kernel,
        grid=(S // tm,),
        in_specs=[
            pl.BlockSpec((tm, DIFF_WIDTH), lambda i: (i, 0)),
            pl.BlockSpec((tm, DIL_WIDTH), lambda i: (i, 0)),
            pl.BlockSpec((DIFF_WIDTH, D), lambda i: (0, 0)),
            pl.BlockSpec((DIL_WIDTH, D), lambda i: (1, 0)),
            pl.BlockSpec((tm, D), lambda i: (i, 0)),
            pl.BlockSpec((1, D), lambda i: (0, 0)),
        ],
        out_specs=[pl.BlockSpec((tm, D), lambda i: (i, 0)), pl.BlockSpec((tm, D), lambda i: (i, 0))],
        out_shape=[jax.ShapeDtypeStruct((S, D), F32), jax.ShapeDtypeStruct((S, D), BF16)],
        compiler_params=pltpu.CompilerParams(
            dimension_semantics=("arbitrary",), vmem_limit_bytes=VMEM_LIMIT),
        name="out_proj",
    )(o_d, o_l, w_bf, w_bf, x2, gain)


def _ffn_up_kernel(hm_ref, hp_ref, hn_ref, wg_ref, wu_ref, cw_ref, cb_ref, o_ref, lhs_ref, *, tm, n_row_tiles):
    i = pl.program_id(0)
    j = pl.program_id(1)
    halo = BF16_SUBLANES

    @pl.when(j == 0)
    def _():
        lhs_ref[0:halo, :] = jnp.where(i == 0, jnp.zeros_like(hp_ref[...]), hp_ref[...])
        lhs_ref[halo:halo + tm, :] = hm_ref[...]
        lhs_ref[halo + tm:, :] = jnp.where(i == n_row_tiles - 1, jnp.zeros_like(hn_ref[...]), hn_ref[...])

    g = jnp.dot(lhs_ref[...], wg_ref[...], preferred_element_type=F32)
    u = jnp.dot(lhs_ref[halo:halo + tm, :], wu_ref[...], preferred_element_type=F32)
    rows = tm + 2 * halo
    g_prev = pltpu.roll(g, 1, axis=0)
    g_next = pltpu.roll(g, rows - 1, axis=0)
    y = cw_ref[0:1, :] * g_prev + cw_ref[1:2, :] * g + cw_ref[2:3, :] * g_next + cb_ref[...]
    y = y[halo:halo + tm, :]
    act = y * (1.0 / (1.0 + jnp.exp(-y))) * u
    o_ref[...] = act.astype(o_ref.dtype)


def _ffn_up(h2, w_bf, conv_w, conv_b, *, tm=1024, tn=512):
    S, D = h2.shape
    d_ff = conv_w.shape[1]
    assert d_ff % tn == 0
    nj = d_ff // tn
    ni = S // tm
    hb = tm // BF16_SUBLANES
    n_hblk = S // BF16_SUBLANES
    kern = functools.partial(_ffn_up_kernel, tm=tm, n_row_tiles=ni)
    return pl.pallas_call(
        kern,
        grid=(ni, nj),
        in_specs=[
            pl.BlockSpec((tm, D), lambda i, j: (i, 0)),
            pl.BlockSpec((BF16_SUBLANES, D), lambda i, j: (jnp.maximum(i * hb - 1, 0), 0)),
            pl.BlockSpec((BF16_SUBLANES, D), lambda i, j: (jnp.minimum((i + 1) * hb, n_hblk - 1), 0)),
            pl.BlockSpec((D, tn), lambda i, j: (0, j)),
            pl.BlockSpec((D, tn), lambda i, j: (0, nj + j)),
            pl.BlockSpec((3, tn), lambda i, j: (0, j)),
            pl.BlockSpec((1, tn), lambda i, j: (0, j)),
        ],
        out_specs=pl.BlockSpec((tm, tn), lambda i, j: (i, j)),
        out_shape=jax.ShapeDtypeStruct((S, d_ff), BF16),
        scratch_shapes=[pltpu.VMEM((tm + 2 * BF16_SUBLANES, D), BF16)],
        compiler_params=pltpu.CompilerParams(
            dimension_semantics=("arbitrary", "arbitrary"), vmem_limit_bytes=VMEM_LIMIT),
        name="ffn_up",
    )(h2, h2, h2, w_bf, w_bf, conv_w, conv_b)


def _ffn_down_kernel(a_ref, w_ref, x1_ref, g_ref, o_ref, *, n_k):
    k = pl.program_id(1)

    @pl.when(k == 0)
    def _():
        o_ref[...] = x1_ref[...]

    o_ref[...] += jnp.dot(a_ref[...], w_ref[...], preferred_element_type=F32)

    @pl.when(k == n_k - 1)
    def _():
        y = o_ref[...]
        ms = jnp.mean(y * y, axis=-1, keepdims=True)
        o_ref[...] = y * lax.rsqrt(ms + NORM_EPS) * g_ref[...]


def _ffn_down(act, w_bf, x1, gain, *, tm=1024, tk=1408):
    S, d_ff = act.shape
    D = x1.shape[1]
    n_k = d_ff // tk
    kern = functools.partial(_ffn_down_kernel, n_k=n_k)
    return pl.pallas_call(
        kern,
        grid=(S // tm, n_k),
        in_specs=[
            pl.BlockSpec((tm, tk), lambda i, k: (i, k)),
            pl.BlockSpec((tk, D), lambda i, k: (k, 0)),
            pl.BlockSpec((tm, D), lambda i, k: (i, 0)),
            pl.BlockSpec((1, D), lambda i, k: (0, 0)),
        ],
        out_specs=pl.BlockSpec((tm, D), lambda i, k: (i, 0)),
        out_shape=jax.ShapeDtypeStruct((S, D), F32),
        compiler_params=pltpu.CompilerParams(
            dimension_semantics=("arbitrary", "arbitrary"), vmem_limit_bytes=VMEM_LIMIT),
        name="ffn_down",
    )(act, w_bf, x1, gain)


def kernel(x, norm1_gain, w_in, rel_bias_table, lambda_q1, lambda_k1, lambda_q2, lambda_k2,
           diff_subln_gain, dil_out_gain, w_out, norm2_gain, w_gate_up, conv_w, conv_b, w_down, final_gain):
    B, S, D = x.shape
    assert B == 1 and w_in.shape[0] == 1
    x2 = x.reshape(S, D)
    n_cols = w_in.shape[2]

    qscale = LOG2E / math.sqrt(HEAD_DIM)
    col = np.arange(n_cols)
    dil_q0 = 2 * DIFF_QK_COLS + DIFF_WIDTH
    is_q = (col < DIFF_QK_COLS) | ((col >= dil_q0) & (col < dil_q0 + DIL_WIDTH))
    colscale = jnp.asarray(np.where(is_q, qscale, 1.0).astype(np.float32)).reshape(1, n_cols)
    tab = rel_bias_table.astype(F32) * LOG2E

    regroup = tuple(d for _, d in DILATED_PATTERNS if d != 1)
    proj, qt_all, vt_all, *cls = _in_proj(x2, norm1_gain.reshape(1, D), w_in[0].astype(BF16), colscale, regroup)
    cls_by_dil = dict(zip(regroup, cls))

    lam = _lambda(lambda_q1.reshape(1, -1), lambda_k1.reshape(1, -1),
                  lambda_q2.reshape(1, -1), lambda_k2.reshape(1, -1))
    o_d, w_out_bf, w_gate_up_bf, w_down_bf = _diff_attention(
        tab, proj, qt_all, vt_all, lam, diff_subln_gain.reshape(-1, 1), (w_out[0], w_gate_up[0], w_down[0]))

    dil_blk0 = (2 * DIFF_QK_COLS + DIFF_WIDTH) // DIL_WIDTH
    dilations = tuple(d for _, d in DILATED_PATTERNS)
    dil_gain = dil_out_gain.reshape(1, -1)

    def pattern_inputs(dilation):
        if dilation == 1:
            return proj.reshape(1, S, proj.shape[1]), dil_blk0
        return cls_by_dil[dilation], 0

    kn2 = _key_norm2(proj, dil_blk0 + 1)
    nums, dens = [], []
    for window, dilation in DILATED_PATTERNS:
        n_p, d_p = _dilated_fast(tab, kn2, *pattern_inputs(dilation), window, dilation)
        nums.append(n_p)
        dens.append(d_p)
    o_l_fast, den_min = _combine_fast(nums, dens, dil_gain, dilations)

    def exact_dilated():
        outs, lses = [], []
        for window, dilation in DILATED_PATTERNS:
            o_p, lse_p = _dilated_pattern(tab, *pattern_inputs(dilation), window, dilation)
            outs.append(o_p)
            lses.append(lse_p)
        return _combine(outs, lses, dil_gain, dilations)

    o_l = lax.cond(jnp.min(den_min) < MIN_DENOMINATOR, lambda fast: exact_dilated(), lambda fast: fast, o_l_fast)

    x1, h2 = _out_proj(o_d, o_l, w_out_bf, x2, norm2_gain.reshape(1, D))
    act = _ffn_up(h2, w_gate_up_bf, conv_w[0], conv_b.reshape(1, -1))
    out = _ffn_down(act, w_down_bf, x1, final_gain.reshape(1, D))
    return out.reshape(B, S, D)
```

```python
import functools
import math

import numpy as np
import jax
import jax.numpy as jnp
from jax import lax
from jax.experimental import pallas as pl
from jax.experimental.pallas import tpu as pltpu

F32 = jnp.float32
BF16 = jnp.bfloat16

HEAD_DIM = 128
N_DIFF_HEADS = 4
DIFF_V_DIM = 2 * HEAD_DIM
N_DIL_HEADS = 8
DIFF_QK_COLS = N_DIFF_HEADS * 2 * HEAD_DIM
DIFF_WIDTH = N_DIFF_HEADS * DIFF_V_DIM
DIL_WIDTH = N_DIL_HEADS * HEAD_DIM
DILATED_PATTERNS = ((128, 1), (512, 4), (2048, 16))
N_REL_BUCKETS = 32
REL_MAX_DISTANCE = 1024
NORM_EPS = 1e-6
SUBLN_EPS = 1e-5
NEG_INF = -1e30
LOG2E = math.log2(math.e)
LAM_INIT = 0.8 - 0.6 * math.exp(-0.3 * 0)

LANES = 128
F32_SUBLANES = 8
BF16_SUBLANES = 16
DIFF_TILE = 1024
DIFF_QUERY_PANEL = 256
FAR_UNROLL = 4
SHIFT_MARGIN = 1.0 + 2.0 ** -8
MIN_DENOMINATOR = 2.0 ** -60
DIL_CHUNK = 1024
DIL_HEAD_UNROLL = 4
DIL_PROB_SLOTS = 4
VMEM_LIMIT = 56 * 1024 * 1024


def _bucket_breaks():
    nb = N_REL_BUCKETS // 2
    max_exact = nb // 2
    rel = np.arange(-2 * REL_MAX_DISTANCE, 2 * REL_MAX_DISTANCE + 1)
    n = np.abs(rel)
    pos = np.log(np.maximum(n, 1) / max_exact) / math.log(REL_MAX_DISTANCE / max_exact) * (nb - max_exact)
    large = np.minimum(max_exact + np.floor(pos).astype(np.int64), nb - 1)
    bucket = np.where(rel > 0, nb, 0) + np.where(n < max_exact, n, large)
    breaks = [(int(rel[i]), int(bucket[i])) for i in range(1, len(rel)) if bucket[i] != bucket[i - 1]]
    return int(bucket[0]), breaks


FIRST_BUCKET, BUCKET_BREAKS = _bucket_breaks()
LAST_BUCKET = BUCKET_BREAKS[-1][1]
FAR_DIST = max(-BUCKET_BREAKS[0][0] + 1, BUCKET_BREAKS[-1][0])


def _bias_from_rel(rel, tab_ref, col):
    val = jnp.full(rel.shape, tab_ref[FIRST_BUCKET, col], F32)
    for thr, b in BUCKET_BREAKS:
        val = jnp.where(rel >= thr, tab_ref[b, col], val)
    return val


N_COL_GROUPS = 6
FIRST_DIL_GROUP = 3


def _inproj_kernel(x_ref, g_ref, w_ref, cs_ref, o_ref, qt_ref, vt_ref, *rest, dilations):
    cls_refs = rest[:len(dilations)]
    h_ref, stage_ref = rest[len(dilations):]
    t = pl.program_id(1)
    tm = x_ref.shape[0]
    n_dil = N_COL_GROUPS - FIRST_DIL_GROUP

    @pl.when(t == 0)
    def _():
        x = x_ref[...]
        ms = jnp.mean(x * x, axis=-1, keepdims=True)
        h_ref[...] = (x * lax.rsqrt(ms + NORM_EPS) * g_ref[...]).astype(BF16)

    def matmul():
        acc = jnp.dot(h_ref[...], w_ref[...], preferred_element_type=F32) * cs_ref[...]
        o_ref[...] = acc.astype(o_ref.dtype)
        return acc

    def stage(acc, slot):
        for cb in range(stage_ref.shape[1]):
            stage_ref[slot, cb] = acc[:, cb * LANES:(cb + 1) * LANES]

    def regroup(slot):
        for cb in range(stage_ref.shape[1]):
            sl = slice(cb * LANES, (cb + 1) * LANES)
            for cls_ref, dil in zip(cls_refs, dilations):
                for c in range(dil):
                    cls_ref[c, :, sl] = stage_ref[slot, cb, pl.ds(c, tm // dil, stride=dil), :].astype(cls_ref.dtype)

    for step in range(N_COL_GROUPS):
        @pl.when(t == step)
        def _(step=step):
            acc = matmul()
            if 1 <= step <= n_dil:
                regroup((step - 1) % 2)
            if step < n_dil:
                stage(acc, step % 2)
            if step == n_dil:
                qt_ref[...] = acc.T.astype(qt_ref.dtype)
            if step == n_dil + 2:
                vt_ref[...] = acc.T.astype(vt_ref.dtype)


def _in_proj(x2, gain, w_bf, colscale, dilations, *, tm=512):
    S, D = x2.shape
    N = w_bf.shape[1]
    tn = DIFF_QK_COLS
    assert DIFF_WIDTH == tn and DIL_WIDTH == tn and N == N_COL_GROUPS * tn
    n_dil = N_COL_GROUPS - FIRST_DIL_GROUP
    kern = functools.partial(_inproj_kernel, dilations=dilations)

    def group(t):
        return (t + FIRST_DIL_GROUP) % N_COL_GROUPS

    cls_specs = [pl.BlockSpec((d, tm // d, tn), lambda i, t: (0, i, jnp.clip(t - 1, 0, n_dil - 1)))
                 for d in dilations]
    cls_shapes = [jax.ShapeDtypeStruct((d, S // d, n_dil * tn), BF16) for d in dilations]
    return pl.pallas_call(
        kern,
        grid=(S // tm, N_COL_GROUPS),
        in_specs=[
            pl.BlockSpec((tm, D), lambda i, t: (i, 0)),
            pl.BlockSpec((1, D), lambda i, t: (0, 0)),
            pl.BlockSpec((D, tn), lambda i, t: (0, group(t))),
            pl.BlockSpec((1, tn), lambda i, t: (0, group(t))),
        ],
        out_specs=[
            pl.BlockSpec((tm, tn), lambda i, t: (i, group(t))),
            pl.BlockSpec((tn, tm), lambda i, t: (0, i)),
            pl.BlockSpec((tn, tm), lambda i, t: (0, i)),
        ] + cls_specs,
        out_shape=[
            jax.ShapeDtypeStruct((S, N), BF16),
            jax.ShapeDtypeStruct((tn, S), BF16),
            jax.ShapeDtypeStruct((tn, S), BF16),
        ] + cls_shapes,
        scratch_shapes=[pltpu.VMEM((tm, D), BF16), pltpu.VMEM((2, tn // LANES, tm, LANES), F32)],
        compiler_params=pltpu.CompilerParams(
            dimension_semantics=("arbitrary", "arbitrary"), vmem_limit_bytes=VMEM_LIMIT),
        name="in_proj",
    )(x2, gain, w_bf, colscale)


def _diff_attn_kernel(*refs, T, QP, n_near, n_tiles, n_cast):
    tab_ref, q1t_ref, q2t_ref, k1_ref, k2_ref, vt_ref, lam_ref, gain_ref = refs[:8]
    cast_in, o_ref, cast_out = refs[8:8 + n_cast], refs[8 + n_cast], refs[9 + n_cast:9 + 2 * n_cast]
    bias_ref, m_ref, l_ref, acc_ref, p_ref, shift_ref, knorm_ref, prev_ref = refs[9 + 2 * n_cast:]
    _diff_attn_body(tab_ref, q1t_ref, q2t_ref, k1_ref, k2_ref, vt_ref, lam_ref, gain_ref, o_ref,
                    bias_ref, m_ref, l_ref, acc_ref, p_ref, shift_ref, knorm_ref, prev_ref,
                    T=T, QP=QP, n_near=n_near, n_tiles=n_tiles)
    for src, dst in zip(cast_in, cast_out):
        dst[...] = src[...].astype(dst.dtype)


def _diff_attn_body(tab_ref, q1t_ref, q2t_ref, k1_ref, k2_ref, vt_ref, lam_ref, gain_ref, o_ref,
                    bias_ref, m_ref, l_ref, acc_ref, p_ref, shift_ref, knorm_ref, prev_ref,
                    *, T, QP, n_near, n_tiles):
    h = pl.program_id(0)
    qi = pl.program_id(1)
    n_chains = 2 * (T // QP)
    qts = (q1t_ref, q2t_ref)
    ks = (k1_ref, k2_ref)

    def chain_of(i):
        qp = i // 2
        return i % 2, slice(qp * QP, (qp + 1) * QP)

    @pl.when(qi == 0)
    def _():
        for mi in range(2):
            def knorm(t, best, mi=mi):
                k = ks[mi][pl.ds(pl.multiple_of(t * T, T), T), :].astype(F32)
                return jnp.maximum(best, jnp.max(jnp.sum(k * k, axis=1, keepdims=True), axis=0, keepdims=True))
            knorm_ref[mi] = lax.fori_loop(0, n_tiles, knorm, jnp.zeros((1, 1), F32))

        x = lax.broadcasted_iota(jnp.int32, (F32_SUBLANES, 2 * T), 1)
        x = jnp.where(x < T, x, x - 2 * T)
        for di, d in enumerate(range(-n_near, n_near + 1)):
            g = _bias_from_rel(d * T - x, tab_ref, h)
            base = jnp.broadcast_to(g[0:1, :], (LANES, 2 * T))
            for rb in range(T // LANES):
                blk = pltpu.roll(base, rb * LANES, 1, stride=1, stride_axis=0)
                bias_ref[di, rb * LANES:(rb + 1) * LANES, :] = blk[:, :T]

    c_left = tab_ref[FIRST_BUCKET, h]
    c_right = tab_ref[LAST_BUCKET, h]

    bias_max = tab_ref[0, h]
    for b in range(1, N_REL_BUCKETS):
        bias_max = jnp.maximum(bias_max, tab_ref[b, h])

    for mi in range(2):
        q = qts[mi][...].astype(F32)
        qnorm2 = jnp.sum(q * q, axis=0, keepdims=True)
        shift_ref[mi] = jnp.sqrt(qnorm2 * knorm_ref[mi]) * SHIFT_MARGIN + bias_max
    l_ref[...] = jnp.zeros(l_ref.shape, F32)
    acc_ref[...] = jnp.zeros(acc_ref.shape, F32)
    p_ref[n_chains - 1] = jnp.zeros((T, QP), BF16)
    prev_ref[0] = 0

    def score_stage(i, k0, bias_di, bias_const):
        mi, qs = chain_of(i)
        s = jnp.dot(ks[mi][pl.ds(k0, T), :], qts[mi][:, qs], preferred_element_type=F32)
        if bias_di is not None:
            p = jnp.exp2(s + bias_ref[bias_di, :, qs] - shift_ref[mi, :, qs])
        else:
            p = jnp.exp2(s - (shift_ref[mi, :, qs] - bias_const))
        l_ref[mi, :, qs] += jnp.sum(p, axis=0, keepdims=True)
        p_ref[i] = p.astype(BF16)

    def value_stage(i, k0):
        mi, qs = chain_of(i)
        acc_ref[mi, :, qs] += jnp.dot(vt_ref[:, pl.ds(k0, T)], p_ref[i], preferred_element_type=F32)

    def tile(kt, bias_di, bias_const, prev_kt=None):
        k0 = pl.multiple_of(kt * T, T)
        pk0 = pl.multiple_of((prev_ref[0] if prev_kt is None else prev_kt) * T, T)
        for i in range(n_chains):
            value_stage((i - 1) % n_chains, pk0 if i < 1 else k0)
            score_stage(i, k0, bias_di, bias_const)
        prev_ref[0] = kt

    lo = jnp.maximum(qi - n_near, 0)
    hi = jnp.minimum(qi + n_near + 1, n_tiles)

    for di, d in enumerate(range(-n_near, n_near + 1)):
        kt = qi + d

        @pl.when(jnp.logical_and(kt >= 0, kt < n_tiles))
        def _(di=di, kt=kt):
            tile(kt, di, None)

    n_far = n_tiles - (hi - lo)

    def far_run(first, length):
        prev_kt = None
        for j in range(length):
            before = first + j < lo
            kt = jnp.where(before, first + j, first + j + (hi - lo))
            tile(kt, None, jnp.where(before, c_left, c_right), prev_kt=prev_kt)
            prev_kt = kt

    done = 0
    run = 1
    while run < FAR_UNROLL:
        has_run = jnp.bitwise_and(n_far, run)

        @pl.when(has_run != 0)
        def _(done=done, run=run):
            far_run(done, run)

        done = done + has_run
        run *= 2

    @pl.loop(0, n_far // FAR_UNROLL)
    def _(trip, done=done):
        far_run(done + trip * FAR_UNROLL, FAR_UNROLL)

    value_stage(n_chains - 1, pl.multiple_of(prev_ref[0] * T, T))

    @pl.when(jnp.min(l_ref[...]) < MIN_DENOMINATOR)
    def _():
        m_ref[...] = jnp.full(m_ref.shape, -jnp.inf, F32)
        l_ref[...] = jnp.zeros(l_ref.shape, F32)
        acc_ref[...] = jnp.zeros(acc_ref.shape, F32)

        @pl.loop(0, n_tiles)
        def _(kt):
            k0 = pl.multiple_of(kt * T, T)
            d = kt - qi
            near = jnp.abs(d) <= n_near
            di = jnp.clip(d + n_near, 0, 2 * n_near)
            c_far = jnp.where(d < 0, c_left, c_right)
            for i in range(n_chains):
                mi, qs = chain_of(i)
                s = jnp.dot(ks[mi][pl.ds(k0, T), :], qts[mi][:, qs], preferred_element_type=F32)
                s = s + jnp.where(near, bias_ref[di, :, qs], c_far)
                m = m_ref[mi, :, qs]
                m_new = jnp.maximum(m, jnp.max(s, axis=0, keepdims=True))
                alpha = jnp.exp2(m - m_new)
                p = jnp.exp2(s - m_new)
                m_ref[mi, :, qs] = m_new
                l_ref[mi, :, qs] = alpha * l_ref[mi, :, qs] + jnp.sum(p, axis=0, keepdims=True)
                pv = jnp.dot(vt_ref[:, pl.ds(k0, T)], p.astype(BF16), preferred_element_type=F32)
                acc_ref[mi, :, qs] = alpha * acc_ref[mi, :, qs] + pv

    lam = lam_ref[0, 0]
    o = acc_ref[0] / l_ref[0] - lam * (acc_ref[1] / l_ref[1])
    ms = jnp.mean(o * o, axis=0, keepdims=True)
    o = o * lax.rsqrt(ms + SUBLN_EPS) * (gain_ref[...] * (1.0 - LAM_INIT))
    o_ref[...] = o.T.astype(o_ref.dtype)


def _lambda_kernel(q1_ref, k1_ref, q2_ref, k2_ref, o_ref):
    a = jnp.sum(q1_ref[...] * k1_ref[...], axis=-1, keepdims=True)
    b = jnp.sum(q2_ref[...] * k2_ref[...], axis=-1, keepdims=True)
    o_ref[...] = jnp.exp(a) - jnp.exp(b) + LAM_INIT


def _lambda(lq1, lk1, lq2, lk2):
    return pl.pallas_call(
        _lambda_kernel, out_shape=jax.ShapeDtypeStruct((1, 1), F32), name="diff_lambda",
    )(lq1, lk1, lq2, lk2)


def _diff_attention(tab_diff, proj, qt_all, vt_all, lam, gain_col, f32_weights):
    S = proj.shape[0]
    T = DIFF_TILE
    n_tiles = S // T
    n_near = -(-(FAR_DIST - 1) // T)
    kern = functools.partial(_diff_attn_kernel, T=T, QP=DIFF_QUERY_PANEL, n_near=n_near, n_tiles=n_tiles,
                             n_cast=len(f32_weights))

    n_steps = N_DIFF_HEADS * n_tiles
    cast_specs = []
    for w in f32_weights:
        rows, cols = w.shape
        every = next(e for e in (1, 2, 4, 8) if rows % (n_steps // e) == 0
                     and (rows // (n_steps // e)) % BF16_SUBLANES == 0)
        cast_specs.append(pl.BlockSpec((rows // (n_steps // every), cols),
                                       lambda h, i, every=every: ((h * n_tiles + i) // every, 0)))
    k_block0 = DIFF_QK_COLS // HEAD_DIM
    smem = pl.BlockSpec(memory_space=pltpu.SMEM)
    return pl.pallas_call(
        kern,
        grid=(N_DIFF_HEADS, n_tiles),
        in_specs=[
            smem,
            pl.BlockSpec((HEAD_DIM, T), lambda h, i: (2 * h, i)),
            pl.BlockSpec((HEAD_DIM, T), lambda h, i: (2 * h + 1, i)),
            pl.BlockSpec((S, HEAD_DIM), lambda h, i: (0, k_block0 + 2 * h), pipeline_mode=pl.Buffered(1)),
            pl.BlockSpec((S, HEAD_DIM), lambda h, i: (0, k_block0 + 2 * h + 1), pipeline_mode=pl.Buffered(1)),
            pl.BlockSpec((DIFF_V_DIM, S), lambda h, i: (h, 0), pipeline_mode=pl.Buffered(1)),
            smem,
            pl.BlockSpec((DIFF_V_DIM, 1), lambda h, i: (0, 0)),
        ] + cast_specs,
        out_specs=[pl.BlockSpec((T, DIFF_V_DIM), lambda h, i: (i, h))] + cast_specs,
        out_shape=[jax.ShapeDtypeStruct((S, DIFF_WIDTH), BF16)]
        + [jax.ShapeDtypeStruct(w.shape, BF16) for w in f32_weights],
        scratch_shapes=[pltpu.VMEM((2 * n_near + 1, T, T), F32),
                        pltpu.VMEM((2, 1, T), F32), pltpu.VMEM((2, 1, T), F32),
                        pltpu.VMEM((2, DIFF_V_DIM, T), F32),
                        pltpu.VMEM((2 * (T // DIFF_QUERY_PANEL), T, DIFF_QUERY_PANEL), BF16),
                        pltpu.VMEM((2, 1, T), F32), pltpu.VMEM((2, 1, 1), F32),
                        pltpu.SMEM((1,), jnp.int32)],
        compiler_params=pltpu.CompilerParams(
            dimension_semantics=("arbitrary", "arbitrary"), vmem_limit_bytes=VMEM_LIMIT),
        name="diff_attn",
    )(tab_diff, qt_all, qt_all, proj, proj, vt_all, lam, gain_col, *f32_weights)


def _dilated_kernel(tab_ref, q_ref, kp_ref, km_ref, kn_ref, vp_ref, vm_ref, vn_ref, o_ref, lse_ref,
                    bias_ref, kx_ref, vx_ref, *, R, B, half, dilation, n_chunks):
    c = pl.program_id(0)
    n = pl.program_id(1)
    W = B + 2 * half
    nblk = R // B

    @pl.when(jnp.logical_and(c == 0, n == 0))
    def _():
        rows = F32_SUBLANES
        col = lax.broadcasted_iota(jnp.int32, (rows, W), 1)
        row = lax.broadcasted_iota(jnp.int32, (rows, W), 0)
        for hh in range(N_DIL_HEADS):
            def fill(r, carry, hh=hh):
                r0 = pl.multiple_of(r * rows, rows)
                off = col - half - (row + r0)
                bias = _bias_from_rel(off * dilation, tab_ref, N_DIFF_HEADS + hh)
                base = jnp.where(jnp.abs(off) <= half, bias, NEG_INF * LOG2E)
                bias_ref[hh, 1, pl.ds(r0, rows), :] = base
                bias_ref[hh, 0, pl.ds(r0, rows), :] = jnp.where(col >= half, base, NEG_INF * LOG2E)
                bias_ref[hh, 2, pl.ds(r0, rows), :] = jnp.where(col < B + half, base, NEG_INF * LOG2E)
                return carry
            lax.fori_loop(0, B // rows, fill, 0)

    kx_ref[0:half, :] = kp_ref[...]
    kx_ref[half:half + R, :] = km_ref[...]
    kx_ref[half + R:, :] = kn_ref[...]
    vx_ref[0:half, :] = vp_ref[...]
    vx_ref[half:half + R, :] = vm_ref[...]
    vx_ref[half + R:, :] = vn_ref[...]

    def chain(hh, b):
        c0 = pl.multiple_of(hh * HEAD_DIM, HEAD_DIM)
        r0 = b * B
        var = 1
        if b == 0:
            var = jnp.where(n == 0, 0, var)
        if b == nblk - 1:
            var = jnp.where(n == n_chunks - 1, 2, var)
        q = q_ref[pl.ds(r0, B), pl.ds(c0, HEAD_DIM)]
        k = kx_ref[pl.ds(r0, W), pl.ds(c0, HEAD_DIM)]
        v = vx_ref[pl.ds(r0, W), pl.ds(c0, HEAD_DIM)]
        s = lax.dot_general(q, k, (((1,), (1,)), ((), ())), preferred_element_type=F32)
        s = s + bias_ref[hh, var]
        m = jnp.max(s, axis=-1, keepdims=True)
        e = jnp.exp2(s - m)
        den = jnp.sum(e, axis=-1, keepdims=True)
        o = jnp.dot(e.astype(BF16), v, preferred_element_type=F32) / den
        o_ref[pl.ds(r0, B), pl.ds(c0, HEAD_DIM)] = o.astype(o_ref.dtype)
        lse = m + jnp.log2(den)
        lse_ref[pl.ds(r0, B), pl.ds(c0, HEAD_DIM)] = jnp.broadcast_to(lse, (B, HEAD_DIM))

    @pl.loop(0, N_DIL_HEADS // DIL_HEAD_UNROLL)
    def _(hg):
        for u in range(DIL_HEAD_UNROLL):
            for b in range(nblk):
                chain(hg * DIL_HEAD_UNROLL + u, b)


def _dilated_pattern(tab, qkv, col_blk0, window, dilation, *, B=256):
    _, L, _ = qkv.shape
    R = min(DIL_CHUNK, L)
    half = window // (2 * dilation)
    assert L % R == 0 and R % B == 0 and half % BF16_SUBLANES == 0 and R % half == 0
    n_chunks = L // R
    q_blk, k_blk, v_blk = col_blk0, col_blk0 + 1, col_blk0 + 2
    hb = R // half
    n_hblk = L // half

    def main(blk):
        return pl.BlockSpec((None, R, DIL_WIDTH), lambda c, n: (c, n, blk))

    def prev(blk):
        return pl.BlockSpec((None, half, DIL_WIDTH), lambda c, n: (c, jnp.maximum(n * hb - 1, 0), blk))

    def nxt(blk):
        return pl.BlockSpec((None, half, DIL_WIDTH), lambda c, n: (c, jnp.minimum((n + 1) * hb, n_hblk - 1), blk))

    kern = functools.partial(_dilated_kernel, R=R, B=B, half=half, dilation=dilation, n_chunks=n_chunks)
    out_spec = pl.BlockSpec((None, R, DIL_WIDTH), lambda c, n: (c, n, 0))
    return pl.pallas_call(
        kern,
        grid=(dilation, n_chunks),
        in_specs=[pl.BlockSpec(memory_space=pltpu.SMEM),
                  main(q_blk), prev(k_blk), main(k_blk), nxt(k_blk), prev(v_blk), main(v_blk), nxt(v_blk)],
        out_specs=[out_spec, out_spec],
        out_shape=[jax.ShapeDtypeStruct((dilation, L, DIL_WIDTH), BF16),
                   jax.ShapeDtypeStruct((dilation, L, DIL_WIDTH), F32)],
        scratch_shapes=[pltpu.VMEM((N_DIL_HEADS, 3, B, B + 2 * half), F32),
                        pltpu.VMEM((R + 2 * half, DIL_WIDTH), BF16),
                        pltpu.VMEM((R + 2 * half, DIL_WIDTH), BF16)],
        compiler_params=pltpu.CompilerParams(
            dimension_semantics=("arbitrary", "arbitrary"), vmem_limit_bytes=VMEM_LIMIT),
        name=f"dilated_d{dilation}",
    )(tab, qkv, qkv, qkv, qkv, qkv, qkv, qkv)


def _combine_kernel(*refs, dilations):
    n = len(dilations)
    o_refs, l_refs = refs[:n], refs[n:2 * n]
    g_ref, out_ref = refs[2 * n], refs[2 * n + 1]
    scratch = refs[2 * n + 2:]
    tm = out_ref.shape[0]

    for hh in range(N_DIL_HEADS):
        sl = slice(hh * HEAD_DIM, (hh + 1) * HEAD_DIM)
        outs, lses = [], []
        si = 0
        for o_ref, l_ref, dil in zip(o_refs, l_refs, dilations):
            if dil == 1:
                outs.append(o_ref[0, :, sl].astype(F32))
                lses.append(l_ref[0, :, sl])
                continue
            os_ref, ls_ref = scratch[si], scratch[si + 1]
            si += 2
            for c in range(dil):
                os_ref[hh, pl.ds(c, tm // dil, stride=dil), :] = o_ref[c, :, sl].astype(F32)
                ls_ref[hh, pl.ds(c, tm // dil, stride=dil), :] = l_ref[c, :, sl]
            outs.append(os_ref[hh])
            lses.append(ls_ref[hh])

        m = functools.reduce(jnp.maximum, lses)
        ws = [jnp.exp2(l - m) for l in lses]
        tot = functools.reduce(lambda a, b: a + b, ws)
        oh = functools.reduce(lambda a, b: a + b, [(w / tot) * op for w, op in zip(ws, outs)])
        ms = jnp.mean(oh * oh, axis=-1, keepdims=True)
        out_ref[:, sl] = (oh * lax.rsqrt(ms + NORM_EPS) * g_ref[:, sl]).astype(out_ref.dtype)


def _combine(outs, lses, gain, dilations, *, tm=512):
    S = outs[0].shape[0] * outs[0].shape[1]
    specs = [pl.BlockSpec((d, tm // d, DIL_WIDTH), lambda i: (0, i, 0)) for d in dilations]
    n_scr = sum(1 for d in dilations if d != 1)
    return pl.pallas_call(
        functools.partial(_combine_kernel, dilations=dilations),
        grid=(S // tm,),
        in_specs=specs + specs + [pl.BlockSpec((1, DIL_WIDTH), lambda i: (0, 0))],
        scratch_shapes=[pltpu.VMEM((N_DIL_HEADS, tm, HEAD_DIM), F32)] * (2 * n_scr),
        out_specs=pl.BlockSpec((tm, DIL_WIDTH), lambda i: (i, 0)),
        out_shape=jax.ShapeDtypeStruct((S, DIL_WIDTH), BF16),
        compiler_params=pltpu.CompilerParams(
            dimension_semantics=("arbitrary",), vmem_limit_bytes=VMEM_LIMIT),
        name="dilated_combine",
    )(*outs, *lses, gain)


def _key_norm2_kernel(k_ref, o_ref):
    @pl.when(pl.program_id(0) == 0)
    def _():
        o_ref[...] = jnp.zeros(o_ref.shape, F32)

    k = k_ref[...].astype(F32)
    for hh in range(N_DIL_HEADS):
        kh = k[:, hh * HEAD_DIM:(hh + 1) * HEAD_DIM]
        n2 = jnp.max(jnp.sum(kh * kh, axis=1, keepdims=True), axis=0, keepdims=True)
        o_ref[hh:hh + 1, :] = jnp.maximum(o_ref[hh:hh + 1, :], jnp.broadcast_to(n2, (1, LANES)))


def _key_norm2(proj, k_blk, *, tm=1024):
    S = proj.shape[0]
    return pl.pallas_call(
        _key_norm2_kernel,
        grid=(S // tm,),
        in_specs=[pl.BlockSpec((tm, DIL_WIDTH), lambda i: (i, k_blk))],
        out_specs=pl.BlockSpec((N_DIL_HEADS, LANES), lambda i: (0, 0)),
        out_shape=jax.ShapeDtypeStruct((N_DIL_HEADS, LANES), F32),
        compiler_params=pltpu.CompilerParams(dimension_semantics=("arbitrary",), vmem_limit_bytes=VMEM_LIMIT),
        name="dilated_key_norm",
    )(proj)


def _dilated_fast_kernel(tab_ref, kn2_ref, q_ref, kp_ref, km_ref, kn_ref, vp_ref, vm_ref, vn_ref, num_ref, den_ref,
                         bias_ref, kx_ref, vx_ref, p_ref, *, R, B, half, dilation, n_chunks):
    c = pl.program_id(0)
    n = pl.program_id(1)
    W = B + 2 * half
    nblk = R // B
    VW = 2 * HEAD_DIM

    @pl.when(jnp.logical_and(c == 0, n == 0))
    def _():
        rows = F32_SUBLANES
        col = lax.broadcasted_iota(jnp.int32, (rows, W), 1)
        row = lax.broadcasted_iota(jnp.int32, (rows, W), 0)
        for hh in range(N_DIL_HEADS):
            def fill(r, carry, hh=hh):
                r0 = pl.multiple_of(r * rows, rows)
                off = col - half - (row + r0)
                bias = _bias_from_rel(off * dilation, tab_ref, N_DIFF_HEADS + hh)
                base = jnp.where(jnp.abs(off) <= half, bias, NEG_INF * LOG2E)
                bias_ref[hh, 1, pl.ds(r0, rows), :] = base
                bias_ref[hh, 0, pl.ds(r0, rows), :] = jnp.where(col >= half, base, NEG_INF * LOG2E)
                bias_ref[hh, 2, pl.ds(r0, rows), :] = jnp.where(col < B + half, base, NEG_INF * LOG2E)
                return carry
            lax.fori_loop(0, B // rows, fill, 0)
        vx_ref[...] = jnp.ones(vx_ref.shape, BF16)

    kx_ref[0:half, :] = kp_ref[...]
    kx_ref[half:half + R, :] = km_ref[...]
    kx_ref[half + R:, :] = kn_ref[...]
    for hh in range(N_DIL_HEADS):
        src = slice(hh * HEAD_DIM, (hh + 1) * HEAD_DIM)
        dst = slice(hh * VW, hh * VW + HEAD_DIM)
        vx_ref[0:half, dst] = vp_ref[:, src]
        vx_ref[half:half + R, dst] = vm_ref[:, src]
        vx_ref[half + R:, dst] = vn_ref[:, src]

    den_ref[...] = jnp.zeros(den_ref.shape, F32)
    n_slots = p_ref.shape[0]
    bias_max = []
    for hh in range(N_DIL_HEADS):
        bm = tab_ref[0, N_DIFF_HEADS + hh]
        for bk in range(1, N_REL_BUCKETS):
            bm = jnp.maximum(bm, tab_ref[bk, N_DIFF_HEADS + hh])
        bias_max.append(bm)

    def score_stage(hh, b, slot):
        cs = slice(hh * HEAD_DIM, (hh + 1) * HEAD_DIM)
        r0 = b * B
        var = 1
        if b == 0:
            var = jnp.where(n == 0, 0, var)
        if b == nblk - 1:
            var = jnp.where(n == n_chunks - 1, 2, var)
        q = q_ref[r0:r0 + B, cs]
        qf = q.astype(F32)
        qn2 = jnp.sum(qf * qf, axis=1, keepdims=True)
        shift = jnp.sqrt(qn2 * kn2_ref[hh:hh + 1, 0:1]) * SHIFT_MARGIN + bias_max[hh]
        s = lax.dot_general(q, kx_ref[r0:r0 + W, cs], (((1,), (1,)), ((), ())), preferred_element_type=F32)
        p_ref[slot] = jnp.exp2(s + bias_ref[hh, var] - shift).astype(BF16)

    def value_stage(hh, b, slot):
        cs = slice(hh * HEAD_DIM, (hh + 1) * HEAD_DIM)
        r0 = b * B
        nd = jnp.dot(p_ref[slot], vx_ref[r0:r0 + W, hh * VW:(hh + 1) * VW], preferred_element_type=F32)
        num_ref[r0:r0 + B, cs] = nd[:, :HEAD_DIM].astype(num_ref.dtype)
        den_ref[r0:r0 + B, hh:hh + 1] = nd[:, HEAD_DIM + hh:HEAD_DIM + hh + 1]

    chains = [(hh, b) for hh in range(N_DIL_HEADS) for b in range(nblk)]
    for i, (hh, b) in enumerate(chains):
        if i > 0:
            value_stage(*chains[i - 1], (i - 1) % n_slots)
        score_stage(hh, b, i % n_slots)
    value_stage(*chains[-1], (len(chains) - 1) % n_slots)


def _dilated_fast(tab, kn2, qkv, col_blk0, window, dilation, *, B=128):
    _, L, _ = qkv.shape
    R = min(DIL_CHUNK, L)
    half = window // (2 * dilation)
    assert L % R == 0 and R % B == 0 and half % BF16_SUBLANES == 0 and R % half == 0
    n_chunks = L // R
    q_blk, k_blk, v_blk = col_blk0, col_blk0 + 1, col_blk0 + 2
    hb = R // half
    n_hblk = L // half

    def main(blk):
        return pl.BlockSpec((None, R, DIL_WIDTH), lambda c, n: (c, n, blk))

    def prev(blk):
        return pl.BlockSpec((None, half, DIL_WIDTH), lambda c, n: (c, jnp.maximum(n * hb - 1, 0), blk))

    def nxt(blk):
        return pl.BlockSpec((None, half, DIL_WIDTH), lambda c, n: (c, jnp.minimum((n + 1) * hb, n_hblk - 1), blk))

    kern = functools.partial(_dilated_fast_kernel, R=R, B=B, half=half, dilation=dilation, n_chunks=n_chunks)
    out_spec = pl.BlockSpec((None, R, DIL_WIDTH), lambda c, n: (c, n, 0))
    return pl.pallas_call(
        kern,
        grid=(dilation, n_chunks),
        in_specs=[pl.BlockSpec(memory_space=pltpu.SMEM),
                  pl.BlockSpec((N_DIL_HEADS, LANES), lambda c, n: (0, 0)),
                  main(q_blk), prev(k_blk), main(k_blk), nxt(k_blk), prev(v_blk), main(v_blk), nxt(v_blk)],
        out_specs=[out_spec, pl.BlockSpec((None, R, LANES), lambda c, n: (c, n, 0))],
        out_shape=[jax.ShapeDtypeStruct((dilation, L, DIL_WIDTH), BF16),
                   jax.ShapeDtypeStruct((dilation, L, LANES), F32)],
        scratch_shapes=[pltpu.VMEM((N_DIL_HEADS, 3, B, B + 2 * half), F32),
                        pltpu.VMEM((R + 2 * half, DIL_WIDTH), BF16),
                        pltpu.VMEM((R + 2 * half, 2 * DIL_WIDTH), BF16),
                        pltpu.VMEM((DIL_PROB_SLOTS, B, B + 2 * half), BF16)],
        compiler_params=pltpu.CompilerParams(
            dimension_semantics=("arbitrary", "arbitrary"), vmem_limit_bytes=VMEM_LIMIT),
        name=f"dilated_fast_d{dilation}",
    )(tab, kn2, qkv, qkv, qkv, qkv, qkv, qkv, qkv)


def _combine_fast_kernel(*refs, dilations):
    n = len(dilations)
    n_refs, d_refs = refs[:n], refs[n:2 * n]
    g_ref, out_ref, dmin_ref = refs[2 * n], refs[2 * n + 1], refs[2 * n + 2]
    scratch = refs[2 * n + 3:]
    tm = out_ref.shape[0]

    den_all = None
    si = 0
    for d_ref, dil in zip(d_refs, dilations):
        if dil == 1:
            den_p = d_ref[0]
        else:
            ds_ref = scratch[si + 1]
            for c in range(dil):
                ds_ref[pl.ds(c, tm // dil, stride=dil), :] = d_ref[c]
            den_p = ds_ref[...]
            si += 2
        den_all = den_p if den_all is None else den_all + den_p

    for hh in range(N_DIL_HEADS):
        sl = slice(hh * HEAD_DIM, (hh + 1) * HEAD_DIM)
        nums = []
        si = 0
        for n_ref, dil in zip(n_refs, dilations):
            if dil == 1:
                nums.append(n_ref[0, :, sl].astype(F32))
                continue
            ns_ref = scratch[si]
            si += 2
            for c in range(dil):
                ns_ref[hh, pl.ds(c, tm // dil, stride=dil), :] = n_ref[c, :, sl].astype(F32)
            nums.append(ns_ref[hh])

        den = den_all[:, hh:hh + 1]
        oh = functools.reduce(lambda a, b: a + b, nums) / den
        ms = jnp.mean(oh * oh, axis=-1, keepdims=True)
        out_ref[:, sl] = (oh * lax.rsqrt(ms + NORM_EPS) * g_ref[:, sl]).astype(out_ref.dtype)
        dmin_ref[0, hh:hh + 1, :] = jnp.broadcast_to(jnp.min(den, axis=0, keepdims=True), (1, LANES))


def _combine_fast(nums, dens, gain, dilations, *, tm=512):
    S = nums[0].shape[0] * nums[0].shape[1]
    num_specs = [pl.BlockSpec((d, tm // d, DIL_WIDTH), lambda i: (0, i, 0)) for d in dilations]
    den_specs = [pl.BlockSpec((d, tm // d, LANES), lambda i: (0, i, 0)) for d in dilations]
    n_scr = sum(1 for d in dilations if d != 1)
    return pl.pallas_call(
        functools.partial(_combine_fast_kernel, dilations=dilations),
        grid=(S // tm,),
        in_specs=num_specs + den_specs + [pl.BlockSpec((1, DIL_WIDTH), lambda i: (0, 0))],
        scratch_shapes=[pltpu.VMEM((N_DIL_HEADS, tm, HEAD_DIM), F32), pltpu.VMEM((tm, LANES), F32)] * n_scr,
        out_specs=[pl.BlockSpec((tm, DIL_WIDTH), lambda i: (i, 0)),
                   pl.BlockSpec((1, N_DIL_HEADS, LANES), lambda i: (i, 0, 0))],
        out_shape=[jax.ShapeDtypeStruct((S, DIL_WIDTH), BF16),
                   jax.ShapeDtypeStruct((S // tm, N_DIL_HEADS, LANES), F32)],
        compiler_params=pltpu.CompilerParams(
            dimension_semantics=("arbitrary",), vmem_limit_bytes=VMEM_LIMIT),
        name="dilated_combine_fast",
    )(*nums, *dens, gain)


def _outproj_kernel(od_ref, ol_ref, wd_ref, wl_ref, x_ref, g_ref, x1_ref, h2_ref):
    acc = jnp.dot(od_ref[...], wd_ref[...], preferred_element_type=F32)
    acc = acc + jnp.dot(ol_ref[...], wl_ref[...], preferred_element_type=F32)
    x1 = x_ref[...] + acc
    x1_ref[...] = x1
    ms = jnp.mean(x1 * x1, axis=-1, keepdims=True)
    h2_ref[...] = (x1 * lax.rsqrt(ms + NORM_EPS) * g_ref[...]).astype(h2_ref.dtype)


def _out_proj(o_d, o_l, w_bf, x2, gain, *, tm=512):
    S, D = x2.shape
    return pl.pallas_call(
        _outproj_kernel,
        grid=(S // tm,),
        in_specs=[
            pl.BlockSpec((tm, DIFF_WIDTH), lambda i: (i, 0)),
            pl.BlockSpec((tm, DIL_WIDTH), lambda i: (i, 0)),
            pl.BlockSpec((DIFF_WIDTH, D), lambda i: (0, 0)),
            pl.BlockSpec((DIL_WIDTH, D), lambda i: (1, 0)),
            pl.BlockSpec((tm, D), lambda i: (i, 0)),
            pl.BlockSpec((1, D), lambda i: (0, 0)),
        ],
        out_specs=[pl.BlockSpec((tm, D), lambda i: (i, 0)), pl.BlockSpec((tm, D), lambda i: (i, 0))],
        out_shape=[jax.ShapeDtypeStruct((S, D), F32), jax.ShapeDtypeStruct((S, D), BF16)],
        compiler_params=pltpu.CompilerParams(
            dimension_semantics=("arbitrary",), vmem_limit_bytes=VMEM_LIMIT),
        name="out_proj",
    )(o_d, o_l, w_bf, w_bf, x2, gain)


def _ffn_up_kernel(hm_ref, hp_ref, hn_ref, wg_ref, wu_ref, cw_ref, cb_ref, o_ref, lhs_ref, *, tm, n_row_tiles):
    i = pl.program_id(0)
    j = pl.program_id(1)
    halo = BF16_SUBLANES

    @pl.when(j == 0)
    def _():
        lhs_ref[0:halo, :] = jnp.where(i == 0, jnp.zeros_like(hp_ref[...]), hp_ref[...])
        lhs_ref[halo:halo + tm, :] = hm_ref[...]
        lhs_ref[halo + tm:, :] = jnp.where(i == n_row_tiles - 1, jnp.zeros_like(hn_ref[...]), hn_ref[...])

    g = jnp.dot(lhs_ref[...], wg_ref[...], preferred_element_type=F32)
    u = jnp.dot(lhs_ref[halo:halo + tm, :], wu_ref[...], preferred_element_type=F32)
    rows = tm + 2 * halo
    g_prev = pltpu.roll(g, 1, axis=0)
    g_next = pltpu.roll(g, rows - 1, axis=0)
    y = cw_ref[0:1, :] * g_prev + cw_ref[1:2, :] * g + cw_ref[2:3, :] * g_next + cb_ref[...]
    y = y[halo:halo + tm, :]
    act = y * (1.0 / (1.0 + jnp.exp(-y))) * u
    o_ref[...] = act.astype(o_ref.dtype)


def _ffn_up(h2, w_bf, conv_w, conv_b, *, tm=1024, tn=512):
    S, D = h2.shape
    d_ff = conv_w.shape[1]
    assert d_ff % tn == 0
    nj = d_ff // tn
    ni = S // tm
    hb = tm // BF16_SUBLANES
    n_hblk = S // BF16_SUBLANES
    kern = functools.partial(_ffn_up_kernel, tm=tm, n_row_tiles=ni)
    return pl.pallas_call(
        kern,
        grid=(ni, nj),
        in_specs=[
            pl.BlockSpec((tm, D), lambda i, j: (i, 0)),
            pl.BlockSpec((BF16_SUBLANES, D), lambda i, j: (jnp.maximum(i * hb - 1, 0), 0)),
            pl.BlockSpec((BF16_SUBLANES, D), lambda i, j: (jnp.minimum((i + 1) * hb, n_hblk - 1), 0)),
            pl.BlockSpec((D, tn), lambda i, j: (0, j)),
            pl.BlockSpec((D, tn), lambda i, j: (0, nj + j)),
            pl.BlockSpec((3, tn), lambda i, j: (0, j)),
            pl.BlockSpec((1, tn), lambda i, j: (0, j)),
        ],
        out_specs=pl.BlockSpec((tm, tn), lambda i, j: (i, j)),
        out_shape=jax.ShapeDtypeStruct((S, d_ff), BF16),
        scratch_shapes=[pltpu.VMEM((tm + 2 * BF16_SUBLANES, D), BF16)],
        compiler_params=pltpu.CompilerParams(
            dimension_semantics=("arbitrary", "arbitrary"), vmem_limit_bytes=VMEM_LIMIT),
        name="ffn_up",
    )(h2, h2, h2, w_bf, w_bf, conv_w, conv_b)


def _ffn_down_kernel(a_ref, w_ref, x1_ref, g_ref, o_ref, *, n_k):
    k = pl.program_id(1)

    @pl.when(k == 0)
    def _():
        o_ref[...] = x1_ref[...]

    o_ref[...] += jnp.dot(a_ref[...], w_ref[...], preferred_element_type=F32)

    @pl.when(k == n_k - 1)
    def _():
        y = o_ref[...]
        ms = jnp.mean(y * y, axis=-1, keepdims=True)
        o_ref[...] = y * lax.rsqrt(ms + NORM_EPS) * g_ref[...]


def _ffn_down(act, w_bf, x1, gain, *, tm=1024, tk=1408):
    S, d_ff = act.shape
    D = x1.shape[1]
    n_k = d_ff // tk
    kern = functools.partial(_ffn_down_kernel, n_k=n_k)
    return pl.pallas_call(
        kern,
        grid=(S // tm, n_k),
        in_specs=[
            pl.BlockSpec((tm, tk), lambda i, k: (i, k)),
            pl.BlockSpec((tk, D), lambda i, k: (k, 0)),
            pl.BlockSpec((tm, D), lambda i, k: (i, 0)),
            pl.BlockSpec((1, D), lambda i, k: (0, 0)),
        ],
        out_specs=pl.BlockSpec((tm, D), lambda i, k: (i, 0)),
        out_shape=jax.ShapeDtypeStruct((S, D), F32),
        compiler_params=pltpu.CompilerParams(
            dimension_semantics=("arbitrary", "arbitrary"), vmem_limit_bytes=VMEM_LIMIT),
        name="ffn_down",
    )(act, w_bf, x1, gain)


def kernel(x, norm1_gain, w_in, rel_bias_table, lambda_q1, lambda_k1, lambda_q2, lambda_k2,
           diff_subln_gain, dil_out_gain, w_out, norm2_gain, w_gate_up, conv_w, conv_b, w_down, final_gain):
    B, S, D = x.shape
    assert B == 1 and w_in.shape[0] == 1
    x2 = x.reshape(S, D)
    n_cols = w_in.shape[2]

    qscale = LOG2E / math.sqrt(HEAD_DIM)
    col = np.arange(n_cols)
    dil_q0 = 2 * DIFF_QK_COLS + DIFF_WIDTH
    is_q = (col < DIFF_QK_COLS) | ((col >= dil_q0) & (col < dil_q0 + DIL_WIDTH))
    colscale = jnp.asarray(np.where(is_q, qscale, 1.0).astype(np.float32)).reshape(1, n_cols)
    tab = rel_bias_table.astype(F32) * LOG2E

    regroup = tuple(d for _, d in DILATED_PATTERNS if d != 1)
    proj, qt_all, vt_all, *cls = _in_proj(x2, norm1_gain.reshape(1, D), w_in[0].astype(BF16), colscale, regroup)
    cls_by_dil = dict(zip(regroup, cls))

    lam = _lambda(lambda_q1.reshape(1, -1), lambda_k1.reshape(1, -1),
                  lambda_q2.reshape(1, -1), lambda_k2.reshape(1, -1))
    o_d, w_out_bf, w_gate_up_bf, w_down_bf = _diff_attention(
        tab, proj, qt_all, vt_all, lam, diff_subln_gain.reshape(-1, 1), (w_out[0], w_gate_up[0], w_down[0]))

    dil_blk0 = (2 * DIFF_QK_COLS + DIFF_WIDTH) // DIL_WIDTH
    dilations = tuple(d for _, d in DILATED_PATTERNS)
    dil_gain = dil_out_gain.reshape(1, -1)

    def pattern_inputs(dilation):
        if dilation == 1:
            return proj.reshape(1, S, proj.shape[1]), dil_blk0
        return cls_by_dil[dilation], 0

    kn2 = _key_norm2(proj, dil_blk0 + 1)
    nums, dens = [], []
    for window, dilation in DILATED_PATTERNS:
        n_p, d_p = _dilated_fast(tab, kn2, *pattern_inputs(dilation), window, dilation)
        nums.append(n_p)
        dens.append(d_p)
    o_l_fast, den_min = _combine_fast(nums, dens, dil_gain, dilations)

    def exact_dilated():
        outs, lses = [], []
        for window, dilation in DILATED_PATTERNS:
            o_p, lse_p = _dilated_pattern(tab, *pattern_inputs(dilation), window, dilation)
            outs.append(o_p)
            lses.append(lse_p)
        return _combine(outs, lses, dil_gain, dilations)

    o_l = lax.cond(jnp.min(den_min) < MIN_DENOMINATOR, lambda fast: exact_dilated(), lambda fast: fast, o_l_fast)

    x1, h2 = _out_proj(o_d, o_l, w_out_bf, x2, norm2_gain.reshape(1, D))
    act = _ffn_up(h2, w_gate_up_bf, conv_w[0], conv_b.reshape(1, -1))
    out = _ffn_down(act, w_down_bf, x1, final_gain.reshape(1, D))
    return out.reshape(B, S, D)
```

```python
import functools
import math

import numpy as np
import jax
import jax.numpy as jnp
from jax import lax
from jax.experimental import pallas as pl
from jax.experimental.pallas import tpu as pltpu

F32 = jnp.float32
BF16 = jnp.bfloat16

HEAD_DIM = 128
N_DIFF_HEADS = 4
DIFF_V_DIM = 2 * HEAD_DIM
N_DIL_HEADS = 8
DIFF_QK_COLS = N_DIFF_HEADS * 2 * HEAD_DIM
DIFF_WIDTH = N_DIFF_HEADS * DIFF_V_DIM
DIL_WIDTH = N_DIL_HEADS * HEAD_DIM
DILATED_PATTERNS = ((128, 1), (512, 4), (2048, 16))
N_REL_BUCKETS = 32
REL_MAX_DISTANCE = 1024
NORM_EPS = 1e-6
SUBLN_EPS = 1e-5
NEG_INF = -1e30
LOG2E = math.log2(math.e)
LAM_INIT = 0.8 - 0.6 * math.exp(-0.3 * 0)

LANES = 128
F32_SUBLANES = 8
BF16_SUBLANES = 16
DIFF_TILE = 1024
DIFF_QUERY_PANEL = 256
FAR_UNROLL = 4
SHIFT_MARGIN = 1.0 + 2.0 ** -8
MIN_DENOMINATOR = 2.0 ** -60
DIL_CHUNK = 1024
DIL_HEAD_UNROLL = 4
DIL_PROB_SLOTS = 4
VMEM_LIMIT = 56 * 1024 * 1024


def _bucket_breaks():
    nb = N_REL_BUCKETS // 2
    max_exact = nb // 2
    rel = np.arange(-2 * REL_MAX_DISTANCE, 2 * REL_MAX_DISTANCE + 1)
    n = np.abs(rel)
    pos = np.log(np.maximum(n, 1) / max_exact) / math.log(REL_MAX_DISTANCE / max_exact) * (nb - max_exact)
    large = np.minimum(max_exact + np.floor(pos).astype(np.int64), nb - 1)
    bucket = np.where(rel > 0, nb, 0) + np.where(n < max_exact, n, large)
    breaks = [(int(rel[i]), int(bucket[i])) for i in range(1, len(rel)) if bucket[i] != bucket[i - 1]]
    return int(bucket[0]), breaks


FIRST_BUCKET, BUCKET_BREAKS = _bucket_breaks()
LAST_BUCKET = BUCKET_BREAKS[-1][1]
FAR_DIST = max(-BUCKET_BREAKS[0][0] + 1, BUCKET_BREAKS[-1][0])


def _bias_from_rel(rel, tab_ref, col):
    val = jnp.full(rel.shape, tab_ref[FIRST_BUCKET, col], F32)
    for thr, b in BUCKET_BREAKS:
        val = jnp.where(rel >= thr, tab_ref[b, col], val)
    return val


N_COL_GROUPS = 6
FIRST_DIL_GROUP = 3


def _inproj_kernel(x_ref, g_ref, w_ref, cs_ref, o_ref, qt_ref, vt_ref, *rest, dilations):
    cls_refs = rest[:len(dilations)]
    h_ref, stage_ref = rest[len(dilations):]
    t = pl.program_id(1)
    tm = x_ref.shape[0]
    n_dil = N_COL_GROUPS - FIRST_DIL_GROUP

    @pl.when(t == 0)
    def _():
        x = x_ref[...]
        ms = jnp.mean(x * x, axis=-1, keepdims=True)
        h_ref[...] = (x * lax.rsqrt(ms + NORM_EPS) * g_ref[...]).astype(BF16)

    def matmul():
        acc = jnp.dot(h_ref[...], w_ref[...], preferred_element_type=F32) * cs_ref[...]
        o_ref[...] = acc.astype(o_ref.dtype)
        return acc

    def stage(acc, slot):
        for cb in range(stage_ref.shape[1]):
            stage_ref[slot, cb] = acc[:, cb * LANES:(cb + 1) * LANES]

    def regroup(slot):
        for cb in range(stage_ref.shape[1]):
            sl = slice(cb * LANES, (cb + 1) * LANES)
            for cls_ref, dil in zip(cls_refs, dilations):
                for c in range(dil):
                    cls_ref[c, :, sl] = stage_ref[slot, cb, pl.ds(c, tm // dil, stride=dil), :].astype(cls_ref.dtype)

    for step in range(N_COL_GROUPS):
        @pl.when(t == step)
        def _(step=step):
            acc = matmul()
            if 1 <= step <= n_dil:
                regroup((step - 1) % 2)
            if step < n_dil:
                stage(acc, step % 2)
            if step == n_dil:
                qt_ref[...] = acc.T.astype(qt_ref.dtype)
            if step == n_dil + 2:
                vt_ref[...] = acc.T.astype(vt_ref.dtype)


def _in_proj(x2, gain, w_bf, colscale, dilations, *, tm=512):
    S, D = x2.shape
    N = w_bf.shape[1]
    tn = DIFF_QK_COLS
    assert DIFF_WIDTH == tn and DIL_WIDTH == tn and N == N_COL_GROUPS * tn
    n_dil = N_COL_GROUPS - FIRST_DIL_GROUP
    kern = functools.partial(_inproj_kernel, dilations=dilations)

    def group(t):
        return (t + FIRST_DIL_GROUP) % N_COL_GROUPS

    cls_specs = [pl.BlockSpec((d, tm // d, tn), lambda i, t: (0, i, jnp.clip(t - 1, 0, n_dil - 1)))
                 for d in dilations]
    cls_shapes = [jax.ShapeDtypeStruct((d, S // d, n_dil * tn), BF16) for d in dilations]
    return pl.pallas_call(
        kern,
        grid=(S // tm, N_COL_GROUPS),
        in_specs=[
            pl.BlockSpec((tm, D), lambda i, t: (i, 0)),
            pl.BlockSpec((1, D), lambda i, t: (0, 0)),
            pl.BlockSpec((D, tn), lambda i, t: (0, group(t))),
            pl.BlockSpec((1, tn), lambda i, t: (0, group(t))),
        ],
        out_specs=[
            pl.BlockSpec((tm, tn), lambda i, t: (i, group(t))),
            pl.BlockSpec((tn, tm), lambda i, t: (0, i)),
            pl.BlockSpec((tn, tm), lambda i, t: (0, i)),
        ] + cls_specs,
        out_shape=[
            jax.ShapeDtypeStruct((S, N), BF16),
            jax.ShapeDtypeStruct((tn, S), BF16),
            jax.ShapeDtypeStruct((tn, S), BF16),
        ] + cls_shapes,
        scratch_shapes=[pltpu.VMEM((tm, D), BF16), pltpu.VMEM((2, tn // LANES, tm, LANES), F32)],
        compiler_params=pltpu.CompilerParams(
            dimension_semantics=("arbitrary", "arbitrary"), vmem_limit_bytes=VMEM_LIMIT),
        name="in_proj",
    )(x2, gain, w_bf, colscale)


def _diff_attn_kernel(*refs, T, QP, n_near, n_tiles, n_cast):
    tab_ref, q1t_ref, q2t_ref, k1_ref, k2_ref, vt_ref, lam_ref, gain_ref = refs[:8]
    cast_in, o_ref, cast_out = refs[8:8 + n_cast], refs[8 + n_cast], refs[9 + n_cast:9 + 2 * n_cast]
    bias_ref, m_ref, l_ref, acc_ref, p_ref, shift_ref, knorm_ref, prev_ref = refs[9 + 2 * n_cast:]
    _diff_attn_body(tab_ref, q1t_ref, q2t_ref, k1_ref, k2_ref, vt_ref, lam_ref, gain_ref, o_ref,
                    bias_ref, m_ref, l_ref, acc_ref, p_ref, shift_ref, knorm_ref, prev_ref,
                    T=T, QP=QP, n_near=n_near, n_tiles=n_tiles)
    for src, dst in zip(cast_in, cast_out):
        dst[...] = src[...].astype(dst.dtype)


def _diff_attn_body(tab_ref, q1t_ref, q2t_ref, k1_ref, k2_ref, vt_ref, lam_ref, gain_ref, o_ref,
                    bias_ref, m_ref, l_ref, acc_ref, p_ref, shift_ref, knorm_ref, prev_ref,
                    *, T, QP, n_near, n_tiles):
    h = pl.program_id(0)
    qi = pl.program_id(1)
    n_chains = 2 * (T // QP)
    qts = (q1t_ref, q2t_ref)
    ks = (k1_ref, k2_ref)

    def chain_of(i):
        qp = i // 2
        return i % 2, slice(qp * QP, (qp + 1) * QP)

    @pl.when(qi == 0)
    def _():
        for mi in range(2):
            def knorm(t, best, mi=mi):
                k = ks[mi][pl.ds(pl.multiple_of(t * T, T), T), :].astype(F32)
                return jnp.maximum(best, jnp.max(jnp.sum(k * k, axis=1, keepdims=True), axis=0, keepdims=True))
            knorm_ref[mi] = lax.fori_loop(0, n_tiles, knorm, jnp.zeros((1, 1), F32))

        x = lax.broadcasted_iota(jnp.int32, (F32_SUBLANES, 2 * T), 1)
        x = jnp.where(x < T, x, x - 2 * T)
        for di, d in enumerate(range(-n_near, n_near + 1)):
            g = _bias_from_rel(d * T - x, tab_ref, h)
            base = jnp.broadcast_to(g[0:1, :], (LANES, 2 * T))
            for rb in range(T // LANES):
                blk = pltpu.roll(base, rb * LANES, 1, stride=1, stride_axis=0)
                bias_ref[di, rb * LANES:(rb + 1) * LANES, :] = blk[:, :T]

    c_left = tab_ref[FIRST_BUCKET, h]
    c_right = tab_ref[LAST_BUCKET, h]

    bias_max = tab_ref[0, h]
    for b in range(1, N_REL_BUCKETS):
        bias_max = jnp.maximum(bias_max, tab_ref[b, h])

    for mi in range(2):
        q = qts[mi][...].astype(F32)
        qnorm2 = jnp.sum(q * q, axis=0, keepdims=True)
        shift_ref[mi] = jnp.sqrt(qnorm2 * knorm_ref[mi]) * SHIFT_MARGIN + bias_max
    l_ref[...] = jnp.zeros(l_ref.shape, F32)
    acc_ref[...] = jnp.zeros(acc_ref.shape, F32)
    p_ref[n_chains - 1] = jnp.zeros((T, QP), BF16)
    prev_ref[0] = 0

    def score_stage(i, k0, bias_di, bias_const):
        mi, qs = chain_of(i)
        s = jnp.dot(ks[mi][pl.ds(k0, T), :], qts[mi][:, qs], preferred_element_type=F32)
        if bias_di is not None:
            p = jnp.exp2(s + bias_ref[bias_di, :, qs] - shift_ref[mi, :, qs])
        else:
            p = jnp.exp2(s - (shift_ref[mi, :, qs] - bias_const))
        l_ref[mi, :, qs] += jnp.sum(p, axis=0, keepdims=True)
        p_ref[i] = p.astype(BF16)

    def value_stage(i, k0):
        mi, qs = chain_of(i)
        acc_ref[mi, :, qs] += jnp.dot(vt_ref[:, pl.ds(k0, T)], p_ref[i], preferred_element_type=F32)

    def tile(kt, bias_di, bias_const, prev_kt=None):
        k0 = pl.multiple_of(kt * T, T)
        pk0 = pl.multiple_of((prev_ref[0] if prev_kt is None else prev_kt) * T, T)
        for i in range(n_chains):
            value_stage((i - 1) % n_chains, pk0 if i < 1 else k0)
            score_stage(i, k0, bias_di, bias_const)
        prev_ref[0] = kt

    lo = jnp.maximum(qi - n_near, 0)
    hi = jnp.minimum(qi + n_near + 1, n_tiles)

    for di, d in enumerate(range(-n_near, n_near + 1)):
        kt = qi + d

        @pl.when(jnp.logical_and(kt >= 0, kt < n_tiles))
        def _(di=di, kt=kt):
            tile(kt, di, None)

    n_far = n_tiles - (hi - lo)

    def far_run(first, length):
        prev_kt = None
        for j in range(length):
            before = first + j < lo
            kt = jnp.where(before, first + j, first + j + (hi - lo))
            tile(kt, None, jnp.where(before, c_left, c_right), prev_kt=prev_kt)
            prev_kt = kt

    done = 0
    run = 1
    while run < FAR_UNROLL:
        has_run = jnp.bitwise_and(n_far, run)

        @pl.when(has_run != 0)
        def _(done=done, run=run):
            far_run(done, run)

        done = done + has_run
        run *= 2

    @pl.loop(0, n_far // FAR_UNROLL)
    def _(trip, done=done):
        far_run(done + trip * FAR_UNROLL, FAR_UNROLL)

    value_stage(n_chains - 1, pl.multiple_of(prev_ref[0] * T, T))

    @pl.when(jnp.min(l_ref[...]) < MIN_DENOMINATOR)
    def _():
        m_ref[...] = jnp.full(m_ref.shape, -jnp.inf, F32)
        l_ref[...] = jnp.zeros(l_ref.shape, F32)
        acc_ref[...] = jnp.zeros(acc_ref.shape, F32)

        @pl.loop(0, n_tiles)
        def _(kt):
            k0 = pl.multiple_of(kt * T, T)
            d = kt - qi
            near = jnp.abs(d) <= n_near
            di = jnp.clip(d + n_near, 0, 2 * n_near)
            c_far = jnp.where(d < 0, c_left, c_right)
            for i in range(n_chains):
                mi, qs = chain_of(i)
                s = jnp.dot(ks[mi][pl.ds(k0, T), :], qts[mi][:, qs], preferred_element_type=F32)
                s = s + jnp.where(near, bias_ref[di, :, qs], c_far)
                m = m_ref[mi, :, qs]
                m_new = jnp.maximum(m, jnp.max(s, axis=0, keepdims=True))
                alpha = jnp.exp2(m - m_new)
                p = jnp.exp2(s - m_new)
                m_ref[mi, :, qs] = m_new
                l_ref[mi, :, qs] = alpha * l_ref[mi, :, qs] + jnp.sum(p, axis=0, keepdims=True)
                pv = jnp.dot(vt_ref[:, pl.ds(k0, T)], p.astype(BF16), preferred_element_type=F32)
                acc_ref[mi, :, qs] = alpha * acc_ref[mi, :, qs] + pv

    lam = lam_ref[0, 0]
    o = acc_ref[0] / l_ref[0] - lam * (acc_ref[1] / l_ref[1])
    ms = jnp.mean(o * o, axis=0, keepdims=True)
    o = o * lax.rsqrt(ms + SUBLN_EPS) * (gain_ref[...] * (1.0 - LAM_INIT))
    o_ref[...] = o.T.astype(o_ref.dtype)


def _lambda_kernel(q1_ref, k1_ref, q2_ref, k2_ref, o_ref):
    a = jnp.sum(q1_ref[...] * k1_ref[...], axis=-1, keepdims=True)
    b = jnp.sum(q2_ref[...] * k2_ref[...], axis=-1, keepdims=True)
    o_ref[...] = jnp.exp(a) - jnp.exp(b) + LAM_INIT


def _lambda(lq1, lk1, lq2, lk2):
    return pl.pallas_call(
        _lambda_kernel, out_shape=jax.ShapeDtypeStruct((1, 1), F32), name="diff_lambda",
    )(lq1, lk1, lq2, lk2)


def _diff_attention(tab_diff, proj, qt_all, vt_all, lam, gain_col, f32_weights):
    S = proj.shape[0]
    T = DIFF_TILE
    n_tiles = S // T
    n_near = -(-(FAR_DIST - 1) // T)
    kern = functools.partial(_diff_attn_kernel, T=T, QP=DIFF_QUERY_PANEL, n_near=n_near, n_tiles=n_tiles,
                             n_cast=len(f32_weights))

    n_steps = N_DIFF_HEADS * n_tiles
    cast_specs = []
    for w in f32_weights:
        rows, cols = w.shape
        every = next(e for e in (1, 2, 4, 8) if rows % (n_steps // e) == 0
                     and (rows // (n_steps // e)) % BF16_SUBLANES == 0)
        cast_specs.append(pl.BlockSpec((rows // (n_steps // every), cols),
                                       lambda h, i, every=every: ((h * n_tiles + i) // every, 0)))
    k_block0 = DIFF_QK_COLS // HEAD_DIM
    smem = pl.BlockSpec(memory_space=pltpu.SMEM)
    return pl.pallas_call(
        kern,
        grid=(N_DIFF_HEADS, n_tiles),
        in_specs=[
            smem,
            pl.BlockSpec((HEAD_DIM, T), lambda h, i: (2 * h, i)),
            pl.BlockSpec((HEAD_DIM, T), lambda h, i: (2 * h + 1, i)),
            pl.BlockSpec((S, HEAD_DIM), lambda h, i: (0, k_block0 + 2 * h), pipeline_mode=pl.Buffered(1)),
            pl.BlockSpec((S, HEAD_DIM), lambda h, i: (0, k_block0 + 2 * h + 1), pipeline_mode=pl.Buffered(1)),
            pl.BlockSpec((DIFF_V_DIM, S), lambda h, i: (h, 0), pipeline_mode=pl.Buffered(1)),
            smem,
            pl.BlockSpec((DIFF_V_DIM, 1), lambda h, i: (0, 0)),
        ] + cast_specs,
        out_specs=[pl.BlockSpec((T, DIFF_V_DIM), lambda h, i: (i, h))] + cast_specs,
        out_shape=[jax.ShapeDtypeStruct((S, DIFF_WIDTH), BF16)]
        + [jax.ShapeDtypeStruct(w.shape, BF16) for w in f32_weights],
        scratch_shapes=[pltpu.VMEM((2 * n_near + 1, T, T), F32),
                        pltpu.VMEM((2, 1, T), F32), pltpu.VMEM((2, 1, T), F32),
                        pltpu.VMEM((2, DIFF_V_DIM, T), F32),
                        pltpu.VMEM((2 * (T // DIFF_QUERY_PANEL), T, DIFF_QUERY_PANEL), BF16),
                        pltpu.VMEM((2, 1, T), F32), pltpu.VMEM((2, 1, 1), F32),
                        pltpu.SMEM((1,), jnp.int32)],
        compiler_params=pltpu.CompilerParams(
            dimension_semantics=("arbitrary", "arbitrary"), vmem_limit_bytes=VMEM_LIMIT),
        name="diff_attn",
    )(tab_diff, qt_all, qt_all, proj, proj, vt_all, lam, gain_col, *f32_weights)


def _dilated_kernel(tab_ref, q_ref, kp_ref, km_ref, kn_ref, vp_ref, vm_ref, vn_ref, o_ref, lse_ref,
                    bias_ref, kx_ref, vx_ref, *, R, B, half, dilation, n_chunks):
    c = pl.program_id(0)
    n = pl.program_id(1)
    W = B + 2 * half
    nblk = R // B

    @pl.when(jnp.logical_and(c == 0, n == 0))
    def _():
        rows = F32_SUBLANES
        col = lax.broadcasted_iota(jnp.int32, (rows, W), 1)
        row = lax.broadcasted_iota(jnp.int32, (rows, W), 0)
        for hh in range(N_DIL_HEADS):
            def fill(r, carry, hh=hh):
                r0 = pl.multiple_of(r * rows, rows)
                off = col - half - (row + r0)
                bias = _bias_from_rel(off * dilation, tab_ref, N_DIFF_HEADS + hh)
                base = jnp.where(jnp.abs(off) <= half, bias, NEG_INF * LOG2E)
                bias_ref[hh, 1, pl.ds(r0, rows), :] = base
                bias_ref[hh, 0, pl.ds(r0, rows), :] = jnp.where(col >= half, base, NEG_INF * LOG2E)
                bias_ref[hh, 2, pl.ds(r0, rows), :] = jnp.where(col < B + half, base, NEG_INF * LOG2E)
                return carry
            lax.fori_loop(0, B // rows, fill, 0)

    kx_ref[0:half, :] = kp_ref[...]
    kx_ref[half:half + R, :] = km_ref[...]
    kx_ref[half + R:, :] = kn_ref[...]
    vx_ref[0:half, :] = vp_ref[...]
    vx_ref[half:half + R, :] = vm_ref[...]
    vx_ref[half + R:, :] = vn_ref[...]

    def chain(hh, b):
        c0 = pl.multiple_of(hh * HEAD_DIM, HEAD_DIM)
        r0 = b * B
        var = 1
        if b == 0:
            var = jnp.where(n == 0, 0, var)
        if b == nblk - 1:
            var = jnp.where(n == n_chunks - 1, 2, var)
        q = q_ref[pl.ds(r0, B), pl.ds(c0, HEAD_DIM)]
        k = kx_ref[pl.ds(r0, W), pl.ds(c0, HEAD_DIM)]
        v = vx_ref[pl.ds(r0, W), pl.ds(c0, HEAD_DIM)]
        s = lax.dot_general(q, k, (((1,), (1,)), ((), ())), preferred_element_type=F32)
        s = s + bias_ref[hh, var]
        m = jnp.max(s, axis=-1, keepdims=True)
        e = jnp.exp2(s - m)
        den = jnp.sum(e, axis=-1, keepdims=True)
        o = jnp.dot(e.astype(BF16), v, preferred_element_type=F32) / den
        o_ref[pl.ds(r0, B), pl.ds(c0, HEAD_DIM)] = o.astype(o_ref.dtype)
        lse = m + jnp.log2(den)
        lse_ref[pl.ds(r0, B), pl.ds(c0, HEAD_DIM)] = jnp.broadcast_to(lse, (B, HEAD_DIM))

    @pl.loop(0, N_DIL_HEADS // DIL_HEAD_UNROLL)
    def _(hg):
        for u in range(DIL_HEAD_UNROLL):
            for b in range(nblk):
                chain(hg * DIL_HEAD_UNROLL + u, b)


def _dilated_pattern(tab, qkv, col_blk0, window, dilation, *, B=256):
    _, L, _ = qkv.shape
    R = min(DIL_CHUNK, L)
    half = window // (2 * dilation)
    assert L % R == 0 and R % B == 0 and half % BF16_SUBLANES == 0 and R % half == 0
    n_chunks = L // R
    q_blk, k_blk, v_blk = col_blk0, col_blk0 + 1, col_blk0 + 2
    hb = R // half
    n_hblk = L // half

    def main(blk):
        return pl.BlockSpec((None, R, DIL_WIDTH), lambda c, n: (c, n, blk))

    def prev(blk):
        return pl.BlockSpec((None, half, DIL_WIDTH), lambda c, n: (c, jnp.maximum(n * hb - 1, 0), blk))

    def nxt(blk):
        return pl.BlockSpec((None, half, DIL_WIDTH), lambda c, n: (c, jnp.minimum((n + 1) * hb, n_hblk - 1), blk))

    kern = functools.partial(_dilated_kernel, R=R, B=B, half=half, dilation=dilation, n_chunks=n_chunks)
    out_spec = pl.BlockSpec((None, R, DIL_WIDTH), lambda c, n: (c, n, 0))
    return pl.pallas_call(
        kern,
        grid=(dilation, n_chunks),
        in_specs=[pl.BlockSpec(memory_space=pltpu.SMEM),
                  main(q_blk), prev(k_blk), main(k_blk), nxt(k_blk), prev(v_blk), main(v_blk), nxt(v_blk)],
        out_specs=[out_spec, out_spec],
        out_shape=[jax.ShapeDtypeStruct((dilation, L, DIL_WIDTH), BF16),
                   jax.ShapeDtypeStruct((dilation, L, DIL_WIDTH), F32)],
        scratch_shapes=[pltpu.VMEM((N_DIL_HEADS, 3, B, B + 2 * half), F32),
                        pltpu.VMEM((R + 2 * half, DIL_WIDTH), BF16),
                        pltpu.VMEM((R + 2 * half, DIL_WIDTH), BF16)],
        compiler_params=pltpu.CompilerParams(
            dimension_semantics=("arbitrary", "arbitrary"), vmem_limit_bytes=VMEM_LIMIT),
        name=f"dilated_d{dilation}",
    )(tab, qkv, qkv, qkv, qkv, qkv, qkv, qkv)


def _combine_kernel(*refs, dilations):
    n = len(dilations)
    o_refs, l_refs = refs[:n], refs[n:2 * n]
    g_ref, out_ref = refs[2 * n], refs[2 * n + 1]
    scratch = refs[2 * n + 2:]
    tm = out_ref.shape[0]

    for hh in range(N_DIL_HEADS):
        sl = slice(hh * HEAD_DIM, (hh + 1) * HEAD_DIM)
        outs, lses = [], []
        si = 0
        for o_ref, l_ref, dil in zip(o_refs, l_refs, dilations):
            if dil == 1:
                outs.append(o_ref[0, :, sl].astype(F32))
                lses.append(l_ref[0, :, sl])
                continue
            os_ref, ls_ref = scratch[si], scratch[si + 1]
            si += 2
            for c in range(dil):
                os_ref[hh, pl.ds(c, tm // dil, stride=dil), :] = o_ref[c, :, sl].astype(F32)
                ls_ref[hh, pl.ds(c, tm // dil, stride=dil), :] = l_ref[c, :, sl]
            outs.append(os_ref[hh])
            lses.append(ls_ref[hh])

        m = functools.reduce(jnp.maximum, lses)
        ws = [jnp.exp2(l - m) for l in lses]
        tot = functools.reduce(lambda a, b: a + b, ws)
        oh = functools.reduce(lambda a, b: a + b, [(w / tot) * op for w, op in zip(ws, outs)])
        ms = jnp.mean(oh * oh, axis=-1, keepdims=True)
        out_ref[:, sl] = (oh * lax.rsqrt(ms + NORM_EPS) * g_ref[:, sl]).astype(out_ref.dtype)


def _combine(outs, lses, gain, dilations, *, tm=512):
    S = outs[0].shape[0] * outs[0].shape[1]
    specs = [pl.BlockSpec((d, tm // d, DIL_WIDTH), lambda i: (0, i, 0)) for d in dilations]
    n_scr = sum(1 for d in dilations if d != 1)
    return pl.pallas_call(
        functools.partial(_combine_kernel, dilations=dilations),
        grid=(S // tm,),
        in_specs=specs + specs + [pl.BlockSpec((1, DIL_WIDTH), lambda i: (0, 0))],
        scratch_shapes=[pltpu.VMEM((N_DIL_HEADS, tm, HEAD_DIM), F32)] * (2 * n_scr),
        out_specs=pl.BlockSpec((tm, DIL_WIDTH), lambda i: (i, 0)),
        out_shape=jax.ShapeDtypeStruct((S, DIL_WIDTH), BF16),
        compiler_params=pltpu.CompilerParams(
            dimension_semantics=("arbitrary",), vmem_limit_bytes=VMEM_LIMIT),
        name="dilated_combine",
    )(*outs, *lses, gain)


def _key_norm2_kernel(k_ref, o_ref):
    @pl.when(pl.program_id(0) == 0)
    def _():
        o_ref[...] = jnp.zeros(o_ref.shape, F32)

    k = k_ref[...].astype(F32)
    for hh in range(N_DIL_HEADS):
        kh = k[:, hh * HEAD_DIM:(hh + 1) * HEAD_DIM]
        n2 = jnp.max(jnp.sum(kh * kh, axis=1, keepdims=True), axis=0, keepdims=True)
        o_ref[hh:hh + 1, :] = jnp.maximum(o_ref[hh:hh + 1, :], jnp.broadcast_to(n2, (1, LANES)))


def _key_norm2(proj, k_blk, *, tm=1024):
    S = proj.shape[0]
    return pl.pallas_call(
        _key_norm2_kernel,
        grid=(S // tm,),
        in_specs=[pl.BlockSpec((tm, DIL_WIDTH), lambda i: (i, k_blk))],
        out_specs=pl.BlockSpec((N_DIL_HEADS, LANES), lambda i: (0, 0)),
        out_shape=jax.ShapeDtypeStruct((N_DIL_HEADS, LANES), F32),
        compiler_params=pltpu.CompilerParams(dimension_semantics=("arbitrary",), vmem_limit_bytes=VMEM_LIMIT),
        name="dilated_key_norm",
    )(proj)


def _dilated_fast_kernel(tab_ref, kn2_ref, q_ref, kp_ref, km_ref, kn_ref, vp_ref, vm_ref, vn_ref, num_ref, den_ref,
                         bias_ref, kx_ref, vx_ref, p_ref, *, R, B, half, dilation, n_chunks):
    c = pl.program_id(0)
    n = pl.program_id(1)
    W = B + 2 * half
    nblk = R // B
    VW = 2 * HEAD_DIM

    @pl.when(jnp.logical_and(c == 0, n == 0))
    def _():
        rows = F32_SUBLANES
        col = lax.broadcasted_iota(jnp.int32, (rows, W), 1)
        row = lax.broadcasted_iota(jnp.int32, (rows, W), 0)
        for hh in range(N_DIL_HEADS):
            def fill(r, carry, hh=hh):
                r0 = pl.multiple_of(r * rows, rows)
                off = col - half - (row + r0)
                bias = _bias_from_rel(off * dilation, tab_ref, N_DIFF_HEADS + hh)
                base = jnp.where(jnp.abs(off) <= half, bias, NEG_INF * LOG2E)
                bias_ref[hh, 1, pl.ds(r0, rows), :] = base
                bias_ref[hh, 0, pl.ds(r0, rows), :] = jnp.where(col >= half, base, NEG_INF * LOG2E)
                bias_ref[hh, 2, pl.ds(r0, rows), :] = jnp.where(col < B + half, base, NEG_INF * LOG2E)
                return carry
            lax.fori_loop(0, B // rows, fill, 0)
        vx_ref[...] = jnp.ones(vx_ref.shape, BF16)

    kx_ref[0:half, :] = kp_ref[...]
    kx_ref[half:half + R, :] = km_ref[...]
    kx_ref[half + R:, :] = kn_ref[...]
    for hh in range(N_DIL_HEADS):
        src = slice(hh * HEAD_DIM, (hh + 1) * HEAD_DIM)
        dst = slice(hh * VW, hh * VW + HEAD_DIM)
        vx_ref[0:half, dst] = vp_ref[:, src]
        vx_ref[half:half + R, dst] = vm_ref[:, src]
        vx_ref[half + R:, dst] = vn_ref[:, src]

    den_ref[...] = jnp.zeros(den_ref.shape, F32)
    n_slots = p_ref.shape[0]
    bias_max = []
    for hh in range(N_DIL_HEADS):
        bm = tab_ref[0, N_DIFF_HEADS + hh]
        for bk in range(1, N_REL_BUCKETS):
            bm = jnp.maximum(bm, tab_ref[bk, N_DIFF_HEADS + hh])
        bias_max.append(bm)

    def score_stage(hh, b, slot):
        cs = slice(hh * HEAD_DIM, (hh + 1) * HEAD_DIM)
        r0 = b * B
        var = 1
        if b == 0:
            var = jnp.where(n == 0, 0, var)
        if b == nblk - 1:
            var = jnp.where(n == n_chunks - 1, 2, var)
        q = q_ref[r0:r0 + B, cs]
        qf = q.astype(F32)
        qn2 = jnp.sum(qf * qf, axis=1, keepdims=True)
        shift = jnp.sqrt(qn2 * kn2_ref[hh:hh + 1, 0:1]) * SHIFT_MARGIN + bias_max[hh]
        s = lax.dot_general(q, kx_ref[r0:r0 + W, cs], (((1,), (1,)), ((), ())), preferred_element_type=F32)
        p_ref[slot] = jnp.exp2(s + bias_ref[hh, var] - shift).astype(BF16)

    def value_stage(hh, b, slot):
        cs = slice(hh * HEAD_DIM, (hh + 1) * HEAD_DIM)
        r0 = b * B
        nd = jnp.dot(p_ref[slot], vx_ref[r0:r0 + W, hh * VW:(hh + 1) * VW], preferred_element_type=F32)
        num_ref[r0:r0 + B, cs] = nd[:, :HEAD_DIM].astype(num_ref.dtype)
        den_ref[r0:r0 + B, hh:hh + 1] = nd[:, HEAD_DIM + hh:HEAD_DIM + hh + 1]

    chains = [(hh, b) for hh in range(N_DIL_HEADS) for b in range(nblk)]
    for i, (hh, b) in enumerate(chains):
        if i > 0:
            value_stage(*chains[i - 1], (i - 1) % n_slots)
        score_stage(hh, b, i % n_slots)
    value_stage(*chains[-1], (len(chains) - 1) % n_slots)


def _dilated_fast(tab, kn2, qkv, col_blk0, window, dilation, *, B=128):
    _, L, _ = qkv.shape
    R = min(DIL_CHUNK, L)
    half = window // (2 * dilation)
    assert L % R == 0 and R % B == 0 and half % BF16_SUBLANES == 0 and R % half == 0
    n_chunks = L // R
    q_blk, k_blk, v_blk = col_blk0, col_blk0 + 1, col_blk0 + 2
    hb = R // half
    n_hblk = L // half

    def main(blk):
        return pl.BlockSpec((None, R, DIL_WIDTH), lambda c, n: (c, n, blk))

    def prev(blk):
        return pl.BlockSpec((None, half, DIL_WIDTH), lambda c, n: (c, jnp.maximum(n * hb - 1, 0), blk))

    def nxt(blk):
        return pl.BlockSpec((None, half, DIL_WIDTH), lambda c, n: (c, jnp.minimum((n + 1) * hb, n_hblk - 1), blk))

    kern = functools.partial(_dilated_fast_kernel, R=R, B=B, half=half, dilation=dilation, n_chunks=n_chunks)
    out_spec = pl.BlockSpec((None, R, DIL_WIDTH), lambda c, n: (c, n, 0))
    return pl.pallas_call(
        kern,
        grid=(dilation, n_chunks),
        in_specs=[pl.BlockSpec(memory_space=pltpu.SMEM),
                  pl.BlockSpec((N_DIL_HEADS, LANES), lambda c, n: (0, 0)),
                  main(q_blk), prev(k_blk), main(k_blk), nxt(k_blk), prev(v_blk), main(v_blk), nxt(v_blk)],
        out_specs=[out_spec, pl.BlockSpec((None, R, LANES), lambda c, n: (c, n, 0))],
        out_shape=[jax.ShapeDtypeStruct((dilation, L, DIL_WIDTH), BF16),
                   jax.ShapeDtypeStruct((dilation, L, LANES), F32)],
        scratch_shapes=[pltpu.VMEM((N_DIL_HEADS, 3, B, B + 2 * half), F32),
                        pltpu.VMEM((R + 2 * half, DIL_WIDTH), BF16),
                        pltpu.VMEM((R + 2 * half, 2 * DIL_WIDTH), BF16),
                        pltpu.VMEM((DIL_PROB_SLOTS, B, B + 2 * half), BF16)],
        compiler_params=pltpu.CompilerParams(
            dimension_semantics=("arbitrary", "arbitrary"), vmem_limit_bytes=VMEM_LIMIT),
        name=f"dilated_fast_d{dilation}",
    )(tab, kn2, qkv, qkv, qkv, qkv, qkv, qkv, qkv)


def _combine_fast_kernel(*refs, dilations):
    n = len(dilations)
    n_refs, d_refs = refs[:n], refs[n:2 * n]
    g_ref, out_ref, dmin_ref = refs[2 * n], refs[2 * n + 1], refs[2 * n + 2]
    scratch = refs[2 * n + 3:]
    tm = out_ref.shape[0]

    den_all = None
    si = 0
    for d_ref, dil in zip(d_refs, dilations):
        if dil == 1:
            den_p = d_ref[0]
        else:
            ds_ref = scratch[si + 1]
            for c in range(dil):
                ds_ref[pl.ds(c, tm // dil, stride=dil), :] = d_ref[c]
            den_p = ds_ref[...]
            si += 2
        den_all = den_p if den_all is None else den_all + den_p

    for hh in range(N_DIL_HEADS):
        sl = slice(hh * HEAD_DIM, (hh + 1) * HEAD_DIM)
        nums = []
        si = 0
        for n_ref, dil in zip(n_refs, dilations):
            if dil == 1:
                nums.append(n_ref[0, :, sl].astype(F32))
                continue
            ns_ref = scratch[si]
            si += 2
            for c in range(dil):
                ns_ref[hh, pl.ds(c, tm // dil, stride=dil), :] = n_ref[c, :, sl].astype(F32)
            nums.append(ns_ref[hh])

        den = den_all[:, hh:hh + 1]
        oh = functools.reduce(lambda a, b: a + b, nums) / den
        ms = jnp.mean(oh * oh, axis=-1, keepdims=True)
        out_ref[:, sl] = (oh * lax.rsqrt(ms + NORM_EPS) * g_ref[:, sl]).astype(out_ref.dtype)
        dmin_ref[0, hh:hh + 1, :] = jnp.broadcast_to(jnp.min(den, axis=0, keepdims=True), (1, LANES))


def _combine_fast(nums, dens, gain, dilations, *, tm=512):
    S = nums[0].shape[0] * nums[0].shape[1]
    num_specs = [pl.BlockSpec((d, tm // d, DIL_WIDTH), lambda i: (0, i, 0)) for d in dilations]
    den_specs = [pl.BlockSpec((d, tm // d, LANES), lambda i: (0, i, 0)) for d in dilations]
    n_scr = sum(1 for d in dilations if d != 1)
    return pl.pallas_call(
        functools.partial(_combine_fast_kernel, dilations=dilations),
        grid=(S // tm,),
        in_specs=num_specs + den_specs + [pl.BlockSpec((1, DIL_WIDTH), lambda i: (0, 0))],
        scratch_shapes=[pltpu.VMEM((N_DIL_HEADS, tm, HEAD_DIM), F32), pltpu.VMEM((tm, LANES), F32)] * n_scr,
        out_specs=[pl.BlockSpec((tm, DIL_WIDTH), lambda i: (i, 0)),
                   pl.BlockSpec((1, N_DIL_HEADS, LANES), lambda i: (i, 0, 0))],
        out_shape=[jax.ShapeDtypeStruct((S, DIL_WIDTH), BF16),
                   jax.ShapeDtypeStruct((S // tm, N_DIL_HEADS, LANES), F32)],
        compiler_params=pltpu.CompilerParams(
            dimension_semantics=("arbitrary",), vmem_limit_bytes=VMEM_LIMIT),
        name="dilated_combine_fast",
    )(*nums, *dens, gain)


def _outproj_kernel(od_ref, ol_ref, wd_ref, wl_ref, x_ref, g_ref, x1_ref, h2_ref):
    acc = jnp.dot(od_ref[...], wd_ref[...], preferred_element_type=F32)
    acc = acc + jnp.dot(ol_ref[...], wl_ref[...], preferred_element_type=F32)
    x1 = x_ref[...] + acc
    x1_ref[...] = x1
    ms = jnp.mean(x1 * x1, axis=-1, keepdims=True)
    h2_ref[...] = (x1 * lax.rsqrt(ms + NORM_EPS) * g_ref[...]).astype(h2_ref.dtype)


def _out_proj(o_d, o_l, w_bf, x2, gain, *, tm=512):
    S, D = x2.shape
    return pl.pallas_call(
        _outproj_kernel,
        grid=(S // tm,),
        in_specs=[
            pl.BlockSpec((tm, DIFF_WIDTH), lambda i: (i, 0)),
            pl.BlockSpec((tm, DIL_WIDTH), lambda i: (i, 0)),
            pl.BlockSpec((DIFF_WIDTH, D), lambda i: (0, 0)),
            pl.BlockSpec((DIL_WIDTH, D), lambda i: (1, 0)),
            pl.BlockSpec((tm, D), lambda i: (i, 0)),
            pl.BlockSpec((1, D), lambda i: (0, 0)),
        ],
        out_specs=[pl.BlockSpec((tm, D), lambda i: (i, 0)), pl.BlockSpec((tm, D), lambda i: (i, 0))],
        out_shape=[jax.ShapeDtypeStruct((S, D), F32), jax.ShapeDtypeStruct((S, D), BF16)],
        compiler_params=pltpu.CompilerParams(
            dimension_semantics=("arbitrary",), vmem_limit_bytes=VMEM_LIMIT),
        name="out_proj",
    )(o_d, o_l, w_bf, w_bf, x2, gain)


def _ffn_up_kernel(hm_ref, hp_ref, hn_ref, wg_ref, wu_ref, cw_ref, cb_ref, o_ref, lhs_ref, *, tm, n_row_tiles):
    i = pl.program_id(0)
    j = pl.program_id(1)
    halo = BF16_SUBLANES

    @pl.when(j == 0)
    def _():
        lhs_ref[0:halo, :] = jnp.where(i == 0, jnp.zeros_like(hp_ref[...]), hp_ref[...])
        lhs_ref[halo:halo + tm, :] = hm_ref[...]
        lhs_ref[halo + tm:, :] = jnp.where(i == n_row_tiles - 1, jnp.zeros_like(hn_ref[...]), hn_ref[...])

    g = jnp.dot(lhs_ref[...], wg_ref[...], preferred_element_type=F32)
    u = jnp.dot(lhs_ref[halo:halo + tm, :], wu_ref[...], preferred_element_type=F32)
    rows = tm + 2 * halo
    g_prev = pltpu.roll(g, 1, axis=0)
    g_next = pltpu.roll(g, rows - 1, axis=0)
    y = cw_ref[0:1, :] * g_prev + cw_ref[1:2, :] * g + cw_ref[2:3, :] * g_next + cb_ref[...]
    y = y[halo:halo + tm, :]
    act = y * (1.0 / (1.0 + jnp.exp(-y))) * u
    o_ref[...] = act.astype(o_ref.dtype)


def _ffn_up(h2, w_bf, conv_w, conv_b, *, tm=1024, tn=512):
    S, D = h2.shape
    d_ff = conv_w.shape[1]
    assert d_ff % tn == 0
    nj = d_ff // tn
    ni = S // tm
    hb = tm // BF16_SUBLANES
    n_hblk = S // BF16_SUBLANES
    kern = functools.partial(_ffn_up_kernel, tm=tm, n_row_tiles=ni)
    return pl.pallas_call(
        kern,
        grid=(ni, nj),
        in_specs=[
            pl.BlockSpec((tm, D), lambda i, j: (i, 0)),
            pl.BlockSpec((BF16_SUBLANES, D), lambda i, j: (jnp.maximum(i * hb - 1, 0), 0)),
            pl.BlockSpec((BF16_SUBLANES, D), lambda i, j: (jnp.minimum((i + 1) * hb, n_hblk - 1), 0)),
            pl.BlockSpec((D, tn), lambda i, j: (0, j)),
            pl.BlockSpec((D, tn), lambda i, j: (0, nj + j)),
            pl.BlockSpec((3, tn), lambda i, j: (0, j)),
            pl.BlockSpec((1, tn), lambda i, j: (0, j)),
        ],
        out_specs=pl.BlockSpec((tm, tn), lambda i, j: (i, j)),
        out_shape=jax.ShapeDtypeStruct((S, d_ff), BF16),
        scratch_shapes=[pltpu.VMEM((tm + 2 * BF16_SUBLANES, D), BF16)],
        compiler_params=pltpu.CompilerParams(
            dimension_semantics=("arbitrary", "arbitrary"), vmem_limit_bytes=VMEM_LIMIT),
        name="ffn_up",
    )(h2, h2, h2, w_bf, w_bf, conv_w, conv_b)


def _ffn_down_kernel(a_ref, w_ref, x1_ref, g_ref, o_ref, *, n_k):
    k = pl.program_id(1)

    @pl.when(k == 0)
    def _():
        o_ref[...] = x1_ref[...]

    o_ref[...] += jnp.dot(a_ref[...], w_ref[...], preferred_element_type=F32)

    @pl.when(k == n_k - 1)
    def _():
        y = o_ref[...]
        ms = jnp.mean(y * y, axis=-1, keepdims=True)
        o_ref[...] = y * lax.rsqrt(ms + NORM_EPS) * g_ref[...]


def _ffn_down(act, w_bf, x1, gain, *, tm=1024, tk=1408):
    S, d_ff = act.shape
    D = x1.shape[1]
    n_k = d_ff // tk
    kern = functools.partial(_ffn_down_kernel, n_k=n_k)
    return pl.pallas_call(
        kern,
        grid=(S // tm, n_k),
        in_specs=[
            pl.BlockSpec((tm, tk), lambda i, k: (i, k)),
            pl.BlockSpec((tk, D), lambda i, k: (k, 0)),
            pl.BlockSpec((tm, D), lambda i, k: (i, 0)),
            pl.BlockSpec((1, D), lambda i, k: (0, 0)),
        ],
        out_specs=pl.BlockSpec((tm, D), lambda i, k: (i, 0)),
        out_shape=jax.ShapeDtypeStruct((S, D), F32),
        compiler_params=pltpu.CompilerParams(
            dimension_semantics=("arbitrary", "arbitrary"), vmem_limit_bytes=VMEM_LIMIT),
        name="ffn_down",
    )(act, w_bf, x1, gain)


def kernel(x, norm1_gain, w_in, rel_bias_table, lambda_q1, lambda_k1, lambda_q2, lambda_k2,
           diff_subln_gain, dil_out_gain, w_out, norm2_gain, w_gate_up, conv_w, conv_b, w_down, final_gain):
    B, S, D = x.shape
    assert B == 1 and w_in.shape[0] == 1
    x2 = x.reshape(S, D)
    n_cols = w_in.shape[2]

    qscale = LOG2E / math.sqrt(HEAD_DIM)
    col = np.arange(n_cols)
    dil_q0 = 2 * DIFF_QK_COLS + DIFF_WIDTH
    is_q = (col < DIFF_QK_COLS) | ((col >= dil_q0) & (col < dil_q0 + DIL_WIDTH))
    colscale = jnp.asarray(np.where(is_q, qscale, 1.0).astype(np.float32)).reshape(1, n_cols)
    tab = rel_bias_table.astype(F32) * LOG2E

    regroup = tuple(d for _, d in DILATED_PATTERNS if d != 1)
    proj, qt_all, vt_all, *cls = _in_proj(x2, norm1_gain.reshape(1, D), w_in[0].astype(BF16), colscale, regroup)
    cls_by_dil = dict(zip(regroup, cls))

    lam = _lambda(lambda_q1.reshape(1, -1), lambda_k1.reshape(1, -1),
                  lambda_q2.reshape(1, -1), lambda_k2.reshape(1, -1))
    o_d, w_out_bf, w_gate_up_bf, w_down_bf = _diff_attention(
        tab, proj, qt_all, vt_all, lam, diff_subln_gain.reshape(-1, 1), (w_out[0], w_gate_up[0], w_down[0]))

    dil_blk0 = (2 * DIFF_QK_COLS + DIFF_WIDTH) // DIL_WIDTH
    dilations = tuple(d for _, d in DILATED_PATTERNS)
    dil_gain = dil_out_gain.reshape(1, -1)

    def pattern_inputs(dilation):
        if dilation == 1:
            return proj.reshape(1, S, proj.shape[1]), dil_blk0
        return cls_by_dil[dilation], 0

    kn2 = _key_norm2(proj, dil_blk0 + 1)
    nums, dens = [], []
    for window, dilation in DILATED_PATTERNS:
        n_p, d_p = _dilated_fast(tab, kn2, *pattern_inputs(dilation), window, dilation)
        nums.append(n_p)
        dens.append(d_p)
    o_l_fast, den_min = _combine_fast(nums, dens, dil_gain, dilations)

    def exact_dilated():
        outs, lses = [], []
        for window, dilation in DILATED_PATTERNS:
            o_p, lse_p = _dilated_pattern(tab, *pattern_inputs(dilation), window, dilation)
            outs.append(o_p)
            lses.append(lse_p)
        return _combine(outs, lses, dil_gain, dilations)

    def project(o_l):
        return _out_proj(o_d, o_l, w_out_bf, x2, norm2_gain.reshape(1, D))

    x1, h2 = lax.cond(jnp.min(den_min) < MIN_DENOMINATOR,
                      lambda: project(exact_dilated()), lambda: project(o_l_fast))
    act = _ffn_up(h2, w_gate_up_bf, conv_w[0], conv_b.reshape(1, -1))
    out = _ffn_down(act, w_down_bf, x1, final_gain.reshape(1, D))
    return out.reshape(B, S, D)
```

```python
import functools
import math

import numpy as np
import jax
import jax.numpy as jnp
from jax import lax
from jax.experimental import pallas as pl
from jax.experimental.pallas import tpu as pltpu

F32 = jnp.float32
BF16 = jnp.bfloat16

HEAD_DIM = 128
N_DIFF_HEADS = 4
DIFF_V_DIM = 2 * HEAD_DIM
N_DIL_HEADS = 8
DIFF_QK_COLS = N_DIFF_HEADS * 2 * HEAD_DIM
DIFF_WIDTH = N_DIFF_HEADS * DIFF_V_DIM
DIL_WIDTH = N_DIL_HEADS * HEAD_DIM
DILATED_PATTERNS = ((128, 1), (512, 4), (2048, 16))
N_REL_BUCKETS = 32
REL_MAX_DISTANCE = 1024
NORM_EPS = 1e-6
SUBLN_EPS = 1e-5
NEG_INF = -1e30
LOG2E = math.log2(math.e)
LAM_INIT = 0.8 - 0.6 * math.exp(-0.3 * 0)

LANES = 128
F32_SUBLANES = 8
BF16_SUBLANES = 16
DIFF_TILE = 1024
DIFF_QUERY_PANEL = 256
FAR_UNROLL = 4
SHIFT_MARGIN = 1.0 + 2.0 ** -8
MIN_DENOMINATOR = 2.0 ** -60
DIL_CHUNK = 1024
DIL_HEAD_UNROLL = 4
DIL_PROB_SLOTS = 4
VMEM_LIMIT = 56 * 1024 * 1024


def _bucket_breaks():
    nb = N_REL_BUCKETS // 2
    max_exact = nb // 2
    rel = np.arange(-2 * REL_MAX_DISTANCE, 2 * REL_MAX_DISTANCE + 1)
    n = np.abs(rel)
    pos = np.log(np.maximum(n, 1) / max_exact) / math.log(REL_MAX_DISTANCE / max_exact) * (nb - max_exact)
    large = np.minimum(max_exact + np.floor(pos).astype(np.int64), nb - 1)
    bucket = np.where(rel > 0, nb, 0) + np.where(n < max_exact, n, large)
    breaks = [(int(rel[i]), int(bucket[i])) for i in range(1, len(rel)) if bucket[i] != bucket[i - 1]]
    return int(bucket[0]), breaks


FIRST_BUCKET, BUCKET_BREAKS = _bucket_breaks()
LAST_BUCKET = BUCKET_BREAKS[-1][1]
FAR_DIST = max(-BUCKET_BREAKS[0][0] + 1, BUCKET_BREAKS[-1][0])


def _bias_from_rel(rel, tab_ref, col):
    val = jnp.full(rel.shape, tab_ref[FIRST_BUCKET, col], F32)
    for thr, b in BUCKET_BREAKS:
        val = jnp.where(rel >= thr, tab_ref[b, col], val)
    return val


N_COL_GROUPS = 6
FIRST_DIL_GROUP = 3


def _inproj_kernel(x_ref, g_ref, w_ref, cs_ref, o_ref, qt_ref, vt_ref, *rest, dilations):
    cls_refs = rest[:len(dilations)]
    h_ref, stage_ref = rest[len(dilations):]
    t = pl.program_id(1)
    tm = x_ref.shape[0]
    n_dil = N_COL_GROUPS - FIRST_DIL_GROUP

    @pl.when(t == 0)
    def _():
        x = x_ref[...]
        ms = jnp.mean(x * x, axis=-1, keepdims=True)
        h_ref[...] = (x * lax.rsqrt(ms + NORM_EPS) * g_ref[...]).astype(BF16)

    def matmul(keep):
        acc = jnp.dot(h_ref[...], w_ref[...], preferred_element_type=F32) * cs_ref[...]
        if keep:
            o_ref[...] = acc.astype(o_ref.dtype)
        return acc

    def stage(acc, slot):
        for cb in range(stage_ref.shape[1]):
            stage_ref[slot, cb] = acc[:, cb * LANES:(cb + 1) * LANES]

    def regroup(slot):
        for cb in range(stage_ref.shape[1]):
            sl = slice(cb * LANES, (cb + 1) * LANES)
            for cls_ref, dil in zip(cls_refs, dilations):
                for c in range(dil):
                    cls_ref[c, :, sl] = stage_ref[slot, cb, pl.ds(c, tm // dil, stride=dil), :].astype(cls_ref.dtype)

    for step in range(N_COL_GROUPS):
        @pl.when(t == step)
        def _(step=step):
            acc = matmul(keep=step not in (n_dil, n_dil + 2))
            if 1 <= step <= n_dil:
                regroup((step - 1) % 2)
            if step < n_dil:
                stage(acc, step % 2)
            if step == n_dil:
                qt_ref[...] = acc.T.astype(qt_ref.dtype)
            if step == n_dil + 2:
                vt_ref[...] = acc.T.astype(vt_ref.dtype)


def _in_proj(x2, gain, w_bf, colscale, dilations, *, tm=512):
    S, D = x2.shape
    N = w_bf.shape[1]
    tn = DIFF_QK_COLS
    assert DIFF_WIDTH == tn and DIL_WIDTH == tn and N == N_COL_GROUPS * tn
    n_dil = N_COL_GROUPS - FIRST_DIL_GROUP
    kern = functools.partial(_inproj_kernel, dilations=dilations)

    def group(t):
        return (t + FIRST_DIL_GROUP) % N_COL_GROUPS

    cls_specs = [pl.BlockSpec((d, tm // d, tn), lambda i, t: (0, i, jnp.clip(t - 1, 0, n_dil - 1)))
                 for d in dilations]
    cls_shapes = [jax.ShapeDtypeStruct((d, S // d, n_dil * tn), BF16) for d in dilations]
    return pl.pallas_call(
        kern,
        grid=(S // tm, N_COL_GROUPS),
        in_specs=[
            pl.BlockSpec((tm, D), lambda i, t: (i, 0)),
            pl.BlockSpec((1, D), lambda i, t: (0, 0)),
            pl.BlockSpec((D, tn), lambda i, t: (0, group(t))),
            pl.BlockSpec((1, tn), lambda i, t: (0, group(t))),
        ],
        out_specs=[
            pl.BlockSpec((tm, tn), lambda i, t: (i, jnp.where(t < n_dil, t + 1, 0))),
            pl.BlockSpec((tn, tm), lambda i, t: (0, i)),
            pl.BlockSpec((tn, tm), lambda i, t: (0, i)),
        ] + cls_specs,
        out_shape=[
            jax.ShapeDtypeStruct((S, (n_dil + 1) * tn), BF16),
            jax.ShapeDtypeStruct((tn, S), BF16),
            jax.ShapeDtypeStruct((tn, S), BF16),
        ] + cls_shapes,
        scratch_shapes=[pltpu.VMEM((tm, D), BF16), pltpu.VMEM((2, tn // LANES, tm, LANES), F32)],
        compiler_params=pltpu.CompilerParams(
            dimension_semantics=("arbitrary", "arbitrary"), vmem_limit_bytes=VMEM_LIMIT),
        name="in_proj",
    )(x2, gain, w_bf, colscale)


def _diff_attn_kernel(*refs, T, QP, n_near, n_tiles, n_cast):
    tab_ref, q1t_ref, q2t_ref, k1_ref, k2_ref, vt_ref, lam_ref, gain_ref = refs[:8]
    cast_in, o_ref, cast_out = refs[8:8 + n_cast], refs[8 + n_cast], refs[9 + n_cast:9 + 2 * n_cast]
    bias_ref, m_ref, l_ref, acc_ref, p_ref, shift_ref, knorm_ref, prev_ref = refs[9 + 2 * n_cast:]
    _diff_attn_body(tab_ref, q1t_ref, q2t_ref, k1_ref, k2_ref, vt_ref, lam_ref, gain_ref, o_ref,
                    bias_ref, m_ref, l_ref, acc_ref, p_ref, shift_ref, knorm_ref, prev_ref,
                    T=T, QP=QP, n_near=n_near, n_tiles=n_tiles)
    for src, dst in zip(cast_in, cast_out):
        dst[...] = src[...].astype(dst.dtype)


def _diff_attn_body(tab_ref, q1t_ref, q2t_ref, k1_ref, k2_ref, vt_ref, lam_ref, gain_ref, o_ref,
                    bias_ref, m_ref, l_ref, acc_ref, p_ref, shift_ref, knorm_ref, prev_ref,
                    *, T, QP, n_near, n_tiles):
    h = pl.program_id(0)
    qi = pl.program_id(1)
    n_chains = 2 * (T // QP)
    qts = (q1t_ref, q2t_ref)
    ks = (k1_ref, k2_ref)

    def chain_of(i):
        qp = i // 2
        return i % 2, slice(qp * QP, (qp + 1) * QP)

    @pl.when(qi == 0)
    def _():
        for mi in range(2):
            def knorm(t, best, mi=mi):
                k = ks[mi][pl.ds(pl.multiple_of(t * T, T), T), :].astype(F32)
                return jnp.maximum(best, jnp.max(jnp.sum(k * k, axis=1, keepdims=True), axis=0, keepdims=True))
            knorm_ref[mi] = lax.fori_loop(0, n_tiles, knorm, jnp.zeros((1, 1), F32))

        x = lax.broadcasted_iota(jnp.int32, (F32_SUBLANES, 2 * T), 1)
        x = jnp.where(x < T, x, x - 2 * T)
        for di, d in enumerate(range(-n_near, n_near + 1)):
            g = _bias_from_rel(d * T - x, tab_ref, h)
            base = jnp.broadcast_to(g[0:1, :], (LANES, 2 * T))
            for rb in range(T // LANES):
                blk = pltpu.roll(base, rb * LANES, 1, stride=1, stride_axis=0)
                bias_ref[di, rb * LANES:(rb + 1) * LANES, :] = blk[:, :T]

    c_left = tab_ref[FIRST_BUCKET, h]
    c_right = tab_ref[LAST_BUCKET, h]

    bias_max = tab_ref[0, h]
    for b in range(1, N_REL_BUCKETS):
        bias_max = jnp.maximum(bias_max, tab_ref[b, h])

    for mi in range(2):
        q = qts[mi][...].astype(F32)
        qnorm2 = jnp.sum(q * q, axis=0, keepdims=True)
        shift_ref[mi] = jnp.sqrt(qnorm2 * knorm_ref[mi]) * SHIFT_MARGIN + bias_max
    l_ref[...] = jnp.zeros(l_ref.shape, F32)
    acc_ref[...] = jnp.zeros(acc_ref.shape, F32)
    p_ref[n_chains - 1] = jnp.zeros((T, QP), BF16)
    prev_ref[0] = 0

    def score_stage(i, k0, bias_di, bias_const):
        mi, qs = chain_of(i)
        s = jnp.dot(ks[mi][pl.ds(k0, T), :], qts[mi][:, qs], preferred_element_type=F32)
        if bias_di is not None:
            p = jnp.exp2(s + bias_ref[bias_di, :, qs] - shift_ref[mi, :, qs])
        else:
            p = jnp.exp2(s - (shift_ref[mi, :, qs] - bias_const))
        l_ref[mi, :, qs] += jnp.sum(p, axis=0, keepdims=True)
        p_ref[i] = p.astype(BF16)

    def value_stage(i, k0):
        mi, qs = chain_of(i)
        acc_ref[mi, :, qs] += jnp.dot(vt_ref[:, pl.ds(k0, T)], p_ref[i], preferred_element_type=F32)

    def tile(kt, bias_di, bias_const, prev_kt=None):
        k0 = pl.multiple_of(kt * T, T)
        pk0 = pl.multiple_of((prev_ref[0] if prev_kt is None else prev_kt) * T, T)
        for i in range(n_chains):
            value_stage((i - 1) % n_chains, pk0 if i < 1 else k0)
            score_stage(i, k0, bias_di, bias_const)
        prev_ref[0] = kt

    lo = jnp.maximum(qi - n_near, 0)
    hi = jnp.minimum(qi + n_near + 1, n_tiles)

    for di, d in enumerate(range(-n_near, n_near + 1)):
        kt = qi + d

        @pl.when(jnp.logical_and(kt >= 0, kt < n_tiles))
        def _(di=di, kt=kt):
            tile(kt, di, None)

    n_far = n_tiles - (hi - lo)

    def far_run(first, length):
        prev_kt = None
        for j in range(length):
            before = first + j < lo
            kt = jnp.where(before, first + j, first + j + (hi - lo))
            tile(kt, None, jnp.where(before, c_left, c_right), prev_kt=prev_kt)
            prev_kt = kt

    done = 0
    run = 1
    while run < FAR_UNROLL:
        has_run = jnp.bitwise_and(n_far, run)

        @pl.when(has_run != 0)
        def _(done=done, run=run):
            far_run(done, run)

        done = done + has_run
        run *= 2

    @pl.loop(0, n_far // FAR_UNROLL)
    def _(trip, done=done):
        far_run(done + trip * FAR_UNROLL, FAR_UNROLL)

    value_stage(n_chains - 1, pl.multiple_of(prev_ref[0] * T, T))

    @pl.when(jnp.min(l_ref[...]) < MIN_DENOMINATOR)
    def _():
        m_ref[...] = jnp.full(m_ref.shape, -jnp.inf, F32)
        l_ref[...] = jnp.zeros(l_ref.shape, F32)
        acc_ref[...] = jnp.zeros(acc_ref.shape, F32)

        @pl.loop(0, n_tiles)
        def _(kt):
            k0 = pl.multiple_of(kt * T, T)
            d = kt - qi
            near = jnp.abs(d) <= n_near
            di = jnp.clip(d + n_near, 0, 2 * n_near)
            c_far = jnp.where(d < 0, c_left, c_right)
            for i in range(n_chains):
                mi, qs = chain_of(i)
                s = jnp.dot(ks[mi][pl.ds(k0, T), :], qts[mi][:, qs], preferred_element_type=F32)
                s = s + jnp.where(near, bias_ref[di, :, qs], c_far)
                m = m_ref[mi, :, qs]
                m_new = jnp.maximum(m, jnp.max(s, axis=0, keepdims=True))
                alpha = jnp.exp2(m - m_new)
                p = jnp.exp2(s - m_new)
                m_ref[mi, :, qs] = m_new
                l_ref[mi, :, qs] = alpha * l_ref[mi, :, qs] + jnp.sum(p, axis=0, keepdims=True)
                pv = jnp.dot(vt_ref[:, pl.ds(k0, T)], p.astype(BF16), preferred_element_type=F32)
                acc_ref[mi, :, qs] = alpha * acc_ref[mi, :, qs] + pv

    lam = lam_ref[0, 0]
    o = acc_ref[0] / l_ref[0] - lam * (acc_ref[1] / l_ref[1])
    ms = jnp.mean(o * o, axis=0, keepdims=True)
    o = o * lax.rsqrt(ms + SUBLN_EPS) * (gain_ref[...] * (1.0 - LAM_INIT))
    o_ref[...] = o.T.astype(o_ref.dtype)


def _lambda_kernel(q1_ref, k1_ref, q2_ref, k2_ref, o_ref):
    a = jnp.sum(q1_ref[...] * k1_ref[...], axis=-1, keepdims=True)
    b = jnp.sum(q2_ref[...] * k2_ref[...], axis=-1, keepdims=True)
    o_ref[...] = jnp.exp(a) - jnp.exp(b) + LAM_INIT


def _lambda(lq1, lk1, lq2, lk2):
    return pl.pallas_call(
        _lambda_kernel, out_shape=jax.ShapeDtypeStruct((1, 1), F32), name="diff_lambda",
    )(lq1, lk1, lq2, lk2)


def _diff_attention(tab_diff, proj, qt_all, vt_all, lam, gain_col, f32_weights):
    S = proj.shape[0]
    T = DIFF_TILE
    n_tiles = S // T
    n_near = -(-(FAR_DIST - 1) // T)
    kern = functools.partial(_diff_attn_kernel, T=T, QP=DIFF_QUERY_PANEL, n_near=n_near, n_tiles=n_tiles,
                             n_cast=len(f32_weights))

    n_steps = N_DIFF_HEADS * n_tiles
    cast_specs = []
    for w in f32_weights:
        rows, cols = w.shape
        every = next(e for e in (1, 2, 4, 8) if rows % (n_steps // e) == 0
                     and (rows // (n_steps // e)) % BF16_SUBLANES == 0)
        cast_specs.append(pl.BlockSpec((rows // (n_steps // every), cols),
                                       lambda h, i, every=every: ((h * n_tiles + i) // every, 0)))
    k_block0 = 0
    smem = pl.BlockSpec(memory_space=pltpu.SMEM)
    return pl.pallas_call(
        kern,
        grid=(N_DIFF_HEADS, n_tiles),
        in_specs=[
            smem,
            pl.BlockSpec((HEAD_DIM, T), lambda h, i: (2 * h, i)),
            pl.BlockSpec((HEAD_DIM, T), lambda h, i: (2 * h + 1, i)),
            pl.BlockSpec((S, HEAD_DIM), lambda h, i: (0, k_block0 + 2 * h), pipeline_mode=pl.Buffered(1)),
            pl.BlockSpec((S, HEAD_DIM), lambda h, i: (0, k_block0 + 2 * h + 1), pipeline_mode=pl.Buffered(1)),
            pl.BlockSpec((DIFF_V_DIM, S), lambda h, i: (h, 0), pipeline_mode=pl.Buffered(1)),
            smem,
            pl.BlockSpec((DIFF_V_DIM, 1), lambda h, i: (0, 0)),
        ] + cast_specs,
        out_specs=[pl.BlockSpec((T, DIFF_V_DIM), lambda h, i: (i, h))] + cast_specs,
        out_shape=[jax.ShapeDtypeStruct((S, DIFF_WIDTH), BF16)]
        + [jax.ShapeDtypeStruct(w.shape, BF16) for w in f32_weights],
        scratch_shapes=[pltpu.VMEM((2 * n_near + 1, T, T), F32),
                        pltpu.VMEM((2, 1, T), F32), pltpu.VMEM((2, 1, T), F32),
                        pltpu.VMEM((2, DIFF_V_DIM, T), F32),
                        pltpu.VMEM((2 * (T // DIFF_QUERY_PANEL), T, DIFF_QUERY_PANEL), BF16),
                        pltpu.VMEM((2, 1, T), F32), pltpu.VMEM((2, 1, 1), F32),
                        pltpu.SMEM((1,), jnp.int32)],
        compiler_params=pltpu.CompilerParams(
            dimension_semantics=("arbitrary", "arbitrary"), vmem_limit_bytes=VMEM_LIMIT),
        name="diff_attn",
    )(tab_diff, qt_all, qt_all, proj, proj, vt_all, lam, gain_col, *f32_weights)


def _dilated_kernel(tab_ref, q_ref, kp_ref, km_ref, kn_ref, vp_ref, vm_ref, vn_ref, o_ref, lse_ref,
                    bias_ref, kx_ref, vx_ref, *, R, B, half, dilation, n_chunks):
    c = pl.program_id(0)
    n = pl.program_id(1)
    W = B + 2 * half
    nblk = R // B

    @pl.when(jnp.logical_and(c == 0, n == 0))
    def _():
        rows = F32_SUBLANES
        col = lax.broadcasted_iota(jnp.int32, (rows, W), 1)
        row = lax.broadcasted_iota(jnp.int32, (rows, W), 0)
        for hh in range(N_DIL_HEADS):
            def fill(r, carry, hh=hh):
                r0 = pl.multiple_of(r * rows, rows)
                off = col - half - (row + r0)
                bias = _bias_from_rel(off * dilation, tab_ref, N_DIFF_HEADS + hh)
                base = jnp.where(jnp.abs(off) <= half, bias, NEG_INF * LOG2E)
                bias_ref[hh, 1, pl.ds(r0, rows), :] = base
                bias_ref[hh, 0, pl.ds(r0, rows), :] = jnp.where(col >= half, base, NEG_INF * LOG2E)
                bias_ref[hh, 2, pl.ds(r0, rows), :] = jnp.where(col < B + half, base, NEG_INF * LOG2E)
                return carry
            lax.fori_loop(0, B // rows, fill, 0)

    kx_ref[0:half, :] = kp_ref[...]
    kx_ref[half:half + R, :] = km_ref[...]
    kx_ref[half + R:, :] = kn_ref[...]
    vx_ref[0:half, :] = vp_ref[...]
    vx_ref[half:half + R, :] = vm_ref[...]
    vx_ref[half + R:, :] = vn_ref[...]

    def chain(hh, b):
        c0 = pl.multiple_of(hh * HEAD_DIM, HEAD_DIM)
        r0 = b * B
        var = 1
        if b == 0:
            var = jnp.where(n == 0, 0, var)
        if b == nblk - 1:
            var = jnp.where(n == n_chunks - 1, 2, var)
        q = q_ref[pl.ds(r0, B), pl.ds(c0, HEAD_DIM)]
        k = kx_ref[pl.ds(r0, W), pl.ds(c0, HEAD_DIM)]
        v = vx_ref[pl.ds(r0, W), pl.ds(c0, HEAD_DIM)]
        s = lax.dot_general(q, k, (((1,), (1,)), ((), ())), preferred_element_type=F32)
        s = s + bias_ref[hh, var]
        m = jnp.max(s, axis=-1, keepdims=True)
        e = jnp.exp2(s - m)
        den = jnp.sum(e, axis=-1, keepdims=True)
        o = jnp.dot(e.astype(BF16), v, preferred_element_type=F32) / den
        o_ref[pl.ds(r0, B), pl.ds(c0, HEAD_DIM)] = o.astype(o_ref.dtype)
        lse = m + jnp.log2(den)
        lse_ref[pl.ds(r0, B), pl.ds(c0, HEAD_DIM)] = jnp.broadcast_to(lse, (B, HEAD_DIM))

    @pl.loop(0, N_DIL_HEADS // DIL_HEAD_UNROLL)
    def _(hg):
        for u in range(DIL_HEAD_UNROLL):
            for b in range(nblk):
                chain(hg * DIL_HEAD_UNROLL + u, b)


def _dilated_pattern(tab, qkv, col_blk0, window, dilation, *, B=256):
    _, L, _ = qkv.shape
    R = min(DIL_CHUNK, L)
    half = window // (2 * dilation)
    assert L % R == 0 and R % B == 0 and half % BF16_SUBLANES == 0 and R % half == 0
    n_chunks = L // R
    q_blk, k_blk, v_blk = col_blk0, col_blk0 + 1, col_blk0 + 2
    hb = R // half
    n_hblk = L // half

    def main(blk):
        return pl.BlockSpec((None, R, DIL_WIDTH), lambda c, n: (c, n, blk))

    def prev(blk):
        return pl.BlockSpec((None, half, DIL_WIDTH), lambda c, n: (c, jnp.maximum(n * hb - 1, 0), blk))

    def nxt(blk):
        return pl.BlockSpec((None, half, DIL_WIDTH), lambda c, n: (c, jnp.minimum((n + 1) * hb, n_hblk - 1), blk))

    kern = functools.partial(_dilated_kernel, R=R, B=B, half=half, dilation=dilation, n_chunks=n_chunks)
    out_spec = pl.BlockSpec((None, R, DIL_WIDTH), lambda c, n: (c, n, 0))
    return pl.pallas_call(
        kern,
        grid=(dilation, n_chunks),
        in_specs=[pl.BlockSpec(memory_space=pltpu.SMEM),
                  main(q_blk), prev(k_blk), main(k_blk), nxt(k_blk), prev(v_blk), main(v_blk), nxt(v_blk)],
        out_specs=[out_spec, out_spec],
        out_shape=[jax.ShapeDtypeStruct((dilation, L, DIL_WIDTH), BF16),
                   jax.ShapeDtypeStruct((dilation, L, DIL_WIDTH), F32)],
        scratch_shapes=[pltpu.VMEM((N_DIL_HEADS, 3, B, B + 2 * half), F32),
                        pltpu.VMEM((R + 2 * half, DIL_WIDTH), BF16),
                        pltpu.VMEM((R + 2 * half, DIL_WIDTH), BF16)],
        compiler_params=pltpu.CompilerParams(
            dimension_semantics=("arbitrary", "arbitrary"), vmem_limit_bytes=VMEM_LIMIT),
        name=f"dilated_d{dilation}",
    )(tab, qkv, qkv, qkv, qkv, qkv, qkv, qkv)


def _combine_kernel(*refs, dilations):
    n = len(dilations)
    o_refs, l_refs = refs[:n], refs[n:2 * n]
    g_ref, out_ref = refs[2 * n], refs[2 * n + 1]
    scratch = refs[2 * n + 2:]
    tm = out_ref.shape[0]

    for hh in range(N_DIL_HEADS):
        sl = slice(hh * HEAD_DIM, (hh + 1) * HEAD_DIM)
        outs, lses = [], []
        si = 0
        for o_ref, l_ref, dil in zip(o_refs, l_refs, dilations):
            if dil == 1:
                outs.append(o_ref[0, :, sl].astype(F32))
                lses.append(l_ref[0, :, sl])
                continue
            os_ref, ls_ref = scratch[si], scratch[si + 1]
            si += 2
            for c in range(dil):
                os_ref[hh, pl.ds(c, tm // dil, stride=dil), :] = o_ref[c, :, sl].astype(F32)
                ls_ref[hh, pl.ds(c, tm // dil, stride=dil), :] = l_ref[c, :, sl]
            outs.append(os_ref[hh])
            lses.append(ls_ref[hh])

        m = functools.reduce(jnp.maximum, lses)
        ws = [jnp.exp2(l - m) for l in lses]
        tot = functools.reduce(lambda a, b: a + b, ws)
        oh = functools.reduce(lambda a, b: a + b, [(w / tot) * op for w, op in zip(ws, outs)])
        ms = jnp.mean(oh * oh, axis=-1, keepdims=True)
        out_ref[:, sl] = (oh * lax.rsqrt(ms + NORM_EPS) * g_ref[:, sl]).astype(out_ref.dtype)


def _combine(outs, lses, gain, dilations, *, tm=512):
    S = outs[0].shape[0] * outs[0].shape[1]
    specs = [pl.BlockSpec((d, tm // d, DIL_WIDTH), lambda i: (0, i, 0)) for d in dilations]
    n_scr = sum(1 for d in dilations if d != 1)
    return pl.pallas_call(
        functools.partial(_combine_kernel, dilations=dilations),
        grid=(S // tm,),
        in_specs=specs + specs + [pl.BlockSpec((1, DIL_WIDTH), lambda i: (0, 0))],
        scratch_shapes=[pltpu.VMEM((N_DIL_HEADS, tm, HEAD_DIM), F32)] * (2 * n_scr),
        out_specs=pl.BlockSpec((tm, DIL_WIDTH), lambda i: (i, 0)),
        out_shape=jax.ShapeDtypeStruct((S, DIL_WIDTH), BF16),
        compiler_params=pltpu.CompilerParams(
            dimension_semantics=("arbitrary",), vmem_limit_bytes=VMEM_LIMIT),
        name="dilated_combine",
    )(*outs, *lses, gain)


def _key_norm2_kernel(k_ref, o_ref):
    @pl.when(pl.program_id(0) == 0)
    def _():
        o_ref[...] = jnp.zeros(o_ref.shape, F32)

    k = k_ref[...].astype(F32)
    for hh in range(N_DIL_HEADS):
        kh = k[:, hh * HEAD_DIM:(hh + 1) * HEAD_DIM]
        n2 = jnp.max(jnp.sum(kh * kh, axis=1, keepdims=True), axis=0, keepdims=True)
        o_ref[hh:hh + 1, :] = jnp.maximum(o_ref[hh:hh + 1, :], jnp.broadcast_to(n2, (1, LANES)))


def _key_norm2(proj, k_blk, *, tm=1024):
    S = proj.shape[0]
    return pl.pallas_call(
        _key_norm2_kernel,
        grid=(S // tm,),
        in_specs=[pl.BlockSpec((tm, DIL_WIDTH), lambda i: (i, k_blk))],
        out_specs=pl.BlockSpec((N_DIL_HEADS, LANES), lambda i: (0, 0)),
        out_shape=jax.ShapeDtypeStruct((N_DIL_HEADS, LANES), F32),
        compiler_params=pltpu.CompilerParams(dimension_semantics=("arbitrary",), vmem_limit_bytes=VMEM_LIMIT),
        name="dilated_key_norm",
    )(proj)


def _dilated_fast_kernel(tab_ref, kn2_ref, q_ref, kp_ref, km_ref, kn_ref, vp_ref, vm_ref, vn_ref, num_ref, den_ref,
                         bias_ref, kx_ref, vx_ref, p_ref, *, R, B, half, dilation, n_chunks):
    c = pl.program_id(0)
    n = pl.program_id(1)
    W = B + 2 * half
    nblk = R // B
    VW = 2 * HEAD_DIM

    @pl.when(jnp.logical_and(c == 0, n == 0))
    def _():
        rows = F32_SUBLANES
        col = lax.broadcasted_iota(jnp.int32, (rows, W), 1)
        row = lax.broadcasted_iota(jnp.int32, (rows, W), 0)
        for hh in range(N_DIL_HEADS):
            def fill(r, carry, hh=hh):
                r0 = pl.multiple_of(r * rows, rows)
                off = col - half - (row + r0)
                bias = _bias_from_rel(off * dilation, tab_ref, N_DIFF_HEADS + hh)
                base = jnp.where(jnp.abs(off) <= half, bias, NEG_INF * LOG2E)
                bias_ref[hh, 1, pl.ds(r0, rows), :] = base
                bias_ref[hh, 0, pl.ds(r0, rows), :] = jnp.where(col >= half, base, NEG_INF * LOG2E)
                bias_ref[hh, 2, pl.ds(r0, rows), :] = jnp.where(col < B + half, base, NEG_INF * LOG2E)
                return carry
            lax.fori_loop(0, B // rows, fill, 0)
        vx_ref[...] = jnp.ones(vx_ref.shape, BF16)

    kx_ref[0:half, :] = kp_ref[...]
    kx_ref[half:half + R, :] = km_ref[...]
    kx_ref[half + R:, :] = kn_ref[...]
    for hh in range(N_DIL_HEADS):
        src = slice(hh * HEAD_DIM, (hh + 1) * HEAD_DIM)
        dst = slice(hh * VW, hh * VW + HEAD_DIM)
        vx_ref[0:half, dst] = vp_ref[:, src]
        vx_ref[half:half + R, dst] = vm_ref[:, src]
        vx_ref[half + R:, dst] = vn_ref[:, src]

    den_ref[...] = jnp.zeros(den_ref.shape, F32)
    n_slots = p_ref.shape[0]
    bias_max = []
    for hh in range(N_DIL_HEADS):
        bm = tab_ref[0, N_DIFF_HEADS + hh]
        for bk in range(1, N_REL_BUCKETS):
            bm = jnp.maximum(bm, tab_ref[bk, N_DIFF_HEADS + hh])
        bias_max.append(bm)

    def score_stage(hh, b, slot):
        cs = slice(hh * HEAD_DIM, (hh + 1) * HEAD_DIM)
        r0 = b * B
        var = 1
        if b == 0:
            var = jnp.where(n == 0, 0, var)
        if b == nblk - 1:
            var = jnp.where(n == n_chunks - 1, 2, var)
        q = q_ref[r0:r0 + B, cs]
        qf = q.astype(F32)
        qn2 = jnp.sum(qf * qf, axis=1, keepdims=True)
        shift = jnp.sqrt(qn2 * kn2_ref[hh:hh + 1, 0:1]) * SHIFT_MARGIN + bias_max[hh]
        s = lax.dot_general(q, kx_ref[r0:r0 + W, cs], (((1,), (1,)), ((), ())), preferred_element_type=F32)
        p_ref[slot] = jnp.exp2(s + bias_ref[hh, var] - shift).astype(BF16)

    def value_stage(hh, b, slot):
        cs = slice(hh * HEAD_DIM, (hh + 1) * HEAD_DIM)
        r0 = b * B
        nd = jnp.dot(p_ref[slot], vx_ref[r0:r0 + W, hh * VW:(hh + 1) * VW], preferred_element_type=F32)
        num_ref[r0:r0 + B, cs] = nd[:, :HEAD_DIM].astype(num_ref.dtype)
        den_ref[r0:r0 + B, hh:hh + 1] = nd[:, HEAD_DIM + hh:HEAD_DIM + hh + 1]

    chains = [(hh, b) for hh in range(N_DIL_HEADS) for b in range(nblk)]
    for i, (hh, b) in enumerate(chains):
        if i > 0:
            value_stage(*chains[i - 1], (i - 1) % n_slots)
        score_stage(hh, b, i % n_slots)
    value_stage(*chains[-1], (len(chains) - 1) % n_slots)


def _dilated_fast(tab, kn2, qkv, col_blk0, window, dilation, *, B=128):
    _, L, _ = qkv.shape
    R = min(DIL_CHUNK, L)
    half = window // (2 * dilation)
    assert L % R == 0 and R % B == 0 and half % BF16_SUBLANES == 0 and R % half == 0
    n_chunks = L // R
    q_blk, k_blk, v_blk = col_blk0, col_blk0 + 1, col_blk0 + 2
    hb = R // half
    n_hblk = L // half

    def main(blk):
        return pl.BlockSpec((None, R, DIL_WIDTH), lambda c, n: (c, n, blk))

    def prev(blk):
        return pl.BlockSpec((None, half, DIL_WIDTH), lambda c, n: (c, jnp.maximum(n * hb - 1, 0), blk))

    def nxt(blk):
        return pl.BlockSpec((None, half, DIL_WIDTH), lambda c, n: (c, jnp.minimum((n + 1) * hb, n_hblk - 1), blk))

    kern = functools.partial(_dilated_fast_kernel, R=R, B=B, half=half, dilation=dilation, n_chunks=n_chunks)
    out_spec = pl.BlockSpec((None, R, DIL_WIDTH), lambda c, n: (c, n, 0))
    return pl.pallas_call(
        kern,
        grid=(dilation, n_chunks),
        in_specs=[pl.BlockSpec(memory_space=pltpu.SMEM),
                  pl.BlockSpec((N_DIL_HEADS, LANES), lambda c, n: (0, 0)),
                  main(q_blk), prev(k_blk), main(k_blk), nxt(k_blk), prev(v_blk), main(v_blk), nxt(v_blk)],
        out_specs=[out_spec, pl.BlockSpec((None, R, LANES), lambda c, n: (c, n, 0))],
        out_shape=[jax.ShapeDtypeStruct((dilation, L, DIL_WIDTH), BF16),
                   jax.ShapeDtypeStruct((dilation, L, LANES), F32)],
        scratch_shapes=[pltpu.VMEM((N_DIL_HEADS, 3, B, B + 2 * half), F32),
                        pltpu.VMEM((R + 2 * half, DIL_WIDTH), BF16),
                        pltpu.VMEM((R + 2 * half, 2 * DIL_WIDTH), BF16),
                        pltpu.VMEM((DIL_PROB_SLOTS, B, B + 2 * half), BF16)],
        compiler_params=pltpu.CompilerParams(
            dimension_semantics=("arbitrary", "arbitrary"), vmem_limit_bytes=VMEM_LIMIT),
        name=f"dilated_fast_d{dilation}",
    )(tab, kn2, qkv, qkv, qkv, qkv, qkv, qkv, qkv)


def _combine_fast_kernel(*refs, dilations):
    n = len(dilations)
    n_refs, d_refs = refs[:n], refs[n:2 * n]
    g_ref, out_ref, dmin_ref = refs[2 * n], refs[2 * n + 1], refs[2 * n + 2]
    scratch = refs[2 * n + 3:]
    tm = out_ref.shape[0]

    den_all = None
    si = 0
    for d_ref, dil in zip(d_refs, dilations):
        if dil == 1:
            den_p = d_ref[0]
        else:
            ds_ref = scratch[si + 1]
            for c in range(dil):
                ds_ref[pl.ds(c, tm // dil, stride=dil), :] = d_ref[c]
            den_p = ds_ref[...]
            si += 2
        den_all = den_p if den_all is None else den_all + den_p

    for hh in range(N_DIL_HEADS):
        sl = slice(hh * HEAD_DIM, (hh + 1) * HEAD_DIM)
        nums = []
        si = 0
        for n_ref, dil in zip(n_refs, dilations):
            if dil == 1:
                nums.append(n_ref[0, :, sl].astype(F32))
                continue
            ns_ref = scratch[si]
            si += 2
            for c in range(dil):
                ns_ref[hh, pl.ds(c, tm // dil, stride=dil), :] = n_ref[c, :, sl].astype(F32)
            nums.append(ns_ref[hh])

        den = den_all[:, hh:hh + 1]
        oh = functools.reduce(lambda a, b: a + b, nums) / den
        ms = jnp.mean(oh * oh, axis=-1, keepdims=True)
        out_ref[:, sl] = (oh * lax.rsqrt(ms + NORM_EPS) * g_ref[:, sl]).astype(out_ref.dtype)
        dmin_ref[0, hh:hh + 1, :] = jnp.broadcast_to(jnp.min(den, axis=0, keepdims=True), (1, LANES))


def _combine_fast(nums, dens, gain, dilations, *, tm=512):
    S = nums[0].shape[0] * nums[0].shape[1]
    num_specs = [pl.BlockSpec((d, tm // d, DIL_WIDTH), lambda i: (0, i, 0)) for d in dilations]
    den_specs = [pl.BlockSpec((d, tm // d, LANES), lambda i: (0, i, 0)) for d in dilations]
    n_scr = sum(1 for d in dilations if d != 1)
    return pl.pallas_call(
        functools.partial(_combine_fast_kernel, dilations=dilations),
        grid=(S // tm,),
        in_specs=num_specs + den_specs + [pl.BlockSpec((1, DIL_WIDTH), lambda i: (0, 0))],
        scratch_shapes=[pltpu.VMEM((N_DIL_HEADS, tm, HEAD_DIM), F32), pltpu.VMEM((tm, LANES), F32)] * n_scr,
        out_specs=[pl.BlockSpec((tm, DIL_WIDTH), lambda i: (i, 0)),
                   pl.BlockSpec((1, N_DIL_HEADS, LANES), lambda i: (i, 0, 0))],
        out_shape=[jax.ShapeDtypeStruct((S, DIL_WIDTH), BF16),
                   jax.ShapeDtypeStruct((S // tm, N_DIL_HEADS, LANES), F32)],
        compiler_params=pltpu.CompilerParams(
            dimension_semantics=("arbitrary",), vmem_limit_bytes=VMEM_LIMIT),
        name="dilated_combine_fast",
    )(*nums, *dens, gain)


def _outproj_kernel(od_ref, ol_ref, wd_ref, wl_ref, x_ref, g_ref, x1_ref, h2_ref):
    acc = jnp.dot(od_ref[...], wd_ref[...], preferred_element_type=F32)
    acc = acc + jnp.dot(ol_ref[...], wl_ref[...], preferred_element_type=F32)
    x1 = x_ref[...] + acc
    x1_ref[...] = x1
    ms = jnp.mean(x1 * x1, axis=-1, keepdims=True)
    h2_ref[...] = (x1 * lax.rsqrt(ms + NORM_EPS) * g_ref[...]).astype(h2_ref.dtype)


def _out_proj(o_d, o_l, w_bf, x2, gain, *, tm=512):
    S, D = x2.shape
    return pl.pallas_call(
        _outproj_kernel,
        grid=(S // tm,),
        in_specs=[
            pl.BlockSpec((tm, DIFF_WIDTH), lambda i: (i, 0)),
            pl.BlockSpec((tm, DIL_WIDTH), lambda i: (i, 0)),
            pl.BlockSpec((DIFF_WIDTH, D), lambda i: (0, 0)),
            pl.BlockSpec((DIL_WIDTH, D), lambda i: (1, 0)),
            pl.BlockSpec((tm, D), lambda i: (i, 0)),
            pl.BlockSpec((1, D), lambda i: (0, 0)),
        ],
        out_specs=[pl.BlockSpec((tm, D), lambda i: (i, 0)), pl.BlockSpec((tm, D), lambda i: (i, 0))],
        out_shape=[jax.ShapeDtypeStruct((S, D), F32), jax.ShapeDtypeStruct((S, D), BF16)],
        compiler_params=pltpu.CompilerParams(
            dimension_semantics=("arbitrary",), vmem_limit_bytes=VMEM_LIMIT),
        name="out_proj",
    )(o_d, o_l, w_bf, w_bf, x2, gain)


def _ffn_up_kernel(hm_ref, hp_ref, hn_ref, wg_ref, wu_ref, cw_ref, cb_ref, o_ref, lhs_ref, *, tm, n_row_tiles):
    i = pl.program_id(0)
    j = pl.program_id(1)
    halo = BF16_SUBLANES

    @pl.when(j == 0)
    def _():
        lhs_ref[0:halo, :] = jnp.where(i == 0, jnp.zeros_like(hp_ref[...]), hp_ref[...])
        lhs_ref[halo:halo + tm, :] = hm_ref[...]
        lhs_ref[halo + tm:, :] = jnp.where(i == n_row_tiles - 1, jnp.zeros_like(hn_ref[...]), hn_ref[...])

    g = jnp.dot(lhs_ref[...], wg_ref[...], preferred_element_type=F32)
    u = jnp.dot(lhs_ref[halo:halo + tm, :], wu_ref[...], preferred_element_type=F32)
    rows = tm + 2 * halo
    g_prev = pltpu.roll(g, 1, axis=0)
    g_next = pltpu.roll(g, rows - 1, axis=0)
    y = cw_ref[0:1, :] * g_prev + cw_ref[1:2, :] * g + cw_ref[2:3, :] * g_next + cb_ref[...]
    y = y[halo:halo + tm, :]
    act = y * (1.0 / (1.0 + jnp.exp(-y))) * u
    o_ref[...] = act.astype(o_ref.dtype)


def _ffn_up(h2, w_bf, conv_w, conv_b, *, tm=1024, tn=512):
    S, D = h2.shape
    d_ff = conv_w.shape[1]
    assert d_ff % tn == 0
    nj = d_ff // tn
    ni = S // tm
    hb = tm // BF16_SUBLANES
    n_hblk = S // BF16_SUBLANES
    kern = functools.partial(_ffn_up_kernel, tm=tm, n_row_tiles=ni)
    return pl.pallas_call(
        kern,
        grid=(ni, nj),
        in_specs=[
            pl.BlockSpec((tm, D), lambda i, j: (i, 0)),
            pl.BlockSpec((BF16_SUBLANES, D), lambda i, j: (jnp.maximum(i * hb - 1, 0), 0)),
            pl.BlockSpec((BF16_SUBLANES, D), lambda i, j: (jnp.minimum((i + 1) * hb, n_hblk - 1), 0)),
            pl.BlockSpec((D, tn), lambda i, j: (0, j)),
            pl.BlockSpec((D, tn), lambda i, j: (0, nj + j)),
            pl.BlockSpec((3, tn), lambda i, j: (0, j)),
            pl.BlockSpec((1, tn), lambda i, j: (0, j)),
        ],
        out_specs=pl.BlockSpec((tm, tn), lambda i, j: (i, j)),
        out_shape=jax.ShapeDtypeStruct((S, d_ff), BF16),
        scratch_shapes=[pltpu.VMEM((tm + 2 * BF16_SUBLANES, D), BF16)],
        compiler_params=pltpu.CompilerParams(
            dimension_semantics=("arbitrary", "arbitrary"), vmem_limit_bytes=VMEM_LIMIT),
        name="ffn_up",
    )(h2, h2, h2, w_bf, w_bf, conv_w, conv_b)


def _ffn_down_kernel(a_ref, w_ref, x1_ref, g_ref, o_ref, *, n_k):
    k = pl.program_id(1)

    @pl.when(k == 0)
    def _():
        o_ref[...] = x1_ref[...]

    o_ref[...] += jnp.dot(a_ref[...], w_ref[...], preferred_element_type=F32)

    @pl.when(k == n_k - 1)
    def _():
        y = o_ref[...]
        ms = jnp.mean(y * y, axis=-1, keepdims=True)
        o_ref[...] = y * lax.rsqrt(ms + NORM_EPS) * g_ref[...]


def _ffn_down(act, w_bf, x1, gain, *, tm=1024, tk=1408):
    S, d_ff = act.shape
    D = x1.shape[1]
    n_k = d_ff // tk
    kern = functools.partial(_ffn_down_kernel, n_k=n_k)
    return pl.pallas_call(
        kern,
        grid=(S // tm, n_k),
        in_specs=[
            pl.BlockSpec((tm, tk), lambda i, k: (i, k)),
            pl.BlockSpec((tk, D), lambda i, k: (k, 0)),
            pl.BlockSpec((tm, D), lambda i, k: (i, 0)),
            pl.BlockSpec((1, D), lambda i, k: (0, 0)),
        ],
        out_specs=pl.BlockSpec((tm, D), lambda i, k: (i, 0)),
        out_shape=jax.ShapeDtypeStruct((S, D), F32),
        compiler_params=pltpu.CompilerParams(
            dimension_semantics=("arbitrary", "arbitrary"), vmem_limit_bytes=VMEM_LIMIT),
        name="ffn_down",
    )(act, w_bf, x1, gain)


def kernel(x, norm1_gain, w_in, rel_bias_table, lambda_q1, lambda_k1, lambda_q2, lambda_k2,
           diff_subln_gain, dil_out_gain, w_out, norm2_gain, w_gate_up, conv_w, conv_b, w_down, final_gain):
    B, S, D = x.shape
    assert B == 1 and w_in.shape[0] == 1
    x2 = x.reshape(S, D)
    n_cols = w_in.shape[2]

    qscale = LOG2E / math.sqrt(HEAD_DIM)
    col = np.arange(n_cols)
    dil_q0 = 2 * DIFF_QK_COLS + DIFF_WIDTH
    is_q = (col < DIFF_QK_COLS) | ((col >= dil_q0) & (col < dil_q0 + DIL_WIDTH))
    colscale = jnp.asarray(np.where(is_q, qscale, 1.0).astype(np.float32)).reshape(1, n_cols)
    tab = rel_bias_table.astype(F32) * LOG2E

    regroup = tuple(d for _, d in DILATED_PATTERNS if d != 1)
    proj, qt_all, vt_all, *cls = _in_proj(x2, norm1_gain.reshape(1, D), w_in[0].astype(BF16), colscale, regroup)
    cls_by_dil = dict(zip(regroup, cls))

    lam = _lambda(lambda_q1.reshape(1, -1), lambda_k1.reshape(1, -1),
                  lambda_q2.reshape(1, -1), lambda_k2.reshape(1, -1))
    o_d, w_out_bf, w_gate_up_bf, w_down_bf = _diff_attention(
        tab, proj, qt_all, vt_all, lam, diff_subln_gain.reshape(-1, 1), (w_out[0], w_gate_up[0], w_down[0]))

    dil_blk0 = 1
    dilations = tuple(d for _, d in DILATED_PATTERNS)
    dil_gain = dil_out_gain.reshape(1, -1)

    def pattern_inputs(dilation):
        if dilation == 1:
            return proj.reshape(1, S, proj.shape[1]), dil_blk0
        return cls_by_dil[dilation], 0

    kn2 = _key_norm2(proj, dil_blk0 + 1)
    nums, dens = [], []
    for window, dilation in DILATED_PATTERNS:
        n_p, d_p = _dilated_fast(tab, kn2, *pattern_inputs(dilation), window, dilation)
        nums.append(n_p)
        dens.append(d_p)
    o_l_fast, den_min = _combine_fast(nums, dens, dil_gain, dilations)

    def exact_dilated():
        outs, lses = [], []
        for window, dilation in DILATED_PATTERNS:
            o_p, lse_p = _dilated_pattern(tab, *pattern_inputs(dilation), window, dilation)
            outs.append(o_p)
            lses.append(lse_p)
        return _combine(outs, lses, dil_gain, dilations)

    o_l = lax.cond(jnp.min(den_min) < MIN_DENOMINATOR, lambda fast: exact_dilated(), lambda fast: fast, o_l_fast)

    x1, h2 = _out_proj(o_d, o_l, w_out_bf, x2, norm2_gain.reshape(1, D))
    act = _ffn_up(h2, w_gate_up_bf, conv_w[0], conv_b.reshape(1, -1))
    out = _ffn_down(act, w_down_bf, x1, final_gain.reshape(1, D))
    return out.reshape(B, S, D)
```

```python
import functools
import math

import numpy as np
import jax
import jax.numpy as jnp
from jax import lax
from jax.experimental import pallas as pl
from jax.experimental.pallas import tpu as pltpu

F32 = jnp.float32
BF16 = jnp.bfloat16

HEAD_DIM = 128
N_DIFF_HEADS = 4
DIFF_V_DIM = 2 * HEAD_DIM
N_DIL_HEADS = 8
DIFF_QK_COLS = N_DIFF_HEADS * 2 * HEAD_DIM
DIFF_WIDTH = N_DIFF_HEADS * DIFF_V_DIM
DIL_WIDTH = N_DIL_HEADS * HEAD_DIM
DILATED_PATTERNS = ((128, 1), (512, 4), (2048, 16))
N_REL_BUCKETS = 32
REL_MAX_DISTANCE = 1024
NORM_EPS = 1e-6
SUBLN_EPS = 1e-5
NEG_INF = -1e30
LOG2E = math.log2(math.e)
LAM_INIT = 0.8 - 0.6 * math.exp(-0.3 * 0)

LANES = 128
F32_SUBLANES = 8
BF16_SUBLANES = 16
DIFF_TILE = 1024
DIFF_QUERY_PANEL = 256
FAR_UNROLL = 4
SHIFT_MARGIN = 1.0 + 2.0 ** -8
MIN_DENOMINATOR = 2.0 ** -60
FFN_UP_ROW_SPLIT = 2
DIL_CHUNK = 1024
DIL_HEAD_UNROLL = 4
DIL_PROB_SLOTS = 4
VMEM_LIMIT = 56 * 1024 * 1024


def _bucket_breaks():
    nb = N_REL_BUCKETS // 2
    max_exact = nb // 2
    rel = np.arange(-2 * REL_MAX_DISTANCE, 2 * REL_MAX_DISTANCE + 1)
    n = np.abs(rel)
    pos = np.log(np.maximum(n, 1) / max_exact) / math.log(REL_MAX_DISTANCE / max_exact) * (nb - max_exact)
    large = np.minimum(max_exact + np.floor(pos).astype(np.int64), nb - 1)
    bucket = np.where(rel > 0, nb, 0) + np.where(n < max_exact, n, large)
    breaks = [(int(rel[i]), int(bucket[i])) for i in range(1, len(rel)) if bucket[i] != bucket[i - 1]]
    return int(bucket[0]), breaks


FIRST_BUCKET, BUCKET_BREAKS = _bucket_breaks()
LAST_BUCKET = BUCKET_BREAKS[-1][1]
FAR_DIST = max(-BUCKET_BREAKS[0][0] + 1, BUCKET_BREAKS[-1][0])


def _bias_from_rel(rel, tab_ref, col):
    val = jnp.full(rel.shape, tab_ref[FIRST_BUCKET, col], F32)
    for thr, b in BUCKET_BREAKS:
        val = jnp.where(rel >= thr, tab_ref[b, col], val)
    return val


N_COL_GROUPS = 6
FIRST_DIL_GROUP = 3


def _inproj_kernel(x_ref, g_ref, w_ref, cs_ref, o_ref, qt_ref, vt_ref, *rest, dilations):
    cls_refs = rest[:len(dilations)]
    h_ref, stage_ref = rest[len(dilations):]
    t = pl.program_id(1)
    tm = x_ref.shape[0]
    n_dil = N_COL_GROUPS - FIRST_DIL_GROUP

    @pl.when(t == 0)
    def _():
        x = x_ref[...]
        ms = jnp.mean(x * x, axis=-1, keepdims=True)
        h_ref[...] = (x * lax.rsqrt(ms + NORM_EPS) * g_ref[...]).astype(BF16)

    def matmul(keep):
        acc = jnp.dot(h_ref[...], w_ref[...], preferred_element_type=F32) * cs_ref[...]
        if keep:
            o_ref[...] = acc.astype(o_ref.dtype)
        return acc

    def stage(acc, slot):
        for cb in range(stage_ref.shape[1]):
            stage_ref[slot, cb] = acc[:, cb * LANES:(cb + 1) * LANES]

    def regroup(slot):
        for cb in range(stage_ref.shape[1]):
            sl = slice(cb * LANES, (cb + 1) * LANES)
            for cls_ref, dil in zip(cls_refs, dilations):
                for c in range(dil):
                    cls_ref[c, :, sl] = stage_ref[slot, cb, pl.ds(c, tm // dil, stride=dil), :].astype(cls_ref.dtype)

    for step in range(N_COL_GROUPS):
        @pl.when(t == step)
        def _(step=step):
            acc = matmul(keep=step not in (n_dil, n_dil + 2))
            if 1 <= step <= n_dil:
                regroup((step - 1) % 2)
            if step < n_dil:
                stage(acc, step % 2)
            if step == n_dil:
                qt_ref[...] = acc.T.astype(qt_ref.dtype)
            if step == n_dil + 2:
                vt_ref[...] = acc.T.astype(vt_ref.dtype)


def _in_proj(x2, gain, w_bf, colscale, dilations, *, tm=512):
    S, D = x2.shape
    N = w_bf.shape[1]
    tn = DIFF_QK_COLS
    assert DIFF_WIDTH == tn and DIL_WIDTH == tn and N == N_COL_GROUPS * tn
    n_dil = N_COL_GROUPS - FIRST_DIL_GROUP
    kern = functools.partial(_inproj_kernel, dilations=dilations)

    def group(t):
        return (t + FIRST_DIL_GROUP) % N_COL_GROUPS

    cls_specs = [pl.BlockSpec((d, tm // d, tn), lambda i, t: (0, i, jnp.clip(t - 1, 0, n_dil - 1)))
                 for d in dilations]
    cls_shapes = [jax.ShapeDtypeStruct((d, S // d, n_dil * tn), BF16) for d in dilations]
    return pl.pallas_call(
        kern,
        grid=(S // tm, N_COL_GROUPS),
        in_specs=[
            pl.BlockSpec((tm, D), lambda i, t: (i, 0)),
            pl.BlockSpec((1, D), lambda i, t: (0, 0)),
            pl.BlockSpec((D, tn), lambda i, t: (0, group(t))),
            pl.BlockSpec((1, tn), lambda i, t: (0, group(t))),
        ],
        out_specs=[
            pl.BlockSpec((tm, tn), lambda i, t: (i, jnp.where(t < n_dil, t + 1, 0))),
            pl.BlockSpec((tn, tm), lambda i, t: (0, i)),
            pl.BlockSpec((tn, tm), lambda i, t: (0, i)),
        ] + cls_specs,
        out_shape=[
            jax.ShapeDtypeStruct((S, (n_dil + 1) * tn), BF16),
            jax.ShapeDtypeStruct((tn, S), BF16),
            jax.ShapeDtypeStruct((tn, S), BF16),
        ] + cls_shapes,
        scratch_shapes=[pltpu.VMEM((tm, D), BF16), pltpu.VMEM((2, tn // LANES, tm, LANES), F32)],
        compiler_params=pltpu.CompilerParams(
            dimension_semantics=("arbitrary", "arbitrary"), vmem_limit_bytes=VMEM_LIMIT),
        name="in_proj",
    )(x2, gain, w_bf, colscale)


def _diff_attn_kernel(*refs, T, QP, n_near, n_tiles, n_cast):
    tab_ref, q1t_ref, q2t_ref, k1_ref, k2_ref, vt_ref, lam_ref, gain_ref = refs[:8]
    cast_in, o_ref, cast_out = refs[8:8 + n_cast], refs[8 + n_cast], refs[9 + n_cast:9 + 2 * n_cast]
    bias_ref, m_ref, l_ref, acc_ref, p_ref, shift_ref, knorm_ref, prev_ref = refs[9 + 2 * n_cast:]
    _diff_attn_body(tab_ref, q1t_ref, q2t_ref, k1_ref, k2_ref, vt_ref, lam_ref, gain_ref, o_ref,
                    bias_ref, m_ref, l_ref, acc_ref, p_ref, shift_ref, knorm_ref, prev_ref,
                    T=T, QP=QP, n_near=n_near, n_tiles=n_tiles)
    for src, dst in zip(cast_in, cast_out):
        dst[...] = src[...].astype(dst.dtype)


def _diff_attn_body(tab_ref, q1t_ref, q2t_ref, k1_ref, k2_ref, vt_ref, lam_ref, gain_ref, o_ref,
                    bias_ref, m_ref, l_ref, acc_ref, p_ref, shift_ref, knorm_ref, prev_ref,
                    *, T, QP, n_near, n_tiles):
    h = pl.program_id(0)
    qi = pl.program_id(1)
    n_chains = 2 * (T // QP)
    qts = (q1t_ref, q2t_ref)
    ks = (k1_ref, k2_ref)

    def chain_of(i):
        qp = i // 2
        return i % 2, slice(qp * QP, (qp + 1) * QP)

    @pl.when(qi == 0)
    def _():
        for mi in range(2):
            def knorm(t, best, mi=mi):
                k = ks[mi][pl.ds(pl.multiple_of(t * T, T), T), :].astype(F32)
                return jnp.maximum(best, jnp.max(jnp.sum(k * k, axis=1, keepdims=True), axis=0, keepdims=True))
            knorm_ref[mi] = lax.fori_loop(0, n_tiles, knorm, jnp.zeros((1, 1), F32))

        x = lax.broadcasted_iota(jnp.int32, (F32_SUBLANES, 2 * T), 1)
        x = jnp.where(x < T, x, x - 2 * T)
        for di, d in enumerate(range(-n_near, n_near + 1)):
            g = _bias_from_rel(d * T - x, tab_ref, h)
            base = jnp.broadcast_to(g[0:1, :], (LANES, 2 * T))
            for rb in range(T // LANES):
                blk = pltpu.roll(base, rb * LANES, 1, stride=1, stride_axis=0)
                bias_ref[di, rb * LANES:(rb + 1) * LANES, :] = blk[:, :T]

    c_left = tab_ref[FIRST_BUCKET, h]
    c_right = tab_ref[LAST_BUCKET, h]

    bias_max = tab_ref[0, h]
    for b in range(1, N_REL_BUCKETS):
        bias_max = jnp.maximum(bias_max, tab_ref[b, h])

    for mi in range(2):
        q = qts[mi][...].astype(F32)
        qnorm2 = jnp.sum(q * q, axis=0, keepdims=True)
        shift_ref[mi] = jnp.sqrt(qnorm2 * knorm_ref[mi]) * SHIFT_MARGIN + bias_max
    l_ref[...] = jnp.zeros(l_ref.shape, F32)
    acc_ref[...] = jnp.zeros(acc_ref.shape, F32)
    p_ref[n_chains - 1] = jnp.zeros((T, QP), BF16)
    prev_ref[0] = 0

    def score_stage(i, k0, bias_di, bias_const):
        mi, qs = chain_of(i)
        s = jnp.dot(ks[mi][pl.ds(k0, T), :], qts[mi][:, qs], preferred_element_type=F32)
        if bias_di is not None:
            p = jnp.exp2(s + bias_ref[bias_di, :, qs] - shift_ref[mi, :, qs])
        else:
            p = jnp.exp2(s - (shift_ref[mi, :, qs] - bias_const))
        l_ref[mi, :, qs] += jnp.sum(p, axis=0, keepdims=True)
        p_ref[i] = p.astype(BF16)

    def value_stage(i, k0):
        mi, qs = chain_of(i)
        acc_ref[mi, :, qs] += jnp.dot(vt_ref[:, pl.ds(k0, T)], p_ref[i], preferred_element_type=F32)

    def tile(kt, bias_di, bias_const, prev_kt=None):
        k0 = pl.multiple_of(kt * T, T)
        pk0 = pl.multiple_of((prev_ref[0] if prev_kt is None else prev_kt) * T, T)
        for i in range(n_chains):
            value_stage((i - 1) % n_chains, pk0 if i < 1 else k0)
            score_stage(i, k0, bias_di, bias_const)
        prev_ref[0] = kt

    lo = jnp.maximum(qi - n_near, 0)
    hi = jnp.minimum(qi + n_near + 1, n_tiles)

    for di, d in enumerate(range(-n_near, n_near + 1)):
        kt = qi + d

        @pl.when(jnp.logical_and(kt >= 0, kt < n_tiles))
        def _(di=di, kt=kt):
            tile(kt, di, None)

    n_far = n_tiles - (hi - lo)

    def far_run(first, length):
        prev_kt = None
        for j in range(length):
            before = first + j < lo
            kt = jnp.where(before, first + j, first + j + (hi - lo))
            tile(kt, None, jnp.where(before, c_left, c_right), prev_kt=prev_kt)
            prev_kt = kt

    done = 0
    run = 1
    while run < FAR_UNROLL:
        has_run = jnp.bitwise_and(n_far, run)

        @pl.when(has_run != 0)
        def _(done=done, run=run):
            far_run(done, run)

        done = done + has_run
        run *= 2

    @pl.loop(0, n_far // FAR_UNROLL)
    def _(trip, done=done):
        far_run(done + trip * FAR_UNROLL, FAR_UNROLL)

    value_stage(n_chains - 1, pl.multiple_of(prev_ref[0] * T, T))

    @pl.when(jnp.min(l_ref[...]) < MIN_DENOMINATOR)
    def _():
        m_ref[...] = jnp.full(m_ref.shape, -jnp.inf, F32)
        l_ref[...] = jnp.zeros(l_ref.shape, F32)
        acc_ref[...] = jnp.zeros(acc_ref.shape, F32)

        @pl.loop(0, n_tiles)
        def _(kt):
            k0 = pl.multiple_of(kt * T, T)
            d = kt - qi
            near = jnp.abs(d) <= n_near
            di = jnp.clip(d + n_near, 0, 2 * n_near)
            c_far = jnp.where(d < 0, c_left, c_right)
            for i in range(n_chains):
                mi, qs = chain_of(i)
                s = jnp.dot(ks[mi][pl.ds(k0, T), :], qts[mi][:, qs], preferred_element_type=F32)
                s = s + jnp.where(near, bias_ref[di, :, qs], c_far)
                m = m_ref[mi, :, qs]
                m_new = jnp.maximum(m, jnp.max(s, axis=0, keepdims=True))
                alpha = jnp.exp2(m - m_new)
                p = jnp.exp2(s - m_new)
                m_ref[mi, :, qs] = m_new
                l_ref[mi, :, qs] = alpha * l_ref[mi, :, qs] + jnp.sum(p, axis=0, keepdims=True)
                pv = jnp.dot(vt_ref[:, pl.ds(k0, T)], p.astype(BF16), preferred_element_type=F32)
                acc_ref[mi, :, qs] = alpha * acc_ref[mi, :, qs] + pv

    lam = lam_ref[0, 0]
    o = acc_ref[0] / l_ref[0] - lam * (acc_ref[1] / l_ref[1])
    ms = jnp.mean(o * o, axis=0, keepdims=True)
    o = o * lax.rsqrt(ms + SUBLN_EPS) * (gain_ref[...] * (1.0 - LAM_INIT))
    o_ref[...] = o.T.astype(o_ref.dtype)


def _lambda_kernel(q1_ref, k1_ref, q2_ref, k2_ref, o_ref):
    a = jnp.sum(q1_ref[...] * k1_ref[...], axis=-1, keepdims=True)
    b = jnp.sum(q2_ref[...] * k2_ref[...], axis=-1, keepdims=True)
    o_ref[...] = jnp.exp(a) - jnp.exp(b) + LAM_INIT


def _lambda(lq1, lk1, lq2, lk2):
    return pl.pallas_call(
        _lambda_kernel, out_shape=jax.ShapeDtypeStruct((1, 1), F32), name="diff_lambda",
    )(lq1, lk1, lq2, lk2)


def _diff_attention(tab_diff, proj, qt_all, vt_all, lam, gain_col, f32_weights):
    S = proj.shape[0]
    T = DIFF_TILE
    n_tiles = S // T
    n_near = -(-(FAR_DIST - 1) // T)
    kern = functools.partial(_diff_attn_kernel, T=T, QP=DIFF_QUERY_PANEL, n_near=n_near, n_tiles=n_tiles,
                             n_cast=len(f32_weights))

    n_steps = N_DIFF_HEADS * n_tiles
    cast_specs = []
    for w in f32_weights:
        rows, cols = w.shape
        every = next(e for e in (1, 2, 4, 8) if rows % (n_steps // e) == 0
                     and (rows // (n_steps // e)) % BF16_SUBLANES == 0)
        cast_specs.append(pl.BlockSpec((rows // (n_steps // every), cols),
                                       lambda h, i, every=every: ((h * n_tiles + i) // every, 0)))
    k_block0 = 0
    smem = pl.BlockSpec(memory_space=pltpu.SMEM)
    return pl.pallas_call(
        kern,
        grid=(N_DIFF_HEADS, n_tiles),
        in_specs=[
            smem,
            pl.BlockSpec((HEAD_DIM, T), lambda h, i: (2 * h, i)),
            pl.BlockSpec((HEAD_DIM, T), lambda h, i: (2 * h + 1, i)),
            pl.BlockSpec((S, HEAD_DIM), lambda h, i: (0, k_block0 + 2 * h), pipeline_mode=pl.Buffered(1)),
            pl.BlockSpec((S, HEAD_DIM), lambda h, i: (0, k_block0 + 2 * h + 1), pipeline_mode=pl.Buffered(1)),
            pl.BlockSpec((DIFF_V_DIM, S), lambda h, i: (h, 0), pipeline_mode=pl.Buffered(1)),
            smem,
            pl.BlockSpec((DIFF_V_DIM, 1), lambda h, i: (0, 0)),
        ] + cast_specs,
        out_specs=[pl.BlockSpec((T, DIFF_V_DIM), lambda h, i: (i, h))] + cast_specs,
        out_shape=[jax.ShapeDtypeStruct((S, DIFF_WIDTH), BF16)]
        + [jax.ShapeDtypeStruct(w.shape, BF16) for w in f32_weights],
        scratch_shapes=[pltpu.VMEM((2 * n_near + 1, T, T), F32),
                        pltpu.VMEM((2, 1, T), F32), pltpu.VMEM((2, 1, T), F32),
                        pltpu.VMEM((2, DIFF_V_DIM, T), F32),
                        pltpu.VMEM((2 * (T // DIFF_QUERY_PANEL), T, DIFF_QUERY_PANEL), BF16),
                        pltpu.VMEM((2, 1, T), F32), pltpu.VMEM((2, 1, 1), F32),
                        pltpu.SMEM((1,), jnp.int32)],
        compiler_params=pltpu.CompilerParams(
            dimension_semantics=("arbitrary", "arbitrary"), vmem_limit_bytes=VMEM_LIMIT),
        name="diff_attn",
    )(tab_diff, qt_all, qt_all, proj, proj, vt_all, lam, gain_col, *f32_weights)


def _dilated_kernel(tab_ref, q_ref, kp_ref, km_ref, kn_ref, vp_ref, vm_ref, vn_ref, o_ref, lse_ref,
                    bias_ref, kx_ref, vx_ref, *, R, B, half, dilation, n_chunks):
    c = pl.program_id(0)
    n = pl.program_id(1)
    W = B + 2 * half
    nblk = R // B

    @pl.when(jnp.logical_and(c == 0, n == 0))
    def _():
        rows = F32_SUBLANES
        col = lax.broadcasted_iota(jnp.int32, (rows, W), 1)
        row = lax.broadcasted_iota(jnp.int32, (rows, W), 0)
        for hh in range(N_DIL_HEADS):
            def fill(r, carry, hh=hh):
                r0 = pl.multiple_of(r * rows, rows)
                off = col - half - (row + r0)
                bias = _bias_from_rel(off * dilation, tab_ref, N_DIFF_HEADS + hh)
                base = jnp.where(jnp.abs(off) <= half, bias, NEG_INF * LOG2E)
                bias_ref[hh, 1, pl.ds(r0, rows), :] = base
                bias_ref[hh, 0, pl.ds(r0, rows), :] = jnp.where(col >= half, base, NEG_INF * LOG2E)
                bias_ref[hh, 2, pl.ds(r0, rows), :] = jnp.where(col < B + half, base, NEG_INF * LOG2E)
                return carry
            lax.fori_loop(0, B // rows, fill, 0)

    kx_ref[0:half, :] = kp_ref[...]
    kx_ref[half:half + R, :] = km_ref[...]
    kx_ref[half + R:, :] = kn_ref[...]
    vx_ref[0:half, :] = vp_ref[...]
    vx_ref[half:half + R, :] = vm_ref[...]
    vx_ref[half + R:, :] = vn_ref[...]

    def chain(hh, b):
        c0 = pl.multiple_of(hh * HEAD_DIM, HEAD_DIM)
        r0 = b * B
        var = 1
        if b == 0:
            var = jnp.where(n == 0, 0, var)
        if b == nblk - 1:
            var = jnp.where(n == n_chunks - 1, 2, var)
        q = q_ref[pl.ds(r0, B), pl.ds(c0, HEAD_DIM)]
        k = kx_ref[pl.ds(r0, W), pl.ds(c0, HEAD_DIM)]
        v = vx_ref[pl.ds(r0, W), pl.ds(c0, HEAD_DIM)]
        s = lax.dot_general(q, k, (((1,), (1,)), ((), ())), preferred_element_type=F32)
        s = s + bias_ref[hh, var]
        m = jnp.max(s, axis=-1, keepdims=True)
        e = jnp.exp2(s - m)
        den = jnp.sum(e, axis=-1, keepdims=True)
        o = jnp.dot(e.astype(BF16), v, preferred_element_type=F32) / den
        o_ref[pl.ds(r0, B), pl.ds(c0, HEAD_DIM)] = o.astype(o_ref.dtype)
        lse = m + jnp.log2(den)
        lse_ref[pl.ds(r0, B), pl.ds(c0, HEAD_DIM)] = jnp.broadcast_to(lse, (B, HEAD_DIM))

    @pl.loop(0, N_DIL_HEADS // DIL_HEAD_UNROLL)
    def _(hg):
        for u in range(DIL_HEAD_UNROLL):
            for b in range(nblk):
                chain(hg * DIL_HEAD_UNROLL + u, b)


def _dilated_pattern(tab, qkv, col_blk0, window, dilation, *, B=256):
    _, L, _ = qkv.shape
    R = min(DIL_CHUNK, L)
    half = window // (2 * dilation)
    assert L % R == 0 and R % B == 0 and half % BF16_SUBLANES == 0 and R % half == 0
    n_chunks = L // R
    q_blk, k_blk, v_blk = col_blk0, col_blk0 + 1, col_blk0 + 2
    hb = R // half
    n_hblk = L // half

    def main(blk):
        return pl.BlockSpec((None, R, DIL_WIDTH), lambda c, n: (c, n, blk))

    def prev(blk):
        return pl.BlockSpec((None, half, DIL_WIDTH), lambda c, n: (c, jnp.maximum(n * hb - 1, 0), blk))

    def nxt(blk):
        return pl.BlockSpec((None, half, DIL_WIDTH), lambda c, n: (c, jnp.minimum((n + 1) * hb, n_hblk - 1), blk))

    kern = functools.partial(_dilated_kernel, R=R, B=B, half=half, dilation=dilation, n_chunks=n_chunks)
    out_spec = pl.BlockSpec((None, R, DIL_WIDTH), lambda c, n: (c, n, 0))
    return pl.pallas_call(
        kern,
        grid=(dilation, n_chunks),
        in_specs=[pl.BlockSpec(memory_space=pltpu.SMEM),
                  main(q_blk), prev(k_blk), main(k_blk), nxt(k_blk), prev(v_blk), main(v_blk), nxt(v_blk)],
        out_specs=[out_spec, out_spec],
        out_shape=[jax.ShapeDtypeStruct((dilation, L, DIL_WIDTH), BF16),
                   jax.ShapeDtypeStruct((dilation, L, DIL_WIDTH), F32)],
        scratch_shapes=[pltpu.VMEM((N_DIL_HEADS, 3, B, B + 2 * half), F32),
                        pltpu.VMEM((R + 2 * half, DIL_WIDTH), BF16),
                        pltpu.VMEM((R + 2 * half, DIL_WIDTH), BF16)],
        compiler_params=pltpu.CompilerParams(
            dimension_semantics=("arbitrary", "arbitrary"), vmem_limit_bytes=VMEM_LIMIT),
        name=f"dilated_d{dilation}",
    )(tab, qkv, qkv, qkv, qkv, qkv, qkv, qkv)


def _combine_kernel(*refs, dilations):
    n = len(dilations)
    o_refs, l_refs = refs[:n], refs[n:2 * n]
    g_ref, out_ref = refs[2 * n], refs[2 * n + 1]
    scratch = refs[2 * n + 2:]
    tm = out_ref.shape[0]

    for hh in range(N_DIL_HEADS):
        sl = slice(hh * HEAD_DIM, (hh + 1) * HEAD_DIM)
        outs, lses = [], []
        si = 0
        for o_ref, l_ref, dil in zip(o_refs, l_refs, dilations):
            if dil == 1:
                outs.append(o_ref[0, :, sl].astype(F32))
                lses.append(l_ref[0, :, sl])
                continue
            os_ref, ls_ref = scratch[si], scratch[si + 1]
            si += 2
            for c in range(dil):
                os_ref[hh, pl.ds(c, tm // dil, stride=dil), :] = o_ref[c, :, sl].astype(F32)
                ls_ref[hh, pl.ds(c, tm // dil, stride=dil), :] = l_ref[c, :, sl]
            outs.append(os_ref[hh])
            lses.append(ls_ref[hh])

        m = functools.reduce(jnp.maximum, lses)
        ws = [jnp.exp2(l - m) for l in lses]
        tot = functools.reduce(lambda a, b: a + b, ws)
        oh = functools.reduce(lambda a, b: a + b, [(w / tot) * op for w, op in zip(ws, outs)])
        ms = jnp.mean(oh * oh, axis=-1, keepdims=True)
        out_ref[:, sl] = (oh * lax.rsqrt(ms + NORM_EPS) * g_ref[:, sl]).astype(out_ref.dtype)


def _combine(outs, lses, gain, dilations, *, tm=512):
    S = outs[0].shape[0] * outs[0].shape[1]
    specs = [pl.BlockSpec((d, tm // d, DIL_WIDTH), lambda i: (0, i, 0)) for d in dilations]
    n_scr = sum(1 for d in dilations if d != 1)
    return pl.pallas_call(
        functools.partial(_combine_kernel, dilations=dilations),
        grid=(S // tm,),
        in_specs=specs + specs + [pl.BlockSpec((1, DIL_WIDTH), lambda i: (0, 0))],
        scratch_shapes=[pltpu.VMEM((N_DIL_HEADS, tm, HEAD_DIM), F32)] * (2 * n_scr),
        out_specs=pl.BlockSpec((tm, DIL_WIDTH), lambda i: (i, 0)),
        out_shape=jax.ShapeDtypeStruct((S, DIL_WIDTH), BF16),
        compiler_params=pltpu.CompilerParams(
            dimension_semantics=("arbitrary",), vmem_limit_bytes=VMEM_LIMIT),
        name="dilated_combine",
    )(*outs, *lses, gain)


def _key_norm2_kernel(k_ref, o_ref):
    @pl.when(pl.program_id(0) == 0)
    def _():
        o_ref[...] = jnp.zeros(o_ref.shape, F32)

    k = k_ref[...].astype(F32)
    for hh in range(N_DIL_HEADS):
        kh = k[:, hh * HEAD_DIM:(hh + 1) * HEAD_DIM]
        n2 = jnp.max(jnp.sum(kh * kh, axis=1, keepdims=True), axis=0, keepdims=True)
        o_ref[hh:hh + 1, :] = jnp.maximum(o_ref[hh:hh + 1, :], jnp.broadcast_to(n2, (1, LANES)))


def _key_norm2(proj, k_blk, *, tm=1024):
    S = proj.shape[0]
    return pl.pallas_call(
        _key_norm2_kernel,
        grid=(S // tm,),
        in_specs=[pl.BlockSpec((tm, DIL_WIDTH), lambda i: (i, k_blk))],
        out_specs=pl.BlockSpec((N_DIL_HEADS, LANES), lambda i: (0, 0)),
        out_shape=jax.ShapeDtypeStruct((N_DIL_HEADS, LANES), F32),
        compiler_params=pltpu.CompilerParams(dimension_semantics=("arbitrary",), vmem_limit_bytes=VMEM_LIMIT),
        name="dilated_key_norm",
    )(proj)


def _dilated_fast_kernel(tab_ref, kn2_ref, q_ref, kp_ref, km_ref, kn_ref, vp_ref, vm_ref, vn_ref, num_ref, den_ref,
                         bias_ref, kx_ref, vx_ref, p_ref, *, R, B, half, dilation, n_chunks):
    c = pl.program_id(0)
    n = pl.program_id(1)
    W = B + 2 * half
    nblk = R // B
    VW = 2 * HEAD_DIM

    @pl.when(jnp.logical_and(c == 0, n == 0))
    def _():
        rows = F32_SUBLANES
        col = lax.broadcasted_iota(jnp.int32, (rows, W), 1)
        row = lax.broadcasted_iota(jnp.int32, (rows, W), 0)
        for hh in range(N_DIL_HEADS):
            def fill(r, carry, hh=hh):
                r0 = pl.multiple_of(r * rows, rows)
                off = col - half - (row + r0)
                bias = _bias_from_rel(off * dilation, tab_ref, N_DIFF_HEADS + hh)
                base = jnp.where(jnp.abs(off) <= half, bias, NEG_INF * LOG2E)
                bias_ref[hh, 1, pl.ds(r0, rows), :] = base
                bias_ref[hh, 0, pl.ds(r0, rows), :] = jnp.where(col >= half, base, NEG_INF * LOG2E)
                bias_ref[hh, 2, pl.ds(r0, rows), :] = jnp.where(col < B + half, base, NEG_INF * LOG2E)
                return carry
            lax.fori_loop(0, B // rows, fill, 0)
        vx_ref[...] = jnp.ones(vx_ref.shape, BF16)

    kx_ref[0:half, :] = kp_ref[...]
    kx_ref[half:half + R, :] = km_ref[...]
    kx_ref[half + R:, :] = kn_ref[...]
    for hh in range(N_DIL_HEADS):
        src = slice(hh * HEAD_DIM, (hh + 1) * HEAD_DIM)
        dst = slice(hh * VW, hh * VW + HEAD_DIM)
        vx_ref[0:half, dst] = vp_ref[:, src]
        vx_ref[half:half + R, dst] = vm_ref[:, src]
        vx_ref[half + R:, dst] = vn_ref[:, src]

    den_ref[...] = jnp.zeros(den_ref.shape, F32)
    n_slots = p_ref.shape[0]
    bias_max = []
    for hh in range(N_DIL_HEADS):
        bm = tab_ref[0, N_DIFF_HEADS + hh]
        for bk in range(1, N_REL_BUCKETS):
            bm = jnp.maximum(bm, tab_ref[bk, N_DIFF_HEADS + hh])
        bias_max.append(bm)

    def score_stage(hh, b, slot):
        cs = slice(hh * HEAD_DIM, (hh + 1) * HEAD_DIM)
        r0 = b * B
        var = 1
        if b == 0:
            var = jnp.where(n == 0, 0, var)
        if b == nblk - 1:
            var = jnp.where(n == n_chunks - 1, 2, var)
        q = q_ref[r0:r0 + B, cs]
        qf = q.astype(F32)
        qn2 = jnp.sum(qf * qf, axis=1, keepdims=True)
        shift = jnp.sqrt(qn2 * kn2_ref[hh:hh + 1, 0:1]) * SHIFT_MARGIN + bias_max[hh]
        s = lax.dot_general(q, kx_ref[r0:r0 + W, cs], (((1,), (1,)), ((), ())), preferred_element_type=F32)
        p_ref[slot] = jnp.exp2(s + bias_ref[hh, var] - shift).astype(BF16)

    def value_stage(hh, b, slot):
        cs = slice(hh * HEAD_DIM, (hh + 1) * HEAD_DIM)
        r0 = b * B
        nd = jnp.dot(p_ref[slot], vx_ref[r0:r0 + W, hh * VW:(hh + 1) * VW], preferred_element_type=F32)
        num_ref[r0:r0 + B, cs] = nd[:, :HEAD_DIM].astype(num_ref.dtype)
        den_ref[r0:r0 + B, hh:hh + 1] = nd[:, HEAD_DIM + hh:HEAD_DIM + hh + 1]

    chains = [(hh, b) for hh in range(N_DIL_HEADS) for b in range(nblk)]
    for i, (hh, b) in enumerate(chains):
        if i > 0:
            value_stage(*chains[i - 1], (i - 1) % n_slots)
        score_stage(hh, b, i % n_slots)
    value_stage(*chains[-1], (len(chains) - 1) % n_slots)


def _dilated_fast(tab, kn2, qkv, col_blk0, window, dilation, *, B=128):
    _, L, _ = qkv.shape
    R = min(DIL_CHUNK, L)
    half = window // (2 * dilation)
    assert L % R == 0 and R % B == 0 and half % BF16_SUBLANES == 0 and R % half == 0
    n_chunks = L // R
    q_blk, k_blk, v_blk = col_blk0, col_blk0 + 1, col_blk0 + 2
    hb = R // half
    n_hblk = L // half

    def main(blk):
        return pl.BlockSpec((None, R, DIL_WIDTH), lambda c, n: (c, n, blk))

    def prev(blk):
        return pl.BlockSpec((None, half, DIL_WIDTH), lambda c, n: (c, jnp.maximum(n * hb - 1, 0), blk))

    def nxt(blk):
        return pl.BlockSpec((None, half, DIL_WIDTH), lambda c, n: (c, jnp.minimum((n + 1) * hb, n_hblk - 1), blk))

    kern = functools.partial(_dilated_fast_kernel, R=R, B=B, half=half, dilation=dilation, n_chunks=n_chunks)
    out_spec = pl.BlockSpec((None, R, DIL_WIDTH), lambda c, n: (c, n, 0))
    return pl.pallas_call(
        kern,
        grid=(dilation, n_chunks),
        in_specs=[pl.BlockSpec(memory_space=pltpu.SMEM),
                  pl.BlockSpec((N_DIL_HEADS, LANES), lambda c, n: (0, 0)),
                  main(q_blk), prev(k_blk), main(k_blk), nxt(k_blk), prev(v_blk), main(v_blk), nxt(v_blk)],
        out_specs=[out_spec, pl.BlockSpec((None, R, LANES), lambda c, n: (c, n, 0))],
        out_shape=[jax.ShapeDtypeStruct((dilation, L, DIL_WIDTH), BF16),
                   jax.ShapeDtypeStruct((dilation, L, LANES), F32)],
        scratch_shapes=[pltpu.VMEM((N_DIL_HEADS, 3, B, B + 2 * half), F32),
                        pltpu.VMEM((R + 2 * half, DIL_WIDTH), BF16),
                        pltpu.VMEM((R + 2 * half, 2 * DIL_WIDTH), BF16),
                        pltpu.VMEM((DIL_PROB_SLOTS, B, B + 2 * half), BF16)],
        compiler_params=pltpu.CompilerParams(
            dimension_semantics=("arbitrary", "arbitrary"), vmem_limit_bytes=VMEM_LIMIT),
        name=f"dilated_fast_d{dilation}",
    )(tab, kn2, qkv, qkv, qkv, qkv, qkv, qkv, qkv)


def _combine_fast_kernel(*refs, dilations):
    n = len(dilations)
    n_refs, d_refs = refs[:n], refs[n:2 * n]
    g_ref, out_ref, dmin_ref = refs[2 * n], refs[2 * n + 1], refs[2 * n + 2]
    scratch = refs[2 * n + 3:]
    tm = out_ref.shape[0]

    den_all = None
    si = 0
    for d_ref, dil in zip(d_refs, dilations):
        if dil == 1:
            den_p = d_ref[0]
        else:
            ds_ref = scratch[si + 1]
            for c in range(dil):
                ds_ref[pl.ds(c, tm // dil, stride=dil), :] = d_ref[c]
            den_p = ds_ref[...]
            si += 2
        den_all = den_p if den_all is None else den_all + den_p

    for hh in range(N_DIL_HEADS):
        sl = slice(hh * HEAD_DIM, (hh + 1) * HEAD_DIM)
        nums = []
        si = 0
        for n_ref, dil in zip(n_refs, dilations):
            if dil == 1:
                nums.append(n_ref[0, :, sl].astype(F32))
                continue
            ns_ref = scratch[si]
            si += 2
            for c in range(dil):
                ns_ref[hh, pl.ds(c, tm // dil, stride=dil), :] = n_ref[c, :, sl].astype(F32)
            nums.append(ns_ref[hh])

        den = den_all[:, hh:hh + 1]
        oh = functools.reduce(lambda a, b: a + b, nums) / den
        ms = jnp.mean(oh * oh, axis=-1, keepdims=True)
        out_ref[:, sl] = (oh * lax.rsqrt(ms + NORM_EPS) * g_ref[:, sl]).astype(out_ref.dtype)
        dmin_ref[0, hh:hh + 1, :] = jnp.broadcast_to(jnp.min(den, axis=0, keepdims=True), (1, LANES))


def _combine_fast(nums, dens, gain, dilations, *, tm=512):
    S = nums[0].shape[0] * nums[0].shape[1]
    num_specs = [pl.BlockSpec((d, tm // d, DIL_WIDTH), lambda i: (0, i, 0)) for d in dilations]
    den_specs = [pl.BlockSpec((d, tm // d, LANES), lambda i: (0, i, 0)) for d in dilations]
    n_scr = sum(1 for d in dilations if d != 1)
    return pl.pallas_call(
        functools.partial(_combine_fast_kernel, dilations=dilations),
        grid=(S // tm,),
        in_specs=num_specs + den_specs + [pl.BlockSpec((1, DIL_WIDTH), lambda i: (0, 0))],
        scratch_shapes=[pltpu.VMEM((N_DIL_HEADS, tm, HEAD_DIM), F32), pltpu.VMEM((tm, LANES), F32)] * n_scr,
        out_specs=[pl.BlockSpec((tm, DIL_WIDTH), lambda i: (i, 0)),
                   pl.BlockSpec((1, N_DIL_HEADS, LANES), lambda i: (i, 0, 0))],
        out_shape=[jax.ShapeDtypeStruct((S, DIL_WIDTH), BF16),
                   jax.ShapeDtypeStruct((S // tm, N_DIL_HEADS, LANES), F32)],
        compiler_params=pltpu.CompilerParams(
            dimension_semantics=("arbitrary",), vmem_limit_bytes=VMEM_LIMIT),
        name="dilated_combine_fast",
    )(*nums, *dens, gain)


def _outproj_kernel(od_ref, ol_ref, wd_ref, wl_ref, x_ref, g_ref, x1_ref, h2_ref):
    acc = jnp.dot(od_ref[...], wd_ref[...], preferred_element_type=F32)
    acc = acc + jnp.dot(ol_ref[...], wl_ref[...], preferred_element_type=F32)
    x1 = x_ref[...] + acc
    x1_ref[...] = x1
    ms = jnp.mean(x1 * x1, axis=-1, keepdims=True)
    h2_ref[...] = (x1 * lax.rsqrt(ms + NORM_EPS) * g_ref[...]).astype(h2_ref.dtype)


def _out_proj(o_d, o_l, w_bf, x2, gain, *, tm=512):
    S, D = x2.shape
    return pl.pallas_call(
        _outproj_kernel,
        grid=(S // tm,),
        in_specs=[
            pl.BlockSpec((tm, DIFF_WIDTH), lambda i: (i, 0)),
            pl.BlockSpec((tm, DIL_WIDTH), lambda i: (i, 0)),
            pl.BlockSpec((DIFF_WIDTH, D), lambda i: (0, 0)),
            pl.BlockSpec((DIL_WIDTH, D), lambda i: (1, 0)),
            pl.BlockSpec((tm, D), lambda i: (i, 0)),
            pl.BlockSpec((1, D), lambda i: (0, 0)),
        ],
        out_specs=[pl.BlockSpec((tm, D), lambda i: (i, 0)), pl.BlockSpec((tm, D), lambda i: (i, 0))],
        out_shape=[jax.ShapeDtypeStruct((S, D), F32), jax.ShapeDtypeStruct((S, D), BF16)],
        compiler_params=pltpu.CompilerParams(
            dimension_semantics=("arbitrary",), vmem_limit_bytes=VMEM_LIMIT),
        name="out_proj",
    )(o_d, o_l, w_bf, w_bf, x2, gain)


def _ffn_up_kernel(hm_ref, hp_ref, hn_ref, wg_ref, wu_ref, cw_ref, cb_ref, o_ref, lhs_ref, *, tm, n_row_tiles):
    i = pl.program_id(0)
    j = pl.program_id(1)
    halo = BF16_SUBLANES

    @pl.when(j == 0)
    def _():
        lhs_ref[0:halo, :] = jnp.where(i == 0, jnp.zeros_like(hp_ref[...]), hp_ref[...])
        lhs_ref[halo:halo + tm, :] = hm_ref[...]
        lhs_ref[halo + tm:, :] = jnp.where(i == n_row_tiles - 1, jnp.zeros_like(hn_ref[...]), hn_ref[...])

    sub = tm // FFN_UP_ROW_SPLIT
    rows = sub + 2 * halo
    for r0 in range(0, tm, sub):
        g = jnp.dot(lhs_ref[r0:r0 + rows, :], wg_ref[...], preferred_element_type=F32)
        u = jnp.dot(lhs_ref[halo + r0:halo + r0 + sub, :], wu_ref[...], preferred_element_type=F32)
        g_prev = pltpu.roll(g, 1, axis=0)
        g_next = pltpu.roll(g, rows - 1, axis=0)
        y = cw_ref[0:1, :] * g_prev + cw_ref[1:2, :] * g + cw_ref[2:3, :] * g_next + cb_ref[...]
        y = y[halo:halo + sub, :]
        act = y * (1.0 / (1.0 + jnp.exp(-y))) * u
        o_ref[r0:r0 + sub, :] = act.astype(o_ref.dtype)


def _ffn_up(h2, w_bf, conv_w, conv_b, *, tm=2048, tn=512):
    S, D = h2.shape
    d_ff = conv_w.shape[1]
    assert d_ff % tn == 0
    nj = d_ff // tn
    ni = S // tm
    hb = tm // BF16_SUBLANES
    n_hblk = S // BF16_SUBLANES
    kern = functools.partial(_ffn_up_kernel, tm=tm, n_row_tiles=ni)
    return pl.pallas_call(
        kern,
        grid=(ni, nj),
        in_specs=[
            pl.BlockSpec((tm, D), lambda i, j: (i, 0)),
            pl.BlockSpec((BF16_SUBLANES, D), lambda i, j: (jnp.maximum(i * hb - 1, 0), 0)),
            pl.BlockSpec((BF16_SUBLANES, D), lambda i, j: (jnp.minimum((i + 1) * hb, n_hblk - 1), 0)),
            pl.BlockSpec((D, tn), lambda i, j: (0, j)),
            pl.BlockSpec((D, tn), lambda i, j: (0, nj + j)),
            pl.BlockSpec((3, tn), lambda i, j: (0, j)),
            pl.BlockSpec((1, tn), lambda i, j: (0, j)),
        ],
        out_specs=pl.BlockSpec((tm, tn), lambda i, j: (i, j)),
        out_shape=jax.ShapeDtypeStruct((S, d_ff), BF16),
        scratch_shapes=[pltpu.VMEM((tm + 2 * BF16_SUBLANES, D), BF16)],
        compiler_params=pltpu.CompilerParams(
            dimension_semantics=("arbitrary", "arbitrary"), vmem_limit_bytes=VMEM_LIMIT),
        name="ffn_up",
    )(h2, h2, h2, w_bf, w_bf, conv_w, conv_b)


def _ffn_down_kernel(a_ref, w_ref, x1_ref, g_ref, o_ref, *, n_k):
    k = pl.program_id(1)

    @pl.when(k == 0)
    def _():
        o_ref[...] = x1_ref[...]

    o_ref[...] += jnp.dot(a_ref[...], w_ref[...], preferred_element_type=F32)

    @pl.when(k == n_k - 1)
    def _():
        y = o_ref[...]
        ms = jnp.mean(y * y, axis=-1, keepdims=True)
        o_ref[...] = y * lax.rsqrt(ms + NORM_EPS) * g_ref[...]


def _ffn_down(act, w_bf, x1, gain, *, tm=1024, tk=1408):
    S, d_ff = act.shape
    D = x1.shape[1]
    n_k = d_ff // tk
    kern = functools.partial(_ffn_down_kernel, n_k=n_k)
    return pl.pallas_call(
        kern,
        grid=(S // tm, n_k),
        in_specs=[
            pl.BlockSpec((tm, tk), lambda i, k: (i, k)),
            pl.BlockSpec((tk, D), lambda i, k: (k, 0)),
            pl.BlockSpec((tm, D), lambda i, k: (i, 0)),
            pl.BlockSpec((1, D), lambda i, k: (0, 0)),
        ],
        out_specs=pl.BlockSpec((tm, D), lambda i, k: (i, 0)),
        out_shape=jax.ShapeDtypeStruct((S, D), F32),
        compiler_params=pltpu.CompilerParams(
            dimension_semantics=("arbitrary", "arbitrary"), vmem_limit_bytes=VMEM_LIMIT),
        name="ffn_down",
    )(act, w_bf, x1, gain)


def kernel(x, norm1_gain, w_in, rel_bias_table, lambda_q1, lambda_k1, lambda_q2, lambda_k2,
           diff_subln_gain, dil_out_gain, w_out, norm2_gain, w_gate_up, conv_w, conv_b, w_down, final_gain):
    B, S, D = x.shape
    assert B == 1 and w_in.shape[0] == 1
    x2 = x.reshape(S, D)
    n_cols = w_in.shape[2]

    qscale = LOG2E / math.sqrt(HEAD_DIM)
    col = np.arange(n_cols)
    dil_q0 = 2 * DIFF_QK_COLS + DIFF_WIDTH
    is_q = (col < DIFF_QK_COLS) | ((col >= dil_q0) & (col < dil_q0 + DIL_WIDTH))
    colscale = jnp.asarray(np.where(is_q, qscale, 1.0).astype(np.float32)).reshape(1, n_cols)
    tab = rel_bias_table.astype(F32) * LOG2E

    regroup = tuple(d for _, d in DILATED_PATTERNS if d != 1)
    proj, qt_all, vt_all, *cls = _in_proj(x2, norm1_gain.reshape(1, D), w_in[0].astype(BF16), colscale, regroup)
    cls_by_dil = dict(zip(regroup, cls))

    lam = _lambda(lambda_q1.reshape(1, -1), lambda_k1.reshape(1, -1),
                  lambda_q2.reshape(1, -1), lambda_k2.reshape(1, -1))
    o_d, w_out_bf, w_gate_up_bf, w_down_bf = _diff_attention(
        tab, proj, qt_all, vt_all, lam, diff_subln_gain.reshape(-1, 1), (w_out[0], w_gate_up[0], w_down[0]))

    dil_blk0 = 1
    dilations = tuple(d for _, d in DILATED_PATTERNS)
    dil_gain = dil_out_gain.reshape(1, -1)

    def pattern_inputs(dilation):
        if dilation == 1:
            return proj.reshape(1, S, proj.shape[1]), dil_blk0
        return cls_by_dil[dilation], 0

    kn2 = _key_norm2(proj, dil_blk0 + 1)
    nums, dens = [], []
    for window, dilation in DILATED_PATTERNS:
        n_p, d_p = _dilated_fast(tab, kn2, *pattern_inputs(dilation), window, dilation)
        nums.append(n_p)
        dens.append(d_p)
    o_l_fast, den_min = _combine_fast(nums, dens, dil_gain, dilations)

    def exact_dilated():
        outs, lses = [], []
        for window, dilation in DILATED_PATTERNS:
            o_p, lse_p = _dilated_pattern(tab, *pattern_inputs(dilation), window, dilation)
            outs.append(o_p)
            lses.append(lse_p)
        return _combine(outs, lses, dil_gain, dilations)

    o_l = lax.cond(jnp.min(den_min) < MIN_DENOMINATOR, lambda fast: exact_dilated(), lambda fast: fast, o_l_fast)

    x1, h2 = _out_proj(o_d, o_l, w_out_bf, x2, norm2_gain.reshape(1, D))
    act = _ffn_up(h2, w_gate_up_bf, conv_w[0], conv_b.reshape(1, -1))
    out = _ffn_down(act, w_down_bf, x1, final_gain.reshape(1, D))
    return out.reshape(B, S, D)
```

```python
import functools
import math

import numpy as np
import jax
import jax.numpy as jnp
from jax import lax
from jax.experimental import pallas as pl
from jax.experimental.pallas import tpu as pltpu

F32 = jnp.float32
BF16 = jnp.bfloat16

HEAD_DIM = 128
N_DIFF_HEADS = 4
DIFF_V_DIM = 2 * HEAD_DIM
N_DIL_HEADS = 8
DIFF_QK_COLS = N_DIFF_HEADS * 2 * HEAD_DIM
DIFF_WIDTH = N_DIFF_HEADS * DIFF_V_DIM
DIL_WIDTH = N_DIL_HEADS * HEAD_DIM
DILATED_PATTERNS = ((128, 1), (512, 4), (2048, 16))
N_REL_BUCKETS = 32
REL_MAX_DISTANCE = 1024
NORM_EPS = 1e-6
SUBLN_EPS = 1e-5
NEG_INF = -1e30
LOG2E = math.log2(math.e)
LAM_INIT = 0.8 - 0.6 * math.exp(-0.3 * 0)

LANES = 128
F32_SUBLANES = 8
BF16_SUBLANES = 16
DIFF_TILE = 1024
DIFF_QUERY_PANEL = 256
FAR_UNROLL = 4
SHIFT_MARGIN = 1.0 + 2.0 ** -8
MIN_DENOMINATOR = 2.0 ** -60
FFN_UP_ROW_SPLIT = 2
DIL_CHUNK = 1024
DIL_HEAD_UNROLL = 4
DIL_PROB_SLOTS = 4
VMEM_LIMIT = 56 * 1024 * 1024


def _bucket_breaks():
    nb = N_REL_BUCKETS // 2
    max_exact = nb // 2
    rel = np.arange(-2 * REL_MAX_DISTANCE, 2 * REL_MAX_DISTANCE + 1)
    n = np.abs(rel)
    pos = np.log(np.maximum(n, 1) / max_exact) / math.log(REL_MAX_DISTANCE / max_exact) * (nb - max_exact)
    large = np.minimum(max_exact + np.floor(pos).astype(np.int64), nb - 1)
    bucket = np.where(rel > 0, nb, 0) + np.where(n < max_exact, n, large)
    breaks = [(int(rel[i]), int(bucket[i])) for i in range(1, len(rel)) if bucket[i] != bucket[i - 1]]
    return int(bucket[0]), breaks


FIRST_BUCKET, BUCKET_BREAKS = _bucket_breaks()
LAST_BUCKET = BUCKET_BREAKS[-1][1]
FAR_DIST = max(-BUCKET_BREAKS[0][0] + 1, BUCKET_BREAKS[-1][0])


def _bias_from_rel(rel, tab_ref, col):
    val = jnp.full(rel.shape, tab_ref[FIRST_BUCKET, col], F32)
    for thr, b in BUCKET_BREAKS:
        val = jnp.where(rel >= thr, tab_ref[b, col], val)
    return val


N_COL_GROUPS = 6
FIRST_DIL_GROUP = 3
W_RING = 3
assert N_COL_GROUPS % W_RING == 0


def _inproj_kernel(x_ref, g_ref, w_ref, cs_ref, o_ref, qt_ref, vt_ref, *rest, dilations):
    cls_refs = rest[:len(dilations)]
    h_ref, stage_ref, wbuf_ref, wsem_ref = rest[len(dilations):]
    i = pl.program_id(0)
    t = pl.program_id(1)
    n_rows = pl.num_programs(0)
    tm = x_ref.shape[0]
    tn = wbuf_ref.shape[2]
    n_dil = N_COL_GROUPS - FIRST_DIL_GROUP

    def weight_copy(step):
        col = (step % N_COL_GROUPS + FIRST_DIL_GROUP) % N_COL_GROUPS
        slot = step % W_RING
        return pltpu.make_async_copy(w_ref.at[:, pl.ds(col * tn, tn)], wbuf_ref.at[slot], wsem_ref.at[slot])

    @pl.when(t == 0)
    def _():
        x = x_ref[...]
        ms = jnp.mean(x * x, axis=-1, keepdims=True)
        h_ref[...] = (x * lax.rsqrt(ms + NORM_EPS) * g_ref[...]).astype(BF16)

    def matmul(keep, slot):
        acc = jnp.dot(h_ref[...], wbuf_ref[slot], preferred_element_type=F32) * cs_ref[...]
        if keep:
            o_ref[...] = acc.astype(o_ref.dtype)
        return acc

    def stage(acc, slot):
        for cb in range(stage_ref.shape[1]):
            stage_ref[slot, cb] = acc[:, cb * LANES:(cb + 1) * LANES]

    def regroup(slot):
        for cb in range(stage_ref.shape[1]):
            sl = slice(cb * LANES, (cb + 1) * LANES)
            for cls_ref, dil in zip(cls_refs, dilations):
                for c in range(dil):
                    cls_ref[c, :, sl] = stage_ref[slot, cb, pl.ds(c, tm // dil, stride=dil), :].astype(cls_ref.dtype)

    for step in range(N_COL_GROUPS):
        @pl.when(t == step)
        def _(step=step):
            if step == 0:
                @pl.when(i == 0)
                def _():
                    for first in range(W_RING - 1):
                        weight_copy(first).start()

            ahead = step + W_RING - 1
            if ahead < N_COL_GROUPS:
                weight_copy(ahead).start()
            else:
                @pl.when(i < n_rows - 1)
                def _():
                    weight_copy(ahead).start()
            weight_copy(step).wait()
            acc = matmul(keep=step not in (n_dil, n_dil + 2), slot=step % W_RING)
            if 1 <= step <= n_dil:
                regroup((step - 1) % 2)
            if step < n_dil:
                stage(acc, step % 2)
            if step == n_dil:
                qt_ref[...] = acc.T.astype(qt_ref.dtype)
            if step == n_dil + 2:
                vt_ref[...] = acc.T.astype(vt_ref.dtype)


def _in_proj(x2, gain, w_bf, colscale, dilations, *, tm=512):
    S, D = x2.shape
    N = w_bf.shape[1]
    tn = DIFF_QK_COLS
    assert DIFF_WIDTH == tn and DIL_WIDTH == tn and N == N_COL_GROUPS * tn
    n_dil = N_COL_GROUPS - FIRST_DIL_GROUP
    kern = functools.partial(_inproj_kernel, dilations=dilations)

    def group(t):
        return (t + FIRST_DIL_GROUP) % N_COL_GROUPS

    cls_specs = [pl.BlockSpec((d, tm // d, tn), lambda i, t: (0, i, jnp.clip(t - 1, 0, n_dil - 1)))
                 for d in dilations]
    cls_shapes = [jax.ShapeDtypeStruct((d, S // d, n_dil * tn), BF16) for d in dilations]
    return pl.pallas_call(
        kern,
        grid=(S // tm, N_COL_GROUPS),
        in_specs=[
            pl.BlockSpec((tm, D), lambda i, t: (i, 0)),
            pl.BlockSpec((1, D), lambda i, t: (0, 0)),
            pl.BlockSpec(memory_space=pl.ANY),
            pl.BlockSpec((1, tn), lambda i, t: (0, group(t))),
        ],
        out_specs=[
            pl.BlockSpec((tm, tn), lambda i, t: (i, jnp.where(t < n_dil, t + 1, 0))),
            pl.BlockSpec((tn, tm), lambda i, t: (0, i)),
            pl.BlockSpec((tn, tm), lambda i, t: (0, i)),
        ] + cls_specs,
        out_shape=[
            jax.ShapeDtypeStruct((S, (n_dil + 1) * tn), BF16),
            jax.ShapeDtypeStruct((tn, S), BF16),
            jax.ShapeDtypeStruct((tn, S), BF16),
        ] + cls_shapes,
        scratch_shapes=[pltpu.VMEM((tm, D), BF16), pltpu.VMEM((2, tn // LANES, tm, LANES), F32),
                        pltpu.VMEM((W_RING, D, tn), BF16), pltpu.SemaphoreType.DMA((W_RING,))],
        compiler_params=pltpu.CompilerParams(
            dimension_semantics=("arbitrary", "arbitrary"), vmem_limit_bytes=VMEM_LIMIT),
        name="in_proj",
    )(x2, gain, w_bf, colscale)


def _diff_attn_kernel(*refs, T, QP, n_near, n_tiles, n_cast):
    tab_ref, q1t_ref, q2t_ref, k1_ref, k2_ref, vt_ref, lam_ref, gain_ref = refs[:8]
    cast_in, o_ref, cast_out = refs[8:8 + n_cast], refs[8 + n_cast], refs[9 + n_cast:9 + 2 * n_cast]
    bias_ref, m_ref, l_ref, acc_ref, p_ref, shift_ref, knorm_ref, prev_ref = refs[9 + 2 * n_cast:]
    _diff_attn_body(tab_ref, q1t_ref, q2t_ref, k1_ref, k2_ref, vt_ref, lam_ref, gain_ref, o_ref,
                    bias_ref, m_ref, l_ref, acc_ref, p_ref, shift_ref, knorm_ref, prev_ref,
                    T=T, QP=QP, n_near=n_near, n_tiles=n_tiles)
    for src, dst in zip(cast_in, cast_out):
        dst[...] = src[...].astype(dst.dtype)


def _diff_attn_body(tab_ref, q1t_ref, q2t_ref, k1_ref, k2_ref, vt_ref, lam_ref, gain_ref, o_ref,
                    bias_ref, m_ref, l_ref, acc_ref, p_ref, shift_ref, knorm_ref, prev_ref,
                    *, T, QP, n_near, n_tiles):
    h = pl.program_id(0)
    qi = pl.program_id(1)
    n_chains = 2 * (T // QP)
    qts = (q1t_ref, q2t_ref)
    ks = (k1_ref, k2_ref)

    def chain_of(i):
        qp = i // 2
        return i % 2, slice(qp * QP, (qp + 1) * QP)

    @pl.when(qi == 0)
    def _():
        for mi in range(2):
            def knorm(t, best, mi=mi):
                k = ks[mi][pl.ds(pl.multiple_of(t * T, T), T), :].astype(F32)
                return jnp.maximum(best, jnp.max(jnp.sum(k * k, axis=1, keepdims=True), axis=0, keepdims=True))
            knorm_ref[mi] = lax.fori_loop(0, n_tiles, knorm, jnp.zeros((1, 1), F32))

        x = lax.broadcasted_iota(jnp.int32, (F32_SUBLANES, 2 * T), 1)
        x = jnp.where(x < T, x, x - 2 * T)
        for di, d in enumerate(range(-n_near, n_near + 1)):
            g = _bias_from_rel(d * T - x, tab_ref, h)
            base = jnp.broadcast_to(g[0:1, :], (LANES, 2 * T))
            for rb in range(T // LANES):
                blk = pltpu.roll(base, rb * LANES, 1, stride=1, stride_axis=0)
                bias_ref[di, rb * LANES:(rb + 1) * LANES, :] = blk[:, :T]

    c_left = tab_ref[FIRST_BUCKET, h]
    c_right = tab_ref[LAST_BUCKET, h]

    bias_max = tab_ref[0, h]
    for b in range(1, N_REL_BUCKETS):
        bias_max = jnp.maximum(bias_max, tab_ref[b, h])

    for mi in range(2):
        q = qts[mi][...].astype(F32)
        qnorm2 = jnp.sum(q * q, axis=0, keepdims=True)
        shift_ref[mi] = jnp.sqrt(qnorm2 * knorm_ref[mi]) * SHIFT_MARGIN + bias_max
    l_ref[...] = jnp.zeros(l_ref.shape, F32)
    acc_ref[...] = jnp.zeros(acc_ref.shape, F32)
    p_ref[n_chains - 1] = jnp.zeros((T, QP), BF16)
    prev_ref[0] = 0

    def score_stage(i, k0, bias_di, bias_const):
        mi, qs = chain_of(i)
        s = jnp.dot(ks[mi][pl.ds(k0, T), :], qts[mi][:, qs], preferred_element_type=F32)
        if bias_di is not None:
            p = jnp.exp2(s + bias_ref[bias_di, :, qs] - shift_ref[mi, :, qs])
        else:
            p = jnp.exp2(s - (shift_ref[mi, :, qs] - bias_const))
        l_ref[mi, :, qs] += jnp.sum(p, axis=0, keepdims=True)
        p_ref[i] = p.astype(BF16)

    def value_stage(i, k0):
        mi, qs = chain_of(i)
        acc_ref[mi, :, qs] += jnp.dot(vt_ref[:, pl.ds(k0, T)], p_ref[i], preferred_element_type=F32)

    def tile(kt, bias_di, bias_const, prev_kt=None):
        k0 = pl.multiple_of(kt * T, T)
        pk0 = pl.multiple_of((prev_ref[0] if prev_kt is None else prev_kt) * T, T)
        for i in range(n_chains):
            value_stage((i - 1) % n_chains, pk0 if i < 1 else k0)
            score_stage(i, k0, bias_di, bias_const)
        prev_ref[0] = kt

    lo = jnp.maximum(qi - n_near, 0)
    hi = jnp.minimum(qi + n_near + 1, n_tiles)

    for di, d in enumerate(range(-n_near, n_near + 1)):
        kt = qi + d

        @pl.when(jnp.logical_and(kt >= 0, kt < n_tiles))
        def _(di=di, kt=kt):
            tile(kt, di, None)

    n_far = n_tiles - (hi - lo)

    def far_run(first, length):
        prev_kt = None
        for j in range(length):
            before = first + j < lo
            kt = jnp.where(before, first + j, first + j + (hi - lo))
            tile(kt, None, jnp.where(before, c_left, c_right), prev_kt=prev_kt)
            prev_kt = kt

    done = 0
    run = 1
    while run < FAR_UNROLL:
        has_run = jnp.bitwise_and(n_far, run)

        @pl.when(has_run != 0)
        def _(done=done, run=run):
            far_run(done, run)

        done = done + has_run
        run *= 2

    @pl.loop(0, n_far // FAR_UNROLL)
    def _(trip, done=done):
        far_run(done + trip * FAR_UNROLL, FAR_UNROLL)

    value_stage(n_chains - 1, pl.multiple_of(prev_ref[0] * T, T))

    @pl.when(jnp.min(l_ref[...]) < MIN_DENOMINATOR)
    def _():
        m_ref[...] = jnp.full(m_ref.shape, -jnp.inf, F32)
        l_ref[...] = jnp.zeros(l_ref.shape, F32)
        acc_ref[...] = jnp.zeros(acc_ref.shape, F32)

        @pl.loop(0, n_tiles)
        def _(kt):
            k0 = pl.multiple_of(kt * T, T)
            d = kt - qi
            near = jnp.abs(d) <= n_near
            di = jnp.clip(d + n_near, 0, 2 * n_near)
            c_far = jnp.where(d < 0, c_left, c_right)
            for i in range(n_chains):
                mi, qs = chain_of(i)
                s = jnp.dot(ks[mi][pl.ds(k0, T), :], qts[mi][:, qs], preferred_element_type=F32)
                s = s + jnp.where(near, bias_ref[di, :, qs], c_far)
                m = m_ref[mi, :, qs]
                m_new = jnp.maximum(m, jnp.max(s, axis=0, keepdims=True))
                alpha = jnp.exp2(m - m_new)
                p = jnp.exp2(s - m_new)
                m_ref[mi, :, qs] = m_new
                l_ref[mi, :, qs] = alpha * l_ref[mi, :, qs] + jnp.sum(p, axis=0, keepdims=True)
                pv = jnp.dot(vt_ref[:, pl.ds(k0, T)], p.astype(BF16), preferred_element_type=F32)
                acc_ref[mi, :, qs] = alpha * acc_ref[mi, :, qs] + pv

    lam = lam_ref[0, 0]
    o = acc_ref[0] / l_ref[0] - lam * (acc_ref[1] / l_ref[1])
    ms = jnp.mean(o * o, axis=0, keepdims=True)
    o = o * lax.rsqrt(ms + SUBLN_EPS) * (gain_ref[...] * (1.0 - LAM_INIT))
    o_ref[...] = o.T.astype(o_ref.dtype)


def _lambda_kernel(q1_ref, k1_ref, q2_ref, k2_ref, o_ref):
    a = jnp.sum(q1_ref[...] * k1_ref[...], axis=-1, keepdims=True)
    b = jnp.sum(q2_ref[...] * k2_ref[...], axis=-1, keepdims=True)
    o_ref[...] = jnp.exp(a) - jnp.exp(b) + LAM_INIT


def _lambda(lq1, lk1, lq2, lk2):
    return pl.pallas_call(
        _lambda_kernel, out_shape=jax.ShapeDtypeStruct((1, 1), F32), name="diff_lambda",
    )(lq1, lk1, lq2, lk2)


def _diff_attention(tab_diff, proj, qt_all, vt_all, lam, gain_col, f32_weights):
    S = proj.shape[0]
    T = DIFF_TILE
    n_tiles = S // T
    n_near = -(-(FAR_DIST - 1) // T)
    kern = functools.partial(_diff_attn_kernel, T=T, QP=DIFF_QUERY_PANEL, n_near=n_near, n_tiles=n_tiles,
                             n_cast=len(f32_weights))

    n_steps = N_DIFF_HEADS * n_tiles
    cast_specs = []
    for w in f32_weights:
        rows, cols = w.shape
        every = next(e for e in (1, 2, 4, 8) if rows % (n_steps // e) == 0
                     and (rows // (n_steps // e)) % BF16_SUBLANES == 0)
        cast_specs.append(pl.BlockSpec((rows // (n_steps // every), cols),
                                       lambda h, i, every=every: ((h * n_tiles + i) // every, 0)))
    k_block0 = 0
    smem = pl.BlockSpec(memory_space=pltpu.SMEM)
    return pl.pallas_call(
        kern,
        grid=(N_DIFF_HEADS, n_tiles),
        in_specs=[
            smem,
            pl.BlockSpec((HEAD_DIM, T), lambda h, i: (2 * h, i)),
            pl.BlockSpec((HEAD_DIM, T), lambda h, i: (2 * h + 1, i)),
            pl.BlockSpec((S, HEAD_DIM), lambda h, i: (0, k_block0 + 2 * h), pipeline_mode=pl.Buffered(1)),
            pl.BlockSpec((S, HEAD_DIM), lambda h, i: (0, k_block0 + 2 * h + 1), pipeline_mode=pl.Buffered(1)),
            pl.BlockSpec((DIFF_V_DIM, S), lambda h, i: (h, 0), pipeline_mode=pl.Buffered(1)),
            smem,
            pl.BlockSpec((DIFF_V_DIM, 1), lambda h, i: (0, 0)),
        ] + cast_specs,
        out_specs=[pl.BlockSpec((T, DIFF_V_DIM), lambda h, i: (i, h))] + cast_specs,
        out_shape=[jax.ShapeDtypeStruct((S, DIFF_WIDTH), BF16)]
        + [jax.ShapeDtypeStruct(w.shape, BF16) for w in f32_weights],
        scratch_shapes=[pltpu.VMEM((2 * n_near + 1, T, T), F32),
                        pltpu.VMEM((2, 1, T), F32), pltpu.VMEM((2, 1, T), F32),
                        pltpu.VMEM((2, DIFF_V_DIM, T), F32),
                        pltpu.VMEM((2 * (T // DIFF_QUERY_PANEL), T, DIFF_QUERY_PANEL), BF16),
                        pltpu.VMEM((2, 1, T), F32), pltpu.VMEM((2, 1, 1), F32),
                        pltpu.SMEM((1,), jnp.int32)],
        compiler_params=pltpu.CompilerParams(
            dimension_semantics=("arbitrary", "arbitrary"), vmem_limit_bytes=VMEM_LIMIT),
        name="diff_attn",
    )(tab_diff, qt_all, qt_all, proj, proj, vt_all, lam, gain_col, *f32_weights)


def _dilated_kernel(tab_ref, q_ref, kp_ref, km_ref, kn_ref, vp_ref, vm_ref, vn_ref, o_ref, lse_ref,
                    bias_ref, kx_ref, vx_ref, *, R, B, half, dilation, n_chunks):
    c = pl.program_id(0)
    n = pl.program_id(1)
    W = B + 2 * half
    nblk = R // B

    @pl.when(jnp.logical_and(c == 0, n == 0))
    def _():
        rows = F32_SUBLANES
        col = lax.broadcasted_iota(jnp.int32, (rows, W), 1)
        row = lax.broadcasted_iota(jnp.int32, (rows, W), 0)
        for hh in range(N_DIL_HEADS):
            def fill(r, carry, hh=hh):
                r0 = pl.multiple_of(r * rows, rows)
                off = col - half - (row + r0)
                bias = _bias_from_rel(off * dilation, tab_ref, N_DIFF_HEADS + hh)
                base = jnp.where(jnp.abs(off) <= half, bias, NEG_INF * LOG2E)
                bias_ref[hh, 1, pl.ds(r0, rows), :] = base
                bias_ref[hh, 0, pl.ds(r0, rows), :] = jnp.where(col >= half, base, NEG_INF * LOG2E)
                bias_ref[hh, 2, pl.ds(r0, rows), :] = jnp.where(col < B + half, base, NEG_INF * LOG2E)
                return carry
            lax.fori_loop(0, B // rows, fill, 0)

    kx_ref[0:half, :] = kp_ref[...]
    kx_ref[half:half + R, :] = km_ref[...]
    kx_ref[half + R:, :] = kn_ref[...]
    vx_ref[0:half, :] = vp_ref[...]
    vx_ref[half:half + R, :] = vm_ref[...]
    vx_ref[half + R:, :] = vn_ref[...]

    def chain(hh, b):
        c0 = pl.multiple_of(hh * HEAD_DIM, HEAD_DIM)
        r0 = b * B
        var = 1
        if b == 0:
            var = jnp.where(n == 0, 0, var)
        if b == nblk - 1:
            var = jnp.where(n == n_chunks - 1, 2, var)
        q = q_ref[pl.ds(r0, B), pl.ds(c0, HEAD_DIM)]
        k = kx_ref[pl.ds(r0, W), pl.ds(c0, HEAD_DIM)]
        v = vx_ref[pl.ds(r0, W), pl.ds(c0, HEAD_DIM)]
        s = lax.dot_general(q, k, (((1,), (1,)), ((), ())), preferred_element_type=F32)
        s = s + bias_ref[hh, var]
        m = jnp.max(s, axis=-1, keepdims=True)
        e = jnp.exp2(s - m)
        den = jnp.sum(e, axis=-1, keepdims=True)
        o = jnp.dot(e.astype(BF16), v, preferred_element_type=F32) / den
        o_ref[pl.ds(r0, B), pl.ds(c0, HEAD_DIM)] = o.astype(o_ref.dtype)
        lse = m + jnp.log2(den)
        lse_ref[pl.ds(r0, B), pl.ds(c0, HEAD_DIM)] = jnp.broadcast_to(lse, (B, HEAD_DIM))

    @pl.loop(0, N_DIL_HEADS // DIL_HEAD_UNROLL)
    def _(hg):
        for u in range(DIL_HEAD_UNROLL):
            for b in range(nblk):
                chain(hg * DIL_HEAD_UNROLL + u, b)


def _dilated_pattern(tab, qkv, col_blk0, window, dilation, *, B=256):
    _, L, _ = qkv.shape
    R = min(DIL_CHUNK, L)
    half = window // (2 * dilation)
    assert L % R == 0 and R % B == 0 and half % BF16_SUBLANES == 0 and R % half == 0
    n_chunks = L // R
    q_blk, k_blk, v_blk = col_blk0, col_blk0 + 1, col_blk0 + 2
    hb = R // half
    n_hblk = L // half

    def main(blk):
        return pl.BlockSpec((None, R, DIL_WIDTH), lambda c, n: (c, n, blk))

    def prev(blk):
        return pl.BlockSpec((None, half, DIL_WIDTH), lambda c, n: (c, jnp.maximum(n * hb - 1, 0), blk))

    def nxt(blk):
        return pl.BlockSpec((None, half, DIL_WIDTH), lambda c, n: (c, jnp.minimum((n + 1) * hb, n_hblk - 1), blk))

    kern = functools.partial(_dilated_kernel, R=R, B=B, half=half, dilation=dilation, n_chunks=n_chunks)
    out_spec = pl.BlockSpec((None, R, DIL_WIDTH), lambda c, n: (c, n, 0))
    return pl.pallas_call(
        kern,
        grid=(dilation, n_chunks),
        in_specs=[pl.BlockSpec(memory_space=pltpu.SMEM),
                  main(q_blk), prev(k_blk), main(k_blk), nxt(k_blk), prev(v_blk), main(v_blk), nxt(v_blk)],
        out_specs=[out_spec, out_spec],
        out_shape=[jax.ShapeDtypeStruct((dilation, L, DIL_WIDTH), BF16),
                   jax.ShapeDtypeStruct((dilation, L, DIL_WIDTH), F32)],
        scratch_shapes=[pltpu.VMEM((N_DIL_HEADS, 3, B, B + 2 * half), F32),
                        pltpu.VMEM((R + 2 * half, DIL_WIDTH), BF16),
                        pltpu.VMEM((R + 2 * half, DIL_WIDTH), BF16)],
        compiler_params=pltpu.CompilerParams(
            dimension_semantics=("arbitrary", "arbitrary"), vmem_limit_bytes=VMEM_LIMIT),
        name=f"dilated_d{dilation}",
    )(tab, qkv, qkv, qkv, qkv, qkv, qkv, qkv)


def _combine_kernel(*refs, dilations):
    n = len(dilations)
    o_refs, l_refs = refs[:n], refs[n:2 * n]
    g_ref, out_ref = refs[2 * n], refs[2 * n + 1]
    scratch = refs[2 * n + 2:]
    tm = out_ref.shape[0]

    for hh in range(N_DIL_HEADS):
        sl = slice(hh * HEAD_DIM, (hh + 1) * HEAD_DIM)
        outs, lses = [], []
        si = 0
        for o_ref, l_ref, dil in zip(o_refs, l_refs, dilations):
            if dil == 1:
                outs.append(o_ref[0, :, sl].astype(F32))
                lses.append(l_ref[0, :, sl])
                continue
            os_ref, ls_ref = scratch[si], scratch[si + 1]
            si += 2
            for c in range(dil):
                os_ref[hh, pl.ds(c, tm // dil, stride=dil), :] = o_ref[c, :, sl].astype(F32)
                ls_ref[hh, pl.ds(c, tm // dil, stride=dil), :] = l_ref[c, :, sl]
            outs.append(os_ref[hh])
            lses.append(ls_ref[hh])

        m = functools.reduce(jnp.maximum, lses)
        ws = [jnp.exp2(l - m) for l in lses]
        tot = functools.reduce(lambda a, b: a + b, ws)
        oh = functools.reduce(lambda a, b: a + b, [(w / tot) * op for w, op in zip(ws, outs)])
        ms = jnp.mean(oh * oh, axis=-1, keepdims=True)
        out_ref[:, sl] = (oh * lax.rsqrt(ms + NORM_EPS) * g_ref[:, sl]).astype(out_ref.dtype)


def _combine(outs, lses, gain, dilations, *, tm=512):
    S = outs[0].shape[0] * outs[0].shape[1]
    specs = [pl.BlockSpec((d, tm // d, DIL_WIDTH), lambda i: (0, i, 0)) for d in dilations]
    n_scr = sum(1 for d in dilations if d != 1)
    return pl.pallas_call(
        functools.partial(_combine_kernel, dilations=dilations),
        grid=(S // tm,),
        in_specs=specs + specs + [pl.BlockSpec((1, DIL_WIDTH), lambda i: (0, 0))],
        scratch_shapes=[pltpu.VMEM((N_DIL_HEADS, tm, HEAD_DIM), F32)] * (2 * n_scr),
        out_specs=pl.BlockSpec((tm, DIL_WIDTH), lambda i: (i, 0)),
        out_shape=jax.ShapeDtypeStruct((S, DIL_WIDTH), BF16),
        compiler_params=pltpu.CompilerParams(
            dimension_semantics=("arbitrary",), vmem_limit_bytes=VMEM_LIMIT),
        name="dilated_combine",
    )(*outs, *lses, gain)


def _key_norm2_kernel(k_ref, o_ref):
    @pl.when(pl.program_id(0) == 0)
    def _():
        o_ref[...] = jnp.zeros(o_ref.shape, F32)

    k = k_ref[...].astype(F32)
    for hh in range(N_DIL_HEADS):
        kh = k[:, hh * HEAD_DIM:(hh + 1) * HEAD_DIM]
        n2 = jnp.max(jnp.sum(kh * kh, axis=1, keepdims=True), axis=0, keepdims=True)
        o_ref[hh:hh + 1, :] = jnp.maximum(o_ref[hh:hh + 1, :], jnp.broadcast_to(n2, (1, LANES)))


def _key_norm2(proj, k_blk, *, tm=1024):
    S = proj.shape[0]
    return pl.pallas_call(
        _key_norm2_kernel,
        grid=(S // tm,),
        in_specs=[pl.BlockSpec((tm, DIL_WIDTH), lambda i: (i, k_blk))],
        out_specs=pl.BlockSpec((N_DIL_HEADS, LANES), lambda i: (0, 0)),
        out_shape=jax.ShapeDtypeStruct((N_DIL_HEADS, LANES), F32),
        compiler_params=pltpu.CompilerParams(dimension_semantics=("arbitrary",), vmem_limit_bytes=VMEM_LIMIT),
        name="dilated_key_norm",
    )(proj)


def _dilated_fast_kernel(tab_ref, kn2_ref, q_ref, kp_ref, km_ref, kn_ref, vp_ref, vm_ref, vn_ref, num_ref, den_ref,
                         bias_ref, kx_ref, vx_ref, p_ref, *, R, B, half, dilation, n_chunks):
    c = pl.program_id(0)
    n = pl.program_id(1)
    W = B + 2 * half
    nblk = R // B
    VW = 2 * HEAD_DIM

    @pl.when(jnp.logical_and(c == 0, n == 0))
    def _():
        rows = F32_SUBLANES
        col = lax.broadcasted_iota(jnp.int32, (rows, W), 1)
        row = lax.broadcasted_iota(jnp.int32, (rows, W), 0)
        for hh in range(N_DIL_HEADS):
            def fill(r, carry, hh=hh):
                r0 = pl.multiple_of(r * rows, rows)
                off = col - half - (row + r0)
                bias = _bias_from_rel(off * dilation, tab_ref, N_DIFF_HEADS + hh)
                base = jnp.where(jnp.abs(off) <= half, bias, NEG_INF * LOG2E)
                bias_ref[hh, 1, pl.ds(r0, rows), :] = base
                bias_ref[hh, 0, pl.ds(r0, rows), :] = jnp.where(col >= half, base, NEG_INF * LOG2E)
                bias_ref[hh, 2, pl.ds(r0, rows), :] = jnp.where(col < B + half, base, NEG_INF * LOG2E)
                return carry
            lax.fori_loop(0, B // rows, fill, 0)
        vx_ref[...] = jnp.ones(vx_ref.shape, BF16)

    kx_ref[0:half, :] = kp_ref[...]
    kx_ref[half:half + R, :] = km_ref[...]
    kx_ref[half + R:, :] = kn_ref[...]
    for hh in range(N_DIL_HEADS):
        src = slice(hh * HEAD_DIM, (hh + 1) * HEAD_DIM)
        dst = slice(hh * VW, hh * VW + HEAD_DIM)
        vx_ref[0:half, dst] = vp_ref[:, src]
        vx_ref[half:half + R, dst] = vm_ref[:, src]
        vx_ref[half + R:, dst] = vn_ref[:, src]

    den_ref[...] = jnp.zeros(den_ref.shape, F32)
    n_slots = p_ref.shape[0]
    bias_max = []
    for hh in range(N_DIL_HEADS):
        bm = tab_ref[0, N_DIFF_HEADS + hh]
        for bk in range(1, N_REL_BUCKETS):
            bm = jnp.maximum(bm, tab_ref[bk, N_DIFF_HEADS + hh])
        bias_max.append(bm)

    def score_stage(hh, b, slot):
        cs = slice(hh * HEAD_DIM, (hh + 1) * HEAD_DIM)
        r0 = b * B
        var = 1
        if b == 0:
            var = jnp.where(n == 0, 0, var)
        if b == nblk - 1:
            var = jnp.where(n == n_chunks - 1, 2, var)
        q = q_ref[r0:r0 + B, cs]
        qf = q.astype(F32)
        qn2 = jnp.sum(qf * qf, axis=1, keepdims=True)
        shift = jnp.sqrt(qn2 * kn2_ref[hh:hh + 1, 0:1]) * SHIFT_MARGIN + bias_max[hh]
        s = lax.dot_general(q, kx_ref[r0:r0 + W, cs], (((1,), (1,)), ((), ())), preferred_element_type=F32)
        p_ref[slot] = jnp.exp2(s + bias_ref[hh, var] - shift).astype(BF16)

    def value_stage(hh, b, slot):
        cs = slice(hh * HEAD_DIM, (hh + 1) * HEAD_DIM)
        r0 = b * B
        nd = jnp.dot(p_ref[slot], vx_ref[r0:r0 + W, hh * VW:(hh + 1) * VW], preferred_element_type=F32)
        num_ref[r0:r0 + B, cs] = nd[:, :HEAD_DIM].astype(num_ref.dtype)
        den_ref[r0:r0 + B, hh:hh + 1] = nd[:, HEAD_DIM + hh:HEAD_DIM + hh + 1]

    chains = [(hh, b) for hh in range(N_DIL_HEADS) for b in range(nblk)]
    for i, (hh, b) in enumerate(chains):
        if i > 0:
            value_stage(*chains[i - 1], (i - 1) % n_slots)
        score_stage(hh, b, i % n_slots)
    value_stage(*chains[-1], (len(chains) - 1) % n_slots)


def _dilated_fast(tab, kn2, qkv, col_blk0, window, dilation, *, B=128):
    _, L, _ = qkv.shape
    R = min(DIL_CHUNK, L)
    half = window // (2 * dilation)
    assert L % R == 0 and R % B == 0 and half % BF16_SUBLANES == 0 and R % half == 0
    n_chunks = L // R
    q_blk, k_blk, v_blk = col_blk0, col_blk0 + 1, col_blk0 + 2
    hb = R // half
    n_hblk = L // half

    def main(blk):
        return pl.BlockSpec((None, R, DIL_WIDTH), lambda c, n: (c, n, blk))

    def prev(blk):
        return pl.BlockSpec((None, half, DIL_WIDTH), lambda c, n: (c, jnp.maximum(n * hb - 1, 0), blk))

    def nxt(blk):
        return pl.BlockSpec((None, half, DIL_WIDTH), lambda c, n: (c, jnp.minimum((n + 1) * hb, n_hblk - 1), blk))

    kern = functools.partial(_dilated_fast_kernel, R=R, B=B, half=half, dilation=dilation, n_chunks=n_chunks)
    out_spec = pl.BlockSpec((None, R, DIL_WIDTH), lambda c, n: (c, n, 0))
    return pl.pallas_call(
        kern,
        grid=(dilation, n_chunks),
        in_specs=[pl.BlockSpec(memory_space=pltpu.SMEM),
                  pl.BlockSpec((N_DIL_HEADS, LANES), lambda c, n: (0, 0)),
                  main(q_blk), prev(k_blk), main(k_blk), nxt(k_blk), prev(v_blk), main(v_blk), nxt(v_blk)],
        out_specs=[out_spec, pl.BlockSpec((None, R, LANES), lambda c, n: (c, n, 0))],
        out_shape=[jax.ShapeDtypeStruct((dilation, L, DIL_WIDTH), BF16),
                   jax.ShapeDtypeStruct((dilation, L, LANES), F32)],
        scratch_shapes=[pltpu.VMEM((N_DIL_HEADS, 3, B, B + 2 * half), F32),
                        pltpu.VMEM((R + 2 * half, DIL_WIDTH), BF16),
                        pltpu.VMEM((R + 2 * half, 2 * DIL_WIDTH), BF16),
                        pltpu.VMEM((DIL_PROB_SLOTS, B, B + 2 * half), BF16)],
        compiler_params=pltpu.CompilerParams(
            dimension_semantics=("arbitrary", "arbitrary"), vmem_limit_bytes=VMEM_LIMIT),
        name=f"dilated_fast_d{dilation}",
    )(tab, kn2, qkv, qkv, qkv, qkv, qkv, qkv, qkv)


def _combine_fast_kernel(*refs, dilations):
    n = len(dilations)
    n_refs, d_refs = refs[:n], refs[n:2 * n]
    g_ref, out_ref, dmin_ref = refs[2 * n], refs[2 * n + 1], refs[2 * n + 2]
    scratch = refs[2 * n + 3:]
    tm = out_ref.shape[0]

    den_all = None
    si = 0
    for d_ref, dil in zip(d_refs, dilations):
        if dil == 1:
            den_p = d_ref[0]
        else:
            ds_ref = scratch[si + 1]
            for c in range(dil):
                ds_ref[pl.ds(c, tm // dil, stride=dil), :] = d_ref[c]
            den_p = ds_ref[...]
            si += 2
        den_all = den_p if den_all is None else den_all + den_p

    for hh in range(N_DIL_HEADS):
        sl = slice(hh * HEAD_DIM, (hh + 1) * HEAD_DIM)
        nums = []
        si = 0
        for n_ref, dil in zip(n_refs, dilations):
            if dil == 1:
                nums.append(n_ref[0, :, sl].astype(F32))
                continue
            ns_ref = scratch[si]
            si += 2
            for c in range(dil):
                ns_ref[hh, pl.ds(c, tm // dil, stride=dil), :] = n_ref[c, :, sl].astype(F32)
            nums.append(ns_ref[hh])

        den = den_all[:, hh:hh + 1]
        oh = functools.reduce(lambda a, b: a + b, nums) / den
        ms = jnp.mean(oh * oh, axis=-1, keepdims=True)
        out_ref[:, sl] = (oh * lax.rsqrt(ms + NORM_EPS) * g_ref[:, sl]).astype(out_ref.dtype)
        dmin_ref[0, hh:hh + 1, :] = jnp.broadcast_to(jnp.min(den, axis=0, keepdims=True), (1, LANES))


def _combine_fast(nums, dens, gain, dilations, *, tm=512):
    S = nums[0].shape[0] * nums[0].shape[1]
    num_specs = [pl.BlockSpec((d, tm // d, DIL_WIDTH), lambda i: (0, i, 0)) for d in dilations]
    den_specs = [pl.BlockSpec((d, tm // d, LANES), lambda i: (0, i, 0)) for d in dilations]
    n_scr = sum(1 for d in dilations if d != 1)
    return pl.pallas_call(
        functools.partial(_combine_fast_kernel, dilations=dilations),
        grid=(S // tm,),
        in_specs=num_specs + den_specs + [pl.BlockSpec((1, DIL_WIDTH), lambda i: (0, 0))],
        scratch_shapes=[pltpu.VMEM((N_DIL_HEADS, tm, HEAD_DIM), F32), pltpu.VMEM((tm, LANES), F32)] * n_scr,
        out_specs=[pl.BlockSpec((tm, DIL_WIDTH), lambda i: (i, 0)),
                   pl.BlockSpec((1, N_DIL_HEADS, LANES), lambda i: (i, 0, 0))],
        out_shape=[jax.ShapeDtypeStruct((S, DIL_WIDTH), BF16),
                   jax.ShapeDtypeStruct((S // tm, N_DIL_HEADS, LANES), F32)],
        compiler_params=pltpu.CompilerParams(
            dimension_semantics=("arbitrary",), vmem_limit_bytes=VMEM_LIMIT),
        name="dilated_combine_fast",
    )(*nums, *dens, gain)


def _outproj_kernel(od_ref, ol_ref, wd_ref, wl_ref, x_ref, g_ref, x1_ref, h2_ref):
    acc = jnp.dot(od_ref[...], wd_ref[...], preferred_element_type=F32)
    acc = acc + jnp.dot(ol_ref[...], wl_ref[...], preferred_element_type=F32)
    x1 = x_ref[...] + acc
    x1_ref[...] = x1
    ms = jnp.mean(x1 * x1, axis=-1, keepdims=True)
    h2_ref[...] = (x1 * lax.rsqrt(ms + NORM_EPS) * g_ref[...]).astype(h2_ref.dtype)


def _out_proj(o_d, o_l, w_bf, x2, gain, *, tm=512):
    S, D = x2.shape
    return pl.pallas_call(
        _outproj_kernel,
        grid=(S // tm,),
        in_specs=[
            pl.BlockSpec((tm, DIFF_WIDTH), lambda i: (i, 0)),
            pl.BlockSpec((tm, DIL_WIDTH), lambda i: (i, 0)),
            pl.BlockSpec((DIFF_WIDTH, D), lambda i: (0, 0)),
            pl.BlockSpec((DIL_WIDTH, D), lambda i: (1, 0)),
            pl.BlockSpec((tm, D), lambda i: (i, 0)),
            pl.BlockSpec((1, D), lambda i: (0, 0)),
        ],
        out_specs=[pl.BlockSpec((tm, D), lambda i: (i, 0)), pl.BlockSpec((tm, D), lambda i: (i, 0))],
        out_shape=[jax.ShapeDtypeStruct((S, D), F32), jax.ShapeDtypeStruct((S, D), BF16)],
        compiler_params=pltpu.CompilerParams(
            dimension_semantics=("arbitrary",), vmem_limit_bytes=VMEM_LIMIT),
        name="out_proj",
    )(o_d, o_l, w_bf, w_bf, x2, gain)


def _ffn_up_kernel(hm_ref, hp_ref, hn_ref, wg_ref, wu_ref, cw_ref, cb_ref, o_ref, lhs_ref, *, tm, n_row_tiles):
    i = pl.program_id(0)
    j = pl.program_id(1)
    halo = BF16_SUBLANES

    @pl.when(j == 0)
    def _():
        lhs_ref[0:halo, :] = jnp.where(i == 0, jnp.zeros_like(hp_ref[...]), hp_ref[...])
        lhs_ref[halo:halo + tm, :] = hm_ref[...]
        lhs_ref[halo + tm:, :] = jnp.where(i == n_row_tiles - 1, jnp.zeros_like(hn_ref[...]), hn_ref[...])

    sub = tm // FFN_UP_ROW_SPLIT
    rows = sub + 2 * halo
    for r0 in range(0, tm, sub):
        g = jnp.dot(lhs_ref[r0:r0 + rows, :], wg_ref[...], preferred_element_type=F32)
        u = jnp.dot(lhs_ref[halo + r0:halo + r0 + sub, :], wu_ref[...], preferred_element_type=F32)
        g_prev = pltpu.roll(g, 1, axis=0)
        g_next = pltpu.roll(g, rows - 1, axis=0)
        y = cw_ref[0:1, :] * g_prev + cw_ref[1:2, :] * g + cw_ref[2:3, :] * g_next + cb_ref[...]
        y = y[halo:halo + sub, :]
        act = y * (1.0 / (1.0 + jnp.exp(-y))) * u
        o_ref[r0:r0 + sub, :] = act.astype(o_ref.dtype)


def _ffn_up(h2, w_bf, conv_w, conv_b, *, tm=2048, tn=512):
    S, D = h2.shape
    d_ff = conv_w.shape[1]
    assert d_ff % tn == 0
    nj = d_ff // tn
    ni = S // tm
    hb = tm // BF16_SUBLANES
    n_hblk = S // BF16_SUBLANES
    kern = functools.partial(_ffn_up_kernel, tm=tm, n_row_tiles=ni)
    return pl.pallas_call(
        kern,
        grid=(ni, nj),
        in_specs=[
            pl.BlockSpec((tm, D), lambda i, j: (i, 0)),
            pl.BlockSpec((BF16_SUBLANES, D), lambda i, j: (jnp.maximum(i * hb - 1, 0), 0)),
            pl.BlockSpec((BF16_SUBLANES, D), lambda i, j: (jnp.minimum((i + 1) * hb, n_hblk - 1), 0)),
            pl.BlockSpec((D, tn), lambda i, j: (0, j)),
            pl.BlockSpec((D, tn), lambda i, j: (0, nj + j)),
            pl.BlockSpec((3, tn), lambda i, j: (0, j)),
            pl.BlockSpec((1, tn), lambda i, j: (0, j)),
        ],
        out_specs=pl.BlockSpec((tm, tn), lambda i, j: (i, j)),
        out_shape=jax.ShapeDtypeStruct((S, d_ff), BF16),
        scratch_shapes=[pltpu.VMEM((tm + 2 * BF16_SUBLANES, D), BF16)],
        compiler_params=pltpu.CompilerParams(
            dimension_semantics=("arbitrary", "arbitrary"), vmem_limit_bytes=VMEM_LIMIT),
        name="ffn_up",
    )(h2, h2, h2, w_bf, w_bf, conv_w, conv_b)


def _ffn_down_kernel(a_ref, w_ref, x1_ref, g_ref, o_ref, *, n_k):
    k = pl.program_id(1)

    @pl.when(k == 0)
    def _():
        o_ref[...] = x1_ref[...]

    o_ref[...] += jnp.dot(a_ref[...], w_ref[...], preferred_element_type=F32)

    @pl.when(k == n_k - 1)
    def _():
        y = o_ref[...]
        ms = jnp.mean(y * y, axis=-1, keepdims=True)
        o_ref[...] = y * lax.rsqrt(ms + NORM_EPS) * g_ref[...]


def _ffn_down(act, w_bf, x1, gain, *, tm=1024, tk=1408):
    S, d_ff = act.shape
    D = x1.shape[1]
    n_k = d_ff // tk
    kern = functools.partial(_ffn_down_kernel, n_k=n_k)
    return pl.pallas_call(
        kern,
        grid=(S // tm, n_k),
        in_specs=[
            pl.BlockSpec((tm, tk), lambda i, k: (i, k)),
            pl.BlockSpec((tk, D), lambda i, k: (k, 0)),
            pl.BlockSpec((tm, D), lambda i, k: (i, 0)),
            pl.BlockSpec((1, D), lambda i, k: (0, 0)),
        ],
        out_specs=pl.BlockSpec((tm, D), lambda i, k: (i, 0)),
        out_shape=jax.ShapeDtypeStruct((S, D), F32),
        compiler_params=pltpu.CompilerParams(
            dimension_semantics=("arbitrary", "arbitrary"), vmem_limit_bytes=VMEM_LIMIT),
        name="ffn_down",
    )(act, w_bf, x1, gain)


def kernel(x, norm1_gain, w_in, rel_bias_table, lambda_q1, lambda_k1, lambda_q2, lambda_k2,
           diff_subln_gain, dil_out_gain, w_out, norm2_gain, w_gate_up, conv_w, conv_b, w_down, final_gain):
    B, S, D = x.shape
    assert B == 1 and w_in.shape[0] == 1
    x2 = x.reshape(S, D)
    n_cols = w_in.shape[2]

    qscale = LOG2E / math.sqrt(HEAD_DIM)
    col = np.arange(n_cols)
    dil_q0 = 2 * DIFF_QK_COLS + DIFF_WIDTH
    is_q = (col < DIFF_QK_COLS) | ((col >= dil_q0) & (col < dil_q0 + DIL_WIDTH))
    colscale = jnp.asarray(np.where(is_q, qscale, 1.0).astype(np.float32)).reshape(1, n_cols)
    tab = rel_bias_table.astype(F32) * LOG2E

    regroup = tuple(d for _, d in DILATED_PATTERNS if d != 1)
    proj, qt_all, vt_all, *cls = _in_proj(x2, norm1_gain.reshape(1, D), w_in[0].astype(BF16), colscale, regroup)
    cls_by_dil = dict(zip(regroup, cls))

    lam = _lambda(lambda_q1.reshape(1, -1), lambda_k1.reshape(1, -1),
                  lambda_q2.reshape(1, -1), lambda_k2.reshape(1, -1))
    o_d, w_out_bf, w_gate_up_bf, w_down_bf = _diff_attention(
        tab, proj, qt_all, vt_all, lam, diff_subln_gain.reshape(-1, 1), (w_out[0], w_gate_up[0], w_down[0]))

    dil_blk0 = 1
    dilations = tuple(d for _, d in DILATED_PATTERNS)
    dil_gain = dil_out_gain.reshape(1, -1)

    def pattern_inputs(dilation):
        if dilation == 1:
            return proj.reshape(1, S, proj.shape[1]), dil_blk0
        return cls_by_dil[dilation], 0

    kn2 = _key_norm2(proj, dil_blk0 + 1)
    nums, dens = [], []
    for window, dilation in DILATED_PATTERNS:
        n_p, d_p = _dilated_fast(tab, kn2, *pattern_inputs(dilation), window, dilation)
        nums.append(n_p)
        dens.append(d_p)
    o_l_fast, den_min = _combine_fast(nums, dens, dil_gain, dilations)

    def exact_dilated():
        outs, lses = [], []
        for window, dilation in DILATED_PATTERNS:
            o_p, lse_p = _dilated_pattern(tab, *pattern_inputs(dilation), window, dilation)
            outs.append(o_p)
            lses.append(lse_p)
        return _combine(outs, lses, dil_gain, dilations)

    o_l = lax.cond(jnp.min(den_min) < MIN_DENOMINATOR, lambda fast: exact_dilated(), lambda fast: fast, o_l_fast)

    x1, h2 = _out_proj(o_d, o_l, w_out_bf, x2, norm2_gain.reshape(1, D))
    act = _ffn_up(h2, w_gate_up_bf, conv_w[0], conv_b.reshape(1, -1))
    out = _ffn_down(act, w_down_bf, x1, final_gain.reshape(1, D))
    return out.reshape(B, S, D)
```
